```python
import math
import jax
import jax.numpy as jnp
from jax import lax
import numpy as np

D_MODEL = 1024
BATCH = 2
SEQ = 8192
DEPTH = 2

GRID_W = 64
CTX_LEN = 256
EPS = 1e-6
POS_BASE = 10000.0
GROUP_W = D_MODEL // 4
MIX_W = 4 * GROUP_W
SHORT_CONV = 3
HY_W = GROUP_W
HY_ORDER = 2
HY_EMB = 33
HY_BANDS = (HY_EMB - 1) // 2
HY_FFN = 64
HY_FAST_DECAY = 0.3
HY_SLOW_DECAY = 1.5
HY_DECAY_TARGET = 1e-2
HG_H = 4
HG_W = GROUP_W
HG_DK = HG_W // HG_H
GLA_H = 4
GLA_KW = GROUP_W // 2
GLA_VW = GROUP_W
GLA_DK = GLA_KW // GLA_H
GLA_DV = GLA_VW // GLA_H
GLA_RANK = 16
GLA_NORMALIZER = 16.0
ML_H = 4
ML_W = GROUP_W
ML_DH = ML_W // ML_H
CHUNK_GATED = 16
CHUNK_ML = 64
N_EXPERTS = 256
TOP_K = 8
N_EXPERT_GROUPS = 8
TOPK_GROUPS = 4
EXPERT_FF = 256
SHARED_FF = 256
ROUTED_SCALE = 2.5
MOE_BLOCK = 128
IN_SPLITS = (HY_W, HY_W, HY_W,
             HG_W, HG_W, HG_W, HG_W, HG_W,
             GLA_KW, GLA_KW, GLA_VW, GLA_RANK, GLA_RANK, GLA_VW,
             ML_W, ML_W, ML_W, 4 * ML_H, ML_W)
D_IN = 3 * HY_W + 5 * HG_W + 2 * GLA_KW + 2 * GLA_VW + 2 * GLA_RANK + 4 * ML_W + 4 * ML_H

kernel_name = 'hybrid_dit_hyena_hgrn2_gla_mlstm_moe'


def _rms(x, g):
    xf = x.astype(jnp.float32)
    y = xf * lax.rsqrt(jnp.mean(xf * xf, axis=-1, keepdims=True) + EPS)
    return (y * g.astype(jnp.float32)).astype(x.dtype)


def _split(p, sizes):
    out = []
    o = 0
    for s in sizes:
        out.append(p[..., o:o + s])
        o += s
    return out


def _heads(a, n):
    b, t, w = a.shape
    return a.reshape(b, t, n, w // n).transpose(0, 2, 1, 3)


def _head_rms(o, g):
    o = o * lax.rsqrt(jnp.mean(o * o, axis=-1, keepdims=True) + EPS)
    b, n, t, d = o.shape
    return o.transpose(0, 2, 1, 3).reshape(b, t, n * d) * g.astype(jnp.float32)


def _flip(a):
    return jnp.flip(a, axis=2)


def _short_conv(u, w, b):
    ch = u.shape[-1]
    y = lax.conv_general_dilated(u, w.astype(u.dtype)[:, None, :], (1,),
                                 ((SHORT_CONV // 2, SHORT_CONV // 2),),
                                 dimension_numbers=('NWC', 'WIO', 'NWC'),
                                 feature_group_count=ch)
    return y + b.astype(u.dtype)


def _pos_embed_2d(rows, d):
    r = jnp.repeat(jnp.arange(rows, dtype=jnp.float32), GRID_W)
    col = (jnp.arange(rows * GRID_W) % GRID_W).astype(jnp.float32)
    quarter = d // 4
    omega = 1.0 / (POS_BASE ** (jnp.arange(quarter, dtype=jnp.float32) / quarter))

    def axis_emb(p):
        ang = p[:, None] * omega[None, :]
        return jnp.concatenate([jnp.sin(ang), jnp.cos(ang)], axis=-1)

    return jnp.concatenate([axis_emb(r), axis_emb(col)], axis=-1)


def _hyena_spectra(L, w1, b1, w2, b2, w3, freq):
    f32 = jnp.float32
    t = jnp.linspace(0.0, 1.0, L, dtype=f32)[:, None]
    w = 2.0 * math.pi * jnp.arange(L, dtype=f32)[:, None] / L
    bands = jnp.linspace(1e-4, HY_BANDS - 1, HY_BANDS, dtype=f32)[None, :]
    feats = jnp.concatenate([t, jnp.cos(bands * w), -jnp.sin(bands * w)], axis=-1)
    z = jnp.sin(freq[0].astype(f32) * (feats @ w1.astype(f32) + b1.astype(f32)))
    z = jnp.sin(freq[1].astype(f32) * (z @ w2.astype(f32) + b2.astype(f32)))
    h = (z @ w3.astype(f32)).reshape(L, HY_ORDER, 2, HY_W)
    max_decay = math.log(HY_DECAY_TARGET) / HY_FAST_DECAY
    min_decay = math.log(HY_DECAY_TARGET) / HY_SLOW_DECAY
    deltas = jnp.abs(jnp.linspace(min_decay, max_decay, HY_W, dtype=f32))
    h = h * jnp.exp(-t[:, :, None, None] * deltas)
    fwd = h[:, :, 0]
    bwd = h[1:, :, 1][::-1]
    l1 = jnp.sum(jnp.abs(fwd), axis=0) + jnp.sum(jnp.abs(bwd), axis=0)
    filt = jnp.concatenate([fwd, jnp.zeros((1, HY_ORDER, HY_W), f32), bwd], axis=0) / l1
    return jnp.fft.rfft(filt, axis=0)


def _fft_conv(u, spec, bias):
    L = u.shape[1]
    y = jnp.fft.irfft(jnp.fft.rfft(u, n=2 * L, axis=1) * spec, n=2 * L, axis=1)[:, :L]
    return y + u * bias.astype(u.dtype)


def _hyena(parts, conv_w, conv_b, w1, b1, w2, b2, w3, freq, bias, norm_g):
    u = _short_conv(jnp.concatenate(parts, axis=-1), conv_w, conv_b)
    v, x1, x2 = u[..., :HY_W], u[..., HY_W:2 * HY_W], u[..., 2 * HY_W:]
    spec = _hyena_spectra(u.shape[1], w1, b1, w2, b2, w3, freq)
    z = x1 * _fft_conv(v, spec[:, 0], bias[0])
    y = x2 * _fft_conv(z, spec[:, 1], bias[1])
    return _rms(y, norm_g)


def _gla_dir(q, k, v, g, s0, with_out):
    bsz, nh, T, dk = k.shape
    dv = v.shape[-1]
    C = CHUNK_GATED
    n = T // C
    kc = k.reshape(bsz, nh, n, C, dk)
    vc = v.reshape(bsz, nh, n, C, dv)
    b = jnp.cumsum(g.reshape(bsz, nh, n, C, dk), axis=3)
    bl = b[:, :, :, -1]
    ds = jnp.einsum('bhncd,bhncv->bhndv', kc * jnp.exp(bl[:, :, :, None] - b), vc)

    def step(s, inp):
        dec, d = inp
        return dec[..., None] * s + d, s

    s_fin, s_prev = lax.scan(step, s0, (jnp.moveaxis(jnp.exp(bl), 2, 0), jnp.moveaxis(ds, 2, 0)))
    if not with_out:
        return None, s_fin
    s_prev = jnp.moveaxis(s_prev, 0, 2)
    qc = q.reshape(bsz, nh, n, C, dk)
    causal = jnp.tril(jnp.ones((C, C), dtype=bool))[:, :, None]
    diff = b[:, :, :, :, None, :] - b[:, :, :, None, :, :]
    att = jnp.sum(qc[:, :, :, :, None, :] * kc[:, :, :, None, :, :]
                  * jnp.exp(jnp.where(causal, diff, -jnp.inf)), axis=-1)
    o = (jnp.einsum('bhnts,bhnsv->bhntv', att, vc)
         + jnp.einsum('bhntd,bhndv->bhntv', qc * jnp.exp(b), s_prev))
    return o.reshape(bsz, nh, T, dv), s_fin


def _gla_bidir(q, k_f, k_b, v, g_f, g_b, s0_f, s0_b, with_out):
    o_f, s_f = _gla_dir(q, k_f, v, g_f, s0_f, with_out)
    o_b, s_b = _gla_dir(_flip(q), _flip(k_b), _flip(v), _flip(g_b), s0_b, with_out)
    o = o_f + _flip(o_b) if with_out else None
    return o, s_f, s_b


def _hgrn2_seq(parts, lb, norm_g, s0_f, s0_b, with_out):
    q, i, zf, zb, g = parts
    log_lb = jnp.log(lb)
    log_ub = jnp.log1p(-lb)

    def gate(z):
        k = _heads((1.0 - lb) * jax.nn.sigmoid(-z), HG_H)
        lf = _heads(jnp.logaddexp(log_lb, log_ub + jax.nn.log_sigmoid(z)), HG_H)
        return k, lf

    k_f, lf_f = gate(zf)
    k_b, lf_b = gate(zb)
    o, s_f, s_b = _gla_bidir(_heads(jax.nn.silu(q), HG_H), k_f, k_b, _heads(i, HG_H),
                             lf_f, lf_b, s0_f, s0_b, with_out)
    y = _head_rms(o, norm_g) * jax.nn.silu(g) if with_out else None
    return y, s_f, s_b


def _hgrn2(pc, pl, lb, norm_g, with_ctx):
    s0 = jnp.zeros((pl[0].shape[0], HG_H, HG_DK, HG_DK), jnp.float32)
    yc, s_f, s_b = _hgrn2_seq(pc, lb, norm_g, s0, s0, with_ctx)
    y, _, _ = _hgrn2_seq(pl, lb, norm_g, s_f, s_b, True)
    return yc, y


def _gla_seq(parts, a_up, a_b, norm_g, s0_f, s0_b, with_out):
    q, k, v, af, ab, r = parts
    qh = _heads(q, GLA_H) * GLA_DK ** -0.5
    kh = _heads(k, GLA_H)
    vh = _heads(v, GLA_H)
    a_up = a_up.astype(jnp.float32)
    a_b = a_b.astype(jnp.float32)
    g_f = _heads(jax.nn.log_sigmoid(af @ a_up[0] + a_b[0]) / GLA_NORMALIZER, GLA_H)
    g_b = _heads(jax.nn.log_sigmoid(ab @ a_up[1] + a_b[1]) / GLA_NORMALIZER, GLA_H)
    o, s_f, s_b = _gla_bidir(qh, kh, kh, vh, g_f, g_b, s0_f, s0_b, with_out)
    y = _head_rms(o, norm_g) * jax.nn.silu(r) if with_out else None
    return y, s_f, s_b


def _gla(pc, pl, a_up, a_b, norm_g, with_ctx):
    s0 = jnp.zeros((pl[0].shape[0], GLA_H, GLA_DK, GLA_DV), jnp.float32)
    yc, s_f, s_b = _gla_seq(pc, a_up, a_b, norm_g, s0, s0, with_ctx)
    y, _, _ = _gla_seq(pl, a_up, a_b, norm_g, s_f, s_b, True)
    return yc, y


def _mlstm_dir(q, k, v, ig, lf, state0, with_out):
    bsz, nh, T, dh = k.shape
    C = CHUNK_ML
    n = T // C
    kc = k.reshape(bsz, nh, n, C, dh)
    vc = v.reshape(bsz, nh, n, C, dh)
    igc = ig.reshape(bsz, nh, n, C)
    b = jnp.cumsum(lf.reshape(bsz, nh, n, C), axis=-1)
    bl = b[..., -1]
    a = bl[..., None] - b + igc
    ma = jnp.max(a, axis=-1)
    w = jnp.exp(a - ma[..., None])
    ds = jnp.einsum('bhnc,bhncd,bhncv->bhndv', w, kc, vc)
    dn = jnp.einsum('bhnc,bhncd->bhnd', w, kc)

    def step(carry, inp):
        s, nv, m = carry
        bl_, ma_, ds_, dn_ = inp
        m_new = jnp.maximum(bl_ + m, ma_)
        d_old = jnp.exp(bl_ + m - m_new)
        d_new = jnp.exp(ma_ - m_new)
        s_new = d_old[..., None, None] * s + d_new[..., None, None] * ds_
        n_new = d_old[..., None] * nv + d_new[..., None] * dn_
        return (s_new, n_new, m_new), (s, nv, m)

    mv = lambda t_: jnp.moveaxis(t_, 2, 0)
    fin, prev = lax.scan(step, state0, (mv(bl), mv(ma), mv(ds), mv(dn)))
    if not with_out:
        return None, fin
    s_prev, n_prev, m_prev = [jnp.moveaxis(p, 0, 2) for p in prev]
    qc = q.reshape(bsz, nh, n, C, dh)
    causal = jnp.tril(jnp.ones((C, C), dtype=bool))
    dmat = jnp.where(causal, b[..., :, None] - b[..., None, :] + igc[..., None, :], -jnp.inf)
    inter = b + m_prev[..., None]
    m_t = jnp.maximum(inter, jnp.max(dmat, axis=-1))
    wq = jnp.exp(dmat - m_t[..., None]) * jnp.einsum('bhntd,bhnsd->bhnts', qc, kc)
    w_int = jnp.exp(inter - m_t)
    num = (jnp.einsum('bhnts,bhnsv->bhntv', wq, vc)
           + w_int[..., None] * jnp.einsum('bhntd,bhndv->bhntv', qc, s_prev))
    den = jnp.sum(wq, axis=-1) + w_int * jnp.einsum('bhntd,bhnd->bhnt', qc, n_prev)
    h = num / jnp.maximum(jnp.abs(den), jnp.exp(-m_t))[..., None]
    return h.reshape(bsz, nh, T, dh), fin


def _mlstm_seq(parts, conv_w, conv_b, gate_b, norm_g, st_f, st_b, with_out):
    q, k, v, gates, o = parts
    qk = jax.nn.silu(_short_conv(jnp.concatenate([q, k], axis=-1), conv_w, conv_b))
    qh = _heads(qk[..., :ML_W], ML_H)
    kh = _heads(qk[..., ML_W:], ML_H) * ML_DH ** -0.5
    vh = _heads(v, ML_H)
    bsz, T, _ = gates.shape
    gt = (gates.reshape(bsz, T, 4, ML_H) + gate_b.astype(jnp.float32)).transpose(2, 0, 3, 1)
    h_f, fin_f = _mlstm_dir(qh, kh, vh, gt[0], jax.nn.log_sigmoid(gt[1]), st_f, with_out)
    h_b, fin_b = _mlstm_dir(_flip(qh), _flip(kh), _flip(vh), _flip(gt[2]),
                            _flip(jax.nn.log_sigmoid(gt[3])), st_b, with_out)
    y = jax.nn.sigmoid(o) * _head_rms(h_f + _flip(h_b), norm_g) if with_out else None
    return y, fin_f, fin_b


def _mlstm(pc, pl, conv_w, conv_b, gate_b, norm_g, with_ctx):
    bsz = pl[0].shape[0]
    f32 = jnp.float32
    st0 = (jnp.zeros((bsz, ML_H, ML_DH, ML_DH), f32), jnp.zeros((bsz, ML_H, ML_DH), f32),
           jnp.zeros((bsz, ML_H), f32))
    yc, st_f, st_b = _mlstm_seq(pc, conv_w, conv_b, gate_b, norm_g, st0, st0, with_ctx)
    y, _, _ = _mlstm_seq(pl, conv_w, conv_b, gate_b, norm_g, st_f, st_b, True)
    return yc, y


def _moe(t, router_w, router_b, w_gu, w_down, sh_gu, sh_down):
    N, D = t.shape
    scores = jax.nn.sigmoid(t.astype(jnp.float32) @ router_w.astype(jnp.float32))
    sel = scores + router_b.astype(jnp.float32)
    per_group = N_EXPERTS // N_EXPERT_GROUPS
    grp_score = jnp.sum(lax.top_k(sel.reshape(N, N_EXPERT_GROUPS, per_group), 2)[0], axis=-1)
    _, gidx = lax.top_k(grp_score, TOPK_GROUPS)
    gmask = jnp.sum(jax.nn.one_hot(gidx, N_EXPERT_GROUPS, dtype=jnp.float32), axis=1) > 0
    masked = jnp.where(jnp.repeat(gmask, per_group, axis=1), sel, -jnp.inf)
    _, eidx = lax.top_k(masked, TOP_K)
    wsel = jnp.take_along_axis(scores, eidx, axis=1)
    wsel = wsel / jnp.sum(wsel, axis=-1, keepdims=True) * ROUTED_SCALE
    A = N * TOP_K
    flat_e = eidx.reshape(A)
    flat_t = jnp.repeat(jnp.arange(N, dtype=jnp.int32), TOP_K)
    flat_w = wsel.reshape(A)
    order = jnp.argsort(flat_e)
    se, st, sw = flat_e[order], flat_t[order], flat_w[order]
    counts = jnp.bincount(flat_e, length=N_EXPERTS)
    start = jnp.cumsum(counts) - counts
    padded = (counts + MOE_BLOCK - 1) // MOE_BLOCK * MOE_BLOCK
    pad_end = jnp.cumsum(padded)
    pad_start = pad_end - padded
    dest = pad_start[se] + jnp.arange(A, dtype=jnp.int32) - start[se]
    n_blocks = (A + N_EXPERTS * (MOE_BLOCK - 1)) // MOE_BLOCK + 1
    P = n_blocks * MOE_BLOCK
    tok_p = jnp.zeros((P,), jnp.int32).at[dest].set(st)
    w_p = jnp.zeros((P,), jnp.float32).at[dest].set(sw)
    blk_e = jnp.minimum(jnp.searchsorted(pad_end, jnp.arange(n_blocks, dtype=jnp.int32) * MOE_BLOCK,
                                         side='right'), N_EXPERTS - 1)

    def body(acc, blk):
        e, toks, gw = blk
        xb = t[toks]
        a, u = jnp.split(xb @ w_gu[e], 2, axis=-1)
        yb = (jax.nn.silu(a) * u) @ w_down[e]
        return acc.at[toks].add(yb * gw[:, None].astype(yb.dtype)), None

    routed, _ = lax.scan(body, jnp.zeros_like(t),
                         (blk_e, tok_p.reshape(n_blocks, MOE_BLOCK), w_p.reshape(n_blocks, MOE_BLOCK)))
    a, u = jnp.split(t @ sh_gu, 2, axis=-1)
    return routed + (jax.nn.silu(a) * u) @ sh_down


def setup_inputs(seed: int = 0) -> dict:
    key = jax.random.key(seed)
    keys = iter(jax.random.split(key, 48))

    def nrm(shape, scale):
        return jax.random.normal(next(keys), shape, jnp.float32) * scale

    d = D_MODEL
    fb = jnp.linspace(3.0, 6.0, ML_H, dtype=jnp.float32)
    zb = jnp.zeros((ML_H,), jnp.float32)
    ml_base = jnp.stack([zb, fb, zb, fb])
    return {
        'x': nrm((BATCH, SEQ, d), 1.0),
        'c': nrm((BATCH, d), 1.0),
        'ctx': nrm((BATCH, CTX_LEN, d), 1.0),
        'c_ctx': nrm((d,), 1.0),
        'ada_w': nrm((DEPTH, d, 6 * d), 0.5 * d ** -0.5),
        'ada_b': nrm((DEPTH, 6 * d), 0.02),
        'norm_g': 1.0 + nrm((DEPTH, 4, d), 0.05),
        'w_in': nrm((DEPTH, d, D_IN), d ** -0.5),
        'w_out': nrm((DEPTH, MIX_W, d), MIX_W ** -0.5),
        'hy_conv_w': nrm((DEPTH, SHORT_CONV, 3 * HY_W), SHORT_CONV ** -0.5),
        'hy_conv_b': nrm((DEPTH, 3 * HY_W), 0.02),
        'hy_ffn_w1': nrm((DEPTH, HY_EMB, HY_FFN), HY_EMB ** -0.5),
        'hy_ffn_b1': nrm((DEPTH, HY_FFN), 0.1),
        'hy_ffn_w2': nrm((DEPTH, HY_FFN, HY_FFN), HY_FFN ** -0.5),
        'hy_ffn_b2': nrm((DEPTH, HY_FFN), 0.1),
        'hy_ffn_w3': nrm((DEPTH, HY_FFN, HY_ORDER * 2 * HY_W), HY_FFN ** -0.5),
        'hy_freq': 1.0 + nrm((DEPTH, 2, HY_FFN), 0.05),
        'hy_bias': nrm((DEPTH, HY_ORDER, HY_W), 0.5),
        'hy_norm': 1.0 + nrm((DEPTH, HY_W), 0.05),
        'hg_lb_logits': nrm((DEPTH, HG_W), 1.0),
        'hg_norm': 1.0 + nrm((DEPTH, HG_W), 0.05),
        'gla_a_up': nrm((DEPTH, 2, GLA_RANK, GLA_KW), GLA_RANK ** -0.5),
        'gla_a_b': nrm((DEPTH, 2, GLA_KW), 0.1),
        'gla_norm': 1.0 + nrm((DEPTH, GLA_VW), 0.05),
        'ml_conv_w': nrm((DEPTH, SHORT_CONV, 2 * ML_W), SHORT_CONV ** -0.5),
        'ml_conv_b': nrm((DEPTH, 2 * ML_W), 0.02),
        'ml_gate_b': ml_base[None] + nrm((DEPTH, 4, ML_H), 0.1),
        'ml_norm': 1.0 + nrm((DEPTH, ML_W), 0.05),
        'router_w': nrm((DEPTH, d, N_EXPERTS), d ** -0.5),
        'router_b': nrm((DEPTH, N_EXPERTS), 0.01),
        'exp_w_gu': nrm((DEPTH, N_EXPERTS, d, 2 * EXPERT_FF), d ** -0.5),
        'exp_w_down': nrm((DEPTH, N_EXPERTS, EXPERT_FF, d), EXPERT_FF ** -0.5),
        'sh_w_gu': nrm((DEPTH, d, 2 * SHARED_FF), d ** -0.5),
        'sh_w_down': nrm((DEPTH, SHARED_FF, d), SHARED_FF ** -0.5),
    }


def reference(x, c, ctx, c_ctx, ada_w, ada_b, norm_g, w_in, w_out,
              hy_conv_w, hy_conv_b, hy_ffn_w1, hy_ffn_b1, hy_ffn_w2, hy_ffn_b2, hy_ffn_w3, hy_freq, hy_bias, hy_norm,
              hg_lb_logits, hg_norm, gla_a_up, gla_a_b, gla_norm,
              ml_conv_w, ml_conv_b, ml_gate_b, ml_norm,
              router_w, router_b, exp_w_gu, exp_w_down, sh_w_gu, sh_w_down):
    f32 = jnp.float32
    bsz, seq, d = x.shape
    n_ctx = ctx.shape[1]
    rows = seq // GRID_W
    x = x + _pos_embed_2d(rows, d).astype(x.dtype)[None]
    xc = ctx
    lb_cum = jnp.cumsum(jax.nn.softmax(hg_lb_logits.astype(f32), axis=0), axis=0)
    lower_bounds = lb_cum - lb_cum[0:1]
    for l in range(DEPTH):
        with_ctx = l < DEPTH - 1
        mod = (jax.nn.silu(c) @ ada_w[l] + ada_b[l])[:, None, :]
        mod_c = jax.nn.silu(c_ctx) @ ada_w[l] + ada_b[l]
        sh1, sc1, g1, sh2, sc2, g2 = jnp.split(mod, 6, axis=-1)
        csh1, csc1, cg1, csh2, csc2, cg2 = jnp.split(mod_c, 6, axis=-1)
        h = _rms(x, norm_g[l, 0]) * (1.0 + sc1) + sh1
        hc = _rms(xc, norm_g[l, 0]) * (1.0 + csc1) + csh1
        pl = _split((h @ w_in[l]).astype(f32), IN_SPLITS)
        pc = _split((hc @ w_in[l]).astype(f32), IN_SPLITS)
        hy_args = (hy_conv_w[l], hy_conv_b[l], hy_ffn_w1[l], hy_ffn_b1[l], hy_ffn_w2[l], hy_ffn_b2[l],
                   hy_ffn_w3[l], hy_freq[l], hy_bias[l], hy_norm[l])
        y_hy = _hyena(pl[0:3], *hy_args)
        yc_hg, y_hg = _hgrn2(pc[3:8], pl[3:8], lower_bounds[l], hg_norm[l], with_ctx)
        yc_gla, y_gla = _gla(pc[8:14], pl[8:14], gla_a_up[l], gla_a_b[l], gla_norm[l], with_ctx)
        yc_ml, y_ml = _mlstm(pc[14:19], pl[14:19], ml_conv_w[l], ml_conv_b[l], ml_gate_b[l], ml_norm[l], with_ctx)
        y = jnp.concatenate([y_hy, y_hg, y_gla, y_ml], axis=-1).astype(x.dtype) @ w_out[l]
        x = x + g1 * _rms(y, norm_g[l, 1])
        h = _rms(x, norm_g[l, 2]) * (1.0 + sc2) + sh2
        moe_args = (router_w[l], router_b[l], exp_w_gu[l], exp_w_down[l], sh_w_gu[l], sh_w_down[l])
        if with_ctx:
            yc = jnp.concatenate([_hyena(pc[0:3], *hy_args), yc_hg, yc_gla, yc_ml],
                                 axis=-1).astype(xc.dtype) @ w_out[l]
            xc = xc + cg1 * _rms(yc, norm_g[l, 1])
            hc = _rms(xc, norm_g[l, 2]) * (1.0 + csc2) + csh2
            f = _moe(jnp.concatenate([h.reshape(bsz * seq, d), hc.reshape(bsz * n_ctx, d)], axis=0), *moe_args)
            xc = xc + cg2 * _rms(f[bsz * seq:].reshape(bsz, n_ctx, d), norm_g[l, 3])
            f = f[:bsz * seq]
        else:
            f = _moe(h.reshape(bsz * seq, d), *moe_args)
        x = x + g2 * _rms(f.reshape(bsz, seq, d), norm_g[l, 3])
    return x
```

```python
import functools
import math

import jax
import jax.numpy as jnp
from jax import lax
from jax.experimental import pallas as pl
from jax.experimental.pallas import tpu as pltpu

F32 = jnp.float32
BF16 = jnp.bfloat16

D_MODEL = 1024
GRID_W = 64
EPS = 1e-6
POS_BASE = 10000.0
GROUP_W = D_MODEL // 4
SHORT_CONV = 3
HY_W = GROUP_W
HY_ORDER = 2
HY_EMB = 33
HY_BANDS = (HY_EMB - 1) // 2
HY_FAST_DECAY = 0.3
HY_SLOW_DECAY = 1.5
HY_DECAY_TARGET = 1e-2
HG_H = 4
HG_W = GROUP_W
HG_DK = HG_W // HG_H
GLA_H = 4
GLA_KW = GROUP_W // 2
GLA_VW = GROUP_W
GLA_DK = GLA_KW // GLA_H
GLA_DV = GLA_VW // GLA_H
GLA_RANK = 16
GLA_NORMALIZER = 16.0
ML_H = 4
ML_W = GROUP_W
ML_DH = ML_W // ML_H
CHUNK_GATED = 16
CHUNK_ML = 64
N_EXPERTS = 256
TOP_K = 8
N_EXPERT_GROUPS = 8
TOPK_GROUPS = 4
EXPERT_FF = 256
ROUTED_SCALE = 2.5
IN_SPLITS = (HY_W, HY_W, HY_W,
             HG_W, HG_W, HG_W, HG_W, HG_W,
             GLA_KW, GLA_KW, GLA_VW, GLA_RANK, GLA_RANK, GLA_VW,
             ML_W, ML_W, ML_W, 4 * ML_H, ML_W)

LANE = 128
ROW_TILE = 512
MOE_ROWS = 256
VMEM_LIMIT = 56 * 1024 * 1024


def _mm_kernel(x_ref, w_ref, o_ref, *, precise):
    if precise:
        o_ref[...] = jnp.dot(x_ref[...].astype(F32), w_ref[...], preferred_element_type=F32,
                             precision=lax.Precision.HIGHEST)
    else:
        o_ref[...] = jnp.dot(x_ref[...].astype(BF16), w_ref[...], preferred_element_type=F32)


def _matmul(x, w, precise=False):
    m, k = x.shape
    n = w.shape[1]
    n_pad = -n % LANE
    w = w.astype(F32 if precise else BF16)
    if n_pad:
        w = jnp.pad(w, ((0, 0), (0, n_pad)))
    tm = ROW_TILE
    assert m % tm == 0
    out = pl.pallas_call(
        functools.partial(_mm_kernel, precise=precise),
        out_shape=jax.ShapeDtypeStruct((m, n + n_pad), F32),
        grid=(m // tm,),
        in_specs=[pl.BlockSpec((tm, k), lambda i: (i, 0)),
                  pl.BlockSpec((k, n + n_pad), lambda i: (0, 0))],
        out_specs=pl.BlockSpec((tm, n + n_pad), lambda i: (i, 0)),
        compiler_params=pltpu.CompilerParams(dimension_semantics=("arbitrary",),
                                             vmem_limit_bytes=VMEM_LIMIT),
        name="dense_matmul",
    )(x, w)
    return out[:, :n] if n_pad else out


def _moe_kernel(blk_e_ref, nused_ref, x_ref, wgu_ref, wdn_ref, o_ref, wgu_s, wdn_s):
    i = pl.program_id(0)
    e = blk_e_ref[i]
    e_prev = blk_e_ref[jnp.maximum(i - 1, 0)]

    @pl.when(i < nused_ref[0])
    def _():
        @pl.when((i == 0) | (e != e_prev))
        def _():
            wgu_s[...] = wgu_ref[...].astype(BF16)
            wdn_s[...] = wdn_ref[...].astype(BF16)

        au = jnp.dot(x_ref[...], wgu_s[...], preferred_element_type=F32)
        a = au[:, :EXPERT_FF]
        u = au[:, EXPERT_FF:]
        h = (a * jax.nn.sigmoid(a)) * u
        o_ref[...] = jnp.dot(h.astype(BF16), wdn_s[...], preferred_element_type=F32)

    @pl.when(i >= nused_ref[0])
    def _():
        o_ref[...] = jnp.zeros_like(o_ref)


def _moe_ffn(xs, blk_e, n_used, w_gu, w_down):
    p, d = xs.shape
    n_blocks = p // MOE_ROWS
    ff2 = w_gu.shape[-1]

    def x_map(i, blk_e, nused):
        return (jnp.minimum(i, nused[0] - 1), 0)

    def w_map(i, blk_e, nused):
        return (blk_e[i], 0, 0)

    return pl.pallas_call(
        _moe_kernel,
        out_shape=jax.ShapeDtypeStruct((p, d), F32),
        grid_spec=pltpu.PrefetchScalarGridSpec(
            num_scalar_prefetch=2,
            grid=(n_blocks,),
            in_specs=[pl.BlockSpec((MOE_ROWS, d), x_map),
                      pl.BlockSpec((None, d, ff2), w_map),
                      pl.BlockSpec((None, ff2 // 2, d), w_map)],
            out_specs=pl.BlockSpec((MOE_ROWS, d), lambda i, blk_e, nused: (i, 0)),
            scratch_shapes=[pltpu.VMEM((d, ff2), BF16), pltpu.VMEM((ff2 // 2, d), BF16)],
        ),
        compiler_params=pltpu.CompilerParams(dimension_semantics=("arbitrary",),
                                             vmem_limit_bytes=VMEM_LIMIT),
        name="moe_expert_ffn",
    )(blk_e, n_used, xs, w_gu, w_down)


def _rms(x, g):
    return x * lax.rsqrt(jnp.mean(x * x, axis=-1, keepdims=True) + EPS) * g


def _split(p, sizes):
    out = []
    o = 0
    for s in sizes:
        out.append(p[..., o:o + s])
        o += s
    return out


def _heads(a, n):
    b, t, w = a.shape
    return a.reshape(b, t, n, w // n).transpose(0, 2, 1, 3)


def _head_rms(o, g):
    o = o * lax.rsqrt(jnp.mean(o * o, axis=-1, keepdims=True) + EPS)
    b, n, t, d = o.shape
    return o.transpose(0, 2, 1, 3).reshape(b, t, n * d) * g


def _flip(a):
    return jnp.flip(a, axis=2)


def _short_conv(u, w, b):
    ch = u.shape[-1]
    y = lax.conv_general_dilated(u, w[:, None, :], (1,), ((SHORT_CONV // 2, SHORT_CONV // 2),),
                                 dimension_numbers=('NWC', 'WIO', 'NWC'), feature_group_count=ch)
    return y + b


def _pos_embed_2d(rows, d):
    r = jnp.repeat(jnp.arange(rows, dtype=F32), GRID_W)
    col = (jnp.arange(rows * GRID_W) % GRID_W).astype(F32)
    quarter = d // 4
    omega = 1.0 / (POS_BASE ** (jnp.arange(quarter, dtype=F32) / quarter))

    def axis_emb(p):
        ang = p[:, None] * omega[None, :]
        return jnp.concatenate([jnp.sin(ang), jnp.cos(ang)], axis=-1)

    return jnp.concatenate([axis_emb(r), axis_emb(col)], axis=-1)


def _hyena_spectra(L, w1, b1, w2, b2, w3, freq):
    t = jnp.linspace(0.0, 1.0, L, dtype=F32)[:, None]
    w = 2.0 * math.pi * jnp.arange(L, dtype=F32)[:, None] / L
    bands = jnp.linspace(1e-4, HY_BANDS - 1, HY_BANDS, dtype=F32)[None, :]
    feats = jnp.concatenate([t, jnp.cos(bands * w), -jnp.sin(bands * w)], axis=-1)
    z = jnp.sin(freq[0] * (feats @ w1 + b1))
    z = jnp.sin(freq[1] * (z @ w2 + b2))
    h = (z @ w3).reshape(L, HY_ORDER, 2, HY_W)
    max_decay = math.log(HY_DECAY_TARGET) / HY_FAST_DECAY
    min_decay = math.log(HY_DECAY_TARGET) / HY_SLOW_DECAY
    deltas = jnp.abs(jnp.linspace(min_decay, max_decay, HY_W, dtype=F32))
    h = h * jnp.exp(-t[:, :, None, None] * deltas)
    fwd = h[:, :, 0]
    bwd = h[1:, :, 1][::-1]
    l1 = jnp.sum(jnp.abs(fwd), axis=0) + jnp.sum(jnp.abs(bwd), axis=0)
    filt = jnp.concatenate([fwd, jnp.zeros((1, HY_ORDER, HY_W), F32), bwd], axis=0) / l1
    return jnp.fft.rfft(filt, axis=0)


def _fft_conv(u, spec, bias):
    L = u.shape[1]
    y = jnp.fft.irfft(jnp.fft.rfft(u, n=2 * L, axis=1) * spec, n=2 * L, axis=1)[:, :L]
    return y + u * bias


def _hyena(parts, conv_w, conv_b, w1, b1, w2, b2, w3, freq, bias, norm_g):
    u = _short_conv(jnp.concatenate(parts, axis=-1), conv_w, conv_b)
    v, x1, x2 = u[..., :HY_W], u[..., HY_W:2 * HY_W], u[..., 2 * HY_W:]
    spec = _hyena_spectra(u.shape[1], w1, b1, w2, b2, w3, freq)
    z = x1 * _fft_conv(v, spec[:, 0], bias[0])
    y = x2 * _fft_conv(z, spec[:, 1], bias[1])
    return _rms(y, norm_g)


def _gla_dir(q, k, v, g, s0, with_out):
    bsz, nh, T, dk = k.shape
    dv = v.shape[-1]
    C = CHUNK_GATED
    n = T // C
    kc = k.reshape(bsz, nh, n, C, dk)
    vc = v.reshape(bsz, nh, n, C, dv)
    b = jnp.cumsum(g.reshape(bsz, nh, n, C, dk), axis=3)
    bl = b[:, :, :, -1]
    ds = jnp.einsum('bhncd,bhncv->bhndv', kc * jnp.exp(bl[:, :, :, None] - b), vc)

    def step(s, inp):
        dec, d = inp
        return dec[..., None] * s + d, s

    s_fin, s_prev = lax.scan(step, s0, (jnp.moveaxis(jnp.exp(bl), 2, 0), jnp.moveaxis(ds, 2, 0)))
    if not with_out:
        return None, s_fin
    s_prev = jnp.moveaxis(s_prev, 0, 2)
    qc = q.reshape(bsz, nh, n, C, dk)
    causal = jnp.tril(jnp.ones((C, C), dtype=bool))[:, :, None]
    diff = b[:, :, :, :, None, :] - b[:, :, :, None, :, :]
    att = jnp.sum(qc[:, :, :, :, None, :] * kc[:, :, :, None, :, :]
                  * jnp.exp(jnp.where(causal, diff, -jnp.inf)), axis=-1)
    o = (jnp.einsum('bhnts,bhnsv->bhntv', att, vc)
         + jnp.einsum('bhntd,bhndv->bhntv', qc * jnp.exp(b), s_prev))
    return o.reshape(bsz, nh, T, dv), s_fin


def _gla_bidir(q, k_f, k_b, v, g_f, g_b, s0_f, s0_b, with_out):
    o_f, s_f = _gla_dir(q, k_f, v, g_f, s0_f, with_out)
    o_b, s_b = _gla_dir(_flip(q), _flip(k_b), _flip(v), _flip(g_b), s0_b, with_out)
    o = o_f + _flip(o_b) if with_out else None
    return o, s_f, s_b


def _hgrn2_seq(parts, lb, norm_g, s0_f, s0_b, with_out):
    q, i, zf, zb, g = parts
    log_lb = jnp.log(lb)
    log_ub = jnp.log1p(-lb)

    def gate(z):
        k = _heads((1.0 - lb) * jax.nn.sigmoid(-z), HG_H)
        lf = _heads(jnp.logaddexp(log_lb, log_ub + jax.nn.log_sigmoid(z)), HG_H)
        return k, lf

    k_f, lf_f = gate(zf)
    k_b, lf_b = gate(zb)
    o, s_f, s_b = _gla_bidir(_heads(jax.nn.silu(q), HG_H), k_f, k_b, _heads(i, HG_H),
                             lf_f, lf_b, s0_f, s0_b, with_out)
    y = _head_rms(o, norm_g) * jax.nn.silu(g) if with_out else None
    return y, s_f, s_b


def _hgrn2(pc, pl_, lb, norm_g, with_ctx):
    s0 = jnp.zeros((pl_[0].shape[0], HG_H, HG_DK, HG_DK), F32)
    yc, s_f, s_b = _hgrn2_seq(pc, lb, norm_g, s0, s0, with_ctx)
    y, _, _ = _hgrn2_seq(pl_, lb, norm_g, s_f, s_b, True)
    return yc, y


def _gla_seq(parts, a_up, a_b, norm_g, s0_f, s0_b, with_out):
    q, k, v, af, ab, r = parts
    qh = _heads(q, GLA_H) * GLA_DK ** -0.5
    kh = _heads(k, GLA_H)
    vh = _heads(v, GLA_H)
    g_f = _heads(jax.nn.log_sigmoid(af @ a_up[0] + a_b[0]) / GLA_NORMALIZER, GLA_H)
    g_b = _heads(jax.nn.log_sigmoid(ab @ a_up[1] + a_b[1]) / GLA_NORMALIZER, GLA_H)
    o, s_f, s_b = _gla_bidir(qh, kh, kh, vh, g_f, g_b, s0_f, s0_b, with_out)
    y = _head_rms(o, norm_g) * jax.nn.silu(r) if with_out else None
    return y, s_f, s_b


def _gla(pc, pl_, a_up, a_b, norm_g, with_ctx):
    s0 = jnp.zeros((pl_[0].shape[0], GLA_H, GLA_DK, GLA_DV), F32)
    yc, s_f, s_b = _gla_seq(pc, a_up, a_b, norm_g, s0, s0, with_ctx)
    y, _, _ = _gla_seq(pl_, a_up, a_b, norm_g, s_f, s_b, True)
    return yc, y


def _mlstm_dir(q, k, v, ig, lf, state0, with_out):
    bsz, nh, T, dh = k.shape
    C = CHUNK_ML
    n = T // C
    kc = k.reshape(bsz, nh, n, C, dh)
    vc = v.reshape(bsz, nh, n, C, dh)
    igc = ig.reshape(bsz, nh, n, C)
    b = jnp.cumsum(lf.reshape(bsz, nh, n, C), axis=-1)
    bl = b[..., -1]
    a = bl[..., None] - b + igc
    ma = jnp.max(a, axis=-1)
    w = jnp.exp(a - ma[..., None])
    ds = jnp.einsum('bhnc,bhncd,bhncv->bhndv', w, kc, vc)
    dn = jnp.einsum('bhnc,bhncd->bhnd', w, kc)

    def step(carry, inp):
        s, nv, m = carry
        bl_, ma_, ds_, dn_ = inp
        m_new = jnp.maximum(bl_ + m, ma_)
        d_old = jnp.exp(bl_ + m - m_new)
        d_new = jnp.exp(ma_ - m_new)
        s_new = d_old[..., None, None] * s + d_new[..., None, None] * ds_
        n_new = d_old[..., None] * nv + d_new[..., None] * dn_
        return (s_new, n_new, m_new), (s, nv, m)

    mv = lambda t_: jnp.moveaxis(t_, 2, 0)
    fin, prev = lax.scan(step, state0, (mv(bl), mv(ma), mv(ds), mv(dn)))
    if not with_out:
        return None, fin
    s_prev, n_prev, m_prev = [jnp.moveaxis(p, 0, 2) for p in prev]
    qc = q.reshape(bsz, nh, n, C, dh)
    causal = jnp.tril(jnp.ones((C, C), dtype=bool))
    dmat = jnp.where(causal, b[..., :, None] - b[..., None, :] + igc[..., None, :], -jnp.inf)
    inter = b + m_prev[..., None]
    m_t = jnp.maximum(inter, jnp.max(dmat, axis=-1))
    wq = jnp.exp(dmat - m_t[..., None]) * jnp.einsum('bhntd,bhnsd->bhnts', qc, kc)
    w_int = jnp.exp(inter - m_t)
    num = (jnp.einsum('bhnts,bhnsv->bhntv', wq, vc)
           + w_int[..., None] * jnp.einsum('bhntd,bhndv->bhntv', qc, s_prev))
    den = jnp.sum(wq, axis=-1) + w_int * jnp.einsum('bhntd,bhnd->bhnt', qc, n_prev)
    h = num / jnp.maximum(jnp.abs(den), jnp.exp(-m_t))[..., None]
    return h.reshape(bsz, nh, T, dh), fin


def _mlstm_seq(parts, conv_w, conv_b, gate_b, norm_g, st_f, st_b, with_out):
    q, k, v, gates, o = parts
    qk = jax.nn.silu(_short_conv(jnp.concatenate([q, k], axis=-1), conv_w, conv_b))
    qh = _heads(qk[..., :ML_W], ML_H)
    kh = _heads(qk[..., ML_W:], ML_H) * ML_DH ** -0.5
    vh = _heads(v, ML_H)
    bsz, T, _ = gates.shape
    gt = (gates.reshape(bsz, T, 4, ML_H) + gate_b).transpose(2, 0, 3, 1)
    h_f, fin_f = _mlstm_dir(qh, kh, vh, gt[0], jax.nn.log_sigmoid(gt[1]), st_f, with_out)
    h_b, fin_b = _mlstm_dir(_flip(qh), _flip(kh), _flip(vh), _flip(gt[2]),
                            _flip(jax.nn.log_sigmoid(gt[3])), st_b, with_out)
    y = jax.nn.sigmoid(o) * _head_rms(h_f + _flip(h_b), norm_g) if with_out else None
    return y, fin_f, fin_b


def _mlstm(pc, pl_, conv_w, conv_b, gate_b, norm_g, with_ctx):
    bsz = pl_[0].shape[0]
    st0 = (jnp.zeros((bsz, ML_H, ML_DH, ML_DH), F32), jnp.zeros((bsz, ML_H, ML_DH), F32),
           jnp.zeros((bsz, ML_H), F32))
    yc, st_f, st_b = _mlstm_seq(pc, conv_w, conv_b, gate_b, norm_g, st0, st0, with_ctx)
    y, _, _ = _mlstm_seq(pl_, conv_w, conv_b, gate_b, norm_g, st_f, st_b, True)
    return yc, y


def _route(scores, router_b):
    n = scores.shape[0]
    sel = scores + router_b
    per_group = N_EXPERTS // N_EXPERT_GROUPS
    grp_score = jnp.sum(lax.top_k(sel.reshape(n, N_EXPERT_GROUPS, per_group), 2)[0], axis=-1)
    _, gidx = lax.top_k(grp_score, TOPK_GROUPS)
    gmask = jnp.sum(jax.nn.one_hot(gidx, N_EXPERT_GROUPS, dtype=F32), axis=1) > 0
    masked = jnp.where(jnp.repeat(gmask, per_group, axis=1), sel, -jnp.inf)
    _, eidx = lax.top_k(masked, TOP_K)
    wsel = jnp.take_along_axis(scores, eidx, axis=1)
    wsel = wsel / jnp.sum(wsel, axis=-1, keepdims=True) * ROUTED_SCALE
    return eidx, wsel


def _moe(t, router_w, router_b, w_gu, w_down, sh_gu, sh_down):
    n, d = t.shape
    scores = jax.nn.sigmoid(_matmul(t, router_w, precise=True))
    eidx, wsel = _route(scores, router_b)
    a_tot = n * TOP_K
    bm = MOE_ROWS
    flat_e = eidx.reshape(a_tot)
    order = jnp.argsort(flat_e)
    se = flat_e[order]
    counts = jnp.bincount(flat_e, length=N_EXPERTS)
    start = jnp.cumsum(counts) - counts
    padded = (counts + bm - 1) // bm * bm
    pad_end = jnp.cumsum(padded)
    pad_start = pad_end - padded
    dest = (pad_start[se] + jnp.arange(a_tot, dtype=jnp.int32) - start[se]).astype(jnp.int32)
    n_blocks = (a_tot + N_EXPERTS * (bm - 1)) // bm + 1
    p = n_blocks * bm
    tok_p = jnp.zeros((p,), jnp.int32).at[dest].set((order // TOP_K).astype(jnp.int32))
    pos = jnp.zeros((a_tot,), jnp.int32).at[order].set(dest)
    blk_e = jnp.minimum(jnp.searchsorted(pad_end, jnp.arange(n_blocks, dtype=jnp.int32) * bm, side='right'),
                        N_EXPERTS - 1).astype(jnp.int32)
    n_used = (pad_end[-1] // bm).astype(jnp.int32).reshape(1)
    tb = t.astype(BF16)
    y_p = _moe_ffn(tb[tok_p], blk_e, n_used, w_gu, w_down)
    routed = jnp.sum(y_p[pos].reshape(n, TOP_K, d) * wsel[:, :, None], axis=1)
    au = _matmul(tb, sh_gu)
    ff = sh_gu.shape[-1] // 2
    shared = _matmul(jax.nn.silu(au[:, :ff]) * au[:, ff:], sh_down)
    return routed + shared


def kernel(x, c, ctx, c_ctx, ada_w, ada_b, norm_g, w_in, w_out, hy_conv_w, hy_conv_b, hy_ffn_w1, hy_ffn_b1, hy_ffn_w2, hy_ffn_b2, hy_ffn_w3, hy_freq, hy_bias, hy_norm, hg_lb_logits, hg_norm, gla_a_up, gla_a_b, gla_norm, ml_conv_w, ml_conv_b, ml_gate_b, ml_norm, router_w, router_b, exp_w_gu, exp_w_down, sh_w_gu, sh_w_down):
    bsz, seq, d = x.shape
    n_ctx = ctx.shape[1]
    depth = ada_w.shape[0]
    rows = seq // GRID_W
    x = x + _pos_embed_2d(rows, d)[None]
    xc = ctx
    lb_cum = jnp.cumsum(jax.nn.softmax(hg_lb_logits, axis=0), axis=0)
    lower_bounds = lb_cum - lb_cum[0:1]
    for l in range(depth):
        with_ctx = l < depth - 1
        mod = (jax.nn.silu(c) @ ada_w[l] + ada_b[l])[:, None, :]
        mod_c = jax.nn.silu(c_ctx) @ ada_w[l] + ada_b[l]
        sh1, sc1, g1, sh2, sc2, g2 = jnp.split(mod, 6, axis=-1)
        csh1, csc1, cg1, csh2, csc2, cg2 = jnp.split(mod_c, 6, axis=-1)
        h = _rms(x, norm_g[l, 0]) * (1.0 + sc1) + sh1
        hc = _rms(xc, norm_g[l, 0]) * (1.0 + csc1) + csh1
        h_all = jnp.concatenate([h.reshape(bsz * seq, d), hc.reshape(bsz * n_ctx, d)], axis=0)
        p_all = _matmul(h_all, w_in[l])
        pl_ = _split(p_all[:bsz * seq].reshape(bsz, seq, -1), IN_SPLITS)
        pc = _split(p_all[bsz * seq:].reshape(bsz, n_ctx, -1), IN_SPLITS)
        hy_args = (hy_conv_w[l], hy_conv_b[l], hy_ffn_w1[l], hy_ffn_b1[l], hy_ffn_w2[l], hy_ffn_b2[l],
                   hy_ffn_w3[l], hy_freq[l], hy_bias[l], hy_norm[l])
        y_hy = _hyena(pl_[0:3], *hy_args)
        yc_hg, y_hg = _hgrn2(pc[3:8], pl_[3:8], lower_bounds[l], hg_norm[l], with_ctx)
        yc_gla, y_gla = _gla(pc[8:14], pl_[8:14], gla_a_up[l], gla_a_b[l], gla_norm[l], with_ctx)
        yc_ml, y_ml = _mlstm(pc[14:19], pl_[14:19], ml_conv_w[l], ml_conv_b[l], ml_gate_b[l], ml_norm[l], with_ctx)
        y_cat = jnp.concatenate([y_hy, y_hg, y_gla, y_ml], axis=-1).reshape(bsz * seq, d)
        if with_ctx:
            yc_cat = jnp.concatenate([_hyena(pc[0:3], *hy_args), yc_hg, yc_gla, yc_ml],
                                     axis=-1).reshape(bsz * n_ctx, d)
            y_all = _matmul(jnp.concatenate([y_cat, yc_cat], axis=0), w_out[l])
            y = y_all[:bsz * seq].reshape(bsz, seq, d)
            yc = y_all[bsz * seq:].reshape(bsz, n_ctx, d)
        else:
            y = _matmul(y_cat, w_out[l]).reshape(bsz, seq, d)
        x = x + g1 * _rms(y, norm_g[l, 1])
        h = _rms(x, norm_g[l, 2]) * (1.0 + sc2) + sh2
        moe_args = (router_w[l], router_b[l], exp_w_gu[l], exp_w_down[l], sh_w_gu[l], sh_w_down[l])
        if with_ctx:
            xc = xc + cg1 * _rms(yc, norm_g[l, 1])
            hc = _rms(xc, norm_g[l, 2]) * (1.0 + csc2) + csh2
            f = _moe(jnp.concatenate([h.reshape(bsz * seq, d), hc.reshape(bsz * n_ctx, d)], axis=0), *moe_args)
            xc = xc + cg2 * _rms(f[bsz * seq:].reshape(bsz, n_ctx, d), norm_g[l, 3])
            f = f[:bsz * seq]
        else:
            f = _moe(h.reshape(bsz * seq, d), *moe_args)
        x = x + g2 * _rms(f.reshape(bsz, seq, d), norm_g[l, 3])
    return x
```

```python
import functools
import math

import jax
import jax.numpy as jnp
from jax import lax
from jax.experimental import pallas as pl
from jax.experimental.pallas import tpu as pltpu

F32 = jnp.float32
BF16 = jnp.bfloat16

D_MODEL = 1024
GRID_W = 64
EPS = 1e-6
POS_BASE = 10000.0
GROUP_W = D_MODEL // 4
SHORT_CONV = 3
HY_W = GROUP_W
HY_ORDER = 2
HY_EMB = 33
HY_BANDS = (HY_EMB - 1) // 2
HY_FAST_DECAY = 0.3
HY_SLOW_DECAY = 1.5
HY_DECAY_TARGET = 1e-2
HG_H = 4
HG_W = GROUP_W
HG_DK = HG_W // HG_H
GLA_H = 4
GLA_KW = GROUP_W // 2
GLA_VW = GROUP_W
GLA_DK = GLA_KW // GLA_H
GLA_DV = GLA_VW // GLA_H
GLA_RANK = 16
GLA_NORMALIZER = 16.0
ML_H = 4
ML_W = GROUP_W
ML_DH = ML_W // ML_H
CHUNK_GATED = 16
CHUNK_ML = 64
N_EXPERTS = 256
TOP_K = 8
N_EXPERT_GROUPS = 8
TOPK_GROUPS = 4
EXPERT_FF = 256
ROUTED_SCALE = 2.5
IN_SPLITS = (HY_W, HY_W, HY_W,
             HG_W, HG_W, HG_W, HG_W, HG_W,
             GLA_KW, GLA_KW, GLA_VW, GLA_RANK, GLA_RANK, GLA_VW,
             ML_W, ML_W, ML_W, 4 * ML_H, ML_W)

LANE = 128
ROW_TILE = 512
MOE_ROWS = 256
ROUTER_TILE = 256
SCATTER_TILE = 256
VMEM_LIMIT = 56 * 1024 * 1024
NEG_INF = float("-inf")


def _mm_kernel(x_ref, w_ref, o_ref, *, precise):
    if precise:
        o_ref[...] = jnp.dot(x_ref[...].astype(F32), w_ref[...], preferred_element_type=F32,
                             precision=lax.Precision.HIGHEST)
    else:
        o_ref[...] = jnp.dot(x_ref[...].astype(BF16), w_ref[...], preferred_element_type=F32)


def _matmul(x, w, precise=False):
    m, k = x.shape
    n = w.shape[1]
    n_pad = -n % LANE
    w = w.astype(F32 if precise else BF16)
    if n_pad:
        w = jnp.pad(w, ((0, 0), (0, n_pad)))
    tm = ROW_TILE
    assert m % tm == 0
    out = pl.pallas_call(
        functools.partial(_mm_kernel, precise=precise),
        out_shape=jax.ShapeDtypeStruct((m, n + n_pad), F32),
        grid=(m // tm,),
        in_specs=[pl.BlockSpec((tm, k), lambda i: (i, 0)),
                  pl.BlockSpec((k, n + n_pad), lambda i: (0, 0))],
        out_specs=pl.BlockSpec((tm, n + n_pad), lambda i: (i, 0)),
        compiler_params=pltpu.CompilerParams(dimension_semantics=("arbitrary",),
                                             vmem_limit_bytes=VMEM_LIMIT),
        name="dense_matmul",
    )(x, w)
    return out[:, :n] if n_pad else out


def _rms(x, g):
    return x * lax.rsqrt(jnp.mean(x * x, axis=-1, keepdims=True) + EPS) * g


def _split(p, sizes):
    out = []
    o = 0
    for s in sizes:
        out.append(p[..., o:o + s])
        o += s
    return out


def _heads(a, n):
    b, t, w = a.shape
    return a.reshape(b, t, n, w // n).transpose(0, 2, 1, 3)


def _head_rms(o, g):
    o = o * lax.rsqrt(jnp.mean(o * o, axis=-1, keepdims=True) + EPS)
    b, n, t, d = o.shape
    return o.transpose(0, 2, 1, 3).reshape(b, t, n * d) * g


def _flip(a):
    return jnp.flip(a, axis=2)


def _short_conv(u, w, b):
    ch = u.shape[-1]
    y = lax.conv_general_dilated(u, w[:, None, :], (1,), ((SHORT_CONV // 2, SHORT_CONV // 2),),
                                 dimension_numbers=('NWC', 'WIO', 'NWC'), feature_group_count=ch)
    return y + b


def _pos_embed_2d(rows, d):
    r = jnp.repeat(jnp.arange(rows, dtype=F32), GRID_W)
    col = (jnp.arange(rows * GRID_W) % GRID_W).astype(F32)
    quarter = d // 4
    omega = 1.0 / (POS_BASE ** (jnp.arange(quarter, dtype=F32) / quarter))

    def axis_emb(p):
        ang = p[:, None] * omega[None, :]
        return jnp.concatenate([jnp.sin(ang), jnp.cos(ang)], axis=-1)

    return jnp.concatenate([axis_emb(r), axis_emb(col)], axis=-1)


def _hyena_spectra(L, w1, b1, w2, b2, w3, freq):
    t = jnp.linspace(0.0, 1.0, L, dtype=F32)[:, None]
    w = 2.0 * math.pi * jnp.arange(L, dtype=F32)[:, None] / L
    bands = jnp.linspace(1e-4, HY_BANDS - 1, HY_BANDS, dtype=F32)[None, :]
    feats = jnp.concatenate([t, jnp.cos(bands * w), -jnp.sin(bands * w)], axis=-1)
    z = jnp.sin(freq[0] * (feats @ w1 + b1))
    z = jnp.sin(freq[1] * (z @ w2 + b2))
    h = (z @ w3).reshape(L, HY_ORDER, 2, HY_W)
    max_decay = math.log(HY_DECAY_TARGET) / HY_FAST_DECAY
    min_decay = math.log(HY_DECAY_TARGET) / HY_SLOW_DECAY
    deltas = jnp.abs(jnp.linspace(min_decay, max_decay, HY_W, dtype=F32))
    h = h * jnp.exp(-t[:, :, None, None] * deltas)
    fwd = h[:, :, 0]
    bwd = h[1:, :, 1][::-1]
    l1 = jnp.sum(jnp.abs(fwd), axis=0) + jnp.sum(jnp.abs(bwd), axis=0)
    filt = jnp.concatenate([fwd, jnp.zeros((1, HY_ORDER, HY_W), F32), bwd], axis=0) / l1
    return jnp.fft.rfft(filt, axis=0)


def _fft_conv(u, spec, bias):
    L = u.shape[1]
    y = jnp.fft.irfft(jnp.fft.rfft(u, n=2 * L, axis=1) * spec, n=2 * L, axis=1)[:, :L]
    return y + u * bias


def _hyena(parts, conv_w, conv_b, w1, b1, w2, b2, w3, freq, bias, norm_g):
    u = _short_conv(jnp.concatenate(parts, axis=-1), conv_w, conv_b)
    v, x1, x2 = u[..., :HY_W], u[..., HY_W:2 * HY_W], u[..., 2 * HY_W:]
    spec = _hyena_spectra(u.shape[1], w1, b1, w2, b2, w3, freq)
    z = x1 * _fft_conv(v, spec[:, 0], bias[0])
    y = x2 * _fft_conv(z, spec[:, 1], bias[1])
    return _rms(y, norm_g)


def _gla_dir(q, k, v, g, s0, with_out):
    bsz, nh, T, dk = k.shape
    dv = v.shape[-1]
    C = CHUNK_GATED
    n = T // C
    kc = k.reshape(bsz, nh, n, C, dk)
    vc = v.reshape(bsz, nh, n, C, dv)
    b = jnp.cumsum(g.reshape(bsz, nh, n, C, dk), axis=3)
    bl = b[:, :, :, -1]
    ds = jnp.einsum('bhncd,bhncv->bhndv', kc * jnp.exp(bl[:, :, :, None] - b), vc)

    def step(s, inp):
        dec, d = inp
        return dec[..., None] * s + d, s

    s_fin, s_prev = lax.scan(step, s0, (jnp.moveaxis(jnp.exp(bl), 2, 0), jnp.moveaxis(ds, 2, 0)))
    if not with_out:
        return None, s_fin
    s_prev = jnp.moveaxis(s_prev, 0, 2)
    qc = q.reshape(bsz, nh, n, C, dk)
    causal = jnp.tril(jnp.ones((C, C), dtype=bool))[:, :, None]
    diff = b[:, :, :, :, None, :] - b[:, :, :, None, :, :]
    att = jnp.sum(qc[:, :, :, :, None, :] * kc[:, :, :, None, :, :]
                  * jnp.exp(jnp.where(causal, diff, -jnp.inf)), axis=-1)
    o = (jnp.einsum('bhnts,bhnsv->bhntv', att, vc)
         + jnp.einsum('bhntd,bhndv->bhntv', qc * jnp.exp(b), s_prev))
    return o.reshape(bsz, nh, T, dv), s_fin


def _gla_bidir(q, k_f, k_b, v, g_f, g_b, s0_f, s0_b, with_out):
    o_f, s_f = _gla_dir(q, k_f, v, g_f, s0_f, with_out)
    o_b, s_b = _gla_dir(_flip(q), _flip(k_b), _flip(v), _flip(g_b), s0_b, with_out)
    o = o_f + _flip(o_b) if with_out else None
    return o, s_f, s_b


def _hgrn2_seq(parts, lb, norm_g, s0_f, s0_b, with_out):
    q, i, zf, zb, g = parts
    log_lb = jnp.log(lb)
    log_ub = jnp.log1p(-lb)

    def gate(z):
        k = _heads((1.0 - lb) * jax.nn.sigmoid(-z), HG_H)
        lf = _heads(jnp.logaddexp(log_lb, log_ub + jax.nn.log_sigmoid(z)), HG_H)
        return k, lf

    k_f, lf_f = gate(zf)
    k_b, lf_b = gate(zb)
    o, s_f, s_b = _gla_bidir(_heads(jax.nn.silu(q), HG_H), k_f, k_b, _heads(i, HG_H),
                             lf_f, lf_b, s0_f, s0_b, with_out)
    y = _head_rms(o, norm_g) * jax.nn.silu(g) if with_out else None
    return y, s_f, s_b


def _hgrn2(pc, pl_, lb, norm_g, with_ctx):
    s0 = jnp.zeros((pl_[0].shape[0], HG_H, HG_DK, HG_DK), F32)
    yc, s_f, s_b = _hgrn2_seq(pc, lb, norm_g, s0, s0, with_ctx)
    y, _, _ = _hgrn2_seq(pl_, lb, norm_g, s_f, s_b, True)
    return yc, y


def _gla_seq(parts, a_up, a_b, norm_g, s0_f, s0_b, with_out):
    q, k, v, af, ab, r = parts
    qh = _heads(q, GLA_H) * GLA_DK ** -0.5
    kh = _heads(k, GLA_H)
    vh = _heads(v, GLA_H)
    g_f = _heads(jax.nn.log_sigmoid(af @ a_up[0] + a_b[0]) / GLA_NORMALIZER, GLA_H)
    g_b = _heads(jax.nn.log_sigmoid(ab @ a_up[1] + a_b[1]) / GLA_NORMALIZER, GLA_H)
    o, s_f, s_b = _gla_bidir(qh, kh, kh, vh, g_f, g_b, s0_f, s0_b, with_out)
    y = _head_rms(o, norm_g) * jax.nn.silu(r) if with_out else None
    return y, s_f, s_b


def _gla(pc, pl_, a_up, a_b, norm_g, with_ctx):
    s0 = jnp.zeros((pl_[0].shape[0], GLA_H, GLA_DK, GLA_DV), F32)
    yc, s_f, s_b = _gla_seq(pc, a_up, a_b, norm_g, s0, s0, with_ctx)
    y, _, _ = _gla_seq(pl_, a_up, a_b, norm_g, s_f, s_b, True)
    return yc, y


def _mlstm_dir(q, k, v, ig, lf, state0, with_out):
    bsz, nh, T, dh = k.shape
    C = CHUNK_ML
    n = T // C
    kc = k.reshape(bsz, nh, n, C, dh)
    vc = v.reshape(bsz, nh, n, C, dh)
    igc = ig.reshape(bsz, nh, n, C)
    b = jnp.cumsum(lf.reshape(bsz, nh, n, C), axis=-1)
    bl = b[..., -1]
    a = bl[..., None] - b + igc
    ma = jnp.max(a, axis=-1)
    w = jnp.exp(a - ma[..., None])
    ds = jnp.einsum('bhnc,bhncd,bhncv->bhndv', w, kc, vc)
    dn = jnp.einsum('bhnc,bhncd->bhnd', w, kc)

    def step(carry, inp):
        s, nv, m = carry
        bl_, ma_, ds_, dn_ = inp
        m_new = jnp.maximum(bl_ + m, ma_)
        d_old = jnp.exp(bl_ + m - m_new)
        d_new = jnp.exp(ma_ - m_new)
        s_new = d_old[..., None, None] * s + d_new[..., None, None] * ds_
        n_new = d_old[..., None] * nv + d_new[..., None] * dn_
        return (s_new, n_new, m_new), (s, nv, m)

    mv = lambda t_: jnp.moveaxis(t_, 2, 0)
    fin, prev = lax.scan(step, state0, (mv(bl), mv(ma), mv(ds), mv(dn)))
    if not with_out:
        return None, fin
    s_prev, n_prev, m_prev = [jnp.moveaxis(p, 0, 2) for p in prev]
    qc = q.reshape(bsz, nh, n, C, dh)
    causal = jnp.tril(jnp.ones((C, C), dtype=bool))
    dmat = jnp.where(causal, b[..., :, None] - b[..., None, :] + igc[..., None, :], -jnp.inf)
    inter = b + m_prev[..., None]
    m_t = jnp.maximum(inter, jnp.max(dmat, axis=-1))
    wq = jnp.exp(dmat - m_t[..., None]) * jnp.einsum('bhntd,bhnsd->bhnts', qc, kc)
    w_int = jnp.exp(inter - m_t)
    num = (jnp.einsum('bhnts,bhnsv->bhntv', wq, vc)
           + w_int[..., None] * jnp.einsum('bhntd,bhndv->bhntv', qc, s_prev))
    den = jnp.sum(wq, axis=-1) + w_int * jnp.einsum('bhntd,bhnd->bhnt', qc, n_prev)
    h = num / jnp.maximum(jnp.abs(den), jnp.exp(-m_t))[..., None]
    return h.reshape(bsz, nh, T, dh), fin


def _mlstm_seq(parts, conv_w, conv_b, gate_b, norm_g, st_f, st_b, with_out):
    q, k, v, gates, o = parts
    qk = jax.nn.silu(_short_conv(jnp.concatenate([q, k], axis=-1), conv_w, conv_b))
    qh = _heads(qk[..., :ML_W], ML_H)
    kh = _heads(qk[..., ML_W:], ML_H) * ML_DH ** -0.5
    vh = _heads(v, ML_H)
    bsz, T, _ = gates.shape
    gt = (gates.reshape(bsz, T, 4, ML_H) + gate_b).transpose(2, 0, 3, 1)
    h_f, fin_f = _mlstm_dir(qh, kh, vh, gt[0], jax.nn.log_sigmoid(gt[1]), st_f, with_out)
    h_b, fin_b = _mlstm_dir(_flip(qh), _flip(kh), _flip(vh), _flip(gt[2]),
                            _flip(jax.nn.log_sigmoid(gt[3])), st_b, with_out)
    y = jax.nn.sigmoid(o) * _head_rms(h_f + _flip(h_b), norm_g) if with_out else None
    return y, fin_f, fin_b


def _mlstm(pc, pl_, conv_w, conv_b, gate_b, norm_g, with_ctx):
    bsz = pl_[0].shape[0]
    st0 = (jnp.zeros((bsz, ML_H, ML_DH, ML_DH), F32), jnp.zeros((bsz, ML_H, ML_DH), F32),
           jnp.zeros((bsz, ML_H), F32))
    yc, st_f, st_b = _mlstm_seq(pc, conv_w, conv_b, gate_b, norm_g, st0, st0, with_ctx)
    y, _, _ = _mlstm_seq(pl_, conv_w, conv_b, gate_b, norm_g, st_f, st_b, True)
    return yc, y


def _router_kernel(h_ref, wt_ref, b_ref, eidx_ref, wsel_ref, cnt_ref):
    i = pl.program_id(0)
    tm = h_ref.shape[0]
    ne = wt_ref.shape[0]
    per_group = ne // N_EXPERT_GROUPS
    logits = lax.dot_general(wt_ref[...], h_ref[...], (((1,), (1,)), ((), ())),
                             preferred_element_type=F32, precision=lax.Precision.HIGHEST)
    s = jax.nn.sigmoid(logits)
    sel = s + b_ref[...]
    row = lax.broadcasted_iota(jnp.int32, (ne, tm), 0)
    gs = []
    for g in range(N_EXPERT_GROUPS):
        blk = sel[g * per_group:(g + 1) * per_group]
        r = lax.broadcasted_iota(jnp.int32, blk.shape, 0)
        m1 = jnp.max(blk, axis=0, keepdims=True)
        i1 = jnp.min(jnp.where(blk == m1, r, per_group), axis=0, keepdims=True)
        m2 = jnp.max(jnp.where(r == i1, NEG_INF, blk), axis=0, keepdims=True)
        gs.append(m1 + m2)
    grp = jnp.concatenate(gs, axis=0)
    grow = lax.broadcasted_iota(jnp.int32, grp.shape, 0)
    gsel = jnp.zeros(grp.shape, F32)
    for _ in range(TOPK_GROUPS):
        m = jnp.max(grp, axis=0, keepdims=True)
        gi = jnp.min(jnp.where(grp == m, grow, N_EXPERT_GROUPS), axis=0, keepdims=True)
        hit = grow == gi
        gsel = jnp.where(hit, 1.0, gsel)
        grp = jnp.where(hit, NEG_INF, grp)
    masked = jnp.concatenate(
        [jnp.where(gsel[g:g + 1] > 0.0, sel[g * per_group:(g + 1) * per_group], NEG_INF)
         for g in range(N_EXPERT_GROUPS)], axis=0)
    eis, ws = [], []
    picked = jnp.zeros((ne, tm), F32)
    for _ in range(TOP_K):
        m = jnp.max(masked, axis=0, keepdims=True)
        ei = jnp.min(jnp.where(masked == m, row, ne), axis=0, keepdims=True)
        hit = row == ei
        ws.append(jnp.sum(jnp.where(hit, s, 0.0), axis=0, keepdims=True))
        eis.append(ei)
        picked = jnp.where(hit, 1.0, picked)
        masked = jnp.where(hit, NEG_INF, masked)
    w = jnp.concatenate(ws, axis=0)
    eidx_ref[...] = jnp.concatenate(eis, axis=0)
    wsel_ref[...] = w / jnp.sum(w, axis=0, keepdims=True) * ROUTED_SCALE
    tot = jnp.dot(picked.astype(BF16), jnp.ones((tm, LANE), BF16), preferred_element_type=F32)

    @pl.when(i == 0)
    def _():
        cnt_ref[...] = jnp.zeros_like(cnt_ref)

    cnt_ref[...] += tot


def _pos_kernel(eidx_ref, base_ref, pos_ref, carry_ref):
    i = pl.program_id(0)
    tm = eidx_ref.shape[1]
    ne = base_ref.shape[0]

    @pl.when(i == 0)
    def _():
        carry_ref[...] = jnp.zeros_like(carry_ref)

    eidx = eidx_ref[...]
    row = lax.broadcasted_iota(jnp.int32, (ne, tm), 0)
    picked = jnp.zeros((ne, tm), F32)
    for k in range(TOP_K):
        picked = jnp.where(row == eidx[k:k + 1], 1.0, picked)
    pb = picked.astype(BF16)
    before = jnp.where(lax.broadcasted_iota(jnp.int32, (tm, tm), 0) < lax.broadcasted_iota(jnp.int32, (tm, tm), 1),
                       1.0, 0.0).astype(BF16)
    rank = jnp.dot(pb, before, preferred_element_type=F32)
    tot = jnp.dot(pb, jnp.ones((tm, LANE), BF16), preferred_element_type=F32)
    dest = rank + (base_ref[...] + carry_ref[:, 0:1])
    pos = [jnp.sum(jnp.where(row == eidx[k:k + 1], dest, 0.0), axis=0, keepdims=True) for k in range(TOP_K)]
    pos_ref[...] = jnp.concatenate(pos, axis=0).astype(jnp.int32)
    carry_ref[...] += tot


def _dispatch_kernel(pos_ref, h_ref, xs_init_ref, xs_ref, sem):
    del xs_init_ref
    tm = h_ref.shape[0]

    def row_copy(n, k):
        return pltpu.make_async_copy(h_ref.at[pl.ds(n, 1)], xs_ref.at[pl.ds(pos_ref[k, n], 1)], sem)

    def issue(n, carry):
        for k in range(TOP_K):
            row_copy(n, k).start()
        return carry

    def drain(n, carry):
        for k in range(TOP_K):
            row_copy(n, k).wait()
        return carry

    lax.fori_loop(0, tm, issue, 0)
    lax.fori_loop(0, tm, drain, 0)


def _moe_ffn_kernel(blk_e_ref, nused_ref, x_ref, wgu_ref, wdn_ref, o_ref, wgu_s, wdn_s):
    i = pl.program_id(0)
    e = blk_e_ref[i]
    e_prev = blk_e_ref[jnp.maximum(i - 1, 0)]

    @pl.when(i < nused_ref[0])
    def _():
        @pl.when((i == 0) | (e != e_prev))
        def _():
            wgu_s[...] = wgu_ref[...].astype(BF16)
            wdn_s[...] = wdn_ref[...].astype(BF16)

        au = jnp.dot(x_ref[...].astype(BF16), wgu_s[...], preferred_element_type=F32)
        a = au[:, :EXPERT_FF]
        u = au[:, EXPERT_FF:]
        h = (a * jax.nn.sigmoid(a)) * u
        o_ref[...] = jnp.dot(h.astype(BF16), wdn_s[...], preferred_element_type=F32)

    @pl.when(i >= nused_ref[0])
    def _():
        o_ref[...] = jnp.zeros_like(o_ref)


def _combine_kernel(pos_ref, w_ref, y_hbm, o_ref, buf, sem):
    tm = o_ref.shape[0]

    def row_copy(n, k):
        return pltpu.make_async_copy(y_hbm.at[pl.ds(pos_ref[k, n], 1)], buf.at[k, pl.ds(n, 1)], sem)

    def issue(n, carry):
        for k in range(TOP_K):
            row_copy(n, k).start()
        return carry

    def drain(n, carry):
        for k in range(TOP_K):
            row_copy(n, k).wait()
        return carry

    lax.fori_loop(0, tm, issue, 0)
    lax.fori_loop(0, tm, drain, 0)
    acc = buf[0] * w_ref[:, 0:1]
    for k in range(1, TOP_K):
        acc = acc + buf[k] * w_ref[:, k:k + 1]
    o_ref[...] = acc


def _moe_routed(t, router_w, router_b, w_gu, w_down):
    n, d = t.shape
    ne = router_w.shape[1]
    ff2 = w_gu.shape[-1]
    params = pltpu.CompilerParams(dimension_semantics=("arbitrary",), vmem_limit_bytes=VMEM_LIMIT)
    tm = ROUTER_TILE
    eidx, wsel, cnt = pl.pallas_call(
        _router_kernel,
        out_shape=[jax.ShapeDtypeStruct((TOP_K, n), jnp.int32), jax.ShapeDtypeStruct((TOP_K, n), F32),
                   jax.ShapeDtypeStruct((ne, LANE), F32)],
        grid=(n // tm,),
        in_specs=[pl.BlockSpec((tm, d), lambda i: (i, 0)), pl.BlockSpec((ne, d), lambda i: (0, 0)),
                  pl.BlockSpec((ne, 1), lambda i: (0, 0))],
        out_specs=[pl.BlockSpec((TOP_K, tm), lambda i: (0, i)), pl.BlockSpec((TOP_K, tm), lambda i: (0, i)),
                   pl.BlockSpec((ne, LANE), lambda i: (0, 0))],
        compiler_params=params, name="moe_router",
    )(t, router_w.T, router_b.reshape(ne, 1))
    bm = MOE_ROWS
    counts = cnt[:, 0].astype(jnp.int32)
    padded = (counts + bm - 1) // bm * bm
    pad_end = jnp.cumsum(padded)
    pad_start = pad_end - padded
    n_blocks = (n * TOP_K + ne * (bm - 1)) // bm + 1
    blk_first = jnp.arange(n_blocks, dtype=jnp.int32) * bm
    blk_e = jnp.minimum(jnp.sum((pad_end[None, :] <= blk_first[:, None]).astype(jnp.int32), axis=1), ne - 1)
    n_used = (pad_end[-1] // bm).astype(jnp.int32).reshape(1)
    pos = pl.pallas_call(
        _pos_kernel,
        out_shape=jax.ShapeDtypeStruct((TOP_K, n), jnp.int32),
        grid=(n // tm,),
        in_specs=[pl.BlockSpec((TOP_K, tm), lambda i: (0, i)), pl.BlockSpec((ne, 1), lambda i: (0, 0))],
        out_specs=pl.BlockSpec((TOP_K, tm), lambda i: (0, i)),
        scratch_shapes=[pltpu.VMEM((ne, LANE), F32)],
        compiler_params=params, name="moe_positions",
    )(eidx, pad_start.astype(F32).reshape(ne, 1))
    ts = SCATTER_TILE
    p = n_blocks * bm
    pos_spec = pl.BlockSpec((TOP_K, ts), lambda i: (0, i), memory_space=pltpu.SMEM)
    xs = pl.pallas_call(
        _dispatch_kernel,
        out_shape=jax.ShapeDtypeStruct((p, d), F32),
        grid=(n // ts,),
        in_specs=[pos_spec, pl.BlockSpec((ts, d), lambda i: (i, 0)), pl.BlockSpec(memory_space=pl.ANY)],
        out_specs=pl.BlockSpec(memory_space=pl.ANY),
        scratch_shapes=[pltpu.SemaphoreType.DMA],
        input_output_aliases={2: 0},
        compiler_params=params, name="moe_dispatch",
    )(pos, t, jnp.zeros((p, d), F32))

    def x_map(i, blk_e, nused):
        return (jnp.minimum(i, nused[0] - 1), 0)

    def w_map(i, blk_e, nused):
        return (blk_e[i], 0, 0)

    y_p = pl.pallas_call(
        _moe_ffn_kernel,
        out_shape=jax.ShapeDtypeStruct((p, d), F32),
        grid_spec=pltpu.PrefetchScalarGridSpec(
            num_scalar_prefetch=2,
            grid=(n_blocks,),
            in_specs=[pl.BlockSpec((bm, d), x_map),
                      pl.BlockSpec((None, d, ff2), w_map),
                      pl.BlockSpec((None, ff2 // 2, d), w_map)],
            out_specs=pl.BlockSpec((bm, d), lambda i, blk_e, nused: (i, 0)),
            scratch_shapes=[pltpu.VMEM((d, ff2), BF16), pltpu.VMEM((ff2 // 2, d), BF16)],
        ),
        compiler_params=params, name="moe_expert_ffn",
    )(blk_e, n_used, xs, w_gu, w_down)
    return pl.pallas_call(
        _combine_kernel,
        out_shape=jax.ShapeDtypeStruct((n, d), F32),
        grid=(n // ts,),
        in_specs=[pos_spec, pl.BlockSpec((ts, TOP_K), lambda i: (i, 0)), pl.BlockSpec(memory_space=pl.ANY)],
        out_specs=pl.BlockSpec((ts, d), lambda i: (i, 0)),
        scratch_shapes=[pltpu.VMEM((TOP_K, ts, d), F32), pltpu.SemaphoreType.DMA],
        compiler_params=params, name="moe_combine",
    )(pos, wsel.T, y_p)


def _moe(t, router_w, router_b, w_gu, w_down, sh_gu, sh_down):
    routed = _moe_routed(t, router_w, router_b, w_gu, w_down)
    au = _matmul(t, sh_gu)
    ff = sh_gu.shape[-1] // 2
    shared = _matmul(jax.nn.silu(au[:, :ff]) * au[:, ff:], sh_down)
    return routed + shared


def kernel(x, c, ctx, c_ctx, ada_w, ada_b, norm_g, w_in, w_out, hy_conv_w, hy_conv_b, hy_ffn_w1, hy_ffn_b1, hy_ffn_w2, hy_ffn_b2, hy_ffn_w3, hy_freq, hy_bias, hy_norm, hg_lb_logits, hg_norm, gla_a_up, gla_a_b, gla_norm, ml_conv_w, ml_conv_b, ml_gate_b, ml_norm, router_w, router_b, exp_w_gu, exp_w_down, sh_w_gu, sh_w_down):
    bsz, seq, d = x.shape
    n_ctx = ctx.shape[1]
    depth = ada_w.shape[0]
    rows = seq // GRID_W
    x = x + _pos_embed_2d(rows, d)[None]
    xc = ctx
    lb_cum = jnp.cumsum(jax.nn.softmax(hg_lb_logits, axis=0), axis=0)
    lower_bounds = lb_cum - lb_cum[0:1]
    for l in range(depth):
        with_ctx = l < depth - 1
        mod = (jax.nn.silu(c) @ ada_w[l] + ada_b[l])[:, None, :]
        mod_c = jax.nn.silu(c_ctx) @ ada_w[l] + ada_b[l]
        sh1, sc1, g1, sh2, sc2, g2 = jnp.split(mod, 6, axis=-1)
        csh1, csc1, cg1, csh2, csc2, cg2 = jnp.split(mod_c, 6, axis=-1)
        h = _rms(x, norm_g[l, 0]) * (1.0 + sc1) + sh1
        hc = _rms(xc, norm_g[l, 0]) * (1.0 + csc1) + csh1
        h_all = jnp.concatenate([h.reshape(bsz * seq, d), hc.reshape(bsz * n_ctx, d)], axis=0)
        p_all = _matmul(h_all, w_in[l])
        pl_ = _split(p_all[:bsz * seq].reshape(bsz, seq, -1), IN_SPLITS)
        pc = _split(p_all[bsz * seq:].reshape(bsz, n_ctx, -1), IN_SPLITS)
        hy_args = (hy_conv_w[l], hy_conv_b[l], hy_ffn_w1[l], hy_ffn_b1[l], hy_ffn_w2[l], hy_ffn_b2[l],
                   hy_ffn_w3[l], hy_freq[l], hy_bias[l], hy_norm[l])
        y_hy = _hyena(pl_[0:3], *hy_args)
        yc_hg, y_hg = _hgrn2(pc[3:8], pl_[3:8], lower_bounds[l], hg_norm[l], with_ctx)
        yc_gla, y_gla = _gla(pc[8:14], pl_[8:14], gla_a_up[l], gla_a_b[l], gla_norm[l], with_ctx)
        yc_ml, y_ml = _mlstm(pc[14:19], pl_[14:19], ml_conv_w[l], ml_conv_b[l], ml_gate_b[l], ml_norm[l], with_ctx)
        y_cat = jnp.concatenate([y_hy, y_hg, y_gla, y_ml], axis=-1).reshape(bsz * seq, d)
        if with_ctx:
            yc_cat = jnp.concatenate([_hyena(pc[0:3], *hy_args), yc_hg, yc_gla, yc_ml],
                                     axis=-1).reshape(bsz * n_ctx, d)
            y_all = _matmul(jnp.concatenate([y_cat, yc_cat], axis=0), w_out[l])
            y = y_all[:bsz * seq].reshape(bsz, seq, d)
            yc = y_all[bsz * seq:].reshape(bsz, n_ctx, d)
        else:
            y = _matmul(y_cat, w_out[l]).reshape(bsz, seq, d)
        x = x + g1 * _rms(y, norm_g[l, 1])
        h = _rms(x, norm_g[l, 2]) * (1.0 + sc2) + sh2
        moe_args = (router_w[l], router_b[l], exp_w_gu[l], exp_w_down[l], sh_w_gu[l], sh_w_down[l])
        if with_ctx:
            xc = xc + cg1 * _rms(yc, norm_g[l, 1])
            hc = _rms(xc, norm_g[l, 2]) * (1.0 + csc2) + csh2
            f = _moe(jnp.concatenate([h.reshape(bsz * seq, d), hc.reshape(bsz * n_ctx, d)], axis=0), *moe_args)
            xc = xc + cg2 * _rms(f[bsz * seq:].reshape(bsz, n_ctx, d), norm_g[l, 3])
            f = f[:bsz * seq]
        else:
            f = _moe(h.reshape(bsz * seq, d), *moe_args)
        x = x + g2 * _rms(f.reshape(bsz, seq, d), norm_g[l, 3])
    return x
```

```python
import functools
import math

import jax
import jax.numpy as jnp
from jax import lax
from jax.experimental import pallas as pl
from jax.experimental.pallas import tpu as pltpu

F32 = jnp.float32
BF16 = jnp.bfloat16

D_MODEL = 1024
GRID_W = 64
EPS = 1e-6
POS_BASE = 10000.0
GROUP_W = D_MODEL // 4
SHORT_CONV = 3
HY_W = GROUP_W
HY_ORDER = 2
HY_EMB = 33
HY_BANDS = (HY_EMB - 1) // 2
HY_FAST_DECAY = 0.3
HY_SLOW_DECAY = 1.5
HY_DECAY_TARGET = 1e-2
HG_H = 4
HG_W = GROUP_W
HG_DK = HG_W // HG_H
GLA_H = 4
GLA_KW = GROUP_W // 2
GLA_VW = GROUP_W
GLA_DK = GLA_KW // GLA_H
GLA_DV = GLA_VW // GLA_H
GLA_RANK = 16
GLA_NORMALIZER = 16.0
ML_H = 4
ML_W = GROUP_W
ML_DH = ML_W // ML_H
CHUNK_GATED = 16
CHUNK_ML = 64
N_EXPERTS = 256
TOP_K = 8
N_EXPERT_GROUPS = 8
TOPK_GROUPS = 4
EXPERT_FF = 256
ROUTED_SCALE = 2.5
IN_SPLITS = (HY_W, HY_W, HY_W,
             HG_W, HG_W, HG_W, HG_W, HG_W,
             GLA_KW, GLA_KW, GLA_VW, GLA_RANK, GLA_RANK, GLA_VW,
             ML_W, ML_W, ML_W, 4 * ML_H, ML_W)

LANE = 128
ROW_TILE = 512
MOE_ROWS = 256
ROUTER_TILE = 256
SCATTER_TILE = 256
GLR_TILE = 128
VMEM_LIMIT = 56 * 1024 * 1024
NEG_INF = float("-inf")


def _mm_kernel(x_ref, w_ref, o_ref, *, precise):
    if precise:
        o_ref[...] = jnp.dot(x_ref[...].astype(F32), w_ref[...], preferred_element_type=F32,
                             precision=lax.Precision.HIGHEST)
    else:
        o_ref[...] = jnp.dot(x_ref[...].astype(BF16), w_ref[...], preferred_element_type=F32)


def _matmul(x, w, precise=False):
    m, k = x.shape
    n = w.shape[1]
    n_pad = -n % LANE
    w = w.astype(F32 if precise else BF16)
    if n_pad:
        w = jnp.pad(w, ((0, 0), (0, n_pad)))
    tm = ROW_TILE
    assert m % tm == 0
    out = pl.pallas_call(
        functools.partial(_mm_kernel, precise=precise),
        out_shape=jax.ShapeDtypeStruct((m, n + n_pad), F32),
        grid=(m // tm,),
        in_specs=[pl.BlockSpec((tm, k), lambda i: (i, 0)),
                  pl.BlockSpec((k, n + n_pad), lambda i: (0, 0))],
        out_specs=pl.BlockSpec((tm, n + n_pad), lambda i: (i, 0)),
        compiler_params=pltpu.CompilerParams(dimension_semantics=("arbitrary",),
                                             vmem_limit_bytes=VMEM_LIMIT),
        name="dense_matmul",
    )(x, w)
    return out[:, :n] if n_pad else out


def _rms(x, g):
    return x * lax.rsqrt(jnp.mean(x * x, axis=-1, keepdims=True) + EPS) * g


def _split(p, sizes):
    out = []
    o = 0
    for s in sizes:
        out.append(p[..., o:o + s])
        o += s
    return out


def _heads(a, n):
    b, t, w = a.shape
    return a.reshape(b, t, n, w // n).transpose(0, 2, 1, 3)


def _head_rms(o, g):
    o = o * lax.rsqrt(jnp.mean(o * o, axis=-1, keepdims=True) + EPS)
    b, n, t, d = o.shape
    return o.transpose(0, 2, 1, 3).reshape(b, t, n * d) * g


def _flip(a):
    return jnp.flip(a, axis=2)


def _short_conv(u, w, b):
    ch = u.shape[-1]
    y = lax.conv_general_dilated(u, w[:, None, :], (1,), ((SHORT_CONV // 2, SHORT_CONV // 2),),
                                 dimension_numbers=('NWC', 'WIO', 'NWC'), feature_group_count=ch)
    return y + b


def _pos_embed_2d(rows, d):
    r = jnp.repeat(jnp.arange(rows, dtype=F32), GRID_W)
    col = (jnp.arange(rows * GRID_W) % GRID_W).astype(F32)
    quarter = d // 4
    omega = 1.0 / (POS_BASE ** (jnp.arange(quarter, dtype=F32) / quarter))

    def axis_emb(p):
        ang = p[:, None] * omega[None, :]
        return jnp.concatenate([jnp.sin(ang), jnp.cos(ang)], axis=-1)

    return jnp.concatenate([axis_emb(r), axis_emb(col)], axis=-1)


def _hyena_spectra(L, w1, b1, w2, b2, w3, freq):
    t = jnp.linspace(0.0, 1.0, L, dtype=F32)[:, None]
    w = 2.0 * math.pi * jnp.arange(L, dtype=F32)[:, None] / L
    bands = jnp.linspace(1e-4, HY_BANDS - 1, HY_BANDS, dtype=F32)[None, :]
    feats = jnp.concatenate([t, jnp.cos(bands * w), -jnp.sin(bands * w)], axis=-1)
    z = jnp.sin(freq[0] * (feats @ w1 + b1))
    z = jnp.sin(freq[1] * (z @ w2 + b2))
    h = (z @ w3).reshape(L, HY_ORDER, 2, HY_W)
    max_decay = math.log(HY_DECAY_TARGET) / HY_FAST_DECAY
    min_decay = math.log(HY_DECAY_TARGET) / HY_SLOW_DECAY
    deltas = jnp.abs(jnp.linspace(min_decay, max_decay, HY_W, dtype=F32))
    h = h * jnp.exp(-t[:, :, None, None] * deltas)
    fwd = h[:, :, 0]
    bwd = h[1:, :, 1][::-1]
    l1 = jnp.sum(jnp.abs(fwd), axis=0) + jnp.sum(jnp.abs(bwd), axis=0)
    filt = jnp.concatenate([fwd, jnp.zeros((1, HY_ORDER, HY_W), F32), bwd], axis=0) / l1
    return jnp.fft.rfft(filt, axis=0)


def _fft_conv(u, spec, bias):
    L = u.shape[1]
    y = jnp.fft.irfft(jnp.fft.rfft(u, n=2 * L, axis=1) * spec, n=2 * L, axis=1)[:, :L]
    return y + u * bias


def _hyena(parts, conv_w, conv_b, w1, b1, w2, b2, w3, freq, bias, norm_g):
    u = _short_conv(jnp.concatenate(parts, axis=-1), conv_w, conv_b)
    v, x1, x2 = u[..., :HY_W], u[..., HY_W:2 * HY_W], u[..., 2 * HY_W:]
    spec = _hyena_spectra(u.shape[1], w1, b1, w2, b2, w3, freq)
    z = x1 * _fft_conv(v, spec[:, 0], bias[0])
    y = x2 * _fft_conv(z, spec[:, 1], bias[1])
    return _rms(y, norm_g)


def _glr_direction(q, k, v, g, st_ref, reverse, nh):
    tc, hk = k.shape
    hv = v.shape[1]
    c = CHUNK_GATED
    hi = lax.Precision.HIGHEST
    ti = lax.broadcasted_iota(jnp.int32, (tc, tc), 0)
    tj = lax.broadcasted_iota(jnp.int32, (tc, tc), 1)
    same = (ti // c) == (tj // c)
    seen = (tj >= ti) if reverse else (tj <= ti)
    bcum = jnp.dot(jnp.where(same, jnp.where(seen, 1.0, 0.0), 0.0), g, precision=hi, preferred_element_type=F32)
    btot = jnp.dot(jnp.where(same, 1.0, 0.0), g, precision=hi, preferred_element_type=F32)
    qd = q * jnp.exp(bcum)
    kd = k * jnp.exp(btot - bcum)
    dec = jnp.exp(btot)
    head_sum = jnp.where(lax.broadcasted_iota(jnp.int32, (hk, hv), 0) // (hk // nh)
                         == lax.broadcasted_iota(jnp.int32, (hk, hv), 1) // (hv // nh), 1.0, 0.0).astype(BF16)
    in_chunk = lax.broadcasted_iota(jnp.int32, (tc, hk), 0) % c
    o = jnp.zeros((tc, hv), F32)
    for lag in range(c):
        if lag == 0:
            ks, bs, vs = k, bcum, v
        else:
            shift = tc - lag if reverse else lag
            ks, bs, vs = pltpu.roll(k, shift, 0), pltpu.roll(bcum, shift, 0), pltpu.roll(v, shift, 0)
        valid = (in_chunk + lag <= c - 1) if reverse else (in_chunk >= lag)
        x = q * ks * jnp.exp(jnp.where(valid, bcum - bs, NEG_INF))
        o = o + jnp.dot(x.astype(BF16), head_sum, preferred_element_type=F32) * vs
    head_mask = (lax.broadcasted_iota(jnp.int32, (hv, hk), 0) // (hv // nh)
                 == lax.broadcasted_iota(jnp.int32, (hv, hk), 1) // (hk // nh))
    st = st_ref[...]
    nch = tc // c
    outs = [None] * nch
    for ci in (range(nch - 1, -1, -1) if reverse else range(nch)):
        sl = slice(ci * c, (ci + 1) * c)
        outs[ci] = lax.dot_general(qd[sl].astype(BF16), st.astype(BF16), (((1,), (1,)), ((), ())),
                                   preferred_element_type=F32)
        ds = lax.dot_general(v[sl].astype(BF16), kd[sl].astype(BF16), (((0,), (0,)), ((), ())),
                             preferred_element_type=F32)
        st = st * dec[ci * c:ci * c + 1] + jnp.where(head_mask, ds, 0.0)
    st_ref[...] = st
    return o + jnp.concatenate(outs, axis=0)


def _glr_kernel(qf_ref, kf_ref, vf_ref, gf_ref, qb_ref, kb_ref, vb_ref, gb_ref, s0f_ref, s0b_ref,
                of_ref, ob_ref, sf_ref, sb_ref, stf, stb, *, nh):
    j = pl.program_id(1)

    @pl.when(j == 0)
    def _():
        stf[...] = s0f_ref[...]
        stb[...] = s0b_ref[...]

    of_ref[...] = _glr_direction(qf_ref[...], kf_ref[...], vf_ref[...], gf_ref[...], stf, False, nh)
    ob_ref[...] = _glr_direction(qb_ref[...], kb_ref[...], vb_ref[...], gb_ref[...], stb, True, nh)

    @pl.when(j == pl.num_programs(1) - 1)
    def _():
        sf_ref[...] = stf[...]
        sb_ref[...] = stb[...]


def _glr_bidir(q, k_f, k_b, v, g_f, g_b, s0_f, s0_b, nh):
    bsz, t, hk = k_f.shape
    hv = v.shape[-1]
    tc = GLR_TILE
    nsb = t // tc
    fwd = lambda w: pl.BlockSpec((None, tc, w), lambda b, j: (b, j, 0))
    bwd = lambda w: pl.BlockSpec((None, tc, w), lambda b, j: (b, nsb - 1 - j, 0))
    st = pl.BlockSpec((None, hv, hk), lambda b, j: (b, 0, 0))
    return pl.pallas_call(
        functools.partial(_glr_kernel, nh=nh),
        out_shape=[jax.ShapeDtypeStruct((bsz, t, hv), F32), jax.ShapeDtypeStruct((bsz, t, hv), F32),
                   jax.ShapeDtypeStruct((bsz, hv, hk), F32), jax.ShapeDtypeStruct((bsz, hv, hk), F32)],
        grid=(bsz, nsb),
        in_specs=[fwd(hk), fwd(hk), fwd(hv), fwd(hk), bwd(hk), bwd(hk), bwd(hv), bwd(hk), st, st],
        out_specs=[fwd(hv), bwd(hv), st, st],
        scratch_shapes=[pltpu.VMEM((hv, hk), F32), pltpu.VMEM((hv, hk), F32)],
        compiler_params=pltpu.CompilerParams(dimension_semantics=("arbitrary", "arbitrary"),
                                             vmem_limit_bytes=VMEM_LIMIT),
        name="gated_linear_recurrence",
    )(q, k_f, v, g_f, q, k_b, v, g_b, s0_f, s0_b)


def _head_rms_tokens(o, g, nh):
    b, t, w = o.shape
    oh = o.reshape(b, t, nh, w // nh)
    oh = oh * lax.rsqrt(jnp.mean(oh * oh, axis=-1, keepdims=True) + EPS)
    return oh.reshape(b, t, w) * g


def _hgrn2_seq(parts, lb, norm_g, s0_f, s0_b, with_out):
    q, i, zf, zb, g = parts
    log_lb = jnp.log(lb)
    log_ub = jnp.log1p(-lb)

    def gate(z):
        return (1.0 - lb) * jax.nn.sigmoid(-z), jnp.logaddexp(log_lb, log_ub + jax.nn.log_sigmoid(z))

    k_f, lf_f = gate(zf)
    k_b, lf_b = gate(zb)
    o_f, o_b, s_f, s_b = _glr_bidir(jax.nn.silu(q), k_f, k_b, i, lf_f, lf_b, s0_f, s0_b, HG_H)
    y = _head_rms_tokens(o_f + o_b, norm_g, HG_H) * jax.nn.silu(g) if with_out else None
    return y, s_f, s_b


def _hgrn2(pc, pl_, lb, norm_g, with_ctx):
    s0 = jnp.zeros((pl_[0].shape[0], HG_W, HG_W), F32)
    yc, s_f, s_b = _hgrn2_seq(pc, lb, norm_g, s0, s0, with_ctx)
    y, _, _ = _hgrn2_seq(pl_, lb, norm_g, s_f, s_b, True)
    return yc, y


def _gla_seq(parts, a_up, a_b, norm_g, s0_f, s0_b, with_out):
    q, k, v, af, ab, r = parts
    g_f = jax.nn.log_sigmoid(af @ a_up[0] + a_b[0]) / GLA_NORMALIZER
    g_b = jax.nn.log_sigmoid(ab @ a_up[1] + a_b[1]) / GLA_NORMALIZER
    o_f, o_b, s_f, s_b = _glr_bidir(q * GLA_DK ** -0.5, k, k, v, g_f, g_b, s0_f, s0_b, GLA_H)
    y = _head_rms_tokens(o_f + o_b, norm_g, GLA_H) * jax.nn.silu(r) if with_out else None
    return y, s_f, s_b


def _gla(pc, pl_, a_up, a_b, norm_g, with_ctx):
    s0 = jnp.zeros((pl_[0].shape[0], GLA_VW, GLA_KW), F32)
    yc, s_f, s_b = _gla_seq(pc, a_up, a_b, norm_g, s0, s0, with_ctx)
    y, _, _ = _gla_seq(pl_, a_up, a_b, norm_g, s_f, s_b, True)
    return yc, y


def _mlstm_dir(q, k, v, ig, lf, state0, with_out):
    bsz, nh, T, dh = k.shape
    C = CHUNK_ML
    n = T // C
    kc = k.reshape(bsz, nh, n, C, dh)
    vc = v.reshape(bsz, nh, n, C, dh)
    igc = ig.reshape(bsz, nh, n, C)
    b = jnp.cumsum(lf.reshape(bsz, nh, n, C), axis=-1)
    bl = b[..., -1]
    a = bl[..., None] - b + igc
    ma = jnp.max(a, axis=-1)
    w = jnp.exp(a - ma[..., None])
    ds = jnp.einsum('bhnc,bhncd,bhncv->bhndv', w, kc, vc)
    dn = jnp.einsum('bhnc,bhncd->bhnd', w, kc)

    def step(carry, inp):
        s, nv, m = carry
        bl_, ma_, ds_, dn_ = inp
        m_new = jnp.maximum(bl_ + m, ma_)
        d_old = jnp.exp(bl_ + m - m_new)
        d_new = jnp.exp(ma_ - m_new)
        s_new = d_old[..., None, None] * s + d_new[..., None, None] * ds_
        n_new = d_old[..., None] * nv + d_new[..., None] * dn_
        return (s_new, n_new, m_new), (s, nv, m)

    mv = lambda t_: jnp.moveaxis(t_, 2, 0)
    fin, prev = lax.scan(step, state0, (mv(bl), mv(ma), mv(ds), mv(dn)))
    if not with_out:
        return None, fin
    s_prev, n_prev, m_prev = [jnp.moveaxis(p, 0, 2) for p in prev]
    qc = q.reshape(bsz, nh, n, C, dh)
    causal = jnp.tril(jnp.ones((C, C), dtype=bool))
    dmat = jnp.where(causal, b[..., :, None] - b[..., None, :] + igc[..., None, :], -jnp.inf)
    inter = b + m_prev[..., None]
    m_t = jnp.maximum(inter, jnp.max(dmat, axis=-1))
    wq = jnp.exp(dmat - m_t[..., None]) * jnp.einsum('bhntd,bhnsd->bhnts', qc, kc)
    w_int = jnp.exp(inter - m_t)
    num = (jnp.einsum('bhnts,bhnsv->bhntv', wq, vc)
           + w_int[..., None] * jnp.einsum('bhntd,bhndv->bhntv', qc, s_prev))
    den = jnp.sum(wq, axis=-1) + w_int * jnp.einsum('bhntd,bhnd->bhnt', qc, n_prev)
    h = num / jnp.maximum(jnp.abs(den), jnp.exp(-m_t))[..., None]
    return h.reshape(bsz, nh, T, dh), fin


def _mlstm_seq(parts, conv_w, conv_b, gate_b, norm_g, st_f, st_b, with_out):
    q, k, v, gates, o = parts
    qk = jax.nn.silu(_short_conv(jnp.concatenate([q, k], axis=-1), conv_w, conv_b))
    qh = _heads(qk[..., :ML_W], ML_H)
    kh = _heads(qk[..., ML_W:], ML_H) * ML_DH ** -0.5
    vh = _heads(v, ML_H)
    bsz, T, _ = gates.shape
    gt = (gates.reshape(bsz, T, 4, ML_H) + gate_b).transpose(2, 0, 3, 1)
    h_f, fin_f = _mlstm_dir(qh, kh, vh, gt[0], jax.nn.log_sigmoid(gt[1]), st_f, with_out)
    h_b, fin_b = _mlstm_dir(_flip(qh), _flip(kh), _flip(vh), _flip(gt[2]),
                            _flip(jax.nn.log_sigmoid(gt[3])), st_b, with_out)
    y = jax.nn.sigmoid(o) * _head_rms(h_f + _flip(h_b), norm_g) if with_out else None
    return y, fin_f, fin_b


def _mlstm(pc, pl_, conv_w, conv_b, gate_b, norm_g, with_ctx):
    bsz = pl_[0].shape[0]
    st0 = (jnp.zeros((bsz, ML_H, ML_DH, ML_DH), F32), jnp.zeros((bsz, ML_H, ML_DH), F32),
           jnp.zeros((bsz, ML_H), F32))
    yc, st_f, st_b = _mlstm_seq(pc, conv_w, conv_b, gate_b, norm_g, st0, st0, with_ctx)
    y, _, _ = _mlstm_seq(pl_, conv_w, conv_b, gate_b, norm_g, st_f, st_b, True)
    return yc, y


def _mlstm_direction(q, k, v, igx, lfx, s_ref, n_ref, m_ref, reverse, nh):
    tc, w = q.shape
    seg = w // nh
    assert tc == seg
    hi = lax.Precision.HIGHEST
    ti = lax.broadcasted_iota(jnp.int32, (tc, tc), 0)
    tj = lax.broadcasted_iota(jnp.int32, (tc, tc), 1)
    seen = (tj >= ti) if reverse else (tj <= ti)
    b = jnp.dot(jnp.where(seen, 1.0, 0.0), lfx, precision=hi, preferred_element_type=F32)
    bl = b[0:1] if reverse else b[tc - 1:tc]
    a = bl - b + igx
    ma = jnp.max(a, axis=0, keepdims=True)
    kw = jnp.exp(a - ma) * k
    s_prev = s_ref[...]
    n_prev = n_ref[...]
    m_prev = m_ref[...]
    lane = lax.broadcasted_iota(jnp.int32, (tc, w), 1)
    row = lax.broadcasted_iota(jnp.int32, (tc, w), 0)
    same_head = (lax.broadcasted_iota(jnp.int32, (w, w), 0) // seg
                 == lax.broadcasted_iota(jnp.int32, (w, w), 1) // seg)
    kexp = jnp.where(same_head, jnp.concatenate([k] * nh, axis=0), 0.0)
    vexp = jnp.where(same_head, jnp.concatenate([v] * nh, axis=0), 0.0)
    scores = lax.dot_general(q.astype(BF16), kexp.astype(BF16), (((1,), (1,)), ((), ())),
                             preferred_element_type=F32)
    s_lane = lane % seg
    by_src = jnp.sum(jnp.where(s_lane == row, igx - b, 0.0), axis=0, keepdims=True)
    ok = (s_lane >= row) if reverse else (s_lane <= row)
    dmat = jnp.where(ok, b + by_src, NEG_INF)
    inter = b + m_prev
    head_of_lane = lane // seg
    seg_max = jnp.full((tc, w), NEG_INF, F32)
    for h in range(nh):
        in_h = head_of_lane == h
        seg_max = jnp.where(in_h, jnp.max(jnp.where(in_h, dmat, NEG_INF), axis=1, keepdims=True), seg_max)
    m_t = jnp.maximum(inter, seg_max)
    wq = jnp.exp(dmat - m_t) * scores
    w_int = jnp.exp(inter - m_t)
    head_sum = jnp.where(same_head, 1.0, 0.0).astype(BF16)
    num = (jnp.dot(wq.astype(BF16), vexp.astype(BF16), preferred_element_type=F32)
           + w_int * jnp.dot(q.astype(BF16), s_prev.astype(BF16), preferred_element_type=F32))
    den = (jnp.dot(wq.astype(BF16), head_sum, preferred_element_type=F32)
           + w_int * jnp.dot((q * n_prev).astype(BF16), head_sum, preferred_element_type=F32))
    h_out = num / jnp.maximum(jnp.abs(den), jnp.exp(-m_t))
    m_new = jnp.maximum(bl + m_prev, ma)
    d_old = jnp.exp(bl + m_prev - m_new)
    d_new = jnp.exp(ma - m_new)
    ds = lax.dot_general(kw.astype(BF16), v.astype(BF16), (((0,), (0,)), ((), ())), preferred_element_type=F32)
    s_ref[...] = d_old * s_prev + d_new * jnp.where(same_head, ds, 0.0)
    n_ref[...] = d_old * n_prev + d_new * jnp.sum(kw, axis=0, keepdims=True)
    m_ref[...] = m_new
    return h_out


def _mlstm_kernel(qf_ref, kf_ref, vf_ref, igf_ref, lff_ref, qb_ref, kb_ref, vb_ref, igb_ref, lfb_ref,
                  s0f_ref, n0f_ref, m0f_ref, s0b_ref, n0b_ref, m0b_ref,
                  hf_ref, hb_ref, sf_ref, nf_ref, mf_ref, sb_ref, nb_ref, mb_ref,
                  s_f, n_f, m_f, s_b, n_b, m_b, *, nh):
    j = pl.program_id(1)

    @pl.when(j == 0)
    def _():
        s_f[...] = s0f_ref[...]
        n_f[...] = n0f_ref[...]
        m_f[...] = m0f_ref[...]
        s_b[...] = s0b_ref[...]
        n_b[...] = n0b_ref[...]
        m_b[...] = m0b_ref[...]

    hf_ref[...] = _mlstm_direction(qf_ref[...], kf_ref[...], vf_ref[...], igf_ref[...], lff_ref[...],
                                   s_f, n_f, m_f, False, nh)
    hb_ref[...] = _mlstm_direction(qb_ref[...], kb_ref[...], vb_ref[...], igb_ref[...], lfb_ref[...],
                                   s_b, n_b, m_b, True, nh)

    @pl.when(j == pl.num_programs(1) - 1)
    def _():
        sf_ref[...] = s_f[...]
        nf_ref[...] = n_f[...]
        mf_ref[...] = m_f[...]
        sb_ref[...] = s_b[...]
        nb_ref[...] = n_b[...]
        mb_ref[...] = m_b[...]


def _mlstm_bidir(q, k, v, ig_f, lf_f, ig_b, lf_b, st_f, st_b, nh):
    bsz, t, w = q.shape
    tc = CHUNK_ML
    nsb = t // tc
    fwd = pl.BlockSpec((None, tc, w), lambda b, j: (b, j, 0))
    bwd = pl.BlockSpec((None, tc, w), lambda b, j: (b, nsb - 1 - j, 0))
    mat = pl.BlockSpec((None, w, w), lambda b, j: (b, 0, 0))
    vec = pl.BlockSpec((None, 1, w), lambda b, j: (b, 0, 0))
    sds = jax.ShapeDtypeStruct
    state_shapes = [sds((bsz, w, w), F32), sds((bsz, 1, w), F32), sds((bsz, 1, w), F32)]
    outs = pl.pallas_call(
        functools.partial(_mlstm_kernel, nh=nh),
        out_shape=[sds((bsz, t, w), F32), sds((bsz, t, w), F32)] + state_shapes + state_shapes,
        grid=(bsz, nsb),
        in_specs=[fwd] * 5 + [bwd] * 5 + [mat, vec, vec] * 2,
        out_specs=[fwd, bwd] + [mat, vec, vec] * 2,
        scratch_shapes=[pltpu.VMEM((w, w), F32), pltpu.VMEM((1, w), F32), pltpu.VMEM((1, w), F32)] * 2,
        compiler_params=pltpu.CompilerParams(dimension_semantics=("arbitrary", "arbitrary"),
                                             vmem_limit_bytes=VMEM_LIMIT),
        name="mlstm_recurrence",
    )(q, k, v, ig_f, lf_f, q, k, v, ig_b, lf_b, *st_f, *st_b)
    return outs[0], outs[1], tuple(outs[2:5]), tuple(outs[5:8])


def _mlstm_seq_p(parts, conv_w, conv_b, gate_b, norm_g, st_f, st_b, with_out):
    q, k, v, gates, o = parts
    qk = jax.nn.silu(_short_conv(jnp.concatenate([q, k], axis=-1), conv_w, conv_b))
    bsz, t, _ = gates.shape
    gt = gates.reshape(bsz, t, 4, ML_H) + gate_b
    expand = lambda a: jnp.repeat(a, ML_DH, axis=-1)
    h_f, h_b, fin_f, fin_b = _mlstm_bidir(
        qk[..., :ML_W], qk[..., ML_W:] * ML_DH ** -0.5, v,
        expand(gt[:, :, 0]), expand(jax.nn.log_sigmoid(gt[:, :, 1])),
        expand(gt[:, :, 2]), expand(jax.nn.log_sigmoid(gt[:, :, 3])), st_f, st_b, ML_H)
    y = jax.nn.sigmoid(o) * _head_rms_tokens(h_f + h_b, norm_g, ML_H) if with_out else None
    return y, fin_f, fin_b


def _mlstm_p(pc, pl_, conv_w, conv_b, gate_b, norm_g, with_ctx):
    bsz = pl_[0].shape[0]
    st0 = (jnp.zeros((bsz, ML_W, ML_W), F32), jnp.zeros((bsz, 1, ML_W), F32), jnp.zeros((bsz, 1, ML_W), F32))
    yc, st_f, st_b = _mlstm_seq_p(pc, conv_w, conv_b, gate_b, norm_g, st0, st0, with_ctx)
    y, _, _ = _mlstm_seq_p(pl_, conv_w, conv_b, gate_b, norm_g, st_f, st_b, True)
    return yc, y


def _router_kernel(h_ref, wt_ref, b_ref, eidx_ref, wsel_ref, cnt_ref):
    i = pl.program_id(0)
    tm = h_ref.shape[0]
    ne = wt_ref.shape[0]
    per_group = ne // N_EXPERT_GROUPS
    logits = lax.dot_general(wt_ref[...], h_ref[...], (((1,), (1,)), ((), ())),
                             preferred_element_type=F32, precision=lax.Precision.HIGHEST)
    s = jax.nn.sigmoid(logits)
    sel = s + b_ref[...]
    row = lax.broadcasted_iota(jnp.int32, (ne, tm), 0)
    gs = []
    for g in range(N_EXPERT_GROUPS):
        blk = sel[g * per_group:(g + 1) * per_group]
        r = lax.broadcasted_iota(jnp.int32, blk.shape, 0)
        m1 = jnp.max(blk, axis=0, keepdims=True)
        i1 = jnp.min(jnp.where(blk == m1, r, per_group), axis=0, keepdims=True)
        m2 = jnp.max(jnp.where(r == i1, NEG_INF, blk), axis=0, keepdims=True)
        gs.append(m1 + m2)
    grp = jnp.concatenate(gs, axis=0)
    grow = lax.broadcasted_iota(jnp.int32, grp.shape, 0)
    gsel = jnp.zeros(grp.shape, F32)
    for _ in range(TOPK_GROUPS):
        m = jnp.max(grp, axis=0, keepdims=True)
        gi = jnp.min(jnp.where(grp == m, grow, N_EXPERT_GROUPS), axis=0, keepdims=True)
        hit = grow == gi
        gsel = jnp.where(hit, 1.0, gsel)
        grp = jnp.where(hit, NEG_INF, grp)
    masked = jnp.concatenate(
        [jnp.where(gsel[g:g + 1] > 0.0, sel[g * per_group:(g + 1) * per_group], NEG_INF)
         for g in range(N_EXPERT_GROUPS)], axis=0)
    eis, ws = [], []
    picked = jnp.zeros((ne, tm), F32)
    for _ in range(TOP_K):
        m = jnp.max(masked, axis=0, keepdims=True)
        ei = jnp.min(jnp.where(masked == m, row, ne), axis=0, keepdims=True)
        hit = row == ei
        ws.append(jnp.sum(jnp.where(hit, s, 0.0), axis=0, keepdims=True))
        eis.append(ei)
        picked = jnp.where(hit, 1.0, picked)
        masked = jnp.where(hit, NEG_INF, masked)
    w = jnp.concatenate(ws, axis=0)
    eidx_ref[...] = jnp.concatenate(eis, axis=0)
    wsel_ref[...] = w / jnp.sum(w, axis=0, keepdims=True) * ROUTED_SCALE
    tot = jnp.dot(picked.astype(BF16), jnp.ones((tm, LANE), BF16), preferred_element_type=F32)

    @pl.when(i == 0)
    def _():
        cnt_ref[...] = jnp.zeros_like(cnt_ref)

    cnt_ref[...] += tot


def _pos_kernel(eidx_ref, base_ref, pos_ref, carry_ref):
    i = pl.program_id(0)
    tm = eidx_ref.shape[1]
    ne = base_ref.shape[0]

    @pl.when(i == 0)
    def _():
        carry_ref[...] = jnp.zeros_like(carry_ref)

    eidx = eidx_ref[...]
    row = lax.broadcasted_iota(jnp.int32, (ne, tm), 0)
    picked = jnp.zeros((ne, tm), F32)
    for k in range(TOP_K):
        picked = jnp.where(row == eidx[k:k + 1], 1.0, picked)
    pb = picked.astype(BF16)
    before = jnp.where(lax.broadcasted_iota(jnp.int32, (tm, tm), 0) < lax.broadcasted_iota(jnp.int32, (tm, tm), 1),
                       1.0, 0.0).astype(BF16)
    rank = jnp.dot(pb, before, preferred_element_type=F32)
    tot = jnp.dot(pb, jnp.ones((tm, LANE), BF16), preferred_element_type=F32)
    dest = rank + (base_ref[...] + carry_ref[:, 0:1])
    pos = [jnp.sum(jnp.where(row == eidx[k:k + 1], dest, 0.0), axis=0, keepdims=True) for k in range(TOP_K)]
    pos_ref[...] = jnp.concatenate(pos, axis=0).astype(jnp.int32)
    carry_ref[...] += tot


def _dispatch_kernel(pos_ref, h_ref, xs_init_ref, xs_ref, sem):
    del xs_init_ref
    tm = h_ref.shape[0]

    def row_copy(n, k):
        return pltpu.make_async_copy(h_ref.at[pl.ds(n, 1)], xs_ref.at[pl.ds(pos_ref[k, n], 1)], sem)

    def issue(n, carry):
        for k in range(TOP_K):
            row_copy(n, k).start()
        return carry

    def drain(n, carry):
        for k in range(TOP_K):
            row_copy(n, k).wait()
        return carry

    lax.fori_loop(0, tm, issue, 0)
    lax.fori_loop(0, tm, drain, 0)


def _moe_ffn_kernel(blk_e_ref, nused_ref, x_ref, wgu_ref, wdn_ref, o_ref, wgu_s, wdn_s):
    i = pl.program_id(0)
    e = blk_e_ref[i]
    e_prev = blk_e_ref[jnp.maximum(i - 1, 0)]

    @pl.when(i < nused_ref[0])
    def _():
        @pl.when((i == 0) | (e != e_prev))
        def _():
            wgu_s[...] = wgu_ref[...].astype(BF16)
            wdn_s[...] = wdn_ref[...].astype(BF16)

        au = jnp.dot(x_ref[...].astype(BF16), wgu_s[...], preferred_element_type=F32)
        a = au[:, :EXPERT_FF]
        u = au[:, EXPERT_FF:]
        h = (a * jax.nn.sigmoid(a)) * u
        o_ref[...] = jnp.dot(h.astype(BF16), wdn_s[...], preferred_element_type=F32)

    @pl.when(i >= nused_ref[0])
    def _():
        o_ref[...] = jnp.zeros_like(o_ref)


def _combine_kernel(pos_ref, w_ref, y_hbm, o_ref, buf, sem):
    tm = o_ref.shape[0]

    def row_copy(n, k):
        return pltpu.make_async_copy(y_hbm.at[pl.ds(pos_ref[k, n], 1)], buf.at[k, pl.ds(n, 1)], sem)

    def issue(n, carry):
        for k in range(TOP_K):
            row_copy(n, k).start()
        return carry

    def drain(n, carry):
        for k in range(TOP_K):
            row_copy(n, k).wait()
        return carry

    lax.fori_loop(0, tm, issue, 0)
    lax.fori_loop(0, tm, drain, 0)
    acc = buf[0] * w_ref[:, 0:1]
    for k in range(1, TOP_K):
        acc = acc + buf[k] * w_ref[:, k:k + 1]
    o_ref[...] = acc


def _moe_routed(t, router_w, router_b, w_gu, w_down):
    n, d = t.shape
    ne = router_w.shape[1]
    ff2 = w_gu.shape[-1]
    params = pltpu.CompilerParams(dimension_semantics=("arbitrary",), vmem_limit_bytes=VMEM_LIMIT)
    tm = ROUTER_TILE
    eidx, wsel, cnt = pl.pallas_call(
        _router_kernel,
        out_shape=[jax.ShapeDtypeStruct((TOP_K, n), jnp.int32), jax.ShapeDtypeStruct((TOP_K, n), F32),
                   jax.ShapeDtypeStruct((ne, LANE), F32)],
        grid=(n // tm,),
        in_specs=[pl.BlockSpec((tm, d), lambda i: (i, 0)), pl.BlockSpec((ne, d), lambda i: (0, 0)),
                  pl.BlockSpec((ne, 1), lambda i: (0, 0))],
        out_specs=[pl.BlockSpec((TOP_K, tm), lambda i: (0, i)), pl.BlockSpec((TOP_K, tm), lambda i: (0, i)),
                   pl.BlockSpec((ne, LANE), lambda i: (0, 0))],
        compiler_params=params, name="moe_router",
    )(t, router_w.T, router_b.reshape(ne, 1))
    bm = MOE_ROWS
    counts = cnt[:, 0].astype(jnp.int32)
    padded = (counts + bm - 1) // bm * bm
    pad_end = jnp.cumsum(padded)
    pad_start = pad_end - padded
    n_blocks = (n * TOP_K + ne * (bm - 1)) // bm + 1
    blk_first = jnp.arange(n_blocks, dtype=jnp.int32) * bm
    blk_e = jnp.minimum(jnp.sum((pad_end[None, :] <= blk_first[:, None]).astype(jnp.int32), axis=1), ne - 1)
    n_used = (pad_end[-1] // bm).astype(jnp.int32).reshape(1)
    pos = pl.pallas_call(
        _pos_kernel,
        out_shape=jax.ShapeDtypeStruct((TOP_K, n), jnp.int32),
        grid=(n // tm,),
        in_specs=[pl.BlockSpec((TOP_K, tm), lambda i: (0, i)), pl.BlockSpec((ne, 1), lambda i: (0, 0))],
        out_specs=pl.BlockSpec((TOP_K, tm), lambda i: (0, i)),
        scratch_shapes=[pltpu.VMEM((ne, LANE), F32)],
        compiler_params=params, name="moe_positions",
    )(eidx, pad_start.astype(F32).reshape(ne, 1))
    ts = SCATTER_TILE
    p = n_blocks * bm
    pos_spec = pl.BlockSpec((TOP_K, ts), lambda i: (0, i), memory_space=pltpu.SMEM)
    xs = pl.pallas_call(
        _dispatch_kernel,
        out_shape=jax.ShapeDtypeStruct((p, d), F32),
        grid=(n // ts,),
        in_specs=[pos_spec, pl.BlockSpec((ts, d), lambda i: (i, 0)), pl.BlockSpec(memory_space=pl.ANY)],
        out_specs=pl.BlockSpec(memory_space=pl.ANY),
        scratch_shapes=[pltpu.SemaphoreType.DMA],
        input_output_aliases={2: 0},
        compiler_params=params, name="moe_dispatch",
    )(pos, t, jnp.zeros((p, d), F32))

    def x_map(i, blk_e, nused):
        return (jnp.minimum(i, nused[0] - 1), 0)

    def w_map(i, blk_e, nused):
        return (blk_e[i], 0, 0)

    y_p = pl.pallas_call(
        _moe_ffn_kernel,
        out_shape=jax.ShapeDtypeStruct((p, d), F32),
        grid_spec=pltpu.PrefetchScalarGridSpec(
            num_scalar_prefetch=2,
            grid=(n_blocks,),
            in_specs=[pl.BlockSpec((bm, d), x_map),
                      pl.BlockSpec((None, d, ff2), w_map),
                      pl.BlockSpec((None, ff2 // 2, d), w_map)],
            out_specs=pl.BlockSpec((bm, d), lambda i, blk_e, nused: (i, 0)),
            scratch_shapes=[pltpu.VMEM((d, ff2), BF16), pltpu.VMEM((ff2 // 2, d), BF16)],
        ),
        compiler_params=params, name="moe_expert_ffn",
    )(blk_e, n_used, xs, w_gu, w_down)
    return pl.pallas_call(
        _combine_kernel,
        out_shape=jax.ShapeDtypeStruct((n, d), F32),
        grid=(n // ts,),
        in_specs=[pos_spec, pl.BlockSpec((ts, TOP_K), lambda i: (i, 0)), pl.BlockSpec(memory_space=pl.ANY)],
        out_specs=pl.BlockSpec((ts, d), lambda i: (i, 0)),
        scratch_shapes=[pltpu.VMEM((TOP_K, ts, d), F32), pltpu.SemaphoreType.DMA],
        compiler_params=params, name="moe_combine",
    )(pos, wsel.T, y_p)


def _moe(t, router_w, router_b, w_gu, w_down, sh_gu, sh_down):
    routed = _moe_routed(t, router_w, router_b, w_gu, w_down)
    au = _matmul(t, sh_gu)
    ff = sh_gu.shape[-1] // 2
    shared = _matmul(jax.nn.silu(au[:, :ff]) * au[:, ff:], sh_down)
    return routed + shared


def kernel(x, c, ctx, c_ctx, ada_w, ada_b, norm_g, w_in, w_out, hy_conv_w, hy_conv_b, hy_ffn_w1, hy_ffn_b1, hy_ffn_w2, hy_ffn_b2, hy_ffn_w3, hy_freq, hy_bias, hy_norm, hg_lb_logits, hg_norm, gla_a_up, gla_a_b, gla_norm, ml_conv_w, ml_conv_b, ml_gate_b, ml_norm, router_w, router_b, exp_w_gu, exp_w_down, sh_w_gu, sh_w_down):
    bsz, seq, d = x.shape
    n_ctx = ctx.shape[1]
    depth = ada_w.shape[0]
    rows = seq // GRID_W
    x = x + _pos_embed_2d(rows, d)[None]
    xc = ctx
    lb_cum = jnp.cumsum(jax.nn.softmax(hg_lb_logits, axis=0), axis=0)
    lower_bounds = lb_cum - lb_cum[0:1]
    for l in range(depth):
        with_ctx = l < depth - 1
        mod = (jax.nn.silu(c) @ ada_w[l] + ada_b[l])[:, None, :]
        mod_c = jax.nn.silu(c_ctx) @ ada_w[l] + ada_b[l]
        sh1, sc1, g1, sh2, sc2, g2 = jnp.split(mod, 6, axis=-1)
        csh1, csc1, cg1, csh2, csc2, cg2 = jnp.split(mod_c, 6, axis=-1)
        h = _rms(x, norm_g[l, 0]) * (1.0 + sc1) + sh1
        hc = _rms(xc, norm_g[l, 0]) * (1.0 + csc1) + csh1
        h_all = jnp.concatenate([h.reshape(bsz * seq, d), hc.reshape(bsz * n_ctx, d)], axis=0)
        p_all = _matmul(h_all, w_in[l])
        pl_ = _split(p_all[:bsz * seq].reshape(bsz, seq, -1), IN_SPLITS)
        pc = _split(p_all[bsz * seq:].reshape(bsz, n_ctx, -1), IN_SPLITS)
        hy_args = (hy_conv_w[l], hy_conv_b[l], hy_ffn_w1[l], hy_ffn_b1[l], hy_ffn_w2[l], hy_ffn_b2[l],
                   hy_ffn_w3[l], hy_freq[l], hy_bias[l], hy_norm[l])
        y_hy = _hyena(pl_[0:3], *hy_args)
        yc_hg, y_hg = _hgrn2(pc[3:8], pl_[3:8], lower_bounds[l], hg_norm[l], with_ctx)
        yc_gla, y_gla = _gla(pc[8:14], pl_[8:14], gla_a_up[l], gla_a_b[l], gla_norm[l], with_ctx)
        yc_ml, y_ml = _mlstm_p(pc[14:19], pl_[14:19], ml_conv_w[l], ml_conv_b[l], ml_gate_b[l], ml_norm[l], with_ctx)
        y_cat = jnp.concatenate([y_hy, y_hg, y_gla, y_ml], axis=-1).reshape(bsz * seq, d)
        if with_ctx:
            yc_cat = jnp.concatenate([_hyena(pc[0:3], *hy_args), yc_hg, yc_gla, yc_ml],
                                     axis=-1).reshape(bsz * n_ctx, d)
            y_all = _matmul(jnp.concatenate([y_cat, yc_cat], axis=0), w_out[l])
            y = y_all[:bsz * seq].reshape(bsz, seq, d)
            yc = y_all[bsz * seq:].reshape(bsz, n_ctx, d)
        else:
            y = _matmul(y_cat, w_out[l]).reshape(bsz, seq, d)
        x = x + g1 * _rms(y, norm_g[l, 1])
        h = _rms(x, norm_g[l, 2]) * (1.0 + sc2) + sh2
        moe_args = (router_w[l], router_b[l], exp_w_gu[l], exp_w_down[l], sh_w_gu[l], sh_w_down[l])
        if with_ctx:
            xc = xc + cg1 * _rms(yc, norm_g[l, 1])
            hc = _rms(xc, norm_g[l, 2]) * (1.0 + csc2) + csh2
            f = _moe(jnp.concatenate([h.reshape(bsz * seq, d), hc.reshape(bsz * n_ctx, d)], axis=0), *moe_args)
            xc = xc + cg2 * _rms(f[bsz * seq:].reshape(bsz, n_ctx, d), norm_g[l, 3])
            f = f[:bsz * seq]
        else:
            f = _moe(h.reshape(bsz * seq, d), *moe_args)
        x = x + g2 * _rms(f.reshape(bsz, seq, d), norm_g[l, 3])
    return x
```

```python
import functools
import math

import jax
import jax.numpy as jnp
import numpy as np
from jax import lax
from jax.experimental import pallas as pl
from jax.experimental.pallas import tpu as pltpu

F32 = jnp.float32
BF16 = jnp.bfloat16

D_MODEL = 1024
GRID_W = 64
EPS = 1e-6
POS_BASE = 10000.0
GROUP_W = D_MODEL // 4
SHORT_CONV = 3
HY_W = GROUP_W
HY_ORDER = 2
HY_EMB = 33
HY_BANDS = (HY_EMB - 1) // 2
HY_FAST_DECAY = 0.3
HY_SLOW_DECAY = 1.5
HY_DECAY_TARGET = 1e-2
HG_H = 4
HG_W = GROUP_W
HG_DK = HG_W // HG_H
GLA_H = 4
GLA_KW = GROUP_W // 2
GLA_VW = GROUP_W
GLA_DK = GLA_KW // GLA_H
GLA_DV = GLA_VW // GLA_H
GLA_RANK = 16
GLA_NORMALIZER = 16.0
ML_H = 4
ML_W = GROUP_W
ML_DH = ML_W // ML_H
CHUNK_GATED = 16
CHUNK_ML = 64
N_EXPERTS = 256
TOP_K = 8
N_EXPERT_GROUPS = 8
TOPK_GROUPS = 4
EXPERT_FF = 256
ROUTED_SCALE = 2.5
IN_SPLITS = (HY_W, HY_W, HY_W,
             HG_W, HG_W, HG_W, HG_W, HG_W,
             GLA_KW, GLA_KW, GLA_VW, GLA_RANK, GLA_RANK, GLA_VW,
             ML_W, ML_W, ML_W, 4 * ML_H, ML_W)

LANE = 128
ROW_TILE = 512
MOE_ROWS = 256
ROUTER_TILE = 256
SCATTER_TILE = 256
GLR_TILE = 128
VMEM_LIMIT = 56 * 1024 * 1024
NEG_INF = float("-inf")


def _mm_kernel(x_ref, w_ref, o_ref, *, precise):
    if precise:
        o_ref[...] = jnp.dot(x_ref[...].astype(F32), w_ref[...], preferred_element_type=F32,
                             precision=lax.Precision.HIGHEST)
    else:
        o_ref[...] = jnp.dot(x_ref[...].astype(BF16), w_ref[...], preferred_element_type=F32)


def _matmul(x, w, precise=False):
    m, k = x.shape
    n = w.shape[1]
    n_pad = -n % LANE
    w = w.astype(F32 if precise else BF16)
    if n_pad:
        w = jnp.pad(w, ((0, 0), (0, n_pad)))
    tm = ROW_TILE
    assert m % tm == 0
    out = pl.pallas_call(
        functools.partial(_mm_kernel, precise=precise),
        out_shape=jax.ShapeDtypeStruct((m, n + n_pad), F32),
        grid=(m // tm,),
        in_specs=[pl.BlockSpec((tm, k), lambda i: (i, 0)),
                  pl.BlockSpec((k, n + n_pad), lambda i: (0, 0))],
        out_specs=pl.BlockSpec((tm, n + n_pad), lambda i: (i, 0)),
        compiler_params=pltpu.CompilerParams(dimension_semantics=("arbitrary",),
                                             vmem_limit_bytes=VMEM_LIMIT),
        name="dense_matmul",
    )(x, w)
    return out[:, :n] if n_pad else out


def _rms(x, g):
    return x * lax.rsqrt(jnp.mean(x * x, axis=-1, keepdims=True) + EPS) * g


def _split(p, sizes):
    out = []
    o = 0
    for s in sizes:
        out.append(p[..., o:o + s])
        o += s
    return out


def _heads(a, n):
    b, t, w = a.shape
    return a.reshape(b, t, n, w // n).transpose(0, 2, 1, 3)


def _head_rms(o, g):
    o = o * lax.rsqrt(jnp.mean(o * o, axis=-1, keepdims=True) + EPS)
    b, n, t, d = o.shape
    return o.transpose(0, 2, 1, 3).reshape(b, t, n * d) * g


def _flip(a):
    return jnp.flip(a, axis=2)


def _short_conv(u, w, b):
    ch = u.shape[-1]
    y = lax.conv_general_dilated(u, w[:, None, :], (1,), ((SHORT_CONV // 2, SHORT_CONV // 2),),
                                 dimension_numbers=('NWC', 'WIO', 'NWC'), feature_group_count=ch)
    return y + b


def _pos_embed_2d(rows, d):
    r = jnp.repeat(jnp.arange(rows, dtype=F32), GRID_W)
    col = (jnp.arange(rows * GRID_W) % GRID_W).astype(F32)
    quarter = d // 4
    omega = 1.0 / (POS_BASE ** (jnp.arange(quarter, dtype=F32) / quarter))

    def axis_emb(p):
        ang = p[:, None] * omega[None, :]
        return jnp.concatenate([jnp.sin(ang), jnp.cos(ang)], axis=-1)

    return jnp.concatenate([axis_emb(r), axis_emb(col)], axis=-1)


def _hyena_spectra(L, w1, b1, w2, b2, w3, freq):
    t = jnp.linspace(0.0, 1.0, L, dtype=F32)[:, None]
    w = 2.0 * math.pi * jnp.arange(L, dtype=F32)[:, None] / L
    bands = jnp.linspace(1e-4, HY_BANDS - 1, HY_BANDS, dtype=F32)[None, :]
    feats = jnp.concatenate([t, jnp.cos(bands * w), -jnp.sin(bands * w)], axis=-1)
    z = jnp.sin(freq[0] * (feats @ w1 + b1))
    z = jnp.sin(freq[1] * (z @ w2 + b2))
    h = (z @ w3).reshape(L, HY_ORDER, 2, HY_W)
    max_decay = math.log(HY_DECAY_TARGET) / HY_FAST_DECAY
    min_decay = math.log(HY_DECAY_TARGET) / HY_SLOW_DECAY
    deltas = jnp.abs(jnp.linspace(min_decay, max_decay, HY_W, dtype=F32))
    h = h * jnp.exp(-t[:, :, None, None] * deltas)
    fwd = h[:, :, 0]
    bwd = h[1:, :, 1][::-1]
    l1 = jnp.sum(jnp.abs(fwd), axis=0) + jnp.sum(jnp.abs(bwd), axis=0)
    filt = jnp.concatenate([fwd, jnp.zeros((1, HY_ORDER, HY_W), F32), bwd], axis=0) / l1
    return jnp.fft.rfft(filt, axis=0)


def _fft_conv(u, spec, bias):
    L = u.shape[1]
    y = jnp.fft.irfft(jnp.fft.rfft(u, n=2 * L, axis=1) * spec, n=2 * L, axis=1)[:, :L]
    return y + u * bias


def _hyena(parts, conv_w, conv_b, w1, b1, w2, b2, w3, freq, bias, norm_g):
    u = _short_conv(jnp.concatenate(parts, axis=-1), conv_w, conv_b)
    v, x1, x2 = u[..., :HY_W], u[..., HY_W:2 * HY_W], u[..., 2 * HY_W:]
    spec = _hyena_spectra(u.shape[1], w1, b1, w2, b2, w3, freq)
    z = x1 * _fft_conv(v, spec[:, 0], bias[0])
    y = x2 * _fft_conv(z, spec[:, 1], bias[1])
    return _rms(y, norm_g)


FFT_N1 = 128
FFT_N2 = 128
FFT_KTILE = 8
FFT_NTILE = 4096


def _dft_tables(n1, n2):
    n = n1 * n2
    k = np.arange(n1)
    f1 = np.exp(-2j * np.pi * np.outer(k, k) / n1)
    f2 = np.exp(-2j * np.pi * np.outer(np.arange(n2), np.arange(n2)) / n2)
    tw = np.exp(-2j * np.pi * np.outer(np.arange(n1), np.arange(n2)) / n)
    as32 = lambda a: jnp.asarray(np.ascontiguousarray(a), F32)
    f1_fwd = as32(np.concatenate([f1.real, f1.imag], axis=0))
    f1_inv = as32(np.concatenate([f1.real, f1.imag], axis=1) / n)
    f2_inv = as32(np.block([[f2.real, f2.imag], [-f2.imag, f2.real]]))
    return f1_fwd, f1_inv, as32(f2.real), as32(f2.imag), f2_inv, as32(tw.real), as32(tw.imag)


def _stage_kernel(w_ref, x_ref, o_ref):
    o_ref[...] = jnp.dot(w_ref[...].astype(BF16), x_ref[...].astype(BF16),
                         preferred_element_type=F32).astype(o_ref.dtype)


def _stage_matmul(w, x, out_dtype):
    g, k, n = x.shape
    m = w.shape[0]
    tn = FFT_NTILE
    return pl.pallas_call(
        _stage_kernel,
        out_shape=jax.ShapeDtypeStruct((g, m, n), out_dtype),
        grid=(g, n // tn),
        in_specs=[pl.BlockSpec((m, k), lambda b, j: (0, 0)), pl.BlockSpec((None, k, tn), lambda b, j: (b, 0, j))],
        out_specs=pl.BlockSpec((None, m, tn), lambda b, j: (b, 0, j)),
        compiler_params=pltpu.CompilerParams(dimension_semantics=("arbitrary", "arbitrary"),
                                             vmem_limit_bytes=VMEM_LIMIT),
        name="dft_stage",
    )(w, x)


def _twiddled_f2(f2r, f2i, tr, ti):
    gr = f2r * tr - f2i * ti
    gi = f2r * ti + f2i * tr
    return jnp.concatenate([jnp.concatenate([gr, -gi], axis=1), jnp.concatenate([gi, gr], axis=1)], axis=0)


def _spectrum_kernel(a_ref, f2r_ref, f2i_ref, tr_ref, ti_ref, x_ref):
    n2 = f2r_ref.shape[0]
    tr = tr_ref[...]
    ti = ti_ref[...]
    for i in range(a_ref.shape[0]):
        g = _twiddled_f2(f2r_ref[...], f2i_ref[...], tr[i:i + 1], ti[i:i + 1])
        x_ref[i] = jnp.dot(g.astype(BF16), a_ref[i], preferred_element_type=F32)


def _conv_mid_kernel(a_ref, h_ref, f2r_ref, f2i_ref, f2inv_ref, tr_ref, ti_ref, z_ref):
    n2 = f2r_ref.shape[0]
    tr = tr_ref[...]
    ti = ti_ref[...]
    tr_col = tr.T
    ti_col = ti.T
    f2inv = f2inv_ref[...].astype(BF16)
    for i in range(a_ref.shape[0]):
        g = _twiddled_f2(f2r_ref[...], f2i_ref[...], tr[i:i + 1], ti[i:i + 1])
        x = jnp.dot(g.astype(BF16), a_ref[i], preferred_element_type=F32)
        xr, xi = x[:n2], x[n2:]
        hr, hi = h_ref[i, :n2], h_ref[i, n2:]
        y = jnp.concatenate([hr * xr - hi * xi, hr * xi + hi * xr], axis=0)
        w = jnp.dot(f2inv, y.astype(BF16), preferred_element_type=F32)
        wr, wi = w[:n2], w[n2:]
        cr, ci = tr_col[:, i:i + 1], ti_col[:, i:i + 1]
        z_ref[i] = jnp.concatenate([cr * wr + ci * wi, cr * wi - ci * wr], axis=0).astype(z_ref.dtype)


def _dft_mid_specs(g, c):
    n1, n2, kt = FFT_N1, FFT_N2, FFT_KTILE
    blk = pl.BlockSpec((None, kt, 2 * n2, c), lambda b, j: (b, j, 0, 0))
    const = lambda r, cc: pl.BlockSpec((r, cc), lambda b, j: (0, 0))
    twid = pl.BlockSpec((kt, n2), lambda b, j: (j, 0))
    params = pltpu.CompilerParams(dimension_semantics=("arbitrary", "arbitrary"), vmem_limit_bytes=VMEM_LIMIT)
    return blk, const, twid, params, (g, n1 // kt)


def _to_k1_major(a2d, c):
    g = a2d.shape[0]
    return a2d.reshape(g, 2, FFT_N1, FFT_N2, c).transpose(0, 2, 1, 3, 4).reshape(g, FFT_N1, 2 * FFT_N2, c)


def _filter_spectrum(filt, tables):
    g, n, c = filt.shape
    f1_fwd, _, f2r, f2i, _, twr, twi = tables
    a = _stage_matmul(f1_fwd, filt.reshape(g, FFT_N1, FFT_N2 * c), BF16)
    blk, const, twid, params, grid = _dft_mid_specs(g, c)
    return pl.pallas_call(
        _spectrum_kernel,
        out_shape=jax.ShapeDtypeStruct((g, FFT_N1, 2 * FFT_N2, c), F32),
        grid=grid,
        in_specs=[blk, const(FFT_N2, FFT_N2), const(FFT_N2, FFT_N2), twid, twid],
        out_specs=blk, compiler_params=params, name="dft_spectrum",
    )(_to_k1_major(a, c), f2r, f2i, twr, twi)


def _long_conv(u, spec, tables):
    g, l, c = u.shape
    f1_fwd, f1_inv, f2r, f2i, f2inv, twr, twi = tables
    half = l // FFT_N2
    a = _stage_matmul(f1_fwd[:, :half], u.reshape(g, half, FFT_N2 * c), BF16)
    blk, const, twid, params, grid = _dft_mid_specs(g, c)
    hspec = pl.BlockSpec((FFT_KTILE, 2 * FFT_N2, c), lambda b, j: (j, 0, 0))
    z = pl.pallas_call(
        _conv_mid_kernel,
        out_shape=jax.ShapeDtypeStruct((g, FFT_N1, 2 * FFT_N2, c), BF16),
        grid=grid,
        in_specs=[blk, hspec, const(FFT_N2, FFT_N2), const(FFT_N2, FFT_N2), const(2 * FFT_N2, 2 * FFT_N2),
                  twid, twid],
        out_specs=blk, compiler_params=params, name="dft_conv_mid",
    )(_to_k1_major(a, c), spec, f2r, f2i, f2inv, twr, twi)
    z2d = z.reshape(g, FFT_N1, 2, FFT_N2, c).transpose(0, 2, 1, 3, 4).reshape(g, 2 * FFT_N1, FFT_N2 * c)
    y = _stage_matmul(f1_inv[:half], z2d, F32)
    return y.reshape(g, l, c)


def _hyena_filters(L, w1, b1, w2, b2, w3, freq):
    t = jnp.linspace(0.0, 1.0, L, dtype=F32)[:, None]
    w = 2.0 * math.pi * jnp.arange(L, dtype=F32)[:, None] / L
    bands = jnp.linspace(1e-4, HY_BANDS - 1, HY_BANDS, dtype=F32)[None, :]
    feats = jnp.concatenate([t, jnp.cos(bands * w), -jnp.sin(bands * w)], axis=-1)
    z = jnp.sin(freq[0] * (feats @ w1 + b1))
    z = jnp.sin(freq[1] * (z @ w2 + b2))
    h = (z @ w3).reshape(L, HY_ORDER, 2, HY_W)
    max_decay = math.log(HY_DECAY_TARGET) / HY_FAST_DECAY
    min_decay = math.log(HY_DECAY_TARGET) / HY_SLOW_DECAY
    deltas = jnp.abs(jnp.linspace(min_decay, max_decay, HY_W, dtype=F32))
    h = h * jnp.exp(-t[:, :, None, None] * deltas)
    fwd = h[:, :, 0]
    bwd = h[1:, :, 1][::-1]
    l1 = jnp.sum(jnp.abs(fwd), axis=0) + jnp.sum(jnp.abs(bwd), axis=0)
    return jnp.concatenate([fwd, jnp.zeros((1, HY_ORDER, HY_W), F32), bwd], axis=0) / l1


def _hyena_long(parts, conv_w, conv_b, w1, b1, w2, b2, w3, freq, bias, norm_g):
    u = _short_conv(jnp.concatenate(parts, axis=-1), conv_w, conv_b)
    v, x1, x2 = u[..., :HY_W], u[..., HY_W:2 * HY_W], u[..., 2 * HY_W:]
    L = u.shape[1]
    assert 2 * L == FFT_N1 * FFT_N2
    tables = _dft_tables(FFT_N1, FFT_N2)
    filt = _hyena_filters(L, w1, b1, w2, b2, w3, freq)
    spec = _filter_spectrum(jnp.moveaxis(filt, 1, 0), tables)
    z = x1 * (_long_conv(v, spec[0], tables) + v * bias[0])
    y = x2 * (_long_conv(z, spec[1], tables) + z * bias[1])
    return _rms(y, norm_g)


def _glr_direction(q, k, v, g, st_ref, reverse, nh):
    tc, hk = k.shape
    hv = v.shape[1]
    c = CHUNK_GATED
    hi = lax.Precision.HIGHEST
    ti = lax.broadcasted_iota(jnp.int32, (tc, tc), 0)
    tj = lax.broadcasted_iota(jnp.int32, (tc, tc), 1)
    same = (ti // c) == (tj // c)
    seen = (tj >= ti) if reverse else (tj <= ti)
    bcum = jnp.dot(jnp.where(same, jnp.where(seen, 1.0, 0.0), 0.0), g, precision=hi, preferred_element_type=F32)
    btot = jnp.dot(jnp.where(same, 1.0, 0.0), g, precision=hi, preferred_element_type=F32)
    qd = q * jnp.exp(bcum)
    kd = k * jnp.exp(btot - bcum)
    dec = jnp.exp(btot)
    head_sum = jnp.where(lax.broadcasted_iota(jnp.int32, (hk, hv), 0) // (hk // nh)
                         == lax.broadcasted_iota(jnp.int32, (hk, hv), 1) // (hv // nh), 1.0, 0.0).astype(BF16)
    in_chunk = lax.broadcasted_iota(jnp.int32, (tc, hk), 0) % c
    o = jnp.zeros((tc, hv), F32)
    for lag in range(c):
        if lag == 0:
            ks, bs, vs = k, bcum, v
        else:
            shift = tc - lag if reverse else lag
            ks, bs, vs = pltpu.roll(k, shift, 0), pltpu.roll(bcum, shift, 0), pltpu.roll(v, shift, 0)
        valid = (in_chunk + lag <= c - 1) if reverse else (in_chunk >= lag)
        x = q * ks * jnp.exp(jnp.where(valid, bcum - bs, NEG_INF))
        o = o + jnp.dot(x.astype(BF16), head_sum, preferred_element_type=F32) * vs
    head_mask = (lax.broadcasted_iota(jnp.int32, (hv, hk), 0) // (hv // nh)
                 == lax.broadcasted_iota(jnp.int32, (hv, hk), 1) // (hk // nh))
    st = st_ref[...]
    nch = tc // c
    outs = [None] * nch
    for ci in (range(nch - 1, -1, -1) if reverse else range(nch)):
        sl = slice(ci * c, (ci + 1) * c)
        outs[ci] = lax.dot_general(qd[sl].astype(BF16), st.astype(BF16), (((1,), (1,)), ((), ())),
                                   preferred_element_type=F32)
        ds = lax.dot_general(v[sl].astype(BF16), kd[sl].astype(BF16), (((0,), (0,)), ((), ())),
                             preferred_element_type=F32)
        st = st * dec[ci * c:ci * c + 1] + jnp.where(head_mask, ds, 0.0)
    st_ref[...] = st
    return o + jnp.concatenate(outs, axis=0)


def _glr_kernel(qf_ref, kf_ref, vf_ref, gf_ref, qb_ref, kb_ref, vb_ref, gb_ref, s0f_ref, s0b_ref,
                of_ref, ob_ref, sf_ref, sb_ref, stf, stb, *, nh):
    j = pl.program_id(1)

    @pl.when(j == 0)
    def _():
        stf[...] = s0f_ref[...]
        stb[...] = s0b_ref[...]

    of_ref[...] = _glr_direction(qf_ref[...], kf_ref[...], vf_ref[...], gf_ref[...], stf, False, nh)
    ob_ref[...] = _glr_direction(qb_ref[...], kb_ref[...], vb_ref[...], gb_ref[...], stb, True, nh)

    @pl.when(j == pl.num_programs(1) - 1)
    def _():
        sf_ref[...] = stf[...]
        sb_ref[...] = stb[...]


def _glr_bidir(q, k_f, k_b, v, g_f, g_b, s0_f, s0_b, nh):
    bsz, t, hk = k_f.shape
    hv = v.shape[-1]
    tc = GLR_TILE
    nsb = t // tc
    fwd = lambda w: pl.BlockSpec((None, tc, w), lambda b, j: (b, j, 0))
    bwd = lambda w: pl.BlockSpec((None, tc, w), lambda b, j: (b, nsb - 1 - j, 0))
    st = pl.BlockSpec((None, hv, hk), lambda b, j: (b, 0, 0))
    return pl.pallas_call(
        functools.partial(_glr_kernel, nh=nh),
        out_shape=[jax.ShapeDtypeStruct((bsz, t, hv), F32), jax.ShapeDtypeStruct((bsz, t, hv), F32),
                   jax.ShapeDtypeStruct((bsz, hv, hk), F32), jax.ShapeDtypeStruct((bsz, hv, hk), F32)],
        grid=(bsz, nsb),
        in_specs=[fwd(hk), fwd(hk), fwd(hv), fwd(hk), bwd(hk), bwd(hk), bwd(hv), bwd(hk), st, st],
        out_specs=[fwd(hv), bwd(hv), st, st],
        scratch_shapes=[pltpu.VMEM((hv, hk), F32), pltpu.VMEM((hv, hk), F32)],
        compiler_params=pltpu.CompilerParams(dimension_semantics=("arbitrary", "arbitrary"),
                                             vmem_limit_bytes=VMEM_LIMIT),
        name="gated_linear_recurrence",
    )(q, k_f, v, g_f, q, k_b, v, g_b, s0_f, s0_b)


def _head_rms_tokens(o, g, nh):
    b, t, w = o.shape
    oh = o.reshape(b, t, nh, w // nh)
    oh = oh * lax.rsqrt(jnp.mean(oh * oh, axis=-1, keepdims=True) + EPS)
    return oh.reshape(b, t, w) * g


def _hgrn2_seq(parts, lb, norm_g, s0_f, s0_b, with_out):
    q, i, zf, zb, g = parts
    log_lb = jnp.log(lb)
    log_ub = jnp.log1p(-lb)

    def gate(z):
        return (1.0 - lb) * jax.nn.sigmoid(-z), jnp.logaddexp(log_lb, log_ub + jax.nn.log_sigmoid(z))

    k_f, lf_f = gate(zf)
    k_b, lf_b = gate(zb)
    o_f, o_b, s_f, s_b = _glr_bidir(jax.nn.silu(q), k_f, k_b, i, lf_f, lf_b, s0_f, s0_b, HG_H)
    y = _head_rms_tokens(o_f + o_b, norm_g, HG_H) * jax.nn.silu(g) if with_out else None
    return y, s_f, s_b


def _hgrn2(pc, pl_, lb, norm_g, with_ctx):
    s0 = jnp.zeros((pl_[0].shape[0], HG_W, HG_W), F32)
    yc, s_f, s_b = _hgrn2_seq(pc, lb, norm_g, s0, s0, with_ctx)
    y, _, _ = _hgrn2_seq(pl_, lb, norm_g, s_f, s_b, True)
    return yc, y


def _gla_seq(parts, a_up, a_b, norm_g, s0_f, s0_b, with_out):
    q, k, v, af, ab, r = parts
    g_f = jax.nn.log_sigmoid(af @ a_up[0] + a_b[0]) / GLA_NORMALIZER
    g_b = jax.nn.log_sigmoid(ab @ a_up[1] + a_b[1]) / GLA_NORMALIZER
    o_f, o_b, s_f, s_b = _glr_bidir(q * GLA_DK ** -0.5, k, k, v, g_f, g_b, s0_f, s0_b, GLA_H)
    y = _head_rms_tokens(o_f + o_b, norm_g, GLA_H) * jax.nn.silu(r) if with_out else None
    return y, s_f, s_b


def _gla(pc, pl_, a_up, a_b, norm_g, with_ctx):
    s0 = jnp.zeros((pl_[0].shape[0], GLA_VW, GLA_KW), F32)
    yc, s_f, s_b = _gla_seq(pc, a_up, a_b, norm_g, s0, s0, with_ctx)
    y, _, _ = _gla_seq(pl_, a_up, a_b, norm_g, s_f, s_b, True)
    return yc, y


def _mlstm_dir(q, k, v, ig, lf, state0, with_out):
    bsz, nh, T, dh = k.shape
    C = CHUNK_ML
    n = T // C
    kc = k.reshape(bsz, nh, n, C, dh)
    vc = v.reshape(bsz, nh, n, C, dh)
    igc = ig.reshape(bsz, nh, n, C)
    b = jnp.cumsum(lf.reshape(bsz, nh, n, C), axis=-1)
    bl = b[..., -1]
    a = bl[..., None] - b + igc
    ma = jnp.max(a, axis=-1)
    w = jnp.exp(a - ma[..., None])
    ds = jnp.einsum('bhnc,bhncd,bhncv->bhndv', w, kc, vc)
    dn = jnp.einsum('bhnc,bhncd->bhnd', w, kc)

    def step(carry, inp):
        s, nv, m = carry
        bl_, ma_, ds_, dn_ = inp
        m_new = jnp.maximum(bl_ + m, ma_)
        d_old = jnp.exp(bl_ + m - m_new)
        d_new = jnp.exp(ma_ - m_new)
        s_new = d_old[..., None, None] * s + d_new[..., None, None] * ds_
        n_new = d_old[..., None] * nv + d_new[..., None] * dn_
        return (s_new, n_new, m_new), (s, nv, m)

    mv = lambda t_: jnp.moveaxis(t_, 2, 0)
    fin, prev = lax.scan(step, state0, (mv(bl), mv(ma), mv(ds), mv(dn)))
    if not with_out:
        return None, fin
    s_prev, n_prev, m_prev = [jnp.moveaxis(p, 0, 2) for p in prev]
    qc = q.reshape(bsz, nh, n, C, dh)
    causal = jnp.tril(jnp.ones((C, C), dtype=bool))
    dmat = jnp.where(causal, b[..., :, None] - b[..., None, :] + igc[..., None, :], -jnp.inf)
    inter = b + m_prev[..., None]
    m_t = jnp.maximum(inter, jnp.max(dmat, axis=-1))
    wq = jnp.exp(dmat - m_t[..., None]) * jnp.einsum('bhntd,bhnsd->bhnts', qc, kc)
    w_int = jnp.exp(inter - m_t)
    num = (jnp.einsum('bhnts,bhnsv->bhntv', wq, vc)
           + w_int[..., None] * jnp.einsum('bhntd,bhndv->bhntv', qc, s_prev))
    den = jnp.sum(wq, axis=-1) + w_int * jnp.einsum('bhntd,bhnd->bhnt', qc, n_prev)
    h = num / jnp.maximum(jnp.abs(den), jnp.exp(-m_t))[..., None]
    return h.reshape(bsz, nh, T, dh), fin


def _mlstm_seq(parts, conv_w, conv_b, gate_b, norm_g, st_f, st_b, with_out):
    q, k, v, gates, o = parts
    qk = jax.nn.silu(_short_conv(jnp.concatenate([q, k], axis=-1), conv_w, conv_b))
    qh = _heads(qk[..., :ML_W], ML_H)
    kh = _heads(qk[..., ML_W:], ML_H) * ML_DH ** -0.5
    vh = _heads(v, ML_H)
    bsz, T, _ = gates.shape
    gt = (gates.reshape(bsz, T, 4, ML_H) + gate_b).transpose(2, 0, 3, 1)
    h_f, fin_f = _mlstm_dir(qh, kh, vh, gt[0], jax.nn.log_sigmoid(gt[1]), st_f, with_out)
    h_b, fin_b = _mlstm_dir(_flip(qh), _flip(kh), _flip(vh), _flip(gt[2]),
                            _flip(jax.nn.log_sigmoid(gt[3])), st_b, with_out)
    y = jax.nn.sigmoid(o) * _head_rms(h_f + _flip(h_b), norm_g) if with_out else None
    return y, fin_f, fin_b


def _mlstm(pc, pl_, conv_w, conv_b, gate_b, norm_g, with_ctx):
    bsz = pl_[0].shape[0]
    st0 = (jnp.zeros((bsz, ML_H, ML_DH, ML_DH), F32), jnp.zeros((bsz, ML_H, ML_DH), F32),
           jnp.zeros((bsz, ML_H), F32))
    yc, st_f, st_b = _mlstm_seq(pc, conv_w, conv_b, gate_b, norm_g, st0, st0, with_ctx)
    y, _, _ = _mlstm_seq(pl_, conv_w, conv_b, gate_b, norm_g, st_f, st_b, True)
    return yc, y


def _mlstm_direction(q, k, v, igx, lfx, s_ref, n_ref, m_ref, reverse, nh):
    tc, w = q.shape
    seg = w // nh
    assert tc == seg
    hi = lax.Precision.HIGHEST
    ti = lax.broadcasted_iota(jnp.int32, (tc, tc), 0)
    tj = lax.broadcasted_iota(jnp.int32, (tc, tc), 1)
    seen = (tj >= ti) if reverse else (tj <= ti)
    b = jnp.dot(jnp.where(seen, 1.0, 0.0), lfx, precision=hi, preferred_element_type=F32)
    bl = b[0:1] if reverse else b[tc - 1:tc]
    a = bl - b + igx
    ma = jnp.max(a, axis=0, keepdims=True)
    kw = jnp.exp(a - ma) * k
    s_prev = s_ref[...]
    n_prev = n_ref[...]
    m_prev = m_ref[...]
    lane = lax.broadcasted_iota(jnp.int32, (tc, w), 1)
    row = lax.broadcasted_iota(jnp.int32, (tc, w), 0)
    same_head = (lax.broadcasted_iota(jnp.int32, (w, w), 0) // seg
                 == lax.broadcasted_iota(jnp.int32, (w, w), 1) // seg)
    kexp = jnp.where(same_head, jnp.concatenate([k] * nh, axis=0), 0.0)
    vexp = jnp.where(same_head, jnp.concatenate([v] * nh, axis=0), 0.0)
    scores = lax.dot_general(q.astype(BF16), kexp.astype(BF16), (((1,), (1,)), ((), ())),
                             preferred_element_type=F32)
    s_lane = lane % seg
    by_src = jnp.sum(jnp.where(s_lane == row, igx - b, 0.0), axis=0, keepdims=True)
    ok = (s_lane >= row) if reverse else (s_lane <= row)
    dmat = jnp.where(ok, b + by_src, NEG_INF)
    inter = b + m_prev
    head_of_lane = lane // seg
    seg_max = jnp.full((tc, w), NEG_INF, F32)
    for h in range(nh):
        in_h = head_of_lane == h
        seg_max = jnp.where(in_h, jnp.max(jnp.where(in_h, dmat, NEG_INF), axis=1, keepdims=True), seg_max)
    m_t = jnp.maximum(inter, seg_max)
    wq = jnp.exp(dmat - m_t) * scores
    w_int = jnp.exp(inter - m_t)
    head_sum = jnp.where(same_head, 1.0, 0.0).astype(BF16)
    num = (jnp.dot(wq.astype(BF16), vexp.astype(BF16), preferred_element_type=F32)
           + w_int * jnp.dot(q.astype(BF16), s_prev.astype(BF16), preferred_element_type=F32))
    den = (jnp.dot(wq.astype(BF16), head_sum, preferred_element_type=F32)
           + w_int * jnp.dot((q * n_prev).astype(BF16), head_sum, preferred_element_type=F32))
    h_out = num / jnp.maximum(jnp.abs(den), jnp.exp(-m_t))
    m_new = jnp.maximum(bl + m_prev, ma)
    d_old = jnp.exp(bl + m_prev - m_new)
    d_new = jnp.exp(ma - m_new)
    ds = lax.dot_general(kw.astype(BF16), v.astype(BF16), (((0,), (0,)), ((), ())), preferred_element_type=F32)
    s_ref[...] = d_old * s_prev + d_new * jnp.where(same_head, ds, 0.0)
    n_ref[...] = d_old * n_prev + d_new * jnp.sum(kw, axis=0, keepdims=True)
    m_ref[...] = m_new
    return h_out


def _mlstm_kernel(qf_ref, kf_ref, vf_ref, igf_ref, lff_ref, qb_ref, kb_ref, vb_ref, igb_ref, lfb_ref,
                  s0f_ref, n0f_ref, m0f_ref, s0b_ref, n0b_ref, m0b_ref,
                  hf_ref, hb_ref, sf_ref, nf_ref, mf_ref, sb_ref, nb_ref, mb_ref,
                  s_f, n_f, m_f, s_b, n_b, m_b, *, nh):
    j = pl.program_id(1)

    @pl.when(j == 0)
    def _():
        s_f[...] = s0f_ref[...]
        n_f[...] = n0f_ref[...]
        m_f[...] = m0f_ref[...]
        s_b[...] = s0b_ref[...]
        n_b[...] = n0b_ref[...]
        m_b[...] = m0b_ref[...]

    hf_ref[...] = _mlstm_direction(qf_ref[...], kf_ref[...], vf_ref[...], igf_ref[...], lff_ref[...],
                                   s_f, n_f, m_f, False, nh)
    hb_ref[...] = _mlstm_direction(qb_ref[...], kb_ref[...], vb_ref[...], igb_ref[...], lfb_ref[...],
                                   s_b, n_b, m_b, True, nh)

    @pl.when(j == pl.num_programs(1) - 1)
    def _():
        sf_ref[...] = s_f[...]
        nf_ref[...] = n_f[...]
        mf_ref[...] = m_f[...]
        sb_ref[...] = s_b[...]
        nb_ref[...] = n_b[...]
        mb_ref[...] = m_b[...]


def _mlstm_bidir(q, k, v, ig_f, lf_f, ig_b, lf_b, st_f, st_b, nh):
    bsz, t, w = q.shape
    tc = CHUNK_ML
    nsb = t // tc
    fwd = pl.BlockSpec((None, tc, w), lambda b, j: (b, j, 0))
    bwd = pl.BlockSpec((None, tc, w), lambda b, j: (b, nsb - 1 - j, 0))
    mat = pl.BlockSpec((None, w, w), lambda b, j: (b, 0, 0))
    vec = pl.BlockSpec((None, 1, w), lambda b, j: (b, 0, 0))
    sds = jax.ShapeDtypeStruct
    state_shapes = [sds((bsz, w, w), F32), sds((bsz, 1, w), F32), sds((bsz, 1, w), F32)]
    outs = pl.pallas_call(
        functools.partial(_mlstm_kernel, nh=nh),
        out_shape=[sds((bsz, t, w), F32), sds((bsz, t, w), F32)] + state_shapes + state_shapes,
        grid=(bsz, nsb),
        in_specs=[fwd] * 5 + [bwd] * 5 + [mat, vec, vec] * 2,
        out_specs=[fwd, bwd] + [mat, vec, vec] * 2,
        scratch_shapes=[pltpu.VMEM((w, w), F32), pltpu.VMEM((1, w), F32), pltpu.VMEM((1, w), F32)] * 2,
        compiler_params=pltpu.CompilerParams(dimension_semantics=("arbitrary", "arbitrary"),
                                             vmem_limit_bytes=VMEM_LIMIT),
        name="mlstm_recurrence",
    )(q, k, v, ig_f, lf_f, q, k, v, ig_b, lf_b, *st_f, *st_b)
    return outs[0], outs[1], tuple(outs[2:5]), tuple(outs[5:8])


def _mlstm_seq_p(parts, conv_w, conv_b, gate_b, norm_g, st_f, st_b, with_out):
    q, k, v, gates, o = parts
    qk = jax.nn.silu(_short_conv(jnp.concatenate([q, k], axis=-1), conv_w, conv_b))
    bsz, t, _ = gates.shape
    gt = gates.reshape(bsz, t, 4, ML_H) + gate_b
    expand = lambda a: jnp.repeat(a, ML_DH, axis=-1)
    h_f, h_b, fin_f, fin_b = _mlstm_bidir(
        qk[..., :ML_W], qk[..., ML_W:] * ML_DH ** -0.5, v,
        expand(gt[:, :, 0]), expand(jax.nn.log_sigmoid(gt[:, :, 1])),
        expand(gt[:, :, 2]), expand(jax.nn.log_sigmoid(gt[:, :, 3])), st_f, st_b, ML_H)
    y = jax.nn.sigmoid(o) * _head_rms_tokens(h_f + h_b, norm_g, ML_H) if with_out else None
    return y, fin_f, fin_b


def _mlstm_p(pc, pl_, conv_w, conv_b, gate_b, norm_g, with_ctx):
    bsz = pl_[0].shape[0]
    st0 = (jnp.zeros((bsz, ML_W, ML_W), F32), jnp.zeros((bsz, 1, ML_W), F32), jnp.zeros((bsz, 1, ML_W), F32))
    yc, st_f, st_b = _mlstm_seq_p(pc, conv_w, conv_b, gate_b, norm_g, st0, st0, with_ctx)
    y, _, _ = _mlstm_seq_p(pl_, conv_w, conv_b, gate_b, norm_g, st_f, st_b, True)
    return yc, y


def _router_kernel(h_ref, wt_ref, b_ref, eidx_ref, wsel_ref, cnt_ref):
    i = pl.program_id(0)
    tm = h_ref.shape[0]
    ne = wt_ref.shape[0]
    per_group = ne // N_EXPERT_GROUPS
    logits = lax.dot_general(wt_ref[...], h_ref[...], (((1,), (1,)), ((), ())),
                             preferred_element_type=F32, precision=lax.Precision.HIGHEST)
    s = jax.nn.sigmoid(logits)
    sel = s + b_ref[...]
    row = lax.broadcasted_iota(jnp.int32, (ne, tm), 0)
    gs = []
    for g in range(N_EXPERT_GROUPS):
        blk = sel[g * per_group:(g + 1) * per_group]
        r = lax.broadcasted_iota(jnp.int32, blk.shape, 0)
        m1 = jnp.max(blk, axis=0, keepdims=True)
        i1 = jnp.min(jnp.where(blk == m1, r, per_group), axis=0, keepdims=True)
        m2 = jnp.max(jnp.where(r == i1, NEG_INF, blk), axis=0, keepdims=True)
        gs.append(m1 + m2)
    grp = jnp.concatenate(gs, axis=0)
    grow = lax.broadcasted_iota(jnp.int32, grp.shape, 0)
    gsel = jnp.zeros(grp.shape, F32)
    for _ in range(TOPK_GROUPS):
        m = jnp.max(grp, axis=0, keepdims=True)
        gi = jnp.min(jnp.where(grp == m, grow, N_EXPERT_GROUPS), axis=0, keepdims=True)
        hit = grow == gi
        gsel = jnp.where(hit, 1.0, gsel)
        grp = jnp.where(hit, NEG_INF, grp)
    masked = jnp.concatenate(
        [jnp.where(gsel[g:g + 1] > 0.0, sel[g * per_group:(g + 1) * per_group], NEG_INF)
         for g in range(N_EXPERT_GROUPS)], axis=0)
    eis, ws = [], []
    picked = jnp.zeros((ne, tm), F32)
    for _ in range(TOP_K):
        m = jnp.max(masked, axis=0, keepdims=True)
        ei = jnp.min(jnp.where(masked == m, row, ne), axis=0, keepdims=True)
        hit = row == ei
        ws.append(jnp.sum(jnp.where(hit, s, 0.0), axis=0, keepdims=True))
        eis.append(ei)
        picked = jnp.where(hit, 1.0, picked)
        masked = jnp.where(hit, NEG_INF, masked)
    w = jnp.concatenate(ws, axis=0)
    eidx_ref[...] = jnp.concatenate(eis, axis=0)
    wsel_ref[...] = w / jnp.sum(w, axis=0, keepdims=True) * ROUTED_SCALE
    tot = jnp.dot(picked.astype(BF16), jnp.ones((tm, LANE), BF16), preferred_element_type=F32)

    @pl.when(i == 0)
    def _():
        cnt_ref[...] = jnp.zeros_like(cnt_ref)

    cnt_ref[...] += tot


def _pos_kernel(eidx_ref, base_ref, pos_ref, carry_ref):
    i = pl.program_id(0)
    tm = eidx_ref.shape[1]
    ne = base_ref.shape[0]

    @pl.when(i == 0)
    def _():
        carry_ref[...] = jnp.zeros_like(carry_ref)

    eidx = eidx_ref[...]
    row = lax.broadcasted_iota(jnp.int32, (ne, tm), 0)
    picked = jnp.zeros((ne, tm), F32)
    for k in range(TOP_K):
        picked = jnp.where(row == eidx[k:k + 1], 1.0, picked)
    pb = picked.astype(BF16)
    before = jnp.where(lax.broadcasted_iota(jnp.int32, (tm, tm), 0) < lax.broadcasted_iota(jnp.int32, (tm, tm), 1),
                       1.0, 0.0).astype(BF16)
    rank = jnp.dot(pb, before, preferred_element_type=F32)
    tot = jnp.dot(pb, jnp.ones((tm, LANE), BF16), preferred_element_type=F32)
    dest = rank + (base_ref[...] + carry_ref[:, 0:1])
    pos = [jnp.sum(jnp.where(row == eidx[k:k + 1], dest, 0.0), axis=0, keepdims=True) for k in range(TOP_K)]
    pos_ref[...] = jnp.concatenate(pos, axis=0).astype(jnp.int32)
    carry_ref[...] += tot


def _dispatch_kernel(pos_ref, h_ref, xs_init_ref, xs_ref, sem):
    del xs_init_ref
    tm = h_ref.shape[0]

    def row_copy(n, k):
        return pltpu.make_async_copy(h_ref.at[pl.ds(n, 1)], xs_ref.at[pl.ds(pos_ref[k, n], 1)], sem)

    def issue(n, carry):
        for k in range(TOP_K):
            row_copy(n, k).start()
        return carry

    def drain(n, carry):
        for k in range(TOP_K):
            row_copy(n, k).wait()
        return carry

    lax.fori_loop(0, tm, issue, 0)
    lax.fori_loop(0, tm, drain, 0)


def _moe_ffn_kernel(blk_e_ref, nused_ref, x_ref, wgu_ref, wdn_ref, o_ref, wgu_s, wdn_s):
    i = pl.program_id(0)
    e = blk_e_ref[i]
    e_prev = blk_e_ref[jnp.maximum(i - 1, 0)]

    @pl.when(i < nused_ref[0])
    def _():
        @pl.when((i == 0) | (e != e_prev))
        def _():
            wgu_s[...] = wgu_ref[...].astype(BF16)
            wdn_s[...] = wdn_ref[...].astype(BF16)

        au = jnp.dot(x_ref[...].astype(BF16), wgu_s[...], preferred_element_type=F32)
        a = au[:, :EXPERT_FF]
        u = au[:, EXPERT_FF:]
        h = (a * jax.nn.sigmoid(a)) * u
        o_ref[...] = jnp.dot(h.astype(BF16), wdn_s[...], preferred_element_type=F32)

    @pl.when(i >= nused_ref[0])
    def _():
        o_ref[...] = jnp.zeros_like(o_ref)


def _combine_kernel(pos_ref, w_ref, y_hbm, o_ref, buf, sem):
    tm = o_ref.shape[0]

    def row_copy(n, k):
        return pltpu.make_async_copy(y_hbm.at[pl.ds(pos_ref[k, n], 1)], buf.at[k, pl.ds(n, 1)], sem)

    def issue(n, carry):
        for k in range(TOP_K):
            row_copy(n, k).start()
        return carry

    def drain(n, carry):
        for k in range(TOP_K):
            row_copy(n, k).wait()
        return carry

    lax.fori_loop(0, tm, issue, 0)
    lax.fori_loop(0, tm, drain, 0)
    acc = buf[0] * w_ref[:, 0:1]
    for k in range(1, TOP_K):
        acc = acc + buf[k] * w_ref[:, k:k + 1]
    o_ref[...] = acc


def _moe_routed(t, router_w, router_b, w_gu, w_down):
    n, d = t.shape
    ne = router_w.shape[1]
    ff2 = w_gu.shape[-1]
    params = pltpu.CompilerParams(dimension_semantics=("arbitrary",), vmem_limit_bytes=VMEM_LIMIT)
    tm = ROUTER_TILE
    eidx, wsel, cnt = pl.pallas_call(
        _router_kernel,
        out_shape=[jax.ShapeDtypeStruct((TOP_K, n), jnp.int32), jax.ShapeDtypeStruct((TOP_K, n), F32),
                   jax.ShapeDtypeStruct((ne, LANE), F32)],
        grid=(n // tm,),
        in_specs=[pl.BlockSpec((tm, d), lambda i: (i, 0)), pl.BlockSpec((ne, d), lambda i: (0, 0)),
                  pl.BlockSpec((ne, 1), lambda i: (0, 0))],
        out_specs=[pl.BlockSpec((TOP_K, tm), lambda i: (0, i)), pl.BlockSpec((TOP_K, tm), lambda i: (0, i)),
                   pl.BlockSpec((ne, LANE), lambda i: (0, 0))],
        compiler_params=params, name="moe_router",
    )(t, router_w.T, router_b.reshape(ne, 1))
    bm = MOE_ROWS
    counts = cnt[:, 0].astype(jnp.int32)
    padded = (counts + bm - 1) // bm * bm
    pad_end = jnp.cumsum(padded)
    pad_start = pad_end - padded
    n_blocks = (n * TOP_K + ne * (bm - 1)) // bm + 1
    blk_first = jnp.arange(n_blocks, dtype=jnp.int32) * bm
    blk_e = jnp.minimum(jnp.sum((pad_end[None, :] <= blk_first[:, None]).astype(jnp.int32), axis=1), ne - 1)
    n_used = (pad_end[-1] // bm).astype(jnp.int32).reshape(1)
    pos = pl.pallas_call(
        _pos_kernel,
        out_shape=jax.ShapeDtypeStruct((TOP_K, n), jnp.int32),
        grid=(n // tm,),
        in_specs=[pl.BlockSpec((TOP_K, tm), lambda i: (0, i)), pl.BlockSpec((ne, 1), lambda i: (0, 0))],
        out_specs=pl.BlockSpec((TOP_K, tm), lambda i: (0, i)),
        scratch_shapes=[pltpu.VMEM((ne, LANE), F32)],
        compiler_params=params, name="moe_positions",
    )(eidx, pad_start.astype(F32).reshape(ne, 1))
    ts = SCATTER_TILE
    p = n_blocks * bm
    pos_spec = pl.BlockSpec((TOP_K, ts), lambda i: (0, i), memory_space=pltpu.SMEM)
    xs = pl.pallas_call(
        _dispatch_kernel,
        out_shape=jax.ShapeDtypeStruct((p, d), F32),
        grid=(n // ts,),
        in_specs=[pos_spec, pl.BlockSpec((ts, d), lambda i: (i, 0)), pl.BlockSpec(memory_space=pl.ANY)],
        out_specs=pl.BlockSpec(memory_space=pl.ANY),
        scratch_shapes=[pltpu.SemaphoreType.DMA],
        input_output_aliases={2: 0},
        compiler_params=params, name="moe_dispatch",
    )(pos, t, jnp.zeros((p, d), F32))

    def x_map(i, blk_e, nused):
        return (jnp.minimum(i, nused[0] - 1), 0)

    def w_map(i, blk_e, nused):
        return (blk_e[i], 0, 0)

    y_p = pl.pallas_call(
        _moe_ffn_kernel,
        out_shape=jax.ShapeDtypeStruct((p, d), F32),
        grid_spec=pltpu.PrefetchScalarGridSpec(
            num_scalar_prefetch=2,
            grid=(n_blocks,),
            in_specs=[pl.BlockSpec((bm, d), x_map),
                      pl.BlockSpec((None, d, ff2), w_map),
                      pl.BlockSpec((None, ff2 // 2, d), w_map)],
            out_specs=pl.BlockSpec((bm, d), lambda i, blk_e, nused: (i, 0)),
            scratch_shapes=[pltpu.VMEM((d, ff2), BF16), pltpu.VMEM((ff2 // 2, d), BF16)],
        ),
        compiler_params=params, name="moe_expert_ffn",
    )(blk_e, n_used, xs, w_gu, w_down)
    return pl.pallas_call(
        _combine_kernel,
        out_shape=jax.ShapeDtypeStruct((n, d), F32),
        grid=(n // ts,),
        in_specs=[pos_spec, pl.BlockSpec((ts, TOP_K), lambda i: (i, 0)), pl.BlockSpec(memory_space=pl.ANY)],
        out_specs=pl.BlockSpec((ts, d), lambda i: (i, 0)),
        scratch_shapes=[pltpu.VMEM((TOP_K, ts, d), F32), pltpu.SemaphoreType.DMA],
        compiler_params=params, name="moe_combine",
    )(pos, wsel.T, y_p)


def _moe(t, router_w, router_b, w_gu, w_down, sh_gu, sh_down):
    routed = _moe_routed(t, router_w, router_b, w_gu, w_down)
    au = _matmul(t, sh_gu)
    ff = sh_gu.shape[-1] // 2
    shared = _matmul(jax.nn.silu(au[:, :ff]) * au[:, ff:], sh_down)
    return routed + shared


def kernel(x, c, ctx, c_ctx, ada_w, ada_b, norm_g, w_in, w_out, hy_conv_w, hy_conv_b, hy_ffn_w1, hy_ffn_b1, hy_ffn_w2, hy_ffn_b2, hy_ffn_w3, hy_freq, hy_bias, hy_norm, hg_lb_logits, hg_norm, gla_a_up, gla_a_b, gla_norm, ml_conv_w, ml_conv_b, ml_gate_b, ml_norm, router_w, router_b, exp_w_gu, exp_w_down, sh_w_gu, sh_w_down):
    bsz, seq, d = x.shape
    n_ctx = ctx.shape[1]
    depth = ada_w.shape[0]
    rows = seq // GRID_W
    x = x + _pos_embed_2d(rows, d)[None]
    xc = ctx
    lb_cum = jnp.cumsum(jax.nn.softmax(hg_lb_logits, axis=0), axis=0)
    lower_bounds = lb_cum - lb_cum[0:1]
    for l in range(depth):
        with_ctx = l < depth - 1
        mod = (jax.nn.silu(c) @ ada_w[l] + ada_b[l])[:, None, :]
        mod_c = jax.nn.silu(c_ctx) @ ada_w[l] + ada_b[l]
        sh1, sc1, g1, sh2, sc2, g2 = jnp.split(mod, 6, axis=-1)
        csh1, csc1, cg1, csh2, csc2, cg2 = jnp.split(mod_c, 6, axis=-1)
        h = _rms(x, norm_g[l, 0]) * (1.0 + sc1) + sh1
        hc = _rms(xc, norm_g[l, 0]) * (1.0 + csc1) + csh1
        h_all = jnp.concatenate([h.reshape(bsz * seq, d), hc.reshape(bsz * n_ctx, d)], axis=0)
        p_all = _matmul(h_all, w_in[l])
        pl_ = _split(p_all[:bsz * seq].reshape(bsz, seq, -1), IN_SPLITS)
        pc = _split(p_all[bsz * seq:].reshape(bsz, n_ctx, -1), IN_SPLITS)
        hy_args = (hy_conv_w[l], hy_conv_b[l], hy_ffn_w1[l], hy_ffn_b1[l], hy_ffn_w2[l], hy_ffn_b2[l],
                   hy_ffn_w3[l], hy_freq[l], hy_bias[l], hy_norm[l])
        y_hy = _hyena_long(pl_[0:3], *hy_args)
        yc_hg, y_hg = _hgrn2(pc[3:8], pl_[3:8], lower_bounds[l], hg_norm[l], with_ctx)
        yc_gla, y_gla = _gla(pc[8:14], pl_[8:14], gla_a_up[l], gla_a_b[l], gla_norm[l], with_ctx)
        yc_ml, y_ml = _mlstm_p(pc[14:19], pl_[14:19], ml_conv_w[l], ml_conv_b[l], ml_gate_b[l], ml_norm[l], with_ctx)
        y_cat = jnp.concatenate([y_hy, y_hg, y_gla, y_ml], axis=-1).reshape(bsz * seq, d)
        if with_ctx:
            yc_cat = jnp.concatenate([_hyena(pc[0:3], *hy_args), yc_hg, yc_gla, yc_ml],
                                     axis=-1).reshape(bsz * n_ctx, d)
            y_all = _matmul(jnp.concatenate([y_cat, yc_cat], axis=0), w_out[l])
            y = y_all[:bsz * seq].reshape(bsz, seq, d)
            yc = y_all[bsz * seq:].reshape(bsz, n_ctx, d)
        else:
            y = _matmul(y_cat, w_out[l]).reshape(bsz, seq, d)
        x = x + g1 * _rms(y, norm_g[l, 1])
        h = _rms(x, norm_g[l, 2]) * (1.0 + sc2) + sh2
        moe_args = (router_w[l], router_b[l], exp_w_gu[l], exp_w_down[l], sh_w_gu[l], sh_w_down[l])
        if with_ctx:
            xc = xc + cg1 * _rms(yc, norm_g[l, 1])
            hc = _rms(xc, norm_g[l, 2]) * (1.0 + csc2) + csh2
            f = _moe(jnp.concatenate([h.reshape(bsz * seq, d), hc.reshape(bsz * n_ctx, d)], axis=0), *moe_args)
            xc = xc + cg2 * _rms(f[bsz * seq:].reshape(bsz, n_ctx, d), norm_g[l, 3])
            f = f[:bsz * seq]
        else:
            f = _moe(h.reshape(bsz * seq, d), *moe_args)
        x = x + g2 * _rms(f.reshape(bsz, seq, d), norm_g[l, 3])
    return x
```

```python
import functools
import math

import jax
import jax.numpy as jnp
import numpy as np
from jax import lax
from jax.experimental import pallas as pl
from jax.experimental.pallas import tpu as pltpu

F32 = jnp.float32
BF16 = jnp.bfloat16

D_MODEL = 1024
GRID_W = 64
EPS = 1e-6
POS_BASE = 10000.0
GROUP_W = D_MODEL // 4
SHORT_CONV = 3
HY_W = GROUP_W
HY_ORDER = 2
HY_EMB = 33
HY_BANDS = (HY_EMB - 1) // 2
HY_FAST_DECAY = 0.3
HY_SLOW_DECAY = 1.5
HY_DECAY_TARGET = 1e-2
HG_H = 4
HG_W = GROUP_W
HG_DK = HG_W // HG_H
GLA_H = 4
GLA_KW = GROUP_W // 2
GLA_VW = GROUP_W
GLA_DK = GLA_KW // GLA_H
GLA_DV = GLA_VW // GLA_H
GLA_RANK = 16
GLA_NORMALIZER = 16.0
ML_H = 4
ML_W = GROUP_W
ML_DH = ML_W // ML_H
CHUNK_GATED = 16
CHUNK_ML = 64
N_EXPERTS = 256
TOP_K = 8
N_EXPERT_GROUPS = 8
TOPK_GROUPS = 4
EXPERT_FF = 256
ROUTED_SCALE = 2.5
IN_SPLITS = (HY_W, HY_W, HY_W,
             HG_W, HG_W, HG_W, HG_W, HG_W,
             GLA_KW, GLA_KW, GLA_VW, GLA_RANK, GLA_RANK, GLA_VW,
             ML_W, ML_W, ML_W, 4 * ML_H, ML_W)
P_ORDER = (0, 1, 2, 3, 4, 5, 6, 7, 8, 9, 10, 13, 14, 15, 16, 18, 11, 12, 17)
COL_HY = 0
COL_HG_Q, COL_HG_I, COL_HG_ZF, COL_HG_ZB, COL_HG_G = 3, 4, 5, 6, 7
COL_GLA_QK, COL_GLA_V, COL_GLA_R = 8, 9, 10
COL_ML_Q, COL_ML_K, COL_ML_V, COL_ML_O = 11, 12, 13, 14
N_WIDE = 15

LANE = 128
ROW_TILE = 512
MOE_ROWS = 256
ROUTER_TILE = 256
SCATTER_TILE = 256
GLR_TILE = 128
VMEM_LIMIT = 56 * 1024 * 1024
NEG_INF = float("-inf")
COL_NARROW = N_WIDE * GROUP_W // LANE
NARROW_ML_GATES = N_WIDE * GROUP_W + 2 * GLA_RANK
P_WIDTH = N_WIDE * GROUP_W + LANE


def _mm_kernel(x_ref, w_ref, o_ref, *, precise):
    if precise:
        o_ref[...] = jnp.dot(x_ref[...].astype(F32), w_ref[...], preferred_element_type=F32,
                             precision=lax.Precision.HIGHEST)
    else:
        o_ref[...] = jnp.dot(x_ref[...].astype(BF16), w_ref[...], preferred_element_type=F32)


def _matmul(x, w, precise=False):
    m, k = x.shape
    n = w.shape[1]
    n_pad = -n % LANE
    w = w.astype(F32 if precise else BF16)
    if n_pad:
        w = jnp.pad(w, ((0, 0), (0, n_pad)))
    tm = ROW_TILE
    assert m % tm == 0
    out = pl.pallas_call(
        functools.partial(_mm_kernel, precise=precise),
        out_shape=jax.ShapeDtypeStruct((m, n + n_pad), F32),
        grid=(m // tm,),
        in_specs=[pl.BlockSpec((tm, k), lambda i: (i, 0)),
                  pl.BlockSpec((k, n + n_pad), lambda i: (0, 0))],
        out_specs=pl.BlockSpec((tm, n + n_pad), lambda i: (i, 0)),
        compiler_params=pltpu.CompilerParams(dimension_semantics=("arbitrary",),
                                             vmem_limit_bytes=VMEM_LIMIT),
        name="dense_matmul",
    )(x, w)
    return out[:, :n] if n_pad else out


def _arrange_w_in(w):
    offs = np.concatenate([[0], np.cumsum(IN_SPLITS)])
    cols = [w[:, offs[i]:offs[i + 1]] for i in P_ORDER]
    used = sum(IN_SPLITS)
    return jnp.concatenate(cols + [jnp.zeros((w.shape[0], P_WIDTH - used), w.dtype)], axis=1).astype(BF16)


def _in_proj_kernel(x_ref, g_ref, sc_ref, sh_ref, w_ref, o_ref):
    x = x_ref[...]
    y = x * lax.rsqrt(jnp.mean(x * x, axis=-1, keepdims=True) + EPS) * g_ref[...]
    h = y * (1.0 + sc_ref[...]) + sh_ref[...]
    o_ref[...] = jnp.dot(h.astype(BF16), w_ref[...], preferred_element_type=F32)


def _in_proj(x, gain, scale, shift, w):
    g, r, d = x.shape
    n = w.shape[1]
    tm = min(ROW_TILE, r)
    assert r % tm == 0
    return pl.pallas_call(
        _in_proj_kernel,
        out_shape=jax.ShapeDtypeStruct((g, r, n), F32),
        grid=(g, r // tm),
        in_specs=[pl.BlockSpec((None, tm, d), lambda b, i: (b, i, 0)),
                  pl.BlockSpec((1, d), lambda b, i: (0, 0)),
                  pl.BlockSpec((None, 1, d), lambda b, i: (b, 0, 0)),
                  pl.BlockSpec((None, 1, d), lambda b, i: (b, 0, 0)),
                  pl.BlockSpec((d, n), lambda b, i: (0, 0))],
        out_specs=pl.BlockSpec((None, tm, n), lambda b, i: (b, i, 0)),
        compiler_params=pltpu.CompilerParams(dimension_semantics=("arbitrary", "arbitrary"),
                                             vmem_limit_bytes=VMEM_LIMIT),
        name="input_projection",
    )(x, gain.reshape(1, d), scale, shift, w)


def _rms(x, g):
    return x * lax.rsqrt(jnp.mean(x * x, axis=-1, keepdims=True) + EPS) * g


def _head_rms_tokens(o, g, nh):
    b, t, w = o.shape
    oh = o.reshape(b, t, nh, w // nh)
    oh = oh * lax.rsqrt(jnp.mean(oh * oh, axis=-1, keepdims=True) + EPS)
    return oh.reshape(b, t, w) * g


def _short_conv(u, w, b):
    ch = u.shape[-1]
    y = lax.conv_general_dilated(u, w[:, None, :], (1,), ((SHORT_CONV // 2, SHORT_CONV // 2),),
                                 dimension_numbers=('NWC', 'WIO', 'NWC'), feature_group_count=ch)
    return y + b


def _pos_embed_2d(rows, d):
    r = jnp.repeat(jnp.arange(rows, dtype=F32), GRID_W)
    col = (jnp.arange(rows * GRID_W) % GRID_W).astype(F32)
    quarter = d // 4
    omega = 1.0 / (POS_BASE ** (jnp.arange(quarter, dtype=F32) / quarter))

    def axis_emb(p):
        ang = p[:, None] * omega[None, :]
        return jnp.concatenate([jnp.sin(ang), jnp.cos(ang)], axis=-1)

    return jnp.concatenate([axis_emb(r), axis_emb(col)], axis=-1)


def _hyena_spectra(L, w1, b1, w2, b2, w3, freq):
    t = jnp.linspace(0.0, 1.0, L, dtype=F32)[:, None]
    w = 2.0 * math.pi * jnp.arange(L, dtype=F32)[:, None] / L
    bands = jnp.linspace(1e-4, HY_BANDS - 1, HY_BANDS, dtype=F32)[None, :]
    feats = jnp.concatenate([t, jnp.cos(bands * w), -jnp.sin(bands * w)], axis=-1)
    z = jnp.sin(freq[0] * (feats @ w1 + b1))
    z = jnp.sin(freq[1] * (z @ w2 + b2))
    h = (z @ w3).reshape(L, HY_ORDER, 2, HY_W)
    max_decay = math.log(HY_DECAY_TARGET) / HY_FAST_DECAY
    min_decay = math.log(HY_DECAY_TARGET) / HY_SLOW_DECAY
    deltas = jnp.abs(jnp.linspace(min_decay, max_decay, HY_W, dtype=F32))
    h = h * jnp.exp(-t[:, :, None, None] * deltas)
    fwd = h[:, :, 0]
    bwd = h[1:, :, 1][::-1]
    l1 = jnp.sum(jnp.abs(fwd), axis=0) + jnp.sum(jnp.abs(bwd), axis=0)
    filt = jnp.concatenate([fwd, jnp.zeros((1, HY_ORDER, HY_W), F32), bwd], axis=0) / l1
    return jnp.fft.rfft(filt, axis=0)


def _fft_conv(u, spec, bias):
    L = u.shape[1]
    y = jnp.fft.irfft(jnp.fft.rfft(u, n=2 * L, axis=1) * spec, n=2 * L, axis=1)[:, :L]
    return y + u * bias


def _hyena(u_raw, conv_w, conv_b, w1, b1, w2, b2, w3, freq, bias, norm_g):
    u = _short_conv(u_raw, conv_w, conv_b)
    v, x1, x2 = u[..., :HY_W], u[..., HY_W:2 * HY_W], u[..., 2 * HY_W:]
    spec = _hyena_spectra(u.shape[1], w1, b1, w2, b2, w3, freq)
    z = x1 * _fft_conv(v, spec[:, 0], bias[0])
    y = x2 * _fft_conv(z, spec[:, 1], bias[1])
    return _rms(y, norm_g)


FFT_N1 = 128
FFT_N2 = 128
FFT_KTILE = 8
FFT_NTILE = 4096


def _dft_tables(n1, n2):
    n = n1 * n2
    k = np.arange(n1)
    f1 = np.exp(-2j * np.pi * np.outer(k, k) / n1)
    f2 = np.exp(-2j * np.pi * np.outer(np.arange(n2), np.arange(n2)) / n2)
    tw = np.exp(-2j * np.pi * np.outer(np.arange(n1), np.arange(n2)) / n)
    as32 = lambda a: jnp.asarray(np.ascontiguousarray(a), F32)
    f1_fwd = as32(np.concatenate([f1.real, f1.imag], axis=0))
    f1_inv = as32(np.concatenate([f1.real, f1.imag], axis=1) / n)
    f2_inv = as32(np.block([[f2.real, f2.imag], [-f2.imag, f2.real]]))
    return f1_fwd, f1_inv, as32(f2.real), as32(f2.imag), f2_inv, as32(tw.real), as32(tw.imag)


def _stage_kernel(w_ref, x_ref, o_ref):
    o_ref[...] = jnp.dot(w_ref[...].astype(BF16), x_ref[...].astype(BF16),
                         preferred_element_type=F32).astype(o_ref.dtype)


def _stage_matmul(w, x, out_dtype):
    g, k, n = x.shape
    m = w.shape[0]
    tn = FFT_NTILE
    return pl.pallas_call(
        _stage_kernel,
        out_shape=jax.ShapeDtypeStruct((g, m, n), out_dtype),
        grid=(g, n // tn),
        in_specs=[pl.BlockSpec((m, k), lambda b, j: (0, 0)), pl.BlockSpec((None, k, tn), lambda b, j: (b, 0, j))],
        out_specs=pl.BlockSpec((None, m, tn), lambda b, j: (b, 0, j)),
        compiler_params=pltpu.CompilerParams(dimension_semantics=("arbitrary", "arbitrary"),
                                             vmem_limit_bytes=VMEM_LIMIT),
        name="dft_stage",
    )(w, x)


def _twiddled_f2(f2r, f2i, tr, ti):
    gr = f2r * tr - f2i * ti
    gi = f2r * ti + f2i * tr
    return jnp.concatenate([jnp.concatenate([gr, -gi], axis=1), jnp.concatenate([gi, gr], axis=1)], axis=0)


def _spectrum_kernel(a_ref, f2r_ref, f2i_ref, tr_ref, ti_ref, x_ref):
    n2 = f2r_ref.shape[0]
    tr = tr_ref[...]
    ti = ti_ref[...]
    for i in range(a_ref.shape[0]):
        g = _twiddled_f2(f2r_ref[...], f2i_ref[...], tr[i:i + 1], ti[i:i + 1])
        x_ref[i] = jnp.dot(g.astype(BF16), a_ref[i], preferred_element_type=F32)


def _conv_mid_kernel(a_ref, h_ref, f2r_ref, f2i_ref, f2inv_ref, tr_ref, ti_ref, z_ref):
    n2 = f2r_ref.shape[0]
    tr = tr_ref[...]
    ti = ti_ref[...]
    tr_col = tr.T
    ti_col = ti.T
    f2inv = f2inv_ref[...].astype(BF16)
    for i in range(a_ref.shape[0]):
        g = _twiddled_f2(f2r_ref[...], f2i_ref[...], tr[i:i + 1], ti[i:i + 1])
        x = jnp.dot(g.astype(BF16), a_ref[i], preferred_element_type=F32)
        xr, xi = x[:n2], x[n2:]
        hr, hi = h_ref[i, :n2], h_ref[i, n2:]
        y = jnp.concatenate([hr * xr - hi * xi, hr * xi + hi * xr], axis=0)
        w = jnp.dot(f2inv, y.astype(BF16), preferred_element_type=F32)
        wr, wi = w[:n2], w[n2:]
        cr, ci = tr_col[:, i:i + 1], ti_col[:, i:i + 1]
        z_ref[i] = jnp.concatenate([cr * wr + ci * wi, cr * wi - ci * wr], axis=0).astype(z_ref.dtype)


def _dft_mid_specs(g, c):
    n1, n2, kt = FFT_N1, FFT_N2, FFT_KTILE
    blk = pl.BlockSpec((None, kt, 2 * n2, c), lambda b, j: (b, j, 0, 0))
    const = lambda r, cc: pl.BlockSpec((r, cc), lambda b, j: (0, 0))
    twid = pl.BlockSpec((kt, n2), lambda b, j: (j, 0))
    params = pltpu.CompilerParams(dimension_semantics=("arbitrary", "arbitrary"), vmem_limit_bytes=VMEM_LIMIT)
    return blk, const, twid, params, (g, n1 // kt)


def _to_k1_major(a2d, c):
    g = a2d.shape[0]
    return a2d.reshape(g, 2, FFT_N1, FFT_N2, c).transpose(0, 2, 1, 3, 4).reshape(g, FFT_N1, 2 * FFT_N2, c)


def _filter_spectrum(filt, tables):
    g, n, c = filt.shape
    f1_fwd, _, f2r, f2i, _, twr, twi = tables
    a = _stage_matmul(f1_fwd, filt.reshape(g, FFT_N1, FFT_N2 * c), BF16)
    blk, const, twid, params, grid = _dft_mid_specs(g, c)
    return pl.pallas_call(
        _spectrum_kernel,
        out_shape=jax.ShapeDtypeStruct((g, FFT_N1, 2 * FFT_N2, c), F32),
        grid=grid,
        in_specs=[blk, const(FFT_N2, FFT_N2), const(FFT_N2, FFT_N2), twid, twid],
        out_specs=blk, compiler_params=params, name="dft_spectrum",
    )(_to_k1_major(a, c), f2r, f2i, twr, twi)


def _long_conv(u, spec, tables):
    g, l, c = u.shape
    f1_fwd, f1_inv, f2r, f2i, f2inv, twr, twi = tables
    half = l // FFT_N2
    a = _stage_matmul(f1_fwd[:, :half], u.reshape(g, half, FFT_N2 * c), BF16)
    blk, const, twid, params, grid = _dft_mid_specs(g, c)
    hspec = pl.BlockSpec((FFT_KTILE, 2 * FFT_N2, c), lambda b, j: (j, 0, 0))
    z = pl.pallas_call(
        _conv_mid_kernel,
        out_shape=jax.ShapeDtypeStruct((g, FFT_N1, 2 * FFT_N2, c), BF16),
        grid=grid,
        in_specs=[blk, hspec, const(FFT_N2, FFT_N2), const(FFT_N2, FFT_N2), const(2 * FFT_N2, 2 * FFT_N2),
                  twid, twid],
        out_specs=blk, compiler_params=params, name="dft_conv_mid",
    )(_to_k1_major(a, c), spec, f2r, f2i, f2inv, twr, twi)
    z2d = z.reshape(g, FFT_N1, 2, FFT_N2, c).transpose(0, 2, 1, 3, 4).reshape(g, 2 * FFT_N1, FFT_N2 * c)
    y = _stage_matmul(f1_inv[:half], z2d, F32)
    return y.reshape(g, l, c)


def _hyena_filters(L, w1, b1, w2, b2, w3, freq):
    t = jnp.linspace(0.0, 1.0, L, dtype=F32)[:, None]
    w = 2.0 * math.pi * jnp.arange(L, dtype=F32)[:, None] / L
    bands = jnp.linspace(1e-4, HY_BANDS - 1, HY_BANDS, dtype=F32)[None, :]
    feats = jnp.concatenate([t, jnp.cos(bands * w), -jnp.sin(bands * w)], axis=-1)
    z = jnp.sin(freq[0] * (feats @ w1 + b1))
    z = jnp.sin(freq[1] * (z @ w2 + b2))
    h = (z @ w3).reshape(L, HY_ORDER, 2, HY_W)
    max_decay = math.log(HY_DECAY_TARGET) / HY_FAST_DECAY
    min_decay = math.log(HY_DECAY_TARGET) / HY_SLOW_DECAY
    deltas = jnp.abs(jnp.linspace(min_decay, max_decay, HY_W, dtype=F32))
    h = h * jnp.exp(-t[:, :, None, None] * deltas)
    fwd = h[:, :, 0]
    bwd = h[1:, :, 1][::-1]
    l1 = jnp.sum(jnp.abs(fwd), axis=0) + jnp.sum(jnp.abs(bwd), axis=0)
    return jnp.concatenate([fwd, jnp.zeros((1, HY_ORDER, HY_W), F32), bwd], axis=0) / l1


def _hyena_long(u_raw, conv_w, conv_b, w1, b1, w2, b2, w3, freq, bias, norm_g):
    u = _short_conv(u_raw, conv_w, conv_b)
    v, x1, x2 = u[..., :HY_W], u[..., HY_W:2 * HY_W], u[..., 2 * HY_W:]
    L = u.shape[1]
    assert 2 * L == FFT_N1 * FFT_N2
    tables = _dft_tables(FFT_N1, FFT_N2)
    filt = _hyena_filters(L, w1, b1, w2, b2, w3, freq)
    spec = _filter_spectrum(jnp.moveaxis(filt, 1, 0), tables)
    z = x1 * (_long_conv(v, spec[0], tables) + v * bias[0])
    y = x2 * (_long_conv(z, spec[1], tables) + z * bias[1])
    return _rms(y, norm_g)


def _glr_direction(q, k, v, g, st_ref, reverse, nh):
    tc, hk = k.shape
    hv = v.shape[1]
    c = CHUNK_GATED
    hi = lax.Precision.HIGHEST
    ti = lax.broadcasted_iota(jnp.int32, (tc, tc), 0)
    tj = lax.broadcasted_iota(jnp.int32, (tc, tc), 1)
    same = (ti // c) == (tj // c)
    seen = (tj >= ti) if reverse else (tj <= ti)
    bcum = jnp.dot(jnp.where(same, jnp.where(seen, 1.0, 0.0), 0.0), g, precision=hi, preferred_element_type=F32)
    btot = jnp.dot(jnp.where(same, 1.0, 0.0), g, precision=hi, preferred_element_type=F32)
    qd = q * jnp.exp(bcum)
    kd = k * jnp.exp(btot - bcum)
    dec = jnp.exp(btot)
    head_sum = jnp.where(lax.broadcasted_iota(jnp.int32, (hk, hv), 0) // (hk // nh)
                         == lax.broadcasted_iota(jnp.int32, (hk, hv), 1) // (hv // nh), 1.0, 0.0).astype(BF16)
    in_chunk = lax.broadcasted_iota(jnp.int32, (tc, hk), 0) % c
    o = jnp.zeros((tc, hv), F32)
    for lag in range(c):
        if lag == 0:
            ks, bs, vs = k, bcum, v
        else:
            shift = tc - lag if reverse else lag
            ks, bs, vs = pltpu.roll(k, shift, 0), pltpu.roll(bcum, shift, 0), pltpu.roll(v, shift, 0)
        valid = (in_chunk + lag <= c - 1) if reverse else (in_chunk >= lag)
        x = q * ks * jnp.exp(jnp.where(valid, bcum - bs, NEG_INF))
        o = o + jnp.dot(x.astype(BF16), head_sum, preferred_element_type=F32) * vs
    head_mask = (lax.broadcasted_iota(jnp.int32, (hv, hk), 0) // (hv // nh)
                 == lax.broadcasted_iota(jnp.int32, (hv, hk), 1) // (hk // nh))
    st = st_ref[...]
    nch = tc // c
    outs = [None] * nch
    for ci in (range(nch - 1, -1, -1) if reverse else range(nch)):
        sl = slice(ci * c, (ci + 1) * c)
        outs[ci] = lax.dot_general(qd[sl].astype(BF16), st.astype(BF16), (((1,), (1,)), ((), ())),
                                   preferred_element_type=F32)
        ds = lax.dot_general(v[sl].astype(BF16), kd[sl].astype(BF16), (((0,), (0,)), ((), ())),
                             preferred_element_type=F32)
        st = st * dec[ci * c:ci * c + 1] + jnp.where(head_mask, ds, 0.0)
    st_ref[...] = st
    return o + jnp.concatenate(outs, axis=0)


def _log_sigmoid(z):
    return jnp.minimum(z, 0.0) - jnp.log1p(jnp.exp(-jnp.abs(z)))


def _hgrn2_kernel(qf_ref, if_ref, zf_ref, qb_ref, ib_ref, zb_ref, lb_ref, s0f_ref, s0b_ref,
                  of_ref, ob_ref, sf_ref, sb_ref, stf, stb):
    j = pl.program_id(1)

    @pl.when(j == 0)
    def _():
        stf[...] = s0f_ref[...]
        stb[...] = s0b_ref[...]

    one_minus_lb, log_lb, log_ub = lb_ref[0:1], lb_ref[1:2], lb_ref[2:3]

    def gate(z):
        return one_minus_lb * jax.nn.sigmoid(-z), jnp.logaddexp(log_lb, log_ub + _log_sigmoid(z))

    silu = lambda a: a * jax.nn.sigmoid(a)
    k_f, g_f = gate(zf_ref[...])
    k_b, g_b = gate(zb_ref[...])
    of_ref[...] = _glr_direction(silu(qf_ref[...]), k_f, if_ref[...], g_f, stf, False, HG_H)
    ob_ref[...] = _glr_direction(silu(qb_ref[...]), k_b, ib_ref[...], g_b, stb, True, HG_H)

    @pl.when(j == pl.num_programs(1) - 1)
    def _():
        sf_ref[...] = stf[...]
        sb_ref[...] = stb[...]


def _gla_kernel(qkf_ref, vf_ref, nf_ref, qkb_ref, vb_ref, nb_ref, aup_ref, ab_ref, s0f_ref, s0b_ref,
                of_ref, ob_ref, sf_ref, sb_ref, stf, stb):
    j = pl.program_id(1)

    @pl.when(j == 0)
    def _():
        stf[...] = s0f_ref[...]
        stb[...] = s0b_ref[...]

    def gate(narrow, idx):
        a = narrow[:, idx * GLA_RANK:(idx + 1) * GLA_RANK]
        lin = jnp.dot(a.astype(BF16), aup_ref[idx].astype(BF16), preferred_element_type=F32) + ab_ref[idx]
        return _log_sigmoid(lin) / GLA_NORMALIZER

    qk_f = qkf_ref[...]
    qk_b = qkb_ref[...]
    of_ref[...] = _glr_direction(qk_f[:, :GLA_KW] * GLA_DK ** -0.5, qk_f[:, GLA_KW:], vf_ref[...],
                                 gate(nf_ref[...], 0), stf, False, GLA_H)
    ob_ref[...] = _glr_direction(qk_b[:, :GLA_KW] * GLA_DK ** -0.5, qk_b[:, GLA_KW:], vb_ref[...],
                                 gate(nb_ref[...], 1), stb, True, GLA_H)

    @pl.when(j == pl.num_programs(1) - 1)
    def _():
        sf_ref[...] = stf[...]
        sb_ref[...] = stb[...]


def _glr_call(kernel_fn, p3, fwd_cols, bwd_cols, consts, s0_f, s0_b, hk, hv, name):
    bsz, t, _ = p3.shape
    tc = GLR_TILE
    nsb = t // tc
    fwd = lambda w, c: pl.BlockSpec((None, tc, w), lambda b, j: (b, j, c))
    bwd = lambda w, c: pl.BlockSpec((None, tc, w), lambda b, j: (b, nsb - 1 - j, c))
    whole = lambda a: pl.BlockSpec(a.shape, lambda b, j: (0,) * a.ndim)
    st = pl.BlockSpec((None, hv, hk), lambda b, j: (b, 0, 0))
    out_f = pl.BlockSpec((None, tc, hv), lambda b, j: (b, j, 0))
    out_b = pl.BlockSpec((None, tc, hv), lambda b, j: (b, nsb - 1 - j, 0))
    return pl.pallas_call(
        kernel_fn,
        out_shape=[jax.ShapeDtypeStruct((bsz, t, hv), F32), jax.ShapeDtypeStruct((bsz, t, hv), F32),
                   jax.ShapeDtypeStruct((bsz, hv, hk), F32), jax.ShapeDtypeStruct((bsz, hv, hk), F32)],
        grid=(bsz, nsb),
        in_specs=([fwd(w, c) for w, c in fwd_cols] + [bwd(w, c) for w, c in bwd_cols]
                  + [whole(a) for a in consts] + [st, st]),
        out_specs=[out_f, out_b, st, st],
        scratch_shapes=[pltpu.VMEM((hv, hk), F32), pltpu.VMEM((hv, hk), F32)],
        compiler_params=pltpu.CompilerParams(dimension_semantics=("arbitrary", "arbitrary"),
                                             vmem_limit_bytes=VMEM_LIMIT),
        name=name,
    )(*([p3] * (len(fwd_cols) + len(bwd_cols))), *consts, s0_f, s0_b)


def _hgrn2_seq(p3, lb, norm_g, s0_f, s0_b, with_out):
    lb_rows = jnp.stack([1.0 - lb, jnp.log(lb), jnp.log1p(-lb)])
    cols = lambda z: [(GROUP_W, COL_HG_Q), (GROUP_W, COL_HG_I), (GROUP_W, z)]
    o_f, o_b, s_f, s_b = _glr_call(_hgrn2_kernel, p3, cols(COL_HG_ZF), cols(COL_HG_ZB), [lb_rows],
                                   s0_f, s0_b, HG_W, HG_W, "hgrn2_recurrence")
    if not with_out:
        return None, s_f, s_b
    gate = p3[..., COL_HG_G * GROUP_W:(COL_HG_G + 1) * GROUP_W]
    return _head_rms_tokens(o_f + o_b, norm_g, HG_H) * jax.nn.silu(gate), s_f, s_b


def _hgrn2(pc3, pl3, lb, norm_g, with_ctx):
    s0 = jnp.zeros((pl3.shape[0], HG_W, HG_W), F32)
    yc, s_f, s_b = _hgrn2_seq(pc3, lb, norm_g, s0, s0, with_ctx)
    y, _, _ = _hgrn2_seq(pl3, lb, norm_g, s_f, s_b, True)
    return yc, y


def _gla_seq(p3, a_up, a_b, norm_g, s0_f, s0_b, with_out):
    cols = [(GROUP_W, COL_GLA_QK), (GROUP_W, COL_GLA_V), (LANE, COL_NARROW)]
    o_f, o_b, s_f, s_b = _glr_call(_gla_kernel, p3, cols, cols, [a_up, a_b.reshape(2, 1, GLA_KW)],
                                   s0_f, s0_b, GLA_KW, GLA_VW, "gla_recurrence")
    if not with_out:
        return None, s_f, s_b
    gate = p3[..., COL_GLA_R * GROUP_W:(COL_GLA_R + 1) * GROUP_W]
    return _head_rms_tokens(o_f + o_b, norm_g, GLA_H) * jax.nn.silu(gate), s_f, s_b


def _gla(pc3, pl3, a_up, a_b, norm_g, with_ctx):
    s0 = jnp.zeros((pl3.shape[0], GLA_VW, GLA_KW), F32)
    yc, s_f, s_b = _gla_seq(pc3, a_up, a_b, norm_g, s0, s0, with_ctx)
    y, _, _ = _gla_seq(pl3, a_up, a_b, norm_g, s_f, s_b, True)
    return yc, y


def _mlstm_direction(q, k, v, igx, lfx, s_ref, n_ref, m_ref, reverse, nh):
    tc, w = q.shape
    seg = w // nh
    assert tc == seg
    hi = lax.Precision.HIGHEST
    ti = lax.broadcasted_iota(jnp.int32, (tc, tc), 0)
    tj = lax.broadcasted_iota(jnp.int32, (tc, tc), 1)
    seen = (tj >= ti) if reverse else (tj <= ti)
    b = jnp.dot(jnp.where(seen, 1.0, 0.0), lfx, precision=hi, preferred_element_type=F32)
    bl = b[0:1] if reverse else b[tc - 1:tc]
    a = bl - b + igx
    ma = jnp.max(a, axis=0, keepdims=True)
    kw = jnp.exp(a - ma) * k
    s_prev = s_ref[...]
    n_prev = n_ref[...]
    m_prev = m_ref[...]
    lane = lax.broadcasted_iota(jnp.int32, (tc, w), 1)
    row = lax.broadcasted_iota(jnp.int32, (tc, w), 0)
    same_head = (lax.broadcasted_iota(jnp.int32, (w, w), 0) // seg
                 == lax.broadcasted_iota(jnp.int32, (w, w), 1) // seg)
    kexp = jnp.where(same_head, jnp.concatenate([k] * nh, axis=0), 0.0)
    vexp = jnp.where(same_head, jnp.concatenate([v] * nh, axis=0), 0.0)
    scores = lax.dot_general(q.astype(BF16), kexp.astype(BF16), (((1,), (1,)), ((), ())),
                             preferred_element_type=F32)
    s_lane = lane % seg
    by_src = jnp.sum(jnp.where(s_lane == row, igx - b, 0.0), axis=0, keepdims=True)
    ok = (s_lane >= row) if reverse else (s_lane <= row)
    dmat = jnp.where(ok, b + by_src, NEG_INF)
    inter = b + m_prev
    head_of_lane = lane // seg
    seg_max = jnp.full((tc, w), NEG_INF, F32)
    for h in range(nh):
        in_h = head_of_lane == h
        seg_max = jnp.where(in_h, jnp.max(jnp.where(in_h, dmat, NEG_INF), axis=1, keepdims=True), seg_max)
    m_t = jnp.maximum(inter, seg_max)
    wq = jnp.exp(dmat - m_t) * scores
    w_int = jnp.exp(inter - m_t)
    head_sum = jnp.where(same_head, 1.0, 0.0).astype(BF16)
    num = (jnp.dot(wq.astype(BF16), vexp.astype(BF16), preferred_element_type=F32)
           + w_int * jnp.dot(q.astype(BF16), s_prev.astype(BF16), preferred_element_type=F32))
    den = (jnp.dot(wq.astype(BF16), head_sum, preferred_element_type=F32)
           + w_int * jnp.dot((q * n_prev).astype(BF16), head_sum, preferred_element_type=F32))
    h_out = num / jnp.maximum(jnp.abs(den), jnp.exp(-m_t))
    m_new = jnp.maximum(bl + m_prev, ma)
    d_old = jnp.exp(bl + m_prev - m_new)
    d_new = jnp.exp(ma - m_new)
    ds = lax.dot_general(kw.astype(BF16), v.astype(BF16), (((0,), (0,)), ((), ())), preferred_element_type=F32)
    s_ref[...] = d_old * s_prev + d_new * jnp.where(same_head, ds, 0.0)
    n_ref[...] = d_old * n_prev + d_new * jnp.sum(kw, axis=0, keepdims=True)
    m_ref[...] = m_new
    return h_out


def _mlstm_kernel(qf_ref, kf_ref, vf_ref, igf_ref, lff_ref, qb_ref, kb_ref, vb_ref, igb_ref, lfb_ref,
                  s0f_ref, n0f_ref, m0f_ref, s0b_ref, n0b_ref, m0b_ref,
                  hf_ref, hb_ref, sf_ref, nf_ref, mf_ref, sb_ref, nb_ref, mb_ref,
                  s_f, n_f, m_f, s_b, n_b, m_b, *, nh):
    j = pl.program_id(1)

    @pl.when(j == 0)
    def _():
        s_f[...] = s0f_ref[...]
        n_f[...] = n0f_ref[...]
        m_f[...] = m0f_ref[...]
        s_b[...] = s0b_ref[...]
        n_b[...] = n0b_ref[...]
        m_b[...] = m0b_ref[...]

    hf_ref[...] = _mlstm_direction(qf_ref[...], kf_ref[...], vf_ref[...], igf_ref[...], lff_ref[...],
                                   s_f, n_f, m_f, False, nh)
    hb_ref[...] = _mlstm_direction(qb_ref[...], kb_ref[...], vb_ref[...], igb_ref[...], lfb_ref[...],
                                   s_b, n_b, m_b, True, nh)

    @pl.when(j == pl.num_programs(1) - 1)
    def _():
        sf_ref[...] = s_f[...]
        nf_ref[...] = n_f[...]
        mf_ref[...] = m_f[...]
        sb_ref[...] = s_b[...]
        nb_ref[...] = n_b[...]
        mb_ref[...] = m_b[...]


def _mlstm_bidir(q, k, v, ig_f, lf_f, ig_b, lf_b, st_f, st_b, nh):
    bsz, t, w = q.shape
    tc = CHUNK_ML
    nsb = t // tc
    fwd = pl.BlockSpec((None, tc, w), lambda b, j: (b, j, 0))
    bwd = pl.BlockSpec((None, tc, w), lambda b, j: (b, nsb - 1 - j, 0))
    mat = pl.BlockSpec((None, w, w), lambda b, j: (b, 0, 0))
    vec = pl.BlockSpec((None, 1, w), lambda b, j: (b, 0, 0))
    sds = jax.ShapeDtypeStruct
    state_shapes = [sds((bsz, w, w), F32), sds((bsz, 1, w), F32), sds((bsz, 1, w), F32)]
    outs = pl.pallas_call(
        functools.partial(_mlstm_kernel, nh=nh),
        out_shape=[sds((bsz, t, w), F32), sds((bsz, t, w), F32)] + state_shapes + state_shapes,
        grid=(bsz, nsb),
        in_specs=[fwd] * 5 + [bwd] * 5 + [mat, vec, vec] * 2,
        out_specs=[fwd, bwd] + [mat, vec, vec] * 2,
        scratch_shapes=[pltpu.VMEM((w, w), F32), pltpu.VMEM((1, w), F32), pltpu.VMEM((1, w), F32)] * 2,
        compiler_params=pltpu.CompilerParams(dimension_semantics=("arbitrary", "arbitrary"),
                                             vmem_limit_bytes=VMEM_LIMIT),
        name="mlstm_recurrence",
    )(q, k, v, ig_f, lf_f, q, k, v, ig_b, lf_b, *st_f, *st_b)
    return outs[0], outs[1], tuple(outs[2:5]), tuple(outs[5:8])


def _mlstm_seq(p3, conv_w, conv_b, gate_b, norm_g, st_f, st_b, with_out):
    part = lambda c: p3[..., c * GROUP_W:(c + 1) * GROUP_W]
    qk = jax.nn.silu(_short_conv(p3[..., COL_ML_Q * GROUP_W:(COL_ML_K + 1) * GROUP_W], conv_w, conv_b))
    v, o = part(COL_ML_V), part(COL_ML_O)
    bsz, t, _ = p3.shape
    gates = p3[..., NARROW_ML_GATES:NARROW_ML_GATES + 4 * ML_H]
    gt = gates.reshape(bsz, t, 4, ML_H) + gate_b
    expand = lambda a: jnp.repeat(a, ML_DH, axis=-1)
    h_f, h_b, fin_f, fin_b = _mlstm_bidir(
        qk[..., :ML_W], qk[..., ML_W:] * ML_DH ** -0.5, v,
        expand(gt[:, :, 0]), expand(jax.nn.log_sigmoid(gt[:, :, 1])),
        expand(gt[:, :, 2]), expand(jax.nn.log_sigmoid(gt[:, :, 3])), st_f, st_b, ML_H)
    y = jax.nn.sigmoid(o) * _head_rms_tokens(h_f + h_b, norm_g, ML_H) if with_out else None
    return y, fin_f, fin_b


def _mlstm(pc3, pl3, conv_w, conv_b, gate_b, norm_g, with_ctx):
    bsz = pl3.shape[0]
    st0 = (jnp.zeros((bsz, ML_W, ML_W), F32), jnp.zeros((bsz, 1, ML_W), F32), jnp.zeros((bsz, 1, ML_W), F32))
    yc, st_f, st_b = _mlstm_seq(pc3, conv_w, conv_b, gate_b, norm_g, st0, st0, with_ctx)
    y, _, _ = _mlstm_seq(pl3, conv_w, conv_b, gate_b, norm_g, st_f, st_b, True)
    return yc, y


def _router_kernel(h_ref, wt_ref, b_ref, eidx_ref, wsel_ref, cnt_ref):
    i = pl.program_id(0)
    tm = h_ref.shape[0]
    ne = wt_ref.shape[0]
    per_group = ne // N_EXPERT_GROUPS
    logits = lax.dot_general(wt_ref[...], h_ref[...], (((1,), (1,)), ((), ())),
                             preferred_element_type=F32, precision=lax.Precision.HIGHEST)
    s = jax.nn.sigmoid(logits)
    sel = s + b_ref[...]
    row = lax.broadcasted_iota(jnp.int32, (ne, tm), 0)
    gs = []
    for g in range(N_EXPERT_GROUPS):
        blk = sel[g * per_group:(g + 1) * per_group]
        r = lax.broadcasted_iota(jnp.int32, blk.shape, 0)
        m1 = jnp.max(blk, axis=0, keepdims=True)
        i1 = jnp.min(jnp.where(blk == m1, r, per_group), axis=0, keepdims=True)
        m2 = jnp.max(jnp.where(r == i1, NEG_INF, blk), axis=0, keepdims=True)
        gs.append(m1 + m2)
    grp = jnp.concatenate(gs, axis=0)
    grow = lax.broadcasted_iota(jnp.int32, grp.shape, 0)
    gsel = jnp.zeros(grp.shape, F32)
    for _ in range(TOPK_GROUPS):
        m = jnp.max(grp, axis=0, keepdims=True)
        gi = jnp.min(jnp.where(grp == m, grow, N_EXPERT_GROUPS), axis=0, keepdims=True)
        hit = grow == gi
        gsel = jnp.where(hit, 1.0, gsel)
        grp = jnp.where(hit, NEG_INF, grp)
    masked = jnp.concatenate(
        [jnp.where(gsel[g:g + 1] > 0.0, sel[g * per_group:(g + 1) * per_group], NEG_INF)
         for g in range(N_EXPERT_GROUPS)], axis=0)
    eis, ws = [], []
    picked = jnp.zeros((ne, tm), F32)
    for _ in range(TOP_K):
        m = jnp.max(masked, axis=0, keepdims=True)
        ei = jnp.min(jnp.where(masked == m, row, ne), axis=0, keepdims=True)
        hit = row == ei
        ws.append(jnp.sum(jnp.where(hit, s, 0.0), axis=0, keepdims=True))
        eis.append(ei)
        picked = jnp.where(hit, 1.0, picked)
        masked = jnp.where(hit, NEG_INF, masked)
    w = jnp.concatenate(ws, axis=0)
    eidx_ref[...] = jnp.concatenate(eis, axis=0)
    wsel_ref[...] = w / jnp.sum(w, axis=0, keepdims=True) * ROUTED_SCALE
    tot = jnp.dot(picked.astype(BF16), jnp.ones((tm, LANE), BF16), preferred_element_type=F32)

    @pl.when(i == 0)
    def _():
        cnt_ref[...] = jnp.zeros_like(cnt_ref)

    cnt_ref[...] += tot


def _pos_kernel(eidx_ref, base_ref, pos_ref, carry_ref):
    i = pl.program_id(0)
    tm = eidx_ref.shape[1]
    ne = base_ref.shape[0]

    @pl.when(i == 0)
    def _():
        carry_ref[...] = jnp.zeros_like(carry_ref)

    eidx = eidx_ref[...]
    row = lax.broadcasted_iota(jnp.int32, (ne, tm), 0)
    picked = jnp.zeros((ne, tm), F32)
    for k in range(TOP_K):
        picked = jnp.where(row == eidx[k:k + 1], 1.0, picked)
    pb = picked.astype(BF16)
    before = jnp.where(lax.broadcasted_iota(jnp.int32, (tm, tm), 0) < lax.broadcasted_iota(jnp.int32, (tm, tm), 1),
                       1.0, 0.0).astype(BF16)
    rank = jnp.dot(pb, before, preferred_element_type=F32)
    tot = jnp.dot(pb, jnp.ones((tm, LANE), BF16), preferred_element_type=F32)
    dest = rank + (base_ref[...] + carry_ref[:, 0:1])
    pos = [jnp.sum(jnp.where(row == eidx[k:k + 1], dest, 0.0), axis=0, keepdims=True) for k in range(TOP_K)]
    pos_ref[...] = jnp.concatenate(pos, axis=0).astype(jnp.int32)
    carry_ref[...] += tot


def _dispatch_kernel(pos_ref, h_ref, xs_init_ref, xs_ref, sem):
    del xs_init_ref
    tm = h_ref.shape[0]

    def row_copy(n, k):
        return pltpu.make_async_copy(h_ref.at[pl.ds(n, 1)], xs_ref.at[pl.ds(pos_ref[k, n], 1)], sem)

    def issue(n, carry):
        for k in range(TOP_K):
            row_copy(n, k).start()
        return carry

    def drain(n, carry):
        for k in range(TOP_K):
            row_copy(n, k).wait()
        return carry

    lax.fori_loop(0, tm, issue, 0)
    lax.fori_loop(0, tm, drain, 0)


def _moe_ffn_kernel(blk_e_ref, nused_ref, x_ref, wgu_ref, wdn_ref, o_ref, wgu_s, wdn_s):
    i = pl.program_id(0)
    e = blk_e_ref[i]
    e_prev = blk_e_ref[jnp.maximum(i - 1, 0)]

    @pl.when(i < nused_ref[0])
    def _():
        @pl.when((i == 0) | (e != e_prev))
        def _():
            wgu_s[...] = wgu_ref[...].astype(BF16)
            wdn_s[...] = wdn_ref[...].astype(BF16)

        au = jnp.dot(x_ref[...].astype(BF16), wgu_s[...], preferred_element_type=F32)
        a = au[:, :EXPERT_FF]
        u = au[:, EXPERT_FF:]
        h = (a * jax.nn.sigmoid(a)) * u
        o_ref[...] = jnp.dot(h.astype(BF16), wdn_s[...], preferred_element_type=F32)

    @pl.when(i >= nused_ref[0])
    def _():
        o_ref[...] = jnp.zeros_like(o_ref)


def _combine_kernel(pos_ref, w_ref, y_hbm, o_ref, buf, sem):
    tm = o_ref.shape[0]

    def row_copy(n, k):
        return pltpu.make_async_copy(y_hbm.at[pl.ds(pos_ref[k, n], 1)], buf.at[k, pl.ds(n, 1)], sem)

    def issue(n, carry):
        for k in range(TOP_K):
            row_copy(n, k).start()
        return carry

    def drain(n, carry):
        for k in range(TOP_K):
            row_copy(n, k).wait()
        return carry

    lax.fori_loop(0, tm, issue, 0)
    lax.fori_loop(0, tm, drain, 0)
    acc = buf[0] * w_ref[:, 0:1]
    for k in range(1, TOP_K):
        acc = acc + buf[k] * w_ref[:, k:k + 1]
    o_ref[...] = acc


def _moe_routed(t, router_w, router_b, w_gu, w_down):
    n, d = t.shape
    ne = router_w.shape[1]
    ff2 = w_gu.shape[-1]
    params = pltpu.CompilerParams(dimension_semantics=("arbitrary",), vmem_limit_bytes=VMEM_LIMIT)
    tm = ROUTER_TILE
    eidx, wsel, cnt = pl.pallas_call(
        _router_kernel,
        out_shape=[jax.ShapeDtypeStruct((TOP_K, n), jnp.int32), jax.ShapeDtypeStruct((TOP_K, n), F32),
                   jax.ShapeDtypeStruct((ne, LANE), F32)],
        grid=(n // tm,),
        in_specs=[pl.BlockSpec((tm, d), lambda i: (i, 0)), pl.BlockSpec((ne, d), lambda i: (0, 0)),
                  pl.BlockSpec((ne, 1), lambda i: (0, 0))],
        out_specs=[pl.BlockSpec((TOP_K, tm), lambda i: (0, i)), pl.BlockSpec((TOP_K, tm), lambda i: (0, i)),
                   pl.BlockSpec((ne, LANE), lambda i: (0, 0))],
        compiler_params=params, name="moe_router",
    )(t, router_w.T, router_b.reshape(ne, 1))
    bm = MOE_ROWS
    counts = cnt[:, 0].astype(jnp.int32)
    padded = (counts + bm - 1) // bm * bm
    pad_end = jnp.cumsum(padded)
    pad_start = pad_end - padded
    n_blocks = (n * TOP_K + ne * (bm - 1)) // bm + 1
    blk_first = jnp.arange(n_blocks, dtype=jnp.int32) * bm
    blk_e = jnp.minimum(jnp.sum((pad_end[None, :] <= blk_first[:, None]).astype(jnp.int32), axis=1), ne - 1)
    n_used = (pad_end[-1] // bm).astype(jnp.int32).reshape(1)
    pos = pl.pallas_call(
        _pos_kernel,
        out_shape=jax.ShapeDtypeStruct((TOP_K, n), jnp.int32),
        grid=(n // tm,),
        in_specs=[pl.BlockSpec((TOP_K, tm), lambda i: (0, i)), pl.BlockSpec((ne, 1), lambda i: (0, 0))],
        out_specs=pl.BlockSpec((TOP_K, tm), lambda i: (0, i)),
        scratch_shapes=[pltpu.VMEM((ne, LANE), F32)],
        compiler_params=params, name="moe_positions",
    )(eidx, pad_start.astype(F32).reshape(ne, 1))
    ts = SCATTER_TILE
    p = n_blocks * bm
    pos_spec = pl.BlockSpec((TOP_K, ts), lambda i: (0, i), memory_space=pltpu.SMEM)
    xs = pl.pallas_call(
        _dispatch_kernel,
        out_shape=jax.ShapeDtypeStruct((p, d), F32),
        grid=(n // ts,),
        in_specs=[pos_spec, pl.BlockSpec((ts, d), lambda i: (i, 0)), pl.BlockSpec(memory_space=pl.ANY)],
        out_specs=pl.BlockSpec(memory_space=pl.ANY),
        scratch_shapes=[pltpu.SemaphoreType.DMA],
        input_output_aliases={2: 0},
        compiler_params=params, name="moe_dispatch",
    )(pos, t, jnp.zeros((p, d), F32))

    def x_map(i, blk_e, nused):
        return (jnp.minimum(i, nused[0] - 1), 0)

    def w_map(i, blk_e, nused):
        return (blk_e[i], 0, 0)

    y_p = pl.pallas_call(
        _moe_ffn_kernel,
        out_shape=jax.ShapeDtypeStruct((p, d), F32),
        grid_spec=pltpu.PrefetchScalarGridSpec(
            num_scalar_prefetch=2,
            grid=(n_blocks,),
            in_specs=[pl.BlockSpec((bm, d), x_map),
                      pl.BlockSpec((None, d, ff2), w_map),
                      pl.BlockSpec((None, ff2 // 2, d), w_map)],
            out_specs=pl.BlockSpec((bm, d), lambda i, blk_e, nused: (i, 0)),
            scratch_shapes=[pltpu.VMEM((d, ff2), BF16), pltpu.VMEM((ff2 // 2, d), BF16)],
        ),
        compiler_params=params, name="moe_expert_ffn",
    )(blk_e, n_used, xs, w_gu, w_down)
    return pl.pallas_call(
        _combine_kernel,
        out_shape=jax.ShapeDtypeStruct((n, d), F32),
        grid=(n // ts,),
        in_specs=[pos_spec, pl.BlockSpec((ts, TOP_K), lambda i: (i, 0)), pl.BlockSpec(memory_space=pl.ANY)],
        out_specs=pl.BlockSpec((ts, d), lambda i: (i, 0)),
        scratch_shapes=[pltpu.VMEM((TOP_K, ts, d), F32), pltpu.SemaphoreType.DMA],
        compiler_params=params, name="moe_combine",
    )(pos, wsel.T, y_p)


def _moe(t, router_w, router_b, w_gu, w_down, sh_gu, sh_down):
    routed = _moe_routed(t, router_w, router_b, w_gu, w_down)
    au = _matmul(t, sh_gu)
    ff = sh_gu.shape[-1] // 2
    shared = _matmul(jax.nn.silu(au[:, :ff]) * au[:, ff:], sh_down)
    return routed + shared


def kernel(x, c, ctx, c_ctx, ada_w, ada_b, norm_g, w_in, w_out, hy_conv_w, hy_conv_b, hy_ffn_w1, hy_ffn_b1, hy_ffn_w2, hy_ffn_b2, hy_ffn_w3, hy_freq, hy_bias, hy_norm, hg_lb_logits, hg_norm, gla_a_up, gla_a_b, gla_norm, ml_conv_w, ml_conv_b, ml_gate_b, ml_norm, router_w, router_b, exp_w_gu, exp_w_down, sh_w_gu, sh_w_down):
    bsz, seq, d = x.shape
    n_ctx = ctx.shape[1]
    depth = ada_w.shape[0]
    rows = seq // GRID_W
    x = x + _pos_embed_2d(rows, d)[None]
    xc = ctx
    lb_cum = jnp.cumsum(jax.nn.softmax(hg_lb_logits, axis=0), axis=0)
    lower_bounds = lb_cum - lb_cum[0:1]
    for l in range(depth):
        with_ctx = l < depth - 1
        mod = (jax.nn.silu(c) @ ada_w[l] + ada_b[l])[:, None, :]
        mod_c = jax.nn.silu(c_ctx) @ ada_w[l] + ada_b[l]
        sh1, sc1, g1, sh2, sc2, g2 = jnp.split(mod, 6, axis=-1)
        csh1, csc1, cg1, csh2, csc2, cg2 = jnp.split(mod_c, 6, axis=-1)
        w_in_l = _arrange_w_in(w_in[l])
        ctx_rows = lambda a: jnp.broadcast_to(a.reshape(1, 1, d), (bsz, 1, d))
        pl3 = _in_proj(x, norm_g[l, 0], sc1, sh1, w_in_l)
        pc3 = _in_proj(xc, norm_g[l, 0], ctx_rows(csc1), ctx_rows(csh1), w_in_l)
        hy_args = (hy_conv_w[l], hy_conv_b[l], hy_ffn_w1[l], hy_ffn_b1[l], hy_ffn_w2[l], hy_ffn_b2[l],
                   hy_ffn_w3[l], hy_freq[l], hy_bias[l], hy_norm[l])
        hy_cols = slice(COL_HY * GROUP_W, (COL_HY + 3) * GROUP_W)
        y_hy = _hyena_long(pl3[..., hy_cols], *hy_args)
        yc_hg, y_hg = _hgrn2(pc3, pl3, lower_bounds[l], hg_norm[l], with_ctx)
        yc_gla, y_gla = _gla(pc3, pl3, gla_a_up[l], gla_a_b[l], gla_norm[l], with_ctx)
        yc_ml, y_ml = _mlstm(pc3, pl3, ml_conv_w[l], ml_conv_b[l], ml_gate_b[l], ml_norm[l], with_ctx)
        y_cat = jnp.concatenate([y_hy, y_hg, y_gla, y_ml], axis=-1).reshape(bsz * seq, d)
        if with_ctx:
            yc_cat = jnp.concatenate([_hyena(pc3[..., hy_cols], *hy_args), yc_hg, yc_gla, yc_ml],
                                     axis=-1).reshape(bsz * n_ctx, d)
            y_all = _matmul(jnp.concatenate([y_cat, yc_cat], axis=0), w_out[l])
            y = y_all[:bsz * seq].reshape(bsz, seq, d)
            yc = y_all[bsz * seq:].reshape(bsz, n_ctx, d)
        else:
            y = _matmul(y_cat, w_out[l]).reshape(bsz, seq, d)
        x = x + g1 * _rms(y, norm_g[l, 1])
        h = _rms(x, norm_g[l, 2]) * (1.0 + sc2) + sh2
        moe_args = (router_w[l], router_b[l], exp_w_gu[l], exp_w_down[l], sh_w_gu[l], sh_w_down[l])
        if with_ctx:
            xc = xc + cg1 * _rms(yc, norm_g[l, 1])
            hc = _rms(xc, norm_g[l, 2]) * (1.0 + csc2) + csh2
            f = _moe(jnp.concatenate([h.reshape(bsz * seq, d), hc.reshape(bsz * n_ctx, d)], axis=0), *moe_args)
            xc = xc + cg2 * _rms(f[bsz * seq:].reshape(bsz, n_ctx, d), norm_g[l, 3])
            f = f[:bsz * seq]
        else:
            f = _moe(h.reshape(bsz * seq, d), *moe_args)
        x = x + g2 * _rms(f.reshape(bsz, seq, d), norm_g[l, 3])
    return x
```

```python
import functools
import math

import jax
import jax.numpy as jnp
import numpy as np
from jax import lax
from jax.experimental import pallas as pl
from jax.experimental.pallas import tpu as pltpu

F32 = jnp.float32
BF16 = jnp.bfloat16

D_MODEL = 1024
GRID_W = 64
EPS = 1e-6
POS_BASE = 10000.0
GROUP_W = D_MODEL // 4
SHORT_CONV = 3
HY_W = GROUP_W
HY_ORDER = 2
HY_EMB = 33
HY_BANDS = (HY_EMB - 1) // 2
HY_FAST_DECAY = 0.3
HY_SLOW_DECAY = 1.5
HY_DECAY_TARGET = 1e-2
HG_H = 4
HG_W = GROUP_W
HG_DK = HG_W // HG_H
GLA_H = 4
GLA_KW = GROUP_W // 2
GLA_VW = GROUP_W
GLA_DK = GLA_KW // GLA_H
GLA_DV = GLA_VW // GLA_H
GLA_RANK = 16
GLA_NORMALIZER = 16.0
ML_H = 4
ML_W = GROUP_W
ML_DH = ML_W // ML_H
CHUNK_GATED = 16
CHUNK_ML = 64
N_EXPERTS = 256
TOP_K = 8
N_EXPERT_GROUPS = 8
TOPK_GROUPS = 4
EXPERT_FF = 256
ROUTED_SCALE = 2.5
IN_SPLITS = (HY_W, HY_W, HY_W,
             HG_W, HG_W, HG_W, HG_W, HG_W,
             GLA_KW, GLA_KW, GLA_VW, GLA_RANK, GLA_RANK, GLA_VW,
             ML_W, ML_W, ML_W, 4 * ML_H, ML_W)
P_ORDER = (0, 1, 2, 3, 4, 5, 6, 7, 8, 9, 10, 13, 14, 15, 16, 18, 11, 12, 17)
COL_HY = 0
COL_HG_Q, COL_HG_I, COL_HG_ZF, COL_HG_ZB, COL_HG_G = 3, 4, 5, 6, 7
COL_GLA_QK, COL_GLA_V, COL_GLA_R = 8, 9, 10
COL_ML_Q, COL_ML_K, COL_ML_V, COL_ML_O = 11, 12, 13, 14
N_WIDE = 15

LANE = 128
SUBLANE = 8
ROW_TILE = 512
MOE_ROWS = 256
ROUTER_TILE = 256
SCATTER_TILE = 256
GLR_TILE = 128
VMEM_LIMIT = 56 * 1024 * 1024
NEG_INF = float("-inf")
COL_NARROW = N_WIDE * GROUP_W // LANE
NARROW_ML_GATES = N_WIDE * GROUP_W + 2 * GLA_RANK
P_WIDTH = N_WIDE * GROUP_W + LANE


def _mm_kernel(x_ref, w_ref, o_ref, *, precise):
    if precise:
        o_ref[...] = jnp.dot(x_ref[...].astype(F32), w_ref[...], preferred_element_type=F32,
                             precision=lax.Precision.HIGHEST)
    else:
        o_ref[...] = jnp.dot(x_ref[...].astype(BF16), w_ref[...], preferred_element_type=F32)


def _matmul(x, w, precise=False):
    m, k = x.shape
    n = w.shape[1]
    n_pad = -n % LANE
    w = w.astype(F32 if precise else BF16)
    if n_pad:
        w = jnp.pad(w, ((0, 0), (0, n_pad)))
    tm = ROW_TILE
    assert m % tm == 0
    out = pl.pallas_call(
        functools.partial(_mm_kernel, precise=precise),
        out_shape=jax.ShapeDtypeStruct((m, n + n_pad), F32),
        grid=(m // tm,),
        in_specs=[pl.BlockSpec((tm, k), lambda i: (i, 0)),
                  pl.BlockSpec((k, n + n_pad), lambda i: (0, 0))],
        out_specs=pl.BlockSpec((tm, n + n_pad), lambda i: (i, 0)),
        compiler_params=pltpu.CompilerParams(dimension_semantics=("arbitrary",),
                                             vmem_limit_bytes=VMEM_LIMIT),
        name="dense_matmul",
    )(x, w)
    return out[:, :n] if n_pad else out


def _arrange_w_in(w):
    offs = np.concatenate([[0], np.cumsum(IN_SPLITS)])
    cols = [w[:, offs[i]:offs[i + 1]] for i in P_ORDER]
    used = sum(IN_SPLITS)
    return jnp.concatenate(cols + [jnp.zeros((w.shape[0], P_WIDTH - used), w.dtype)], axis=1).astype(BF16)


def _in_proj_kernel(x_ref, g_ref, sc_ref, sh_ref, w_ref, o_ref):
    x = x_ref[...]
    y = x * lax.rsqrt(jnp.mean(x * x, axis=-1, keepdims=True) + EPS) * g_ref[...]
    h = y * (1.0 + sc_ref[...]) + sh_ref[...]
    o_ref[...] = jnp.dot(h.astype(BF16), w_ref[...], preferred_element_type=F32)


def _in_proj(x, gain, scale, shift, w):
    g, r, d = x.shape
    n = w.shape[1]
    tm = min(ROW_TILE, r)
    assert r % tm == 0
    return pl.pallas_call(
        _in_proj_kernel,
        out_shape=jax.ShapeDtypeStruct((g, r, n), F32),
        grid=(g, r // tm),
        in_specs=[pl.BlockSpec((None, tm, d), lambda b, i: (b, i, 0)),
                  pl.BlockSpec((1, d), lambda b, i: (0, 0)),
                  pl.BlockSpec((None, 1, d), lambda b, i: (b, 0, 0)),
                  pl.BlockSpec((None, 1, d), lambda b, i: (b, 0, 0)),
                  pl.BlockSpec((d, n), lambda b, i: (0, 0))],
        out_specs=pl.BlockSpec((None, tm, n), lambda b, i: (b, i, 0)),
        compiler_params=pltpu.CompilerParams(dimension_semantics=("arbitrary", "arbitrary"),
                                             vmem_limit_bytes=VMEM_LIMIT),
        name="input_projection",
    )(x, gain.reshape(1, d), scale, shift, w)


def _rms(x, g):
    return x * lax.rsqrt(jnp.mean(x * x, axis=-1, keepdims=True) + EPS) * g


def _head_rms_tokens(o, g, nh):
    b, t, w = o.shape
    oh = o.reshape(b, t, nh, w // nh)
    oh = oh * lax.rsqrt(jnp.mean(oh * oh, axis=-1, keepdims=True) + EPS)
    return oh.reshape(b, t, w) * g


def _short_conv_kernel(prev_ref, cur_ref, next_ref, w_ref, b_ref, o_ref, *, act):
    j = pl.program_id(1)
    u = cur_ref[...]
    tt = u.shape[0]
    row = lax.broadcasted_iota(jnp.int32, u.shape, 0)
    before = jnp.where(j > 0, prev_ref[SUBLANE - 1:SUBLANE, :], 0.0)
    after = jnp.where(j < pl.num_programs(1) - 1, next_ref[0:1, :], 0.0)
    up = jnp.where(row == 0, before, pltpu.roll(u, 1, 0))
    dn = jnp.where(row == tt - 1, after, pltpu.roll(u, tt - 1, 0))
    y = w_ref[0:1] * up + w_ref[1:2] * u + w_ref[2:3] * dn + b_ref[...]
    if act:
        y = y * jax.nn.sigmoid(y)
    o_ref[...] = y


def _short_conv(p3, col0, ncols, w, b, act):
    assert SHORT_CONV == 3
    bsz, t, _ = p3.shape
    tt = min(ROW_TILE, t)
    halo = tt // SUBLANE
    last = t // SUBLANE - 1
    gw = GROUP_W
    cur = pl.BlockSpec((None, tt, gw), lambda bi, j, c: (bi, j, col0 + c))
    prev = pl.BlockSpec((None, SUBLANE, gw), lambda bi, j, c: (bi, jnp.maximum(j * halo - 1, 0), col0 + c))
    nxt = pl.BlockSpec((None, SUBLANE, gw), lambda bi, j, c: (bi, jnp.minimum((j + 1) * halo, last), col0 + c))
    return pl.pallas_call(
        functools.partial(_short_conv_kernel, act=act),
        out_shape=jax.ShapeDtypeStruct((bsz, t, ncols * gw), F32),
        grid=(bsz, t // tt, ncols),
        in_specs=[prev, cur, nxt, pl.BlockSpec((SHORT_CONV, gw), lambda bi, j, c: (0, c)),
                  pl.BlockSpec((1, gw), lambda bi, j, c: (0, c))],
        out_specs=pl.BlockSpec((None, tt, gw), lambda bi, j, c: (bi, j, c)),
        compiler_params=pltpu.CompilerParams(dimension_semantics=("arbitrary",) * 3, vmem_limit_bytes=VMEM_LIMIT),
        name="short_conv",
    )(p3, p3, p3, w, b.reshape(1, ncols * gw))


def _pos_embed_2d(rows, d):
    r = jnp.repeat(jnp.arange(rows, dtype=F32), GRID_W)
    col = (jnp.arange(rows * GRID_W) % GRID_W).astype(F32)
    quarter = d // 4
    omega = 1.0 / (POS_BASE ** (jnp.arange(quarter, dtype=F32) / quarter))

    def axis_emb(p):
        ang = p[:, None] * omega[None, :]
        return jnp.concatenate([jnp.sin(ang), jnp.cos(ang)], axis=-1)

    return jnp.concatenate([axis_emb(r), axis_emb(col)], axis=-1)


def _hyena_spectra(L, w1, b1, w2, b2, w3, freq):
    t = jnp.linspace(0.0, 1.0, L, dtype=F32)[:, None]
    w = 2.0 * math.pi * jnp.arange(L, dtype=F32)[:, None] / L
    bands = jnp.linspace(1e-4, HY_BANDS - 1, HY_BANDS, dtype=F32)[None, :]
    feats = jnp.concatenate([t, jnp.cos(bands * w), -jnp.sin(bands * w)], axis=-1)
    z = jnp.sin(freq[0] * (feats @ w1 + b1))
    z = jnp.sin(freq[1] * (z @ w2 + b2))
    h = (z @ w3).reshape(L, HY_ORDER, 2, HY_W)
    max_decay = math.log(HY_DECAY_TARGET) / HY_FAST_DECAY
    min_decay = math.log(HY_DECAY_TARGET) / HY_SLOW_DECAY
    deltas = jnp.abs(jnp.linspace(min_decay, max_decay, HY_W, dtype=F32))
    h = h * jnp.exp(-t[:, :, None, None] * deltas)
    fwd = h[:, :, 0]
    bwd = h[1:, :, 1][::-1]
    l1 = jnp.sum(jnp.abs(fwd), axis=0) + jnp.sum(jnp.abs(bwd), axis=0)
    filt = jnp.concatenate([fwd, jnp.zeros((1, HY_ORDER, HY_W), F32), bwd], axis=0) / l1
    return jnp.fft.rfft(filt, axis=0)


def _fft_conv(u, spec, bias):
    L = u.shape[1]
    y = jnp.fft.irfft(jnp.fft.rfft(u, n=2 * L, axis=1) * spec, n=2 * L, axis=1)[:, :L]
    return y + u * bias


def _hyena(p3, conv_w, conv_b, w1, b1, w2, b2, w3, freq, bias, norm_g):
    u = _short_conv(p3, COL_HY, 3, conv_w, conv_b, False)
    v, x1, x2 = u[..., :HY_W], u[..., HY_W:2 * HY_W], u[..., 2 * HY_W:]
    spec = _hyena_spectra(u.shape[1], w1, b1, w2, b2, w3, freq)
    z = x1 * _fft_conv(v, spec[:, 0], bias[0])
    y = x2 * _fft_conv(z, spec[:, 1], bias[1])
    return _rms(y, norm_g)


FFT_N1 = 128
FFT_N2 = 128
FFT_KTILE = 8
FFT_NTILE = 4096


def _dft_tables(n1, n2):
    n = n1 * n2
    k = np.arange(n1)
    f1 = np.exp(-2j * np.pi * np.outer(k, k) / n1)
    f2 = np.exp(-2j * np.pi * np.outer(np.arange(n2), np.arange(n2)) / n2)
    tw = np.exp(-2j * np.pi * np.outer(np.arange(n1), np.arange(n2)) / n)
    as32 = lambda a: jnp.asarray(np.ascontiguousarray(a), F32)
    f1_fwd = as32(np.concatenate([f1.real, f1.imag], axis=0))
    f1_inv = as32(np.concatenate([f1.real, f1.imag], axis=1) / n)
    f2_inv = as32(np.block([[f2.real, f2.imag], [-f2.imag, f2.real]]))
    return f1_fwd, f1_inv, as32(f2.real), as32(f2.imag), f2_inv, as32(tw.real), as32(tw.imag)


def _stage_kernel(w_ref, x_ref, o_ref):
    o_ref[...] = jnp.dot(w_ref[...].astype(BF16), x_ref[...].astype(BF16),
                         preferred_element_type=F32).astype(o_ref.dtype)


def _stage_matmul(w, x, out_dtype):
    g, k, n = x.shape
    m = w.shape[0]
    tn = FFT_NTILE
    return pl.pallas_call(
        _stage_kernel,
        out_shape=jax.ShapeDtypeStruct((g, m, n), out_dtype),
        grid=(g, n // tn),
        in_specs=[pl.BlockSpec((m, k), lambda b, j: (0, 0)), pl.BlockSpec((None, k, tn), lambda b, j: (b, 0, j))],
        out_specs=pl.BlockSpec((None, m, tn), lambda b, j: (b, 0, j)),
        compiler_params=pltpu.CompilerParams(dimension_semantics=("arbitrary", "arbitrary"),
                                             vmem_limit_bytes=VMEM_LIMIT),
        name="dft_stage",
    )(w, x)


def _twiddled_f2(f2r, f2i, tr, ti):
    gr = f2r * tr - f2i * ti
    gi = f2r * ti + f2i * tr
    return jnp.concatenate([jnp.concatenate([gr, -gi], axis=1), jnp.concatenate([gi, gr], axis=1)], axis=0)


def _spectrum_kernel(a_ref, f2r_ref, f2i_ref, tr_ref, ti_ref, x_ref):
    n2 = f2r_ref.shape[0]
    tr = tr_ref[...]
    ti = ti_ref[...]
    for i in range(a_ref.shape[0]):
        g = _twiddled_f2(f2r_ref[...], f2i_ref[...], tr[i:i + 1], ti[i:i + 1])
        x_ref[i] = jnp.dot(g.astype(BF16), a_ref[i], preferred_element_type=F32)


def _conv_mid_kernel(a_ref, h_ref, f2r_ref, f2i_ref, f2inv_ref, tr_ref, ti_ref, z_ref):
    n2 = f2r_ref.shape[0]
    tr = tr_ref[...]
    ti = ti_ref[...]
    tr_col = tr.T
    ti_col = ti.T
    f2inv = f2inv_ref[...].astype(BF16)
    for i in range(a_ref.shape[0]):
        g = _twiddled_f2(f2r_ref[...], f2i_ref[...], tr[i:i + 1], ti[i:i + 1])
        x = jnp.dot(g.astype(BF16), a_ref[i], preferred_element_type=F32)
        xr, xi = x[:n2], x[n2:]
        hr, hi = h_ref[i, :n2], h_ref[i, n2:]
        y = jnp.concatenate([hr * xr - hi * xi, hr * xi + hi * xr], axis=0)
        w = jnp.dot(f2inv, y.astype(BF16), preferred_element_type=F32)
        wr, wi = w[:n2], w[n2:]
        cr, ci = tr_col[:, i:i + 1], ti_col[:, i:i + 1]
        z_ref[i] = jnp.concatenate([cr * wr + ci * wi, cr * wi - ci * wr], axis=0).astype(z_ref.dtype)


def _dft_mid_specs(g, c):
    n1, n2, kt = FFT_N1, FFT_N2, FFT_KTILE
    blk = pl.BlockSpec((None, kt, 2 * n2, c), lambda b, j: (b, j, 0, 0))
    const = lambda r, cc: pl.BlockSpec((r, cc), lambda b, j: (0, 0))
    twid = pl.BlockSpec((kt, n2), lambda b, j: (j, 0))
    params = pltpu.CompilerParams(dimension_semantics=("arbitrary", "arbitrary"), vmem_limit_bytes=VMEM_LIMIT)
    return blk, const, twid, params, (g, n1 // kt)


def _to_k1_major(a2d, c):
    g = a2d.shape[0]
    return a2d.reshape(g, 2, FFT_N1, FFT_N2, c).transpose(0, 2, 1, 3, 4).reshape(g, FFT_N1, 2 * FFT_N2, c)


def _filter_spectrum(filt, tables):
    g, n, c = filt.shape
    f1_fwd, _, f2r, f2i, _, twr, twi = tables
    a = _stage_matmul(f1_fwd, filt.reshape(g, FFT_N1, FFT_N2 * c), BF16)
    blk, const, twid, params, grid = _dft_mid_specs(g, c)
    return pl.pallas_call(
        _spectrum_kernel,
        out_shape=jax.ShapeDtypeStruct((g, FFT_N1, 2 * FFT_N2, c), F32),
        grid=grid,
        in_specs=[blk, const(FFT_N2, FFT_N2), const(FFT_N2, FFT_N2), twid, twid],
        out_specs=blk, compiler_params=params, name="dft_spectrum",
    )(_to_k1_major(a, c), f2r, f2i, twr, twi)


def _long_conv(u, spec, tables):
    g, l, c = u.shape
    f1_fwd, f1_inv, f2r, f2i, f2inv, twr, twi = tables
    half = l // FFT_N2
    a = _stage_matmul(f1_fwd[:, :half], u.reshape(g, half, FFT_N2 * c), BF16)
    blk, const, twid, params, grid = _dft_mid_specs(g, c)
    hspec = pl.BlockSpec((FFT_KTILE, 2 * FFT_N2, c), lambda b, j: (j, 0, 0))
    z = pl.pallas_call(
        _conv_mid_kernel,
        out_shape=jax.ShapeDtypeStruct((g, FFT_N1, 2 * FFT_N2, c), BF16),
        grid=grid,
        in_specs=[blk, hspec, const(FFT_N2, FFT_N2), const(FFT_N2, FFT_N2), const(2 * FFT_N2, 2 * FFT_N2),
                  twid, twid],
        out_specs=blk, compiler_params=params, name="dft_conv_mid",
    )(_to_k1_major(a, c), spec, f2r, f2i, f2inv, twr, twi)
    z2d = z.reshape(g, FFT_N1, 2, FFT_N2, c).transpose(0, 2, 1, 3, 4).reshape(g, 2 * FFT_N1, FFT_N2 * c)
    y = _stage_matmul(f1_inv[:half], z2d, F32)
    return y.reshape(g, l, c)


def _hyena_filters(L, w1, b1, w2, b2, w3, freq):
    t = jnp.linspace(0.0, 1.0, L, dtype=F32)[:, None]
    w = 2.0 * math.pi * jnp.arange(L, dtype=F32)[:, None] / L
    bands = jnp.linspace(1e-4, HY_BANDS - 1, HY_BANDS, dtype=F32)[None, :]
    feats = jnp.concatenate([t, jnp.cos(bands * w), -jnp.sin(bands * w)], axis=-1)
    max_decay = math.log(HY_DECAY_TARGET) / HY_FAST_DECAY
    min_decay = math.log(HY_DECAY_TARGET) / HY_SLOW_DECAY
    deltas = jnp.abs(jnp.linspace(min_decay, max_decay, HY_W, dtype=F32))
    w3d = w3.reshape(w3.shape[0], HY_ORDER, 2, HY_W)

    def side(f, tt, direction):
        z = jnp.sin(freq[0] * (f @ w1 + b1))
        z = jnp.sin(freq[1] * (z @ w2 + b2))
        h = (z @ w3d[:, :, direction].reshape(w3.shape[0], HY_ORDER * HY_W)).reshape(-1, HY_ORDER, HY_W)
        return h * jnp.exp(-tt[:, :, None] * deltas)

    fwd = side(feats, t, 0)
    bwd = side(feats[::-1], t[::-1], 1)[:L - 1]
    l1 = jnp.sum(jnp.abs(fwd), axis=0) + jnp.sum(jnp.abs(bwd), axis=0)
    return jnp.concatenate([fwd, jnp.zeros((1, HY_ORDER, HY_W), F32), bwd], axis=0) / l1


def _hyena_long(p3, conv_w, conv_b, w1, b1, w2, b2, w3, freq, bias, norm_g):
    u = _short_conv(p3, COL_HY, 3, conv_w, conv_b, False)
    v, x1, x2 = u[..., :HY_W], u[..., HY_W:2 * HY_W], u[..., 2 * HY_W:]
    L = u.shape[1]
    assert 2 * L == FFT_N1 * FFT_N2
    tables = _dft_tables(FFT_N1, FFT_N2)
    filt = _hyena_filters(L, w1, b1, w2, b2, w3, freq)
    spec = _filter_spectrum(jnp.moveaxis(filt, 1, 0), tables)
    z = x1 * (_long_conv(v, spec[0], tables) + v * bias[0])
    y = x2 * (_long_conv(z, spec[1], tables) + z * bias[1])
    return _rms(y, norm_g)


def _glr_direction(q, k, v, g, st_ref, reverse, nh):
    tc, hk = k.shape
    hv = v.shape[1]
    c = CHUNK_GATED
    hi = lax.Precision.HIGHEST
    ti = lax.broadcasted_iota(jnp.int32, (tc, tc), 0)
    tj = lax.broadcasted_iota(jnp.int32, (tc, tc), 1)
    same = (ti // c) == (tj // c)
    seen = (tj >= ti) if reverse else (tj <= ti)
    bcum = jnp.dot(jnp.where(same, jnp.where(seen, 1.0, 0.0), 0.0), g, precision=hi, preferred_element_type=F32)
    btot = jnp.dot(jnp.where(same, 1.0, 0.0), g, precision=hi, preferred_element_type=F32)
    qd = q * jnp.exp(bcum)
    kd = k * jnp.exp(btot - bcum)
    dec = jnp.exp(btot)
    head_sum = jnp.where(lax.broadcasted_iota(jnp.int32, (hk, hv), 0) // (hk // nh)
                         == lax.broadcasted_iota(jnp.int32, (hk, hv), 1) // (hv // nh), 1.0, 0.0).astype(BF16)
    in_chunk = lax.broadcasted_iota(jnp.int32, (tc, hk), 0) % c
    o = jnp.zeros((tc, hv), F32)
    for lag in range(c):
        if lag == 0:
            ks, bs, vs = k, bcum, v
        else:
            shift = tc - lag if reverse else lag
            ks, bs, vs = pltpu.roll(k, shift, 0), pltpu.roll(bcum, shift, 0), pltpu.roll(v, shift, 0)
        valid = (in_chunk + lag <= c - 1) if reverse else (in_chunk >= lag)
        x = q * ks * jnp.exp(jnp.where(valid, bcum - bs, NEG_INF))
        o = o + jnp.dot(x.astype(BF16), head_sum, preferred_element_type=F32) * vs
    head_mask = (lax.broadcasted_iota(jnp.int32, (hv, hk), 0) // (hv // nh)
                 == lax.broadcasted_iota(jnp.int32, (hv, hk), 1) // (hk // nh))
    st = st_ref[...]
    nch = tc // c
    outs = [None] * nch
    for ci in (range(nch - 1, -1, -1) if reverse else range(nch)):
        sl = slice(ci * c, (ci + 1) * c)
        outs[ci] = lax.dot_general(qd[sl].astype(BF16), st.astype(BF16), (((1,), (1,)), ((), ())),
                                   preferred_element_type=F32)
        ds = lax.dot_general(v[sl].astype(BF16), kd[sl].astype(BF16), (((0,), (0,)), ((), ())),
                             preferred_element_type=F32)
        st = st * dec[ci * c:ci * c + 1] + jnp.where(head_mask, ds, 0.0)
    st_ref[...] = st
    return o + jnp.concatenate(outs, axis=0)


def _log_sigmoid(z):
    return jnp.minimum(z, 0.0) - jnp.log1p(jnp.exp(-jnp.abs(z)))


def _hgrn2_kernel(qf_ref, if_ref, zf_ref, qb_ref, ib_ref, zb_ref, lb_ref, s0f_ref, s0b_ref,
                  of_ref, ob_ref, sf_ref, sb_ref, stf, stb):
    j = pl.program_id(1)

    @pl.when(j == 0)
    def _():
        stf[...] = s0f_ref[...]
        stb[...] = s0b_ref[...]

    one_minus_lb, log_lb, log_ub = lb_ref[0:1], lb_ref[1:2], lb_ref[2:3]

    def gate(z):
        return one_minus_lb * jax.nn.sigmoid(-z), jnp.logaddexp(log_lb, log_ub + _log_sigmoid(z))

    silu = lambda a: a * jax.nn.sigmoid(a)
    k_f, g_f = gate(zf_ref[...])
    k_b, g_b = gate(zb_ref[...])
    of_ref[...] = _glr_direction(silu(qf_ref[...]), k_f, if_ref[...], g_f, stf, False, HG_H)
    ob_ref[...] = _glr_direction(silu(qb_ref[...]), k_b, ib_ref[...], g_b, stb, True, HG_H)

    @pl.when(j == pl.num_programs(1) - 1)
    def _():
        sf_ref[...] = stf[...]
        sb_ref[...] = stb[...]


def _gla_kernel(qkf_ref, vf_ref, nf_ref, qkb_ref, vb_ref, nb_ref, aup_ref, ab_ref, s0f_ref, s0b_ref,
                of_ref, ob_ref, sf_ref, sb_ref, stf, stb):
    j = pl.program_id(1)

    @pl.when(j == 0)
    def _():
        stf[...] = s0f_ref[...]
        stb[...] = s0b_ref[...]

    def gate(narrow, idx):
        a = narrow[:, idx * GLA_RANK:(idx + 1) * GLA_RANK]
        lin = jnp.dot(a.astype(BF16), aup_ref[idx].astype(BF16), preferred_element_type=F32) + ab_ref[idx]
        return _log_sigmoid(lin) / GLA_NORMALIZER

    qk_f = qkf_ref[...]
    qk_b = qkb_ref[...]
    of_ref[...] = _glr_direction(qk_f[:, :GLA_KW] * GLA_DK ** -0.5, qk_f[:, GLA_KW:], vf_ref[...],
                                 gate(nf_ref[...], 0), stf, False, GLA_H)
    ob_ref[...] = _glr_direction(qk_b[:, :GLA_KW] * GLA_DK ** -0.5, qk_b[:, GLA_KW:], vb_ref[...],
                                 gate(nb_ref[...], 1), stb, True, GLA_H)

    @pl.when(j == pl.num_programs(1) - 1)
    def _():
        sf_ref[...] = stf[...]
        sb_ref[...] = stb[...]


def _glr_call(kernel_fn, p3, fwd_cols, bwd_cols, consts, s0_f, s0_b, hk, hv, name):
    bsz, t, _ = p3.shape
    tc = GLR_TILE
    nsb = t // tc
    fwd = lambda w, c: pl.BlockSpec((None, tc, w), lambda b, j: (b, j, c))
    bwd = lambda w, c: pl.BlockSpec((None, tc, w), lambda b, j: (b, nsb - 1 - j, c))
    whole = lambda a: pl.BlockSpec(a.shape, lambda b, j: (0,) * a.ndim)
    st = pl.BlockSpec((None, hv, hk), lambda b, j: (b, 0, 0))
    out_f = pl.BlockSpec((None, tc, hv), lambda b, j: (b, j, 0))
    out_b = pl.BlockSpec((None, tc, hv), lambda b, j: (b, nsb - 1 - j, 0))
    return pl.pallas_call(
        kernel_fn,
        out_shape=[jax.ShapeDtypeStruct((bsz, t, hv), F32), jax.ShapeDtypeStruct((bsz, t, hv), F32),
                   jax.ShapeDtypeStruct((bsz, hv, hk), F32), jax.ShapeDtypeStruct((bsz, hv, hk), F32)],
        grid=(bsz, nsb),
        in_specs=([fwd(w, c) for w, c in fwd_cols] + [bwd(w, c) for w, c in bwd_cols]
                  + [whole(a) for a in consts] + [st, st]),
        out_specs=[out_f, out_b, st, st],
        scratch_shapes=[pltpu.VMEM((hv, hk), F32), pltpu.VMEM((hv, hk), F32)],
        compiler_params=pltpu.CompilerParams(dimension_semantics=("arbitrary", "arbitrary"),
                                             vmem_limit_bytes=VMEM_LIMIT),
        name=name,
    )(*([p3] * (len(fwd_cols) + len(bwd_cols))), *consts, s0_f, s0_b)


def _hgrn2_seq(p3, lb, norm_g, s0_f, s0_b, with_out):
    lb_rows = jnp.stack([1.0 - lb, jnp.log(lb), jnp.log1p(-lb)])
    cols = lambda z: [(GROUP_W, COL_HG_Q), (GROUP_W, COL_HG_I), (GROUP_W, z)]
    o_f, o_b, s_f, s_b = _glr_call(_hgrn2_kernel, p3, cols(COL_HG_ZF), cols(COL_HG_ZB), [lb_rows],
                                   s0_f, s0_b, HG_W, HG_W, "hgrn2_recurrence")
    if not with_out:
        return None, s_f, s_b
    gate = p3[..., COL_HG_G * GROUP_W:(COL_HG_G + 1) * GROUP_W]
    return _head_rms_tokens(o_f + o_b, norm_g, HG_H) * jax.nn.silu(gate), s_f, s_b


def _hgrn2(pc3, pl3, lb, norm_g, with_ctx):
    s0 = jnp.zeros((pl3.shape[0], HG_W, HG_W), F32)
    yc, s_f, s_b = _hgrn2_seq(pc3, lb, norm_g, s0, s0, with_ctx)
    y, _, _ = _hgrn2_seq(pl3, lb, norm_g, s_f, s_b, True)
    return yc, y


def _gla_seq(p3, a_up, a_b, norm_g, s0_f, s0_b, with_out):
    cols = [(GROUP_W, COL_GLA_QK), (GROUP_W, COL_GLA_V), (LANE, COL_NARROW)]
    o_f, o_b, s_f, s_b = _glr_call(_gla_kernel, p3, cols, cols, [a_up, a_b.reshape(2, 1, GLA_KW)],
                                   s0_f, s0_b, GLA_KW, GLA_VW, "gla_recurrence")
    if not with_out:
        return None, s_f, s_b
    gate = p3[..., COL_GLA_R * GROUP_W:(COL_GLA_R + 1) * GROUP_W]
    return _head_rms_tokens(o_f + o_b, norm_g, GLA_H) * jax.nn.silu(gate), s_f, s_b


def _gla(pc3, pl3, a_up, a_b, norm_g, with_ctx):
    s0 = jnp.zeros((pl3.shape[0], GLA_VW, GLA_KW), F32)
    yc, s_f, s_b = _gla_seq(pc3, a_up, a_b, norm_g, s0, s0, with_ctx)
    y, _, _ = _gla_seq(pl3, a_up, a_b, norm_g, s_f, s_b, True)
    return yc, y


def _mlstm_direction(q, k, v, igx, lfx, s_ref, n_ref, m_ref, reverse, nh):
    tc, w = q.shape
    seg = w // nh
    assert tc == seg
    hi = lax.Precision.HIGHEST
    ti = lax.broadcasted_iota(jnp.int32, (tc, tc), 0)
    tj = lax.broadcasted_iota(jnp.int32, (tc, tc), 1)
    seen = (tj >= ti) if reverse else (tj <= ti)
    b = jnp.dot(jnp.where(seen, 1.0, 0.0), lfx, precision=hi, preferred_element_type=F32)
    bl = b[0:1] if reverse else b[tc - 1:tc]
    a = bl - b + igx
    ma = jnp.max(a, axis=0, keepdims=True)
    kw = jnp.exp(a - ma) * k
    s_prev = s_ref[...]
    n_prev = n_ref[...]
    m_prev = m_ref[...]
    lane = lax.broadcasted_iota(jnp.int32, (tc, w), 1)
    row = lax.broadcasted_iota(jnp.int32, (tc, w), 0)
    same_head = (lax.broadcasted_iota(jnp.int32, (w, w), 0) // seg
                 == lax.broadcasted_iota(jnp.int32, (w, w), 1) // seg)
    kexp = jnp.where(same_head, jnp.concatenate([k] * nh, axis=0), 0.0)
    vexp = jnp.where(same_head, jnp.concatenate([v] * nh, axis=0), 0.0)
    scores = lax.dot_general(q.astype(BF16), kexp.astype(BF16), (((1,), (1,)), ((), ())),
                             preferred_element_type=F32)
    s_lane = lane % seg
    by_src = jnp.sum(jnp.where(s_lane == row, igx - b, 0.0), axis=0, keepdims=True)
    ok = (s_lane >= row) if reverse else (s_lane <= row)
    dmat = jnp.where(ok, b + by_src, NEG_INF)
    inter = b + m_prev
    head_of_lane = lane // seg
    seg_max = jnp.full((tc, w), NEG_INF, F32)
    for h in range(nh):
        in_h = head_of_lane == h
        seg_max = jnp.where(in_h, jnp.max(jnp.where(in_h, dmat, NEG_INF), axis=1, keepdims=True), seg_max)
    m_t = jnp.maximum(inter, seg_max)
    wq = jnp.exp(dmat - m_t) * scores
    w_int = jnp.exp(inter - m_t)
    head_sum = jnp.where(same_head, 1.0, 0.0).astype(BF16)
    num = (jnp.dot(wq.astype(BF16), vexp.astype(BF16), preferred_element_type=F32)
           + w_int * jnp.dot(q.astype(BF16), s_prev.astype(BF16), preferred_element_type=F32))
    den = (jnp.dot(wq.astype(BF16), head_sum, preferred_element_type=F32)
           + w_int * jnp.dot((q * n_prev).astype(BF16), head_sum, preferred_element_type=F32))
    h_out = num / jnp.maximum(jnp.abs(den), jnp.exp(-m_t))
    m_new = jnp.maximum(bl + m_prev, ma)
    d_old = jnp.exp(bl + m_prev - m_new)
    d_new = jnp.exp(ma - m_new)
    ds = lax.dot_general(kw.astype(BF16), v.astype(BF16), (((0,), (0,)), ((), ())), preferred_element_type=F32)
    s_ref[...] = d_old * s_prev + d_new * jnp.where(same_head, ds, 0.0)
    n_ref[...] = d_old * n_prev + d_new * jnp.sum(kw, axis=0, keepdims=True)
    m_ref[...] = m_new
    return h_out


def _mlstm_kernel(qf_ref, kf_ref, vf_ref, igf_ref, lff_ref, qb_ref, kb_ref, vb_ref, igb_ref, lfb_ref,
                  s0f_ref, n0f_ref, m0f_ref, s0b_ref, n0b_ref, m0b_ref,
                  hf_ref, hb_ref, sf_ref, nf_ref, mf_ref, sb_ref, nb_ref, mb_ref,
                  s_f, n_f, m_f, s_b, n_b, m_b, *, nh):
    j = pl.program_id(1)

    @pl.when(j == 0)
    def _():
        s_f[...] = s0f_ref[...]
        n_f[...] = n0f_ref[...]
        m_f[...] = m0f_ref[...]
        s_b[...] = s0b_ref[...]
        n_b[...] = n0b_ref[...]
        m_b[...] = m0b_ref[...]

    hf_ref[...] = _mlstm_direction(qf_ref[...], kf_ref[...], vf_ref[...], igf_ref[...], lff_ref[...],
                                   s_f, n_f, m_f, False, nh)
    hb_ref[...] = _mlstm_direction(qb_ref[...], kb_ref[...], vb_ref[...], igb_ref[...], lfb_ref[...],
                                   s_b, n_b, m_b, True, nh)

    @pl.when(j == pl.num_programs(1) - 1)
    def _():
        sf_ref[...] = s_f[...]
        nf_ref[...] = n_f[...]
        mf_ref[...] = m_f[...]
        sb_ref[...] = s_b[...]
        nb_ref[...] = n_b[...]
        mb_ref[...] = m_b[...]


def _mlstm_bidir(q, k, v, ig_f, lf_f, ig_b, lf_b, st_f, st_b, nh):
    bsz, t, w = q.shape
    tc = CHUNK_ML
    nsb = t // tc
    fwd = pl.BlockSpec((None, tc, w), lambda b, j: (b, j, 0))
    bwd = pl.BlockSpec((None, tc, w), lambda b, j: (b, nsb - 1 - j, 0))
    mat = pl.BlockSpec((None, w, w), lambda b, j: (b, 0, 0))
    vec = pl.BlockSpec((None, 1, w), lambda b, j: (b, 0, 0))
    sds = jax.ShapeDtypeStruct
    state_shapes = [sds((bsz, w, w), F32), sds((bsz, 1, w), F32), sds((bsz, 1, w), F32)]
    outs = pl.pallas_call(
        functools.partial(_mlstm_kernel, nh=nh),
        out_shape=[sds((bsz, t, w), F32), sds((bsz, t, w), F32)] + state_shapes + state_shapes,
        grid=(bsz, nsb),
        in_specs=[fwd] * 5 + [bwd] * 5 + [mat, vec, vec] * 2,
        out_specs=[fwd, bwd] + [mat, vec, vec] * 2,
        scratch_shapes=[pltpu.VMEM((w, w), F32), pltpu.VMEM((1, w), F32), pltpu.VMEM((1, w), F32)] * 2,
        compiler_params=pltpu.CompilerParams(dimension_semantics=("arbitrary", "arbitrary"),
                                             vmem_limit_bytes=VMEM_LIMIT),
        name="mlstm_recurrence",
    )(q, k, v, ig_f, lf_f, q, k, v, ig_b, lf_b, *st_f, *st_b)
    return outs[0], outs[1], tuple(outs[2:5]), tuple(outs[5:8])


def _mlstm_seq(p3, conv_w, conv_b, gate_b, norm_g, st_f, st_b, with_out):
    part = lambda c: p3[..., c * GROUP_W:(c + 1) * GROUP_W]
    qk = _short_conv(p3, COL_ML_Q, 2, conv_w, conv_b, True)
    v, o = part(COL_ML_V), part(COL_ML_O)
    bsz, t, _ = p3.shape
    gates = p3[..., NARROW_ML_GATES:NARROW_ML_GATES + 4 * ML_H]
    gt = gates.reshape(bsz, t, 4, ML_H) + gate_b
    expand = lambda a: jnp.repeat(a, ML_DH, axis=-1)
    h_f, h_b, fin_f, fin_b = _mlstm_bidir(
        qk[..., :ML_W], qk[..., ML_W:] * ML_DH ** -0.5, v,
        expand(gt[:, :, 0]), expand(jax.nn.log_sigmoid(gt[:, :, 1])),
        expand(gt[:, :, 2]), expand(jax.nn.log_sigmoid(gt[:, :, 3])), st_f, st_b, ML_H)
    y = jax.nn.sigmoid(o) * _head_rms_tokens(h_f + h_b, norm_g, ML_H) if with_out else None
    return y, fin_f, fin_b


def _mlstm(pc3, pl3, conv_w, conv_b, gate_b, norm_g, with_ctx):
    bsz = pl3.shape[0]
    st0 = (jnp.zeros((bsz, ML_W, ML_W), F32), jnp.zeros((bsz, 1, ML_W), F32), jnp.zeros((bsz, 1, ML_W), F32))
    yc, st_f, st_b = _mlstm_seq(pc3, conv_w, conv_b, gate_b, norm_g, st0, st0, with_ctx)
    y, _, _ = _mlstm_seq(pl3, conv_w, conv_b, gate_b, norm_g, st_f, st_b, True)
    return yc, y


def _router_kernel(h_ref, wt_ref, b_ref, eidx_ref, wsel_ref, cnt_ref):
    i = pl.program_id(0)
    tm = h_ref.shape[0]
    ne = wt_ref.shape[0]
    per_group = ne // N_EXPERT_GROUPS
    logits = lax.dot_general(wt_ref[...], h_ref[...], (((1,), (1,)), ((), ())),
                             preferred_element_type=F32, precision=lax.Precision.HIGHEST)
    s = jax.nn.sigmoid(logits)
    sel = s + b_ref[...]
    row = lax.broadcasted_iota(jnp.int32, (ne, tm), 0)
    gs = []
    for g in range(N_EXPERT_GROUPS):
        blk = sel[g * per_group:(g + 1) * per_group]
        r = lax.broadcasted_iota(jnp.int32, blk.shape, 0)
        m1 = jnp.max(blk, axis=0, keepdims=True)
        i1 = jnp.min(jnp.where(blk == m1, r, per_group), axis=0, keepdims=True)
        m2 = jnp.max(jnp.where(r == i1, NEG_INF, blk), axis=0, keepdims=True)
        gs.append(m1 + m2)
    grp = jnp.concatenate(gs, axis=0)
    grow = lax.broadcasted_iota(jnp.int32, grp.shape, 0)
    gsel = jnp.zeros(grp.shape, F32)
    for _ in range(TOPK_GROUPS):
        m = jnp.max(grp, axis=0, keepdims=True)
        gi = jnp.min(jnp.where(grp == m, grow, N_EXPERT_GROUPS), axis=0, keepdims=True)
        hit = grow == gi
        gsel = jnp.where(hit, 1.0, gsel)
        grp = jnp.where(hit, NEG_INF, grp)
    masked = jnp.concatenate(
        [jnp.where(gsel[g:g + 1] > 0.0, sel[g * per_group:(g + 1) * per_group], NEG_INF)
         for g in range(N_EXPERT_GROUPS)], axis=0)
    eis, ws = [], []
    picked = jnp.zeros((ne, tm), F32)
    for _ in range(TOP_K):
        m = jnp.max(masked, axis=0, keepdims=True)
        ei = jnp.min(jnp.where(masked == m, row, ne), axis=0, keepdims=True)
        hit = row == ei
        ws.append(jnp.sum(jnp.where(hit, s, 0.0), axis=0, keepdims=True))
        eis.append(ei)
        picked = jnp.where(hit, 1.0, picked)
        masked = jnp.where(hit, NEG_INF, masked)
    w = jnp.concatenate(ws, axis=0)
    eidx_ref[...] = jnp.concatenate(eis, axis=0)
    wsel_ref[...] = w / jnp.sum(w, axis=0, keepdims=True) * ROUTED_SCALE
    tot = jnp.dot(picked.astype(BF16), jnp.ones((tm, LANE), BF16), preferred_element_type=F32)

    @pl.when(i == 0)
    def _():
        cnt_ref[...] = jnp.zeros_like(cnt_ref)

    cnt_ref[...] += tot


def _pos_kernel(eidx_ref, base_ref, pos_ref, carry_ref):
    i = pl.program_id(0)
    tm = eidx_ref.shape[1]
    ne = base_ref.shape[0]

    @pl.when(i == 0)
    def _():
        carry_ref[...] = jnp.zeros_like(carry_ref)

    eidx = eidx_ref[...]
    row = lax.broadcasted_iota(jnp.int32, (ne, tm), 0)
    picked = jnp.zeros((ne, tm), F32)
    for k in range(TOP_K):
        picked = jnp.where(row == eidx[k:k + 1], 1.0, picked)
    pb = picked.astype(BF16)
    before = jnp.where(lax.broadcasted_iota(jnp.int32, (tm, tm), 0) < lax.broadcasted_iota(jnp.int32, (tm, tm), 1),
                       1.0, 0.0).astype(BF16)
    rank = jnp.dot(pb, before, preferred_element_type=F32)
    tot = jnp.dot(pb, jnp.ones((tm, LANE), BF16), preferred_element_type=F32)
    dest = rank + (base_ref[...] + carry_ref[:, 0:1])
    pos = [jnp.sum(jnp.where(row == eidx[k:k + 1], dest, 0.0), axis=0, keepdims=True) for k in range(TOP_K)]
    pos_ref[...] = jnp.concatenate(pos, axis=0).astype(jnp.int32)
    carry_ref[...] += tot


def _dispatch_kernel(zstart_ref, zlen_ref, nused_ref, pos_ref, h_ref, xs_ref, zeros, sem, zsem, *, n_blocks):
    tm = h_ref.shape[0]
    bm = zeros.shape[0]

    @pl.when(pl.program_id(0) == 0)
    def _():
        zeros[...] = jnp.zeros_like(zeros)

        def zero_copy(start, size):
            return pltpu.make_async_copy(zeros.at[pl.ds(0, size)], xs_ref.at[pl.ds(start, size)], zsem)

        def pieces(e, act):
            start = zstart_ref[e]
            rem = zlen_ref[e]
            ragged = rem & (SUBLANE - 1)
            for q in range(SUBLANE - 1):
                @pl.when(q < ragged)
                def _(q=q):
                    act(zero_copy(start + q, 1))

            start = pl.multiple_of(start + ragged, SUBLANE)
            size = bm // 2
            while size >= SUBLANE:
                @pl.when((rem & size) != 0)
                def _(start=start, size=size):
                    act(zero_copy(start, size))

                start = pl.multiple_of(start + (rem & size), SUBLANE)
                size //= 2

        def loop(act):
            def per_expert(e, carry):
                pieces(e, act)
                return carry

            def per_block(b, carry):
                act(zero_copy(pl.multiple_of(b * bm, bm), bm))
                return carry

            lax.fori_loop(0, zstart_ref.shape[0], per_expert, 0)
            lax.fori_loop(nused_ref[0], n_blocks, per_block, 0)

        loop(lambda cp: cp.start())
        loop(lambda cp: cp.wait())

    def row_copy(n, k):
        return pltpu.make_async_copy(h_ref.at[pl.ds(n, 1)], xs_ref.at[pl.ds(pos_ref[k, n], 1)], sem)

    def issue(n, carry):
        for k in range(TOP_K):
            row_copy(n, k).start()
        return carry

    def drain(n, carry):
        for k in range(TOP_K):
            row_copy(n, k).wait()
        return carry

    lax.fori_loop(0, tm, issue, 0)
    lax.fori_loop(0, tm, drain, 0)


def _moe_ffn_kernel(blk_e_ref, nused_ref, x_ref, wgu_ref, wdn_ref, o_ref, wgu_s, wdn_s):
    i = pl.program_id(0)
    e = blk_e_ref[i]
    e_prev = blk_e_ref[jnp.maximum(i - 1, 0)]

    @pl.when(i < nused_ref[0])
    def _():
        @pl.when((i == 0) | (e != e_prev))
        def _():
            wgu_s[...] = wgu_ref[...].astype(BF16)
            wdn_s[...] = wdn_ref[...].astype(BF16)

        au = jnp.dot(x_ref[...].astype(BF16), wgu_s[...], preferred_element_type=F32)
        a = au[:, :EXPERT_FF]
        u = au[:, EXPERT_FF:]
        h = (a * jax.nn.sigmoid(a)) * u
        o_ref[...] = jnp.dot(h.astype(BF16), wdn_s[...], preferred_element_type=F32)

    @pl.when(i >= nused_ref[0])
    def _():
        o_ref[...] = jnp.zeros_like(o_ref)


def _combine_kernel(pos_ref, w_ref, y_hbm, o_ref, buf, sem):
    tm = o_ref.shape[0]

    def row_copy(n, k):
        return pltpu.make_async_copy(y_hbm.at[pl.ds(pos_ref[k, n], 1)], buf.at[k, pl.ds(n, 1)], sem)

    def issue(n, carry):
        for k in range(TOP_K):
            row_copy(n, k).start()
        return carry

    def drain(n, carry):
        for k in range(TOP_K):
            row_copy(n, k).wait()
        return carry

    lax.fori_loop(0, tm, issue, 0)
    lax.fori_loop(0, tm, drain, 0)
    acc = buf[0] * w_ref[:, 0:1]
    for k in range(1, TOP_K):
        acc = acc + buf[k] * w_ref[:, k:k + 1]
    o_ref[...] = acc


def _moe_routed(t, router_w, router_b, w_gu, w_down, layer):
    n, d = t.shape
    ne = router_w.shape[1]
    ff2 = w_gu.shape[-1]
    params = pltpu.CompilerParams(dimension_semantics=("arbitrary",), vmem_limit_bytes=VMEM_LIMIT)
    tm = ROUTER_TILE
    eidx, wsel, cnt = pl.pallas_call(
        _router_kernel,
        out_shape=[jax.ShapeDtypeStruct((TOP_K, n), jnp.int32), jax.ShapeDtypeStruct((TOP_K, n), F32),
                   jax.ShapeDtypeStruct((ne, LANE), F32)],
        grid=(n // tm,),
        in_specs=[pl.BlockSpec((tm, d), lambda i: (i, 0)), pl.BlockSpec((ne, d), lambda i: (0, 0)),
                  pl.BlockSpec((ne, 1), lambda i: (0, 0))],
        out_specs=[pl.BlockSpec((TOP_K, tm), lambda i: (0, i)), pl.BlockSpec((TOP_K, tm), lambda i: (0, i)),
                   pl.BlockSpec((ne, LANE), lambda i: (0, 0))],
        compiler_params=params, name="moe_router",
    )(t, router_w.T, router_b.reshape(ne, 1))
    bm = MOE_ROWS
    counts = cnt[:, 0].astype(jnp.int32)
    padded = (counts + bm - 1) // bm * bm
    pad_end = jnp.cumsum(padded)
    pad_start = pad_end - padded
    n_blocks = (n * TOP_K + ne * (bm - 1)) // bm + 1
    blk_first = jnp.arange(n_blocks, dtype=jnp.int32) * bm
    blk_e = jnp.minimum(jnp.sum((pad_end[None, :] <= blk_first[:, None]).astype(jnp.int32), axis=1), ne - 1)
    n_used = (pad_end[-1] // bm).astype(jnp.int32).reshape(1)
    pos = pl.pallas_call(
        _pos_kernel,
        out_shape=jax.ShapeDtypeStruct((TOP_K, n), jnp.int32),
        grid=(n // tm,),
        in_specs=[pl.BlockSpec((TOP_K, tm), lambda i: (0, i)), pl.BlockSpec((ne, 1), lambda i: (0, 0))],
        out_specs=pl.BlockSpec((TOP_K, tm), lambda i: (0, i)),
        scratch_shapes=[pltpu.VMEM((ne, LANE), F32)],
        compiler_params=params, name="moe_positions",
    )(eidx, pad_start.astype(F32).reshape(ne, 1))
    ts = SCATTER_TILE
    p = n_blocks * bm
    pos_spec = pl.BlockSpec((TOP_K, ts), lambda i: (0, i), memory_space=pltpu.SMEM)
    xs = pl.pallas_call(
        functools.partial(_dispatch_kernel, n_blocks=n_blocks),
        out_shape=jax.ShapeDtypeStruct((p, d), F32),
        grid_spec=pltpu.PrefetchScalarGridSpec(
            num_scalar_prefetch=3,
            grid=(n // ts,),
            in_specs=[pl.BlockSpec((TOP_K, ts), lambda i, *_: (0, i), memory_space=pltpu.SMEM),
                      pl.BlockSpec((ts, d), lambda i, *_: (i, 0))],
            out_specs=pl.BlockSpec(memory_space=pl.ANY),
            scratch_shapes=[pltpu.VMEM((bm, d), F32), pltpu.SemaphoreType.DMA, pltpu.SemaphoreType.DMA],
        ),
        compiler_params=params, name="moe_dispatch",
    )(pad_start + counts, padded - counts, n_used, pos, t)

    def x_map(i, blk_e, nused):
        return (jnp.minimum(i, nused[0] - 1), 0)

    def w_map(i, blk_e, nused):
        return (layer, blk_e[i], 0, 0)

    y_p = pl.pallas_call(
        _moe_ffn_kernel,
        out_shape=jax.ShapeDtypeStruct((p, d), F32),
        grid_spec=pltpu.PrefetchScalarGridSpec(
            num_scalar_prefetch=2,
            grid=(n_blocks,),
            in_specs=[pl.BlockSpec((bm, d), x_map),
                      pl.BlockSpec((None, None, d, ff2), w_map),
                      pl.BlockSpec((None, None, ff2 // 2, d), w_map)],
            out_specs=pl.BlockSpec((bm, d), lambda i, blk_e, nused: (i, 0)),
            scratch_shapes=[pltpu.VMEM((d, ff2), BF16), pltpu.VMEM((ff2 // 2, d), BF16)],
        ),
        compiler_params=params, name="moe_expert_ffn",
    )(blk_e, n_used, xs, w_gu, w_down)
    return pl.pallas_call(
        _combine_kernel,
        out_shape=jax.ShapeDtypeStruct((n, d), F32),
        grid=(n // ts,),
        in_specs=[pos_spec, pl.BlockSpec((ts, TOP_K), lambda i: (i, 0)), pl.BlockSpec(memory_space=pl.ANY)],
        out_specs=pl.BlockSpec((ts, d), lambda i: (i, 0)),
        scratch_shapes=[pltpu.VMEM((TOP_K, ts, d), F32), pltpu.SemaphoreType.DMA],
        compiler_params=params, name="moe_combine",
    )(pos, wsel.T, y_p)


def _moe(t, router_w, router_b, w_gu, w_down, sh_gu, sh_down, layer):
    routed = _moe_routed(t, router_w, router_b, w_gu, w_down, layer)
    au = _matmul(t, sh_gu)
    ff = sh_gu.shape[-1] // 2
    shared = _matmul(jax.nn.silu(au[:, :ff]) * au[:, ff:], sh_down)
    return routed + shared


def kernel(x, c, ctx, c_ctx, ada_w, ada_b, norm_g, w_in, w_out, hy_conv_w, hy_conv_b, hy_ffn_w1, hy_ffn_b1, hy_ffn_w2, hy_ffn_b2, hy_ffn_w3, hy_freq, hy_bias, hy_norm, hg_lb_logits, hg_norm, gla_a_up, gla_a_b, gla_norm, ml_conv_w, ml_conv_b, ml_gate_b, ml_norm, router_w, router_b, exp_w_gu, exp_w_down, sh_w_gu, sh_w_down):
    bsz, seq, d = x.shape
    n_ctx = ctx.shape[1]
    depth = ada_w.shape[0]
    rows = seq // GRID_W
    x = x + _pos_embed_2d(rows, d)[None]
    xc = ctx
    lb_cum = jnp.cumsum(jax.nn.softmax(hg_lb_logits, axis=0), axis=0)
    lower_bounds = lb_cum - lb_cum[0:1]
    for l in range(depth):
        with_ctx = l < depth - 1
        mod = (jax.nn.silu(c) @ ada_w[l] + ada_b[l])[:, None, :]
        mod_c = jax.nn.silu(c_ctx) @ ada_w[l] + ada_b[l]
        sh1, sc1, g1, sh2, sc2, g2 = jnp.split(mod, 6, axis=-1)
        csh1, csc1, cg1, csh2, csc2, cg2 = jnp.split(mod_c, 6, axis=-1)
        w_in_l = _arrange_w_in(w_in[l])
        ctx_rows = lambda a: jnp.broadcast_to(a.reshape(1, 1, d), (bsz, 1, d))
        pl3 = _in_proj(x, norm_g[l, 0], sc1, sh1, w_in_l)
        pc3 = _in_proj(xc, norm_g[l, 0], ctx_rows(csc1), ctx_rows(csh1), w_in_l)
        hy_args = (hy_conv_w[l], hy_conv_b[l], hy_ffn_w1[l], hy_ffn_b1[l], hy_ffn_w2[l], hy_ffn_b2[l],
                   hy_ffn_w3[l], hy_freq[l], hy_bias[l], hy_norm[l])
        y_hy = _hyena_long(pl3, *hy_args)
        yc_hg, y_hg = _hgrn2(pc3, pl3, lower_bounds[l], hg_norm[l], with_ctx)
        yc_gla, y_gla = _gla(pc3, pl3, gla_a_up[l], gla_a_b[l], gla_norm[l], with_ctx)
        yc_ml, y_ml = _mlstm(pc3, pl3, ml_conv_w[l], ml_conv_b[l], ml_gate_b[l], ml_norm[l], with_ctx)
        y_cat = jnp.concatenate([y_hy, y_hg, y_gla, y_ml], axis=-1).reshape(bsz * seq, d)
        if with_ctx:
            yc_cat = jnp.concatenate([_hyena(pc3, *hy_args), yc_hg, yc_gla, yc_ml],
                                     axis=-1).reshape(bsz * n_ctx, d)
            y_all = _matmul(jnp.concatenate([y_cat, yc_cat], axis=0), w_out[l])
            y = y_all[:bsz * seq].reshape(bsz, seq, d)
            yc = y_all[bsz * seq:].reshape(bsz, n_ctx, d)
        else:
            y = _matmul(y_cat, w_out[l]).reshape(bsz, seq, d)
        x = x + g1 * _rms(y, norm_g[l, 1])
        h = _rms(x, norm_g[l, 2]) * (1.0 + sc2) + sh2
        moe_args = (router_w[l], router_b[l], exp_w_gu, exp_w_down, sh_w_gu[l], sh_w_down[l], l)
        if with_ctx:
            xc = xc + cg1 * _rms(yc, norm_g[l, 1])
            hc = _rms(xc, norm_g[l, 2]) * (1.0 + csc2) + csh2
            f = _moe(jnp.concatenate([h.reshape(bsz * seq, d), hc.reshape(bsz * n_ctx, d)], axis=0), *moe_args)
            xc = xc + cg2 * _rms(f[bsz * seq:].reshape(bsz, n_ctx, d), norm_g[l, 3])
            f = f[:bsz * seq]
        else:
            f = _moe(h.reshape(bsz * seq, d), *moe_args)
        x = x + g2 * _rms(f.reshape(bsz, seq, d), norm_g[l, 3])
    return x
```

```python
import functools
import math

import jax
import jax.numpy as jnp
import numpy as np
from jax import lax
from jax.experimental import pallas as pl
from jax.experimental.pallas import tpu as pltpu

F32 = jnp.float32
BF16 = jnp.bfloat16

D_MODEL = 1024
GRID_W = 64
EPS = 1e-6
POS_BASE = 10000.0
GROUP_W = D_MODEL // 4
SHORT_CONV = 3
HY_W = GROUP_W
HY_ORDER = 2
HY_EMB = 33
HY_BANDS = (HY_EMB - 1) // 2
HY_FAST_DECAY = 0.3
HY_SLOW_DECAY = 1.5
HY_DECAY_TARGET = 1e-2
HG_H = 4
HG_W = GROUP_W
HG_DK = HG_W // HG_H
GLA_H = 4
GLA_KW = GROUP_W // 2
GLA_VW = GROUP_W
GLA_DK = GLA_KW // GLA_H
GLA_DV = GLA_VW // GLA_H
GLA_RANK = 16
GLA_NORMALIZER = 16.0
ML_H = 4
ML_W = GROUP_W
ML_DH = ML_W // ML_H
CHUNK_GATED = 16
CHUNK_ML = 64
N_EXPERTS = 256
TOP_K = 8
N_EXPERT_GROUPS = 8
TOPK_GROUPS = 4
EXPERT_FF = 256
ROUTED_SCALE = 2.5
IN_SPLITS = (HY_W, HY_W, HY_W,
             HG_W, HG_W, HG_W, HG_W, HG_W,
             GLA_KW, GLA_KW, GLA_VW, GLA_RANK, GLA_RANK, GLA_VW,
             ML_W, ML_W, ML_W, 4 * ML_H, ML_W)
P_ORDER = (0, 1, 2, 3, 4, 5, 6, 7, 8, 9, 10, 13, 14, 15, 16, 18, 11, 12, 17)
COL_HY = 0
COL_HG_Q, COL_HG_I, COL_HG_ZF, COL_HG_ZB, COL_HG_G = 3, 4, 5, 6, 7
COL_GLA_QK, COL_GLA_V, COL_GLA_R = 8, 9, 10
COL_ML_Q, COL_ML_K, COL_ML_V, COL_ML_O = 11, 12, 13, 14
N_WIDE = 15

LANE = 128
SUBLANE = 8
ROW_TILE = 512
MOE_ROWS = 256
ROUTER_TILE = 256
SCATTER_TILE = 256
GLR_TILE = 128
VMEM_LIMIT = 56 * 1024 * 1024
NEG_INF = float("-inf")
COL_NARROW = N_WIDE * GROUP_W // LANE
NARROW_ML_GATES = N_WIDE * GROUP_W + 2 * GLA_RANK
P_WIDTH = N_WIDE * GROUP_W + LANE


def _arrange_w_in(w):
    offs = np.concatenate([[0], np.cumsum(IN_SPLITS)])
    cols = [w[:, offs[i]:offs[i + 1]] for i in P_ORDER]
    used = sum(IN_SPLITS)
    return jnp.concatenate(cols + [jnp.zeros((w.shape[0], P_WIDTH - used), w.dtype)], axis=1).astype(BF16)


def _in_proj_kernel(x_ref, g_ref, sc_ref, sh_ref, w_ref, o_ref):
    x = x_ref[...]
    y = x * lax.rsqrt(jnp.mean(x * x, axis=-1, keepdims=True) + EPS) * g_ref[...]
    h = y * (1.0 + sc_ref[...]) + sh_ref[...]
    o_ref[...] = jnp.dot(h.astype(BF16), w_ref[...], preferred_element_type=F32)


def _in_proj(x, gain, scale, shift, w):
    g, r, d = x.shape
    n = w.shape[1]
    tm = min(ROW_TILE, r)
    assert r % tm == 0
    return pl.pallas_call(
        _in_proj_kernel,
        out_shape=jax.ShapeDtypeStruct((g, r, n), F32),
        grid=(g, r // tm),
        in_specs=[pl.BlockSpec((None, tm, d), lambda b, i: (b, i, 0)),
                  pl.BlockSpec((1, d), lambda b, i: (0, 0)),
                  pl.BlockSpec((None, 1, d), lambda b, i: (b, 0, 0)),
                  pl.BlockSpec((None, 1, d), lambda b, i: (b, 0, 0)),
                  pl.BlockSpec((d, n), lambda b, i: (0, 0))],
        out_specs=pl.BlockSpec((None, tm, n), lambda b, i: (b, i, 0)),
        compiler_params=pltpu.CompilerParams(dimension_semantics=("arbitrary", "arbitrary"),
                                             vmem_limit_bytes=VMEM_LIMIT),
        name="input_projection",
    )(x, gain.reshape(1, d), scale, shift, w)


def _mixer_out_kernel(hy_ref, hgf_ref, hgb_ref, glf_ref, glb_ref, mlf_ref, mlb_ref, ghg_ref, ggl_ref, gml_ref,
                      x_ref, w_ref, hn_ref, n1_ref, n2_ref, mod_ref, xo_ref, h_ref):
    gw = hy_ref.shape[-1]
    seg = jnp.where(lax.broadcasted_iota(jnp.int32, (gw, gw), 0) // HG_DK
                    == lax.broadcasted_iota(jnp.int32, (gw, gw), 1) // HG_DK, 1.0 / HG_DK, 0.0)

    def head_norm(o, gain):
        ms = jnp.dot(o * o, seg, precision=lax.Precision.HIGHEST, preferred_element_type=F32)
        return o * lax.rsqrt(ms + EPS) * gain

    silu = lambda a: a * jax.nn.sigmoid(a)
    groups = (hy_ref[...],
              head_norm(hgf_ref[...] + hgb_ref[...], hn_ref[0:1]) * silu(ghg_ref[...]),
              head_norm(glf_ref[...] + glb_ref[...], hn_ref[1:2]) * silu(ggl_ref[...]),
              jax.nn.sigmoid(gml_ref[...]) * head_norm(mlf_ref[...] + mlb_ref[...], hn_ref[2:3]))
    y = None
    for i, part in enumerate(groups):
        term = jnp.dot(part.astype(BF16), w_ref[i * gw:(i + 1) * gw, :], preferred_element_type=F32)
        y = term if y is None else y + term
    rms = lambda a, g: a * lax.rsqrt(jnp.mean(a * a, axis=-1, keepdims=True) + EPS) * g
    x = x_ref[...] + mod_ref[0:1] * rms(y, n1_ref[...])
    xo_ref[...] = x
    h_ref[...] = rms(x, n2_ref[...]) * (1.0 + mod_ref[1:2]) + mod_ref[2:3]


def _mixer_out(y_hy, o_hg, o_gla, h_ml, p3, x, w_out, head_gains, gain1, gain2, mod):
    assert HG_DK == GLA_DV == ML_DH and HG_H == GLA_H == ML_H
    bsz, t, d = x.shape
    gw = GROUP_W
    tm = min(ROW_TILE, t)
    part = pl.BlockSpec((None, tm, gw), lambda b, i: (b, i, 0))
    gate = lambda c: pl.BlockSpec((None, tm, gw), lambda b, i: (b, i, c))
    full = pl.BlockSpec((None, tm, d), lambda b, i: (b, i, 0))
    whole = lambda a: pl.BlockSpec(a.shape, lambda b, i: (0,) * a.ndim)
    consts = [w_out.astype(BF16), head_gains, gain1.reshape(1, d), gain2.reshape(1, d)]
    return pl.pallas_call(
        _mixer_out_kernel,
        out_shape=[jax.ShapeDtypeStruct((bsz, t, d), F32), jax.ShapeDtypeStruct((bsz, t, d), F32)],
        grid=(bsz, t // tm),
        in_specs=([part] * 7 + [gate(COL_HG_G), gate(COL_GLA_R), gate(COL_ML_O), full]
                  + [whole(a) for a in consts] + [pl.BlockSpec((None, 3, d), lambda b, i: (b, 0, 0))]),
        out_specs=[full, full],
        compiler_params=pltpu.CompilerParams(dimension_semantics=("arbitrary", "arbitrary"),
                                             vmem_limit_bytes=VMEM_LIMIT),
        name="mixer_output",
    )(y_hy, *o_hg, *o_gla, *h_ml, p3, p3, p3, x, *consts, mod)


def _rms(x, g):
    return x * lax.rsqrt(jnp.mean(x * x, axis=-1, keepdims=True) + EPS) * g


def _short_conv_kernel(prev_ref, cur_ref, next_ref, w_ref, b_ref, o_ref, *, act):
    j = pl.program_id(1)
    u = cur_ref[...]
    tt = u.shape[0]
    row = lax.broadcasted_iota(jnp.int32, u.shape, 0)
    before = jnp.where(j > 0, prev_ref[SUBLANE - 1:SUBLANE, :], 0.0)
    after = jnp.where(j < pl.num_programs(1) - 1, next_ref[0:1, :], 0.0)
    up = jnp.where(row == 0, before, pltpu.roll(u, 1, 0))
    dn = jnp.where(row == tt - 1, after, pltpu.roll(u, tt - 1, 0))
    y = w_ref[0:1] * up + w_ref[1:2] * u + w_ref[2:3] * dn + b_ref[...]
    if act:
        y = y * jax.nn.sigmoid(y)
    o_ref[...] = y


def _short_conv(p3, col0, ncols, w, b, act):
    assert SHORT_CONV == 3
    bsz, t, _ = p3.shape
    tt = min(ROW_TILE, t)
    halo = tt // SUBLANE
    last = t // SUBLANE - 1
    gw = GROUP_W
    cur = pl.BlockSpec((None, tt, gw), lambda bi, j, c: (bi, j, col0 + c))
    prev = pl.BlockSpec((None, SUBLANE, gw), lambda bi, j, c: (bi, jnp.maximum(j * halo - 1, 0), col0 + c))
    nxt = pl.BlockSpec((None, SUBLANE, gw), lambda bi, j, c: (bi, jnp.minimum((j + 1) * halo, last), col0 + c))
    return pl.pallas_call(
        functools.partial(_short_conv_kernel, act=act),
        out_shape=jax.ShapeDtypeStruct((bsz, t, ncols * gw), F32),
        grid=(bsz, t // tt, ncols),
        in_specs=[prev, cur, nxt, pl.BlockSpec((SHORT_CONV, gw), lambda bi, j, c: (0, c)),
                  pl.BlockSpec((1, gw), lambda bi, j, c: (0, c))],
        out_specs=pl.BlockSpec((None, tt, gw), lambda bi, j, c: (bi, j, c)),
        compiler_params=pltpu.CompilerParams(dimension_semantics=("arbitrary",) * 3, vmem_limit_bytes=VMEM_LIMIT),
        name="short_conv",
    )(p3, p3, p3, w, b.reshape(1, ncols * gw))


def _pos_embed_2d(rows, d):
    r = jnp.repeat(jnp.arange(rows, dtype=F32), GRID_W)
    col = (jnp.arange(rows * GRID_W) % GRID_W).astype(F32)
    quarter = d // 4
    omega = 1.0 / (POS_BASE ** (jnp.arange(quarter, dtype=F32) / quarter))

    def axis_emb(p):
        ang = p[:, None] * omega[None, :]
        return jnp.concatenate([jnp.sin(ang), jnp.cos(ang)], axis=-1)

    return jnp.concatenate([axis_emb(r), axis_emb(col)], axis=-1)


def _hyena_spectra(L, w1, b1, w2, b2, w3, freq):
    t = jnp.linspace(0.0, 1.0, L, dtype=F32)[:, None]
    w = 2.0 * math.pi * jnp.arange(L, dtype=F32)[:, None] / L
    bands = jnp.linspace(1e-4, HY_BANDS - 1, HY_BANDS, dtype=F32)[None, :]
    feats = jnp.concatenate([t, jnp.cos(bands * w), -jnp.sin(bands * w)], axis=-1)
    z = jnp.sin(freq[0] * (feats @ w1 + b1))
    z = jnp.sin(freq[1] * (z @ w2 + b2))
    h = (z @ w3).reshape(L, HY_ORDER, 2, HY_W)
    max_decay = math.log(HY_DECAY_TARGET) / HY_FAST_DECAY
    min_decay = math.log(HY_DECAY_TARGET) / HY_SLOW_DECAY
    deltas = jnp.abs(jnp.linspace(min_decay, max_decay, HY_W, dtype=F32))
    h = h * jnp.exp(-t[:, :, None, None] * deltas)
    fwd = h[:, :, 0]
    bwd = h[1:, :, 1][::-1]
    l1 = jnp.sum(jnp.abs(fwd), axis=0) + jnp.sum(jnp.abs(bwd), axis=0)
    filt = jnp.concatenate([fwd, jnp.zeros((1, HY_ORDER, HY_W), F32), bwd], axis=0) / l1
    return jnp.fft.rfft(filt, axis=0)


def _fft_conv(u, spec, bias):
    L = u.shape[1]
    y = jnp.fft.irfft(jnp.fft.rfft(u, n=2 * L, axis=1) * spec, n=2 * L, axis=1)[:, :L]
    return y + u * bias


def _hyena(p3, conv_w, conv_b, w1, b1, w2, b2, w3, freq, bias, norm_g):
    u = _short_conv(p3, COL_HY, 3, conv_w, conv_b, False)
    v, x1, x2 = u[..., :HY_W], u[..., HY_W:2 * HY_W], u[..., 2 * HY_W:]
    spec = _hyena_spectra(u.shape[1], w1, b1, w2, b2, w3, freq)
    z = x1 * _fft_conv(v, spec[:, 0], bias[0])
    y = x2 * _fft_conv(z, spec[:, 1], bias[1])
    return _rms(y, norm_g)


FFT_N1 = 128
FFT_N2 = 128
FFT_KTILE = 8
FFT_NTILE = 4096


def _dft_tables(n1, n2):
    n = n1 * n2
    k = np.arange(n1)
    f1 = np.exp(-2j * np.pi * np.outer(k, k) / n1)
    f2 = np.exp(-2j * np.pi * np.outer(np.arange(n2), np.arange(n2)) / n2)
    tw = np.exp(-2j * np.pi * np.outer(np.arange(n1), np.arange(n2)) / n)
    as32 = lambda a: jnp.asarray(np.ascontiguousarray(a), F32)
    f1_fwd = as32(np.concatenate([f1.real, f1.imag], axis=0))
    f1_inv = as32(np.concatenate([f1.real, f1.imag], axis=1) / n)
    f2_inv = as32(np.block([[f2.real, f2.imag], [-f2.imag, f2.real]]))
    return f1_fwd, f1_inv, as32(f2.real), as32(f2.imag), f2_inv, as32(tw.real), as32(tw.imag)


def _stage_kernel(w_ref, x_ref, o_ref):
    o_ref[...] = jnp.dot(w_ref[...].astype(BF16), x_ref[...].astype(BF16),
                         preferred_element_type=F32).astype(o_ref.dtype)


def _stage_matmul(w, x, out_dtype):
    g, k, n = x.shape
    m = w.shape[0]
    tn = FFT_NTILE
    return pl.pallas_call(
        _stage_kernel,
        out_shape=jax.ShapeDtypeStruct((g, m, n), out_dtype),
        grid=(g, n // tn),
        in_specs=[pl.BlockSpec((m, k), lambda b, j: (0, 0)), pl.BlockSpec((None, k, tn), lambda b, j: (b, 0, j))],
        out_specs=pl.BlockSpec((None, m, tn), lambda b, j: (b, 0, j)),
        compiler_params=pltpu.CompilerParams(dimension_semantics=("arbitrary", "arbitrary"),
                                             vmem_limit_bytes=VMEM_LIMIT),
        name="dft_stage",
    )(w, x)


def _twiddled_f2(f2r, f2i, tr, ti):
    gr = f2r * tr - f2i * ti
    gi = f2r * ti + f2i * tr
    return jnp.concatenate([jnp.concatenate([gr, -gi], axis=1), jnp.concatenate([gi, gr], axis=1)], axis=0)


def _spectrum_kernel(a_ref, f2r_ref, f2i_ref, tr_ref, ti_ref, x_ref):
    n2 = f2r_ref.shape[0]
    tr = tr_ref[...]
    ti = ti_ref[...]
    for i in range(a_ref.shape[0]):
        g = _twiddled_f2(f2r_ref[...], f2i_ref[...], tr[i:i + 1], ti[i:i + 1])
        x_ref[i] = jnp.dot(g.astype(BF16), a_ref[i], preferred_element_type=F32)


def _conv_mid_kernel(a_ref, h_ref, f2r_ref, f2i_ref, f2inv_ref, tr_ref, ti_ref, z_ref):
    n2 = f2r_ref.shape[0]
    tr = tr_ref[...]
    ti = ti_ref[...]
    tr_col = tr.T
    ti_col = ti.T
    f2inv = f2inv_ref[...].astype(BF16)
    for i in range(a_ref.shape[0]):
        g = _twiddled_f2(f2r_ref[...], f2i_ref[...], tr[i:i + 1], ti[i:i + 1])
        x = jnp.dot(g.astype(BF16), a_ref[i], preferred_element_type=F32)
        xr, xi = x[:n2], x[n2:]
        hr, hi = h_ref[i, :n2], h_ref[i, n2:]
        y = jnp.concatenate([hr * xr - hi * xi, hr * xi + hi * xr], axis=0)
        w = jnp.dot(f2inv, y.astype(BF16), preferred_element_type=F32)
        wr, wi = w[:n2], w[n2:]
        cr, ci = tr_col[:, i:i + 1], ti_col[:, i:i + 1]
        z_ref[i] = jnp.concatenate([cr * wr + ci * wi, cr * wi - ci * wr], axis=0).astype(z_ref.dtype)


def _dft_mid_specs(g, c):
    n1, n2, kt = FFT_N1, FFT_N2, FFT_KTILE
    blk = pl.BlockSpec((None, kt, 2 * n2, c), lambda b, j: (b, j, 0, 0))
    const = lambda r, cc: pl.BlockSpec((r, cc), lambda b, j: (0, 0))
    twid = pl.BlockSpec((kt, n2), lambda b, j: (j, 0))
    params = pltpu.CompilerParams(dimension_semantics=("arbitrary", "arbitrary"), vmem_limit_bytes=VMEM_LIMIT)
    return blk, const, twid, params, (g, n1 // kt)


def _to_k1_major(a2d, c):
    g = a2d.shape[0]
    return a2d.reshape(g, 2, FFT_N1, FFT_N2, c).transpose(0, 2, 1, 3, 4).reshape(g, FFT_N1, 2 * FFT_N2, c)


def _filter_spectrum(filt, tables):
    g, n, c = filt.shape
    f1_fwd, _, f2r, f2i, _, twr, twi = tables
    a = _stage_matmul(f1_fwd, filt.reshape(g, FFT_N1, FFT_N2 * c), BF16)
    blk, const, twid, params, grid = _dft_mid_specs(g, c)
    return pl.pallas_call(
        _spectrum_kernel,
        out_shape=jax.ShapeDtypeStruct((g, FFT_N1, 2 * FFT_N2, c), F32),
        grid=grid,
        in_specs=[blk, const(FFT_N2, FFT_N2), const(FFT_N2, FFT_N2), twid, twid],
        out_specs=blk, compiler_params=params, name="dft_spectrum",
    )(_to_k1_major(a, c), f2r, f2i, twr, twi)


def _long_conv(u, spec, tables):
    g, l, c = u.shape
    f1_fwd, f1_inv, f2r, f2i, f2inv, twr, twi = tables
    half = l // FFT_N2
    a = _stage_matmul(f1_fwd[:, :half], u.reshape(g, half, FFT_N2 * c), BF16)
    blk, const, twid, params, grid = _dft_mid_specs(g, c)
    hspec = pl.BlockSpec((FFT_KTILE, 2 * FFT_N2, c), lambda b, j: (j, 0, 0))
    z = pl.pallas_call(
        _conv_mid_kernel,
        out_shape=jax.ShapeDtypeStruct((g, FFT_N1, 2 * FFT_N2, c), BF16),
        grid=grid,
        in_specs=[blk, hspec, const(FFT_N2, FFT_N2), const(FFT_N2, FFT_N2), const(2 * FFT_N2, 2 * FFT_N2),
                  twid, twid],
        out_specs=blk, compiler_params=params, name="dft_conv_mid",
    )(_to_k1_major(a, c), spec, f2r, f2i, f2inv, twr, twi)
    z2d = z.reshape(g, FFT_N1, 2, FFT_N2, c).transpose(0, 2, 1, 3, 4).reshape(g, 2 * FFT_N1, FFT_N2 * c)
    y = _stage_matmul(f1_inv[:half], z2d, F32)
    return y.reshape(g, l, c)


def _hyena_filters(L, w1, b1, w2, b2, w3, freq):
    t = jnp.linspace(0.0, 1.0, L, dtype=F32)[:, None]
    w = 2.0 * math.pi * jnp.arange(L, dtype=F32)[:, None] / L
    bands = jnp.linspace(1e-4, HY_BANDS - 1, HY_BANDS, dtype=F32)[None, :]
    feats = jnp.concatenate([t, jnp.cos(bands * w), -jnp.sin(bands * w)], axis=-1)
    max_decay = math.log(HY_DECAY_TARGET) / HY_FAST_DECAY
    min_decay = math.log(HY_DECAY_TARGET) / HY_SLOW_DECAY
    deltas = jnp.abs(jnp.linspace(min_decay, max_decay, HY_W, dtype=F32))
    w3d = w3.reshape(w3.shape[0], HY_ORDER, 2, HY_W)

    def side(f, tt, direction):
        z = jnp.sin(freq[0] * (f @ w1 + b1))
        z = jnp.sin(freq[1] * (z @ w2 + b2))
        h = (z @ w3d[:, :, direction].reshape(w3.shape[0], HY_ORDER * HY_W)).reshape(-1, HY_ORDER, HY_W)
        return h * jnp.exp(-tt[:, :, None] * deltas)

    fwd = side(feats, t, 0)
    bwd = side(feats[::-1], t[::-1], 1)[:L - 1]
    l1 = jnp.sum(jnp.abs(fwd), axis=0) + jnp.sum(jnp.abs(bwd), axis=0)
    return jnp.concatenate([fwd, jnp.zeros((1, HY_ORDER, HY_W), F32), bwd], axis=0) / l1


def _hyena_long(p3, conv_w, conv_b, w1, b1, w2, b2, w3, freq, bias, norm_g):
    u = _short_conv(p3, COL_HY, 3, conv_w, conv_b, False)
    v, x1, x2 = u[..., :HY_W], u[..., HY_W:2 * HY_W], u[..., 2 * HY_W:]
    L = u.shape[1]
    assert 2 * L == FFT_N1 * FFT_N2
    tables = _dft_tables(FFT_N1, FFT_N2)
    filt = _hyena_filters(L, w1, b1, w2, b2, w3, freq)
    spec = _filter_spectrum(jnp.moveaxis(filt, 1, 0), tables)
    z = x1 * (_long_conv(v, spec[0], tables) + v * bias[0])
    y = x2 * (_long_conv(z, spec[1], tables) + z * bias[1])
    return _rms(y, norm_g)


def _glr_direction(q, k, v, g, st_ref, reverse, nh):
    tc, hk = k.shape
    hv = v.shape[1]
    c = CHUNK_GATED
    hi = lax.Precision.HIGHEST
    ti = lax.broadcasted_iota(jnp.int32, (tc, tc), 0)
    tj = lax.broadcasted_iota(jnp.int32, (tc, tc), 1)
    same = (ti // c) == (tj // c)
    seen = (tj >= ti) if reverse else (tj <= ti)
    bcum = jnp.dot(jnp.where(same, jnp.where(seen, 1.0, 0.0), 0.0), g, precision=hi, preferred_element_type=F32)
    btot = jnp.dot(jnp.where(same, 1.0, 0.0), g, precision=hi, preferred_element_type=F32)
    qd = q * jnp.exp(bcum)
    kd = k * jnp.exp(btot - bcum)
    dec = jnp.exp(btot)
    head_sum = jnp.where(lax.broadcasted_iota(jnp.int32, (hk, hv), 0) // (hk // nh)
                         == lax.broadcasted_iota(jnp.int32, (hk, hv), 1) // (hv // nh), 1.0, 0.0).astype(BF16)
    in_chunk = lax.broadcasted_iota(jnp.int32, (tc, hk), 0) % c
    o = jnp.zeros((tc, hv), F32)
    for lag in range(c):
        if lag == 0:
            ks, bs, vs = k, bcum, v
        else:
            shift = tc - lag if reverse else lag
            ks, bs, vs = pltpu.roll(k, shift, 0), pltpu.roll(bcum, shift, 0), pltpu.roll(v, shift, 0)
        valid = (in_chunk + lag <= c - 1) if reverse else (in_chunk >= lag)
        x = q * ks * jnp.exp(jnp.where(valid, bcum - bs, NEG_INF))
        o = o + jnp.dot(x.astype(BF16), head_sum, preferred_element_type=F32) * vs
    head_mask = (lax.broadcasted_iota(jnp.int32, (hv, hk), 0) // (hv // nh)
                 == lax.broadcasted_iota(jnp.int32, (hv, hk), 1) // (hk // nh))
    st = st_ref[...]
    nch = tc // c
    outs = [None] * nch
    for ci in (range(nch - 1, -1, -1) if reverse else range(nch)):
        sl = slice(ci * c, (ci + 1) * c)
        outs[ci] = lax.dot_general(qd[sl].astype(BF16), st.astype(BF16), (((1,), (1,)), ((), ())),
                                   preferred_element_type=F32)
        ds = lax.dot_general(v[sl].astype(BF16), kd[sl].astype(BF16), (((0,), (0,)), ((), ())),
                             preferred_element_type=F32)
        st = st * dec[ci * c:ci * c + 1] + jnp.where(head_mask, ds, 0.0)
    st_ref[...] = st
    return o + jnp.concatenate(outs, axis=0)


def _log_sigmoid(z):
    return jnp.minimum(z, 0.0) - jnp.log1p(jnp.exp(-jnp.abs(z)))


def _hgrn2_kernel(qf_ref, if_ref, zf_ref, qb_ref, ib_ref, zb_ref, lb_ref, s0f_ref, s0b_ref,
                  of_ref, ob_ref, sf_ref, sb_ref, stf, stb):
    j = pl.program_id(1)

    @pl.when(j == 0)
    def _():
        stf[...] = s0f_ref[...]
        stb[...] = s0b_ref[...]

    one_minus_lb, log_lb, log_ub = lb_ref[0:1], lb_ref[1:2], lb_ref[2:3]

    def gate(z):
        return one_minus_lb * jax.nn.sigmoid(-z), jnp.logaddexp(log_lb, log_ub + _log_sigmoid(z))

    silu = lambda a: a * jax.nn.sigmoid(a)
    k_f, g_f = gate(zf_ref[...])
    k_b, g_b = gate(zb_ref[...])
    of_ref[...] = _glr_direction(silu(qf_ref[...]), k_f, if_ref[...], g_f, stf, False, HG_H)
    ob_ref[...] = _glr_direction(silu(qb_ref[...]), k_b, ib_ref[...], g_b, stb, True, HG_H)

    @pl.when(j == pl.num_programs(1) - 1)
    def _():
        sf_ref[...] = stf[...]
        sb_ref[...] = stb[...]


def _gla_kernel(qkf_ref, vf_ref, nf_ref, qkb_ref, vb_ref, nb_ref, aup_ref, ab_ref, s0f_ref, s0b_ref,
                of_ref, ob_ref, sf_ref, sb_ref, stf, stb):
    j = pl.program_id(1)

    @pl.when(j == 0)
    def _():
        stf[...] = s0f_ref[...]
        stb[...] = s0b_ref[...]

    def gate(narrow, idx):
        a = narrow[:, idx * GLA_RANK:(idx + 1) * GLA_RANK]
        lin = jnp.dot(a.astype(BF16), aup_ref[idx].astype(BF16), preferred_element_type=F32) + ab_ref[idx]
        return _log_sigmoid(lin) / GLA_NORMALIZER

    qk_f = qkf_ref[...]
    qk_b = qkb_ref[...]
    of_ref[...] = _glr_direction(qk_f[:, :GLA_KW] * GLA_DK ** -0.5, qk_f[:, GLA_KW:], vf_ref[...],
                                 gate(nf_ref[...], 0), stf, False, GLA_H)
    ob_ref[...] = _glr_direction(qk_b[:, :GLA_KW] * GLA_DK ** -0.5, qk_b[:, GLA_KW:], vb_ref[...],
                                 gate(nb_ref[...], 1), stb, True, GLA_H)

    @pl.when(j == pl.num_programs(1) - 1)
    def _():
        sf_ref[...] = stf[...]
        sb_ref[...] = stb[...]


def _glr_call(kernel_fn, p3, fwd_cols, bwd_cols, consts, s0_f, s0_b, hk, hv, name):
    bsz, t, _ = p3.shape
    tc = GLR_TILE
    nsb = t // tc
    fwd = lambda w, c: pl.BlockSpec((None, tc, w), lambda b, j: (b, j, c))
    bwd = lambda w, c: pl.BlockSpec((None, tc, w), lambda b, j: (b, nsb - 1 - j, c))
    whole = lambda a: pl.BlockSpec(a.shape, lambda b, j: (0,) * a.ndim)
    st = pl.BlockSpec((None, hv, hk), lambda b, j: (b, 0, 0))
    out_f = pl.BlockSpec((None, tc, hv), lambda b, j: (b, j, 0))
    out_b = pl.BlockSpec((None, tc, hv), lambda b, j: (b, nsb - 1 - j, 0))
    return pl.pallas_call(
        kernel_fn,
        out_shape=[jax.ShapeDtypeStruct((bsz, t, hv), F32), jax.ShapeDtypeStruct((bsz, t, hv), F32),
                   jax.ShapeDtypeStruct((bsz, hv, hk), F32), jax.ShapeDtypeStruct((bsz, hv, hk), F32)],
        grid=(bsz, nsb),
        in_specs=([fwd(w, c) for w, c in fwd_cols] + [bwd(w, c) for w, c in bwd_cols]
                  + [whole(a) for a in consts] + [st, st]),
        out_specs=[out_f, out_b, st, st],
        scratch_shapes=[pltpu.VMEM((hv, hk), F32), pltpu.VMEM((hv, hk), F32)],
        compiler_params=pltpu.CompilerParams(dimension_semantics=("arbitrary", "arbitrary"),
                                             vmem_limit_bytes=VMEM_LIMIT),
        name=name,
    )(*([p3] * (len(fwd_cols) + len(bwd_cols))), *consts, s0_f, s0_b)


def _hgrn2_seq(p3, lb, s0_f, s0_b):
    lb_rows = jnp.stack([1.0 - lb, jnp.log(lb), jnp.log1p(-lb)])
    cols = lambda z: [(GROUP_W, COL_HG_Q), (GROUP_W, COL_HG_I), (GROUP_W, z)]
    o_f, o_b, s_f, s_b = _glr_call(_hgrn2_kernel, p3, cols(COL_HG_ZF), cols(COL_HG_ZB), [lb_rows],
                                   s0_f, s0_b, HG_W, HG_W, "hgrn2_recurrence")
    return (o_f, o_b), s_f, s_b


def _hgrn2(pc3, pl3, lb):
    s0 = jnp.zeros((pl3.shape[0], HG_W, HG_W), F32)
    oc, s_f, s_b = _hgrn2_seq(pc3, lb, s0, s0)
    o, _, _ = _hgrn2_seq(pl3, lb, s_f, s_b)
    return oc, o


def _gla_seq(p3, a_up, a_b, s0_f, s0_b):
    cols = [(GROUP_W, COL_GLA_QK), (GROUP_W, COL_GLA_V), (LANE, COL_NARROW)]
    o_f, o_b, s_f, s_b = _glr_call(_gla_kernel, p3, cols, cols, [a_up, a_b.reshape(2, 1, GLA_KW)],
                                   s0_f, s0_b, GLA_KW, GLA_VW, "gla_recurrence")
    return (o_f, o_b), s_f, s_b


def _gla(pc3, pl3, a_up, a_b):
    s0 = jnp.zeros((pl3.shape[0], GLA_VW, GLA_KW), F32)
    oc, s_f, s_b = _gla_seq(pc3, a_up, a_b, s0, s0)
    o, _, _ = _gla_seq(pl3, a_up, a_b, s_f, s_b)
    return oc, o


def _mlstm_direction(q, k, v, igx, lfx, s_ref, n_ref, m_ref, reverse, nh):
    tc, w = q.shape
    seg = w // nh
    assert tc == seg
    hi = lax.Precision.HIGHEST
    ti = lax.broadcasted_iota(jnp.int32, (tc, tc), 0)
    tj = lax.broadcasted_iota(jnp.int32, (tc, tc), 1)
    seen = (tj >= ti) if reverse else (tj <= ti)
    b = jnp.dot(jnp.where(seen, 1.0, 0.0), lfx, precision=hi, preferred_element_type=F32)
    bl = b[0:1] if reverse else b[tc - 1:tc]
    a = bl - b + igx
    ma = jnp.max(a, axis=0, keepdims=True)
    kw = jnp.exp(a - ma) * k
    s_prev = s_ref[...]
    n_prev = n_ref[...]
    m_prev = m_ref[...]
    lane = lax.broadcasted_iota(jnp.int32, (tc, w), 1)
    row = lax.broadcasted_iota(jnp.int32, (tc, w), 0)
    same_head = (lax.broadcasted_iota(jnp.int32, (w, w), 0) // seg
                 == lax.broadcasted_iota(jnp.int32, (w, w), 1) // seg)
    kexp = jnp.where(same_head, jnp.concatenate([k] * nh, axis=0), 0.0)
    vexp = jnp.where(same_head, jnp.concatenate([v] * nh, axis=0), 0.0)
    scores = lax.dot_general(q.astype(BF16), kexp.astype(BF16), (((1,), (1,)), ((), ())),
                             preferred_element_type=F32)
    s_lane = lane % seg
    by_src = jnp.sum(jnp.where(s_lane == row, igx - b, 0.0), axis=0, keepdims=True)
    ok = (s_lane >= row) if reverse else (s_lane <= row)
    dmat = jnp.where(ok, b + by_src, NEG_INF)
    inter = b + m_prev
    head_of_lane = lane // seg
    seg_max = jnp.full((tc, w), NEG_INF, F32)
    for h in range(nh):
        in_h = head_of_lane == h
        seg_max = jnp.where(in_h, jnp.max(jnp.where(in_h, dmat, NEG_INF), axis=1, keepdims=True), seg_max)
    m_t = jnp.maximum(inter, seg_max)
    wq = jnp.exp(dmat - m_t) * scores
    w_int = jnp.exp(inter - m_t)
    head_sum = jnp.where(same_head, 1.0, 0.0).astype(BF16)
    num = (jnp.dot(wq.astype(BF16), vexp.astype(BF16), preferred_element_type=F32)
           + w_int * jnp.dot(q.astype(BF16), s_prev.astype(BF16), preferred_element_type=F32))
    den = (jnp.dot(wq.astype(BF16), head_sum, preferred_element_type=F32)
           + w_int * jnp.dot((q * n_prev).astype(BF16), head_sum, preferred_element_type=F32))
    h_out = num / jnp.maximum(jnp.abs(den), jnp.exp(-m_t))
    m_new = jnp.maximum(bl + m_prev, ma)
    d_old = jnp.exp(bl + m_prev - m_new)
    d_new = jnp.exp(ma - m_new)
    ds = lax.dot_general(kw.astype(BF16), v.astype(BF16), (((0,), (0,)), ((), ())), preferred_element_type=F32)
    s_ref[...] = d_old * s_prev + d_new * jnp.where(same_head, ds, 0.0)
    n_ref[...] = d_old * n_prev + d_new * jnp.sum(kw, axis=0, keepdims=True)
    m_ref[...] = m_new
    return h_out


def _mlstm_kernel(qf_ref, kf_ref, vf_ref, igf_ref, lff_ref, qb_ref, kb_ref, vb_ref, igb_ref, lfb_ref,
                  s0f_ref, n0f_ref, m0f_ref, s0b_ref, n0b_ref, m0b_ref,
                  hf_ref, hb_ref, sf_ref, nf_ref, mf_ref, sb_ref, nb_ref, mb_ref,
                  s_f, n_f, m_f, s_b, n_b, m_b, *, nh):
    j = pl.program_id(1)

    @pl.when(j == 0)
    def _():
        s_f[...] = s0f_ref[...]
        n_f[...] = n0f_ref[...]
        m_f[...] = m0f_ref[...]
        s_b[...] = s0b_ref[...]
        n_b[...] = n0b_ref[...]
        m_b[...] = m0b_ref[...]

    hf_ref[...] = _mlstm_direction(qf_ref[...], kf_ref[...], vf_ref[...], igf_ref[...], lff_ref[...],
                                   s_f, n_f, m_f, False, nh)
    hb_ref[...] = _mlstm_direction(qb_ref[...], kb_ref[...], vb_ref[...], igb_ref[...], lfb_ref[...],
                                   s_b, n_b, m_b, True, nh)

    @pl.when(j == pl.num_programs(1) - 1)
    def _():
        sf_ref[...] = s_f[...]
        nf_ref[...] = n_f[...]
        mf_ref[...] = m_f[...]
        sb_ref[...] = s_b[...]
        nb_ref[...] = n_b[...]
        mb_ref[...] = m_b[...]


def _mlstm_bidir(q, k, v, ig_f, lf_f, ig_b, lf_b, st_f, st_b, nh):
    bsz, t, w = q.shape
    tc = CHUNK_ML
    nsb = t // tc
    fwd = pl.BlockSpec((None, tc, w), lambda b, j: (b, j, 0))
    bwd = pl.BlockSpec((None, tc, w), lambda b, j: (b, nsb - 1 - j, 0))
    mat = pl.BlockSpec((None, w, w), lambda b, j: (b, 0, 0))
    vec = pl.BlockSpec((None, 1, w), lambda b, j: (b, 0, 0))
    sds = jax.ShapeDtypeStruct
    state_shapes = [sds((bsz, w, w), F32), sds((bsz, 1, w), F32), sds((bsz, 1, w), F32)]
    outs = pl.pallas_call(
        functools.partial(_mlstm_kernel, nh=nh),
        out_shape=[sds((bsz, t, w), F32), sds((bsz, t, w), F32)] + state_shapes + state_shapes,
        grid=(bsz, nsb),
        in_specs=[fwd] * 5 + [bwd] * 5 + [mat, vec, vec] * 2,
        out_specs=[fwd, bwd] + [mat, vec, vec] * 2,
        scratch_shapes=[pltpu.VMEM((w, w), F32), pltpu.VMEM((1, w), F32), pltpu.VMEM((1, w), F32)] * 2,
        compiler_params=pltpu.CompilerParams(dimension_semantics=("arbitrary", "arbitrary"),
                                             vmem_limit_bytes=VMEM_LIMIT),
        name="mlstm_recurrence",
    )(q, k, v, ig_f, lf_f, q, k, v, ig_b, lf_b, *st_f, *st_b)
    return outs[0], outs[1], tuple(outs[2:5]), tuple(outs[5:8])


def _mlstm_seq(p3, conv_w, conv_b, gate_b, st_f, st_b):
    qk = _short_conv(p3, COL_ML_Q, 2, conv_w, conv_b, True)
    v = p3[..., COL_ML_V * GROUP_W:(COL_ML_V + 1) * GROUP_W]
    bsz, t, _ = p3.shape
    gates = p3[..., NARROW_ML_GATES:NARROW_ML_GATES + 4 * ML_H]
    gt = gates.reshape(bsz, t, 4, ML_H) + gate_b
    expand = lambda a: jnp.repeat(a, ML_DH, axis=-1)
    h_f, h_b, fin_f, fin_b = _mlstm_bidir(
        qk[..., :ML_W], qk[..., ML_W:] * ML_DH ** -0.5, v,
        expand(gt[:, :, 0]), expand(jax.nn.log_sigmoid(gt[:, :, 1])),
        expand(gt[:, :, 2]), expand(jax.nn.log_sigmoid(gt[:, :, 3])), st_f, st_b, ML_H)
    return (h_f, h_b), fin_f, fin_b


def _mlstm(pc3, pl3, conv_w, conv_b, gate_b):
    bsz = pl3.shape[0]
    st0 = (jnp.zeros((bsz, ML_W, ML_W), F32), jnp.zeros((bsz, 1, ML_W), F32), jnp.zeros((bsz, 1, ML_W), F32))
    hc, st_f, st_b = _mlstm_seq(pc3, conv_w, conv_b, gate_b, st0, st0)
    h, _, _ = _mlstm_seq(pl3, conv_w, conv_b, gate_b, st_f, st_b)
    return hc, h


def _router_kernel(h_ref, wt_ref, b_ref, eidx_ref, wsel_ref, cnt_ref):
    i = pl.program_id(0)
    tm = h_ref.shape[0]
    ne = wt_ref.shape[0]
    per_group = ne // N_EXPERT_GROUPS
    logits = lax.dot_general(wt_ref[...], h_ref[...], (((1,), (1,)), ((), ())),
                             preferred_element_type=F32, precision=lax.Precision.HIGHEST)
    s = jax.nn.sigmoid(logits)
    sel = s + b_ref[...]
    row = lax.broadcasted_iota(jnp.int32, (ne, tm), 0)
    gs = []
    for g in range(N_EXPERT_GROUPS):
        blk = sel[g * per_group:(g + 1) * per_group]
        r = lax.broadcasted_iota(jnp.int32, blk.shape, 0)
        m1 = jnp.max(blk, axis=0, keepdims=True)
        i1 = jnp.min(jnp.where(blk == m1, r, per_group), axis=0, keepdims=True)
        m2 = jnp.max(jnp.where(r == i1, NEG_INF, blk), axis=0, keepdims=True)
        gs.append(m1 + m2)
    grp = jnp.concatenate(gs, axis=0)
    grow = lax.broadcasted_iota(jnp.int32, grp.shape, 0)
    gsel = jnp.zeros(grp.shape, F32)
    for _ in range(TOPK_GROUPS):
        m = jnp.max(grp, axis=0, keepdims=True)
        gi = jnp.min(jnp.where(grp == m, grow, N_EXPERT_GROUPS), axis=0, keepdims=True)
        hit = grow == gi
        gsel = jnp.where(hit, 1.0, gsel)
        grp = jnp.where(hit, NEG_INF, grp)
    masked = jnp.concatenate(
        [jnp.where(gsel[g:g + 1] > 0.0, sel[g * per_group:(g + 1) * per_group], NEG_INF)
         for g in range(N_EXPERT_GROUPS)], axis=0)
    eis, ws = [], []
    picked = jnp.zeros((ne, tm), F32)
    for _ in range(TOP_K):
        m = jnp.max(masked, axis=0, keepdims=True)
        ei = jnp.min(jnp.where(masked == m, row, ne), axis=0, keepdims=True)
        hit = row == ei
        ws.append(jnp.sum(jnp.where(hit, s, 0.0), axis=0, keepdims=True))
        eis.append(ei)
        picked = jnp.where(hit, 1.0, picked)
        masked = jnp.where(hit, NEG_INF, masked)
    w = jnp.concatenate(ws, axis=0)
    eidx_ref[...] = jnp.concatenate(eis, axis=0)
    wsel_ref[...] = w / jnp.sum(w, axis=0, keepdims=True) * ROUTED_SCALE
    tot = jnp.dot(picked.astype(BF16), jnp.ones((tm, LANE), BF16), preferred_element_type=F32)

    @pl.when(i == 0)
    def _():
        cnt_ref[...] = jnp.zeros_like(cnt_ref)

    cnt_ref[...] += tot


def _pos_kernel(eidx_ref, base_ref, pos_ref, carry_ref):
    i = pl.program_id(0)
    tm = eidx_ref.shape[1]
    ne = base_ref.shape[0]

    @pl.when(i == 0)
    def _():
        carry_ref[...] = jnp.zeros_like(carry_ref)

    eidx = eidx_ref[...]
    row = lax.broadcasted_iota(jnp.int32, (ne, tm), 0)
    picked = jnp.zeros((ne, tm), F32)
    for k in range(TOP_K):
        picked = jnp.where(row == eidx[k:k + 1], 1.0, picked)
    pb = picked.astype(BF16)
    before = jnp.where(lax.broadcasted_iota(jnp.int32, (tm, tm), 0) < lax.broadcasted_iota(jnp.int32, (tm, tm), 1),
                       1.0, 0.0).astype(BF16)
    rank = jnp.dot(pb, before, preferred_element_type=F32)
    tot = jnp.dot(pb, jnp.ones((tm, LANE), BF16), preferred_element_type=F32)
    dest = rank + (base_ref[...] + carry_ref[:, 0:1])
    pos = [jnp.sum(jnp.where(row == eidx[k:k + 1], dest, 0.0), axis=0, keepdims=True) for k in range(TOP_K)]
    pos_ref[...] = jnp.concatenate(pos, axis=0).astype(jnp.int32)
    carry_ref[...] += tot


def _dispatch_kernel(zstart_ref, zlen_ref, nused_ref, pos_ref, h_ref, xs_ref, zeros, sem, zsem, *, n_blocks):
    tm = h_ref.shape[0]
    bm = zeros.shape[0]

    @pl.when(pl.program_id(0) == 0)
    def _():
        zeros[...] = jnp.zeros_like(zeros)

        def zero_copy(start, size):
            return pltpu.make_async_copy(zeros.at[pl.ds(0, size)], xs_ref.at[pl.ds(start, size)], zsem)

        def pieces(e, act):
            start = zstart_ref[e]
            rem = zlen_ref[e]
            ragged = rem & (SUBLANE - 1)
            for q in range(SUBLANE - 1):
                @pl.when(q < ragged)
                def _(q=q):
                    act(zero_copy(start + q, 1))

            start = pl.multiple_of(start + ragged, SUBLANE)
            size = bm // 2
            while size >= SUBLANE:
                @pl.when((rem & size) != 0)
                def _(start=start, size=size):
                    act(zero_copy(start, size))

                start = pl.multiple_of(start + (rem & size), SUBLANE)
                size //= 2

        def loop(act):
            def per_expert(e, carry):
                pieces(e, act)
                return carry

            def per_block(b, carry):
                act(zero_copy(pl.multiple_of(b * bm, bm), bm))
                return carry

            lax.fori_loop(0, zstart_ref.shape[0], per_expert, 0)
            lax.fori_loop(nused_ref[0], n_blocks, per_block, 0)

        loop(lambda cp: cp.start())
        loop(lambda cp: cp.wait())

    def row_copy(n, k):
        return pltpu.make_async_copy(h_ref.at[pl.ds(n, 1)], xs_ref.at[pl.ds(pos_ref[k, n], 1)], sem)

    def issue(n, carry):
        for k in range(TOP_K):
            row_copy(n, k).start()
        return carry

    def drain(n, carry):
        for k in range(TOP_K):
            row_copy(n, k).wait()
        return carry

    lax.fori_loop(0, tm, issue, 0)
    lax.fori_loop(0, tm, drain, 0)


def _moe_ffn_kernel(blk_e_ref, nused_ref, x_ref, wgu_ref, wdn_ref, o_ref, wgu_s, wdn_s):
    i = pl.program_id(0)
    e = blk_e_ref[i]
    e_prev = blk_e_ref[jnp.maximum(i - 1, 0)]

    @pl.when(i < nused_ref[0])
    def _():
        @pl.when((i == 0) | (e != e_prev))
        def _():
            wgu_s[...] = wgu_ref[...].astype(BF16)
            wdn_s[...] = wdn_ref[...].astype(BF16)

        au = jnp.dot(x_ref[...].astype(BF16), wgu_s[...], preferred_element_type=F32)
        a = au[:, :EXPERT_FF]
        u = au[:, EXPERT_FF:]
        h = (a * jax.nn.sigmoid(a)) * u
        o_ref[...] = jnp.dot(h.astype(BF16), wdn_s[...], preferred_element_type=F32)

    @pl.when(i >= nused_ref[0])
    def _():
        o_ref[...] = jnp.zeros_like(o_ref)


def _combine_kernel(pos_ref, w_ref, t_ref, x_ref, sgu_ref, sdn_ref, gain_ref, g2_ref, y_hbm, o_ref, buf, sem):
    tm = o_ref.shape[0]

    def row_copy(n, k):
        return pltpu.make_async_copy(y_hbm.at[pl.ds(pos_ref[k, n], 1)], buf.at[k, pl.ds(n, 1)], sem)

    def issue(n, carry):
        for k in range(TOP_K):
            row_copy(n, k).start()
        return carry

    def drain(n, carry):
        for k in range(TOP_K):
            row_copy(n, k).wait()
        return carry

    lax.fori_loop(0, tm, issue, 0)
    ff = sdn_ref.shape[0]
    au = jnp.dot(t_ref[...].astype(BF16), sgu_ref[...], preferred_element_type=F32)
    a, u = au[:, :ff], au[:, ff:]
    f = jnp.dot(((a * jax.nn.sigmoid(a)) * u).astype(BF16), sdn_ref[...], preferred_element_type=F32)
    lax.fori_loop(0, tm, drain, 0)
    for k in range(TOP_K):
        f = f + buf[k] * w_ref[:, k:k + 1]
    o_ref[...] = x_ref[...] + g2_ref[...] * (f * lax.rsqrt(jnp.mean(f * f, axis=-1, keepdims=True) + EPS)
                                             * gain_ref[...])


def _moe(t, x_res, router_w, router_b, w_gu, w_down, sh_gu, sh_down, layer, gain, g2_rows, rows_per_gate):
    n, d = t.shape
    ne = router_w.shape[1]
    ff2 = w_gu.shape[-1]
    params = pltpu.CompilerParams(dimension_semantics=("arbitrary",), vmem_limit_bytes=VMEM_LIMIT)
    tm = ROUTER_TILE
    eidx, wsel, cnt = pl.pallas_call(
        _router_kernel,
        out_shape=[jax.ShapeDtypeStruct((TOP_K, n), jnp.int32), jax.ShapeDtypeStruct((TOP_K, n), F32),
                   jax.ShapeDtypeStruct((ne, LANE), F32)],
        grid=(n // tm,),
        in_specs=[pl.BlockSpec((tm, d), lambda i: (i, 0)), pl.BlockSpec((ne, d), lambda i: (0, 0)),
                  pl.BlockSpec((ne, 1), lambda i: (0, 0))],
        out_specs=[pl.BlockSpec((TOP_K, tm), lambda i: (0, i)), pl.BlockSpec((TOP_K, tm), lambda i: (0, i)),
                   pl.BlockSpec((ne, LANE), lambda i: (0, 0))],
        compiler_params=params, name="moe_router",
    )(t, router_w.T, router_b.reshape(ne, 1))
    bm = MOE_ROWS
    counts = cnt[:, 0].astype(jnp.int32)
    padded = (counts + bm - 1) // bm * bm
    pad_end = jnp.cumsum(padded)
    pad_start = pad_end - padded
    n_blocks = (n * TOP_K + ne * (bm - 1)) // bm + 1
    blk_first = jnp.arange(n_blocks, dtype=jnp.int32) * bm
    blk_e = jnp.minimum(jnp.sum((pad_end[None, :] <= blk_first[:, None]).astype(jnp.int32), axis=1), ne - 1)
    n_used = (pad_end[-1] // bm).astype(jnp.int32).reshape(1)
    pos = pl.pallas_call(
        _pos_kernel,
        out_shape=jax.ShapeDtypeStruct((TOP_K, n), jnp.int32),
        grid=(n // tm,),
        in_specs=[pl.BlockSpec((TOP_K, tm), lambda i: (0, i)), pl.BlockSpec((ne, 1), lambda i: (0, 0))],
        out_specs=pl.BlockSpec((TOP_K, tm), lambda i: (0, i)),
        scratch_shapes=[pltpu.VMEM((ne, LANE), F32)],
        compiler_params=params, name="moe_positions",
    )(eidx, pad_start.astype(F32).reshape(ne, 1))
    ts = SCATTER_TILE
    p = n_blocks * bm
    pos_spec = pl.BlockSpec((TOP_K, ts), lambda i: (0, i), memory_space=pltpu.SMEM)
    xs = pl.pallas_call(
        functools.partial(_dispatch_kernel, n_blocks=n_blocks),
        out_shape=jax.ShapeDtypeStruct((p, d), F32),
        grid_spec=pltpu.PrefetchScalarGridSpec(
            num_scalar_prefetch=3,
            grid=(n // ts,),
            in_specs=[pl.BlockSpec((TOP_K, ts), lambda i, *_: (0, i), memory_space=pltpu.SMEM),
                      pl.BlockSpec((ts, d), lambda i, *_: (i, 0))],
            out_specs=pl.BlockSpec(memory_space=pl.ANY),
            scratch_shapes=[pltpu.VMEM((bm, d), F32), pltpu.SemaphoreType.DMA, pltpu.SemaphoreType.DMA],
        ),
        compiler_params=params, name="moe_dispatch",
    )(pad_start + counts, padded - counts, n_used, pos, t)

    def x_map(i, blk_e, nused):
        return (jnp.minimum(i, nused[0] - 1), 0)

    def w_map(i, blk_e, nused):
        return (layer, blk_e[i], 0, 0)

    y_p = pl.pallas_call(
        _moe_ffn_kernel,
        out_shape=jax.ShapeDtypeStruct((p, d), F32),
        grid_spec=pltpu.PrefetchScalarGridSpec(
            num_scalar_prefetch=2,
            grid=(n_blocks,),
            in_specs=[pl.BlockSpec((bm, d), x_map),
                      pl.BlockSpec((None, None, d, ff2), w_map),
                      pl.BlockSpec((None, None, ff2 // 2, d), w_map)],
            out_specs=pl.BlockSpec((bm, d), lambda i, blk_e, nused: (i, 0)),
            scratch_shapes=[pltpu.VMEM((d, ff2), BF16), pltpu.VMEM((ff2 // 2, d), BF16)],
        ),
        compiler_params=params, name="moe_expert_ffn",
    )(blk_e, n_used, xs, w_gu, w_down)
    rows = pl.BlockSpec((ts, d), lambda i: (i, 0))
    whole = lambda a: pl.BlockSpec(a.shape, lambda i: (0,) * a.ndim)
    last_gate = g2_rows.shape[0] - 1
    gate_spec = pl.BlockSpec((None, 1, d), lambda i: (jnp.minimum(i // (rows_per_gate // ts), last_gate), 0, 0))
    consts = [sh_gu.astype(BF16), sh_down.astype(BF16), gain.reshape(1, d)]
    return pl.pallas_call(
        _combine_kernel,
        out_shape=jax.ShapeDtypeStruct((n, d), F32),
        grid=(n // ts,),
        in_specs=([pos_spec, pl.BlockSpec((ts, TOP_K), lambda i: (i, 0)), rows, rows]
                  + [whole(a) for a in consts] + [gate_spec, pl.BlockSpec(memory_space=pl.ANY)]),
        out_specs=rows,
        scratch_shapes=[pltpu.VMEM((TOP_K, ts, d), F32), pltpu.SemaphoreType.DMA],
        compiler_params=params, name="moe_combine",
    )(pos, wsel.T, t, x_res, *consts, g2_rows, y_p)


def kernel(x, c, ctx, c_ctx, ada_w, ada_b, norm_g, w_in, w_out, hy_conv_w, hy_conv_b, hy_ffn_w1, hy_ffn_b1, hy_ffn_w2, hy_ffn_b2, hy_ffn_w3, hy_freq, hy_bias, hy_norm, hg_lb_logits, hg_norm, gla_a_up, gla_a_b, gla_norm, ml_conv_w, ml_conv_b, ml_gate_b, ml_norm, router_w, router_b, exp_w_gu, exp_w_down, sh_w_gu, sh_w_down):
    bsz, seq, d = x.shape
    n_ctx = ctx.shape[1]
    depth = ada_w.shape[0]
    rows = seq // GRID_W
    x = x + _pos_embed_2d(rows, d)[None]
    xc = ctx
    lb_cum = jnp.cumsum(jax.nn.softmax(hg_lb_logits, axis=0), axis=0)
    lower_bounds = lb_cum - lb_cum[0:1]
    for l in range(depth):
        with_ctx = l < depth - 1
        mod = (jax.nn.silu(c) @ ada_w[l] + ada_b[l])[:, None, :]
        mod_c = jax.nn.silu(c_ctx) @ ada_w[l] + ada_b[l]
        sh1, sc1, g1, sh2, sc2, g2 = jnp.split(mod, 6, axis=-1)
        csh1, csc1, cg1, csh2, csc2, cg2 = jnp.split(mod_c, 6, axis=-1)
        w_in_l = _arrange_w_in(w_in[l])
        ctx_rows = lambda a: jnp.broadcast_to(a.reshape(1, 1, d), (bsz, 1, d))
        pl3 = _in_proj(x, norm_g[l, 0], sc1, sh1, w_in_l)
        pc3 = _in_proj(xc, norm_g[l, 0], ctx_rows(csc1), ctx_rows(csh1), w_in_l)
        hy_args = (hy_conv_w[l], hy_conv_b[l], hy_ffn_w1[l], hy_ffn_b1[l], hy_ffn_w2[l], hy_ffn_b2[l],
                   hy_ffn_w3[l], hy_freq[l], hy_bias[l], hy_norm[l])
        y_hy = _hyena_long(pl3, *hy_args)
        oc_hg, o_hg = _hgrn2(pc3, pl3, lower_bounds[l])
        oc_gla, o_gla = _gla(pc3, pl3, gla_a_up[l], gla_a_b[l])
        hc_ml, h_ml = _mlstm(pc3, pl3, ml_conv_w[l], ml_conv_b[l], ml_gate_b[l])
        head_gains = jnp.stack([hg_norm[l], gla_norm[l], ml_norm[l]])
        out_args = (w_out[l], head_gains, norm_g[l, 1], norm_g[l, 2])
        x, h = _mixer_out(y_hy, o_hg, o_gla, h_ml, pl3, x, *out_args, jnp.concatenate([g1, sc2, sh2], axis=1))
        moe_args = (router_w[l], router_b[l], exp_w_gu, exp_w_down, sh_w_gu[l], sh_w_down[l], l, norm_g[l, 3])
        if with_ctx:
            mod_ctx = jnp.broadcast_to(jnp.stack([cg1, csc2, csh2])[None], (bsz, 3, d))
            xc, hc = _mixer_out(_hyena(pc3, *hy_args), oc_hg, oc_gla, hc_ml, pc3, xc, *out_args, mod_ctx)
            tokens = lambda a, ac: jnp.concatenate([a.reshape(bsz * seq, d), ac.reshape(bsz * n_ctx, d)], axis=0)
            gates = jnp.concatenate([g2, cg2.reshape(1, 1, d)], axis=0)
            x_all = _moe(tokens(h, hc), tokens(x, xc), *moe_args, gates, seq)
            x = x_all[:bsz * seq].reshape(bsz, seq, d)
            xc = x_all[bsz * seq:].reshape(bsz, n_ctx, d)
        else:
            x = _moe(h.reshape(bsz * seq, d), x.reshape(bsz * seq, d), *moe_args, g2, seq).reshape(bsz, seq, d)
    return x
```

```python
import functools
import math

import jax
import jax.numpy as jnp
import numpy as np
from jax import lax
from jax.experimental import pallas as pl
from jax.experimental.pallas import tpu as pltpu

F32 = jnp.float32
BF16 = jnp.bfloat16

D_MODEL = 1024
GRID_W = 64
EPS = 1e-6
POS_BASE = 10000.0
GROUP_W = D_MODEL // 4
SHORT_CONV = 3
HY_W = GROUP_W
HY_ORDER = 2
HY_EMB = 33
HY_BANDS = (HY_EMB - 1) // 2
HY_FAST_DECAY = 0.3
HY_SLOW_DECAY = 1.5
HY_DECAY_TARGET = 1e-2
HG_H = 4
HG_W = GROUP_W
HG_DK = HG_W // HG_H
GLA_H = 4
GLA_KW = GROUP_W // 2
GLA_VW = GROUP_W
GLA_DK = GLA_KW // GLA_H
GLA_DV = GLA_VW // GLA_H
GLA_RANK = 16
GLA_NORMALIZER = 16.0
ML_H = 4
ML_W = GROUP_W
ML_DH = ML_W // ML_H
CHUNK_GATED = 16
CHUNK_ML = 64
N_EXPERTS = 256
TOP_K = 8
N_EXPERT_GROUPS = 8
TOPK_GROUPS = 4
EXPERT_FF = 256
ROUTED_SCALE = 2.5
IN_SPLITS = (HY_W, HY_W, HY_W,
             HG_W, HG_W, HG_W, HG_W, HG_W,
             GLA_KW, GLA_KW, GLA_VW, GLA_RANK, GLA_RANK, GLA_VW,
             ML_W, ML_W, ML_W, 4 * ML_H, ML_W)
P_ORDER = (0, 1, 2, 3, 4, 5, 6, 7, 8, 9, 10, 13, 14, 15, 16, 18, 11, 12, 17)
COL_HY = 0
COL_HG_Q, COL_HG_I, COL_HG_ZF, COL_HG_ZB, COL_HG_G = 3, 4, 5, 6, 7
COL_GLA_QK, COL_GLA_V, COL_GLA_R = 8, 9, 10
COL_ML_Q, COL_ML_K, COL_ML_V, COL_ML_O = 11, 12, 13, 14
N_WIDE = 15

LANE = 128
SUBLANE = 8
ROW_TILE = 512
MOE_ROWS = 256
ROUTER_TILE = 256
SCATTER_TILE = 256
GLR_TILE = 128
VMEM_LIMIT = 56 * 1024 * 1024
NEG_INF = float("-inf")
COL_NARROW = N_WIDE * GROUP_W // LANE
NARROW_ML_GATES = N_WIDE * GROUP_W + 2 * GLA_RANK
P_WIDTH = N_WIDE * GROUP_W + LANE


def _arrange_w_in(w):
    offs = np.concatenate([[0], np.cumsum(IN_SPLITS)])
    cols = [w[:, offs[i]:offs[i + 1]] for i in P_ORDER]
    used = sum(IN_SPLITS)
    return jnp.concatenate(cols + [jnp.zeros((w.shape[0], P_WIDTH - used), w.dtype)], axis=1).astype(BF16)


def _in_proj_kernel(x_ref, g_ref, sc_ref, sh_ref, w_ref, o_ref):
    x = x_ref[...]
    y = x * lax.rsqrt(jnp.mean(x * x, axis=-1, keepdims=True) + EPS) * g_ref[...]
    h = y * (1.0 + sc_ref[...]) + sh_ref[...]
    o_ref[...] = jnp.dot(h.astype(BF16), w_ref[...], preferred_element_type=F32)


def _in_proj(x, gain, scale, shift, w):
    g, r, d = x.shape
    n = w.shape[1]
    tm = min(ROW_TILE, r)
    assert r % tm == 0
    return pl.pallas_call(
        _in_proj_kernel,
        out_shape=jax.ShapeDtypeStruct((g, r, n), F32),
        grid=(g, r // tm),
        in_specs=[pl.BlockSpec((None, tm, d), lambda b, i: (b, i, 0)),
                  pl.BlockSpec((1, d), lambda b, i: (0, 0)),
                  pl.BlockSpec((None, 1, d), lambda b, i: (b, 0, 0)),
                  pl.BlockSpec((None, 1, d), lambda b, i: (b, 0, 0)),
                  pl.BlockSpec((d, n), lambda b, i: (0, 0))],
        out_specs=pl.BlockSpec((None, tm, n), lambda b, i: (b, i, 0)),
        compiler_params=pltpu.CompilerParams(dimension_semantics=("arbitrary", "arbitrary"),
                                             vmem_limit_bytes=VMEM_LIMIT),
        name="input_projection",
    )(x, gain.reshape(1, d), scale, shift, w)


def _mixer_out_kernel(hy_ref, hgf_ref, hgb_ref, glf_ref, glb_ref, mlf_ref, mlb_ref, ghg_ref, ggl_ref, gml_ref,
                      x_ref, w_ref, hn_ref, n1_ref, n2_ref, mod_ref, xo_ref, h_ref):
    gw = hy_ref.shape[-1]
    seg = jnp.where(lax.broadcasted_iota(jnp.int32, (gw, gw), 0) // HG_DK
                    == lax.broadcasted_iota(jnp.int32, (gw, gw), 1) // HG_DK, 1.0 / HG_DK, 0.0)

    def head_norm(o, gain):
        ms = jnp.dot(o * o, seg, precision=lax.Precision.HIGHEST, preferred_element_type=F32)
        return o * lax.rsqrt(ms + EPS) * gain

    silu = lambda a: a * jax.nn.sigmoid(a)
    groups = (hy_ref[...],
              head_norm(hgf_ref[...] + hgb_ref[...], hn_ref[0:1]) * silu(ghg_ref[...]),
              head_norm(glf_ref[...] + glb_ref[...], hn_ref[1:2]) * silu(ggl_ref[...]),
              jax.nn.sigmoid(gml_ref[...]) * head_norm(mlf_ref[...] + mlb_ref[...], hn_ref[2:3]))
    y = None
    for i, part in enumerate(groups):
        term = jnp.dot(part.astype(BF16), w_ref[i * gw:(i + 1) * gw, :], preferred_element_type=F32)
        y = term if y is None else y + term
    rms = lambda a, g: a * lax.rsqrt(jnp.mean(a * a, axis=-1, keepdims=True) + EPS) * g
    x = x_ref[...] + mod_ref[0:1] * rms(y, n1_ref[...])
    xo_ref[...] = x
    h_ref[...] = rms(x, n2_ref[...]) * (1.0 + mod_ref[1:2]) + mod_ref[2:3]


def _mixer_out(y_hy, o_hg, o_gla, h_ml, p3, x, w_out, head_gains, gain1, gain2, mod):
    assert HG_DK == GLA_DV == ML_DH and HG_H == GLA_H == ML_H
    bsz, t, d = x.shape
    gw = GROUP_W
    tm = min(ROW_TILE, t)
    part = pl.BlockSpec((None, tm, gw), lambda b, i: (b, i, 0))
    gate = lambda c: pl.BlockSpec((None, tm, gw), lambda b, i: (b, i, c))
    full = pl.BlockSpec((None, tm, d), lambda b, i: (b, i, 0))
    whole = lambda a: pl.BlockSpec(a.shape, lambda b, i: (0,) * a.ndim)
    consts = [w_out.astype(BF16), head_gains, gain1.reshape(1, d), gain2.reshape(1, d)]
    return pl.pallas_call(
        _mixer_out_kernel,
        out_shape=[jax.ShapeDtypeStruct((bsz, t, d), F32), jax.ShapeDtypeStruct((bsz, t, d), F32)],
        grid=(bsz, t // tm),
        in_specs=([part] * 7 + [gate(COL_HG_G), gate(COL_GLA_R), gate(COL_ML_O), full]
                  + [whole(a) for a in consts] + [pl.BlockSpec((None, 3, d), lambda b, i: (b, 0, 0))]),
        out_specs=[full, full],
        compiler_params=pltpu.CompilerParams(dimension_semantics=("arbitrary", "arbitrary"),
                                             vmem_limit_bytes=VMEM_LIMIT),
        name="mixer_output",
    )(y_hy, *o_hg, *o_gla, *h_ml, p3, p3, p3, x, *consts, mod)


def _rms(x, g):
    return x * lax.rsqrt(jnp.mean(x * x, axis=-1, keepdims=True) + EPS) * g


def _short_conv_kernel(prev_ref, cur_ref, next_ref, w_ref, b_ref, o_ref, *, act):
    j = pl.program_id(1)
    u = cur_ref[...]
    tt = u.shape[0]
    row = lax.broadcasted_iota(jnp.int32, u.shape, 0)
    before = jnp.where(j > 0, prev_ref[SUBLANE - 1:SUBLANE, :], 0.0)
    after = jnp.where(j < pl.num_programs(1) - 1, next_ref[0:1, :], 0.0)
    up = jnp.where(row == 0, before, pltpu.roll(u, 1, 0))
    dn = jnp.where(row == tt - 1, after, pltpu.roll(u, tt - 1, 0))
    y = w_ref[0:1] * up + w_ref[1:2] * u + w_ref[2:3] * dn + b_ref[...]
    if act:
        y = y * jax.nn.sigmoid(y)
    o_ref[...] = y


def _short_conv(p3, col0, ncols, w, b, act):
    assert SHORT_CONV == 3
    bsz, t, _ = p3.shape
    tt = min(ROW_TILE, t)
    halo = tt // SUBLANE
    last = t // SUBLANE - 1
    gw = GROUP_W
    cur = pl.BlockSpec((None, tt, gw), lambda bi, j, c: (bi, j, col0 + c))
    prev = pl.BlockSpec((None, SUBLANE, gw), lambda bi, j, c: (bi, jnp.maximum(j * halo - 1, 0), col0 + c))
    nxt = pl.BlockSpec((None, SUBLANE, gw), lambda bi, j, c: (bi, jnp.minimum((j + 1) * halo, last), col0 + c))
    return pl.pallas_call(
        functools.partial(_short_conv_kernel, act=act),
        out_shape=jax.ShapeDtypeStruct((bsz, t, ncols * gw), F32),
        grid=(bsz, t // tt, ncols),
        in_specs=[prev, cur, nxt, pl.BlockSpec((SHORT_CONV, gw), lambda bi, j, c: (0, c)),
                  pl.BlockSpec((1, gw), lambda bi, j, c: (0, c))],
        out_specs=pl.BlockSpec((None, tt, gw), lambda bi, j, c: (bi, j, c)),
        compiler_params=pltpu.CompilerParams(dimension_semantics=("arbitrary",) * 3, vmem_limit_bytes=VMEM_LIMIT),
        name="short_conv",
    )(p3, p3, p3, w, b.reshape(1, ncols * gw))


def _pos_embed_2d(rows, d):
    r = jnp.repeat(jnp.arange(rows, dtype=F32), GRID_W)
    col = (jnp.arange(rows * GRID_W) % GRID_W).astype(F32)
    quarter = d // 4
    omega = 1.0 / (POS_BASE ** (jnp.arange(quarter, dtype=F32) / quarter))

    def axis_emb(p):
        ang = p[:, None] * omega[None, :]
        return jnp.concatenate([jnp.sin(ang), jnp.cos(ang)], axis=-1)

    return jnp.concatenate([axis_emb(r), axis_emb(col)], axis=-1)


def _hyena_spectra(L, w1, b1, w2, b2, w3, freq):
    t = jnp.linspace(0.0, 1.0, L, dtype=F32)[:, None]
    w = 2.0 * math.pi * jnp.arange(L, dtype=F32)[:, None] / L
    bands = jnp.linspace(1e-4, HY_BANDS - 1, HY_BANDS, dtype=F32)[None, :]
    feats = jnp.concatenate([t, jnp.cos(bands * w), -jnp.sin(bands * w)], axis=-1)
    z = jnp.sin(freq[0] * (feats @ w1 + b1))
    z = jnp.sin(freq[1] * (z @ w2 + b2))
    h = (z @ w3).reshape(L, HY_ORDER, 2, HY_W)
    max_decay = math.log(HY_DECAY_TARGET) / HY_FAST_DECAY
    min_decay = math.log(HY_DECAY_TARGET) / HY_SLOW_DECAY
    deltas = jnp.abs(jnp.linspace(min_decay, max_decay, HY_W, dtype=F32))
    h = h * jnp.exp(-t[:, :, None, None] * deltas)
    fwd = h[:, :, 0]
    bwd = h[1:, :, 1][::-1]
    l1 = jnp.sum(jnp.abs(fwd), axis=0) + jnp.sum(jnp.abs(bwd), axis=0)
    filt = jnp.concatenate([fwd, jnp.zeros((1, HY_ORDER, HY_W), F32), bwd], axis=0) / l1
    return jnp.fft.rfft(filt, axis=0)


def _fft_conv(u, spec, bias):
    L = u.shape[1]
    y = jnp.fft.irfft(jnp.fft.rfft(u, n=2 * L, axis=1) * spec, n=2 * L, axis=1)[:, :L]
    return y + u * bias


def _hyena(p3, conv_w, conv_b, w1, b1, w2, b2, w3, freq, bias, norm_g):
    u = _short_conv(p3, COL_HY, 3, conv_w, conv_b, False)
    v, x1, x2 = u[..., :HY_W], u[..., HY_W:2 * HY_W], u[..., 2 * HY_W:]
    spec = _hyena_spectra(u.shape[1], w1, b1, w2, b2, w3, freq)
    z = x1 * _fft_conv(v, spec[:, 0], bias[0])
    y = x2 * _fft_conv(z, spec[:, 1], bias[1])
    return _rms(y, norm_g)


FFT_N1 = 128
FFT_N2 = 128
FFT_KTILE = 8
FFT_NTILE = 4096


def _dft_tables(n1, n2):
    n = n1 * n2
    k = np.arange(n1)
    f1 = np.exp(-2j * np.pi * np.outer(k, k) / n1)
    f2 = np.exp(-2j * np.pi * np.outer(np.arange(n2), np.arange(n2)) / n2)
    tw = np.exp(-2j * np.pi * np.outer(np.arange(n1), np.arange(n2)) / n)
    as32 = lambda a: jnp.asarray(np.ascontiguousarray(a), F32)
    f1_fwd = as32(np.concatenate([f1.real, f1.imag], axis=0))
    f1_inv = as32(np.concatenate([f1.real, f1.imag], axis=1) / n)
    f2_inv = as32(np.block([[f2.real, f2.imag], [-f2.imag, f2.real]]))
    return f1_fwd, f1_inv, as32(f2.real), as32(f2.imag), f2_inv, as32(tw.real), as32(tw.imag)


def _stage_kernel(w_ref, x_ref, o_ref):
    o_ref[...] = jnp.dot(w_ref[...].astype(BF16), x_ref[...].astype(BF16),
                         preferred_element_type=F32).astype(o_ref.dtype)


def _stage_matmul(w, x, out_dtype):
    g, k, n = x.shape
    m = w.shape[0]
    tn = FFT_NTILE
    return pl.pallas_call(
        _stage_kernel,
        out_shape=jax.ShapeDtypeStruct((g, m, n), out_dtype),
        grid=(g, n // tn),
        in_specs=[pl.BlockSpec((m, k), lambda b, j: (0, 0)), pl.BlockSpec((None, k, tn), lambda b, j: (b, 0, j))],
        out_specs=pl.BlockSpec((None, m, tn), lambda b, j: (b, 0, j)),
        compiler_params=pltpu.CompilerParams(dimension_semantics=("arbitrary", "arbitrary"),
                                             vmem_limit_bytes=VMEM_LIMIT),
        name="dft_stage",
    )(w, x)


def _twiddled_f2(f2r, f2i, tr, ti):
    gr = f2r * tr - f2i * ti
    gi = f2r * ti + f2i * tr
    return jnp.concatenate([jnp.concatenate([gr, -gi], axis=1), jnp.concatenate([gi, gr], axis=1)], axis=0)


def _spectrum_kernel(a_ref, f2r_ref, f2i_ref, tr_ref, ti_ref, x_ref):
    n2 = f2r_ref.shape[0]
    tr = tr_ref[...]
    ti = ti_ref[...]
    for i in range(a_ref.shape[0]):
        g = _twiddled_f2(f2r_ref[...], f2i_ref[...], tr[i:i + 1], ti[i:i + 1])
        x_ref[i] = jnp.dot(g.astype(BF16), a_ref[i], preferred_element_type=F32)


def _conv_mid_kernel(a_ref, h_ref, f2r_ref, f2i_ref, f2inv_ref, tr_ref, ti_ref, z_ref):
    n2 = f2r_ref.shape[0]
    tr = tr_ref[...]
    ti = ti_ref[...]
    tr_col = tr.T
    ti_col = ti.T
    f2inv = f2inv_ref[...].astype(BF16)
    for i in range(a_ref.shape[0]):
        g = _twiddled_f2(f2r_ref[...], f2i_ref[...], tr[i:i + 1], ti[i:i + 1])
        x = jnp.dot(g.astype(BF16), a_ref[i], preferred_element_type=F32)
        xr, xi = x[:n2], x[n2:]
        hr, hi = h_ref[i, :n2], h_ref[i, n2:]
        y = jnp.concatenate([hr * xr - hi * xi, hr * xi + hi * xr], axis=0)
        w = jnp.dot(f2inv, y.astype(BF16), preferred_element_type=F32)
        wr, wi = w[:n2], w[n2:]
        cr, ci = tr_col[:, i:i + 1], ti_col[:, i:i + 1]
        z_ref[i] = jnp.concatenate([cr * wr + ci * wi, cr * wi - ci * wr], axis=0).astype(z_ref.dtype)


def _dft_mid_specs(g, c):
    n1, n2, kt = FFT_N1, FFT_N2, FFT_KTILE
    blk = pl.BlockSpec((None, kt, 2 * n2, c), lambda b, j: (b, j, 0, 0))
    const = lambda r, cc: pl.BlockSpec((r, cc), lambda b, j: (0, 0))
    twid = pl.BlockSpec((kt, n2), lambda b, j: (j, 0))
    params = pltpu.CompilerParams(dimension_semantics=("arbitrary", "arbitrary"), vmem_limit_bytes=VMEM_LIMIT)
    return blk, const, twid, params, (g, n1 // kt)


def _to_k1_major(a2d, c):
    g = a2d.shape[0]
    return a2d.reshape(g, 2, FFT_N1, FFT_N2, c).transpose(0, 2, 1, 3, 4).reshape(g, FFT_N1, 2 * FFT_N2, c)


def _filter_spectrum(filt, tables):
    g, n, c = filt.shape
    f1_fwd, _, f2r, f2i, _, twr, twi = tables
    a = _stage_matmul(f1_fwd, filt.reshape(g, FFT_N1, FFT_N2 * c), BF16)
    blk, const, twid, params, grid = _dft_mid_specs(g, c)
    return pl.pallas_call(
        _spectrum_kernel,
        out_shape=jax.ShapeDtypeStruct((g, FFT_N1, 2 * FFT_N2, c), F32),
        grid=grid,
        in_specs=[blk, const(FFT_N2, FFT_N2), const(FFT_N2, FFT_N2), twid, twid],
        out_specs=blk, compiler_params=params, name="dft_spectrum",
    )(_to_k1_major(a, c), f2r, f2i, twr, twi)


def _long_conv(u, spec, tables):
    g, l, c = u.shape
    f1_fwd, f1_inv, f2r, f2i, f2inv, twr, twi = tables
    half = l // FFT_N2
    a = _stage_matmul(f1_fwd[:, :half], u.reshape(g, half, FFT_N2 * c), BF16)
    blk, const, twid, params, grid = _dft_mid_specs(g, c)
    hspec = pl.BlockSpec((FFT_KTILE, 2 * FFT_N2, c), lambda b, j: (j, 0, 0))
    z = pl.pallas_call(
        _conv_mid_kernel,
        out_shape=jax.ShapeDtypeStruct((g, FFT_N1, 2 * FFT_N2, c), BF16),
        grid=grid,
        in_specs=[blk, hspec, const(FFT_N2, FFT_N2), const(FFT_N2, FFT_N2), const(2 * FFT_N2, 2 * FFT_N2),
                  twid, twid],
        out_specs=blk, compiler_params=params, name="dft_conv_mid",
    )(_to_k1_major(a, c), spec, f2r, f2i, f2inv, twr, twi)
    z2d = z.reshape(g, FFT_N1, 2, FFT_N2, c).transpose(0, 2, 1, 3, 4).reshape(g, 2 * FFT_N1, FFT_N2 * c)
    y = _stage_matmul(f1_inv[:half], z2d, F32)
    return y.reshape(g, l, c)


def _hyena_filters(L, w1, b1, w2, b2, w3, freq):
    t = jnp.linspace(0.0, 1.0, L, dtype=F32)[:, None]
    w = 2.0 * math.pi * jnp.arange(L, dtype=F32)[:, None] / L
    bands = jnp.linspace(1e-4, HY_BANDS - 1, HY_BANDS, dtype=F32)[None, :]
    feats = jnp.concatenate([t, jnp.cos(bands * w), -jnp.sin(bands * w)], axis=-1)
    max_decay = math.log(HY_DECAY_TARGET) / HY_FAST_DECAY
    min_decay = math.log(HY_DECAY_TARGET) / HY_SLOW_DECAY
    deltas = jnp.abs(jnp.linspace(min_decay, max_decay, HY_W, dtype=F32))
    w3d = w3.reshape(w3.shape[0], HY_ORDER, 2, HY_W)

    def side(f, tt, direction):
        z = jnp.sin(freq[0] * (f @ w1 + b1))
        z = jnp.sin(freq[1] * (z @ w2 + b2))
        h = (z @ w3d[:, :, direction].reshape(w3.shape[0], HY_ORDER * HY_W)).reshape(-1, HY_ORDER, HY_W)
        return h * jnp.exp(-tt[:, :, None] * deltas)

    fwd = side(feats, t, 0)
    bwd = side(feats[::-1], t[::-1], 1)[:L - 1]
    l1 = jnp.sum(jnp.abs(fwd), axis=0) + jnp.sum(jnp.abs(bwd), axis=0)
    return jnp.concatenate([fwd, jnp.zeros((1, HY_ORDER, HY_W), F32), bwd], axis=0) / l1


def _hyena_long(p3, conv_w, conv_b, w1, b1, w2, b2, w3, freq, bias, norm_g):
    u = _short_conv(p3, COL_HY, 3, conv_w, conv_b, False)
    v, x1, x2 = u[..., :HY_W], u[..., HY_W:2 * HY_W], u[..., 2 * HY_W:]
    L = u.shape[1]
    assert 2 * L == FFT_N1 * FFT_N2
    tables = _dft_tables(FFT_N1, FFT_N2)
    filt = _hyena_filters(L, w1, b1, w2, b2, w3, freq)
    spec = _filter_spectrum(jnp.moveaxis(filt, 1, 0), tables)
    z = x1 * (_long_conv(v, spec[0], tables) + v * bias[0])
    y = x2 * (_long_conv(z, spec[1], tables) + z * bias[1])
    return _rms(y, norm_g)


def _glr_direction(q, k, v, g, st_ref, reverse, nh):
    tc, hk = k.shape
    hv = v.shape[1]
    c = CHUNK_GATED
    hi = lax.Precision.HIGHEST
    ti = lax.broadcasted_iota(jnp.int32, (tc, tc), 0)
    tj = lax.broadcasted_iota(jnp.int32, (tc, tc), 1)
    same = (ti // c) == (tj // c)
    seen = (tj >= ti) if reverse else (tj <= ti)
    bcum = jnp.dot(jnp.where(same, jnp.where(seen, 1.0, 0.0), 0.0), g, precision=hi, preferred_element_type=F32)
    btot = jnp.dot(jnp.where(same, 1.0, 0.0), g, precision=hi, preferred_element_type=F32)
    qd = q * jnp.exp(bcum)
    kd = k * jnp.exp(btot - bcum)
    dec = jnp.exp(btot)
    head_sum = jnp.where(lax.broadcasted_iota(jnp.int32, (hk, hv), 0) // (hk // nh)
                         == lax.broadcasted_iota(jnp.int32, (hk, hv), 1) // (hv // nh), 1.0, 0.0).astype(BF16)
    in_chunk = lax.broadcasted_iota(jnp.int32, (tc, hk), 0) % c
    o = jnp.zeros((tc, hv), F32)
    for lag in range(c):
        if lag == 0:
            ks, bs, vs = k, bcum, v
        else:
            shift = tc - lag if reverse else lag
            ks, bs, vs = pltpu.roll(k, shift, 0), pltpu.roll(bcum, shift, 0), pltpu.roll(v, shift, 0)
        valid = (in_chunk + lag <= c - 1) if reverse else (in_chunk >= lag)
        x = q * ks * jnp.exp(jnp.where(valid, bcum - bs, NEG_INF))
        o = o + jnp.dot(x.astype(BF16), head_sum, preferred_element_type=F32) * vs
    head_mask = (lax.broadcasted_iota(jnp.int32, (hv, hk), 0) // (hv // nh)
                 == lax.broadcasted_iota(jnp.int32, (hv, hk), 1) // (hk // nh))
    st = st_ref[...]
    nch = tc // c
    outs = [None] * nch
    for ci in (range(nch - 1, -1, -1) if reverse else range(nch)):
        sl = slice(ci * c, (ci + 1) * c)
        outs[ci] = lax.dot_general(qd[sl].astype(BF16), st.astype(BF16), (((1,), (1,)), ((), ())),
                                   preferred_element_type=F32)
        ds = lax.dot_general(v[sl].astype(BF16), kd[sl].astype(BF16), (((0,), (0,)), ((), ())),
                             preferred_element_type=F32)
        st = st * dec[ci * c:ci * c + 1] + jnp.where(head_mask, ds, 0.0)
    st_ref[...] = st
    return o + jnp.concatenate(outs, axis=0)


def _log_sigmoid(z):
    return jnp.minimum(z, 0.0) - jnp.log1p(jnp.exp(-jnp.abs(z)))


def _hgrn2_kernel(qf_ref, if_ref, zf_ref, qb_ref, ib_ref, zb_ref, lb_ref, s0f_ref, s0b_ref,
                  of_ref, ob_ref, sf_ref, sb_ref, stf, stb):
    j = pl.program_id(1)

    @pl.when(j == 0)
    def _():
        stf[...] = s0f_ref[...]
        stb[...] = s0b_ref[...]

    one_minus_lb, log_lb, log_ub = lb_ref[0:1], lb_ref[1:2], lb_ref[2:3]

    def gate(z):
        return one_minus_lb * jax.nn.sigmoid(-z), jnp.logaddexp(log_lb, log_ub + _log_sigmoid(z))

    silu = lambda a: a * jax.nn.sigmoid(a)
    k_f, g_f = gate(zf_ref[...])
    k_b, g_b = gate(zb_ref[...])
    of_ref[...] = _glr_direction(silu(qf_ref[...]), k_f, if_ref[...], g_f, stf, False, HG_H)
    ob_ref[...] = _glr_direction(silu(qb_ref[...]), k_b, ib_ref[...], g_b, stb, True, HG_H)

    @pl.when(j == pl.num_programs(1) - 1)
    def _():
        sf_ref[...] = stf[...]
        sb_ref[...] = stb[...]


def _gla_kernel(qkf_ref, vf_ref, nf_ref, qkb_ref, vb_ref, nb_ref, aup_ref, ab_ref, s0f_ref, s0b_ref,
                of_ref, ob_ref, sf_ref, sb_ref, stf, stb):
    j = pl.program_id(1)

    @pl.when(j == 0)
    def _():
        stf[...] = s0f_ref[...]
        stb[...] = s0b_ref[...]

    def gate(narrow, idx):
        a = narrow[:, idx * GLA_RANK:(idx + 1) * GLA_RANK]
        lin = jnp.dot(a.astype(BF16), aup_ref[idx].astype(BF16), preferred_element_type=F32) + ab_ref[idx]
        return _log_sigmoid(lin) / GLA_NORMALIZER

    qk_f = qkf_ref[...]
    qk_b = qkb_ref[...]
    of_ref[...] = _glr_direction(qk_f[:, :GLA_KW] * GLA_DK ** -0.5, qk_f[:, GLA_KW:], vf_ref[...],
                                 gate(nf_ref[...], 0), stf, False, GLA_H)
    ob_ref[...] = _glr_direction(qk_b[:, :GLA_KW] * GLA_DK ** -0.5, qk_b[:, GLA_KW:], vb_ref[...],
                                 gate(nb_ref[...], 1), stb, True, GLA_H)

    @pl.when(j == pl.num_programs(1) - 1)
    def _():
        sf_ref[...] = stf[...]
        sb_ref[...] = stb[...]


def _glr_call(kernel_fn, p3, fwd_cols, bwd_cols, consts, s0_f, s0_b, hk, hv, name):
    bsz, t, _ = p3.shape
    tc = GLR_TILE
    nsb = t // tc
    fwd = lambda w, c: pl.BlockSpec((None, tc, w), lambda b, j: (b, j, c))
    bwd = lambda w, c: pl.BlockSpec((None, tc, w), lambda b, j: (b, nsb - 1 - j, c))
    whole = lambda a: pl.BlockSpec(a.shape, lambda b, j: (0,) * a.ndim)
    st = pl.BlockSpec((None, hv, hk), lambda b, j: (b, 0, 0))
    out_f = pl.BlockSpec((None, tc, hv), lambda b, j: (b, j, 0))
    out_b = pl.BlockSpec((None, tc, hv), lambda b, j: (b, nsb - 1 - j, 0))
    return pl.pallas_call(
        kernel_fn,
        out_shape=[jax.ShapeDtypeStruct((bsz, t, hv), F32), jax.ShapeDtypeStruct((bsz, t, hv), F32),
                   jax.ShapeDtypeStruct((bsz, hv, hk), F32), jax.ShapeDtypeStruct((bsz, hv, hk), F32)],
        grid=(bsz, nsb),
        in_specs=([fwd(w, c) for w, c in fwd_cols] + [bwd(w, c) for w, c in bwd_cols]
                  + [whole(a) for a in consts] + [st, st]),
        out_specs=[out_f, out_b, st, st],
        scratch_shapes=[pltpu.VMEM((hv, hk), F32), pltpu.VMEM((hv, hk), F32)],
        compiler_params=pltpu.CompilerParams(dimension_semantics=("arbitrary", "arbitrary"),
                                             vmem_limit_bytes=VMEM_LIMIT),
        name=name,
    )(*([p3] * (len(fwd_cols) + len(bwd_cols))), *consts, s0_f, s0_b)


def _hgrn2_seq(p3, lb, s0_f, s0_b):
    lb_rows = jnp.stack([1.0 - lb, jnp.log(lb), jnp.log1p(-lb)])
    cols = lambda z: [(GROUP_W, COL_HG_Q), (GROUP_W, COL_HG_I), (GROUP_W, z)]
    o_f, o_b, s_f, s_b = _glr_call(_hgrn2_kernel, p3, cols(COL_HG_ZF), cols(COL_HG_ZB), [lb_rows],
                                   s0_f, s0_b, HG_W, HG_W, "hgrn2_recurrence")
    return (o_f, o_b), s_f, s_b


def _hgrn2(pc3, pl3, lb):
    s0 = jnp.zeros((pl3.shape[0], HG_W, HG_W), F32)
    oc, s_f, s_b = _hgrn2_seq(pc3, lb, s0, s0)
    o, _, _ = _hgrn2_seq(pl3, lb, s_f, s_b)
    return oc, o


def _gla_seq(p3, a_up, a_b, s0_f, s0_b):
    cols = [(GROUP_W, COL_GLA_QK), (GROUP_W, COL_GLA_V), (LANE, COL_NARROW)]
    o_f, o_b, s_f, s_b = _glr_call(_gla_kernel, p3, cols, cols, [a_up, a_b.reshape(2, 1, GLA_KW)],
                                   s0_f, s0_b, GLA_KW, GLA_VW, "gla_recurrence")
    return (o_f, o_b), s_f, s_b


def _gla(pc3, pl3, a_up, a_b):
    s0 = jnp.zeros((pl3.shape[0], GLA_VW, GLA_KW), F32)
    oc, s_f, s_b = _gla_seq(pc3, a_up, a_b, s0, s0)
    o, _, _ = _gla_seq(pl3, a_up, a_b, s_f, s_b)
    return oc, o


def _mlstm_direction(q, k, v, igx, lfx, s_ref, n_ref, m_ref, reverse, nh):
    tc, w = q.shape
    seg = w // nh
    assert tc == seg
    hi = lax.Precision.HIGHEST
    ti = lax.broadcasted_iota(jnp.int32, (tc, tc), 0)
    tj = lax.broadcasted_iota(jnp.int32, (tc, tc), 1)
    seen = (tj >= ti) if reverse else (tj <= ti)
    b = jnp.dot(jnp.where(seen, 1.0, 0.0), lfx, precision=hi, preferred_element_type=F32)
    bl = b[0:1] if reverse else b[tc - 1:tc]
    a = bl - b + igx
    ma = jnp.max(a, axis=0, keepdims=True)
    kw = jnp.exp(a - ma) * k
    s_prev = s_ref[...]
    n_prev = n_ref[...]
    m_prev = m_ref[...]
    lane = lax.broadcasted_iota(jnp.int32, (tc, w), 1)
    row = lax.broadcasted_iota(jnp.int32, (tc, w), 0)
    same_head = (lax.broadcasted_iota(jnp.int32, (w, w), 0) // seg
                 == lax.broadcasted_iota(jnp.int32, (w, w), 1) // seg)
    kexp = jnp.where(same_head, jnp.concatenate([k] * nh, axis=0), 0.0)
    vexp = jnp.where(same_head, jnp.concatenate([v] * nh, axis=0), 0.0)
    scores = lax.dot_general(q.astype(BF16), kexp.astype(BF16), (((1,), (1,)), ((), ())),
                             preferred_element_type=F32)
    s_lane = lane % seg
    by_src = jnp.sum(jnp.where(s_lane == row, igx - b, 0.0), axis=0, keepdims=True)
    ok = (s_lane >= row) if reverse else (s_lane <= row)
    dmat = jnp.where(ok, b + by_src, NEG_INF)
    inter = b + m_prev
    head_of_lane = lane // seg
    seg_max = jnp.full((tc, w), NEG_INF, F32)
    for h in range(nh):
        in_h = head_of_lane == h
        seg_max = jnp.where(in_h, jnp.max(jnp.where(in_h, dmat, NEG_INF), axis=1, keepdims=True), seg_max)
    m_t = jnp.maximum(inter, seg_max)
    wq = jnp.exp(dmat - m_t) * scores
    w_int = jnp.exp(inter - m_t)
    head_sum = jnp.where(same_head, 1.0, 0.0).astype(BF16)
    num = (jnp.dot(wq.astype(BF16), vexp.astype(BF16), preferred_element_type=F32)
           + w_int * jnp.dot(q.astype(BF16), s_prev.astype(BF16), preferred_element_type=F32))
    den = (jnp.dot(wq.astype(BF16), head_sum, preferred_element_type=F32)
           + w_int * jnp.dot((q * n_prev).astype(BF16), head_sum, preferred_element_type=F32))
    h_out = num / jnp.maximum(jnp.abs(den), jnp.exp(-m_t))
    m_new = jnp.maximum(bl + m_prev, ma)
    d_old = jnp.exp(bl + m_prev - m_new)
    d_new = jnp.exp(ma - m_new)
    ds = lax.dot_general(kw.astype(BF16), v.astype(BF16), (((0,), (0,)), ((), ())), preferred_element_type=F32)
    s_ref[...] = d_old * s_prev + d_new * jnp.where(same_head, ds, 0.0)
    n_ref[...] = d_old * n_prev + d_new * jnp.sum(kw, axis=0, keepdims=True)
    m_ref[...] = m_new
    return h_out


def _mlstm_kernel(qf_ref, kf_ref, vf_ref, igf_ref, lff_ref, qb_ref, kb_ref, vb_ref, igb_ref, lfb_ref,
                  s0f_ref, n0f_ref, m0f_ref, s0b_ref, n0b_ref, m0b_ref,
                  hf_ref, hb_ref, sf_ref, nf_ref, mf_ref, sb_ref, nb_ref, mb_ref,
                  s_f, n_f, m_f, s_b, n_b, m_b, *, nh):
    j = pl.program_id(1)

    @pl.when(j == 0)
    def _():
        s_f[...] = s0f_ref[...]
        n_f[...] = n0f_ref[...]
        m_f[...] = m0f_ref[...]
        s_b[...] = s0b_ref[...]
        n_b[...] = n0b_ref[...]
        m_b[...] = m0b_ref[...]

    hf_ref[...] = _mlstm_direction(qf_ref[...], kf_ref[...], vf_ref[...], igf_ref[...], lff_ref[...],
                                   s_f, n_f, m_f, False, nh)
    hb_ref[...] = _mlstm_direction(qb_ref[...], kb_ref[...], vb_ref[...], igb_ref[...], lfb_ref[...],
                                   s_b, n_b, m_b, True, nh)

    @pl.when(j == pl.num_programs(1) - 1)
    def _():
        sf_ref[...] = s_f[...]
        nf_ref[...] = n_f[...]
        mf_ref[...] = m_f[...]
        sb_ref[...] = s_b[...]
        nb_ref[...] = n_b[...]
        mb_ref[...] = m_b[...]


def _mlstm_bidir(q, k, v, ig_f, lf_f, ig_b, lf_b, st_f, st_b, nh):
    bsz, t, w = q.shape
    tc = CHUNK_ML
    nsb = t // tc
    fwd = pl.BlockSpec((None, tc, w), lambda b, j: (b, j, 0))
    bwd = pl.BlockSpec((None, tc, w), lambda b, j: (b, nsb - 1 - j, 0))
    mat = pl.BlockSpec((None, w, w), lambda b, j: (b, 0, 0))
    vec = pl.BlockSpec((None, 1, w), lambda b, j: (b, 0, 0))
    sds = jax.ShapeDtypeStruct
    state_shapes = [sds((bsz, w, w), F32), sds((bsz, 1, w), F32), sds((bsz, 1, w), F32)]
    outs = pl.pallas_call(
        functools.partial(_mlstm_kernel, nh=nh),
        out_shape=[sds((bsz, t, w), F32), sds((bsz, t, w), F32)] + state_shapes + state_shapes,
        grid=(bsz, nsb),
        in_specs=[fwd] * 5 + [bwd] * 5 + [mat, vec, vec] * 2,
        out_specs=[fwd, bwd] + [mat, vec, vec] * 2,
        scratch_shapes=[pltpu.VMEM((w, w), F32), pltpu.VMEM((1, w), F32), pltpu.VMEM((1, w), F32)] * 2,
        compiler_params=pltpu.CompilerParams(dimension_semantics=("arbitrary", "arbitrary"),
                                             vmem_limit_bytes=VMEM_LIMIT),
        name="mlstm_recurrence",
    )(q, k, v, ig_f, lf_f, q, k, v, ig_b, lf_b, *st_f, *st_b)
    return outs[0], outs[1], tuple(outs[2:5]), tuple(outs[5:8])


def _mlstm_seq(p3, conv_w, conv_b, gate_b, st_f, st_b):
    qk = _short_conv(p3, COL_ML_Q, 2, conv_w, conv_b, True)
    v = p3[..., COL_ML_V * GROUP_W:(COL_ML_V + 1) * GROUP_W]
    bsz, t, _ = p3.shape
    gates = p3[..., NARROW_ML_GATES:NARROW_ML_GATES + 4 * ML_H]
    gt = gates.reshape(bsz, t, 4, ML_H) + gate_b
    expand = lambda a: jnp.repeat(a, ML_DH, axis=-1)
    h_f, h_b, fin_f, fin_b = _mlstm_bidir(
        qk[..., :ML_W], qk[..., ML_W:] * ML_DH ** -0.5, v,
        expand(gt[:, :, 0]), expand(jax.nn.log_sigmoid(gt[:, :, 1])),
        expand(gt[:, :, 2]), expand(jax.nn.log_sigmoid(gt[:, :, 3])), st_f, st_b, ML_H)
    return (h_f, h_b), fin_f, fin_b


def _mlstm(pc3, pl3, conv_w, conv_b, gate_b):
    bsz = pl3.shape[0]
    st0 = (jnp.zeros((bsz, ML_W, ML_W), F32), jnp.zeros((bsz, 1, ML_W), F32), jnp.zeros((bsz, 1, ML_W), F32))
    hc, st_f, st_b = _mlstm_seq(pc3, conv_w, conv_b, gate_b, st0, st0)
    h, _, _ = _mlstm_seq(pl3, conv_w, conv_b, gate_b, st_f, st_b)
    return hc, h


def _router_kernel(h_ref, wt_ref, b_ref, eidx_ref, wsel_ref, cnt_ref):
    i = pl.program_id(0)
    tm = h_ref.shape[0]
    ne = wt_ref.shape[0]
    per_group = ne // N_EXPERT_GROUPS
    logits = lax.dot_general(wt_ref[...], h_ref[...], (((1,), (1,)), ((), ())),
                             preferred_element_type=F32, precision=lax.Precision.HIGHEST)
    s = jax.nn.sigmoid(logits)
    sel = s + b_ref[...]
    row = lax.broadcasted_iota(jnp.int32, (ne, tm), 0)
    gs = []
    for g in range(N_EXPERT_GROUPS):
        blk = sel[g * per_group:(g + 1) * per_group]
        r = lax.broadcasted_iota(jnp.int32, blk.shape, 0)
        m1 = jnp.max(blk, axis=0, keepdims=True)
        i1 = jnp.min(jnp.where(blk == m1, r, per_group), axis=0, keepdims=True)
        m2 = jnp.max(jnp.where(r == i1, NEG_INF, blk), axis=0, keepdims=True)
        gs.append(m1 + m2)
    grp = jnp.concatenate(gs, axis=0)
    grow = lax.broadcasted_iota(jnp.int32, grp.shape, 0)
    gsel = jnp.zeros(grp.shape, F32)
    for _ in range(TOPK_GROUPS):
        m = jnp.max(grp, axis=0, keepdims=True)
        gi = jnp.min(jnp.where(grp == m, grow, N_EXPERT_GROUPS), axis=0, keepdims=True)
        hit = grow == gi
        gsel = jnp.where(hit, 1.0, gsel)
        grp = jnp.where(hit, NEG_INF, grp)
    masked = jnp.concatenate(
        [jnp.where(gsel[g:g + 1] > 0.0, sel[g * per_group:(g + 1) * per_group], NEG_INF)
         for g in range(N_EXPERT_GROUPS)], axis=0)
    eis, ws = [], []
    picked = jnp.zeros((ne, tm), F32)
    for _ in range(TOP_K):
        m = jnp.max(masked, axis=0, keepdims=True)
        ei = jnp.min(jnp.where(masked == m, row, ne), axis=0, keepdims=True)
        hit = row == ei
        ws.append(jnp.sum(jnp.where(hit, s, 0.0), axis=0, keepdims=True))
        eis.append(ei)
        picked = jnp.where(hit, 1.0, picked)
        masked = jnp.where(hit, NEG_INF, masked)
    w = jnp.concatenate(ws, axis=0)
    eidx_ref[...] = jnp.concatenate(eis, axis=0)
    wsel_ref[...] = w / jnp.sum(w, axis=0, keepdims=True) * ROUTED_SCALE
    tot = jnp.dot(picked.astype(BF16), jnp.ones((tm, LANE), BF16), preferred_element_type=F32)

    @pl.when(i == 0)
    def _():
        cnt_ref[...] = jnp.zeros_like(cnt_ref)

    cnt_ref[...] += tot


def _pos_kernel(eidx_ref, base_ref, pos_ref, carry_ref):
    i = pl.program_id(0)
    tm = eidx_ref.shape[1]
    ne = base_ref.shape[0]

    @pl.when(i == 0)
    def _():
        carry_ref[...] = jnp.zeros_like(carry_ref)

    eidx = eidx_ref[...]
    row = lax.broadcasted_iota(jnp.int32, (ne, tm), 0)
    picked = jnp.zeros((ne, tm), F32)
    for k in range(TOP_K):
        picked = jnp.where(row == eidx[k:k + 1], 1.0, picked)
    pb = picked.astype(BF16)
    before = jnp.where(lax.broadcasted_iota(jnp.int32, (tm, tm), 0) < lax.broadcasted_iota(jnp.int32, (tm, tm), 1),
                       1.0, 0.0).astype(BF16)
    rank = jnp.dot(pb, before, preferred_element_type=F32)
    tot = jnp.dot(pb, jnp.ones((tm, LANE), BF16), preferred_element_type=F32)
    dest = rank + (base_ref[...] + carry_ref[:, 0:1])
    pos = [jnp.sum(jnp.where(row == eidx[k:k + 1], dest, 0.0), axis=0, keepdims=True) for k in range(TOP_K)]
    pos_ref[...] = jnp.concatenate(pos, axis=0).astype(jnp.int32)
    carry_ref[...] += tot


def _pack_bf16_pairs(x):
    half = x.shape[1] // 2
    bits = lambda a: pltpu.bitcast(a.astype(BF16).astype(F32), jnp.int32)
    return (bits(x[:, half:]) & -65536) | lax.shift_right_logical(bits(x[:, :half]), 16)


def _unpack_bf16_pairs(w):
    return pltpu.bitcast(lax.shift_left(w, 16), F32), pltpu.bitcast(w & -65536, F32)


def _dispatch_kernel(zstart_ref, zlen_ref, nused_ref, pos_ref, h_ref, xs_ref, packed, zeros, sem, zsem, *,
                     n_blocks):
    tm = h_ref.shape[0]
    bm = zeros.shape[0]
    packed[...] = _pack_bf16_pairs(h_ref[...])

    @pl.when(pl.program_id(0) == 0)
    def _():
        zeros[...] = jnp.zeros_like(zeros)

        def zero_copy(start, size):
            return pltpu.make_async_copy(zeros.at[pl.ds(0, size)], xs_ref.at[pl.ds(start, size)], zsem)

        def pieces(e, act):
            start = zstart_ref[e]
            rem = zlen_ref[e]
            ragged = rem & (SUBLANE - 1)
            for q in range(SUBLANE - 1):
                @pl.when(q < ragged)
                def _(q=q):
                    act(zero_copy(start + q, 1))

            start = pl.multiple_of(start + ragged, SUBLANE)
            size = bm // 2
            while size >= SUBLANE:
                @pl.when((rem & size) != 0)
                def _(start=start, size=size):
                    act(zero_copy(start, size))

                start = pl.multiple_of(start + (rem & size), SUBLANE)
                size //= 2

        def loop(act):
            def per_expert(e, carry):
                pieces(e, act)
                return carry

            def per_block(b, carry):
                act(zero_copy(pl.multiple_of(b * bm, bm), bm))
                return carry

            lax.fori_loop(0, zstart_ref.shape[0], per_expert, 0)
            lax.fori_loop(nused_ref[0], n_blocks, per_block, 0)

        loop(lambda cp: cp.start())
        loop(lambda cp: cp.wait())

    def row_copy(n, k):
        return pltpu.make_async_copy(packed.at[pl.ds(n, 1)], xs_ref.at[pl.ds(pos_ref[k, n], 1)], sem)

    def issue(n, carry):
        for k in range(TOP_K):
            row_copy(n, k).start()
        return carry

    def drain(n, carry):
        for k in range(TOP_K):
            row_copy(n, k).wait()
        return carry

    lax.fori_loop(0, tm, issue, 0)
    lax.fori_loop(0, tm, drain, 0)


def _moe_ffn_kernel(blk_e_ref, nused_ref, x_ref, wgu_ref, wdn_ref, o_ref, wgu_s, wdn_s):
    i = pl.program_id(0)
    e = blk_e_ref[i]
    e_prev = blk_e_ref[jnp.maximum(i - 1, 0)]

    @pl.when(i < nused_ref[0])
    def _():
        @pl.when((i == 0) | (e != e_prev))
        def _():
            wgu_s[...] = wgu_ref[...].astype(BF16)
            wdn_s[...] = wdn_ref[...].astype(BF16)

        x_lo, x_hi = _unpack_bf16_pairs(x_ref[...])
        half = x_lo.shape[1]
        au = (jnp.dot(x_lo.astype(BF16), wgu_s[:half, :], preferred_element_type=F32)
              + jnp.dot(x_hi.astype(BF16), wgu_s[half:, :], preferred_element_type=F32))
        a = au[:, :EXPERT_FF]
        u = au[:, EXPERT_FF:]
        h = (a * jax.nn.sigmoid(a)) * u
        o_ref[...] = _pack_bf16_pairs(jnp.dot(h.astype(BF16), wdn_s[...], preferred_element_type=F32))

    @pl.when(i >= nused_ref[0])
    def _():
        o_ref[...] = jnp.zeros_like(o_ref)


def _combine_kernel(pos_ref, w_ref, t_ref, x_ref, sgu_ref, sdn_ref, gain_ref, g2_ref, y_hbm, o_ref, buf, sem):
    tm = o_ref.shape[0]

    def row_copy(n, k):
        return pltpu.make_async_copy(y_hbm.at[pl.ds(pos_ref[k, n], 1)], buf.at[k, pl.ds(n, 1)], sem)

    def issue(n, carry):
        for k in range(TOP_K):
            row_copy(n, k).start()
        return carry

    def drain(n, carry):
        for k in range(TOP_K):
            row_copy(n, k).wait()
        return carry

    lax.fori_loop(0, tm, issue, 0)
    ff = sdn_ref.shape[0]
    au = jnp.dot(t_ref[...].astype(BF16), sgu_ref[...], preferred_element_type=F32)
    a, u = au[:, :ff], au[:, ff:]
    f = jnp.dot(((a * jax.nn.sigmoid(a)) * u).astype(BF16), sdn_ref[...], preferred_element_type=F32)
    lax.fori_loop(0, tm, drain, 0)
    r_lo, r_hi = None, None
    for k in range(TOP_K):
        y_lo, y_hi = _unpack_bf16_pairs(buf[k])
        wk = w_ref[:, k:k + 1]
        r_lo = y_lo * wk if r_lo is None else r_lo + y_lo * wk
        r_hi = y_hi * wk if r_hi is None else r_hi + y_hi * wk
    f = f + jnp.concatenate([r_lo, r_hi], axis=1)
    o_ref[...] = x_ref[...] + g2_ref[...] * (f * lax.rsqrt(jnp.mean(f * f, axis=-1, keepdims=True) + EPS)
                                             * gain_ref[...])


def _moe(t, x_res, router_w, router_b, w_gu, w_down, sh_gu, sh_down, layer, gain, g2_rows, rows_per_gate):
    n, d = t.shape
    ne = router_w.shape[1]
    ff2 = w_gu.shape[-1]
    params = pltpu.CompilerParams(dimension_semantics=("arbitrary",), vmem_limit_bytes=VMEM_LIMIT)
    tm = ROUTER_TILE
    eidx, wsel, cnt = pl.pallas_call(
        _router_kernel,
        out_shape=[jax.ShapeDtypeStruct((TOP_K, n), jnp.int32), jax.ShapeDtypeStruct((TOP_K, n), F32),
                   jax.ShapeDtypeStruct((ne, LANE), F32)],
        grid=(n // tm,),
        in_specs=[pl.BlockSpec((tm, d), lambda i: (i, 0)), pl.BlockSpec((ne, d), lambda i: (0, 0)),
                  pl.BlockSpec((ne, 1), lambda i: (0, 0))],
        out_specs=[pl.BlockSpec((TOP_K, tm), lambda i: (0, i)), pl.BlockSpec((TOP_K, tm), lambda i: (0, i)),
                   pl.BlockSpec((ne, LANE), lambda i: (0, 0))],
        compiler_params=params, name="moe_router",
    )(t, router_w.T, router_b.reshape(ne, 1))
    bm = MOE_ROWS
    counts = cnt[:, 0].astype(jnp.int32)
    padded = (counts + bm - 1) // bm * bm
    pad_end = jnp.cumsum(padded)
    pad_start = pad_end - padded
    n_blocks = (n * TOP_K + ne * (bm - 1)) // bm + 1
    blk_first = jnp.arange(n_blocks, dtype=jnp.int32) * bm
    blk_e = jnp.minimum(jnp.sum((pad_end[None, :] <= blk_first[:, None]).astype(jnp.int32), axis=1), ne - 1)
    n_used = (pad_end[-1] // bm).astype(jnp.int32).reshape(1)
    pos = pl.pallas_call(
        _pos_kernel,
        out_shape=jax.ShapeDtypeStruct((TOP_K, n), jnp.int32),
        grid=(n // tm,),
        in_specs=[pl.BlockSpec((TOP_K, tm), lambda i: (0, i)), pl.BlockSpec((ne, 1), lambda i: (0, 0))],
        out_specs=pl.BlockSpec((TOP_K, tm), lambda i: (0, i)),
        scratch_shapes=[pltpu.VMEM((ne, LANE), F32)],
        compiler_params=params, name="moe_positions",
    )(eidx, pad_start.astype(F32).reshape(ne, 1))
    ts = SCATTER_TILE
    p = n_blocks * bm
    pos_spec = pl.BlockSpec((TOP_K, ts), lambda i: (0, i), memory_space=pltpu.SMEM)
    dp = d // 2
    xs = pl.pallas_call(
        functools.partial(_dispatch_kernel, n_blocks=n_blocks),
        out_shape=jax.ShapeDtypeStruct((p, dp), jnp.int32),
        grid_spec=pltpu.PrefetchScalarGridSpec(
            num_scalar_prefetch=3,
            grid=(n // ts,),
            in_specs=[pl.BlockSpec((TOP_K, ts), lambda i, *_: (0, i), memory_space=pltpu.SMEM),
                      pl.BlockSpec((ts, d), lambda i, *_: (i, 0))],
            out_specs=pl.BlockSpec(memory_space=pl.ANY),
            scratch_shapes=[pltpu.VMEM((ts, dp), jnp.int32), pltpu.VMEM((bm, dp), jnp.int32),
                            pltpu.SemaphoreType.DMA, pltpu.SemaphoreType.DMA],
        ),
        compiler_params=params, name="moe_dispatch",
    )(pad_start + counts, padded - counts, n_used, pos, t)

    def x_map(i, blk_e, nused):
        return (jnp.minimum(i, nused[0] - 1), 0)

    def w_map(i, blk_e, nused):
        return (layer, blk_e[i], 0, 0)

    y_p = pl.pallas_call(
        _moe_ffn_kernel,
        out_shape=jax.ShapeDtypeStruct((p, dp), jnp.int32),
        grid_spec=pltpu.PrefetchScalarGridSpec(
            num_scalar_prefetch=2,
            grid=(n_blocks,),
            in_specs=[pl.BlockSpec((bm, dp), x_map),
                      pl.BlockSpec((None, None, d, ff2), w_map),
                      pl.BlockSpec((None, None, ff2 // 2, d), w_map)],
            out_specs=pl.BlockSpec((bm, dp), lambda i, blk_e, nused: (i, 0)),
            scratch_shapes=[pltpu.VMEM((d, ff2), BF16), pltpu.VMEM((ff2 // 2, d), BF16)],
        ),
        compiler_params=params, name="moe_expert_ffn",
    )(blk_e, n_used, xs, w_gu, w_down)
    rows = pl.BlockSpec((ts, d), lambda i: (i, 0))
    whole = lambda a: pl.BlockSpec(a.shape, lambda i: (0,) * a.ndim)
    last_gate = g2_rows.shape[0] - 1
    gate_spec = pl.BlockSpec((None, 1, d), lambda i: (jnp.minimum(i // (rows_per_gate // ts), last_gate), 0, 0))
    consts = [sh_gu.astype(BF16), sh_down.astype(BF16), gain.reshape(1, d)]
    return pl.pallas_call(
        _combine_kernel,
        out_shape=jax.ShapeDtypeStruct((n, d), F32),
        grid=(n // ts,),
        in_specs=([pos_spec, pl.BlockSpec((ts, TOP_K), lambda i: (i, 0)), rows, rows]
                  + [whole(a) for a in consts] + [gate_spec, pl.BlockSpec(memory_space=pl.ANY)]),
        out_specs=rows,
        scratch_shapes=[pltpu.VMEM((TOP_K, ts, dp), jnp.int32), pltpu.SemaphoreType.DMA],
        compiler_params=params, name="moe_combine",
    )(pos, wsel.T, t, x_res, *consts, g2_rows, y_p)


def kernel(x, c, ctx, c_ctx, ada_w, ada_b, norm_g, w_in, w_out, hy_conv_w, hy_conv_b, hy_ffn_w1, hy_ffn_b1, hy_ffn_w2, hy_ffn_b2, hy_ffn_w3, hy_freq, hy_bias, hy_norm, hg_lb_logits, hg_norm, gla_a_up, gla_a_b, gla_norm, ml_conv_w, ml_conv_b, ml_gate_b, ml_norm, router_w, router_b, exp_w_gu, exp_w_down, sh_w_gu, sh_w_down):
    bsz, seq, d = x.shape
    n_ctx = ctx.shape[1]
    depth = ada_w.shape[0]
    rows = seq // GRID_W
    x = x + _pos_embed_2d(rows, d)[None]
    xc = ctx
    lb_cum = jnp.cumsum(jax.nn.softmax(hg_lb_logits, axis=0), axis=0)
    lower_bounds = lb_cum - lb_cum[0:1]
    for l in range(depth):
        with_ctx = l < depth - 1
        mod = (jax.nn.silu(c) @ ada_w[l] + ada_b[l])[:, None, :]
        mod_c = jax.nn.silu(c_ctx) @ ada_w[l] + ada_b[l]
        sh1, sc1, g1, sh2, sc2, g2 = jnp.split(mod, 6, axis=-1)
        csh1, csc1, cg1, csh2, csc2, cg2 = jnp.split(mod_c, 6, axis=-1)
        w_in_l = _arrange_w_in(w_in[l])
        ctx_rows = lambda a: jnp.broadcast_to(a.reshape(1, 1, d), (bsz, 1, d))
        pl3 = _in_proj(x, norm_g[l, 0], sc1, sh1, w_in_l)
        pc3 = _in_proj(xc, norm_g[l, 0], ctx_rows(csc1), ctx_rows(csh1), w_in_l)
        hy_args = (hy_conv_w[l], hy_conv_b[l], hy_ffn_w1[l], hy_ffn_b1[l], hy_ffn_w2[l], hy_ffn_b2[l],
                   hy_ffn_w3[l], hy_freq[l], hy_bias[l], hy_norm[l])
        y_hy = _hyena_long(pl3, *hy_args)
        oc_hg, o_hg = _hgrn2(pc3, pl3, lower_bounds[l])
        oc_gla, o_gla = _gla(pc3, pl3, gla_a_up[l], gla_a_b[l])
        hc_ml, h_ml = _mlstm(pc3, pl3, ml_conv_w[l], ml_conv_b[l], ml_gate_b[l])
        head_gains = jnp.stack([hg_norm[l], gla_norm[l], ml_norm[l]])
        out_args = (w_out[l], head_gains, norm_g[l, 1], norm_g[l, 2])
        x, h = _mixer_out(y_hy, o_hg, o_gla, h_ml, pl3, x, *out_args, jnp.concatenate([g1, sc2, sh2], axis=1))
        moe_args = (router_w[l], router_b[l], exp_w_gu, exp_w_down, sh_w_gu[l], sh_w_down[l], l, norm_g[l, 3])
        if with_ctx:
            mod_ctx = jnp.broadcast_to(jnp.stack([cg1, csc2, csh2])[None], (bsz, 3, d))
            xc, hc = _mixer_out(_hyena(pc3, *hy_args), oc_hg, oc_gla, hc_ml, pc3, xc, *out_args, mod_ctx)
            tokens = lambda a, ac: jnp.concatenate([a.reshape(bsz * seq, d), ac.reshape(bsz * n_ctx, d)], axis=0)
            gates = jnp.concatenate([g2, cg2.reshape(1, 1, d)], axis=0)
            x_all = _moe(tokens(h, hc), tokens(x, xc), *moe_args, gates, seq)
            x = x_all[:bsz * seq].reshape(bsz, seq, d)
            xc = x_all[bsz * seq:].reshape(bsz, n_ctx, d)
        else:
            x = _moe(h.reshape(bsz * seq, d), x.reshape(bsz * seq, d), *moe_args, g2, seq).reshape(bsz, seq, d)
    return x
```

```python
import functools
import math

import jax
import jax.numpy as jnp
import numpy as np
from jax import lax
from jax.experimental import pallas as pl
from jax.experimental.pallas import tpu as pltpu

F32 = jnp.float32
BF16 = jnp.bfloat16

D_MODEL = 1024
GRID_W = 64
EPS = 1e-6
POS_BASE = 10000.0
GROUP_W = D_MODEL // 4
SHORT_CONV = 3
HY_W = GROUP_W
HY_ORDER = 2
HY_EMB = 33
HY_BANDS = (HY_EMB - 1) // 2
HY_FAST_DECAY = 0.3
HY_SLOW_DECAY = 1.5
HY_DECAY_TARGET = 1e-2
HG_H = 4
HG_W = GROUP_W
HG_DK = HG_W // HG_H
GLA_H = 4
GLA_KW = GROUP_W // 2
GLA_VW = GROUP_W
GLA_DK = GLA_KW // GLA_H
GLA_DV = GLA_VW // GLA_H
GLA_RANK = 16
GLA_NORMALIZER = 16.0
ML_H = 4
ML_W = GROUP_W
ML_DH = ML_W // ML_H
CHUNK_GATED = 16
CHUNK_ML = 64
N_EXPERTS = 256
TOP_K = 8
N_EXPERT_GROUPS = 8
TOPK_GROUPS = 4
EXPERT_FF = 256
ROUTED_SCALE = 2.5
IN_SPLITS = (HY_W, HY_W, HY_W,
             HG_W, HG_W, HG_W, HG_W, HG_W,
             GLA_KW, GLA_KW, GLA_VW, GLA_RANK, GLA_RANK, GLA_VW,
             ML_W, ML_W, ML_W, 4 * ML_H, ML_W)
P_ORDER = (0, 1, 2, 3, 4, 5, 6, 7, 8, 9, 10, 13, 14, 15, 16, 18, 11, 12, 17)
COL_HY = 0
COL_HG_Q, COL_HG_I, COL_HG_ZF, COL_HG_ZB, COL_HG_G = 3, 4, 5, 6, 7
COL_GLA_QK, COL_GLA_V, COL_GLA_R = 8, 9, 10
COL_ML_Q, COL_ML_K, COL_ML_V, COL_ML_O = 11, 12, 13, 14
N_WIDE = 15

LANE = 128
SUBLANE = 8
ROW_TILE = 512
MOE_ROWS = 256
ROUTER_TILE = 256
SCATTER_TILE = 256
GLR_TILE = 128
VMEM_LIMIT = 56 * 1024 * 1024
NEG_INF = float("-inf")
COL_NARROW = N_WIDE * GROUP_W // LANE
NARROW_ML_GATES = N_WIDE * GROUP_W + 2 * GLA_RANK
P_WIDTH = N_WIDE * GROUP_W + LANE


def _arrange_w_in(w):
    offs = np.concatenate([[0], np.cumsum(IN_SPLITS)])
    cols = [w[:, offs[i]:offs[i + 1]] for i in P_ORDER]
    used = sum(IN_SPLITS)
    return jnp.concatenate(cols + [jnp.zeros((w.shape[0], P_WIDTH - used), w.dtype)], axis=1).astype(BF16)


def _in_proj_kernel(x_ref, g_ref, sc_ref, sh_ref, w_ref, o_ref):
    x = x_ref[...]
    y = x * lax.rsqrt(jnp.mean(x * x, axis=-1, keepdims=True) + EPS) * g_ref[...]
    h = y * (1.0 + sc_ref[...]) + sh_ref[...]
    o_ref[...] = jnp.dot(h.astype(BF16), w_ref[...], preferred_element_type=F32)


def _in_proj(x, gain, scale, shift, w):
    g, r, d = x.shape
    n = w.shape[1]
    tm = min(ROW_TILE, r)
    assert r % tm == 0
    return pl.pallas_call(
        _in_proj_kernel,
        out_shape=jax.ShapeDtypeStruct((g, r, n), F32),
        grid=(g, r // tm),
        in_specs=[pl.BlockSpec((None, tm, d), lambda b, i: (b, i, 0)),
                  pl.BlockSpec((1, d), lambda b, i: (0, 0)),
                  pl.BlockSpec((None, 1, d), lambda b, i: (b, 0, 0)),
                  pl.BlockSpec((None, 1, d), lambda b, i: (b, 0, 0)),
                  pl.BlockSpec((d, n), lambda b, i: (0, 0))],
        out_specs=pl.BlockSpec((None, tm, n), lambda b, i: (b, i, 0)),
        compiler_params=pltpu.CompilerParams(dimension_semantics=("arbitrary", "arbitrary"),
                                             vmem_limit_bytes=VMEM_LIMIT),
        name="input_projection",
    )(x, gain.reshape(1, d), scale, shift, w)


def _mixer_out_kernel(hy_ref, hgf_ref, hgb_ref, glf_ref, glb_ref, mlf_ref, mlb_ref, ghg_ref, ggl_ref, gml_ref,
                      x_ref, w_ref, hn_ref, n1_ref, n2_ref, mod_ref, xo_ref, h_ref):
    gw = hy_ref.shape[-1]
    seg = jnp.where(lax.broadcasted_iota(jnp.int32, (gw, gw), 0) // HG_DK
                    == lax.broadcasted_iota(jnp.int32, (gw, gw), 1) // HG_DK, 1.0 / HG_DK, 0.0)

    def head_norm(o, gain):
        ms = jnp.dot(o * o, seg, precision=lax.Precision.HIGHEST, preferred_element_type=F32)
        return o * lax.rsqrt(ms + EPS) * gain

    silu = lambda a: a * jax.nn.sigmoid(a)
    groups = (hy_ref[...],
              head_norm(hgf_ref[...] + hgb_ref[...], hn_ref[0:1]) * silu(ghg_ref[...]),
              head_norm(glf_ref[...] + glb_ref[...], hn_ref[1:2]) * silu(ggl_ref[...]),
              jax.nn.sigmoid(gml_ref[...]) * head_norm(mlf_ref[...] + mlb_ref[...], hn_ref[2:3]))
    y = None
    for i, part in enumerate(groups):
        term = jnp.dot(part.astype(BF16), w_ref[i * gw:(i + 1) * gw, :], preferred_element_type=F32)
        y = term if y is None else y + term
    rms = lambda a, g: a * lax.rsqrt(jnp.mean(a * a, axis=-1, keepdims=True) + EPS) * g
    x = x_ref[...] + mod_ref[0:1] * rms(y, n1_ref[...])
    xo_ref[...] = x
    h_ref[...] = rms(x, n2_ref[...]) * (1.0 + mod_ref[1:2]) + mod_ref[2:3]


def _mixer_out(y_hy, o_hg, o_gla, h_ml, p3, x, w_out, head_gains, gain1, gain2, mod):
    assert HG_DK == GLA_DV == ML_DH and HG_H == GLA_H == ML_H
    bsz, t, d = x.shape
    gw = GROUP_W
    tm = min(ROW_TILE, t)
    part = pl.BlockSpec((None, tm, gw), lambda b, i: (b, i, 0))
    gate = lambda c: pl.BlockSpec((None, tm, gw), lambda b, i: (b, i, c))
    full = pl.BlockSpec((None, tm, d), lambda b, i: (b, i, 0))
    whole = lambda a: pl.BlockSpec(a.shape, lambda b, i: (0,) * a.ndim)
    consts = [w_out.astype(BF16), head_gains, gain1.reshape(1, d), gain2.reshape(1, d)]
    return pl.pallas_call(
        _mixer_out_kernel,
        out_shape=[jax.ShapeDtypeStruct((bsz, t, d), F32), jax.ShapeDtypeStruct((bsz, t, d), F32)],
        grid=(bsz, t // tm),
        in_specs=([part] * 7 + [gate(COL_HG_G), gate(COL_GLA_R), gate(COL_ML_O), full]
                  + [whole(a) for a in consts] + [pl.BlockSpec((None, 3, d), lambda b, i: (b, 0, 0))]),
        out_specs=[full, full],
        compiler_params=pltpu.CompilerParams(dimension_semantics=("arbitrary", "arbitrary"),
                                             vmem_limit_bytes=VMEM_LIMIT),
        name="mixer_output",
    )(y_hy, *o_hg, *o_gla, *h_ml, p3, p3, p3, x, *consts, mod)


def _rms(x, g):
    return x * lax.rsqrt(jnp.mean(x * x, axis=-1, keepdims=True) + EPS) * g


def _short_conv_kernel(prev_ref, cur_ref, next_ref, w_ref, b_ref, o_ref, *, act):
    j = pl.program_id(1)
    u = cur_ref[...]
    tt = u.shape[0]
    row = lax.broadcasted_iota(jnp.int32, u.shape, 0)
    before = jnp.where(j > 0, prev_ref[SUBLANE - 1:SUBLANE, :], 0.0)
    after = jnp.where(j < pl.num_programs(1) - 1, next_ref[0:1, :], 0.0)
    up = jnp.where(row == 0, before, pltpu.roll(u, 1, 0))
    dn = jnp.where(row == tt - 1, after, pltpu.roll(u, tt - 1, 0))
    y = w_ref[0:1] * up + w_ref[1:2] * u + w_ref[2:3] * dn + b_ref[...]
    if act:
        y = y * jax.nn.sigmoid(y)
    o_ref[...] = y


def _short_conv(p3, col0, ncols, w, b, act):
    assert SHORT_CONV == 3
    bsz, t, _ = p3.shape
    tt = min(ROW_TILE, t)
    halo = tt // SUBLANE
    last = t // SUBLANE - 1
    gw = GROUP_W
    cur = pl.BlockSpec((None, tt, gw), lambda bi, j, c: (bi, j, col0 + c))
    prev = pl.BlockSpec((None, SUBLANE, gw), lambda bi, j, c: (bi, jnp.maximum(j * halo - 1, 0), col0 + c))
    nxt = pl.BlockSpec((None, SUBLANE, gw), lambda bi, j, c: (bi, jnp.minimum((j + 1) * halo, last), col0 + c))
    return pl.pallas_call(
        functools.partial(_short_conv_kernel, act=act),
        out_shape=jax.ShapeDtypeStruct((bsz, t, ncols * gw), F32),
        grid=(bsz, t // tt, ncols),
        in_specs=[prev, cur, nxt, pl.BlockSpec((SHORT_CONV, gw), lambda bi, j, c: (0, c)),
                  pl.BlockSpec((1, gw), lambda bi, j, c: (0, c))],
        out_specs=pl.BlockSpec((None, tt, gw), lambda bi, j, c: (bi, j, c)),
        compiler_params=pltpu.CompilerParams(dimension_semantics=("arbitrary",) * 3, vmem_limit_bytes=VMEM_LIMIT),
        name="short_conv",
    )(p3, p3, p3, w, b.reshape(1, ncols * gw))


def _pos_embed_2d(rows, d):
    r = jnp.repeat(jnp.arange(rows, dtype=F32), GRID_W)
    col = (jnp.arange(rows * GRID_W) % GRID_W).astype(F32)
    quarter = d // 4
    omega = 1.0 / (POS_BASE ** (jnp.arange(quarter, dtype=F32) / quarter))

    def axis_emb(p):
        ang = p[:, None] * omega[None, :]
        return jnp.concatenate([jnp.sin(ang), jnp.cos(ang)], axis=-1)

    return jnp.concatenate([axis_emb(r), axis_emb(col)], axis=-1)


def _hyena_spectra(L, w1, b1, w2, b2, w3, freq):
    t = jnp.linspace(0.0, 1.0, L, dtype=F32)[:, None]
    w = 2.0 * math.pi * jnp.arange(L, dtype=F32)[:, None] / L
    bands = jnp.linspace(1e-4, HY_BANDS - 1, HY_BANDS, dtype=F32)[None, :]
    feats = jnp.concatenate([t, jnp.cos(bands * w), -jnp.sin(bands * w)], axis=-1)
    z = jnp.sin(freq[0] * (feats @ w1 + b1))
    z = jnp.sin(freq[1] * (z @ w2 + b2))
    h = (z @ w3).reshape(L, HY_ORDER, 2, HY_W)
    max_decay = math.log(HY_DECAY_TARGET) / HY_FAST_DECAY
    min_decay = math.log(HY_DECAY_TARGET) / HY_SLOW_DECAY
    deltas = jnp.abs(jnp.linspace(min_decay, max_decay, HY_W, dtype=F32))
    h = h * jnp.exp(-t[:, :, None, None] * deltas)
    fwd = h[:, :, 0]
    bwd = h[1:, :, 1][::-1]
    l1 = jnp.sum(jnp.abs(fwd), axis=0) + jnp.sum(jnp.abs(bwd), axis=0)
    filt = jnp.concatenate([fwd, jnp.zeros((1, HY_ORDER, HY_W), F32), bwd], axis=0) / l1
    return jnp.fft.rfft(filt, axis=0)


def _fft_conv(u, spec, bias):
    L = u.shape[1]
    y = jnp.fft.irfft(jnp.fft.rfft(u, n=2 * L, axis=1) * spec, n=2 * L, axis=1)[:, :L]
    return y + u * bias


def _hyena(p3, conv_w, conv_b, w1, b1, w2, b2, w3, freq, bias, norm_g):
    u = _short_conv(p3, COL_HY, 3, conv_w, conv_b, False)
    v, x1, x2 = u[..., :HY_W], u[..., HY_W:2 * HY_W], u[..., 2 * HY_W:]
    spec = _hyena_spectra(u.shape[1], w1, b1, w2, b2, w3, freq)
    z = x1 * _fft_conv(v, spec[:, 0], bias[0])
    y = x2 * _fft_conv(z, spec[:, 1], bias[1])
    return _rms(y, norm_g)


FFT_N1 = 128
FFT_N2 = 128
FFT_KTILE = 8
FFT_NTILE = 4096


def _dft_tables(n1, n2):
    n = n1 * n2
    k = np.arange(n1)
    f1 = np.exp(-2j * np.pi * np.outer(k, k) / n1)
    f2 = np.exp(-2j * np.pi * np.outer(np.arange(n2), np.arange(n2)) / n2)
    tw = np.exp(-2j * np.pi * np.outer(np.arange(n1), np.arange(n2)) / n)
    as32 = lambda a: jnp.asarray(np.ascontiguousarray(a), F32)
    f1_fwd = as32(np.concatenate([f1.real, f1.imag], axis=0))
    f1_inv = as32(np.concatenate([f1.real, f1.imag], axis=1) / n)
    f2_inv = as32(np.block([[f2.real, f2.imag], [-f2.imag, f2.real]]))
    return f1_fwd, f1_inv, as32(f2.real), as32(f2.imag), f2_inv, as32(tw.real), as32(tw.imag)


def _stage_kernel(w_ref, x_ref, o_ref):
    o_ref[...] = jnp.dot(w_ref[...].astype(BF16), x_ref[...].astype(BF16),
                         preferred_element_type=F32).astype(o_ref.dtype)


def _stage_matmul(w, x, out_dtype):
    g, k, n = x.shape
    m = w.shape[0]
    tn = FFT_NTILE
    return pl.pallas_call(
        _stage_kernel,
        out_shape=jax.ShapeDtypeStruct((g, m, n), out_dtype),
        grid=(g, n // tn),
        in_specs=[pl.BlockSpec((m, k), lambda b, j: (0, 0)), pl.BlockSpec((None, k, tn), lambda b, j: (b, 0, j))],
        out_specs=pl.BlockSpec((None, m, tn), lambda b, j: (b, 0, j)),
        compiler_params=pltpu.CompilerParams(dimension_semantics=("arbitrary", "arbitrary"),
                                             vmem_limit_bytes=VMEM_LIMIT),
        name="dft_stage",
    )(w, x)


def _twiddled_f2(f2r, f2i, tr, ti):
    gr = f2r * tr - f2i * ti
    gi = f2r * ti + f2i * tr
    return jnp.concatenate([jnp.concatenate([gr, -gi], axis=1), jnp.concatenate([gi, gr], axis=1)], axis=0)


def _spectrum_kernel(a_ref, f2r_ref, f2i_ref, tr_ref, ti_ref, x_ref):
    n2 = f2r_ref.shape[0]
    tr = tr_ref[...]
    ti = ti_ref[...]
    for i in range(a_ref.shape[0]):
        g = _twiddled_f2(f2r_ref[...], f2i_ref[...], tr[i:i + 1], ti[i:i + 1])
        x_ref[i] = jnp.dot(g.astype(BF16), a_ref[i], preferred_element_type=F32)


def _conv_mid_kernel(a_ref, h_ref, f2r_ref, f2i_ref, f2inv_ref, tr_ref, ti_ref, z_ref):
    n2 = f2r_ref.shape[0]
    tr = tr_ref[...]
    ti = ti_ref[...]
    tr_col = tr.T
    ti_col = ti.T
    f2inv = f2inv_ref[...].astype(BF16)
    for i in range(a_ref.shape[0]):
        g = _twiddled_f2(f2r_ref[...], f2i_ref[...], tr[i:i + 1], ti[i:i + 1])
        x = jnp.dot(g.astype(BF16), a_ref[i], preferred_element_type=F32)
        xr, xi = x[:n2], x[n2:]
        hr, hi = h_ref[i, :n2], h_ref[i, n2:]
        y = jnp.concatenate([hr * xr - hi * xi, hr * xi + hi * xr], axis=0)
        w = jnp.dot(f2inv, y.astype(BF16), preferred_element_type=F32)
        wr, wi = w[:n2], w[n2:]
        cr, ci = tr_col[:, i:i + 1], ti_col[:, i:i + 1]
        z_ref[i] = jnp.concatenate([cr * wr + ci * wi, cr * wi - ci * wr], axis=0).astype(z_ref.dtype)


def _dft_mid_specs(g, c):
    n1, n2, kt = FFT_N1, FFT_N2, FFT_KTILE
    blk = pl.BlockSpec((None, kt, 2 * n2, c), lambda b, j: (b, j, 0, 0))
    const = lambda r, cc: pl.BlockSpec((r, cc), lambda b, j: (0, 0))
    twid = pl.BlockSpec((kt, n2), lambda b, j: (j, 0))
    params = pltpu.CompilerParams(dimension_semantics=("arbitrary", "arbitrary"), vmem_limit_bytes=VMEM_LIMIT)
    return blk, const, twid, params, (g, n1 // kt)


def _to_k1_major(a2d, c):
    g = a2d.shape[0]
    return a2d.reshape(g, 2, FFT_N1, FFT_N2, c).transpose(0, 2, 1, 3, 4).reshape(g, FFT_N1, 2 * FFT_N2, c)


def _filter_spectrum(filt, tables):
    g, n, c = filt.shape
    f1_fwd, _, f2r, f2i, _, twr, twi = tables
    a = _stage_matmul(f1_fwd, filt.reshape(g, FFT_N1, FFT_N2 * c), BF16)
    blk, const, twid, params, grid = _dft_mid_specs(g, c)
    return pl.pallas_call(
        _spectrum_kernel,
        out_shape=jax.ShapeDtypeStruct((g, FFT_N1, 2 * FFT_N2, c), F32),
        grid=grid,
        in_specs=[blk, const(FFT_N2, FFT_N2), const(FFT_N2, FFT_N2), twid, twid],
        out_specs=blk, compiler_params=params, name="dft_spectrum",
    )(_to_k1_major(a, c), f2r, f2i, twr, twi)


def _long_conv(u, spec, tables):
    g, l, c = u.shape
    f1_fwd, f1_inv, f2r, f2i, f2inv, twr, twi = tables
    half = l // FFT_N2
    a = _stage_matmul(f1_fwd[:, :half], u.reshape(g, half, FFT_N2 * c), BF16)
    blk, const, twid, params, grid = _dft_mid_specs(g, c)
    hspec = pl.BlockSpec((FFT_KTILE, 2 * FFT_N2, c), lambda b, j: (j, 0, 0))
    z = pl.pallas_call(
        _conv_mid_kernel,
        out_shape=jax.ShapeDtypeStruct((g, FFT_N1, 2 * FFT_N2, c), BF16),
        grid=grid,
        in_specs=[blk, hspec, const(FFT_N2, FFT_N2), const(FFT_N2, FFT_N2), const(2 * FFT_N2, 2 * FFT_N2),
                  twid, twid],
        out_specs=blk, compiler_params=params, name="dft_conv_mid",
    )(_to_k1_major(a, c), spec, f2r, f2i, f2inv, twr, twi)
    z2d = z.reshape(g, FFT_N1, 2, FFT_N2, c).transpose(0, 2, 1, 3, 4).reshape(g, 2 * FFT_N1, FFT_N2 * c)
    y = _stage_matmul(f1_inv[:half], z2d, F32)
    return y.reshape(g, l, c)


def _hyena_filters(L, w1, b1, w2, b2, w3, freq):
    t = jnp.linspace(0.0, 1.0, L, dtype=F32)[:, None]
    w = 2.0 * math.pi * jnp.arange(L, dtype=F32)[:, None] / L
    bands = jnp.linspace(1e-4, HY_BANDS - 1, HY_BANDS, dtype=F32)[None, :]
    feats = jnp.concatenate([t, jnp.cos(bands * w), -jnp.sin(bands * w)], axis=-1)
    max_decay = math.log(HY_DECAY_TARGET) / HY_FAST_DECAY
    min_decay = math.log(HY_DECAY_TARGET) / HY_SLOW_DECAY
    deltas = jnp.abs(jnp.linspace(min_decay, max_decay, HY_W, dtype=F32))
    w3d = w3.reshape(w3.shape[0], HY_ORDER, 2, HY_W)

    def side(f, tt, direction):
        z = jnp.sin(freq[0] * (f @ w1 + b1))
        z = jnp.sin(freq[1] * (z @ w2 + b2))
        h = (z @ w3d[:, :, direction].reshape(w3.shape[0], HY_ORDER * HY_W)).reshape(-1, HY_ORDER, HY_W)
        return h * jnp.exp(-tt[:, :, None] * deltas)

    fwd = side(feats, t, 0)
    bwd = side(feats[::-1], t[::-1], 1)[:L - 1]
    l1 = jnp.sum(jnp.abs(fwd), axis=0) + jnp.sum(jnp.abs(bwd), axis=0)
    return jnp.concatenate([fwd, jnp.zeros((1, HY_ORDER, HY_W), F32), bwd], axis=0) / l1


def _hyena_long(p3, conv_w, conv_b, w1, b1, w2, b2, w3, freq, bias, norm_g):
    u = _short_conv(p3, COL_HY, 3, conv_w, conv_b, False)
    v, x1, x2 = u[..., :HY_W], u[..., HY_W:2 * HY_W], u[..., 2 * HY_W:]
    L = u.shape[1]
    assert 2 * L == FFT_N1 * FFT_N2
    tables = _dft_tables(FFT_N1, FFT_N2)
    filt = _hyena_filters(L, w1, b1, w2, b2, w3, freq)
    spec = _filter_spectrum(jnp.moveaxis(filt, 1, 0), tables)
    z = x1 * (_long_conv(v, spec[0], tables) + v * bias[0])
    y = x2 * (_long_conv(z, spec[1], tables) + z * bias[1])
    return _rms(y, norm_g)


def _glr_direction(q, k, v, g, st_ref, reverse, nh):
    tc, hk = k.shape
    hv = v.shape[1]
    c = CHUNK_GATED
    hi = lax.Precision.HIGHEST
    ti = lax.broadcasted_iota(jnp.int32, (tc, tc), 0)
    tj = lax.broadcasted_iota(jnp.int32, (tc, tc), 1)
    same = (ti // c) == (tj // c)
    seen = (tj >= ti) if reverse else (tj <= ti)
    bcum = jnp.dot(jnp.where(same, jnp.where(seen, 1.0, 0.0), 0.0), g, precision=hi, preferred_element_type=F32)
    btot = jnp.dot(jnp.where(same, 1.0, 0.0), g, precision=hi, preferred_element_type=F32)
    qd = q * jnp.exp(bcum)
    kd = k * jnp.exp(btot - bcum)
    dec = jnp.exp(btot)
    head_sum = jnp.where(lax.broadcasted_iota(jnp.int32, (hk, hv), 0) // (hk // nh)
                         == lax.broadcasted_iota(jnp.int32, (hk, hv), 1) // (hv // nh), 1.0, 0.0).astype(BF16)
    in_chunk = lax.broadcasted_iota(jnp.int32, (tc, hk), 0) % c
    o = jnp.zeros((tc, hv), F32)
    for lag in range(c):
        if lag == 0:
            ks, bs, vs = k, bcum, v
        else:
            shift = tc - lag if reverse else lag
            ks, bs, vs = pltpu.roll(k, shift, 0), pltpu.roll(bcum, shift, 0), pltpu.roll(v, shift, 0)
        valid = (in_chunk + lag <= c - 1) if reverse else (in_chunk >= lag)
        x = q * ks * jnp.exp(jnp.where(valid, bcum - bs, NEG_INF))
        o = o + jnp.dot(x.astype(BF16), head_sum, preferred_element_type=F32) * vs
    head_mask = (lax.broadcasted_iota(jnp.int32, (hv, hk), 0) // (hv // nh)
                 == lax.broadcasted_iota(jnp.int32, (hv, hk), 1) // (hk // nh))
    st = st_ref[...]
    nch = tc // c
    outs = [None] * nch
    for ci in (range(nch - 1, -1, -1) if reverse else range(nch)):
        sl = slice(ci * c, (ci + 1) * c)
        outs[ci] = lax.dot_general(qd[sl].astype(BF16), st.astype(BF16), (((1,), (1,)), ((), ())),
                                   preferred_element_type=F32)
        ds = lax.dot_general(v[sl].astype(BF16), kd[sl].astype(BF16), (((0,), (0,)), ((), ())),
                             preferred_element_type=F32)
        st = st * dec[ci * c:ci * c + 1] + jnp.where(head_mask, ds, 0.0)
    st_ref[...] = st
    return o + jnp.concatenate(outs, axis=0)


def _log_sigmoid(z):
    return jnp.minimum(z, 0.0) - jnp.log1p(jnp.exp(-jnp.abs(z)))


def _hgrn2_kernel(qf_ref, if_ref, zf_ref, qb_ref, ib_ref, zb_ref, lb_ref, s0f_ref, s0b_ref,
                  of_ref, ob_ref, sf_ref, sb_ref, stf, stb):
    j = pl.program_id(1)

    @pl.when(j == 0)
    def _():
        stf[...] = s0f_ref[...]
        stb[...] = s0b_ref[...]

    one_minus_lb, log_lb, log_ub = lb_ref[0:1], lb_ref[1:2], lb_ref[2:3]

    def gate(z):
        return one_minus_lb * jax.nn.sigmoid(-z), jnp.logaddexp(log_lb, log_ub + _log_sigmoid(z))

    silu = lambda a: a * jax.nn.sigmoid(a)
    k_f, g_f = gate(zf_ref[...])
    k_b, g_b = gate(zb_ref[...])
    of_ref[...] = _glr_direction(silu(qf_ref[...]), k_f, if_ref[...], g_f, stf, False, HG_H)
    ob_ref[...] = _glr_direction(silu(qb_ref[...]), k_b, ib_ref[...], g_b, stb, True, HG_H)

    @pl.when(j == pl.num_programs(1) - 1)
    def _():
        sf_ref[...] = stf[...]
        sb_ref[...] = stb[...]


def _gla_kernel(qkf_ref, vf_ref, nf_ref, qkb_ref, vb_ref, nb_ref, aup_ref, ab_ref, s0f_ref, s0b_ref,
                of_ref, ob_ref, sf_ref, sb_ref, stf, stb):
    j = pl.program_id(1)

    @pl.when(j == 0)
    def _():
        stf[...] = s0f_ref[...]
        stb[...] = s0b_ref[...]

    def gate(narrow, idx):
        a = narrow[:, idx * GLA_RANK:(idx + 1) * GLA_RANK]
        lin = jnp.dot(a.astype(BF16), aup_ref[idx].astype(BF16), preferred_element_type=F32) + ab_ref[idx]
        return _log_sigmoid(lin) / GLA_NORMALIZER

    qk_f = qkf_ref[...]
    qk_b = qkb_ref[...]
    of_ref[...] = _glr_direction(qk_f[:, :GLA_KW] * GLA_DK ** -0.5, qk_f[:, GLA_KW:], vf_ref[...],
                                 gate(nf_ref[...], 0), stf, False, GLA_H)
    ob_ref[...] = _glr_direction(qk_b[:, :GLA_KW] * GLA_DK ** -0.5, qk_b[:, GLA_KW:], vb_ref[...],
                                 gate(nb_ref[...], 1), stb, True, GLA_H)

    @pl.when(j == pl.num_programs(1) - 1)
    def _():
        sf_ref[...] = stf[...]
        sb_ref[...] = stb[...]


def _glr_call(kernel_fn, p3, fwd_cols, bwd_cols, consts, s0_f, s0_b, hk, hv, name):
    bsz, t, _ = p3.shape
    tc = GLR_TILE
    nsb = t // tc
    fwd = lambda w, c: pl.BlockSpec((None, tc, w), lambda b, j: (b, j, c))
    bwd = lambda w, c: pl.BlockSpec((None, tc, w), lambda b, j: (b, nsb - 1 - j, c))
    whole = lambda a: pl.BlockSpec(a.shape, lambda b, j: (0,) * a.ndim)
    st = pl.BlockSpec((None, hv, hk), lambda b, j: (b, 0, 0))
    out_f = pl.BlockSpec((None, tc, hv), lambda b, j: (b, j, 0))
    out_b = pl.BlockSpec((None, tc, hv), lambda b, j: (b, nsb - 1 - j, 0))
    return pl.pallas_call(
        kernel_fn,
        out_shape=[jax.ShapeDtypeStruct((bsz, t, hv), F32), jax.ShapeDtypeStruct((bsz, t, hv), F32),
                   jax.ShapeDtypeStruct((bsz, hv, hk), F32), jax.ShapeDtypeStruct((bsz, hv, hk), F32)],
        grid=(bsz, nsb),
        in_specs=([fwd(w, c) for w, c in fwd_cols] + [bwd(w, c) for w, c in bwd_cols]
                  + [whole(a) for a in consts] + [st, st]),
        out_specs=[out_f, out_b, st, st],
        scratch_shapes=[pltpu.VMEM((hv, hk), F32), pltpu.VMEM((hv, hk), F32)],
        compiler_params=pltpu.CompilerParams(dimension_semantics=("arbitrary", "arbitrary"),
                                             vmem_limit_bytes=VMEM_LIMIT),
        name=name,
    )(*([p3] * (len(fwd_cols) + len(bwd_cols))), *consts, s0_f, s0_b)


def _hgrn2_seq(p3, lb, s0_f, s0_b):
    lb_rows = jnp.stack([1.0 - lb, jnp.log(lb), jnp.log1p(-lb)])
    cols = lambda z: [(GROUP_W, COL_HG_Q), (GROUP_W, COL_HG_I), (GROUP_W, z)]
    o_f, o_b, s_f, s_b = _glr_call(_hgrn2_kernel, p3, cols(COL_HG_ZF), cols(COL_HG_ZB), [lb_rows],
                                   s0_f, s0_b, HG_W, HG_W, "hgrn2_recurrence")
    return (o_f, o_b), s_f, s_b


def _hgrn2(pc3, pl3, lb):
    s0 = jnp.zeros((pl3.shape[0], HG_W, HG_W), F32)
    oc, s_f, s_b = _hgrn2_seq(pc3, lb, s0, s0)
    o, _, _ = _hgrn2_seq(pl3, lb, s_f, s_b)
    return oc, o


def _gla_seq(p3, a_up, a_b, s0_f, s0_b):
    cols = [(GROUP_W, COL_GLA_QK), (GROUP_W, COL_GLA_V), (LANE, COL_NARROW)]
    o_f, o_b, s_f, s_b = _glr_call(_gla_kernel, p3, cols, cols, [a_up, a_b.reshape(2, 1, GLA_KW)],
                                   s0_f, s0_b, GLA_KW, GLA_VW, "gla_recurrence")
    return (o_f, o_b), s_f, s_b


def _gla(pc3, pl3, a_up, a_b):
    s0 = jnp.zeros((pl3.shape[0], GLA_VW, GLA_KW), F32)
    oc, s_f, s_b = _gla_seq(pc3, a_up, a_b, s0, s0)
    o, _, _ = _gla_seq(pl3, a_up, a_b, s_f, s_b)
    return oc, o


def _mlstm_direction(q, k, v, igx, lfx, s_ref, n_ref, m_ref, reverse, nh):
    tc, w = q.shape
    seg = w // nh
    assert tc == seg
    hi = lax.Precision.HIGHEST
    ti = lax.broadcasted_iota(jnp.int32, (tc, tc), 0)
    tj = lax.broadcasted_iota(jnp.int32, (tc, tc), 1)
    seen = (tj >= ti) if reverse else (tj <= ti)
    b = jnp.dot(jnp.where(seen, 1.0, 0.0), lfx, precision=hi, preferred_element_type=F32)
    bl = b[0:1] if reverse else b[tc - 1:tc]
    a = bl - b + igx
    ma = jnp.max(a, axis=0, keepdims=True)
    kw = jnp.exp(a - ma) * k
    s_prev = s_ref[...]
    n_prev = n_ref[...]
    m_prev = m_ref[...]
    lane = lax.broadcasted_iota(jnp.int32, (tc, w), 1)
    row = lax.broadcasted_iota(jnp.int32, (tc, w), 0)
    same_head = (lax.broadcasted_iota(jnp.int32, (w, w), 0) // seg
                 == lax.broadcasted_iota(jnp.int32, (w, w), 1) // seg)
    kexp = jnp.where(same_head, jnp.concatenate([k] * nh, axis=0), 0.0)
    vexp = jnp.where(same_head, jnp.concatenate([v] * nh, axis=0), 0.0)
    scores = lax.dot_general(q.astype(BF16), kexp.astype(BF16), (((1,), (1,)), ((), ())),
                             preferred_element_type=F32)
    s_lane = lane % seg
    by_src = jnp.sum(jnp.where(s_lane == row, igx - b, 0.0), axis=0, keepdims=True)
    ok = (s_lane >= row) if reverse else (s_lane <= row)
    dmat = jnp.where(ok, b + by_src, NEG_INF)
    inter = b + m_prev
    head_of_lane = lane // seg
    seg_max = jnp.full((tc, w), NEG_INF, F32)
    for h in range(nh):
        in_h = head_of_lane == h
        seg_max = jnp.where(in_h, jnp.max(jnp.where(in_h, dmat, NEG_INF), axis=1, keepdims=True), seg_max)
    m_t = jnp.maximum(inter, seg_max)
    wq = jnp.exp(dmat - m_t) * scores
    w_int = jnp.exp(inter - m_t)
    head_sum = jnp.where(same_head, 1.0, 0.0).astype(BF16)
    num = (jnp.dot(wq.astype(BF16), vexp.astype(BF16), preferred_element_type=F32)
           + w_int * jnp.dot(q.astype(BF16), s_prev.astype(BF16), preferred_element_type=F32))
    den = (jnp.dot(wq.astype(BF16), head_sum, preferred_element_type=F32)
           + w_int * jnp.dot((q * n_prev).astype(BF16), head_sum, preferred_element_type=F32))
    h_out = num / jnp.maximum(jnp.abs(den), jnp.exp(-m_t))
    m_new = jnp.maximum(bl + m_prev, ma)
    d_old = jnp.exp(bl + m_prev - m_new)
    d_new = jnp.exp(ma - m_new)
    ds = lax.dot_general(kw.astype(BF16), v.astype(BF16), (((0,), (0,)), ((), ())), preferred_element_type=F32)
    s_ref[...] = d_old * s_prev + d_new * jnp.where(same_head, ds, 0.0)
    n_ref[...] = d_old * n_prev + d_new * jnp.sum(kw, axis=0, keepdims=True)
    m_ref[...] = m_new
    return h_out


def _mlstm_kernel(qf_ref, kf_ref, vf_ref, igf_ref, lff_ref, qb_ref, kb_ref, vb_ref, igb_ref, lfb_ref,
                  s0f_ref, n0f_ref, m0f_ref, s0b_ref, n0b_ref, m0b_ref,
                  hf_ref, hb_ref, sf_ref, nf_ref, mf_ref, sb_ref, nb_ref, mb_ref,
                  s_f, n_f, m_f, s_b, n_b, m_b, *, nh):
    j = pl.program_id(1)

    @pl.when(j == 0)
    def _():
        s_f[...] = s0f_ref[...]
        n_f[...] = n0f_ref[...]
        m_f[...] = m0f_ref[...]
        s_b[...] = s0b_ref[...]
        n_b[...] = n0b_ref[...]
        m_b[...] = m0b_ref[...]

    hf_ref[...] = _mlstm_direction(qf_ref[...], kf_ref[...], vf_ref[...], igf_ref[...], lff_ref[...],
                                   s_f, n_f, m_f, False, nh)
    hb_ref[...] = _mlstm_direction(qb_ref[...], kb_ref[...], vb_ref[...], igb_ref[...], lfb_ref[...],
                                   s_b, n_b, m_b, True, nh)

    @pl.when(j == pl.num_programs(1) - 1)
    def _():
        sf_ref[...] = s_f[...]
        nf_ref[...] = n_f[...]
        mf_ref[...] = m_f[...]
        sb_ref[...] = s_b[...]
        nb_ref[...] = n_b[...]
        mb_ref[...] = m_b[...]


def _mlstm_bidir(q, k, v, ig_f, lf_f, ig_b, lf_b, st_f, st_b, nh):
    bsz, t, w = q.shape
    tc = CHUNK_ML
    nsb = t // tc
    fwd = pl.BlockSpec((None, tc, w), lambda b, j: (b, j, 0))
    bwd = pl.BlockSpec((None, tc, w), lambda b, j: (b, nsb - 1 - j, 0))
    mat = pl.BlockSpec((None, w, w), lambda b, j: (b, 0, 0))
    vec = pl.BlockSpec((None, 1, w), lambda b, j: (b, 0, 0))
    sds = jax.ShapeDtypeStruct
    state_shapes = [sds((bsz, w, w), F32), sds((bsz, 1, w), F32), sds((bsz, 1, w), F32)]
    outs = pl.pallas_call(
        functools.partial(_mlstm_kernel, nh=nh),
        out_shape=[sds((bsz, t, w), F32), sds((bsz, t, w), F32)] + state_shapes + state_shapes,
        grid=(bsz, nsb),
        in_specs=[fwd] * 5 + [bwd] * 5 + [mat, vec, vec] * 2,
        out_specs=[fwd, bwd] + [mat, vec, vec] * 2,
        scratch_shapes=[pltpu.VMEM((w, w), F32), pltpu.VMEM((1, w), F32), pltpu.VMEM((1, w), F32)] * 2,
        compiler_params=pltpu.CompilerParams(dimension_semantics=("arbitrary", "arbitrary"),
                                             vmem_limit_bytes=VMEM_LIMIT),
        name="mlstm_recurrence",
    )(q, k, v, ig_f, lf_f, q, k, v, ig_b, lf_b, *st_f, *st_b)
    return outs[0], outs[1], tuple(outs[2:5]), tuple(outs[5:8])


def _mlstm_seq(p3, conv_w, conv_b, gate_b, st_f, st_b):
    qk = _short_conv(p3, COL_ML_Q, 2, conv_w, conv_b, True)
    v = p3[..., COL_ML_V * GROUP_W:(COL_ML_V + 1) * GROUP_W]
    bsz, t, _ = p3.shape
    gates = p3[..., NARROW_ML_GATES:NARROW_ML_GATES + 4 * ML_H]
    gt = gates.reshape(bsz, t, 4, ML_H) + gate_b
    expand = lambda a: jnp.repeat(a, ML_DH, axis=-1)
    h_f, h_b, fin_f, fin_b = _mlstm_bidir(
        qk[..., :ML_W], qk[..., ML_W:] * ML_DH ** -0.5, v,
        expand(gt[:, :, 0]), expand(jax.nn.log_sigmoid(gt[:, :, 1])),
        expand(gt[:, :, 2]), expand(jax.nn.log_sigmoid(gt[:, :, 3])), st_f, st_b, ML_H)
    return (h_f, h_b), fin_f, fin_b


def _mlstm(pc3, pl3, conv_w, conv_b, gate_b):
    bsz = pl3.shape[0]
    st0 = (jnp.zeros((bsz, ML_W, ML_W), F32), jnp.zeros((bsz, 1, ML_W), F32), jnp.zeros((bsz, 1, ML_W), F32))
    hc, st_f, st_b = _mlstm_seq(pc3, conv_w, conv_b, gate_b, st0, st0)
    h, _, _ = _mlstm_seq(pl3, conv_w, conv_b, gate_b, st_f, st_b)
    return hc, h


def _router_kernel(h_ref, wt_ref, b_ref, eidx_ref, wsel_ref, cnt_ref):
    i = pl.program_id(0)
    tm = h_ref.shape[0]
    ne = wt_ref.shape[0]
    per_group = ne // N_EXPERT_GROUPS
    logits = lax.dot_general(wt_ref[...], h_ref[...], (((1,), (1,)), ((), ())),
                             preferred_element_type=F32, precision=lax.Precision.HIGHEST)
    s = jax.nn.sigmoid(logits)
    sel = s + b_ref[...]
    row = lax.broadcasted_iota(jnp.int32, (ne, tm), 0)
    gs = []
    for g in range(N_EXPERT_GROUPS):
        blk = sel[g * per_group:(g + 1) * per_group]
        r = lax.broadcasted_iota(jnp.int32, blk.shape, 0)
        m1 = jnp.max(blk, axis=0, keepdims=True)
        i1 = jnp.min(jnp.where(blk == m1, r, per_group), axis=0, keepdims=True)
        m2 = jnp.max(jnp.where(r == i1, NEG_INF, blk), axis=0, keepdims=True)
        gs.append(m1 + m2)
    grp = jnp.concatenate(gs, axis=0)
    grow = lax.broadcasted_iota(jnp.int32, grp.shape, 0)
    gsel = jnp.zeros(grp.shape, F32)
    for _ in range(TOPK_GROUPS):
        m = jnp.max(grp, axis=0, keepdims=True)
        gi = jnp.min(jnp.where(grp == m, grow, N_EXPERT_GROUPS), axis=0, keepdims=True)
        hit = grow == gi
        gsel = jnp.where(hit, 1.0, gsel)
        grp = jnp.where(hit, NEG_INF, grp)
    masked = jnp.concatenate(
        [jnp.where(gsel[g:g + 1] > 0.0, sel[g * per_group:(g + 1) * per_group], NEG_INF)
         for g in range(N_EXPERT_GROUPS)], axis=0)
    eis, ws = [], []
    picked = jnp.zeros((ne, tm), F32)
    for _ in range(TOP_K):
        m = jnp.max(masked, axis=0, keepdims=True)
        ei = jnp.min(jnp.where(masked == m, row, ne), axis=0, keepdims=True)
        hit = row == ei
        ws.append(jnp.sum(jnp.where(hit, s, 0.0), axis=0, keepdims=True))
        eis.append(ei)
        picked = jnp.where(hit, 1.0, picked)
        masked = jnp.where(hit, NEG_INF, masked)
    w = jnp.concatenate(ws, axis=0)
    eidx_ref[...] = jnp.concatenate(eis, axis=0)
    wsel_ref[...] = w / jnp.sum(w, axis=0, keepdims=True) * ROUTED_SCALE
    tot = jnp.dot(picked.astype(BF16), jnp.ones((tm, LANE), BF16), preferred_element_type=F32)

    @pl.when(i == 0)
    def _():
        cnt_ref[...] = jnp.zeros_like(cnt_ref)

    cnt_ref[...] += tot


def _pos_kernel(eidx_ref, base_ref, pos_ref, carry_ref):
    i = pl.program_id(0)
    tm = eidx_ref.shape[1]
    ne = base_ref.shape[0]

    @pl.when(i == 0)
    def _():
        carry_ref[...] = jnp.zeros_like(carry_ref)

    eidx = eidx_ref[...]
    row = lax.broadcasted_iota(jnp.int32, (ne, tm), 0)
    picked = jnp.zeros((ne, tm), F32)
    for k in range(TOP_K):
        picked = jnp.where(row == eidx[k:k + 1], 1.0, picked)
    pb = picked.astype(BF16)
    before = jnp.where(lax.broadcasted_iota(jnp.int32, (tm, tm), 0) < lax.broadcasted_iota(jnp.int32, (tm, tm), 1),
                       1.0, 0.0).astype(BF16)
    rank = jnp.dot(pb, before, preferred_element_type=F32)
    tot = jnp.dot(pb, jnp.ones((tm, LANE), BF16), preferred_element_type=F32)
    dest = rank + (base_ref[...] + carry_ref[:, 0:1])
    pos = [jnp.sum(jnp.where(row == eidx[k:k + 1], dest, 0.0), axis=0, keepdims=True) for k in range(TOP_K)]
    pos_ref[...] = jnp.concatenate(pos, axis=0).astype(jnp.int32)
    carry_ref[...] += tot


def _pack_bf16_pairs(x):
    half = x.shape[1] // 2
    bits = lambda a: pltpu.bitcast(a.astype(BF16).astype(F32), jnp.int32)
    return (bits(x[:, half:]) & -65536) | lax.shift_right_logical(bits(x[:, :half]), 16)


def _unpack_bf16_pairs(w):
    return pltpu.bitcast(lax.shift_left(w, 16), F32), pltpu.bitcast(w & -65536, F32)


def _dispatch_kernel(zstart_ref, zlen_ref, nused_ref, pos_ref, h_ref, xs_ref, packed, zeros, sem, zsem, *,
                     n_blocks):
    tm = h_ref.shape[0]
    bm = zeros.shape[0]
    packed[...] = _pack_bf16_pairs(h_ref[...])

    @pl.when(pl.program_id(0) == 0)
    def _():
        zeros[...] = jnp.zeros_like(zeros)

        def zero_copy(start, size):
            return pltpu.make_async_copy(zeros.at[pl.ds(0, size)], xs_ref.at[pl.ds(start, size)], zsem)

        def pieces(e, act):
            start = zstart_ref[e]
            rem = zlen_ref[e]
            ragged = rem & (SUBLANE - 1)
            for q in range(SUBLANE - 1):
                @pl.when(q < ragged)
                def _(q=q):
                    act(zero_copy(start + q, 1))

            start = pl.multiple_of(start + ragged, SUBLANE)
            size = bm // 2
            while size >= SUBLANE:
                @pl.when((rem & size) != 0)
                def _(start=start, size=size):
                    act(zero_copy(start, size))

                start = pl.multiple_of(start + (rem & size), SUBLANE)
                size //= 2

        def loop(act):
            def per_expert(e, carry):
                pieces(e, act)
                return carry

            def per_block(b, carry):
                act(zero_copy(pl.multiple_of(b * bm, bm), bm))
                return carry

            lax.fori_loop(0, zstart_ref.shape[0], per_expert, 0)
            lax.fori_loop(nused_ref[0], n_blocks, per_block, 0)

        loop(lambda cp: cp.start())
        loop(lambda cp: cp.wait())

    def row_copy(n, k):
        return pltpu.make_async_copy(packed.at[pl.ds(n, 1)], xs_ref.at[pl.ds(pos_ref[k, n], 1)], sem)

    def issue(n, carry):
        for k in range(TOP_K):
            row_copy(n, k).start()
        return carry

    def drain(n, carry):
        for k in range(TOP_K):
            row_copy(n, k).wait()
        return carry

    lax.fori_loop(0, tm, issue, 0)
    lax.fori_loop(0, tm, drain, 0)


def _moe_ffn_kernel(blk_e_ref, run_ref, next_e_ref, nused_ref, x_ref, wgu_hbm, wdn_hbm, o_ref,
                    wgu_f, wdn_f, wgu_s, wdn_s, sems, *, layer):
    i = pl.program_id(0)
    e = blk_e_ref[i]
    run = run_ref[i]
    first_of_run = (i == 0) | (run != run_ref[jnp.maximum(i - 1, 0)])

    def fetch(expert, slot):
        return (pltpu.make_async_copy(wgu_hbm.at[layer, expert], wgu_f.at[slot], sems.at[0, slot]),
                pltpu.make_async_copy(wdn_hbm.at[layer, expert], wdn_f.at[slot], sems.at[1, slot]))

    @pl.when(i < nused_ref[0])
    def _():
        @pl.when(first_of_run)
        def _():
            slot = run & 1

            @pl.when(i == 0)
            def _():
                for cp in fetch(e, 0):
                    cp.start()

            for cp in fetch(e, slot):
                cp.wait()
            wgu_s[...] = wgu_f[slot].astype(BF16)
            wdn_s[...] = wdn_f[slot].astype(BF16)

            @pl.when(next_e_ref[i] >= 0)
            def _():
                for cp in fetch(next_e_ref[i], 1 - slot):
                    cp.start()

        x_lo, x_hi = _unpack_bf16_pairs(x_ref[...])
        half = x_lo.shape[1]
        au = (jnp.dot(x_lo.astype(BF16), wgu_s[:half, :], preferred_element_type=F32)
              + jnp.dot(x_hi.astype(BF16), wgu_s[half:, :], preferred_element_type=F32))
        a = au[:, :EXPERT_FF]
        u = au[:, EXPERT_FF:]
        h = (a * jax.nn.sigmoid(a)) * u
        o_ref[...] = _pack_bf16_pairs(jnp.dot(h.astype(BF16), wdn_s[...], preferred_element_type=F32))

    @pl.when(i >= nused_ref[0])
    def _():
        o_ref[...] = jnp.zeros_like(o_ref)


def _combine_kernel(pos_ref, w_ref, t_ref, x_ref, sgu_ref, sdn_ref, gain_ref, g2_ref, y_hbm, o_ref, buf, sem):
    tm = o_ref.shape[0]

    def row_copy(n, k):
        return pltpu.make_async_copy(y_hbm.at[pl.ds(pos_ref[k, n], 1)], buf.at[k, pl.ds(n, 1)], sem)

    def issue(n, carry):
        for k in range(TOP_K):
            row_copy(n, k).start()
        return carry

    def drain(n, carry):
        for k in range(TOP_K):
            row_copy(n, k).wait()
        return carry

    lax.fori_loop(0, tm, issue, 0)
    ff = sdn_ref.shape[0]
    au = jnp.dot(t_ref[...].astype(BF16), sgu_ref[...], preferred_element_type=F32)
    a, u = au[:, :ff], au[:, ff:]
    f = jnp.dot(((a * jax.nn.sigmoid(a)) * u).astype(BF16), sdn_ref[...], preferred_element_type=F32)
    lax.fori_loop(0, tm, drain, 0)
    r_lo, r_hi = None, None
    for k in range(TOP_K):
        y_lo, y_hi = _unpack_bf16_pairs(buf[k])
        wk = w_ref[:, k:k + 1]
        r_lo = y_lo * wk if r_lo is None else r_lo + y_lo * wk
        r_hi = y_hi * wk if r_hi is None else r_hi + y_hi * wk
    f = f + jnp.concatenate([r_lo, r_hi], axis=1)
    o_ref[...] = x_ref[...] + g2_ref[...] * (f * lax.rsqrt(jnp.mean(f * f, axis=-1, keepdims=True) + EPS)
                                             * gain_ref[...])


def _moe(t, x_res, router_w, router_b, w_gu, w_down, sh_gu, sh_down, layer, gain, g2_rows, rows_per_gate):
    n, d = t.shape
    ne = router_w.shape[1]
    ff2 = w_gu.shape[-1]
    params = pltpu.CompilerParams(dimension_semantics=("arbitrary",), vmem_limit_bytes=VMEM_LIMIT)
    tm = ROUTER_TILE
    eidx, wsel, cnt = pl.pallas_call(
        _router_kernel,
        out_shape=[jax.ShapeDtypeStruct((TOP_K, n), jnp.int32), jax.ShapeDtypeStruct((TOP_K, n), F32),
                   jax.ShapeDtypeStruct((ne, LANE), F32)],
        grid=(n // tm,),
        in_specs=[pl.BlockSpec((tm, d), lambda i: (i, 0)), pl.BlockSpec((ne, d), lambda i: (0, 0)),
                  pl.BlockSpec((ne, 1), lambda i: (0, 0))],
        out_specs=[pl.BlockSpec((TOP_K, tm), lambda i: (0, i)), pl.BlockSpec((TOP_K, tm), lambda i: (0, i)),
                   pl.BlockSpec((ne, LANE), lambda i: (0, 0))],
        compiler_params=params, name="moe_router",
    )(t, router_w.T, router_b.reshape(ne, 1))
    bm = MOE_ROWS
    counts = cnt[:, 0].astype(jnp.int32)
    padded = (counts + bm - 1) // bm * bm
    pad_end = jnp.cumsum(padded)
    pad_start = pad_end - padded
    n_blocks = (n * TOP_K + ne * (bm - 1)) // bm + 1
    blk_first = jnp.arange(n_blocks, dtype=jnp.int32) * bm
    blk_e = jnp.minimum(jnp.sum((pad_end[None, :] <= blk_first[:, None]).astype(jnp.int32), axis=1), ne - 1)
    n_used = (pad_end[-1] // bm).astype(jnp.int32).reshape(1)
    pos = pl.pallas_call(
        _pos_kernel,
        out_shape=jax.ShapeDtypeStruct((TOP_K, n), jnp.int32),
        grid=(n // tm,),
        in_specs=[pl.BlockSpec((TOP_K, tm), lambda i: (0, i)), pl.BlockSpec((ne, 1), lambda i: (0, 0))],
        out_specs=pl.BlockSpec((TOP_K, tm), lambda i: (0, i)),
        scratch_shapes=[pltpu.VMEM((ne, LANE), F32)],
        compiler_params=params, name="moe_positions",
    )(eidx, pad_start.astype(F32).reshape(ne, 1))
    ts = SCATTER_TILE
    p = n_blocks * bm
    pos_spec = pl.BlockSpec((TOP_K, ts), lambda i: (0, i), memory_space=pltpu.SMEM)
    dp = d // 2
    xs = pl.pallas_call(
        functools.partial(_dispatch_kernel, n_blocks=n_blocks),
        out_shape=jax.ShapeDtypeStruct((p, dp), jnp.int32),
        grid_spec=pltpu.PrefetchScalarGridSpec(
            num_scalar_prefetch=3,
            grid=(n // ts,),
            in_specs=[pl.BlockSpec((TOP_K, ts), lambda i, *_: (0, i), memory_space=pltpu.SMEM),
                      pl.BlockSpec((ts, d), lambda i, *_: (i, 0))],
            out_specs=pl.BlockSpec(memory_space=pl.ANY),
            scratch_shapes=[pltpu.VMEM((ts, dp), jnp.int32), pltpu.VMEM((bm, dp), jnp.int32),
                            pltpu.SemaphoreType.DMA, pltpu.SemaphoreType.DMA],
        ),
        compiler_params=params, name="moe_dispatch",
    )(pad_start + counts, padded - counts, n_used, pos, t)

    is_start = jnp.concatenate([jnp.ones((1,), bool), blk_e[1:] != blk_e[:-1]])
    run_id = jnp.cumsum(is_start.astype(jnp.int32)) - 1
    ids = jnp.arange(ne, dtype=jnp.int32)
    later_busy = jnp.where((counts[None, :] > 0) & (ids[None, :] > ids[:, None]), ids[None, :], ne)
    next_busy = jnp.min(later_busy, axis=1)
    next_e = jnp.sum(jnp.where(blk_e[:, None] == ids[None, :], next_busy[None, :], 0), axis=1)
    next_e = jnp.where(next_e >= ne, -1, next_e).astype(jnp.int32)

    def x_map(i, blk_e, run, nxt, nused):
        return (jnp.minimum(i, nused[0] - 1), 0)

    y_p = pl.pallas_call(
        functools.partial(_moe_ffn_kernel, layer=layer),
        out_shape=jax.ShapeDtypeStruct((p, dp), jnp.int32),
        grid_spec=pltpu.PrefetchScalarGridSpec(
            num_scalar_prefetch=4,
            grid=(n_blocks,),
            in_specs=[pl.BlockSpec((bm, dp), x_map),
                      pl.BlockSpec(memory_space=pl.ANY), pl.BlockSpec(memory_space=pl.ANY)],
            out_specs=pl.BlockSpec((bm, dp), lambda i, *_: (i, 0)),
            scratch_shapes=[pltpu.VMEM((2, d, ff2), F32), pltpu.VMEM((2, ff2 // 2, d), F32),
                            pltpu.VMEM((d, ff2), BF16), pltpu.VMEM((ff2 // 2, d), BF16),
                            pltpu.SemaphoreType.DMA((2, 2))],
        ),
        compiler_params=params, name="moe_expert_ffn",
    )(blk_e, run_id, next_e, n_used, xs, w_gu, w_down)
    rows = pl.BlockSpec((ts, d), lambda i: (i, 0))
    whole = lambda a: pl.BlockSpec(a.shape, lambda i: (0,) * a.ndim)
    last_gate = g2_rows.shape[0] - 1
    gate_spec = pl.BlockSpec((None, 1, d), lambda i: (jnp.minimum(i // (rows_per_gate // ts), last_gate), 0, 0))
    consts = [sh_gu.astype(BF16), sh_down.astype(BF16), gain.reshape(1, d)]
    return pl.pallas_call(
        _combine_kernel,
        out_shape=jax.ShapeDtypeStruct((n, d), F32),
        grid=(n // ts,),
        in_specs=([pos_spec, pl.BlockSpec((ts, TOP_K), lambda i: (i, 0)), rows, rows]
                  + [whole(a) for a in consts] + [gate_spec, pl.BlockSpec(memory_space=pl.ANY)]),
        out_specs=rows,
        scratch_shapes=[pltpu.VMEM((TOP_K, ts, dp), jnp.int32), pltpu.SemaphoreType.DMA],
        compiler_params=params, name="moe_combine",
    )(pos, wsel.T, t, x_res, *consts, g2_rows, y_p)


def kernel(x, c, ctx, c_ctx, ada_w, ada_b, norm_g, w_in, w_out, hy_conv_w, hy_conv_b, hy_ffn_w1, hy_ffn_b1, hy_ffn_w2, hy_ffn_b2, hy_ffn_w3, hy_freq, hy_bias, hy_norm, hg_lb_logits, hg_norm, gla_a_up, gla_a_b, gla_norm, ml_conv_w, ml_conv_b, ml_gate_b, ml_norm, router_w, router_b, exp_w_gu, exp_w_down, sh_w_gu, sh_w_down):
    bsz, seq, d = x.shape
    n_ctx = ctx.shape[1]
    depth = ada_w.shape[0]
    rows = seq // GRID_W
    x = x + _pos_embed_2d(rows, d)[None]
    xc = ctx
    lb_cum = jnp.cumsum(jax.nn.softmax(hg_lb_logits, axis=0), axis=0)
    lower_bounds = lb_cum - lb_cum[0:1]
    for l in range(depth):
        with_ctx = l < depth - 1
        mod = (jax.nn.silu(c) @ ada_w[l] + ada_b[l])[:, None, :]
        mod_c = jax.nn.silu(c_ctx) @ ada_w[l] + ada_b[l]
        sh1, sc1, g1, sh2, sc2, g2 = jnp.split(mod, 6, axis=-1)
        csh1, csc1, cg1, csh2, csc2, cg2 = jnp.split(mod_c, 6, axis=-1)
        w_in_l = _arrange_w_in(w_in[l])
        ctx_rows = lambda a: jnp.broadcast_to(a.reshape(1, 1, d), (bsz, 1, d))
        pl3 = _in_proj(x, norm_g[l, 0], sc1, sh1, w_in_l)
        pc3 = _in_proj(xc, norm_g[l, 0], ctx_rows(csc1), ctx_rows(csh1), w_in_l)
        hy_args = (hy_conv_w[l], hy_conv_b[l], hy_ffn_w1[l], hy_ffn_b1[l], hy_ffn_w2[l], hy_ffn_b2[l],
                   hy_ffn_w3[l], hy_freq[l], hy_bias[l], hy_norm[l])
        y_hy = _hyena_long(pl3, *hy_args)
        oc_hg, o_hg = _hgrn2(pc3, pl3, lower_bounds[l])
        oc_gla, o_gla = _gla(pc3, pl3, gla_a_up[l], gla_a_b[l])
        hc_ml, h_ml = _mlstm(pc3, pl3, ml_conv_w[l], ml_conv_b[l], ml_gate_b[l])
        head_gains = jnp.stack([hg_norm[l], gla_norm[l], ml_norm[l]])
        out_args = (w_out[l], head_gains, norm_g[l, 1], norm_g[l, 2])
        x, h = _mixer_out(y_hy, o_hg, o_gla, h_ml, pl3, x, *out_args, jnp.concatenate([g1, sc2, sh2], axis=1))
        moe_args = (router_w[l], router_b[l], exp_w_gu, exp_w_down, sh_w_gu[l], sh_w_down[l], l, norm_g[l, 3])
        if with_ctx:
            mod_ctx = jnp.broadcast_to(jnp.stack([cg1, csc2, csh2])[None], (bsz, 3, d))
            xc, hc = _mixer_out(_hyena(pc3, *hy_args), oc_hg, oc_gla, hc_ml, pc3, xc, *out_args, mod_ctx)
            tokens = lambda a, ac: jnp.concatenate([a.reshape(bsz * seq, d), ac.reshape(bsz * n_ctx, d)], axis=0)
            gates = jnp.concatenate([g2, cg2.reshape(1, 1, d)], axis=0)
            x_all = _moe(tokens(h, hc), tokens(x, xc), *moe_args, gates, seq)
            x = x_all[:bsz * seq].reshape(bsz, seq, d)
            xc = x_all[bsz * seq:].reshape(bsz, n_ctx, d)
        else:
            x = _moe(h.reshape(bsz * seq, d), x.reshape(bsz * seq, d), *moe_args, g2, seq).reshape(bsz, seq, d)
    return x
```

```python
import functools
import math

import jax
import jax.numpy as jnp
import numpy as np
from jax import lax
from jax.experimental import pallas as pl
from jax.experimental.pallas import tpu as pltpu

F32 = jnp.float32
BF16 = jnp.bfloat16

D_MODEL = 1024
GRID_W = 64
EPS = 1e-6
POS_BASE = 10000.0
GROUP_W = D_MODEL // 4
SHORT_CONV = 3
HY_W = GROUP_W
HY_ORDER = 2
HY_EMB = 33
HY_BANDS = (HY_EMB - 1) // 2
HY_FAST_DECAY = 0.3
HY_SLOW_DECAY = 1.5
HY_DECAY_TARGET = 1e-2
HG_H = 4
HG_W = GROUP_W
HG_DK = HG_W // HG_H
GLA_H = 4
GLA_KW = GROUP_W // 2
GLA_VW = GROUP_W
GLA_DK = GLA_KW // GLA_H
GLA_DV = GLA_VW // GLA_H
GLA_RANK = 16
GLA_NORMALIZER = 16.0
ML_H = 4
ML_W = GROUP_W
ML_DH = ML_W // ML_H
CHUNK_GATED = 16
CHUNK_ML = 64
N_EXPERTS = 256
TOP_K = 8
N_EXPERT_GROUPS = 8
TOPK_GROUPS = 4
EXPERT_FF = 256
ROUTED_SCALE = 2.5
IN_SPLITS = (HY_W, HY_W, HY_W,
             HG_W, HG_W, HG_W, HG_W, HG_W,
             GLA_KW, GLA_KW, GLA_VW, GLA_RANK, GLA_RANK, GLA_VW,
             ML_W, ML_W, ML_W, 4 * ML_H, ML_W)
P_ORDER = (0, 1, 2, 3, 4, 5, 6, 7, 8, 9, 10, 13, 14, 15, 16, 18, 11, 12, 17)
COL_HY = 0
COL_HG_Q, COL_HG_I, COL_HG_ZF, COL_HG_ZB, COL_HG_G = 3, 4, 5, 6, 7
COL_GLA_QK, COL_GLA_V, COL_GLA_R = 8, 9, 10
COL_ML_Q, COL_ML_K, COL_ML_V, COL_ML_O = 11, 12, 13, 14
N_WIDE = 15

LANE = 128
SUBLANE = 8
ROW_TILE = 512
MOE_ROWS = 256
ROUTER_TILE = 256
SCATTER_TILE = 256
GLR_TILE = 128
VMEM_LIMIT = 56 * 1024 * 1024
NEG_INF = float("-inf")
COL_NARROW = N_WIDE * GROUP_W // LANE
NARROW_ML_GATES = N_WIDE * GROUP_W + 2 * GLA_RANK
P_WIDTH = N_WIDE * GROUP_W + LANE


def _arrange_w_in(w):
    offs = np.concatenate([[0], np.cumsum(IN_SPLITS)])
    cols = [w[:, offs[i]:offs[i + 1]] for i in P_ORDER]
    used = sum(IN_SPLITS)
    return jnp.concatenate(cols + [jnp.zeros((w.shape[0], P_WIDTH - used), w.dtype)], axis=1).astype(BF16)


def _in_proj_kernel(x_ref, g_ref, sc_ref, sh_ref, w_ref, o_ref):
    x = x_ref[...]
    y = x * lax.rsqrt(jnp.mean(x * x, axis=-1, keepdims=True) + EPS) * g_ref[...]
    h = y * (1.0 + sc_ref[...]) + sh_ref[...]
    o_ref[...] = jnp.dot(h.astype(BF16), w_ref[...], preferred_element_type=F32).astype(o_ref.dtype)


def _in_proj(x, gain, scale, shift, w):
    g, r, d = x.shape
    n = w.shape[1]
    tm = min(ROW_TILE, r)
    assert r % tm == 0
    return pl.pallas_call(
        _in_proj_kernel,
        out_shape=jax.ShapeDtypeStruct((g, r, n), BF16),
        grid=(g, r // tm),
        in_specs=[pl.BlockSpec((None, tm, d), lambda b, i: (b, i, 0)),
                  pl.BlockSpec((1, d), lambda b, i: (0, 0)),
                  pl.BlockSpec((None, 1, d), lambda b, i: (b, 0, 0)),
                  pl.BlockSpec((None, 1, d), lambda b, i: (b, 0, 0)),
                  pl.BlockSpec((d, n), lambda b, i: (0, 0))],
        out_specs=pl.BlockSpec((None, tm, n), lambda b, i: (b, i, 0)),
        compiler_params=pltpu.CompilerParams(dimension_semantics=("arbitrary", "arbitrary"),
                                             vmem_limit_bytes=VMEM_LIMIT),
        name="input_projection",
    )(x, gain.reshape(1, d), scale, shift, w)


def _mixer_out_kernel(hy_ref, hgf_ref, hgb_ref, glf_ref, glb_ref, mlf_ref, mlb_ref, ghg_ref, ggl_ref, gml_ref,
                      x_ref, w_ref, hn_ref, n1_ref, n2_ref, mod_ref, xo_ref, h_ref):
    gw = hy_ref.shape[-1]
    seg = jnp.where(lax.broadcasted_iota(jnp.int32, (gw, gw), 0) // HG_DK
                    == lax.broadcasted_iota(jnp.int32, (gw, gw), 1) // HG_DK, 1.0 / HG_DK, 0.0)

    def head_norm(o, gain):
        ms = jnp.dot(o * o, seg, precision=lax.Precision.HIGHEST, preferred_element_type=F32)
        return o * lax.rsqrt(ms + EPS) * gain

    silu = lambda a: a * jax.nn.sigmoid(a)
    groups = (hy_ref[...],
              head_norm(hgf_ref[...] + hgb_ref[...], hn_ref[0:1]) * silu(ghg_ref[...].astype(F32)),
              head_norm(glf_ref[...] + glb_ref[...], hn_ref[1:2]) * silu(ggl_ref[...].astype(F32)),
              jax.nn.sigmoid(gml_ref[...].astype(F32)) * head_norm(mlf_ref[...] + mlb_ref[...], hn_ref[2:3]))
    y = None
    for i, part in enumerate(groups):
        term = jnp.dot(part.astype(BF16), w_ref[i * gw:(i + 1) * gw, :], preferred_element_type=F32)
        y = term if y is None else y + term
    rms = lambda a, g: a * lax.rsqrt(jnp.mean(a * a, axis=-1, keepdims=True) + EPS) * g
    x = x_ref[...] + mod_ref[0:1] * rms(y, n1_ref[...])
    xo_ref[...] = x
    h_ref[...] = rms(x, n2_ref[...]) * (1.0 + mod_ref[1:2]) + mod_ref[2:3]


def _mixer_out(y_hy, o_hg, o_gla, h_ml, p3, x, w_out, head_gains, gain1, gain2, mod):
    assert HG_DK == GLA_DV == ML_DH and HG_H == GLA_H == ML_H
    bsz, t, d = x.shape
    gw = GROUP_W
    tm = min(ROW_TILE, t)
    part = pl.BlockSpec((None, tm, gw), lambda b, i: (b, i, 0))
    gate = lambda c: pl.BlockSpec((None, tm, gw), lambda b, i: (b, i, c))
    full = pl.BlockSpec((None, tm, d), lambda b, i: (b, i, 0))
    whole = lambda a: pl.BlockSpec(a.shape, lambda b, i: (0,) * a.ndim)
    consts = [w_out.astype(BF16), head_gains, gain1.reshape(1, d), gain2.reshape(1, d)]
    return pl.pallas_call(
        _mixer_out_kernel,
        out_shape=[jax.ShapeDtypeStruct((bsz, t, d), F32), jax.ShapeDtypeStruct((bsz, t, d), F32)],
        grid=(bsz, t // tm),
        in_specs=([part] * 7 + [gate(COL_HG_G), gate(COL_GLA_R), gate(COL_ML_O), full]
                  + [whole(a) for a in consts] + [pl.BlockSpec((None, 3, d), lambda b, i: (b, 0, 0))]),
        out_specs=[full, full],
        compiler_params=pltpu.CompilerParams(dimension_semantics=("arbitrary", "arbitrary"),
                                             vmem_limit_bytes=VMEM_LIMIT),
        name="mixer_output",
    )(y_hy, *o_hg, *o_gla, *h_ml, p3, p3, p3, x, *consts, mod)


def _rms(x, g):
    return x * lax.rsqrt(jnp.mean(x * x, axis=-1, keepdims=True) + EPS) * g


def _short_conv_kernel(prev_ref, cur_ref, next_ref, w_ref, b_ref, o_ref, *, act):
    j = pl.program_id(1)
    u = cur_ref[...].astype(F32)
    tt = u.shape[0]
    row = lax.broadcasted_iota(jnp.int32, u.shape, 0)
    halo = prev_ref.shape[0]
    before = jnp.where(j > 0, prev_ref[...].astype(F32)[halo - 1:halo, :], 0.0)
    after = jnp.where(j < pl.num_programs(1) - 1, next_ref[...].astype(F32)[0:1, :], 0.0)
    up = jnp.where(row == 0, before, pltpu.roll(u, 1, 0))
    dn = jnp.where(row == tt - 1, after, pltpu.roll(u, tt - 1, 0))
    y = w_ref[0:1] * up + w_ref[1:2] * u + w_ref[2:3] * dn + b_ref[...]
    if act:
        y = y * jax.nn.sigmoid(y)
    o_ref[...] = y


def _short_conv(p3, col0, ncols, w, b, act):
    assert SHORT_CONV == 3
    bsz, t, _ = p3.shape
    tt = min(ROW_TILE, t)
    hrows = SUBLANE * (4 // p3.dtype.itemsize)
    halo = tt // hrows
    last = t // hrows - 1
    gw = GROUP_W
    cur = pl.BlockSpec((None, tt, gw), lambda bi, j, c: (bi, j, col0 + c))
    prev = pl.BlockSpec((None, hrows, gw), lambda bi, j, c: (bi, jnp.maximum(j * halo - 1, 0), col0 + c))
    nxt = pl.BlockSpec((None, hrows, gw), lambda bi, j, c: (bi, jnp.minimum((j + 1) * halo, last), col0 + c))
    return pl.pallas_call(
        functools.partial(_short_conv_kernel, act=act),
        out_shape=jax.ShapeDtypeStruct((bsz, t, ncols * gw), F32),
        grid=(bsz, t // tt, ncols),
        in_specs=[prev, cur, nxt, pl.BlockSpec((SHORT_CONV, gw), lambda bi, j, c: (0, c)),
                  pl.BlockSpec((1, gw), lambda bi, j, c: (0, c))],
        out_specs=pl.BlockSpec((None, tt, gw), lambda bi, j, c: (bi, j, c)),
        compiler_params=pltpu.CompilerParams(dimension_semantics=("arbitrary",) * 3, vmem_limit_bytes=VMEM_LIMIT),
        name="short_conv",
    )(p3, p3, p3, w, b.reshape(1, ncols * gw))


def _pos_embed_2d(rows, d):
    r = jnp.repeat(jnp.arange(rows, dtype=F32), GRID_W)
    col = (jnp.arange(rows * GRID_W) % GRID_W).astype(F32)
    quarter = d // 4
    omega = 1.0 / (POS_BASE ** (jnp.arange(quarter, dtype=F32) / quarter))

    def axis_emb(p):
        ang = p[:, None] * omega[None, :]
        return jnp.concatenate([jnp.sin(ang), jnp.cos(ang)], axis=-1)

    return jnp.concatenate([axis_emb(r), axis_emb(col)], axis=-1)


def _hyena_spectra(L, w1, b1, w2, b2, w3, freq):
    t = jnp.linspace(0.0, 1.0, L, dtype=F32)[:, None]
    w = 2.0 * math.pi * jnp.arange(L, dtype=F32)[:, None] / L
    bands = jnp.linspace(1e-4, HY_BANDS - 1, HY_BANDS, dtype=F32)[None, :]
    feats = jnp.concatenate([t, jnp.cos(bands * w), -jnp.sin(bands * w)], axis=-1)
    z = jnp.sin(freq[0] * (feats @ w1 + b1))
    z = jnp.sin(freq[1] * (z @ w2 + b2))
    h = (z @ w3).reshape(L, HY_ORDER, 2, HY_W)
    max_decay = math.log(HY_DECAY_TARGET) / HY_FAST_DECAY
    min_decay = math.log(HY_DECAY_TARGET) / HY_SLOW_DECAY
    deltas = jnp.abs(jnp.linspace(min_decay, max_decay, HY_W, dtype=F32))
    h = h * jnp.exp(-t[:, :, None, None] * deltas)
    fwd = h[:, :, 0]
    bwd = h[1:, :, 1][::-1]
    l1 = jnp.sum(jnp.abs(fwd), axis=0) + jnp.sum(jnp.abs(bwd), axis=0)
    filt = jnp.concatenate([fwd, jnp.zeros((1, HY_ORDER, HY_W), F32), bwd], axis=0) / l1
    return jnp.fft.rfft(filt, axis=0)


def _fft_conv(u, spec, bias):
    L = u.shape[1]
    y = jnp.fft.irfft(jnp.fft.rfft(u, n=2 * L, axis=1) * spec, n=2 * L, axis=1)[:, :L]
    return y + u * bias


def _hyena(p3, conv_w, conv_b, w1, b1, w2, b2, w3, freq, bias, norm_g):
    u = _short_conv(p3, COL_HY, 3, conv_w, conv_b, False)
    v, x1, x2 = u[..., :HY_W], u[..., HY_W:2 * HY_W], u[..., 2 * HY_W:]
    spec = _hyena_spectra(u.shape[1], w1, b1, w2, b2, w3, freq)
    z = x1 * _fft_conv(v, spec[:, 0], bias[0])
    y = x2 * _fft_conv(z, spec[:, 1], bias[1])
    return _rms(y, norm_g)


FFT_N1 = 128
FFT_N2 = 128
FFT_KTILE = 8
FFT_NTILE = 4096


def _dft_tables(n1, n2):
    n = n1 * n2
    k = np.arange(n1)
    f1 = np.exp(-2j * np.pi * np.outer(k, k) / n1)
    f2 = np.exp(-2j * np.pi * np.outer(np.arange(n2), np.arange(n2)) / n2)
    tw = np.exp(-2j * np.pi * np.outer(np.arange(n1), np.arange(n2)) / n)
    as32 = lambda a: jnp.asarray(np.ascontiguousarray(a), F32)
    f1_fwd = as32(np.concatenate([f1.real, f1.imag], axis=0))
    f1_inv = as32(np.concatenate([f1.real, f1.imag], axis=1) / n)
    f2_inv = as32(np.block([[f2.real, f2.imag], [-f2.imag, f2.real]]))
    return f1_fwd, f1_inv, as32(f2.real), as32(f2.imag), f2_inv, as32(tw.real), as32(tw.imag)


def _stage_kernel(w_ref, x_ref, o_ref):
    o_ref[...] = jnp.dot(w_ref[...].astype(BF16), x_ref[...].astype(BF16),
                         preferred_element_type=F32).astype(o_ref.dtype)


def _stage_matmul(w, x, out_dtype):
    g, k, n = x.shape
    m = w.shape[0]
    tn = FFT_NTILE
    return pl.pallas_call(
        _stage_kernel,
        out_shape=jax.ShapeDtypeStruct((g, m, n), out_dtype),
        grid=(g, n // tn),
        in_specs=[pl.BlockSpec((m, k), lambda b, j: (0, 0)), pl.BlockSpec((None, k, tn), lambda b, j: (b, 0, j))],
        out_specs=pl.BlockSpec((None, m, tn), lambda b, j: (b, 0, j)),
        compiler_params=pltpu.CompilerParams(dimension_semantics=("arbitrary", "arbitrary"),
                                             vmem_limit_bytes=VMEM_LIMIT),
        name="dft_stage",
    )(w, x)


def _twiddled_f2(f2r, f2i, tr, ti):
    gr = f2r * tr - f2i * ti
    gi = f2r * ti + f2i * tr
    return jnp.concatenate([jnp.concatenate([gr, -gi], axis=1), jnp.concatenate([gi, gr], axis=1)], axis=0)


def _spectrum_kernel(a_ref, f2r_ref, f2i_ref, tr_ref, ti_ref, x_ref):
    n2 = f2r_ref.shape[0]
    tr = tr_ref[...]
    ti = ti_ref[...]
    for i in range(a_ref.shape[0]):
        g = _twiddled_f2(f2r_ref[...], f2i_ref[...], tr[i:i + 1], ti[i:i + 1])
        x_ref[i] = jnp.dot(g.astype(BF16), a_ref[i], preferred_element_type=F32)


def _conv_mid_kernel(a_ref, h_ref, f2r_ref, f2i_ref, f2inv_ref, tr_ref, ti_ref, z_ref):
    n2 = f2r_ref.shape[0]
    tr = tr_ref[...]
    ti = ti_ref[...]
    tr_col = tr.T
    ti_col = ti.T
    f2inv = f2inv_ref[...].astype(BF16)
    for i in range(a_ref.shape[0]):
        g = _twiddled_f2(f2r_ref[...], f2i_ref[...], tr[i:i + 1], ti[i:i + 1])
        x = jnp.dot(g.astype(BF16), a_ref[i], preferred_element_type=F32)
        xr, xi = x[:n2], x[n2:]
        hr, hi = h_ref[i, :n2], h_ref[i, n2:]
        y = jnp.concatenate([hr * xr - hi * xi, hr * xi + hi * xr], axis=0)
        w = jnp.dot(f2inv, y.astype(BF16), preferred_element_type=F32)
        wr, wi = w[:n2], w[n2:]
        cr, ci = tr_col[:, i:i + 1], ti_col[:, i:i + 1]
        z_ref[i] = jnp.concatenate([cr * wr + ci * wi, cr * wi - ci * wr], axis=0).astype(z_ref.dtype)


def _dft_mid_specs(g, c):
    n1, n2, kt = FFT_N1, FFT_N2, FFT_KTILE
    blk = pl.BlockSpec((None, kt, 2 * n2, c), lambda b, j: (b, j, 0, 0))
    const = lambda r, cc: pl.BlockSpec((r, cc), lambda b, j: (0, 0))
    twid = pl.BlockSpec((kt, n2), lambda b, j: (j, 0))
    params = pltpu.CompilerParams(dimension_semantics=("arbitrary", "arbitrary"), vmem_limit_bytes=VMEM_LIMIT)
    return blk, const, twid, params, (g, n1 // kt)


def _to_k1_major(a2d, c):
    g = a2d.shape[0]
    return a2d.reshape(g, 2, FFT_N1, FFT_N2, c).transpose(0, 2, 1, 3, 4).reshape(g, FFT_N1, 2 * FFT_N2, c)


def _filter_spectrum(filt, tables):
    g, n, c = filt.shape
    f1_fwd, _, f2r, f2i, _, twr, twi = tables
    a = _stage_matmul(f1_fwd, filt.reshape(g, FFT_N1, FFT_N2 * c), BF16)
    blk, const, twid, params, grid = _dft_mid_specs(g, c)
    return pl.pallas_call(
        _spectrum_kernel,
        out_shape=jax.ShapeDtypeStruct((g, FFT_N1, 2 * FFT_N2, c), F32),
        grid=grid,
        in_specs=[blk, const(FFT_N2, FFT_N2), const(FFT_N2, FFT_N2), twid, twid],
        out_specs=blk, compiler_params=params, name="dft_spectrum",
    )(_to_k1_major(a, c), f2r, f2i, twr, twi)


def _long_conv(u, spec, tables):
    g, l, c = u.shape
    f1_fwd, f1_inv, f2r, f2i, f2inv, twr, twi = tables
    half = l // FFT_N2
    a = _stage_matmul(f1_fwd[:, :half], u.reshape(g, half, FFT_N2 * c), BF16)
    blk, const, twid, params, grid = _dft_mid_specs(g, c)
    hspec = pl.BlockSpec((FFT_KTILE, 2 * FFT_N2, c), lambda b, j: (j, 0, 0))
    z = pl.pallas_call(
        _conv_mid_kernel,
        out_shape=jax.ShapeDtypeStruct((g, FFT_N1, 2 * FFT_N2, c), BF16),
        grid=grid,
        in_specs=[blk, hspec, const(FFT_N2, FFT_N2), const(FFT_N2, FFT_N2), const(2 * FFT_N2, 2 * FFT_N2),
                  twid, twid],
        out_specs=blk, compiler_params=params, name="dft_conv_mid",
    )(_to_k1_major(a, c), spec, f2r, f2i, f2inv, twr, twi)
    z2d = z.reshape(g, FFT_N1, 2, FFT_N2, c).transpose(0, 2, 1, 3, 4).reshape(g, 2 * FFT_N1, FFT_N2 * c)
    y = _stage_matmul(f1_inv[:half], z2d, F32)
    return y.reshape(g, l, c)


def _hyena_filters(L, w1, b1, w2, b2, w3, freq):
    t = jnp.linspace(0.0, 1.0, L, dtype=F32)[:, None]
    w = 2.0 * math.pi * jnp.arange(L, dtype=F32)[:, None] / L
    bands = jnp.linspace(1e-4, HY_BANDS - 1, HY_BANDS, dtype=F32)[None, :]
    feats = jnp.concatenate([t, jnp.cos(bands * w), -jnp.sin(bands * w)], axis=-1)
    max_decay = math.log(HY_DECAY_TARGET) / HY_FAST_DECAY
    min_decay = math.log(HY_DECAY_TARGET) / HY_SLOW_DECAY
    deltas = jnp.abs(jnp.linspace(min_decay, max_decay, HY_W, dtype=F32))
    w3d = w3.reshape(w3.shape[0], HY_ORDER, 2, HY_W)

    def side(f, tt, direction):
        z = jnp.sin(freq[0] * (f @ w1 + b1))
        z = jnp.sin(freq[1] * (z @ w2 + b2))
        h = (z @ w3d[:, :, direction].reshape(w3.shape[0], HY_ORDER * HY_W)).reshape(-1, HY_ORDER, HY_W)
        return h * jnp.exp(-tt[:, :, None] * deltas)

    fwd = side(feats, t, 0)
    bwd = side(feats[::-1], t[::-1], 1)[:L - 1]
    l1 = jnp.sum(jnp.abs(fwd), axis=0) + jnp.sum(jnp.abs(bwd), axis=0)
    return jnp.concatenate([fwd, jnp.zeros((1, HY_ORDER, HY_W), F32), bwd], axis=0) / l1


def _hyena_long(p3, conv_w, conv_b, w1, b1, w2, b2, w3, freq, bias, norm_g):
    u = _short_conv(p3, COL_HY, 3, conv_w, conv_b, False)
    v, x1, x2 = u[..., :HY_W], u[..., HY_W:2 * HY_W], u[..., 2 * HY_W:]
    L = u.shape[1]
    assert 2 * L == FFT_N1 * FFT_N2
    tables = _dft_tables(FFT_N1, FFT_N2)
    filt = _hyena_filters(L, w1, b1, w2, b2, w3, freq)
    spec = _filter_spectrum(jnp.moveaxis(filt, 1, 0), tables)
    z = x1 * (_long_conv(v, spec[0], tables) + v * bias[0])
    y = x2 * (_long_conv(z, spec[1], tables) + z * bias[1])
    return _rms(y, norm_g)


def _glr_direction(q, k, v, g, st_ref, reverse, nh):
    tc, hk = k.shape
    hv = v.shape[1]
    c = CHUNK_GATED
    hi = lax.Precision.HIGHEST
    ti = lax.broadcasted_iota(jnp.int32, (tc, tc), 0)
    tj = lax.broadcasted_iota(jnp.int32, (tc, tc), 1)
    same = (ti // c) == (tj // c)
    seen = (tj >= ti) if reverse else (tj <= ti)
    bcum = jnp.dot(jnp.where(same, jnp.where(seen, 1.0, 0.0), 0.0), g, precision=hi, preferred_element_type=F32)
    btot = jnp.dot(jnp.where(same, 1.0, 0.0), g, precision=hi, preferred_element_type=F32)
    qd = q * jnp.exp(bcum)
    kd = k * jnp.exp(btot - bcum)
    dec = jnp.exp(btot)
    head_sum = jnp.where(lax.broadcasted_iota(jnp.int32, (hk, hv), 0) // (hk // nh)
                         == lax.broadcasted_iota(jnp.int32, (hk, hv), 1) // (hv // nh), 1.0, 0.0).astype(BF16)
    in_chunk = lax.broadcasted_iota(jnp.int32, (tc, hk), 0) % c
    o = jnp.zeros((tc, hv), F32)
    for lag in range(c):
        if lag == 0:
            ks, bs, vs = k, bcum, v
        else:
            shift = tc - lag if reverse else lag
            ks, bs, vs = pltpu.roll(k, shift, 0), pltpu.roll(bcum, shift, 0), pltpu.roll(v, shift, 0)
        valid = (in_chunk + lag <= c - 1) if reverse else (in_chunk >= lag)
        x = q * ks * jnp.exp(jnp.where(valid, bcum - bs, NEG_INF))
        o = o + jnp.dot(x.astype(BF16), head_sum, preferred_element_type=F32) * vs
    head_mask = (lax.broadcasted_iota(jnp.int32, (hv, hk), 0) // (hv // nh)
                 == lax.broadcasted_iota(jnp.int32, (hv, hk), 1) // (hk // nh))
    st = st_ref[...]
    nch = tc // c
    outs = [None] * nch
    for ci in (range(nch - 1, -1, -1) if reverse else range(nch)):
        sl = slice(ci * c, (ci + 1) * c)
        outs[ci] = lax.dot_general(qd[sl].astype(BF16), st.astype(BF16), (((1,), (1,)), ((), ())),
                                   preferred_element_type=F32)
        ds = lax.dot_general(v[sl].astype(BF16), kd[sl].astype(BF16), (((0,), (0,)), ((), ())),
                             preferred_element_type=F32)
        st = st * dec[ci * c:ci * c + 1] + jnp.where(head_mask, ds, 0.0)
    st_ref[...] = st
    return o + jnp.concatenate(outs, axis=0)


def _log_sigmoid(z):
    return jnp.minimum(z, 0.0) - jnp.log1p(jnp.exp(-jnp.abs(z)))


def _hgrn2_kernel(qf_ref, if_ref, zf_ref, qb_ref, ib_ref, zb_ref, lb_ref, s0f_ref, s0b_ref,
                  of_ref, ob_ref, sf_ref, sb_ref, stf, stb):
    j = pl.program_id(1)

    @pl.when(j == 0)
    def _():
        stf[...] = s0f_ref[...]
        stb[...] = s0b_ref[...]

    one_minus_lb, log_lb, log_ub = lb_ref[0:1], lb_ref[1:2], lb_ref[2:3]

    def gate(z):
        return one_minus_lb * jax.nn.sigmoid(-z), jnp.logaddexp(log_lb, log_ub + _log_sigmoid(z))

    silu = lambda a: a * jax.nn.sigmoid(a)
    ld = lambda r: r[...].astype(F32)
    k_f, g_f = gate(ld(zf_ref))
    k_b, g_b = gate(ld(zb_ref))
    of_ref[...] = _glr_direction(silu(ld(qf_ref)), k_f, ld(if_ref), g_f, stf, False, HG_H)
    ob_ref[...] = _glr_direction(silu(ld(qb_ref)), k_b, ld(ib_ref), g_b, stb, True, HG_H)

    @pl.when(j == pl.num_programs(1) - 1)
    def _():
        sf_ref[...] = stf[...]
        sb_ref[...] = stb[...]


def _gla_kernel(qkf_ref, vf_ref, nf_ref, qkb_ref, vb_ref, nb_ref, aup_ref, ab_ref, s0f_ref, s0b_ref,
                of_ref, ob_ref, sf_ref, sb_ref, stf, stb):
    j = pl.program_id(1)

    @pl.when(j == 0)
    def _():
        stf[...] = s0f_ref[...]
        stb[...] = s0b_ref[...]

    def gate(narrow, idx):
        a = narrow[:, idx * GLA_RANK:(idx + 1) * GLA_RANK]
        lin = jnp.dot(a.astype(BF16), aup_ref[idx].astype(BF16), preferred_element_type=F32) + ab_ref[idx]
        return _log_sigmoid(lin) / GLA_NORMALIZER

    ld = lambda r: r[...].astype(F32)
    qk_f = ld(qkf_ref)
    qk_b = ld(qkb_ref)
    of_ref[...] = _glr_direction(qk_f[:, :GLA_KW] * GLA_DK ** -0.5, qk_f[:, GLA_KW:], ld(vf_ref),
                                 gate(ld(nf_ref), 0), stf, False, GLA_H)
    ob_ref[...] = _glr_direction(qk_b[:, :GLA_KW] * GLA_DK ** -0.5, qk_b[:, GLA_KW:], ld(vb_ref),
                                 gate(ld(nb_ref), 1), stb, True, GLA_H)

    @pl.when(j == pl.num_programs(1) - 1)
    def _():
        sf_ref[...] = stf[...]
        sb_ref[...] = stb[...]


def _glr_call(kernel_fn, p3, fwd_cols, bwd_cols, consts, s0_f, s0_b, hk, hv, name):
    bsz, t, _ = p3.shape
    tc = GLR_TILE
    nsb = t // tc
    fwd = lambda w, c: pl.BlockSpec((None, tc, w), lambda b, j: (b, j, c))
    bwd = lambda w, c: pl.BlockSpec((None, tc, w), lambda b, j: (b, nsb - 1 - j, c))
    whole = lambda a: pl.BlockSpec(a.shape, lambda b, j: (0,) * a.ndim)
    st = pl.BlockSpec((None, hv, hk), lambda b, j: (b, 0, 0))
    out_f = pl.BlockSpec((None, tc, hv), lambda b, j: (b, j, 0))
    out_b = pl.BlockSpec((None, tc, hv), lambda b, j: (b, nsb - 1 - j, 0))
    return pl.pallas_call(
        kernel_fn,
        out_shape=[jax.ShapeDtypeStruct((bsz, t, hv), F32), jax.ShapeDtypeStruct((bsz, t, hv), F32),
                   jax.ShapeDtypeStruct((bsz, hv, hk), F32), jax.ShapeDtypeStruct((bsz, hv, hk), F32)],
        grid=(bsz, nsb),
        in_specs=([fwd(w, c) for w, c in fwd_cols] + [bwd(w, c) for w, c in bwd_cols]
                  + [whole(a) for a in consts] + [st, st]),
        out_specs=[out_f, out_b, st, st],
        scratch_shapes=[pltpu.VMEM((hv, hk), F32), pltpu.VMEM((hv, hk), F32)],
        compiler_params=pltpu.CompilerParams(dimension_semantics=("arbitrary", "arbitrary"),
                                             vmem_limit_bytes=VMEM_LIMIT),
        name=name,
    )(*([p3] * (len(fwd_cols) + len(bwd_cols))), *consts, s0_f, s0_b)


def _hgrn2_seq(p3, lb, s0_f, s0_b):
    lb_rows = jnp.stack([1.0 - lb, jnp.log(lb), jnp.log1p(-lb)])
    cols = lambda z: [(GROUP_W, COL_HG_Q), (GROUP_W, COL_HG_I), (GROUP_W, z)]
    o_f, o_b, s_f, s_b = _glr_call(_hgrn2_kernel, p3, cols(COL_HG_ZF), cols(COL_HG_ZB), [lb_rows],
                                   s0_f, s0_b, HG_W, HG_W, "hgrn2_recurrence")
    return (o_f, o_b), s_f, s_b


def _hgrn2(pc3, pl3, lb):
    s0 = jnp.zeros((pl3.shape[0], HG_W, HG_W), F32)
    oc, s_f, s_b = _hgrn2_seq(pc3, lb, s0, s0)
    o, _, _ = _hgrn2_seq(pl3, lb, s_f, s_b)
    return oc, o


def _gla_seq(p3, a_up, a_b, s0_f, s0_b):
    cols = [(GROUP_W, COL_GLA_QK), (GROUP_W, COL_GLA_V), (LANE, COL_NARROW)]
    o_f, o_b, s_f, s_b = _glr_call(_gla_kernel, p3, cols, cols, [a_up, a_b.reshape(2, 1, GLA_KW)],
                                   s0_f, s0_b, GLA_KW, GLA_VW, "gla_recurrence")
    return (o_f, o_b), s_f, s_b


def _gla(pc3, pl3, a_up, a_b):
    s0 = jnp.zeros((pl3.shape[0], GLA_VW, GLA_KW), F32)
    oc, s_f, s_b = _gla_seq(pc3, a_up, a_b, s0, s0)
    o, _, _ = _gla_seq(pl3, a_up, a_b, s_f, s_b)
    return oc, o


def _mlstm_direction(q, k, v, igx, lfx, s_ref, n_ref, m_ref, reverse, nh):
    tc, w = q.shape
    seg = w // nh
    assert tc == seg
    hi = lax.Precision.HIGHEST
    ti = lax.broadcasted_iota(jnp.int32, (tc, tc), 0)
    tj = lax.broadcasted_iota(jnp.int32, (tc, tc), 1)
    seen = (tj >= ti) if reverse else (tj <= ti)
    b = jnp.dot(jnp.where(seen, 1.0, 0.0), lfx, precision=hi, preferred_element_type=F32)
    bl = b[0:1] if reverse else b[tc - 1:tc]
    a = bl - b + igx
    ma = jnp.max(a, axis=0, keepdims=True)
    kw = jnp.exp(a - ma) * k
    s_prev = s_ref[...]
    n_prev = n_ref[...]
    m_prev = m_ref[...]
    lane = lax.broadcasted_iota(jnp.int32, (tc, w), 1)
    row = lax.broadcasted_iota(jnp.int32, (tc, w), 0)
    same_head = (lax.broadcasted_iota(jnp.int32, (w, w), 0) // seg
                 == lax.broadcasted_iota(jnp.int32, (w, w), 1) // seg)
    kexp = jnp.where(same_head, jnp.concatenate([k] * nh, axis=0), 0.0)
    vexp = jnp.where(same_head, jnp.concatenate([v] * nh, axis=0), 0.0)
    scores = lax.dot_general(q.astype(BF16), kexp.astype(BF16), (((1,), (1,)), ((), ())),
                             preferred_element_type=F32)
    s_lane = lane % seg
    by_src = jnp.sum(jnp.where(s_lane == row, igx - b, 0.0), axis=0, keepdims=True)
    ok = (s_lane >= row) if reverse else (s_lane <= row)
    dmat = jnp.where(ok, b + by_src, NEG_INF)
    inter = b + m_prev
    head_of_lane = lane // seg
    seg_max = jnp.full((tc, w), NEG_INF, F32)
    for h in range(nh):
        in_h = head_of_lane == h
        seg_max = jnp.where(in_h, jnp.max(jnp.where(in_h, dmat, NEG_INF), axis=1, keepdims=True), seg_max)
    m_t = jnp.maximum(inter, seg_max)
    wq = jnp.exp(dmat - m_t) * scores
    w_int = jnp.exp(inter - m_t)
    head_sum = jnp.where(same_head, 1.0, 0.0).astype(BF16)
    num = (jnp.dot(wq.astype(BF16), vexp.astype(BF16), preferred_element_type=F32)
           + w_int * jnp.dot(q.astype(BF16), s_prev.astype(BF16), preferred_element_type=F32))
    den = (jnp.dot(wq.astype(BF16), head_sum, preferred_element_type=F32)
           + w_int * jnp.dot((q * n_prev).astype(BF16), head_sum, preferred_element_type=F32))
    h_out = num / jnp.maximum(jnp.abs(den), jnp.exp(-m_t))
    m_new = jnp.maximum(bl + m_prev, ma)
    d_old = jnp.exp(bl + m_prev - m_new)
    d_new = jnp.exp(ma - m_new)
    ds = lax.dot_general(kw.astype(BF16), v.astype(BF16), (((0,), (0,)), ((), ())), preferred_element_type=F32)
    s_ref[...] = d_old * s_prev + d_new * jnp.where(same_head, ds, 0.0)
    n_ref[...] = d_old * n_prev + d_new * jnp.sum(kw, axis=0, keepdims=True)
    m_ref[...] = m_new
    return h_out


def _mlstm_kernel(qf_ref, kf_ref, vf_ref, igf_ref, lff_ref, qb_ref, kb_ref, vb_ref, igb_ref, lfb_ref,
                  s0f_ref, n0f_ref, m0f_ref, s0b_ref, n0b_ref, m0b_ref,
                  hf_ref, hb_ref, sf_ref, nf_ref, mf_ref, sb_ref, nb_ref, mb_ref,
                  s_f, n_f, m_f, s_b, n_b, m_b, *, nh):
    j = pl.program_id(1)

    @pl.when(j == 0)
    def _():
        s_f[...] = s0f_ref[...]
        n_f[...] = n0f_ref[...]
        m_f[...] = m0f_ref[...]
        s_b[...] = s0b_ref[...]
        n_b[...] = n0b_ref[...]
        m_b[...] = m0b_ref[...]

    hf_ref[...] = _mlstm_direction(qf_ref[...], kf_ref[...], vf_ref[...], igf_ref[...], lff_ref[...],
                                   s_f, n_f, m_f, False, nh)
    hb_ref[...] = _mlstm_direction(qb_ref[...], kb_ref[...], vb_ref[...], igb_ref[...], lfb_ref[...],
                                   s_b, n_b, m_b, True, nh)

    @pl.when(j == pl.num_programs(1) - 1)
    def _():
        sf_ref[...] = s_f[...]
        nf_ref[...] = n_f[...]
        mf_ref[...] = m_f[...]
        sb_ref[...] = s_b[...]
        nb_ref[...] = n_b[...]
        mb_ref[...] = m_b[...]


def _mlstm_bidir(q, k, v, ig_f, lf_f, ig_b, lf_b, st_f, st_b, nh):
    bsz, t, w = q.shape
    tc = CHUNK_ML
    nsb = t // tc
    fwd = pl.BlockSpec((None, tc, w), lambda b, j: (b, j, 0))
    bwd = pl.BlockSpec((None, tc, w), lambda b, j: (b, nsb - 1 - j, 0))
    mat = pl.BlockSpec((None, w, w), lambda b, j: (b, 0, 0))
    vec = pl.BlockSpec((None, 1, w), lambda b, j: (b, 0, 0))
    sds = jax.ShapeDtypeStruct
    state_shapes = [sds((bsz, w, w), F32), sds((bsz, 1, w), F32), sds((bsz, 1, w), F32)]
    outs = pl.pallas_call(
        functools.partial(_mlstm_kernel, nh=nh),
        out_shape=[sds((bsz, t, w), F32), sds((bsz, t, w), F32)] + state_shapes + state_shapes,
        grid=(bsz, nsb),
        in_specs=[fwd] * 5 + [bwd] * 5 + [mat, vec, vec] * 2,
        out_specs=[fwd, bwd] + [mat, vec, vec] * 2,
        scratch_shapes=[pltpu.VMEM((w, w), F32), pltpu.VMEM((1, w), F32), pltpu.VMEM((1, w), F32)] * 2,
        compiler_params=pltpu.CompilerParams(dimension_semantics=("arbitrary", "arbitrary"),
                                             vmem_limit_bytes=VMEM_LIMIT),
        name="mlstm_recurrence",
    )(q, k, v, ig_f, lf_f, q, k, v, ig_b, lf_b, *st_f, *st_b)
    return outs[0], outs[1], tuple(outs[2:5]), tuple(outs[5:8])


def _mlstm_seq(p3, conv_w, conv_b, gate_b, st_f, st_b):
    qk = _short_conv(p3, COL_ML_Q, 2, conv_w, conv_b, True)
    v = p3[..., COL_ML_V * GROUP_W:(COL_ML_V + 1) * GROUP_W].astype(F32)
    bsz, t, _ = p3.shape
    gates = p3[..., NARROW_ML_GATES:NARROW_ML_GATES + 4 * ML_H].astype(F32)
    gt = gates.reshape(bsz, t, 4, ML_H) + gate_b
    expand = lambda a: jnp.repeat(a, ML_DH, axis=-1)
    h_f, h_b, fin_f, fin_b = _mlstm_bidir(
        qk[..., :ML_W], qk[..., ML_W:] * ML_DH ** -0.5, v,
        expand(gt[:, :, 0]), expand(jax.nn.log_sigmoid(gt[:, :, 1])),
        expand(gt[:, :, 2]), expand(jax.nn.log_sigmoid(gt[:, :, 3])), st_f, st_b, ML_H)
    return (h_f, h_b), fin_f, fin_b


def _mlstm(pc3, pl3, conv_w, conv_b, gate_b):
    bsz = pl3.shape[0]
    st0 = (jnp.zeros((bsz, ML_W, ML_W), F32), jnp.zeros((bsz, 1, ML_W), F32), jnp.zeros((bsz, 1, ML_W), F32))
    hc, st_f, st_b = _mlstm_seq(pc3, conv_w, conv_b, gate_b, st0, st0)
    h, _, _ = _mlstm_seq(pl3, conv_w, conv_b, gate_b, st_f, st_b)
    return hc, h


def _router_kernel(h_ref, wt_ref, b_ref, eidx_ref, wsel_ref, cnt_ref):
    i = pl.program_id(0)
    tm = h_ref.shape[0]
    ne = wt_ref.shape[0]
    per_group = ne // N_EXPERT_GROUPS
    logits = lax.dot_general(wt_ref[...], h_ref[...], (((1,), (1,)), ((), ())),
                             preferred_element_type=F32, precision=lax.Precision.HIGHEST)
    s = jax.nn.sigmoid(logits)
    sel = s + b_ref[...]
    row = lax.broadcasted_iota(jnp.int32, (ne, tm), 0)
    gs = []
    for g in range(N_EXPERT_GROUPS):
        blk = sel[g * per_group:(g + 1) * per_group]
        r = lax.broadcasted_iota(jnp.int32, blk.shape, 0)
        m1 = jnp.max(blk, axis=0, keepdims=True)
        i1 = jnp.min(jnp.where(blk == m1, r, per_group), axis=0, keepdims=True)
        m2 = jnp.max(jnp.where(r == i1, NEG_INF, blk), axis=0, keepdims=True)
        gs.append(m1 + m2)
    grp = jnp.concatenate(gs, axis=0)
    grow = lax.broadcasted_iota(jnp.int32, grp.shape, 0)
    gsel = jnp.zeros(grp.shape, F32)
    for _ in range(TOPK_GROUPS):
        m = jnp.max(grp, axis=0, keepdims=True)
        gi = jnp.min(jnp.where(grp == m, grow, N_EXPERT_GROUPS), axis=0, keepdims=True)
        hit = grow == gi
        gsel = jnp.where(hit, 1.0, gsel)
        grp = jnp.where(hit, NEG_INF, grp)
    masked = jnp.concatenate(
        [jnp.where(gsel[g:g + 1] > 0.0, sel[g * per_group:(g + 1) * per_group], NEG_INF)
         for g in range(N_EXPERT_GROUPS)], axis=0)
    eis, ws = [], []
    picked = jnp.zeros((ne, tm), F32)
    for _ in range(TOP_K):
        m = jnp.max(masked, axis=0, keepdims=True)
        ei = jnp.min(jnp.where(masked == m, row, ne), axis=0, keepdims=True)
        hit = row == ei
        ws.append(jnp.sum(jnp.where(hit, s, 0.0), axis=0, keepdims=True))
        eis.append(ei)
        picked = jnp.where(hit, 1.0, picked)
        masked = jnp.where(hit, NEG_INF, masked)
    w = jnp.concatenate(ws, axis=0)
    eidx_ref[...] = jnp.concatenate(eis, axis=0)
    wsel_ref[...] = w / jnp.sum(w, axis=0, keepdims=True) * ROUTED_SCALE
    tot = jnp.dot(picked.astype(BF16), jnp.ones((tm, LANE), BF16), preferred_element_type=F32)

    @pl.when(i == 0)
    def _():
        cnt_ref[...] = jnp.zeros_like(cnt_ref)

    cnt_ref[...] += tot


def _pos_kernel(eidx_ref, base_ref, pos_ref, carry_ref):
    i = pl.program_id(0)
    tm = eidx_ref.shape[1]
    ne = base_ref.shape[0]

    @pl.when(i == 0)
    def _():
        carry_ref[...] = jnp.zeros_like(carry_ref)

    eidx = eidx_ref[...]
    row = lax.broadcasted_iota(jnp.int32, (ne, tm), 0)
    picked = jnp.zeros((ne, tm), F32)
    for k in range(TOP_K):
        picked = jnp.where(row == eidx[k:k + 1], 1.0, picked)
    pb = picked.astype(BF16)
    before = jnp.where(lax.broadcasted_iota(jnp.int32, (tm, tm), 0) < lax.broadcasted_iota(jnp.int32, (tm, tm), 1),
                       1.0, 0.0).astype(BF16)
    rank = jnp.dot(pb, before, preferred_element_type=F32)
    tot = jnp.dot(pb, jnp.ones((tm, LANE), BF16), preferred_element_type=F32)
    dest = rank + (base_ref[...] + carry_ref[:, 0:1])
    pos = [jnp.sum(jnp.where(row == eidx[k:k + 1], dest, 0.0), axis=0, keepdims=True) for k in range(TOP_K)]
    pos_ref[...] = jnp.concatenate(pos, axis=0).astype(jnp.int32)
    carry_ref[...] += tot


def _pack_bf16_pairs(x):
    half = x.shape[1] // 2
    bits = lambda a: pltpu.bitcast(a.astype(BF16).astype(F32), jnp.int32)
    return (bits(x[:, half:]) & -65536) | lax.shift_right_logical(bits(x[:, :half]), 16)


def _unpack_bf16_pairs(w):
    return pltpu.bitcast(lax.shift_left(w, 16), F32), pltpu.bitcast(w & -65536, F32)


def _dispatch_kernel(zstart_ref, zlen_ref, nused_ref, pos_ref, h_ref, xs_ref, packed, zeros, sem, zsem, *,
                     n_blocks):
    tm = h_ref.shape[0]
    bm = zeros.shape[0]
    packed[...] = _pack_bf16_pairs(h_ref[...])

    @pl.when(pl.program_id(0) == 0)
    def _():
        zeros[...] = jnp.zeros_like(zeros)

        def zero_copy(start, size):
            return pltpu.make_async_copy(zeros.at[pl.ds(0, size)], xs_ref.at[pl.ds(start, size)], zsem)

        def pieces(e, act):
            start = zstart_ref[e]
            rem = zlen_ref[e]
            ragged = rem & (SUBLANE - 1)
            for q in range(SUBLANE - 1):
                @pl.when(q < ragged)
                def _(q=q):
                    act(zero_copy(start + q, 1))

            start = pl.multiple_of(start + ragged, SUBLANE)
            size = bm // 2
            while size >= SUBLANE:
                @pl.when((rem & size) != 0)
                def _(start=start, size=size):
                    act(zero_copy(start, size))

                start = pl.multiple_of(start + (rem & size), SUBLANE)
                size //= 2

        def loop(act):
            def per_expert(e, carry):
                pieces(e, act)
                return carry

            def per_block(b, carry):
                act(zero_copy(pl.multiple_of(b * bm, bm), bm))
                return carry

            lax.fori_loop(0, zstart_ref.shape[0], per_expert, 0)
            lax.fori_loop(nused_ref[0], n_blocks, per_block, 0)

        loop(lambda cp: cp.start())
        loop(lambda cp: cp.wait())

    def row_copy(n, k):
        return pltpu.make_async_copy(packed.at[pl.ds(n, 1)], xs_ref.at[pl.ds(pos_ref[k, n], 1)], sem)

    def issue(n, carry):
        for k in range(TOP_K):
            row_copy(n, k).start()
        return carry

    def drain(n, carry):
        for k in range(TOP_K):
            row_copy(n, k).wait()
        return carry

    lax.fori_loop(0, tm, issue, 0)
    lax.fori_loop(0, tm, drain, 0)


def _moe_ffn_kernel(blk_e_ref, run_ref, next_e_ref, nused_ref, x_ref, wgu_hbm, wdn_hbm, o_ref,
                    wgu_f, wdn_f, wgu_s, wdn_s, sems, *, layer):
    i = pl.program_id(0)
    e = blk_e_ref[i]
    run = run_ref[i]
    first_of_run = (i == 0) | (run != run_ref[jnp.maximum(i - 1, 0)])

    def fetch(expert, slot):
        return (pltpu.make_async_copy(wgu_hbm.at[layer, expert], wgu_f.at[slot], sems.at[0, slot]),
                pltpu.make_async_copy(wdn_hbm.at[layer, expert], wdn_f.at[slot], sems.at[1, slot]))

    @pl.when(i < nused_ref[0])
    def _():
        @pl.when(first_of_run)
        def _():
            slot = run & 1

            @pl.when(i == 0)
            def _():
                for cp in fetch(e, 0):
                    cp.start()

            for cp in fetch(e, slot):
                cp.wait()
            wgu_s[...] = wgu_f[slot].astype(BF16)
            wdn_s[...] = wdn_f[slot].astype(BF16)

            @pl.when(next_e_ref[i] >= 0)
            def _():
                for cp in fetch(next_e_ref[i], 1 - slot):
                    cp.start()

        x_lo, x_hi = _unpack_bf16_pairs(x_ref[...])
        half = x_lo.shape[1]
        au = (jnp.dot(x_lo.astype(BF16), wgu_s[:half, :], preferred_element_type=F32)
              + jnp.dot(x_hi.astype(BF16), wgu_s[half:, :], preferred_element_type=F32))
        a = au[:, :EXPERT_FF]
        u = au[:, EXPERT_FF:]
        h = (a * jax.nn.sigmoid(a)) * u
        o_ref[...] = _pack_bf16_pairs(jnp.dot(h.astype(BF16), wdn_s[...], preferred_element_type=F32))

    @pl.when(i >= nused_ref[0])
    def _():
        o_ref[...] = jnp.zeros_like(o_ref)


def _combine_kernel(pos_ref, w_ref, t_ref, x_ref, sgu_ref, sdn_ref, gain_ref, g2_ref, y_hbm, o_ref, buf, sem):
    tm = o_ref.shape[0]

    def row_copy(n, k):
        return pltpu.make_async_copy(y_hbm.at[pl.ds(pos_ref[k, n], 1)], buf.at[k, pl.ds(n, 1)], sem)

    def issue(n, carry):
        for k in range(TOP_K):
            row_copy(n, k).start()
        return carry

    def drain(n, carry):
        for k in range(TOP_K):
            row_copy(n, k).wait()
        return carry

    lax.fori_loop(0, tm, issue, 0)
    ff = sdn_ref.shape[0]
    au = jnp.dot(t_ref[...].astype(BF16), sgu_ref[...], preferred_element_type=F32)
    a, u = au[:, :ff], au[:, ff:]
    f = jnp.dot(((a * jax.nn.sigmoid(a)) * u).astype(BF16), sdn_ref[...], preferred_element_type=F32)
    lax.fori_loop(0, tm, drain, 0)
    r_lo, r_hi = None, None
    for k in range(TOP_K):
        y_lo, y_hi = _unpack_bf16_pairs(buf[k])
        wk = w_ref[:, k:k + 1]
        r_lo = y_lo * wk if r_lo is None else r_lo + y_lo * wk
        r_hi = y_hi * wk if r_hi is None else r_hi + y_hi * wk
    f = f + jnp.concatenate([r_lo, r_hi], axis=1)
    o_ref[...] = x_ref[...] + g2_ref[...] * (f * lax.rsqrt(jnp.mean(f * f, axis=-1, keepdims=True) + EPS)
                                             * gain_ref[...])


def _moe(t, x_res, router_w, router_b, w_gu, w_down, sh_gu, sh_down, layer, gain, g2_rows, rows_per_gate):
    n, d = t.shape
    ne = router_w.shape[1]
    ff2 = w_gu.shape[-1]
    params = pltpu.CompilerParams(dimension_semantics=("arbitrary",), vmem_limit_bytes=VMEM_LIMIT)
    tm = ROUTER_TILE
    eidx, wsel, cnt = pl.pallas_call(
        _router_kernel,
        out_shape=[jax.ShapeDtypeStruct((TOP_K, n), jnp.int32), jax.ShapeDtypeStruct((TOP_K, n), F32),
                   jax.ShapeDtypeStruct((ne, LANE), F32)],
        grid=(n // tm,),
        in_specs=[pl.BlockSpec((tm, d), lambda i: (i, 0)), pl.BlockSpec((ne, d), lambda i: (0, 0)),
                  pl.BlockSpec((ne, 1), lambda i: (0, 0))],
        out_specs=[pl.BlockSpec((TOP_K, tm), lambda i: (0, i)), pl.BlockSpec((TOP_K, tm), lambda i: (0, i)),
                   pl.BlockSpec((ne, LANE), lambda i: (0, 0))],
        compiler_params=params, name="moe_router",
    )(t, router_w.T, router_b.reshape(ne, 1))
    bm = MOE_ROWS
    counts = cnt[:, 0].astype(jnp.int32)
    padded = (counts + bm - 1) // bm * bm
    pad_end = jnp.cumsum(padded)
    pad_start = pad_end - padded
    n_blocks = (n * TOP_K + ne * (bm - 1)) // bm + 1
    blk_first = jnp.arange(n_blocks, dtype=jnp.int32) * bm
    blk_e = jnp.minimum(jnp.sum((pad_end[None, :] <= blk_first[:, None]).astype(jnp.int32), axis=1), ne - 1)
    n_used = (pad_end[-1] // bm).astype(jnp.int32).reshape(1)
    pos = pl.pallas_call(
        _pos_kernel,
        out_shape=jax.ShapeDtypeStruct((TOP_K, n), jnp.int32),
        grid=(n // tm,),
        in_specs=[pl.BlockSpec((TOP_K, tm), lambda i: (0, i)), pl.BlockSpec((ne, 1), lambda i: (0, 0))],
        out_specs=pl.BlockSpec((TOP_K, tm), lambda i: (0, i)),
        scratch_shapes=[pltpu.VMEM((ne, LANE), F32)],
        compiler_params=params, name="moe_positions",
    )(eidx, pad_start.astype(F32).reshape(ne, 1))
    ts = SCATTER_TILE
    p = n_blocks * bm
    pos_spec = pl.BlockSpec((TOP_K, ts), lambda i: (0, i), memory_space=pltpu.SMEM)
    dp = d // 2
    xs = pl.pallas_call(
        functools.partial(_dispatch_kernel, n_blocks=n_blocks),
        out_shape=jax.ShapeDtypeStruct((p, dp), jnp.int32),
        grid_spec=pltpu.PrefetchScalarGridSpec(
            num_scalar_prefetch=3,
            grid=(n // ts,),
            in_specs=[pl.BlockSpec((TOP_K, ts), lambda i, *_: (0, i), memory_space=pltpu.SMEM),
                      pl.BlockSpec((ts, d), lambda i, *_: (i, 0))],
            out_specs=pl.BlockSpec(memory_space=pl.ANY),
            scratch_shapes=[pltpu.VMEM((ts, dp), jnp.int32), pltpu.VMEM((bm, dp), jnp.int32),
                            pltpu.SemaphoreType.DMA, pltpu.SemaphoreType.DMA],
        ),
        compiler_params=params, name="moe_dispatch",
    )(pad_start + counts, padded - counts, n_used, pos, t)

    is_start = jnp.concatenate([jnp.ones((1,), bool), blk_e[1:] != blk_e[:-1]])
    run_id = jnp.cumsum(is_start.astype(jnp.int32)) - 1
    ids = jnp.arange(ne, dtype=jnp.int32)
    later_busy = jnp.where((counts[None, :] > 0) & (ids[None, :] > ids[:, None]), ids[None, :], ne)
    next_busy = jnp.min(later_busy, axis=1)
    next_e = jnp.sum(jnp.where(blk_e[:, None] == ids[None, :], next_busy[None, :], 0), axis=1)
    next_e = jnp.where(next_e >= ne, -1, next_e).astype(jnp.int32)

    def x_map(i, blk_e, run, nxt, nused):
        return (jnp.minimum(i, nused[0] - 1), 0)

    y_p = pl.pallas_call(
        functools.partial(_moe_ffn_kernel, layer=layer),
        out_shape=jax.ShapeDtypeStruct((p, dp), jnp.int32),
        grid_spec=pltpu.PrefetchScalarGridSpec(
            num_scalar_prefetch=4,
            grid=(n_blocks,),
            in_specs=[pl.BlockSpec((bm, dp), x_map),
                      pl.BlockSpec(memory_space=pl.ANY), pl.BlockSpec(memory_space=pl.ANY)],
            out_specs=pl.BlockSpec((bm, dp), lambda i, *_: (i, 0)),
            scratch_shapes=[pltpu.VMEM((2, d, ff2), F32), pltpu.VMEM((2, ff2 // 2, d), F32),
                            pltpu.VMEM((d, ff2), BF16), pltpu.VMEM((ff2 // 2, d), BF16),
                            pltpu.SemaphoreType.DMA((2, 2))],
        ),
        compiler_params=params, name="moe_expert_ffn",
    )(blk_e, run_id, next_e, n_used, xs, w_gu, w_down)
    rows = pl.BlockSpec((ts, d), lambda i: (i, 0))
    whole = lambda a: pl.BlockSpec(a.shape, lambda i: (0,) * a.ndim)
    last_gate = g2_rows.shape[0] - 1
    gate_spec = pl.BlockSpec((None, 1, d), lambda i: (jnp.minimum(i // (rows_per_gate // ts), last_gate), 0, 0))
    consts = [sh_gu.astype(BF16), sh_down.astype(BF16), gain.reshape(1, d)]
    return pl.pallas_call(
        _combine_kernel,
        out_shape=jax.ShapeDtypeStruct((n, d), F32),
        grid=(n // ts,),
        in_specs=([pos_spec, pl.BlockSpec((ts, TOP_K), lambda i: (i, 0)), rows, rows]
                  + [whole(a) for a in consts] + [gate_spec, pl.BlockSpec(memory_space=pl.ANY)]),
        out_specs=rows,
        scratch_shapes=[pltpu.VMEM((TOP_K, ts, dp), jnp.int32), pltpu.SemaphoreType.DMA],
        compiler_params=params, name="moe_combine",
    )(pos, wsel.T, t, x_res, *consts, g2_rows, y_p)


def kernel(x, c, ctx, c_ctx, ada_w, ada_b, norm_g, w_in, w_out, hy_conv_w, hy_conv_b, hy_ffn_w1, hy_ffn_b1, hy_ffn_w2, hy_ffn_b2, hy_ffn_w3, hy_freq, hy_bias, hy_norm, hg_lb_logits, hg_norm, gla_a_up, gla_a_b, gla_norm, ml_conv_w, ml_conv_b, ml_gate_b, ml_norm, router_w, router_b, exp_w_gu, exp_w_down, sh_w_gu, sh_w_down):
    bsz, seq, d = x.shape
    n_ctx = ctx.shape[1]
    depth = ada_w.shape[0]
    rows = seq // GRID_W
    x = x + _pos_embed_2d(rows, d)[None]
    xc = ctx
    lb_cum = jnp.cumsum(jax.nn.softmax(hg_lb_logits, axis=0), axis=0)
    lower_bounds = lb_cum - lb_cum[0:1]
    for l in range(depth):
        with_ctx = l < depth - 1
        mod = (jax.nn.silu(c) @ ada_w[l] + ada_b[l])[:, None, :]
        mod_c = jax.nn.silu(c_ctx) @ ada_w[l] + ada_b[l]
        sh1, sc1, g1, sh2, sc2, g2 = jnp.split(mod, 6, axis=-1)
        csh1, csc1, cg1, csh2, csc2, cg2 = jnp.split(mod_c, 6, axis=-1)
        w_in_l = _arrange_w_in(w_in[l])
        ctx_rows = lambda a: jnp.broadcast_to(a.reshape(1, 1, d), (bsz, 1, d))
        pl3 = _in_proj(x, norm_g[l, 0], sc1, sh1, w_in_l)
        pc3 = _in_proj(xc, norm_g[l, 0], ctx_rows(csc1), ctx_rows(csh1), w_in_l)
        hy_args = (hy_conv_w[l], hy_conv_b[l], hy_ffn_w1[l], hy_ffn_b1[l], hy_ffn_w2[l], hy_ffn_b2[l],
                   hy_ffn_w3[l], hy_freq[l], hy_bias[l], hy_norm[l])
        y_hy = _hyena_long(pl3, *hy_args)
        oc_hg, o_hg = _hgrn2(pc3, pl3, lower_bounds[l])
        oc_gla, o_gla = _gla(pc3, pl3, gla_a_up[l], gla_a_b[l])
        hc_ml, h_ml = _mlstm(pc3, pl3, ml_conv_w[l], ml_conv_b[l], ml_gate_b[l])
        head_gains = jnp.stack([hg_norm[l], gla_norm[l], ml_norm[l]])
        out_args = (w_out[l], head_gains, norm_g[l, 1], norm_g[l, 2])
        x, h = _mixer_out(y_hy, o_hg, o_gla, h_ml, pl3, x, *out_args, jnp.concatenate([g1, sc2, sh2], axis=1))
        moe_args = (router_w[l], router_b[l], exp_w_gu, exp_w_down, sh_w_gu[l], sh_w_down[l], l, norm_g[l, 3])
        if with_ctx:
            mod_ctx = jnp.broadcast_to(jnp.stack([cg1, csc2, csh2])[None], (bsz, 3, d))
            xc, hc = _mixer_out(_hyena(pc3, *hy_args), oc_hg, oc_gla, hc_ml, pc3, xc, *out_args, mod_ctx)
            tokens = lambda a, ac: jnp.concatenate([a.reshape(bsz * seq, d), ac.reshape(bsz * n_ctx, d)], axis=0)
            gates = jnp.concatenate([g2, cg2.reshape(1, 1, d)], axis=0)
            x_all = _moe(tokens(h, hc), tokens(x, xc), *moe_args, gates, seq)
            x = x_all[:bsz * seq].reshape(bsz, seq, d)
            xc = x_all[bsz * seq:].reshape(bsz, n_ctx, d)
        else:
            x = _moe(h.reshape(bsz * seq, d), x.reshape(bsz * seq, d), *moe_args, g2, seq).reshape(bsz, seq, d)
    return x
```

```python
import functools
import math

import jax
import jax.numpy as jnp
import numpy as np
from jax import lax
from jax.experimental import pallas as pl
from jax.experimental.pallas import tpu as pltpu

F32 = jnp.float32
BF16 = jnp.bfloat16

D_MODEL = 1024
GRID_W = 64
EPS = 1e-6
POS_BASE = 10000.0
GROUP_W = D_MODEL // 4
SHORT_CONV = 3
HY_W = GROUP_W
HY_ORDER = 2
HY_EMB = 33
HY_BANDS = (HY_EMB - 1) // 2
HY_FAST_DECAY = 0.3
HY_SLOW_DECAY = 1.5
HY_DECAY_TARGET = 1e-2
HG_H = 4
HG_W = GROUP_W
HG_DK = HG_W // HG_H
GLA_H = 4
GLA_KW = GROUP_W // 2
GLA_VW = GROUP_W
GLA_DK = GLA_KW // GLA_H
GLA_DV = GLA_VW // GLA_H
GLA_RANK = 16
GLA_NORMALIZER = 16.0
ML_H = 4
ML_W = GROUP_W
ML_DH = ML_W // ML_H
CHUNK_GATED = 16
CHUNK_ML = 64
N_EXPERTS = 256
TOP_K = 8
N_EXPERT_GROUPS = 8
TOPK_GROUPS = 4
EXPERT_FF = 256
ROUTED_SCALE = 2.5
IN_SPLITS = (HY_W, HY_W, HY_W,
             HG_W, HG_W, HG_W, HG_W, HG_W,
             GLA_KW, GLA_KW, GLA_VW, GLA_RANK, GLA_RANK, GLA_VW,
             ML_W, ML_W, ML_W, 4 * ML_H, ML_W)
P_ORDER = (0, 1, 2, 3, 4, 5, 6, 7, 8, 9, 10, 13, 14, 15, 16, 18, 11, 12, 17)
COL_HY = 0
COL_HG_Q, COL_HG_I, COL_HG_ZF, COL_HG_ZB, COL_HG_G = 3, 4, 5, 6, 7
COL_GLA_QK, COL_GLA_V, COL_GLA_R = 8, 9, 10
COL_ML_Q, COL_ML_K, COL_ML_V, COL_ML_O = 11, 12, 13, 14
N_WIDE = 15

LANE = 128
SUBLANE = 8
ROW_TILE = 512
MOE_ROWS = 256
ROUTER_TILE = 256
SCATTER_TILE = 256
GLR_TILE = 128
VMEM_LIMIT = 56 * 1024 * 1024
NEG_INF = float("-inf")
COL_NARROW = N_WIDE * GROUP_W // LANE
NARROW_ML_GATES = N_WIDE * GROUP_W + 2 * GLA_RANK
P_WIDTH = N_WIDE * GROUP_W + LANE


def _arrange_w_in(w):
    offs = np.concatenate([[0], np.cumsum(IN_SPLITS)])
    cols = [w[:, offs[i]:offs[i + 1]] for i in P_ORDER]
    used = sum(IN_SPLITS)
    return jnp.concatenate(cols + [jnp.zeros((w.shape[0], P_WIDTH - used), w.dtype)], axis=1).astype(BF16)


def _in_proj_kernel(x_ref, g_ref, sc_ref, sh_ref, w_ref, o_ref):
    x = x_ref[...]
    y = x * lax.rsqrt(jnp.mean(x * x, axis=-1, keepdims=True) + EPS) * g_ref[...]
    h = y * (1.0 + sc_ref[...]) + sh_ref[...]
    o_ref[...] = jnp.dot(h.astype(BF16), w_ref[...], preferred_element_type=F32)


def _in_proj(x, gain, scale, shift, w):
    g, r, d = x.shape
    n = w.shape[1]
    tm = min(ROW_TILE, r)
    assert r % tm == 0
    return pl.pallas_call(
        _in_proj_kernel,
        out_shape=jax.ShapeDtypeStruct((g, r, n), F32),
        grid=(g, r // tm),
        in_specs=[pl.BlockSpec((None, tm, d), lambda b, i: (b, i, 0)),
                  pl.BlockSpec((1, d), lambda b, i: (0, 0)),
                  pl.BlockSpec((None, 1, d), lambda b, i: (b, 0, 0)),
                  pl.BlockSpec((None, 1, d), lambda b, i: (b, 0, 0)),
                  pl.BlockSpec((d, n), lambda b, i: (0, 0))],
        out_specs=pl.BlockSpec((None, tm, n), lambda b, i: (b, i, 0)),
        compiler_params=pltpu.CompilerParams(dimension_semantics=("arbitrary", "arbitrary"),
                                             vmem_limit_bytes=VMEM_LIMIT),
        name="input_projection",
    )(x, gain.reshape(1, d), scale, shift, w)


def _mixer_out_kernel(hy_ref, hgf_ref, hgb_ref, glf_ref, glb_ref, mlf_ref, mlb_ref, ghg_ref, ggl_ref, gml_ref,
                      x_ref, w_ref, hn_ref, n1_ref, n2_ref, mod_ref, xo_ref, h_ref):
    gw = hy_ref.shape[-1]
    seg = jnp.where(lax.broadcasted_iota(jnp.int32, (gw, gw), 0) // HG_DK
                    == lax.broadcasted_iota(jnp.int32, (gw, gw), 1) // HG_DK, 1.0 / HG_DK, 0.0)

    def head_norm(o, gain):
        ms = jnp.dot(o * o, seg, precision=lax.Precision.HIGHEST, preferred_element_type=F32)
        return o * lax.rsqrt(ms + EPS) * gain

    silu = lambda a: a * jax.nn.sigmoid(a)
    groups = (hy_ref[...],
              head_norm(hgf_ref[...] + hgb_ref[...], hn_ref[0:1]) * silu(ghg_ref[...]),
              head_norm(glf_ref[...] + glb_ref[...], hn_ref[1:2]) * silu(ggl_ref[...]),
              jax.nn.sigmoid(gml_ref[...]) * head_norm(mlf_ref[...] + mlb_ref[...], hn_ref[2:3]))
    y = None
    for i, part in enumerate(groups):
        term = jnp.dot(part.astype(BF16), w_ref[i * gw:(i + 1) * gw, :], preferred_element_type=F32)
        y = term if y is None else y + term
    rms = lambda a, g: a * lax.rsqrt(jnp.mean(a * a, axis=-1, keepdims=True) + EPS) * g
    x = x_ref[...] + mod_ref[0:1] * rms(y, n1_ref[...])
    xo_ref[...] = x
    h_ref[...] = rms(x, n2_ref[...]) * (1.0 + mod_ref[1:2]) + mod_ref[2:3]


def _mixer_out(y_hy, o_hg, o_gla, h_ml, p3, x, w_out, head_gains, gain1, gain2, mod):
    assert HG_DK == GLA_DV == ML_DH and HG_H == GLA_H == ML_H
    bsz, t, d = x.shape
    gw = GROUP_W
    tm = min(ROW_TILE, t)
    part = pl.BlockSpec((None, tm, gw), lambda b, i: (b, i, 0))
    gate = lambda c: pl.BlockSpec((None, tm, gw), lambda b, i: (b, i, c))
    full = pl.BlockSpec((None, tm, d), lambda b, i: (b, i, 0))
    whole = lambda a: pl.BlockSpec(a.shape, lambda b, i: (0,) * a.ndim)
    consts = [w_out.astype(BF16), head_gains, gain1.reshape(1, d), gain2.reshape(1, d)]
    return pl.pallas_call(
        _mixer_out_kernel,
        out_shape=[jax.ShapeDtypeStruct((bsz, t, d), F32), jax.ShapeDtypeStruct((bsz, t, d), F32)],
        grid=(bsz, t // tm),
        in_specs=([part] * 7 + [gate(COL_HG_G), gate(COL_GLA_R), gate(COL_ML_O), full]
                  + [whole(a) for a in consts] + [pl.BlockSpec((None, 3, d), lambda b, i: (b, 0, 0))]),
        out_specs=[full, full],
        compiler_params=pltpu.CompilerParams(dimension_semantics=("arbitrary", "arbitrary"),
                                             vmem_limit_bytes=VMEM_LIMIT),
        name="mixer_output",
    )(y_hy, *o_hg, *o_gla, *h_ml, p3, p3, p3, x, *consts, mod)


def _rms(x, g):
    return x * lax.rsqrt(jnp.mean(x * x, axis=-1, keepdims=True) + EPS) * g


def _short_conv_kernel(prev_ref, cur_ref, next_ref, w_ref, b_ref, o_ref, *, act):
    j = pl.program_id(1)
    u = cur_ref[...]
    tt = u.shape[0]
    row = lax.broadcasted_iota(jnp.int32, u.shape, 0)
    before = jnp.where(j > 0, prev_ref[SUBLANE - 1:SUBLANE, :], 0.0)
    after = jnp.where(j < pl.num_programs(1) - 1, next_ref[0:1, :], 0.0)
    up = jnp.where(row == 0, before, pltpu.roll(u, 1, 0))
    dn = jnp.where(row == tt - 1, after, pltpu.roll(u, tt - 1, 0))
    y = w_ref[0:1] * up + w_ref[1:2] * u + w_ref[2:3] * dn + b_ref[...]
    if act:
        y = y * jax.nn.sigmoid(y)
    o_ref[...] = y


def _short_conv(p3, col0, ncols, w, b, act):
    assert SHORT_CONV == 3
    bsz, t, _ = p3.shape
    tt = min(ROW_TILE, t)
    halo = tt // SUBLANE
    last = t // SUBLANE - 1
    gw = GROUP_W
    cur = pl.BlockSpec((None, tt, gw), lambda bi, j, c: (bi, j, col0 + c))
    prev = pl.BlockSpec((None, SUBLANE, gw), lambda bi, j, c: (bi, jnp.maximum(j * halo - 1, 0), col0 + c))
    nxt = pl.BlockSpec((None, SUBLANE, gw), lambda bi, j, c: (bi, jnp.minimum((j + 1) * halo, last), col0 + c))
    return pl.pallas_call(
        functools.partial(_short_conv_kernel, act=act),
        out_shape=jax.ShapeDtypeStruct((bsz, t, ncols * gw), F32),
        grid=(bsz, t // tt, ncols),
        in_specs=[prev, cur, nxt, pl.BlockSpec((SHORT_CONV, gw), lambda bi, j, c: (0, c)),
                  pl.BlockSpec((1, gw), lambda bi, j, c: (0, c))],
        out_specs=pl.BlockSpec((None, tt, gw), lambda bi, j, c: (bi, j, c)),
        compiler_params=pltpu.CompilerParams(dimension_semantics=("arbitrary",) * 3, vmem_limit_bytes=VMEM_LIMIT),
        name="short_conv",
    )(p3, p3, p3, w, b.reshape(1, ncols * gw))


def _pos_embed_2d(rows, d):
    r = jnp.repeat(jnp.arange(rows, dtype=F32), GRID_W)
    col = (jnp.arange(rows * GRID_W) % GRID_W).astype(F32)
    quarter = d // 4
    omega = 1.0 / (POS_BASE ** (jnp.arange(quarter, dtype=F32) / quarter))

    def axis_emb(p):
        ang = p[:, None] * omega[None, :]
        return jnp.concatenate([jnp.sin(ang), jnp.cos(ang)], axis=-1)

    return jnp.concatenate([axis_emb(r), axis_emb(col)], axis=-1)


def _hyena_spectra(L, w1, b1, w2, b2, w3, freq):
    t = jnp.linspace(0.0, 1.0, L, dtype=F32)[:, None]
    w = 2.0 * math.pi * jnp.arange(L, dtype=F32)[:, None] / L
    bands = jnp.linspace(1e-4, HY_BANDS - 1, HY_BANDS, dtype=F32)[None, :]
    feats = jnp.concatenate([t, jnp.cos(bands * w), -jnp.sin(bands * w)], axis=-1)
    z = jnp.sin(freq[0] * (feats @ w1 + b1))
    z = jnp.sin(freq[1] * (z @ w2 + b2))
    h = (z @ w3).reshape(L, HY_ORDER, 2, HY_W)
    max_decay = math.log(HY_DECAY_TARGET) / HY_FAST_DECAY
    min_decay = math.log(HY_DECAY_TARGET) / HY_SLOW_DECAY
    deltas = jnp.abs(jnp.linspace(min_decay, max_decay, HY_W, dtype=F32))
    h = h * jnp.exp(-t[:, :, None, None] * deltas)
    fwd = h[:, :, 0]
    bwd = h[1:, :, 1][::-1]
    l1 = jnp.sum(jnp.abs(fwd), axis=0) + jnp.sum(jnp.abs(bwd), axis=0)
    filt = jnp.concatenate([fwd, jnp.zeros((1, HY_ORDER, HY_W), F32), bwd], axis=0) / l1
    return jnp.fft.rfft(filt, axis=0)


def _fft_conv(u, spec, bias):
    L = u.shape[1]
    y = jnp.fft.irfft(jnp.fft.rfft(u, n=2 * L, axis=1) * spec, n=2 * L, axis=1)[:, :L]
    return y + u * bias


def _hyena(p3, conv_w, conv_b, w1, b1, w2, b2, w3, freq, bias, norm_g):
    u = _short_conv(p3, COL_HY, 3, conv_w, conv_b, False)
    v, x1, x2 = u[..., :HY_W], u[..., HY_W:2 * HY_W], u[..., 2 * HY_W:]
    spec = _hyena_spectra(u.shape[1], w1, b1, w2, b2, w3, freq)
    z = x1 * _fft_conv(v, spec[:, 0], bias[0])
    y = x2 * _fft_conv(z, spec[:, 1], bias[1])
    return _rms(y, norm_g)


FFT_N1 = 128
FFT_N2 = 128
FFT_KTILE = 8
FFT_NTILE = 4096


def _dft_tables(n1, n2):
    n = n1 * n2
    k = np.arange(n1)
    f1 = np.exp(-2j * np.pi * np.outer(k, k) / n1)
    f2 = np.exp(-2j * np.pi * np.outer(np.arange(n2), np.arange(n2)) / n2)
    tw = np.exp(-2j * np.pi * np.outer(np.arange(n1), np.arange(n2)) / n)
    as32 = lambda a: jnp.asarray(np.ascontiguousarray(a), F32)
    f1_fwd = as32(np.concatenate([f1.real, f1.imag], axis=0))
    f1_inv = as32(np.concatenate([f1.real, f1.imag], axis=1) / n)
    f2_inv = as32(np.block([[f2.real, f2.imag], [-f2.imag, f2.real]]))
    return f1_fwd, f1_inv, as32(f2.real), as32(f2.imag), f2_inv, as32(tw.real), as32(tw.imag)


def _stage_kernel(w_ref, x_ref, o_ref):
    o_ref[...] = jnp.dot(w_ref[...].astype(BF16), x_ref[...].astype(BF16),
                         preferred_element_type=F32).astype(o_ref.dtype)


def _stage_matmul(w, x, out_dtype):
    g, k, n = x.shape
    m = w.shape[0]
    tn = FFT_NTILE
    return pl.pallas_call(
        _stage_kernel,
        out_shape=jax.ShapeDtypeStruct((g, m, n), out_dtype),
        grid=(g, n // tn),
        in_specs=[pl.BlockSpec((m, k), lambda b, j: (0, 0)), pl.BlockSpec((None, k, tn), lambda b, j: (b, 0, j))],
        out_specs=pl.BlockSpec((None, m, tn), lambda b, j: (b, 0, j)),
        compiler_params=pltpu.CompilerParams(dimension_semantics=("arbitrary", "arbitrary"),
                                             vmem_limit_bytes=VMEM_LIMIT),
        name="dft_stage",
    )(w, x)


def _twiddled_f2(f2r, f2i, tr, ti):
    gr = f2r * tr - f2i * ti
    gi = f2r * ti + f2i * tr
    return jnp.concatenate([jnp.concatenate([gr, -gi], axis=1), jnp.concatenate([gi, gr], axis=1)], axis=0)


def _spectrum_kernel(a_ref, f2r_ref, f2i_ref, tr_ref, ti_ref, x_ref):
    n2 = f2r_ref.shape[0]
    tr = tr_ref[...]
    ti = ti_ref[...]
    for i in range(a_ref.shape[0]):
        g = _twiddled_f2(f2r_ref[...], f2i_ref[...], tr[i:i + 1], ti[i:i + 1])
        x_ref[i] = jnp.dot(g.astype(BF16), a_ref[i], preferred_element_type=F32)


def _conv_mid_kernel(a_ref, h_ref, f2r_ref, f2i_ref, f2inv_ref, tr_ref, ti_ref, z_ref):
    n2 = f2r_ref.shape[0]
    tr = tr_ref[...]
    ti = ti_ref[...]
    tr_col = tr.T
    ti_col = ti.T
    f2inv = f2inv_ref[...].astype(BF16)
    for i in range(a_ref.shape[0]):
        g = _twiddled_f2(f2r_ref[...], f2i_ref[...], tr[i:i + 1], ti[i:i + 1])
        x = jnp.dot(g.astype(BF16), a_ref[i], preferred_element_type=F32)
        xr, xi = x[:n2], x[n2:]
        hr, hi = h_ref[i, :n2], h_ref[i, n2:]
        y = jnp.concatenate([hr * xr - hi * xi, hr * xi + hi * xr], axis=0)
        w = jnp.dot(f2inv, y.astype(BF16), preferred_element_type=F32)
        wr, wi = w[:n2], w[n2:]
        cr, ci = tr_col[:, i:i + 1], ti_col[:, i:i + 1]
        z_ref[i] = jnp.concatenate([cr * wr + ci * wi, cr * wi - ci * wr], axis=0).astype(z_ref.dtype)


def _dft_mid_specs(g, c):
    n1, n2, kt = FFT_N1, FFT_N2, FFT_KTILE
    blk = pl.BlockSpec((None, kt, 2 * n2, c), lambda b, j: (b, j, 0, 0))
    const = lambda r, cc: pl.BlockSpec((r, cc), lambda b, j: (0, 0))
    twid = pl.BlockSpec((kt, n2), lambda b, j: (j, 0))
    params = pltpu.CompilerParams(dimension_semantics=("arbitrary", "arbitrary"), vmem_limit_bytes=VMEM_LIMIT)
    return blk, const, twid, params, (g, n1 // kt)


def _to_k1_major(a2d, c):
    g = a2d.shape[0]
    return a2d.reshape(g, 2, FFT_N1, FFT_N2, c).transpose(0, 2, 1, 3, 4).reshape(g, FFT_N1, 2 * FFT_N2, c)


def _filter_spectrum(filt, tables):
    g, n, c = filt.shape
    f1_fwd, _, f2r, f2i, _, twr, twi = tables
    a = _stage_matmul(f1_fwd, filt.reshape(g, FFT_N1, FFT_N2 * c), BF16)
    blk, const, twid, params, grid = _dft_mid_specs(g, c)
    return pl.pallas_call(
        _spectrum_kernel,
        out_shape=jax.ShapeDtypeStruct((g, FFT_N1, 2 * FFT_N2, c), F32),
        grid=grid,
        in_specs=[blk, const(FFT_N2, FFT_N2), const(FFT_N2, FFT_N2), twid, twid],
        out_specs=blk, compiler_params=params, name="dft_spectrum",
    )(_to_k1_major(a, c), f2r, f2i, twr, twi)


def _long_conv(u, spec, tables):
    g, l, c = u.shape
    f1_fwd, f1_inv, f2r, f2i, f2inv, twr, twi = tables
    half = l // FFT_N2
    a = _stage_matmul(f1_fwd[:, :half], u.reshape(g, half, FFT_N2 * c), BF16)
    blk, const, twid, params, grid = _dft_mid_specs(g, c)
    hspec = pl.BlockSpec((FFT_KTILE, 2 * FFT_N2, c), lambda b, j: (j, 0, 0))
    z = pl.pallas_call(
        _conv_mid_kernel,
        out_shape=jax.ShapeDtypeStruct((g, FFT_N1, 2 * FFT_N2, c), BF16),
        grid=grid,
        in_specs=[blk, hspec, const(FFT_N2, FFT_N2), const(FFT_N2, FFT_N2), const(2 * FFT_N2, 2 * FFT_N2),
                  twid, twid],
        out_specs=blk, compiler_params=params, name="dft_conv_mid",
    )(_to_k1_major(a, c), spec, f2r, f2i, f2inv, twr, twi)
    z2d = z.reshape(g, FFT_N1, 2, FFT_N2, c).transpose(0, 2, 1, 3, 4).reshape(g, 2 * FFT_N1, FFT_N2 * c)
    y = _stage_matmul(f1_inv[:half], z2d, F32)
    return y.reshape(g, l, c)


def _hyena_filters(L, w1, b1, w2, b2, w3, freq):
    t = jnp.linspace(0.0, 1.0, L, dtype=F32)[:, None]
    w = 2.0 * math.pi * jnp.arange(L, dtype=F32)[:, None] / L
    bands = jnp.linspace(1e-4, HY_BANDS - 1, HY_BANDS, dtype=F32)[None, :]
    feats = jnp.concatenate([t, jnp.cos(bands * w), -jnp.sin(bands * w)], axis=-1)
    max_decay = math.log(HY_DECAY_TARGET) / HY_FAST_DECAY
    min_decay = math.log(HY_DECAY_TARGET) / HY_SLOW_DECAY
    deltas = jnp.abs(jnp.linspace(min_decay, max_decay, HY_W, dtype=F32))
    w3d = w3.reshape(w3.shape[0], HY_ORDER, 2, HY_W)

    def side(f, tt, direction):
        z = jnp.sin(freq[0] * (f @ w1 + b1))
        z = jnp.sin(freq[1] * (z @ w2 + b2))
        h = (z @ w3d[:, :, direction].reshape(w3.shape[0], HY_ORDER * HY_W)).reshape(-1, HY_ORDER, HY_W)
        return h * jnp.exp(-tt[:, :, None] * deltas)

    fwd = side(feats, t, 0)
    bwd = side(feats[::-1], t[::-1], 1)[:L - 1]
    l1 = jnp.sum(jnp.abs(fwd), axis=0) + jnp.sum(jnp.abs(bwd), axis=0)
    return jnp.concatenate([fwd, jnp.zeros((1, HY_ORDER, HY_W), F32), bwd], axis=0) / l1


def _hyena_long(p3, conv_w, conv_b, w1, b1, w2, b2, w3, freq, bias, norm_g):
    u = _short_conv(p3, COL_HY, 3, conv_w, conv_b, False)
    v, x1, x2 = u[..., :HY_W], u[..., HY_W:2 * HY_W], u[..., 2 * HY_W:]
    L = u.shape[1]
    assert 2 * L == FFT_N1 * FFT_N2
    tables = _dft_tables(FFT_N1, FFT_N2)
    filt = _hyena_filters(L, w1, b1, w2, b2, w3, freq)
    spec = _filter_spectrum(jnp.moveaxis(filt, 1, 0), tables)
    z = x1 * (_long_conv(v, spec[0], tables) + v * bias[0])
    y = x2 * (_long_conv(z, spec[1], tables) + z * bias[1])
    return _rms(y, norm_g)


def _glr_direction(q, k, v, g, st_ref, reverse, nh):
    tc, hk = k.shape
    hv = v.shape[1]
    c = CHUNK_GATED
    hi = lax.Precision.HIGHEST
    ti = lax.broadcasted_iota(jnp.int32, (tc, tc), 0)
    tj = lax.broadcasted_iota(jnp.int32, (tc, tc), 1)
    same = (ti // c) == (tj // c)
    seen = (tj >= ti) if reverse else (tj <= ti)
    bcum = jnp.dot(jnp.where(same, jnp.where(seen, 1.0, 0.0), 0.0), g, precision=hi, preferred_element_type=F32)
    btot = jnp.dot(jnp.where(same, 1.0, 0.0), g, precision=hi, preferred_element_type=F32)
    qd = q * jnp.exp(bcum)
    kd = k * jnp.exp(btot - bcum)
    dec = jnp.exp(btot)
    head_sum = jnp.where(lax.broadcasted_iota(jnp.int32, (hk, hv), 0) // (hk // nh)
                         == lax.broadcasted_iota(jnp.int32, (hk, hv), 1) // (hv // nh), 1.0, 0.0).astype(BF16)
    in_chunk = lax.broadcasted_iota(jnp.int32, (tc, hk), 0) % c
    o = jnp.zeros((tc, hv), F32)
    for lag in range(c):
        if lag == 0:
            ks, bs, vs = k, bcum, v
        else:
            shift = tc - lag if reverse else lag
            ks, bs, vs = pltpu.roll(k, shift, 0), pltpu.roll(bcum, shift, 0), pltpu.roll(v, shift, 0)
        valid = (in_chunk + lag <= c - 1) if reverse else (in_chunk >= lag)
        x = q * ks * jnp.exp(jnp.where(valid, bcum - bs, NEG_INF))
        o = o + jnp.dot(x.astype(BF16), head_sum, preferred_element_type=F32) * vs
    head_mask = (lax.broadcasted_iota(jnp.int32, (hv, hk), 0) // (hv // nh)
                 == lax.broadcasted_iota(jnp.int32, (hv, hk), 1) // (hk // nh))
    st = st_ref[...]
    nch = tc // c
    outs = [None] * nch
    for ci in (range(nch - 1, -1, -1) if reverse else range(nch)):
        sl = slice(ci * c, (ci + 1) * c)
        outs[ci] = lax.dot_general(qd[sl].astype(BF16), st.astype(BF16), (((1,), (1,)), ((), ())),
                                   preferred_element_type=F32)
        ds = lax.dot_general(v[sl].astype(BF16), kd[sl].astype(BF16), (((0,), (0,)), ((), ())),
                             preferred_element_type=F32)
        st = st * dec[ci * c:ci * c + 1] + jnp.where(head_mask, ds, 0.0)
    st_ref[...] = st
    return o + jnp.concatenate(outs, axis=0)


def _log_sigmoid(z):
    return jnp.minimum(z, 0.0) - jnp.log1p(jnp.exp(-jnp.abs(z)))


def _hgrn2_kernel(qf_ref, if_ref, zf_ref, qb_ref, ib_ref, zb_ref, lb_ref, s0f_ref, s0b_ref,
                  of_ref, ob_ref, sf_ref, sb_ref, stf, stb):
    j = pl.program_id(1)

    @pl.when(j == 0)
    def _():
        stf[...] = s0f_ref[...]
        stb[...] = s0b_ref[...]

    one_minus_lb, log_lb, log_ub = lb_ref[0:1], lb_ref[1:2], lb_ref[2:3]

    def gate(z):
        return one_minus_lb * jax.nn.sigmoid(-z), jnp.logaddexp(log_lb, log_ub + _log_sigmoid(z))

    silu = lambda a: a * jax.nn.sigmoid(a)
    k_f, g_f = gate(zf_ref[...])
    k_b, g_b = gate(zb_ref[...])
    of_ref[...] = _glr_direction(silu(qf_ref[...]), k_f, if_ref[...], g_f, stf, False, HG_H)
    ob_ref[...] = _glr_direction(silu(qb_ref[...]), k_b, ib_ref[...], g_b, stb, True, HG_H)

    @pl.when(j == pl.num_programs(1) - 1)
    def _():
        sf_ref[...] = stf[...]
        sb_ref[...] = stb[...]


def _gla_kernel(qkf_ref, vf_ref, nf_ref, qkb_ref, vb_ref, nb_ref, aup_ref, ab_ref, s0f_ref, s0b_ref,
                of_ref, ob_ref, sf_ref, sb_ref, stf, stb):
    j = pl.program_id(1)

    @pl.when(j == 0)
    def _():
        stf[...] = s0f_ref[...]
        stb[...] = s0b_ref[...]

    def gate(narrow, idx):
        a = narrow[:, idx * GLA_RANK:(idx + 1) * GLA_RANK]
        lin = jnp.dot(a.astype(BF16), aup_ref[idx].astype(BF16), preferred_element_type=F32) + ab_ref[idx]
        return _log_sigmoid(lin) / GLA_NORMALIZER

    qk_f = qkf_ref[...]
    qk_b = qkb_ref[...]
    of_ref[...] = _glr_direction(qk_f[:, :GLA_KW] * GLA_DK ** -0.5, qk_f[:, GLA_KW:], vf_ref[...],
                                 gate(nf_ref[...], 0), stf, False, GLA_H)
    ob_ref[...] = _glr_direction(qk_b[:, :GLA_KW] * GLA_DK ** -0.5, qk_b[:, GLA_KW:], vb_ref[...],
                                 gate(nb_ref[...], 1), stb, True, GLA_H)

    @pl.when(j == pl.num_programs(1) - 1)
    def _():
        sf_ref[...] = stf[...]
        sb_ref[...] = stb[...]


def _glr_call(kernel_fn, p3, fwd_cols, bwd_cols, consts, s0_f, s0_b, hk, hv, name):
    bsz, t, _ = p3.shape
    tc = GLR_TILE
    nsb = t // tc
    fwd = lambda w, c: pl.BlockSpec((None, tc, w), lambda b, j: (b, j, c))
    bwd = lambda w, c: pl.BlockSpec((None, tc, w), lambda b, j: (b, nsb - 1 - j, c))
    whole = lambda a: pl.BlockSpec(a.shape, lambda b, j: (0,) * a.ndim)
    st = pl.BlockSpec((None, hv, hk), lambda b, j: (b, 0, 0))
    out_f = pl.BlockSpec((None, tc, hv), lambda b, j: (b, j, 0))
    out_b = pl.BlockSpec((None, tc, hv), lambda b, j: (b, nsb - 1 - j, 0))
    return pl.pallas_call(
        kernel_fn,
        out_shape=[jax.ShapeDtypeStruct((bsz, t, hv), F32), jax.ShapeDtypeStruct((bsz, t, hv), F32),
                   jax.ShapeDtypeStruct((bsz, hv, hk), F32), jax.ShapeDtypeStruct((bsz, hv, hk), F32)],
        grid=(bsz, nsb),
        in_specs=([fwd(w, c) for w, c in fwd_cols] + [bwd(w, c) for w, c in bwd_cols]
                  + [whole(a) for a in consts] + [st, st]),
        out_specs=[out_f, out_b, st, st],
        scratch_shapes=[pltpu.VMEM((hv, hk), F32), pltpu.VMEM((hv, hk), F32)],
        compiler_params=pltpu.CompilerParams(dimension_semantics=("arbitrary", "arbitrary"),
                                             vmem_limit_bytes=VMEM_LIMIT),
        name=name,
    )(*([p3] * (len(fwd_cols) + len(bwd_cols))), *consts, s0_f, s0_b)


def _hgrn2_seq(p3, lb, s0_f, s0_b):
    lb_rows = jnp.stack([1.0 - lb, jnp.log(lb), jnp.log1p(-lb)])
    cols = lambda z: [(GROUP_W, COL_HG_Q), (GROUP_W, COL_HG_I), (GROUP_W, z)]
    o_f, o_b, s_f, s_b = _glr_call(_hgrn2_kernel, p3, cols(COL_HG_ZF), cols(COL_HG_ZB), [lb_rows],
                                   s0_f, s0_b, HG_W, HG_W, "hgrn2_recurrence")
    return (o_f, o_b), s_f, s_b


def _hgrn2(pc3, pl3, lb):
    s0 = jnp.zeros((pl3.shape[0], HG_W, HG_W), F32)
    oc, s_f, s_b = _hgrn2_seq(pc3, lb, s0, s0)
    o, _, _ = _hgrn2_seq(pl3, lb, s_f, s_b)
    return oc, o


def _gla_seq(p3, a_up, a_b, s0_f, s0_b):
    cols = [(GROUP_W, COL_GLA_QK), (GROUP_W, COL_GLA_V), (LANE, COL_NARROW)]
    o_f, o_b, s_f, s_b = _glr_call(_gla_kernel, p3, cols, cols, [a_up, a_b.reshape(2, 1, GLA_KW)],
                                   s0_f, s0_b, GLA_KW, GLA_VW, "gla_recurrence")
    return (o_f, o_b), s_f, s_b


def _gla(pc3, pl3, a_up, a_b):
    s0 = jnp.zeros((pl3.shape[0], GLA_VW, GLA_KW), F32)
    oc, s_f, s_b = _gla_seq(pc3, a_up, a_b, s0, s0)
    o, _, _ = _gla_seq(pl3, a_up, a_b, s_f, s_b)
    return oc, o


def _mlstm_direction(q, k, v, igx, lfx, s_ref, n_ref, m_ref, reverse, nh):
    tc, w = q.shape
    seg = w // nh
    assert tc == seg
    hi = lax.Precision.HIGHEST
    ti = lax.broadcasted_iota(jnp.int32, (tc, tc), 0)
    tj = lax.broadcasted_iota(jnp.int32, (tc, tc), 1)
    seen = (tj >= ti) if reverse else (tj <= ti)
    b = jnp.dot(jnp.where(seen, 1.0, 0.0), lfx, precision=hi, preferred_element_type=F32)
    bl = b[0:1] if reverse else b[tc - 1:tc]
    a = bl - b + igx
    ma = jnp.max(a, axis=0, keepdims=True)
    kw = jnp.exp(a - ma) * k
    s_prev = s_ref[...]
    n_prev = n_ref[...]
    m_prev = m_ref[...]
    lane = lax.broadcasted_iota(jnp.int32, (tc, w), 1)
    row = lax.broadcasted_iota(jnp.int32, (tc, w), 0)
    same_head = (lax.broadcasted_iota(jnp.int32, (w, w), 0) // seg
                 == lax.broadcasted_iota(jnp.int32, (w, w), 1) // seg)
    kexp = jnp.where(same_head, jnp.concatenate([k] * nh, axis=0), 0.0)
    vexp = jnp.where(same_head, jnp.concatenate([v] * nh, axis=0), 0.0)
    scores = lax.dot_general(q.astype(BF16), kexp.astype(BF16), (((1,), (1,)), ((), ())),
                             preferred_element_type=F32)
    s_lane = lane % seg
    by_src = jnp.sum(jnp.where(s_lane == row, igx - b, 0.0), axis=0, keepdims=True)
    ok = (s_lane >= row) if reverse else (s_lane <= row)
    dmat = jnp.where(ok, b + by_src, NEG_INF)
    inter = b + m_prev
    head_of_lane = lane // seg
    seg_max = jnp.full((tc, w), NEG_INF, F32)
    for h in range(nh):
        in_h = head_of_lane == h
        seg_max = jnp.where(in_h, jnp.max(jnp.where(in_h, dmat, NEG_INF), axis=1, keepdims=True), seg_max)
    m_t = jnp.maximum(inter, seg_max)
    wq = jnp.exp(dmat - m_t) * scores
    w_int = jnp.exp(inter - m_t)
    head_sum = jnp.where(same_head, 1.0, 0.0).astype(BF16)
    num = (jnp.dot(wq.astype(BF16), vexp.astype(BF16), preferred_element_type=F32)
           + w_int * jnp.dot(q.astype(BF16), s_prev.astype(BF16), preferred_element_type=F32))
    den = (jnp.dot(wq.astype(BF16), head_sum, preferred_element_type=F32)
           + w_int * jnp.dot((q * n_prev).astype(BF16), head_sum, preferred_element_type=F32))
    h_out = num / jnp.maximum(jnp.abs(den), jnp.exp(-m_t))
    m_new = jnp.maximum(bl + m_prev, ma)
    d_old = jnp.exp(bl + m_prev - m_new)
    d_new = jnp.exp(ma - m_new)
    ds = lax.dot_general(kw.astype(BF16), v.astype(BF16), (((0,), (0,)), ((), ())), preferred_element_type=F32)
    s_ref[...] = d_old * s_prev + d_new * jnp.where(same_head, ds, 0.0)
    n_ref[...] = d_old * n_prev + d_new * jnp.sum(kw, axis=0, keepdims=True)
    m_ref[...] = m_new
    return h_out


def _mlstm_kernel(qf_ref, kf_ref, vf_ref, igf_ref, lff_ref, qb_ref, kb_ref, vb_ref, igb_ref, lfb_ref,
                  s0f_ref, n0f_ref, m0f_ref, s0b_ref, n0b_ref, m0b_ref,
                  hf_ref, hb_ref, sf_ref, nf_ref, mf_ref, sb_ref, nb_ref, mb_ref,
                  s_f, n_f, m_f, s_b, n_b, m_b, *, nh):
    j = pl.program_id(1)

    @pl.when(j == 0)
    def _():
        s_f[...] = s0f_ref[...]
        n_f[...] = n0f_ref[...]
        m_f[...] = m0f_ref[...]
        s_b[...] = s0b_ref[...]
        n_b[...] = n0b_ref[...]
        m_b[...] = m0b_ref[...]

    hf_ref[...] = _mlstm_direction(qf_ref[...], kf_ref[...], vf_ref[...], igf_ref[...], lff_ref[...],
                                   s_f, n_f, m_f, False, nh)
    hb_ref[...] = _mlstm_direction(qb_ref[...], kb_ref[...], vb_ref[...], igb_ref[...], lfb_ref[...],
                                   s_b, n_b, m_b, True, nh)

    @pl.when(j == pl.num_programs(1) - 1)
    def _():
        sf_ref[...] = s_f[...]
        nf_ref[...] = n_f[...]
        mf_ref[...] = m_f[...]
        sb_ref[...] = s_b[...]
        nb_ref[...] = n_b[...]
        mb_ref[...] = m_b[...]


def _mlstm_bidir(q, k, v, ig_f, lf_f, ig_b, lf_b, st_f, st_b, nh):
    bsz, t, w = q.shape
    tc = CHUNK_ML
    nsb = t // tc
    fwd = pl.BlockSpec((None, tc, w), lambda b, j: (b, j, 0))
    bwd = pl.BlockSpec((None, tc, w), lambda b, j: (b, nsb - 1 - j, 0))
    mat = pl.BlockSpec((None, w, w), lambda b, j: (b, 0, 0))
    vec = pl.BlockSpec((None, 1, w), lambda b, j: (b, 0, 0))
    sds = jax.ShapeDtypeStruct
    state_shapes = [sds((bsz, w, w), F32), sds((bsz, 1, w), F32), sds((bsz, 1, w), F32)]
    outs = pl.pallas_call(
        functools.partial(_mlstm_kernel, nh=nh),
        out_shape=[sds((bsz, t, w), F32), sds((bsz, t, w), F32)] + state_shapes + state_shapes,
        grid=(bsz, nsb),
        in_specs=[fwd] * 5 + [bwd] * 5 + [mat, vec, vec] * 2,
        out_specs=[fwd, bwd] + [mat, vec, vec] * 2,
        scratch_shapes=[pltpu.VMEM((w, w), F32), pltpu.VMEM((1, w), F32), pltpu.VMEM((1, w), F32)] * 2,
        compiler_params=pltpu.CompilerParams(dimension_semantics=("arbitrary", "arbitrary"),
                                             vmem_limit_bytes=VMEM_LIMIT),
        name="mlstm_recurrence",
    )(q, k, v, ig_f, lf_f, q, k, v, ig_b, lf_b, *st_f, *st_b)
    return outs[0], outs[1], tuple(outs[2:5]), tuple(outs[5:8])


def _mlstm_seq(p3, conv_w, conv_b, gate_b, st_f, st_b):
    qk = _short_conv(p3, COL_ML_Q, 2, conv_w, conv_b, True)
    v = p3[..., COL_ML_V * GROUP_W:(COL_ML_V + 1) * GROUP_W]
    bsz, t, _ = p3.shape
    gates = p3[..., NARROW_ML_GATES:NARROW_ML_GATES + 4 * ML_H]
    gt = gates.reshape(bsz, t, 4, ML_H) + gate_b
    expand = lambda a: jnp.repeat(a, ML_DH, axis=-1)
    h_f, h_b, fin_f, fin_b = _mlstm_bidir(
        qk[..., :ML_W], qk[..., ML_W:] * ML_DH ** -0.5, v,
        expand(gt[:, :, 0]), expand(jax.nn.log_sigmoid(gt[:, :, 1])),
        expand(gt[:, :, 2]), expand(jax.nn.log_sigmoid(gt[:, :, 3])), st_f, st_b, ML_H)
    return (h_f, h_b), fin_f, fin_b


def _mlstm(pc3, pl3, conv_w, conv_b, gate_b):
    bsz = pl3.shape[0]
    st0 = (jnp.zeros((bsz, ML_W, ML_W), F32), jnp.zeros((bsz, 1, ML_W), F32), jnp.zeros((bsz, 1, ML_W), F32))
    hc, st_f, st_b = _mlstm_seq(pc3, conv_w, conv_b, gate_b, st0, st0)
    h, _, _ = _mlstm_seq(pl3, conv_w, conv_b, gate_b, st_f, st_b)
    return hc, h


def _router_kernel(h_ref, wt_ref, b_ref, eidx_ref, wsel_ref, cnt_ref):
    i = pl.program_id(0)
    tm = h_ref.shape[0]
    ne = wt_ref.shape[0]
    per_group = ne // N_EXPERT_GROUPS
    logits = lax.dot_general(wt_ref[...], h_ref[...], (((1,), (1,)), ((), ())),
                             preferred_element_type=F32, precision=lax.Precision.HIGHEST)
    s = jax.nn.sigmoid(logits)
    sel = s + b_ref[...]
    row = lax.broadcasted_iota(jnp.int32, (ne, tm), 0)
    gs = []
    for g in range(N_EXPERT_GROUPS):
        blk = sel[g * per_group:(g + 1) * per_group]
        r = lax.broadcasted_iota(jnp.int32, blk.shape, 0)
        m1 = jnp.max(blk, axis=0, keepdims=True)
        i1 = jnp.min(jnp.where(blk == m1, r, per_group), axis=0, keepdims=True)
        m2 = jnp.max(jnp.where(r == i1, NEG_INF, blk), axis=0, keepdims=True)
        gs.append(m1 + m2)
    grp = jnp.concatenate(gs, axis=0)
    grow = lax.broadcasted_iota(jnp.int32, grp.shape, 0)
    gsel = jnp.zeros(grp.shape, F32)
    for _ in range(TOPK_GROUPS):
        m = jnp.max(grp, axis=0, keepdims=True)
        gi = jnp.min(jnp.where(grp == m, grow, N_EXPERT_GROUPS), axis=0, keepdims=True)
        hit = grow == gi
        gsel = jnp.where(hit, 1.0, gsel)
        grp = jnp.where(hit, NEG_INF, grp)
    masked = jnp.concatenate(
        [jnp.where(gsel[g:g + 1] > 0.0, sel[g * per_group:(g + 1) * per_group], NEG_INF)
         for g in range(N_EXPERT_GROUPS)], axis=0)
    eis, ws = [], []
    picked = jnp.zeros((ne, tm), F32)
    for _ in range(TOP_K):
        m = jnp.max(masked, axis=0, keepdims=True)
        ei = jnp.min(jnp.where(masked == m, row, ne), axis=0, keepdims=True)
        hit = row == ei
        ws.append(jnp.sum(jnp.where(hit, s, 0.0), axis=0, keepdims=True))
        eis.append(ei)
        picked = jnp.where(hit, 1.0, picked)
        masked = jnp.where(hit, NEG_INF, masked)
    w = jnp.concatenate(ws, axis=0)
    eidx_ref[...] = jnp.concatenate(eis, axis=0)
    wsel_ref[...] = w / jnp.sum(w, axis=0, keepdims=True) * ROUTED_SCALE
    tot = jnp.dot(picked.astype(BF16), jnp.ones((tm, LANE), BF16), preferred_element_type=F32)

    @pl.when(i == 0)
    def _():
        cnt_ref[...] = jnp.zeros_like(cnt_ref)

    cnt_ref[...] += tot


def _pos_kernel(eidx_ref, base_ref, pos_ref, carry_ref):
    i = pl.program_id(0)
    tm = eidx_ref.shape[1]
    ne = base_ref.shape[0]

    @pl.when(i == 0)
    def _():
        carry_ref[...] = jnp.zeros_like(carry_ref)

    eidx = eidx_ref[...]
    row = lax.broadcasted_iota(jnp.int32, (ne, tm), 0)
    picked = jnp.zeros((ne, tm), F32)
    for k in range(TOP_K):
        picked = jnp.where(row == eidx[k:k + 1], 1.0, picked)
    pb = picked.astype(BF16)
    before = jnp.where(lax.broadcasted_iota(jnp.int32, (tm, tm), 0) < lax.broadcasted_iota(jnp.int32, (tm, tm), 1),
                       1.0, 0.0).astype(BF16)
    rank = jnp.dot(pb, before, preferred_element_type=F32)
    tot = jnp.dot(pb, jnp.ones((tm, LANE), BF16), preferred_element_type=F32)
    dest = rank + (base_ref[...] + carry_ref[:, 0:1])
    pos = [jnp.sum(jnp.where(row == eidx[k:k + 1], dest, 0.0), axis=0, keepdims=True) for k in range(TOP_K)]
    pos_ref[...] = jnp.concatenate(pos, axis=0).astype(jnp.int32)
    carry_ref[...] += tot


def _pack_bf16_pairs(x):
    half = x.shape[1] // 2
    bits = lambda a: pltpu.bitcast(a.astype(BF16).astype(F32), jnp.int32)
    return (bits(x[:, half:]) & -65536) | lax.shift_right_logical(bits(x[:, :half]), 16)


def _unpack_bf16_pairs(w):
    return pltpu.bitcast(lax.shift_left(w, 16), F32), pltpu.bitcast(w & -65536, F32)


def _dispatch_kernel(zstart_ref, zlen_ref, nused_ref, pos_ref, h_ref, xs_ref, packed, zeros, sem, zsem, *,
                     n_blocks):
    tm = h_ref.shape[0]
    bm = zeros.shape[0]
    packed[...] = _pack_bf16_pairs(h_ref[...])

    @pl.when(pl.program_id(0) == 0)
    def _():
        zeros[...] = jnp.zeros_like(zeros)

        def zero_copy(start, size):
            return pltpu.make_async_copy(zeros.at[pl.ds(0, size)], xs_ref.at[pl.ds(start, size)], zsem)

        def pieces(e, act):
            start = zstart_ref[e]
            rem = zlen_ref[e]
            ragged = rem & (SUBLANE - 1)
            for q in range(SUBLANE - 1):
                @pl.when(q < ragged)
                def _(q=q):
                    act(zero_copy(start + q, 1))

            start = pl.multiple_of(start + ragged, SUBLANE)
            size = bm // 2
            while size >= SUBLANE:
                @pl.when((rem & size) != 0)
                def _(start=start, size=size):
                    act(zero_copy(start, size))

                start = pl.multiple_of(start + (rem & size), SUBLANE)
                size //= 2

        def loop(act):
            def per_expert(e, carry):
                pieces(e, act)
                return carry

            def per_block(b, carry):
                act(zero_copy(pl.multiple_of(b * bm, bm), bm))
                return carry

            lax.fori_loop(0, zstart_ref.shape[0], per_expert, 0)
            lax.fori_loop(nused_ref[0], n_blocks, per_block, 0)

        loop(lambda cp: cp.start())
        loop(lambda cp: cp.wait())

    def row_copy(n, k):
        return pltpu.make_async_copy(packed.at[pl.ds(n, 1)], xs_ref.at[pl.ds(pos_ref[k, n], 1)], sem)

    def issue(n, carry):
        for k in range(TOP_K):
            row_copy(n, k).start(priority=k % 2)
        return carry

    def drain(n, carry):
        for k in range(TOP_K):
            row_copy(n, k).wait()
        return carry

    lax.fori_loop(0, tm, issue, 0)
    lax.fori_loop(0, tm, drain, 0)


def _moe_ffn_kernel(blk_e_ref, run_ref, next_e_ref, nused_ref, x_ref, wgu_hbm, wdn_hbm, o_ref,
                    wgu_f, wdn_f, wgu_s, wdn_s, sems, *, layer):
    i = pl.program_id(0)
    e = blk_e_ref[i]
    run = run_ref[i]
    first_of_run = (i == 0) | (run != run_ref[jnp.maximum(i - 1, 0)])

    def fetch(expert, slot):
        return (pltpu.make_async_copy(wgu_hbm.at[layer, expert], wgu_f.at[slot], sems.at[0, slot]),
                pltpu.make_async_copy(wdn_hbm.at[layer, expert], wdn_f.at[slot], sems.at[1, slot]))

    @pl.when(i < nused_ref[0])
    def _():
        @pl.when(first_of_run)
        def _():
            slot = run & 1

            @pl.when(i == 0)
            def _():
                for cp in fetch(e, 0):
                    cp.start()

            for cp in fetch(e, slot):
                cp.wait()
            wgu_s[...] = wgu_f[slot].astype(BF16)
            wdn_s[...] = wdn_f[slot].astype(BF16)

            @pl.when(next_e_ref[i] >= 0)
            def _():
                for cp in fetch(next_e_ref[i], 1 - slot):
                    cp.start()

        x_lo, x_hi = _unpack_bf16_pairs(x_ref[...])
        half = x_lo.shape[1]
        au = (jnp.dot(x_lo.astype(BF16), wgu_s[:half, :], preferred_element_type=F32)
              + jnp.dot(x_hi.astype(BF16), wgu_s[half:, :], preferred_element_type=F32))
        a = au[:, :EXPERT_FF]
        u = au[:, EXPERT_FF:]
        h = (a * jax.nn.sigmoid(a)) * u
        o_ref[...] = _pack_bf16_pairs(jnp.dot(h.astype(BF16), wdn_s[...], preferred_element_type=F32))

    @pl.when(i >= nused_ref[0])
    def _():
        o_ref[...] = jnp.zeros_like(o_ref)


def _combine_kernel(pos_ref, w_ref, t_ref, x_ref, sgu_ref, sdn_ref, gain_ref, g2_ref, y_hbm, o_ref, buf, sem):
    tm = o_ref.shape[0]

    def row_copy(n, k):
        return pltpu.make_async_copy(y_hbm.at[pl.ds(pos_ref[k, n], 1)], buf.at[k, pl.ds(n, 1)], sem)

    def issue(n, carry):
        for k in range(TOP_K):
            row_copy(n, k).start(priority=k % 2)
        return carry

    def drain(n, carry):
        for k in range(TOP_K):
            row_copy(n, k).wait()
        return carry

    lax.fori_loop(0, tm, issue, 0)
    ff = sdn_ref.shape[0]
    au = jnp.dot(t_ref[...].astype(BF16), sgu_ref[...], preferred_element_type=F32)
    a, u = au[:, :ff], au[:, ff:]
    f = jnp.dot(((a * jax.nn.sigmoid(a)) * u).astype(BF16), sdn_ref[...], preferred_element_type=F32)
    lax.fori_loop(0, tm, drain, 0)
    r_lo, r_hi = None, None
    for k in range(TOP_K):
        y_lo, y_hi = _unpack_bf16_pairs(buf[k])
        wk = w_ref[:, k:k + 1]
        r_lo = y_lo * wk if r_lo is None else r_lo + y_lo * wk
        r_hi = y_hi * wk if r_hi is None else r_hi + y_hi * wk
    f = f + jnp.concatenate([r_lo, r_hi], axis=1)
    o_ref[...] = x_ref[...] + g2_ref[...] * (f * lax.rsqrt(jnp.mean(f * f, axis=-1, keepdims=True) + EPS)
                                             * gain_ref[...])


def _moe(t, x_res, router_w, router_b, w_gu, w_down, sh_gu, sh_down, layer, gain, g2_rows, rows_per_gate):
    n, d = t.shape
    ne = router_w.shape[1]
    ff2 = w_gu.shape[-1]
    params = pltpu.CompilerParams(dimension_semantics=("arbitrary",), vmem_limit_bytes=VMEM_LIMIT)
    tm = ROUTER_TILE
    eidx, wsel, cnt = pl.pallas_call(
        _router_kernel,
        out_shape=[jax.ShapeDtypeStruct((TOP_K, n), jnp.int32), jax.ShapeDtypeStruct((TOP_K, n), F32),
                   jax.ShapeDtypeStruct((ne, LANE), F32)],
        grid=(n // tm,),
        in_specs=[pl.BlockSpec((tm, d), lambda i: (i, 0)), pl.BlockSpec((ne, d), lambda i: (0, 0)),
                  pl.BlockSpec((ne, 1), lambda i: (0, 0))],
        out_specs=[pl.BlockSpec((TOP_K, tm), lambda i: (0, i)), pl.BlockSpec((TOP_K, tm), lambda i: (0, i)),
                   pl.BlockSpec((ne, LANE), lambda i: (0, 0))],
        compiler_params=params, name="moe_router",
    )(t, router_w.T, router_b.reshape(ne, 1))
    bm = MOE_ROWS
    counts = cnt[:, 0].astype(jnp.int32)
    padded = (counts + bm - 1) // bm * bm
    pad_end = jnp.cumsum(padded)
    pad_start = pad_end - padded
    n_blocks = (n * TOP_K + ne * (bm - 1)) // bm + 1
    blk_first = jnp.arange(n_blocks, dtype=jnp.int32) * bm
    blk_e = jnp.minimum(jnp.sum((pad_end[None, :] <= blk_first[:, None]).astype(jnp.int32), axis=1), ne - 1)
    n_used = (pad_end[-1] // bm).astype(jnp.int32).reshape(1)
    pos = pl.pallas_call(
        _pos_kernel,
        out_shape=jax.ShapeDtypeStruct((TOP_K, n), jnp.int32),
        grid=(n // tm,),
        in_specs=[pl.BlockSpec((TOP_K, tm), lambda i: (0, i)), pl.BlockSpec((ne, 1), lambda i: (0, 0))],
        out_specs=pl.BlockSpec((TOP_K, tm), lambda i: (0, i)),
        scratch_shapes=[pltpu.VMEM((ne, LANE), F32)],
        compiler_params=params, name="moe_positions",
    )(eidx, pad_start.astype(F32).reshape(ne, 1))
    ts = SCATTER_TILE
    p = n_blocks * bm
    pos_spec = pl.BlockSpec((TOP_K, ts), lambda i: (0, i), memory_space=pltpu.SMEM)
    dp = d // 2
    xs = pl.pallas_call(
        functools.partial(_dispatch_kernel, n_blocks=n_blocks),
        out_shape=jax.ShapeDtypeStruct((p, dp), jnp.int32),
        grid_spec=pltpu.PrefetchScalarGridSpec(
            num_scalar_prefetch=3,
            grid=(n // ts,),
            in_specs=[pl.BlockSpec((TOP_K, ts), lambda i, *_: (0, i), memory_space=pltpu.SMEM),
                      pl.BlockSpec((ts, d), lambda i, *_: (i, 0))],
            out_specs=pl.BlockSpec(memory_space=pl.ANY),
            scratch_shapes=[pltpu.VMEM((ts, dp), jnp.int32), pltpu.VMEM((bm, dp), jnp.int32),
                            pltpu.SemaphoreType.DMA, pltpu.SemaphoreType.DMA],
        ),
        compiler_params=params, name="moe_dispatch",
    )(pad_start + counts, padded - counts, n_used, pos, t)

    is_start = jnp.concatenate([jnp.ones((1,), bool), blk_e[1:] != blk_e[:-1]])
    run_id = jnp.cumsum(is_start.astype(jnp.int32)) - 1
    ids = jnp.arange(ne, dtype=jnp.int32)
    later_busy = jnp.where((counts[None, :] > 0) & (ids[None, :] > ids[:, None]), ids[None, :], ne)
    next_busy = jnp.min(later_busy, axis=1)
    next_e = jnp.sum(jnp.where(blk_e[:, None] == ids[None, :], next_busy[None, :], 0), axis=1)
    next_e = jnp.where(next_e >= ne, -1, next_e).astype(jnp.int32)

    def x_map(i, blk_e, run, nxt, nused):
        return (jnp.minimum(i, nused[0] - 1), 0)

    y_p = pl.pallas_call(
        functools.partial(_moe_ffn_kernel, layer=layer),
        out_shape=jax.ShapeDtypeStruct((p, dp), jnp.int32),
        grid_spec=pltpu.PrefetchScalarGridSpec(
            num_scalar_prefetch=4,
            grid=(n_blocks,),
            in_specs=[pl.BlockSpec((bm, dp), x_map),
                      pl.BlockSpec(memory_space=pl.ANY), pl.BlockSpec(memory_space=pl.ANY)],
            out_specs=pl.BlockSpec((bm, dp), lambda i, *_: (i, 0)),
            scratch_shapes=[pltpu.VMEM((2, d, ff2), F32), pltpu.VMEM((2, ff2 // 2, d), F32),
                            pltpu.VMEM((d, ff2), BF16), pltpu.VMEM((ff2 // 2, d), BF16),
                            pltpu.SemaphoreType.DMA((2, 2))],
        ),
        compiler_params=params, name="moe_expert_ffn",
    )(blk_e, run_id, next_e, n_used, xs, w_gu, w_down)
    rows = pl.BlockSpec((ts, d), lambda i: (i, 0))
    whole = lambda a: pl.BlockSpec(a.shape, lambda i: (0,) * a.ndim)
    last_gate = g2_rows.shape[0] - 1
    gate_spec = pl.BlockSpec((None, 1, d), lambda i: (jnp.minimum(i // (rows_per_gate // ts), last_gate), 0, 0))
    consts = [sh_gu.astype(BF16), sh_down.astype(BF16), gain.reshape(1, d)]
    return pl.pallas_call(
        _combine_kernel,
        out_shape=jax.ShapeDtypeStruct((n, d), F32),
        grid=(n // ts,),
        in_specs=([pos_spec, pl.BlockSpec((ts, TOP_K), lambda i: (i, 0)), rows, rows]
                  + [whole(a) for a in consts] + [gate_spec, pl.BlockSpec(memory_space=pl.ANY)]),
        out_specs=rows,
        scratch_shapes=[pltpu.VMEM((TOP_K, ts, dp), jnp.int32), pltpu.SemaphoreType.DMA],
        compiler_params=params, name="moe_combine",
    )(pos, wsel.T, t, x_res, *consts, g2_rows, y_p)


def kernel(x, c, ctx, c_ctx, ada_w, ada_b, norm_g, w_in, w_out, hy_conv_w, hy_conv_b, hy_ffn_w1, hy_ffn_b1, hy_ffn_w2, hy_ffn_b2, hy_ffn_w3, hy_freq, hy_bias, hy_norm, hg_lb_logits, hg_norm, gla_a_up, gla_a_b, gla_norm, ml_conv_w, ml_conv_b, ml_gate_b, ml_norm, router_w, router_b, exp_w_gu, exp_w_down, sh_w_gu, sh_w_down):
    bsz, seq, d = x.shape
    n_ctx = ctx.shape[1]
    depth = ada_w.shape[0]
    rows = seq // GRID_W
    x = x + _pos_embed_2d(rows, d)[None]
    xc = ctx
    lb_cum = jnp.cumsum(jax.nn.softmax(hg_lb_logits, axis=0), axis=0)
    lower_bounds = lb_cum - lb_cum[0:1]
    for l in range(depth):
        with_ctx = l < depth - 1
        mod = (jax.nn.silu(c) @ ada_w[l] + ada_b[l])[:, None, :]
        mod_c = jax.nn.silu(c_ctx) @ ada_w[l] + ada_b[l]
        sh1, sc1, g1, sh2, sc2, g2 = jnp.split(mod, 6, axis=-1)
        csh1, csc1, cg1, csh2, csc2, cg2 = jnp.split(mod_c, 6, axis=-1)
        w_in_l = _arrange_w_in(w_in[l])
        ctx_rows = lambda a: jnp.broadcast_to(a.reshape(1, 1, d), (bsz, 1, d))
        pl3 = _in_proj(x, norm_g[l, 0], sc1, sh1, w_in_l)
        pc3 = _in_proj(xc, norm_g[l, 0], ctx_rows(csc1), ctx_rows(csh1), w_in_l)
        hy_args = (hy_conv_w[l], hy_conv_b[l], hy_ffn_w1[l], hy_ffn_b1[l], hy_ffn_w2[l], hy_ffn_b2[l],
                   hy_ffn_w3[l], hy_freq[l], hy_bias[l], hy_norm[l])
        y_hy = _hyena_long(pl3, *hy_args)
        oc_hg, o_hg = _hgrn2(pc3, pl3, lower_bounds[l])
        oc_gla, o_gla = _gla(pc3, pl3, gla_a_up[l], gla_a_b[l])
        hc_ml, h_ml = _mlstm(pc3, pl3, ml_conv_w[l], ml_conv_b[l], ml_gate_b[l])
        head_gains = jnp.stack([hg_norm[l], gla_norm[l], ml_norm[l]])
        out_args = (w_out[l], head_gains, norm_g[l, 1], norm_g[l, 2])
        x, h = _mixer_out(y_hy, o_hg, o_gla, h_ml, pl3, x, *out_args, jnp.concatenate([g1, sc2, sh2], axis=1))
        moe_args = (router_w[l], router_b[l], exp_w_gu, exp_w_down, sh_w_gu[l], sh_w_down[l], l, norm_g[l, 3])
        if with_ctx:
            mod_ctx = jnp.broadcast_to(jnp.stack([cg1, csc2, csh2])[None], (bsz, 3, d))
            xc, hc = _mixer_out(_hyena(pc3, *hy_args), oc_hg, oc_gla, hc_ml, pc3, xc, *out_args, mod_ctx)
            tokens = lambda a, ac: jnp.concatenate([a.reshape(bsz * seq, d), ac.reshape(bsz * n_ctx, d)], axis=0)
            gates = jnp.concatenate([g2, cg2.reshape(1, 1, d)], axis=0)
            x_all = _moe(tokens(h, hc), tokens(x, xc), *moe_args, gates, seq)
            x = x_all[:bsz * seq].reshape(bsz, seq, d)
            xc = x_all[bsz * seq:].reshape(bsz, n_ctx, d)
        else:
            x = _moe(h.reshape(bsz * seq, d), x.reshape(bsz * seq, d), *moe_args, g2, seq).reshape(bsz, seq, d)
    return x
```

```python
import functools
import math

import jax
import jax.numpy as jnp
import numpy as np
from jax import lax
from jax.experimental import pallas as pl
from jax.experimental.pallas import tpu as pltpu

F32 = jnp.float32
BF16 = jnp.bfloat16

D_MODEL = 1024
GRID_W = 64
EPS = 1e-6
POS_BASE = 10000.0
GROUP_W = D_MODEL // 4
SHORT_CONV = 3
HY_W = GROUP_W
HY_ORDER = 2
HY_EMB = 33
HY_BANDS = (HY_EMB - 1) // 2
HY_FAST_DECAY = 0.3
HY_SLOW_DECAY = 1.5
HY_DECAY_TARGET = 1e-2
HG_H = 4
HG_W = GROUP_W
HG_DK = HG_W // HG_H
GLA_H = 4
GLA_KW = GROUP_W // 2
GLA_VW = GROUP_W
GLA_DK = GLA_KW // GLA_H
GLA_DV = GLA_VW // GLA_H
GLA_RANK = 16
GLA_NORMALIZER = 16.0
ML_H = 4
ML_W = GROUP_W
ML_DH = ML_W // ML_H
CHUNK_GATED = 16
CHUNK_ML = 64
TOP_K = 8
N_EXPERT_GROUPS = 8
TOPK_GROUPS = 4
EXPERT_FF = 256
ROUTED_SCALE = 2.5
IN_SPLITS = (HY_W, HY_W, HY_W,
             HG_W, HG_W, HG_W, HG_W, HG_W,
             GLA_KW, GLA_KW, GLA_VW, GLA_RANK, GLA_RANK, GLA_VW,
             ML_W, ML_W, ML_W, 4 * ML_H, ML_W)
P_ORDER = (0, 1, 2, 3, 4, 5, 6, 7, 8, 9, 10, 13, 14, 15, 16, 18, 11, 12, 17)
COL_HY = 0
COL_HG_Q, COL_HG_I, COL_HG_ZF, COL_HG_ZB, COL_HG_G = 3, 4, 5, 6, 7
COL_GLA_QK, COL_GLA_V, COL_GLA_R = 8, 9, 10
COL_ML_Q, COL_ML_K, COL_ML_V, COL_ML_O = 11, 12, 13, 14
N_WIDE = 15

LANE = 128
SUBLANE = 8
V7X_VMEM_BYTES = 64 * 1024 * 1024
ROW_TILE = 512
MOE_ROWS = 256
ROUTER_TILE = 256
SCATTER_TILE = 512
GLR_TILE = 128
VMEM_LIMIT = V7X_VMEM_BYTES * 7 // 8
NEG_INF = float("-inf")
COL_NARROW = N_WIDE * GROUP_W // LANE
NARROW_ML_GATES = N_WIDE * GROUP_W + 2 * GLA_RANK
P_WIDTH = N_WIDE * GROUP_W + LANE


def _arrange_w_in(w):
    offs = np.concatenate([[0], np.cumsum(IN_SPLITS)])
    cols = [w[:, offs[i]:offs[i + 1]] for i in P_ORDER]
    used = sum(IN_SPLITS)
    return jnp.concatenate(cols + [jnp.zeros((w.shape[0], P_WIDTH - used), w.dtype)], axis=1).astype(BF16)


def _in_proj_kernel(x_ref, g_ref, sc_ref, sh_ref, w_ref, o_ref):
    x = x_ref[...]
    y = x * lax.rsqrt(jnp.mean(x * x, axis=-1, keepdims=True) + EPS) * g_ref[...]
    h = y * (1.0 + sc_ref[...]) + sh_ref[...]
    o_ref[...] = jnp.dot(h.astype(BF16), w_ref[...], preferred_element_type=F32)


def _in_proj(x, gain, scale, shift, w):
    g, r, d = x.shape
    n = w.shape[1]
    tm = min(ROW_TILE, r)
    assert r % tm == 0
    return pl.pallas_call(
        _in_proj_kernel,
        out_shape=jax.ShapeDtypeStruct((g, r, n), F32),
        grid=(g, r // tm),
        in_specs=[pl.BlockSpec((None, tm, d), lambda b, i: (b, i, 0)),
                  pl.BlockSpec((1, d), lambda b, i: (0, 0)),
                  pl.BlockSpec((None, 1, d), lambda b, i: (b, 0, 0)),
                  pl.BlockSpec((None, 1, d), lambda b, i: (b, 0, 0)),
                  pl.BlockSpec((d, n), lambda b, i: (0, 0))],
        out_specs=pl.BlockSpec((None, tm, n), lambda b, i: (b, i, 0)),
        compiler_params=pltpu.CompilerParams(dimension_semantics=("arbitrary", "arbitrary"),
                                             vmem_limit_bytes=VMEM_LIMIT),
        name="input_projection",
    )(x, gain.reshape(1, d), scale, shift, w)


def _mixer_out_kernel(hy_ref, hgf_ref, hgb_ref, glf_ref, glb_ref, mlf_ref, mlb_ref, ghg_ref, ggl_ref, gml_ref,
                      x_ref, w_ref, hn_ref, n1_ref, n2_ref, mod_ref, xo_ref, h_ref):
    gw = hy_ref.shape[-1]
    seg = jnp.where(lax.broadcasted_iota(jnp.int32, (gw, gw), 0) // HG_DK
                    == lax.broadcasted_iota(jnp.int32, (gw, gw), 1) // HG_DK, 1.0 / HG_DK, 0.0)

    def head_norm(o, gain):
        ms = jnp.dot(o * o, seg, precision=lax.Precision.HIGHEST, preferred_element_type=F32)
        return o * lax.rsqrt(ms + EPS) * gain

    silu = lambda a: a * jax.nn.sigmoid(a)
    groups = (hy_ref[...],
              head_norm(hgf_ref[...] + hgb_ref[...], hn_ref[0:1]) * silu(ghg_ref[...]),
              head_norm(glf_ref[...] + glb_ref[...], hn_ref[1:2]) * silu(ggl_ref[...]),
              jax.nn.sigmoid(gml_ref[...]) * head_norm(mlf_ref[...] + mlb_ref[...], hn_ref[2:3]))
    y = None
    for i, part in enumerate(groups):
        term = jnp.dot(part.astype(BF16), w_ref[i * gw:(i + 1) * gw, :], preferred_element_type=F32)
        y = term if y is None else y + term
    rms = lambda a, g: a * lax.rsqrt(jnp.mean(a * a, axis=-1, keepdims=True) + EPS) * g
    x = x_ref[...] + mod_ref[0:1] * rms(y, n1_ref[...])
    xo_ref[...] = x
    h_ref[...] = rms(x, n2_ref[...]) * (1.0 + mod_ref[1:2]) + mod_ref[2:3]


def _mixer_out(y_hy, o_hg, o_gla, h_ml, p3, x, w_out, head_gains, gain1, gain2, mod):
    assert HG_DK == GLA_DV == ML_DH and HG_H == GLA_H == ML_H
    bsz, t, d = x.shape
    gw = GROUP_W
    tm = min(ROW_TILE, t)
    part = pl.BlockSpec((None, tm, gw), lambda b, i: (b, i, 0))
    gate = lambda c: pl.BlockSpec((None, tm, gw), lambda b, i: (b, i, c))
    full = pl.BlockSpec((None, tm, d), lambda b, i: (b, i, 0))
    whole = lambda a: pl.BlockSpec(a.shape, lambda b, i: (0,) * a.ndim)
    consts = [w_out.astype(BF16), head_gains, gain1.reshape(1, d), gain2.reshape(1, d)]
    return pl.pallas_call(
        _mixer_out_kernel,
        out_shape=[jax.ShapeDtypeStruct((bsz, t, d), F32), jax.ShapeDtypeStruct((bsz, t, d), F32)],
        grid=(bsz, t // tm),
        in_specs=([part] * 7 + [gate(COL_HG_G), gate(COL_GLA_R), gate(COL_ML_O), full]
                  + [whole(a) for a in consts] + [pl.BlockSpec((None, 3, d), lambda b, i: (b, 0, 0))]),
        out_specs=[full, full],
        compiler_params=pltpu.CompilerParams(dimension_semantics=("arbitrary", "arbitrary"),
                                             vmem_limit_bytes=VMEM_LIMIT),
        name="mixer_output",
    )(y_hy, *o_hg, *o_gla, *h_ml, p3, p3, p3, x, *consts, mod)


def _rms(x, g):
    return x * lax.rsqrt(jnp.mean(x * x, axis=-1, keepdims=True) + EPS) * g


def _short_conv_kernel(prev_ref, cur_ref, next_ref, w_ref, b_ref, o_ref, *, act):
    j = pl.program_id(1)
    u = cur_ref[...]
    tt = u.shape[0]
    row = lax.broadcasted_iota(jnp.int32, u.shape, 0)
    before = jnp.where(j > 0, prev_ref[SUBLANE - 1:SUBLANE, :], 0.0)
    after = jnp.where(j < pl.num_programs(1) - 1, next_ref[0:1, :], 0.0)
    up = jnp.where(row == 0, before, pltpu.roll(u, 1, 0))
    dn = jnp.where(row == tt - 1, after, pltpu.roll(u, tt - 1, 0))
    y = w_ref[0:1] * up + w_ref[1:2] * u + w_ref[2:3] * dn + b_ref[...]
    if act:
        y = y * jax.nn.sigmoid(y)
    o_ref[...] = y


def _short_conv(p3, col0, ncols, w, b, act):
    assert SHORT_CONV == 3
    bsz, t, _ = p3.shape
    tt = min(ROW_TILE, t)
    halo = tt // SUBLANE
    last = t // SUBLANE - 1
    gw = GROUP_W
    cur = pl.BlockSpec((None, tt, gw), lambda bi, j, c: (bi, j, col0 + c))
    prev = pl.BlockSpec((None, SUBLANE, gw), lambda bi, j, c: (bi, jnp.maximum(j * halo - 1, 0), col0 + c))
    nxt = pl.BlockSpec((None, SUBLANE, gw), lambda bi, j, c: (bi, jnp.minimum((j + 1) * halo, last), col0 + c))
    return pl.pallas_call(
        functools.partial(_short_conv_kernel, act=act),
        out_shape=jax.ShapeDtypeStruct((bsz, t, ncols * gw), F32),
        grid=(bsz, t // tt, ncols),
        in_specs=[prev, cur, nxt, pl.BlockSpec((SHORT_CONV, gw), lambda bi, j, c: (0, c)),
                  pl.BlockSpec((1, gw), lambda bi, j, c: (0, c))],
        out_specs=pl.BlockSpec((None, tt, gw), lambda bi, j, c: (bi, j, c)),
        compiler_params=pltpu.CompilerParams(dimension_semantics=("arbitrary",) * 3, vmem_limit_bytes=VMEM_LIMIT),
        name="short_conv",
    )(p3, p3, p3, w, b.reshape(1, ncols * gw))


def _pos_embed_2d(rows, d):
    r = jnp.repeat(jnp.arange(rows, dtype=F32), GRID_W)
    col = (jnp.arange(rows * GRID_W) % GRID_W).astype(F32)
    quarter = d // 4
    omega = 1.0 / (POS_BASE ** (jnp.arange(quarter, dtype=F32) / quarter))

    def axis_emb(p):
        ang = p[:, None] * omega[None, :]
        return jnp.concatenate([jnp.sin(ang), jnp.cos(ang)], axis=-1)

    return jnp.concatenate([axis_emb(r), axis_emb(col)], axis=-1)


def _hyena_spectra(L, w1, b1, w2, b2, w3, freq):
    t = jnp.linspace(0.0, 1.0, L, dtype=F32)[:, None]
    w = 2.0 * math.pi * jnp.arange(L, dtype=F32)[:, None] / L
    bands = jnp.linspace(1e-4, HY_BANDS - 1, HY_BANDS, dtype=F32)[None, :]
    feats = jnp.concatenate([t, jnp.cos(bands * w), -jnp.sin(bands * w)], axis=-1)
    z = jnp.sin(freq[0] * (feats @ w1 + b1))
    z = jnp.sin(freq[1] * (z @ w2 + b2))
    h = (z @ w3).reshape(L, HY_ORDER, 2, HY_W)
    max_decay = math.log(HY_DECAY_TARGET) / HY_FAST_DECAY
    min_decay = math.log(HY_DECAY_TARGET) / HY_SLOW_DECAY
    deltas = jnp.abs(jnp.linspace(min_decay, max_decay, HY_W, dtype=F32))
    h = h * jnp.exp(-t[:, :, None, None] * deltas)
    fwd = h[:, :, 0]
    bwd = h[1:, :, 1][::-1]
    l1 = jnp.sum(jnp.abs(fwd), axis=0) + jnp.sum(jnp.abs(bwd), axis=0)
    filt = jnp.concatenate([fwd, jnp.zeros((1, HY_ORDER, HY_W), F32), bwd], axis=0) / l1
    return jnp.fft.rfft(filt, axis=0)


def _fft_conv(u, spec, bias):
    L = u.shape[1]
    y = jnp.fft.irfft(jnp.fft.rfft(u, n=2 * L, axis=1) * spec, n=2 * L, axis=1)[:, :L]
    return y + u * bias


def _hyena(p3, conv_w, conv_b, w1, b1, w2, b2, w3, freq, bias, norm_g):
    u = _short_conv(p3, COL_HY, 3, conv_w, conv_b, False)
    v, x1, x2 = u[..., :HY_W], u[..., HY_W:2 * HY_W], u[..., 2 * HY_W:]
    spec = _hyena_spectra(u.shape[1], w1, b1, w2, b2, w3, freq)
    z = x1 * _fft_conv(v, spec[:, 0], bias[0])
    y = x2 * _fft_conv(z, spec[:, 1], bias[1])
    return _rms(y, norm_g)


FFT_N1 = 128
FFT_N2 = 128
FFT_KTILE = 8
FFT_NTILE = 4096


def _dft_tables(n1, n2):
    n = n1 * n2
    k = np.arange(n1)
    f1 = np.exp(-2j * np.pi * np.outer(k, k) / n1)
    f2 = np.exp(-2j * np.pi * np.outer(np.arange(n2), np.arange(n2)) / n2)
    tw = np.exp(-2j * np.pi * np.outer(np.arange(n1), np.arange(n2)) / n)
    as32 = lambda a: jnp.asarray(np.ascontiguousarray(a), F32)
    f1_fwd = as32(np.concatenate([f1.real, f1.imag], axis=0))
    f1_inv = as32(np.concatenate([f1.real, f1.imag], axis=1) / n)
    f2_inv = as32(np.block([[f2.real, f2.imag], [-f2.imag, f2.real]]))
    return f1_fwd, f1_inv, as32(f2.real), as32(f2.imag), f2_inv, as32(tw.real), as32(tw.imag)


def _stage_kernel(w_ref, x_ref, o_ref):
    o_ref[...] = jnp.dot(w_ref[...].astype(BF16), x_ref[...].astype(BF16),
                         preferred_element_type=F32).astype(o_ref.dtype)


def _stage_matmul(w, x, out_dtype):
    g, k, n = x.shape
    m = w.shape[0]
    tn = FFT_NTILE
    return pl.pallas_call(
        _stage_kernel,
        out_shape=jax.ShapeDtypeStruct((g, m, n), out_dtype),
        grid=(g, n // tn),
        in_specs=[pl.BlockSpec((m, k), lambda b, j: (0, 0)), pl.BlockSpec((None, k, tn), lambda b, j: (b, 0, j))],
        out_specs=pl.BlockSpec((None, m, tn), lambda b, j: (b, 0, j)),
        compiler_params=pltpu.CompilerParams(dimension_semantics=("arbitrary", "arbitrary"),
                                             vmem_limit_bytes=VMEM_LIMIT),
        name="dft_stage",
    )(w, x)


def _twiddled_f2(f2r, f2i, tr, ti):
    gr = f2r * tr - f2i * ti
    gi = f2r * ti + f2i * tr
    return jnp.concatenate([jnp.concatenate([gr, -gi], axis=1), jnp.concatenate([gi, gr], axis=1)], axis=0)


def _spectrum_kernel(a_ref, f2r_ref, f2i_ref, tr_ref, ti_ref, x_ref):
    n2 = f2r_ref.shape[0]
    tr = tr_ref[...]
    ti = ti_ref[...]
    for i in range(a_ref.shape[0]):
        g = _twiddled_f2(f2r_ref[...], f2i_ref[...], tr[i:i + 1], ti[i:i + 1])
        x_ref[i] = jnp.dot(g.astype(BF16), a_ref[i], preferred_element_type=F32)


def _conv_mid_kernel(a_ref, h_ref, f2r_ref, f2i_ref, f2inv_ref, tr_ref, ti_ref, z_ref):
    n2 = f2r_ref.shape[0]
    tr = tr_ref[...]
    ti = ti_ref[...]
    tr_col = tr.T
    ti_col = ti.T
    f2inv = f2inv_ref[...].astype(BF16)
    for i in range(a_ref.shape[0]):
        g = _twiddled_f2(f2r_ref[...], f2i_ref[...], tr[i:i + 1], ti[i:i + 1])
        x = jnp.dot(g.astype(BF16), a_ref[i], preferred_element_type=F32)
        xr, xi = x[:n2], x[n2:]
        hr, hi = h_ref[i, :n2], h_ref[i, n2:]
        y = jnp.concatenate([hr * xr - hi * xi, hr * xi + hi * xr], axis=0)
        w = jnp.dot(f2inv, y.astype(BF16), preferred_element_type=F32)
        wr, wi = w[:n2], w[n2:]
        cr, ci = tr_col[:, i:i + 1], ti_col[:, i:i + 1]
        z_ref[i] = jnp.concatenate([cr * wr + ci * wi, cr * wi - ci * wr], axis=0).astype(z_ref.dtype)


def _dft_mid_specs(g, c):
    n1, n2, kt = FFT_N1, FFT_N2, FFT_KTILE
    blk = pl.BlockSpec((None, kt, 2 * n2, c), lambda b, j: (b, j, 0, 0))
    const = lambda r, cc: pl.BlockSpec((r, cc), lambda b, j: (0, 0))
    twid = pl.BlockSpec((kt, n2), lambda b, j: (j, 0))
    params = pltpu.CompilerParams(dimension_semantics=("arbitrary", "arbitrary"), vmem_limit_bytes=VMEM_LIMIT)
    return blk, const, twid, params, (g, n1 // kt)


def _to_k1_major(a2d, c):
    g = a2d.shape[0]
    return a2d.reshape(g, 2, FFT_N1, FFT_N2, c).transpose(0, 2, 1, 3, 4).reshape(g, FFT_N1, 2 * FFT_N2, c)


def _filter_spectrum(filt, tables):
    g, n, c = filt.shape
    f1_fwd, _, f2r, f2i, _, twr, twi = tables
    a = _stage_matmul(f1_fwd, filt.reshape(g, FFT_N1, FFT_N2 * c), BF16)
    blk, const, twid, params, grid = _dft_mid_specs(g, c)
    return pl.pallas_call(
        _spectrum_kernel,
        out_shape=jax.ShapeDtypeStruct((g, FFT_N1, 2 * FFT_N2, c), F32),
        grid=grid,
        in_specs=[blk, const(FFT_N2, FFT_N2), const(FFT_N2, FFT_N2), twid, twid],
        out_specs=blk, compiler_params=params, name="dft_spectrum",
    )(_to_k1_major(a, c), f2r, f2i, twr, twi)


def _long_conv(u, spec, tables):
    g, l, c = u.shape
    f1_fwd, f1_inv, f2r, f2i, f2inv, twr, twi = tables
    half = l // FFT_N2
    a = _stage_matmul(f1_fwd[:, :half], u.reshape(g, half, FFT_N2 * c), BF16)
    blk, const, twid, params, grid = _dft_mid_specs(g, c)
    hspec = pl.BlockSpec((FFT_KTILE, 2 * FFT_N2, c), lambda b, j: (j, 0, 0))
    z = pl.pallas_call(
        _conv_mid_kernel,
        out_shape=jax.ShapeDtypeStruct((g, FFT_N1, 2 * FFT_N2, c), BF16),
        grid=grid,
        in_specs=[blk, hspec, const(FFT_N2, FFT_N2), const(FFT_N2, FFT_N2), const(2 * FFT_N2, 2 * FFT_N2),
                  twid, twid],
        out_specs=blk, compiler_params=params, name="dft_conv_mid",
    )(_to_k1_major(a, c), spec, f2r, f2i, f2inv, twr, twi)
    z2d = z.reshape(g, FFT_N1, 2, FFT_N2, c).transpose(0, 2, 1, 3, 4).reshape(g, 2 * FFT_N1, FFT_N2 * c)
    y = _stage_matmul(f1_inv[:half], z2d, F32)
    return y.reshape(g, l, c)


def _hyena_filters(L, w1, b1, w2, b2, w3, freq):
    t = jnp.linspace(0.0, 1.0, L, dtype=F32)[:, None]
    w = 2.0 * math.pi * jnp.arange(L, dtype=F32)[:, None] / L
    bands = jnp.linspace(1e-4, HY_BANDS - 1, HY_BANDS, dtype=F32)[None, :]
    feats = jnp.concatenate([t, jnp.cos(bands * w), -jnp.sin(bands * w)], axis=-1)
    max_decay = math.log(HY_DECAY_TARGET) / HY_FAST_DECAY
    min_decay = math.log(HY_DECAY_TARGET) / HY_SLOW_DECAY
    deltas = jnp.abs(jnp.linspace(min_decay, max_decay, HY_W, dtype=F32))
    w3d = w3.reshape(w3.shape[0], HY_ORDER, 2, HY_W)

    def side(f, tt, direction):
        z = jnp.sin(freq[0] * (f @ w1 + b1))
        z = jnp.sin(freq[1] * (z @ w2 + b2))
        h = (z @ w3d[:, :, direction].reshape(w3.shape[0], HY_ORDER * HY_W)).reshape(-1, HY_ORDER, HY_W)
        return h * jnp.exp(-tt[:, :, None] * deltas)

    fwd = side(feats, t, 0)
    bwd = side(feats[::-1], t[::-1], 1)[:L - 1]
    l1 = jnp.sum(jnp.abs(fwd), axis=0) + jnp.sum(jnp.abs(bwd), axis=0)
    return jnp.concatenate([fwd, jnp.zeros((1, HY_ORDER, HY_W), F32), bwd], axis=0) / l1


def _hyena_long(p3, conv_w, conv_b, w1, b1, w2, b2, w3, freq, bias, norm_g):
    u = _short_conv(p3, COL_HY, 3, conv_w, conv_b, False)
    v, x1, x2 = u[..., :HY_W], u[..., HY_W:2 * HY_W], u[..., 2 * HY_W:]
    L = u.shape[1]
    assert 2 * L == FFT_N1 * FFT_N2
    tables = _dft_tables(FFT_N1, FFT_N2)
    filt = _hyena_filters(L, w1, b1, w2, b2, w3, freq)
    spec = _filter_spectrum(jnp.moveaxis(filt, 1, 0), tables)
    z = x1 * (_long_conv(v, spec[0], tables) + v * bias[0])
    y = x2 * (_long_conv(z, spec[1], tables) + z * bias[1])
    return _rms(y, norm_g)


def _glr_direction(q, k, v, g, st_ref, reverse, nh):
    tc, hk = k.shape
    hv = v.shape[1]
    c = CHUNK_GATED
    hi = lax.Precision.HIGHEST
    ti = lax.broadcasted_iota(jnp.int32, (tc, tc), 0)
    tj = lax.broadcasted_iota(jnp.int32, (tc, tc), 1)
    same = (ti // c) == (tj // c)
    seen = (tj >= ti) if reverse else (tj <= ti)
    bcum = jnp.dot(jnp.where(same, jnp.where(seen, 1.0, 0.0), 0.0), g, precision=hi, preferred_element_type=F32)
    btot = jnp.dot(jnp.where(same, 1.0, 0.0), g, precision=hi, preferred_element_type=F32)
    qd = q * jnp.exp(bcum)
    kd = k * jnp.exp(btot - bcum)
    dec = jnp.exp(btot)
    head_sum = jnp.where(lax.broadcasted_iota(jnp.int32, (hk, hv), 0) // (hk // nh)
                         == lax.broadcasted_iota(jnp.int32, (hk, hv), 1) // (hv // nh), 1.0, 0.0).astype(BF16)
    in_chunk = lax.broadcasted_iota(jnp.int32, (tc, hk), 0) % c
    o = jnp.zeros((tc, hv), F32)
    for lag in range(c):
        if lag == 0:
            ks, bs, vs = k, bcum, v
        else:
            shift = tc - lag if reverse else lag
            ks, bs, vs = pltpu.roll(k, shift, 0), pltpu.roll(bcum, shift, 0), pltpu.roll(v, shift, 0)
        valid = (in_chunk + lag <= c - 1) if reverse else (in_chunk >= lag)
        x = q * ks * jnp.exp(jnp.where(valid, bcum - bs, NEG_INF))
        o = o + jnp.dot(x.astype(BF16), head_sum, preferred_element_type=F32) * vs
    head_mask = (lax.broadcasted_iota(jnp.int32, (hv, hk), 0) // (hv // nh)
                 == lax.broadcasted_iota(jnp.int32, (hv, hk), 1) // (hk // nh))
    st = st_ref[...]
    nch = tc // c
    outs = [None] * nch
    for ci in (range(nch - 1, -1, -1) if reverse else range(nch)):
        sl = slice(ci * c, (ci + 1) * c)
        outs[ci] = lax.dot_general(qd[sl].astype(BF16), st.astype(BF16), (((1,), (1,)), ((), ())),
                                   preferred_element_type=F32)
        ds = lax.dot_general(v[sl].astype(BF16), kd[sl].astype(BF16), (((0,), (0,)), ((), ())),
                             preferred_element_type=F32)
        st = st * dec[ci * c:ci * c + 1] + jnp.where(head_mask, ds, 0.0)
    st_ref[...] = st
    return o + jnp.concatenate(outs, axis=0)


def _log_sigmoid(z):
    return jnp.minimum(z, 0.0) - jnp.log1p(jnp.exp(-jnp.abs(z)))


def _hgrn2_kernel(qf_ref, if_ref, zf_ref, qb_ref, ib_ref, zb_ref, lb_ref, s0f_ref, s0b_ref,
                  of_ref, ob_ref, sf_ref, sb_ref, stf, stb):
    j = pl.program_id(1)

    @pl.when(j == 0)
    def _():
        stf[...] = s0f_ref[...]
        stb[...] = s0b_ref[...]

    one_minus_lb, log_lb, log_ub = lb_ref[0:1], lb_ref[1:2], lb_ref[2:3]

    def gate(z):
        return one_minus_lb * jax.nn.sigmoid(-z), jnp.logaddexp(log_lb, log_ub + _log_sigmoid(z))

    silu = lambda a: a * jax.nn.sigmoid(a)
    k_f, g_f = gate(zf_ref[...])
    k_b, g_b = gate(zb_ref[...])
    of_ref[...] = _glr_direction(silu(qf_ref[...]), k_f, if_ref[...], g_f, stf, False, HG_H)
    ob_ref[...] = _glr_direction(silu(qb_ref[...]), k_b, ib_ref[...], g_b, stb, True, HG_H)

    @pl.when(j == pl.num_programs(1) - 1)
    def _():
        sf_ref[...] = stf[...]
        sb_ref[...] = stb[...]


def _gla_kernel(qkf_ref, vf_ref, nf_ref, qkb_ref, vb_ref, nb_ref, aup_ref, ab_ref, s0f_ref, s0b_ref,
                of_ref, ob_ref, sf_ref, sb_ref, stf, stb):
    j = pl.program_id(1)

    @pl.when(j == 0)
    def _():
        stf[...] = s0f_ref[...]
        stb[...] = s0b_ref[...]

    def gate(narrow, idx):
        a = narrow[:, idx * GLA_RANK:(idx + 1) * GLA_RANK]
        lin = jnp.dot(a.astype(BF16), aup_ref[idx].astype(BF16), preferred_element_type=F32) + ab_ref[idx]
        return _log_sigmoid(lin) / GLA_NORMALIZER

    qk_f = qkf_ref[...]
    qk_b = qkb_ref[...]
    of_ref[...] = _glr_direction(qk_f[:, :GLA_KW] * GLA_DK ** -0.5, qk_f[:, GLA_KW:], vf_ref[...],
                                 gate(nf_ref[...], 0), stf, False, GLA_H)
    ob_ref[...] = _glr_direction(qk_b[:, :GLA_KW] * GLA_DK ** -0.5, qk_b[:, GLA_KW:], vb_ref[...],
                                 gate(nb_ref[...], 1), stb, True, GLA_H)

    @pl.when(j == pl.num_programs(1) - 1)
    def _():
        sf_ref[...] = stf[...]
        sb_ref[...] = stb[...]


def _glr_call(kernel_fn, p3, fwd_cols, bwd_cols, consts, s0_f, s0_b, hk, hv, name):
    bsz, t, _ = p3.shape
    tc = GLR_TILE
    nsb = t // tc
    fwd = lambda w, c: pl.BlockSpec((None, tc, w), lambda b, j: (b, j, c))
    bwd = lambda w, c: pl.BlockSpec((None, tc, w), lambda b, j: (b, nsb - 1 - j, c))
    whole = lambda a: pl.BlockSpec(a.shape, lambda b, j: (0,) * a.ndim)
    st = pl.BlockSpec((None, hv, hk), lambda b, j: (b, 0, 0))
    out_f = pl.BlockSpec((None, tc, hv), lambda b, j: (b, j, 0))
    out_b = pl.BlockSpec((None, tc, hv), lambda b, j: (b, nsb - 1 - j, 0))
    return pl.pallas_call(
        kernel_fn,
        out_shape=[jax.ShapeDtypeStruct((bsz, t, hv), F32), jax.ShapeDtypeStruct((bsz, t, hv), F32),
                   jax.ShapeDtypeStruct((bsz, hv, hk), F32), jax.ShapeDtypeStruct((bsz, hv, hk), F32)],
        grid=(bsz, nsb),
        in_specs=([fwd(w, c) for w, c in fwd_cols] + [bwd(w, c) for w, c in bwd_cols]
                  + [whole(a) for a in consts] + [st, st]),
        out_specs=[out_f, out_b, st, st],
        scratch_shapes=[pltpu.VMEM((hv, hk), F32), pltpu.VMEM((hv, hk), F32)],
        compiler_params=pltpu.CompilerParams(dimension_semantics=("arbitrary", "arbitrary"),
                                             vmem_limit_bytes=VMEM_LIMIT),
        name=name,
    )(*([p3] * (len(fwd_cols) + len(bwd_cols))), *consts, s0_f, s0_b)


def _hgrn2_seq(p3, lb, s0_f, s0_b):
    lb_rows = jnp.stack([1.0 - lb, jnp.log(lb), jnp.log1p(-lb)])
    cols = lambda z: [(GROUP_W, COL_HG_Q), (GROUP_W, COL_HG_I), (GROUP_W, z)]
    o_f, o_b, s_f, s_b = _glr_call(_hgrn2_kernel, p3, cols(COL_HG_ZF), cols(COL_HG_ZB), [lb_rows],
                                   s0_f, s0_b, HG_W, HG_W, "hgrn2_recurrence")
    return (o_f, o_b), s_f, s_b


def _hgrn2(pc3, pl3, lb):
    s0 = jnp.zeros((pl3.shape[0], HG_W, HG_W), F32)
    oc, s_f, s_b = _hgrn2_seq(pc3, lb, s0, s0)
    o, _, _ = _hgrn2_seq(pl3, lb, s_f, s_b)
    return oc, o


def _gla_seq(p3, a_up, a_b, s0_f, s0_b):
    cols = [(GROUP_W, COL_GLA_QK), (GROUP_W, COL_GLA_V), (LANE, COL_NARROW)]
    o_f, o_b, s_f, s_b = _glr_call(_gla_kernel, p3, cols, cols, [a_up, a_b.reshape(2, 1, GLA_KW)],
                                   s0_f, s0_b, GLA_KW, GLA_VW, "gla_recurrence")
    return (o_f, o_b), s_f, s_b


def _gla(pc3, pl3, a_up, a_b):
    s0 = jnp.zeros((pl3.shape[0], GLA_VW, GLA_KW), F32)
    oc, s_f, s_b = _gla_seq(pc3, a_up, a_b, s0, s0)
    o, _, _ = _gla_seq(pl3, a_up, a_b, s_f, s_b)
    return oc, o


def _mlstm_direction(q, k, v, igx, lfx, s_ref, n_ref, m_ref, reverse, nh):
    tc, w = q.shape
    seg = w // nh
    assert tc == seg
    hi = lax.Precision.HIGHEST
    ti = lax.broadcasted_iota(jnp.int32, (tc, tc), 0)
    tj = lax.broadcasted_iota(jnp.int32, (tc, tc), 1)
    seen = (tj >= ti) if reverse else (tj <= ti)
    b = jnp.dot(jnp.where(seen, 1.0, 0.0), lfx, precision=hi, preferred_element_type=F32)
    bl = b[0:1] if reverse else b[tc - 1:tc]
    a = bl - b + igx
    ma = jnp.max(a, axis=0, keepdims=True)
    kw = jnp.exp(a - ma) * k
    s_prev = s_ref[...]
    n_prev = n_ref[...]
    m_prev = m_ref[...]
    lane = lax.broadcasted_iota(jnp.int32, (tc, w), 1)
    row = lax.broadcasted_iota(jnp.int32, (tc, w), 0)
    same_head = (lax.broadcasted_iota(jnp.int32, (w, w), 0) // seg
                 == lax.broadcasted_iota(jnp.int32, (w, w), 1) // seg)
    kexp = jnp.where(same_head, jnp.concatenate([k] * nh, axis=0), 0.0)
    vexp = jnp.where(same_head, jnp.concatenate([v] * nh, axis=0), 0.0)
    scores = lax.dot_general(q.astype(BF16), kexp.astype(BF16), (((1,), (1,)), ((), ())),
                             preferred_element_type=F32)
    s_lane = lane % seg
    by_src = jnp.sum(jnp.where(s_lane == row, igx - b, 0.0), axis=0, keepdims=True)
    ok = (s_lane >= row) if reverse else (s_lane <= row)
    dmat = jnp.where(ok, b + by_src, NEG_INF)
    inter = b + m_prev
    head_of_lane = lane // seg
    seg_max = jnp.full((tc, w), NEG_INF, F32)
    for h in range(nh):
        in_h = head_of_lane == h
        seg_max = jnp.where(in_h, jnp.max(jnp.where(in_h, dmat, NEG_INF), axis=1, keepdims=True), seg_max)
    m_t = jnp.maximum(inter, seg_max)
    wq = jnp.exp(dmat - m_t) * scores
    w_int = jnp.exp(inter - m_t)
    head_sum = jnp.where(same_head, 1.0, 0.0).astype(BF16)
    num = (jnp.dot(wq.astype(BF16), vexp.astype(BF16), preferred_element_type=F32)
           + w_int * jnp.dot(q.astype(BF16), s_prev.astype(BF16), preferred_element_type=F32))
    den = (jnp.dot(wq.astype(BF16), head_sum, preferred_element_type=F32)
           + w_int * jnp.dot((q * n_prev).astype(BF16), head_sum, preferred_element_type=F32))
    h_out = num / jnp.maximum(jnp.abs(den), jnp.exp(-m_t))
    m_new = jnp.maximum(bl + m_prev, ma)
    d_old = jnp.exp(bl + m_prev - m_new)
    d_new = jnp.exp(ma - m_new)
    ds = lax.dot_general(kw.astype(BF16), v.astype(BF16), (((0,), (0,)), ((), ())), preferred_element_type=F32)
    s_ref[...] = d_old * s_prev + d_new * jnp.where(same_head, ds, 0.0)
    n_ref[...] = d_old * n_prev + d_new * jnp.sum(kw, axis=0, keepdims=True)
    m_ref[...] = m_new
    return h_out


def _mlstm_kernel(qf_ref, kf_ref, vf_ref, igf_ref, lff_ref, qb_ref, kb_ref, vb_ref, igb_ref, lfb_ref,
                  s0f_ref, n0f_ref, m0f_ref, s0b_ref, n0b_ref, m0b_ref,
                  hf_ref, hb_ref, sf_ref, nf_ref, mf_ref, sb_ref, nb_ref, mb_ref,
                  s_f, n_f, m_f, s_b, n_b, m_b, *, nh):
    j = pl.program_id(1)

    @pl.when(j == 0)
    def _():
        s_f[...] = s0f_ref[...]
        n_f[...] = n0f_ref[...]
        m_f[...] = m0f_ref[...]
        s_b[...] = s0b_ref[...]
        n_b[...] = n0b_ref[...]
        m_b[...] = m0b_ref[...]

    hf_ref[...] = _mlstm_direction(qf_ref[...], kf_ref[...], vf_ref[...], igf_ref[...], lff_ref[...],
                                   s_f, n_f, m_f, False, nh)
    hb_ref[...] = _mlstm_direction(qb_ref[...], kb_ref[...], vb_ref[...], igb_ref[...], lfb_ref[...],
                                   s_b, n_b, m_b, True, nh)

    @pl.when(j == pl.num_programs(1) - 1)
    def _():
        sf_ref[...] = s_f[...]
        nf_ref[...] = n_f[...]
        mf_ref[...] = m_f[...]
        sb_ref[...] = s_b[...]
        nb_ref[...] = n_b[...]
        mb_ref[...] = m_b[...]


def _mlstm_bidir(q, k, v, ig_f, lf_f, ig_b, lf_b, st_f, st_b, nh):
    bsz, t, w = q.shape
    tc = CHUNK_ML
    nsb = t // tc
    fwd = pl.BlockSpec((None, tc, w), lambda b, j: (b, j, 0))
    bwd = pl.BlockSpec((None, tc, w), lambda b, j: (b, nsb - 1 - j, 0))
    mat = pl.BlockSpec((None, w, w), lambda b, j: (b, 0, 0))
    vec = pl.BlockSpec((None, 1, w), lambda b, j: (b, 0, 0))
    sds = jax.ShapeDtypeStruct
    state_shapes = [sds((bsz, w, w), F32), sds((bsz, 1, w), F32), sds((bsz, 1, w), F32)]
    outs = pl.pallas_call(
        functools.partial(_mlstm_kernel, nh=nh),
        out_shape=[sds((bsz, t, w), F32), sds((bsz, t, w), F32)] + state_shapes + state_shapes,
        grid=(bsz, nsb),
        in_specs=[fwd] * 5 + [bwd] * 5 + [mat, vec, vec] * 2,
        out_specs=[fwd, bwd] + [mat, vec, vec] * 2,
        scratch_shapes=[pltpu.VMEM((w, w), F32), pltpu.VMEM((1, w), F32), pltpu.VMEM((1, w), F32)] * 2,
        compiler_params=pltpu.CompilerParams(dimension_semantics=("arbitrary", "arbitrary"),
                                             vmem_limit_bytes=VMEM_LIMIT),
        name="mlstm_recurrence",
    )(q, k, v, ig_f, lf_f, q, k, v, ig_b, lf_b, *st_f, *st_b)
    return outs[0], outs[1], tuple(outs[2:5]), tuple(outs[5:8])


def _mlstm_seq(p3, conv_w, conv_b, gate_b, st_f, st_b):
    qk = _short_conv(p3, COL_ML_Q, 2, conv_w, conv_b, True)
    v = p3[..., COL_ML_V * GROUP_W:(COL_ML_V + 1) * GROUP_W]
    bsz, t, _ = p3.shape
    gates = p3[..., NARROW_ML_GATES:NARROW_ML_GATES + 4 * ML_H]
    gt = gates.reshape(bsz, t, 4, ML_H) + gate_b
    expand = lambda a: jnp.repeat(a, ML_DH, axis=-1)
    h_f, h_b, fin_f, fin_b = _mlstm_bidir(
        qk[..., :ML_W], qk[..., ML_W:] * ML_DH ** -0.5, v,
        expand(gt[:, :, 0]), expand(jax.nn.log_sigmoid(gt[:, :, 1])),
        expand(gt[:, :, 2]), expand(jax.nn.log_sigmoid(gt[:, :, 3])), st_f, st_b, ML_H)
    return (h_f, h_b), fin_f, fin_b


def _mlstm(pc3, pl3, conv_w, conv_b, gate_b):
    bsz = pl3.shape[0]
    st0 = (jnp.zeros((bsz, ML_W, ML_W), F32), jnp.zeros((bsz, 1, ML_W), F32), jnp.zeros((bsz, 1, ML_W), F32))
    hc, st_f, st_b = _mlstm_seq(pc3, conv_w, conv_b, gate_b, st0, st0)
    h, _, _ = _mlstm_seq(pl3, conv_w, conv_b, gate_b, st_f, st_b)
    return hc, h


def _router_kernel(h_ref, wt_ref, b_ref, eidx_ref, wsel_ref, cnt_ref):
    i = pl.program_id(0)
    tm = h_ref.shape[0]
    ne = wt_ref.shape[0]
    per_group = ne // N_EXPERT_GROUPS
    logits = lax.dot_general(wt_ref[...], h_ref[...], (((1,), (1,)), ((), ())),
                             preferred_element_type=F32, precision=lax.Precision.HIGHEST)
    s = jax.nn.sigmoid(logits)
    sel = s + b_ref[...]
    row = lax.broadcasted_iota(jnp.int32, (ne, tm), 0)
    gs = []
    for g in range(N_EXPERT_GROUPS):
        blk = sel[g * per_group:(g + 1) * per_group]
        r = lax.broadcasted_iota(jnp.int32, blk.shape, 0)
        m1 = jnp.max(blk, axis=0, keepdims=True)
        i1 = jnp.min(jnp.where(blk == m1, r, per_group), axis=0, keepdims=True)
        m2 = jnp.max(jnp.where(r == i1, NEG_INF, blk), axis=0, keepdims=True)
        gs.append(m1 + m2)
    grp = jnp.concatenate(gs, axis=0)
    grow = lax.broadcasted_iota(jnp.int32, grp.shape, 0)
    gsel = jnp.zeros(grp.shape, F32)
    for _ in range(TOPK_GROUPS):
        m = jnp.max(grp, axis=0, keepdims=True)
        gi = jnp.min(jnp.where(grp == m, grow, N_EXPERT_GROUPS), axis=0, keepdims=True)
        hit = grow == gi
        gsel = jnp.where(hit, 1.0, gsel)
        grp = jnp.where(hit, NEG_INF, grp)
    masked = jnp.concatenate(
        [jnp.where(gsel[g:g + 1] > 0.0, sel[g * per_group:(g + 1) * per_group], NEG_INF)
         for g in range(N_EXPERT_GROUPS)], axis=0)
    eis, ws = [], []
    picked = jnp.zeros((ne, tm), F32)
    for _ in range(TOP_K):
        m = jnp.max(masked, axis=0, keepdims=True)
        ei = jnp.min(jnp.where(masked == m, row, ne), axis=0, keepdims=True)
        hit = row == ei
        ws.append(jnp.sum(jnp.where(hit, s, 0.0), axis=0, keepdims=True))
        eis.append(ei)
        picked = jnp.where(hit, 1.0, picked)
        masked = jnp.where(hit, NEG_INF, masked)
    w = jnp.concatenate(ws, axis=0)
    eidx_ref[...] = jnp.concatenate(eis, axis=0)
    wsel_ref[...] = w / jnp.sum(w, axis=0, keepdims=True) * ROUTED_SCALE
    tot = jnp.dot(picked.astype(BF16), jnp.ones((tm, LANE), BF16), preferred_element_type=F32)

    @pl.when(i == 0)
    def _():
        cnt_ref[...] = jnp.zeros_like(cnt_ref)

    cnt_ref[...] += tot


def _pos_kernel(eidx_ref, base_ref, pos_ref, carry_ref):
    i = pl.program_id(0)
    tm = eidx_ref.shape[1]
    ne = base_ref.shape[0]

    @pl.when(i == 0)
    def _():
        carry_ref[...] = jnp.zeros_like(carry_ref)

    eidx = eidx_ref[...]
    row = lax.broadcasted_iota(jnp.int32, (ne, tm), 0)
    picked = jnp.zeros((ne, tm), F32)
    for k in range(TOP_K):
        picked = jnp.where(row == eidx[k:k + 1], 1.0, picked)
    pb = picked.astype(BF16)
    before = jnp.where(lax.broadcasted_iota(jnp.int32, (tm, tm), 0) < lax.broadcasted_iota(jnp.int32, (tm, tm), 1),
                       1.0, 0.0).astype(BF16)
    rank = jnp.dot(pb, before, preferred_element_type=F32)
    tot = jnp.dot(pb, jnp.ones((tm, LANE), BF16), preferred_element_type=F32)
    dest = rank + (base_ref[...] + carry_ref[:, 0:1])
    pos = [jnp.sum(jnp.where(row == eidx[k:k + 1], dest, 0.0), axis=0, keepdims=True) for k in range(TOP_K)]
    pos_ref[...] = jnp.concatenate(pos, axis=0).astype(jnp.int32)
    carry_ref[...] += tot


def _pack_bf16_pairs(x):
    half = x.shape[1] // 2
    bits = lambda a: pltpu.bitcast(a.astype(BF16).astype(F32), jnp.int32)
    return (bits(x[:, half:]) & -65536) | lax.shift_right_logical(bits(x[:, :half]), 16)


def _unpack_bf16_pairs(w):
    return pltpu.bitcast(lax.shift_left(w, 16), F32), pltpu.bitcast(w & -65536, F32)


def _dispatch_kernel(zstart_ref, zlen_ref, nused_ref, pos_ref, h_ref, xs_ref, packed, zeros, sem, zsem, *,
                     n_blocks):
    tm = h_ref.shape[0]
    bm = zeros.shape[0]
    packed[...] = _pack_bf16_pairs(h_ref[...])

    @pl.when(pl.program_id(0) == 0)
    def _():
        zeros[...] = jnp.zeros_like(zeros)

        def zero_copy(start, size):
            return pltpu.make_async_copy(zeros.at[pl.ds(0, size)], xs_ref.at[pl.ds(start, size)], zsem)

        def pieces(e, act):
            start = zstart_ref[e]
            rem = zlen_ref[e]
            ragged = rem & (SUBLANE - 1)
            for q in range(SUBLANE - 1):
                @pl.when(q < ragged)
                def _(q=q):
                    act(zero_copy(start + q, 1))

            start = pl.multiple_of(start + ragged, SUBLANE)
            size = bm // 2
            while size >= SUBLANE:
                @pl.when((rem & size) != 0)
                def _(start=start, size=size):
                    act(zero_copy(start, size))

                start = pl.multiple_of(start + (rem & size), SUBLANE)
                size //= 2

        def loop(act):
            def per_expert(e, carry):
                pieces(e, act)
                return carry

            def per_block(b, carry):
                act(zero_copy(pl.multiple_of(b * bm, bm), bm))
                return carry

            lax.fori_loop(0, zstart_ref.shape[0], per_expert, 0)
            lax.fori_loop(nused_ref[0], n_blocks, per_block, 0)

        loop(lambda cp: cp.start())
        loop(lambda cp: cp.wait())

    def row_copy(n, k):
        return pltpu.make_async_copy(packed.at[pl.ds(n, 1)], xs_ref.at[pl.ds(pos_ref[k, n], 1)], sem)

    def issue(n, carry):
        for k in range(TOP_K):
            row_copy(n, k).start()
        return carry

    def drain(n, carry):
        for k in range(TOP_K):
            row_copy(n, k).wait()
        return carry

    lax.fori_loop(0, tm, issue, 0)
    lax.fori_loop(0, tm, drain, 0)


def _moe_ffn_kernel(blk_e_ref, run_ref, next_e_ref, nused_ref, x_ref, wgu_hbm, wdn_hbm, o_ref,
                    wgu_f, wdn_f, wgu_s, wdn_s, sems, *, layer):
    i = pl.program_id(0)
    e = blk_e_ref[i]
    run = run_ref[i]
    first_of_run = (i == 0) | (run != run_ref[jnp.maximum(i - 1, 0)])

    def fetch(expert, slot):
        return (pltpu.make_async_copy(wgu_hbm.at[layer, expert], wgu_f.at[slot], sems.at[0, slot]),
                pltpu.make_async_copy(wdn_hbm.at[layer, expert], wdn_f.at[slot], sems.at[1, slot]))

    @pl.when(i < nused_ref[0])
    def _():
        @pl.when(first_of_run)
        def _():
            slot = run & 1

            @pl.when(i == 0)
            def _():
                for cp in fetch(e, 0):
                    cp.start()

            for cp in fetch(e, slot):
                cp.wait()
            wgu_s[...] = wgu_f[slot].astype(BF16)
            wdn_s[...] = wdn_f[slot].astype(BF16)

            @pl.when(next_e_ref[i] >= 0)
            def _():
                for cp in fetch(next_e_ref[i], 1 - slot):
                    cp.start()

        x_lo, x_hi = _unpack_bf16_pairs(x_ref[...])
        half = x_lo.shape[1]
        au = (jnp.dot(x_lo.astype(BF16), wgu_s[:half, :], preferred_element_type=F32)
              + jnp.dot(x_hi.astype(BF16), wgu_s[half:, :], preferred_element_type=F32))
        a = au[:, :EXPERT_FF]
        u = au[:, EXPERT_FF:]
        h = (a * jax.nn.sigmoid(a)) * u
        o_ref[...] = _pack_bf16_pairs(jnp.dot(h.astype(BF16), wdn_s[...], preferred_element_type=F32))

    @pl.when(i >= nused_ref[0])
    def _():
        o_ref[...] = jnp.zeros_like(o_ref)


def _combine_kernel(pos_ref, w_ref, t_ref, x_ref, sgu_ref, sdn_ref, gain_ref, g2_ref, y_hbm, o_ref, buf, sem):
    tm = o_ref.shape[0]

    def row_copy(n, k):
        return pltpu.make_async_copy(y_hbm.at[pl.ds(pos_ref[k, n], 1)], buf.at[k, pl.ds(n, 1)], sem)

    def issue(n, carry):
        for k in range(TOP_K):
            row_copy(n, k).start()
        return carry

    def drain(n, carry):
        for k in range(TOP_K):
            row_copy(n, k).wait()
        return carry

    lax.fori_loop(0, tm, issue, 0)
    ff = sdn_ref.shape[0]
    au = jnp.dot(t_ref[...].astype(BF16), sgu_ref[...], preferred_element_type=F32)
    a, u = au[:, :ff], au[:, ff:]
    f = jnp.dot(((a * jax.nn.sigmoid(a)) * u).astype(BF16), sdn_ref[...], preferred_element_type=F32)
    lax.fori_loop(0, tm, drain, 0)
    r_lo, r_hi = None, None
    for k in range(TOP_K):
        y_lo, y_hi = _unpack_bf16_pairs(buf[k])
        wk = w_ref[:, k:k + 1]
        r_lo = y_lo * wk if r_lo is None else r_lo + y_lo * wk
        r_hi = y_hi * wk if r_hi is None else r_hi + y_hi * wk
    f = f + jnp.concatenate([r_lo, r_hi], axis=1)
    o_ref[...] = x_ref[...] + g2_ref[...] * (f * lax.rsqrt(jnp.mean(f * f, axis=-1, keepdims=True) + EPS)
                                             * gain_ref[...])


def _moe(t, x_res, router_w, router_b, w_gu, w_down, sh_gu, sh_down, layer, gain, g2_rows, rows_per_gate):
    n, d = t.shape
    ne = router_w.shape[1]
    ff2 = w_gu.shape[-1]
    params = pltpu.CompilerParams(dimension_semantics=("arbitrary",), vmem_limit_bytes=VMEM_LIMIT)
    tm = ROUTER_TILE
    eidx, wsel, cnt = pl.pallas_call(
        _router_kernel,
        out_shape=[jax.ShapeDtypeStruct((TOP_K, n), jnp.int32), jax.ShapeDtypeStruct((TOP_K, n), F32),
                   jax.ShapeDtypeStruct((ne, LANE), F32)],
        grid=(n // tm,),
        in_specs=[pl.BlockSpec((tm, d), lambda i: (i, 0)), pl.BlockSpec((ne, d), lambda i: (0, 0)),
                  pl.BlockSpec((ne, 1), lambda i: (0, 0))],
        out_specs=[pl.BlockSpec((TOP_K, tm), lambda i: (0, i)), pl.BlockSpec((TOP_K, tm), lambda i: (0, i)),
                   pl.BlockSpec((ne, LANE), lambda i: (0, 0))],
        compiler_params=params, name="moe_router",
    )(t, router_w.T, router_b.reshape(ne, 1))
    bm = MOE_ROWS
    counts = cnt[:, 0].astype(jnp.int32)
    padded = (counts + bm - 1) // bm * bm
    pad_end = jnp.cumsum(padded)
    pad_start = pad_end - padded
    n_blocks = (n * TOP_K + ne * (bm - 1)) // bm + 1
    blk_first = jnp.arange(n_blocks, dtype=jnp.int32) * bm
    blk_e = jnp.minimum(jnp.sum((pad_end[None, :] <= blk_first[:, None]).astype(jnp.int32), axis=1), ne - 1)
    n_used = (pad_end[-1] // bm).astype(jnp.int32).reshape(1)
    pos = pl.pallas_call(
        _pos_kernel,
        out_shape=jax.ShapeDtypeStruct((TOP_K, n), jnp.int32),
        grid=(n // tm,),
        in_specs=[pl.BlockSpec((TOP_K, tm), lambda i: (0, i)), pl.BlockSpec((ne, 1), lambda i: (0, 0))],
        out_specs=pl.BlockSpec((TOP_K, tm), lambda i: (0, i)),
        scratch_shapes=[pltpu.VMEM((ne, LANE), F32)],
        compiler_params=params, name="moe_positions",
    )(eidx, pad_start.astype(F32).reshape(ne, 1))
    ts = SCATTER_TILE
    p = n_blocks * bm
    pos_spec = pl.BlockSpec((TOP_K, ts), lambda i: (0, i), memory_space=pltpu.SMEM)
    dp = d // 2
    xs = pl.pallas_call(
        functools.partial(_dispatch_kernel, n_blocks=n_blocks),
        out_shape=jax.ShapeDtypeStruct((p, dp), jnp.int32),
        grid_spec=pltpu.PrefetchScalarGridSpec(
            num_scalar_prefetch=3,
            grid=(n // ts,),
            in_specs=[pl.BlockSpec((TOP_K, ts), lambda i, *_: (0, i), memory_space=pltpu.SMEM),
                      pl.BlockSpec((ts, d), lambda i, *_: (i, 0))],
            out_specs=pl.BlockSpec(memory_space=pl.ANY),
            scratch_shapes=[pltpu.VMEM((ts, dp), jnp.int32), pltpu.VMEM((bm, dp), jnp.int32),
                            pltpu.SemaphoreType.DMA, pltpu.SemaphoreType.DMA],
        ),
        compiler_params=params, name="moe_dispatch",
    )(pad_start + counts, padded - counts, n_used, pos, t)

    is_start = jnp.concatenate([jnp.ones((1,), bool), blk_e[1:] != blk_e[:-1]])
    run_id = jnp.cumsum(is_start.astype(jnp.int32)) - 1
    ids = jnp.arange(ne, dtype=jnp.int32)
    later_busy = jnp.where((counts[None, :] > 0) & (ids[None, :] > ids[:, None]), ids[None, :], ne)
    next_busy = jnp.min(later_busy, axis=1)
    next_e = jnp.sum(jnp.where(blk_e[:, None] == ids[None, :], next_busy[None, :], 0), axis=1)
    next_e = jnp.where(next_e >= ne, -1, next_e).astype(jnp.int32)

    def x_map(i, blk_e, run, nxt, nused):
        return (jnp.minimum(i, nused[0] - 1), 0)

    y_p = pl.pallas_call(
        functools.partial(_moe_ffn_kernel, layer=layer),
        out_shape=jax.ShapeDtypeStruct((p, dp), jnp.int32),
        grid_spec=pltpu.PrefetchScalarGridSpec(
            num_scalar_prefetch=4,
            grid=(n_blocks,),
            in_specs=[pl.BlockSpec((bm, dp), x_map),
                      pl.BlockSpec(memory_space=pl.ANY), pl.BlockSpec(memory_space=pl.ANY)],
            out_specs=pl.BlockSpec((bm, dp), lambda i, *_: (i, 0)),
            scratch_shapes=[pltpu.VMEM((2, d, ff2), F32), pltpu.VMEM((2, ff2 // 2, d), F32),
                            pltpu.VMEM((d, ff2), BF16), pltpu.VMEM((ff2 // 2, d), BF16),
                            pltpu.SemaphoreType.DMA((2, 2))],
        ),
        compiler_params=params, name="moe_expert_ffn",
    )(blk_e, run_id, next_e, n_used, xs, w_gu, w_down)
    rows = pl.BlockSpec((ts, d), lambda i: (i, 0))
    whole = lambda a: pl.BlockSpec(a.shape, lambda i: (0,) * a.ndim)
    last_gate = g2_rows.shape[0] - 1
    gate_spec = pl.BlockSpec((None, 1, d), lambda i: (jnp.minimum(i // (rows_per_gate // ts), last_gate), 0, 0))
    consts = [sh_gu.astype(BF16), sh_down.astype(BF16), gain.reshape(1, d)]
    return pl.pallas_call(
        _combine_kernel,
        out_shape=jax.ShapeDtypeStruct((n, d), F32),
        grid=(n // ts,),
        in_specs=([pos_spec, pl.BlockSpec((ts, TOP_K), lambda i: (i, 0)), rows, rows]
                  + [whole(a) for a in consts] + [gate_spec, pl.BlockSpec(memory_space=pl.ANY)]),
        out_specs=rows,
        scratch_shapes=[pltpu.VMEM((TOP_K, ts, dp), jnp.int32), pltpu.SemaphoreType.DMA],
        compiler_params=params, name="moe_combine",
    )(pos, wsel.T, t, x_res, *consts, g2_rows, y_p)


def kernel(x, c, ctx, c_ctx, ada_w, ada_b, norm_g, w_in, w_out, hy_conv_w, hy_conv_b, hy_ffn_w1, hy_ffn_b1, hy_ffn_w2, hy_ffn_b2, hy_ffn_w3, hy_freq, hy_bias, hy_norm, hg_lb_logits, hg_norm, gla_a_up, gla_a_b, gla_norm, ml_conv_w, ml_conv_b, ml_gate_b, ml_norm, router_w, router_b, exp_w_gu, exp_w_down, sh_w_gu, sh_w_down):
    bsz, seq, d = x.shape
    n_ctx = ctx.shape[1]
    depth = ada_w.shape[0]
    rows = seq // GRID_W
    x = x + _pos_embed_2d(rows, d)[None]
    xc = ctx
    lb_cum = jnp.cumsum(jax.nn.softmax(hg_lb_logits, axis=0), axis=0)
    lower_bounds = lb_cum - lb_cum[0:1]
    for l in range(depth):
        with_ctx = l < depth - 1
        mod = (jax.nn.silu(c) @ ada_w[l] + ada_b[l])[:, None, :]
        mod_c = jax.nn.silu(c_ctx) @ ada_w[l] + ada_b[l]
        sh1, sc1, g1, sh2, sc2, g2 = jnp.split(mod, 6, axis=-1)
        csh1, csc1, cg1, csh2, csc2, cg2 = jnp.split(mod_c, 6, axis=-1)
        w_in_l = _arrange_w_in(w_in[l])
        ctx_rows = lambda a: jnp.broadcast_to(a.reshape(1, 1, d), (bsz, 1, d))
        pl3 = _in_proj(x, norm_g[l, 0], sc1, sh1, w_in_l)
        pc3 = _in_proj(xc, norm_g[l, 0], ctx_rows(csc1), ctx_rows(csh1), w_in_l)
        hy_args = (hy_conv_w[l], hy_conv_b[l], hy_ffn_w1[l], hy_ffn_b1[l], hy_ffn_w2[l], hy_ffn_b2[l],
                   hy_ffn_w3[l], hy_freq[l], hy_bias[l], hy_norm[l])
        y_hy = _hyena_long(pl3, *hy_args)
        oc_hg, o_hg = _hgrn2(pc3, pl3, lower_bounds[l])
        oc_gla, o_gla = _gla(pc3, pl3, gla_a_up[l], gla_a_b[l])
        hc_ml, h_ml = _mlstm(pc3, pl3, ml_conv_w[l], ml_conv_b[l], ml_gate_b[l])
        head_gains = jnp.stack([hg_norm[l], gla_norm[l], ml_norm[l]])
        out_args = (w_out[l], head_gains, norm_g[l, 1], norm_g[l, 2])
        x, h = _mixer_out(y_hy, o_hg, o_gla, h_ml, pl3, x, *out_args, jnp.concatenate([g1, sc2, sh2], axis=1))
        moe_args = (router_w[l], router_b[l], exp_w_gu, exp_w_down, sh_w_gu[l], sh_w_down[l], l, norm_g[l, 3])
        if with_ctx:
            mod_ctx = jnp.broadcast_to(jnp.stack([cg1, csc2, csh2])[None], (bsz, 3, d))
            xc, hc = _mixer_out(_hyena(pc3, *hy_args), oc_hg, oc_gla, hc_ml, pc3, xc, *out_args, mod_ctx)
            tokens = lambda a, ac: jnp.concatenate([a.reshape(bsz * seq, d), ac.reshape(bsz * n_ctx, d)], axis=0)
            gates = jnp.concatenate([g2, cg2.reshape(1, 1, d)], axis=0)
            x_all = _moe(tokens(h, hc), tokens(x, xc), *moe_args, gates, seq)
            x = x_all[:bsz * seq].reshape(bsz, seq, d)
            xc = x_all[bsz * seq:].reshape(bsz, n_ctx, d)
        else:
            x = _moe(h.reshape(bsz * seq, d), x.reshape(bsz * seq, d), *moe_args, g2, seq).reshape(bsz, seq, d)
    return x
```

```python
import functools
import math

import jax
import jax.numpy as jnp
import numpy as np
from jax import lax
from jax.experimental import pallas as pl
from jax.experimental.pallas import tpu as pltpu

F32 = jnp.float32
BF16 = jnp.bfloat16

D_MODEL = 1024
GRID_W = 64
EPS = 1e-6
POS_BASE = 10000.0
GROUP_W = D_MODEL // 4
SHORT_CONV = 3
HY_W = GROUP_W
HY_ORDER = 2
HY_EMB = 33
HY_BANDS = (HY_EMB - 1) // 2
HY_FAST_DECAY = 0.3
HY_SLOW_DECAY = 1.5
HY_DECAY_TARGET = 1e-2
HG_H = 4
HG_W = GROUP_W
HG_DK = HG_W // HG_H
GLA_H = 4
GLA_KW = GROUP_W // 2
GLA_VW = GROUP_W
GLA_DK = GLA_KW // GLA_H
GLA_DV = GLA_VW // GLA_H
GLA_RANK = 16
GLA_NORMALIZER = 16.0
ML_H = 4
ML_W = GROUP_W
ML_DH = ML_W // ML_H
CHUNK_GATED = 16
CHUNK_ML = 64
TOP_K = 8
N_EXPERT_GROUPS = 8
TOPK_GROUPS = 4
EXPERT_FF = 256
ROUTED_SCALE = 2.5
IN_SPLITS = (HY_W, HY_W, HY_W,
             HG_W, HG_W, HG_W, HG_W, HG_W,
             GLA_KW, GLA_KW, GLA_VW, GLA_RANK, GLA_RANK, GLA_VW,
             ML_W, ML_W, ML_W, 4 * ML_H, ML_W)
P_ORDER = (0, 1, 2, 3, 4, 5, 6, 7, 8, 9, 10, 13, 14, 15, 16, 18, 11, 12, 17)
COL_HY = 0
COL_HG_Q, COL_HG_I, COL_HG_ZF, COL_HG_ZB, COL_HG_G = 3, 4, 5, 6, 7
COL_GLA_QK, COL_GLA_V, COL_GLA_R = 8, 9, 10
COL_ML_Q, COL_ML_K, COL_ML_V, COL_ML_O = 11, 12, 13, 14
N_WIDE = 15

LANE = 128
SUBLANE = 8
V7X_VMEM_BYTES = 64 * 1024 * 1024
ROW_TILE = 512
MOE_ROWS = 256
ROUTER_TILE = 256
SCATTER_TILE = 512
GLR_TILE = 128
VMEM_LIMIT = V7X_VMEM_BYTES * 7 // 8
NEG_INF = float("-inf")
COL_NARROW = N_WIDE * GROUP_W // LANE
NARROW_ML_GATES = N_WIDE * GROUP_W + 2 * GLA_RANK
P_WIDTH = N_WIDE * GROUP_W + LANE


def _arrange_w_in(w):
    offs = np.concatenate([[0], np.cumsum(IN_SPLITS)])
    cols = [w[:, offs[i]:offs[i + 1]] for i in P_ORDER]
    used = sum(IN_SPLITS)
    return jnp.concatenate(cols + [jnp.zeros((w.shape[0], P_WIDTH - used), w.dtype)], axis=1).astype(BF16)


def _in_proj_kernel(x_ref, g_ref, sc_ref, sh_ref, w_ref, o_ref):
    x = x_ref[...]
    y = x * lax.rsqrt(jnp.mean(x * x, axis=-1, keepdims=True) + EPS) * g_ref[...]
    h = y * (1.0 + sc_ref[...]) + sh_ref[...]
    o_ref[...] = jnp.dot(h.astype(BF16), w_ref[...], preferred_element_type=F32)


def _in_proj(x, gain, scale, shift, w):
    g, r, d = x.shape
    n = w.shape[1]
    tm = min(ROW_TILE, r)
    assert r % tm == 0
    return pl.pallas_call(
        _in_proj_kernel,
        out_shape=jax.ShapeDtypeStruct((g, r, n), F32),
        grid=(g, r // tm),
        in_specs=[pl.BlockSpec((None, tm, d), lambda b, i: (b, i, 0)),
                  pl.BlockSpec((1, d), lambda b, i: (0, 0)),
                  pl.BlockSpec((None, 1, d), lambda b, i: (b, 0, 0)),
                  pl.BlockSpec((None, 1, d), lambda b, i: (b, 0, 0)),
                  pl.BlockSpec((d, n), lambda b, i: (0, 0))],
        out_specs=pl.BlockSpec((None, tm, n), lambda b, i: (b, i, 0)),
        compiler_params=pltpu.CompilerParams(dimension_semantics=("arbitrary", "arbitrary"),
                                             vmem_limit_bytes=VMEM_LIMIT),
        name="input_projection",
    )(x, gain.reshape(1, d), scale, shift, w)


def _mixer_out_kernel(hy_ref, hgf_ref, hgb_ref, glf_ref, glb_ref, mlf_ref, mlb_ref, ghg_ref, ggl_ref, gml_ref,
                      x_ref, w_ref, hn_ref, n1_ref, n2_ref, mod_ref, xo_ref, h_ref):
    gw = hy_ref.shape[-1]
    seg = jnp.where(lax.broadcasted_iota(jnp.int32, (gw, gw), 0) // HG_DK
                    == lax.broadcasted_iota(jnp.int32, (gw, gw), 1) // HG_DK, 1.0 / HG_DK, 0.0)

    def head_norm(o, gain):
        ms = jnp.dot(o * o, seg, precision=lax.Precision.HIGHEST, preferred_element_type=F32)
        return o * lax.rsqrt(ms + EPS) * gain

    silu = lambda a: a * jax.nn.sigmoid(a)
    groups = (hy_ref[...],
              head_norm(hgf_ref[...] + hgb_ref[...], hn_ref[0:1]) * silu(ghg_ref[...]),
              head_norm(glf_ref[...] + glb_ref[...], hn_ref[1:2]) * silu(ggl_ref[...]),
              jax.nn.sigmoid(gml_ref[...]) * head_norm(mlf_ref[...] + mlb_ref[...], hn_ref[2:3]))
    y = None
    for i, part in enumerate(groups):
        term = jnp.dot(part.astype(BF16), w_ref[i * gw:(i + 1) * gw, :], preferred_element_type=F32)
        y = term if y is None else y + term
    rms = lambda a, g: a * lax.rsqrt(jnp.mean(a * a, axis=-1, keepdims=True) + EPS) * g
    x = x_ref[...] + mod_ref[0:1] * rms(y, n1_ref[...])
    xo_ref[...] = x
    h_ref[...] = rms(x, n2_ref[...]) * (1.0 + mod_ref[1:2]) + mod_ref[2:3]


def _mixer_out(y_hy, o_hg, o_gla, h_ml, p3, x, w_out, head_gains, gain1, gain2, mod):
    assert HG_DK == GLA_DV == ML_DH and HG_H == GLA_H == ML_H
    bsz, t, d = x.shape
    gw = GROUP_W
    tm = min(ROW_TILE, t)
    part = pl.BlockSpec((None, tm, gw), lambda b, i: (b, i, 0))
    gate = lambda c: pl.BlockSpec((None, tm, gw), lambda b, i: (b, i, c))
    full = pl.BlockSpec((None, tm, d), lambda b, i: (b, i, 0))
    whole = lambda a: pl.BlockSpec(a.shape, lambda b, i: (0,) * a.ndim)
    consts = [w_out.astype(BF16), head_gains, gain1.reshape(1, d), gain2.reshape(1, d)]
    return pl.pallas_call(
        _mixer_out_kernel,
        out_shape=[jax.ShapeDtypeStruct((bsz, t, d), F32), jax.ShapeDtypeStruct((bsz, t, d), F32)],
        grid=(bsz, t // tm),
        in_specs=([part] * 7 + [gate(COL_HG_G), gate(COL_GLA_R), gate(COL_ML_O), full]
                  + [whole(a) for a in consts] + [pl.BlockSpec((None, 3, d), lambda b, i: (b, 0, 0))]),
        out_specs=[full, full],
        compiler_params=pltpu.CompilerParams(dimension_semantics=("arbitrary", "arbitrary"),
                                             vmem_limit_bytes=VMEM_LIMIT),
        name="mixer_output",
    )(y_hy, *o_hg, *o_gla, *h_ml, p3, p3, p3, x, *consts, mod)


def _rms(x, g):
    return x * lax.rsqrt(jnp.mean(x * x, axis=-1, keepdims=True) + EPS) * g


def _short_conv_kernel(prev_ref, cur_ref, next_ref, w_ref, b_ref, o_ref, *, act):
    j = pl.program_id(1)
    u = cur_ref[...]
    tt = u.shape[0]
    row = lax.broadcasted_iota(jnp.int32, u.shape, 0)
    before = jnp.where(j > 0, prev_ref[SUBLANE - 1:SUBLANE, :], 0.0)
    after = jnp.where(j < pl.num_programs(1) - 1, next_ref[0:1, :], 0.0)
    up = jnp.where(row == 0, before, pltpu.roll(u, 1, 0))
    dn = jnp.where(row == tt - 1, after, pltpu.roll(u, tt - 1, 0))
    y = w_ref[0:1] * up + w_ref[1:2] * u + w_ref[2:3] * dn + b_ref[...]
    if act:
        y = y * jax.nn.sigmoid(y)
    o_ref[...] = y


def _short_conv(p3, col0, ncols, w, b, act):
    assert SHORT_CONV == 3
    bsz, t, _ = p3.shape
    tt = min(ROW_TILE, t)
    halo = tt // SUBLANE
    last = t // SUBLANE - 1
    gw = GROUP_W
    cur = pl.BlockSpec((None, tt, gw), lambda bi, j, c: (bi, j, col0 + c))
    prev = pl.BlockSpec((None, SUBLANE, gw), lambda bi, j, c: (bi, jnp.maximum(j * halo - 1, 0), col0 + c))
    nxt = pl.BlockSpec((None, SUBLANE, gw), lambda bi, j, c: (bi, jnp.minimum((j + 1) * halo, last), col0 + c))
    return pl.pallas_call(
        functools.partial(_short_conv_kernel, act=act),
        out_shape=jax.ShapeDtypeStruct((bsz, t, ncols * gw), F32),
        grid=(bsz, t // tt, ncols),
        in_specs=[prev, cur, nxt, pl.BlockSpec((SHORT_CONV, gw), lambda bi, j, c: (0, c)),
                  pl.BlockSpec((1, gw), lambda bi, j, c: (0, c))],
        out_specs=pl.BlockSpec((None, tt, gw), lambda bi, j, c: (bi, j, c)),
        compiler_params=pltpu.CompilerParams(dimension_semantics=("arbitrary",) * 3, vmem_limit_bytes=VMEM_LIMIT),
        name="short_conv",
    )(p3, p3, p3, w, b.reshape(1, ncols * gw))


def _pos_embed_2d(rows, d):
    r = jnp.repeat(jnp.arange(rows, dtype=F32), GRID_W)
    col = (jnp.arange(rows * GRID_W) % GRID_W).astype(F32)
    quarter = d // 4
    omega = 1.0 / (POS_BASE ** (jnp.arange(quarter, dtype=F32) / quarter))

    def axis_emb(p):
        ang = p[:, None] * omega[None, :]
        return jnp.concatenate([jnp.sin(ang), jnp.cos(ang)], axis=-1)

    return jnp.concatenate([axis_emb(r), axis_emb(col)], axis=-1)


def _hyena_spectra(L, w1, b1, w2, b2, w3, freq):
    t = jnp.linspace(0.0, 1.0, L, dtype=F32)[:, None]
    w = 2.0 * math.pi * jnp.arange(L, dtype=F32)[:, None] / L
    bands = jnp.linspace(1e-4, HY_BANDS - 1, HY_BANDS, dtype=F32)[None, :]
    feats = jnp.concatenate([t, jnp.cos(bands * w), -jnp.sin(bands * w)], axis=-1)
    z = jnp.sin(freq[0] * (feats @ w1 + b1))
    z = jnp.sin(freq[1] * (z @ w2 + b2))
    h = (z @ w3).reshape(L, HY_ORDER, 2, HY_W)
    max_decay = math.log(HY_DECAY_TARGET) / HY_FAST_DECAY
    min_decay = math.log(HY_DECAY_TARGET) / HY_SLOW_DECAY
    deltas = jnp.abs(jnp.linspace(min_decay, max_decay, HY_W, dtype=F32))
    h = h * jnp.exp(-t[:, :, None, None] * deltas)
    fwd = h[:, :, 0]
    bwd = h[1:, :, 1][::-1]
    l1 = jnp.sum(jnp.abs(fwd), axis=0) + jnp.sum(jnp.abs(bwd), axis=0)
    filt = jnp.concatenate([fwd, jnp.zeros((1, HY_ORDER, HY_W), F32), bwd], axis=0) / l1
    return jnp.fft.rfft(filt, axis=0)


def _fft_conv(u, spec, bias):
    L = u.shape[1]
    y = jnp.fft.irfft(jnp.fft.rfft(u, n=2 * L, axis=1) * spec, n=2 * L, axis=1)[:, :L]
    return y + u * bias


def _hyena(p3, conv_w, conv_b, w1, b1, w2, b2, w3, freq, bias, norm_g):
    u = _short_conv(p3, COL_HY, 3, conv_w, conv_b, False)
    v, x1, x2 = u[..., :HY_W], u[..., HY_W:2 * HY_W], u[..., 2 * HY_W:]
    spec = _hyena_spectra(u.shape[1], w1, b1, w2, b2, w3, freq)
    z = x1 * _fft_conv(v, spec[:, 0], bias[0])
    y = x2 * _fft_conv(z, spec[:, 1], bias[1])
    return _rms(y, norm_g)


FFT_N1 = 128
FFT_N2 = 128
FFT_KTILE = 8
FFT_NTILE = 4096


def _dft_tables(n1, n2):
    n = n1 * n2
    k = np.arange(n1)
    f1 = np.exp(-2j * np.pi * np.outer(k, k) / n1)
    f2 = np.exp(-2j * np.pi * np.outer(np.arange(n2), np.arange(n2)) / n2)
    tw = np.exp(-2j * np.pi * np.outer(np.arange(n1), np.arange(n2)) / n)
    as32 = lambda a: jnp.asarray(np.ascontiguousarray(a), F32)
    f1_fwd = as32(np.concatenate([f1.real, f1.imag], axis=0))
    f1_inv = as32(np.concatenate([f1.real, f1.imag], axis=1) / n)
    f2_inv = as32(np.block([[f2.real, f2.imag], [-f2.imag, f2.real]]))
    return f1_fwd, f1_inv, as32(f2.real), as32(f2.imag), f2_inv, as32(tw.real), as32(tw.imag)


def _stage_kernel(w_ref, x_ref, o_ref):
    o_ref[...] = jnp.dot(w_ref[...].astype(BF16), x_ref[...].astype(BF16),
                         preferred_element_type=F32).astype(o_ref.dtype)


def _stage_matmul(w, x, out_dtype):
    g, k, n = x.shape
    m = w.shape[0]
    tn = FFT_NTILE
    return pl.pallas_call(
        _stage_kernel,
        out_shape=jax.ShapeDtypeStruct((g, m, n), out_dtype),
        grid=(g, n // tn),
        in_specs=[pl.BlockSpec((m, k), lambda b, j: (0, 0)), pl.BlockSpec((None, k, tn), lambda b, j: (b, 0, j))],
        out_specs=pl.BlockSpec((None, m, tn), lambda b, j: (b, 0, j)),
        compiler_params=pltpu.CompilerParams(dimension_semantics=("arbitrary", "arbitrary"),
                                             vmem_limit_bytes=VMEM_LIMIT),
        name="dft_stage",
    )(w, x)


def _twiddled_f2(f2r, f2i, tr, ti):
    gr = f2r * tr - f2i * ti
    gi = f2r * ti + f2i * tr
    return jnp.concatenate([jnp.concatenate([gr, -gi], axis=1), jnp.concatenate([gi, gr], axis=1)], axis=0)


def _spectrum_kernel(a_ref, f2r_ref, f2i_ref, tr_ref, ti_ref, x_ref):
    n2 = f2r_ref.shape[0]
    tr = tr_ref[...]
    ti = ti_ref[...]
    for i in range(a_ref.shape[0]):
        g = _twiddled_f2(f2r_ref[...], f2i_ref[...], tr[i:i + 1], ti[i:i + 1])
        x_ref[i] = jnp.dot(g.astype(BF16), a_ref[i], preferred_element_type=F32)


def _conv_mid_kernel(a_ref, h_ref, f2r_ref, f2i_ref, f2inv_ref, tr_ref, ti_ref, z_ref):
    n2 = f2r_ref.shape[0]
    tr = tr_ref[...]
    ti = ti_ref[...]
    tr_col = tr.T
    ti_col = ti.T
    f2inv = f2inv_ref[...].astype(BF16)
    for i in range(a_ref.shape[0]):
        g = _twiddled_f2(f2r_ref[...], f2i_ref[...], tr[i:i + 1], ti[i:i + 1])
        x = jnp.dot(g.astype(BF16), a_ref[i], preferred_element_type=F32)
        xr, xi = x[:n2], x[n2:]
        hr, hi = h_ref[i, :n2], h_ref[i, n2:]
        y = jnp.concatenate([hr * xr - hi * xi, hr * xi + hi * xr], axis=0)
        w = jnp.dot(f2inv, y.astype(BF16), preferred_element_type=F32)
        wr, wi = w[:n2], w[n2:]
        cr, ci = tr_col[:, i:i + 1], ti_col[:, i:i + 1]
        z_ref[i] = jnp.concatenate([cr * wr + ci * wi, cr * wi - ci * wr], axis=0).astype(z_ref.dtype)


def _dft_mid_specs(g, c):
    n1, n2, kt = FFT_N1, FFT_N2, FFT_KTILE
    blk = pl.BlockSpec((None, kt, 2 * n2, c), lambda b, j: (b, j, 0, 0))
    const = lambda r, cc: pl.BlockSpec((r, cc), lambda b, j: (0, 0))
    twid = pl.BlockSpec((kt, n2), lambda b, j: (j, 0))
    params = pltpu.CompilerParams(dimension_semantics=("arbitrary", "arbitrary"), vmem_limit_bytes=VMEM_LIMIT)
    return blk, const, twid, params, (g, n1 // kt)


def _to_k1_major(a2d, c):
    g = a2d.shape[0]
    return a2d.reshape(g, 2, FFT_N1, FFT_N2, c).transpose(0, 2, 1, 3, 4).reshape(g, FFT_N1, 2 * FFT_N2, c)


def _filter_spectrum(filt, tables):
    g, n, c = filt.shape
    f1_fwd, _, f2r, f2i, _, twr, twi = tables
    a = _stage_matmul(f1_fwd, filt.reshape(g, FFT_N1, FFT_N2 * c), BF16)
    blk, const, twid, params, grid = _dft_mid_specs(g, c)
    return pl.pallas_call(
        _spectrum_kernel,
        out_shape=jax.ShapeDtypeStruct((g, FFT_N1, 2 * FFT_N2, c), F32),
        grid=grid,
        in_specs=[blk, const(FFT_N2, FFT_N2), const(FFT_N2, FFT_N2), twid, twid],
        out_specs=blk, compiler_params=params, name="dft_spectrum",
    )(_to_k1_major(a, c), f2r, f2i, twr, twi)


def _long_conv(u, spec, tables):
    g, l, c = u.shape
    f1_fwd, f1_inv, f2r, f2i, f2inv, twr, twi = tables
    half = l // FFT_N2
    a = _stage_matmul(f1_fwd[:, :half], u.reshape(g, half, FFT_N2 * c), BF16)
    blk, const, twid, params, grid = _dft_mid_specs(g, c)
    hspec = pl.BlockSpec((FFT_KTILE, 2 * FFT_N2, c), lambda b, j: (j, 0, 0))
    z = pl.pallas_call(
        _conv_mid_kernel,
        out_shape=jax.ShapeDtypeStruct((g, FFT_N1, 2 * FFT_N2, c), BF16),
        grid=grid,
        in_specs=[blk, hspec, const(FFT_N2, FFT_N2), const(FFT_N2, FFT_N2), const(2 * FFT_N2, 2 * FFT_N2),
                  twid, twid],
        out_specs=blk, compiler_params=params, name="dft_conv_mid",
    )(_to_k1_major(a, c), spec, f2r, f2i, f2inv, twr, twi)
    z2d = z.reshape(g, FFT_N1, 2, FFT_N2, c).transpose(0, 2, 1, 3, 4).reshape(g, 2 * FFT_N1, FFT_N2 * c)
    y = _stage_matmul(f1_inv[:half], z2d, F32)
    return y.reshape(g, l, c)


def _hyena_filters(L, w1, b1, w2, b2, w3, freq):
    t = jnp.linspace(0.0, 1.0, L, dtype=F32)[:, None]
    w = 2.0 * math.pi * jnp.arange(L, dtype=F32)[:, None] / L
    bands = jnp.linspace(1e-4, HY_BANDS - 1, HY_BANDS, dtype=F32)[None, :]
    feats = jnp.concatenate([t, jnp.cos(bands * w), -jnp.sin(bands * w)], axis=-1)
    max_decay = math.log(HY_DECAY_TARGET) / HY_FAST_DECAY
    min_decay = math.log(HY_DECAY_TARGET) / HY_SLOW_DECAY
    deltas = jnp.abs(jnp.linspace(min_decay, max_decay, HY_W, dtype=F32))
    w3d = w3.reshape(w3.shape[0], HY_ORDER, 2, HY_W)

    def side(f, tt, direction):
        z = jnp.sin(freq[0] * (f @ w1 + b1))
        z = jnp.sin(freq[1] * (z @ w2 + b2))
        h = (z @ w3d[:, :, direction].reshape(w3.shape[0], HY_ORDER * HY_W)).reshape(-1, HY_ORDER, HY_W)
        return h * jnp.exp(-tt[:, :, None] * deltas)

    fwd = side(feats, t, 0)
    bwd = side(feats[::-1], t[::-1], 1)[:L - 1]
    l1 = jnp.sum(jnp.abs(fwd), axis=0) + jnp.sum(jnp.abs(bwd), axis=0)
    return jnp.concatenate([fwd, jnp.zeros((1, HY_ORDER, HY_W), F32), bwd], axis=0) / l1


def _hyena_long(p3, conv_w, conv_b, w1, b1, w2, b2, w3, freq, bias, norm_g):
    u = _short_conv(p3, COL_HY, 3, conv_w, conv_b, False)
    v, x1, x2 = u[..., :HY_W], u[..., HY_W:2 * HY_W], u[..., 2 * HY_W:]
    L = u.shape[1]
    assert 2 * L == FFT_N1 * FFT_N2
    tables = _dft_tables(FFT_N1, FFT_N2)
    filt = _hyena_filters(L, w1, b1, w2, b2, w3, freq)
    spec = _filter_spectrum(jnp.moveaxis(filt, 1, 0), tables)
    z = x1 * (_long_conv(v, spec[0], tables) + v * bias[0])
    y = x2 * (_long_conv(z, spec[1], tables) + z * bias[1])
    return _rms(y, norm_g)


def _glr_direction(q, k, v, g, st_ref, reverse, nh):
    tc, hk = k.shape
    hv = v.shape[1]
    c = CHUNK_GATED
    hi = lax.Precision.HIGHEST
    ti = lax.broadcasted_iota(jnp.int32, (tc, tc), 0)
    tj = lax.broadcasted_iota(jnp.int32, (tc, tc), 1)
    same = (ti // c) == (tj // c)
    seen = (tj >= ti) if reverse else (tj <= ti)
    bcum = jnp.dot(jnp.where(same, jnp.where(seen, 1.0, 0.0), 0.0), g, precision=hi, preferred_element_type=F32)
    btot = jnp.dot(jnp.where(same, 1.0, 0.0), g, precision=hi, preferred_element_type=F32)
    qd = q * jnp.exp(bcum)
    kd = k * jnp.exp(btot - bcum)
    dec = jnp.exp(btot)
    head_sum = jnp.where(lax.broadcasted_iota(jnp.int32, (hk, hv), 0) // (hk // nh)
                         == lax.broadcasted_iota(jnp.int32, (hk, hv), 1) // (hv // nh), 1.0, 0.0).astype(BF16)
    in_chunk = lax.broadcasted_iota(jnp.int32, (tc, hk), 0) % c
    o = jnp.zeros((tc, hv), F32)
    for lag in range(c):
        if lag == 0:
            ks, bs, vs = k, bcum, v
        else:
            shift = tc - lag if reverse else lag
            ks, bs, vs = pltpu.roll(k, shift, 0), pltpu.roll(bcum, shift, 0), pltpu.roll(v, shift, 0)
        valid = (in_chunk + lag <= c - 1) if reverse else (in_chunk >= lag)
        x = q * ks * jnp.exp(jnp.where(valid, bcum - bs, NEG_INF))
        o = o + jnp.dot(x.astype(BF16), head_sum, preferred_element_type=F32) * vs
    head_mask = (lax.broadcasted_iota(jnp.int32, (hv, hk), 0) // (hv // nh)
                 == lax.broadcasted_iota(jnp.int32, (hv, hk), 1) // (hk // nh))
    st = st_ref[...]
    nch = tc // c
    outs = [None] * nch
    for ci in (range(nch - 1, -1, -1) if reverse else range(nch)):
        sl = slice(ci * c, (ci + 1) * c)
        outs[ci] = lax.dot_general(qd[sl].astype(BF16), st.astype(BF16), (((1,), (1,)), ((), ())),
                                   preferred_element_type=F32)
        ds = lax.dot_general(v[sl].astype(BF16), kd[sl].astype(BF16), (((0,), (0,)), ((), ())),
                             preferred_element_type=F32)
        st = st * dec[ci * c:ci * c + 1] + jnp.where(head_mask, ds, 0.0)
    st_ref[...] = st
    return o + jnp.concatenate(outs, axis=0)


def _log_sigmoid(z):
    return jnp.minimum(z, 0.0) - jnp.log1p(jnp.exp(-jnp.abs(z)))


def _hgrn2_kernel(qf_ref, if_ref, zf_ref, qb_ref, ib_ref, zb_ref, lb_ref, s0f_ref, s0b_ref,
                  of_ref, ob_ref, sf_ref, sb_ref, stf, stb):
    j = pl.program_id(1)

    @pl.when(j == 0)
    def _():
        stf[...] = s0f_ref[...]
        stb[...] = s0b_ref[...]

    one_minus_lb, log_lb, log_ub = lb_ref[0:1], lb_ref[1:2], lb_ref[2:3]

    def gate(z):
        return one_minus_lb * jax.nn.sigmoid(-z), jnp.logaddexp(log_lb, log_ub + _log_sigmoid(z))

    silu = lambda a: a * jax.nn.sigmoid(a)
    k_f, g_f = gate(zf_ref[...])
    k_b, g_b = gate(zb_ref[...])
    of_ref[...] = _glr_direction(silu(qf_ref[...]), k_f, if_ref[...], g_f, stf, False, HG_H)
    ob_ref[...] = _glr_direction(silu(qb_ref[...]), k_b, ib_ref[...], g_b, stb, True, HG_H)

    @pl.when(j == pl.num_programs(1) - 1)
    def _():
        sf_ref[...] = stf[...]
        sb_ref[...] = stb[...]


def _gla_kernel(qkf_ref, vf_ref, nf_ref, qkb_ref, vb_ref, nb_ref, aup_ref, ab_ref, s0f_ref, s0b_ref,
                of_ref, ob_ref, sf_ref, sb_ref, stf, stb):
    j = pl.program_id(1)

    @pl.when(j == 0)
    def _():
        stf[...] = s0f_ref[...]
        stb[...] = s0b_ref[...]

    def gate(narrow, idx):
        a = narrow[:, idx * GLA_RANK:(idx + 1) * GLA_RANK]
        lin = jnp.dot(a.astype(BF16), aup_ref[idx].astype(BF16), preferred_element_type=F32) + ab_ref[idx]
        return _log_sigmoid(lin) / GLA_NORMALIZER

    qk_f = qkf_ref[...]
    qk_b = qkb_ref[...]
    of_ref[...] = _glr_direction(qk_f[:, :GLA_KW] * GLA_DK ** -0.5, qk_f[:, GLA_KW:], vf_ref[...],
                                 gate(nf_ref[...], 0), stf, False, GLA_H)
    ob_ref[...] = _glr_direction(qk_b[:, :GLA_KW] * GLA_DK ** -0.5, qk_b[:, GLA_KW:], vb_ref[...],
                                 gate(nb_ref[...], 1), stb, True, GLA_H)

    @pl.when(j == pl.num_programs(1) - 1)
    def _():
        sf_ref[...] = stf[...]
        sb_ref[...] = stb[...]


def _glr_call(kernel_fn, p3, fwd_cols, bwd_cols, consts, s0_f, s0_b, hk, hv, name):
    bsz, t, _ = p3.shape
    tc = GLR_TILE
    nsb = t // tc
    fwd = lambda w, c: pl.BlockSpec((None, tc, w), lambda b, j: (b, j, c))
    bwd = lambda w, c: pl.BlockSpec((None, tc, w), lambda b, j: (b, nsb - 1 - j, c))
    whole = lambda a: pl.BlockSpec(a.shape, lambda b, j: (0,) * a.ndim)
    st = pl.BlockSpec((None, hv, hk), lambda b, j: (b, 0, 0))
    out_f = pl.BlockSpec((None, tc, hv), lambda b, j: (b, j, 0))
    out_b = pl.BlockSpec((None, tc, hv), lambda b, j: (b, nsb - 1 - j, 0))
    return pl.pallas_call(
        kernel_fn,
        out_shape=[jax.ShapeDtypeStruct((bsz, t, hv), F32), jax.ShapeDtypeStruct((bsz, t, hv), F32),
                   jax.ShapeDtypeStruct((bsz, hv, hk), F32), jax.ShapeDtypeStruct((bsz, hv, hk), F32)],
        grid=(bsz, nsb),
        in_specs=([fwd(w, c) for w, c in fwd_cols] + [bwd(w, c) for w, c in bwd_cols]
                  + [whole(a) for a in consts] + [st, st]),
        out_specs=[out_f, out_b, st, st],
        scratch_shapes=[pltpu.VMEM((hv, hk), F32), pltpu.VMEM((hv, hk), F32)],
        compiler_params=pltpu.CompilerParams(dimension_semantics=("arbitrary", "arbitrary"),
                                             vmem_limit_bytes=VMEM_LIMIT),
        name=name,
    )(*([p3] * (len(fwd_cols) + len(bwd_cols))), *consts, s0_f, s0_b)


def _hgrn2_seq(p3, lb, s0_f, s0_b):
    lb_rows = jnp.stack([1.0 - lb, jnp.log(lb), jnp.log1p(-lb)])
    cols = lambda z: [(GROUP_W, COL_HG_Q), (GROUP_W, COL_HG_I), (GROUP_W, z)]
    o_f, o_b, s_f, s_b = _glr_call(_hgrn2_kernel, p3, cols(COL_HG_ZF), cols(COL_HG_ZB), [lb_rows],
                                   s0_f, s0_b, HG_W, HG_W, "hgrn2_recurrence")
    return (o_f, o_b), s_f, s_b


def _hgrn2(pc3, pl3, lb):
    s0 = jnp.zeros((pl3.shape[0], HG_W, HG_W), F32)
    oc, s_f, s_b = _hgrn2_seq(pc3, lb, s0, s0)
    o, _, _ = _hgrn2_seq(pl3, lb, s_f, s_b)
    return oc, o


def _gla_seq(p3, a_up, a_b, s0_f, s0_b):
    cols = [(GROUP_W, COL_GLA_QK), (GROUP_W, COL_GLA_V), (LANE, COL_NARROW)]
    o_f, o_b, s_f, s_b = _glr_call(_gla_kernel, p3, cols, cols, [a_up, a_b.reshape(2, 1, GLA_KW)],
                                   s0_f, s0_b, GLA_KW, GLA_VW, "gla_recurrence")
    return (o_f, o_b), s_f, s_b


def _gla(pc3, pl3, a_up, a_b):
    s0 = jnp.zeros((pl3.shape[0], GLA_VW, GLA_KW), F32)
    oc, s_f, s_b = _gla_seq(pc3, a_up, a_b, s0, s0)
    o, _, _ = _gla_seq(pl3, a_up, a_b, s_f, s_b)
    return oc, o


def _mlstm_direction(q, k, v, igx, lfx, s_ref, n_ref, m_ref, reverse, nh):
    tc, w = q.shape
    seg = w // nh
    assert tc == seg
    hi = lax.Precision.HIGHEST
    ti = lax.broadcasted_iota(jnp.int32, (tc, tc), 0)
    tj = lax.broadcasted_iota(jnp.int32, (tc, tc), 1)
    seen = (tj >= ti) if reverse else (tj <= ti)
    b = jnp.dot(jnp.where(seen, 1.0, 0.0), lfx, precision=hi, preferred_element_type=F32)
    bl = b[0:1] if reverse else b[tc - 1:tc]
    a = bl - b + igx
    ma = jnp.max(a, axis=0, keepdims=True)
    kw = jnp.exp(a - ma) * k
    s_prev = s_ref[...]
    n_prev = n_ref[...]
    m_prev = m_ref[...]
    lane = lax.broadcasted_iota(jnp.int32, (tc, w), 1)
    row = lax.broadcasted_iota(jnp.int32, (tc, w), 0)
    same_head = (lax.broadcasted_iota(jnp.int32, (w, w), 0) // seg
                 == lax.broadcasted_iota(jnp.int32, (w, w), 1) // seg)
    kexp = jnp.where(same_head, jnp.concatenate([k] * nh, axis=0), 0.0)
    vexp = jnp.where(same_head, jnp.concatenate([v] * nh, axis=0), 0.0)
    scores = lax.dot_general(q.astype(BF16), kexp.astype(BF16), (((1,), (1,)), ((), ())),
                             preferred_element_type=F32)
    s_lane = lane % seg
    by_src = jnp.sum(jnp.where(s_lane == row, igx - b, 0.0), axis=0, keepdims=True)
    ok = (s_lane >= row) if reverse else (s_lane <= row)
    dmat = jnp.where(ok, b + by_src, NEG_INF)
    inter = b + m_prev
    head_of_lane = lane // seg
    seg_max = jnp.full((tc, w), NEG_INF, F32)
    for h in range(nh):
        in_h = head_of_lane == h
        seg_max = jnp.where(in_h, jnp.max(jnp.where(in_h, dmat, NEG_INF), axis=1, keepdims=True), seg_max)
    m_t = jnp.maximum(inter, seg_max)
    wq = jnp.exp(dmat - m_t) * scores
    w_int = jnp.exp(inter - m_t)
    head_sum = jnp.where(same_head, 1.0, 0.0).astype(BF16)
    num = (jnp.dot(wq.astype(BF16), vexp.astype(BF16), preferred_element_type=F32)
           + w_int * jnp.dot(q.astype(BF16), s_prev.astype(BF16), preferred_element_type=F32))
    den = (jnp.dot(wq.astype(BF16), head_sum, preferred_element_type=F32)
           + w_int * jnp.dot((q * n_prev).astype(BF16), head_sum, preferred_element_type=F32))
    h_out = num / jnp.maximum(jnp.abs(den), jnp.exp(-m_t))
    m_new = jnp.maximum(bl + m_prev, ma)
    d_old = jnp.exp(bl + m_prev - m_new)
    d_new = jnp.exp(ma - m_new)
    ds = lax.dot_general(kw.astype(BF16), v.astype(BF16), (((0,), (0,)), ((), ())), preferred_element_type=F32)
    s_ref[...] = d_old * s_prev + d_new * jnp.where(same_head, ds, 0.0)
    n_ref[...] = d_old * n_prev + d_new * jnp.sum(kw, axis=0, keepdims=True)
    m_ref[...] = m_new
    return h_out


def _mlstm_kernel(qf_ref, kf_ref, vf_ref, igf_ref, lff_ref, qb_ref, kb_ref, vb_ref, igb_ref, lfb_ref,
                  s0f_ref, n0f_ref, m0f_ref, s0b_ref, n0b_ref, m0b_ref,
                  hf_ref, hb_ref, sf_ref, nf_ref, mf_ref, sb_ref, nb_ref, mb_ref,
                  s_f, n_f, m_f, s_b, n_b, m_b, *, nh):
    j = pl.program_id(1)

    @pl.when(j == 0)
    def _():
        s_f[...] = s0f_ref[...]
        n_f[...] = n0f_ref[...]
        m_f[...] = m0f_ref[...]
        s_b[...] = s0b_ref[...]
        n_b[...] = n0b_ref[...]
        m_b[...] = m0b_ref[...]

    hf_ref[...] = _mlstm_direction(qf_ref[...], kf_ref[...], vf_ref[...], igf_ref[...], lff_ref[...],
                                   s_f, n_f, m_f, False, nh)
    hb_ref[...] = _mlstm_direction(qb_ref[...], kb_ref[...], vb_ref[...], igb_ref[...], lfb_ref[...],
                                   s_b, n_b, m_b, True, nh)

    @pl.when(j == pl.num_programs(1) - 1)
    def _():
        sf_ref[...] = s_f[...]
        nf_ref[...] = n_f[...]
        mf_ref[...] = m_f[...]
        sb_ref[...] = s_b[...]
        nb_ref[...] = n_b[...]
        mb_ref[...] = m_b[...]


def _mlstm_bidir(q, k, v, ig_f, lf_f, ig_b, lf_b, st_f, st_b, nh):
    bsz, t, w = q.shape
    tc = CHUNK_ML
    nsb = t // tc
    fwd = pl.BlockSpec((None, tc, w), lambda b, j: (b, j, 0))
    bwd = pl.BlockSpec((None, tc, w), lambda b, j: (b, nsb - 1 - j, 0))
    mat = pl.BlockSpec((None, w, w), lambda b, j: (b, 0, 0))
    vec = pl.BlockSpec((None, 1, w), lambda b, j: (b, 0, 0))
    sds = jax.ShapeDtypeStruct
    state_shapes = [sds((bsz, w, w), F32), sds((bsz, 1, w), F32), sds((bsz, 1, w), F32)]
    outs = pl.pallas_call(
        functools.partial(_mlstm_kernel, nh=nh),
        out_shape=[sds((bsz, t, w), F32), sds((bsz, t, w), F32)] + state_shapes + state_shapes,
        grid=(bsz, nsb),
        in_specs=[fwd] * 5 + [bwd] * 5 + [mat, vec, vec] * 2,
        out_specs=[fwd, bwd] + [mat, vec, vec] * 2,
        scratch_shapes=[pltpu.VMEM((w, w), F32), pltpu.VMEM((1, w), F32), pltpu.VMEM((1, w), F32)] * 2,
        compiler_params=pltpu.CompilerParams(dimension_semantics=("arbitrary", "arbitrary"),
                                             vmem_limit_bytes=VMEM_LIMIT),
        name="mlstm_recurrence",
    )(q, k, v, ig_f, lf_f, q, k, v, ig_b, lf_b, *st_f, *st_b)
    return outs[0], outs[1], tuple(outs[2:5]), tuple(outs[5:8])


def _mlstm_seq(p3, conv_w, conv_b, gate_b, st_f, st_b):
    qk = _short_conv(p3, COL_ML_Q, 2, conv_w, conv_b, True)
    v = p3[..., COL_ML_V * GROUP_W:(COL_ML_V + 1) * GROUP_W]
    bsz, t, _ = p3.shape
    gates = p3[..., NARROW_ML_GATES:NARROW_ML_GATES + 4 * ML_H]
    gt = gates.reshape(bsz, t, 4, ML_H) + gate_b
    expand = lambda a: jnp.repeat(a, ML_DH, axis=-1)
    h_f, h_b, fin_f, fin_b = _mlstm_bidir(
        qk[..., :ML_W], qk[..., ML_W:] * ML_DH ** -0.5, v,
        expand(gt[:, :, 0]), expand(jax.nn.log_sigmoid(gt[:, :, 1])),
        expand(gt[:, :, 2]), expand(jax.nn.log_sigmoid(gt[:, :, 3])), st_f, st_b, ML_H)
    return (h_f, h_b), fin_f, fin_b


def _mlstm(pc3, pl3, conv_w, conv_b, gate_b):
    bsz = pl3.shape[0]
    st0 = (jnp.zeros((bsz, ML_W, ML_W), F32), jnp.zeros((bsz, 1, ML_W), F32), jnp.zeros((bsz, 1, ML_W), F32))
    hc, st_f, st_b = _mlstm_seq(pc3, conv_w, conv_b, gate_b, st0, st0)
    h, _, _ = _mlstm_seq(pl3, conv_w, conv_b, gate_b, st_f, st_b)
    return hc, h


def _router_kernel(h_ref, wt_ref, b_ref, eidx_ref, wsel_ref, cnt_ref):
    i = pl.program_id(0)
    tm = h_ref.shape[0]
    ne = wt_ref.shape[0]
    per_group = ne // N_EXPERT_GROUPS
    logits = lax.dot_general(wt_ref[...], h_ref[...], (((1,), (1,)), ((), ())),
                             preferred_element_type=F32, precision=lax.Precision.HIGHEST)
    s = jax.nn.sigmoid(logits)
    sel = s + b_ref[...]
    row = lax.broadcasted_iota(jnp.int32, (ne, tm), 0)
    gs = []
    for g in range(N_EXPERT_GROUPS):
        blk = sel[g * per_group:(g + 1) * per_group]
        r = lax.broadcasted_iota(jnp.int32, blk.shape, 0)
        m1 = jnp.max(blk, axis=0, keepdims=True)
        i1 = jnp.min(jnp.where(blk == m1, r, per_group), axis=0, keepdims=True)
        m2 = jnp.max(jnp.where(r == i1, NEG_INF, blk), axis=0, keepdims=True)
        gs.append(m1 + m2)
    grp = jnp.concatenate(gs, axis=0)
    grow = lax.broadcasted_iota(jnp.int32, grp.shape, 0)
    gsel = jnp.zeros(grp.shape, F32)
    for _ in range(TOPK_GROUPS):
        m = jnp.max(grp, axis=0, keepdims=True)
        gi = jnp.min(jnp.where(grp == m, grow, N_EXPERT_GROUPS), axis=0, keepdims=True)
        hit = grow == gi
        gsel = jnp.where(hit, 1.0, gsel)
        grp = jnp.where(hit, NEG_INF, grp)
    masked = jnp.concatenate(
        [jnp.where(gsel[g:g + 1] > 0.0, sel[g * per_group:(g + 1) * per_group], NEG_INF)
         for g in range(N_EXPERT_GROUPS)], axis=0)
    eis, ws = [], []
    picked = jnp.zeros((ne, tm), F32)
    for _ in range(TOP_K):
        m = jnp.max(masked, axis=0, keepdims=True)
        ei = jnp.min(jnp.where(masked == m, row, ne), axis=0, keepdims=True)
        hit = row == ei
        ws.append(jnp.sum(jnp.where(hit, s, 0.0), axis=0, keepdims=True))
        eis.append(ei)
        picked = jnp.where(hit, 1.0, picked)
        masked = jnp.where(hit, NEG_INF, masked)
    w = jnp.concatenate(ws, axis=0)
    eidx_ref[...] = jnp.concatenate(eis, axis=0)
    wsel_ref[...] = w / jnp.sum(w, axis=0, keepdims=True) * ROUTED_SCALE
    tot = jnp.dot(picked.astype(BF16), jnp.ones((tm, LANE), BF16), preferred_element_type=F32)

    @pl.when(i == 0)
    def _():
        cnt_ref[...] = jnp.zeros_like(cnt_ref)

    cnt_ref[...] += tot


def _pos_kernel(eidx_ref, base_ref, pos_ref, carry_ref):
    i = pl.program_id(0)
    tm = eidx_ref.shape[1]
    ne = base_ref.shape[0]

    @pl.when(i == 0)
    def _():
        carry_ref[...] = jnp.zeros_like(carry_ref)

    eidx = eidx_ref[...]
    row = lax.broadcasted_iota(jnp.int32, (ne, tm), 0)
    picked = jnp.zeros((ne, tm), F32)
    for k in range(TOP_K):
        picked = jnp.where(row == eidx[k:k + 1], 1.0, picked)
    pb = picked.astype(BF16)
    before = jnp.where(lax.broadcasted_iota(jnp.int32, (tm, tm), 0) < lax.broadcasted_iota(jnp.int32, (tm, tm), 1),
                       1.0, 0.0).astype(BF16)
    rank = jnp.dot(pb, before, preferred_element_type=F32)
    tot = jnp.dot(pb, jnp.ones((tm, LANE), BF16), preferred_element_type=F32)
    dest = rank + (base_ref[...] + carry_ref[:, 0:1])
    pos = [jnp.sum(jnp.where(row == eidx[k:k + 1], dest, 0.0), axis=0, keepdims=True) for k in range(TOP_K)]
    pos_ref[...] = jnp.concatenate(pos, axis=0).astype(jnp.int32)
    carry_ref[...] += tot


def _pack_bf16_pairs(x):
    half = x.shape[1] // 2
    bits = lambda a: pltpu.bitcast(a.astype(BF16).astype(F32), jnp.int32)
    return (bits(x[:, half:]) & -65536) | lax.shift_right_logical(bits(x[:, :half]), 16)


def _unpack_bf16_pairs(w):
    return pltpu.bitcast(lax.shift_left(w, 16), F32), pltpu.bitcast(w & -65536, F32)


def _dispatch_kernel(zstart_ref, zlen_ref, nused_ref, pos_ref, h_ref, xs_ref, packed, zeros, sem, zsem, *,
                     n_blocks):
    tm = h_ref.shape[0]
    bm = zeros.shape[0]
    packed[...] = _pack_bf16_pairs(h_ref[...])

    @pl.when(pl.program_id(0) == 0)
    def _():
        zeros[...] = jnp.zeros_like(zeros)

        def zero_copy(start, size):
            return pltpu.make_async_copy(zeros.at[pl.ds(0, size)], xs_ref.at[pl.ds(start, size)], zsem)

        def pieces(e, act):
            start = zstart_ref[e]
            rem = zlen_ref[e]
            ragged = rem & (SUBLANE - 1)
            for q in range(SUBLANE - 1):
                @pl.when(q < ragged)
                def _(q=q):
                    act(zero_copy(start + q, 1))

            start = pl.multiple_of(start + ragged, SUBLANE)
            size = bm // 2
            while size >= SUBLANE:
                @pl.when((rem & size) != 0)
                def _(start=start, size=size):
                    act(zero_copy(start, size))

                start = pl.multiple_of(start + (rem & size), SUBLANE)
                size //= 2

        def loop(act):
            def per_expert(e, carry):
                pieces(e, act)
                return carry

            def per_block(b, carry):
                act(zero_copy(pl.multiple_of(b * bm, bm), bm))
                return carry

            lax.fori_loop(0, zstart_ref.shape[0], per_expert, 0)
            lax.fori_loop(nused_ref[0], n_blocks, per_block, 0)

        loop(lambda cp: cp.start())
        loop(lambda cp: cp.wait())

    def row_copy(n, k):
        return pltpu.make_async_copy(packed.at[pl.ds(n, 1)], xs_ref.at[pl.ds(pos_ref[n * TOP_K + k], 1)], sem)

    def issue(n, carry):
        for k in range(TOP_K):
            row_copy(n, k).start()
        return carry

    def drain(n, carry):
        for k in range(TOP_K):
            row_copy(n, k).wait()
        return carry

    lax.fori_loop(0, tm, issue, 0)
    lax.fori_loop(0, tm, drain, 0)


def _moe_ffn_kernel(blk_e_ref, run_ref, next_e_ref, nused_ref, x_ref, wgu_hbm, wdn_hbm, o_ref,
                    wgu_f, wdn_f, wgu_s, wdn_s, sems, *, layer):
    i = pl.program_id(0)
    e = blk_e_ref[i]
    run = run_ref[i]
    first_of_run = (i == 0) | (run != run_ref[jnp.maximum(i - 1, 0)])

    def fetch(expert, slot):
        return (pltpu.make_async_copy(wgu_hbm.at[layer, expert], wgu_f.at[slot], sems.at[0, slot]),
                pltpu.make_async_copy(wdn_hbm.at[layer, expert], wdn_f.at[slot], sems.at[1, slot]))

    @pl.when(i < nused_ref[0])
    def _():
        @pl.when(first_of_run)
        def _():
            slot = run & 1

            @pl.when(i == 0)
            def _():
                for cp in fetch(e, 0):
                    cp.start()

            for cp in fetch(e, slot):
                cp.wait()
            wgu_s[...] = wgu_f[slot].astype(BF16)
            wdn_s[...] = wdn_f[slot].astype(BF16)

            @pl.when(next_e_ref[i] >= 0)
            def _():
                for cp in fetch(next_e_ref[i], 1 - slot):
                    cp.start()

        x_lo, x_hi = _unpack_bf16_pairs(x_ref[...])
        half = x_lo.shape[1]
        au = (jnp.dot(x_lo.astype(BF16), wgu_s[:half, :], preferred_element_type=F32)
              + jnp.dot(x_hi.astype(BF16), wgu_s[half:, :], preferred_element_type=F32))
        a = au[:, :EXPERT_FF]
        u = au[:, EXPERT_FF:]
        h = (a * jax.nn.sigmoid(a)) * u
        o_ref[...] = _pack_bf16_pairs(jnp.dot(h.astype(BF16), wdn_s[...], preferred_element_type=F32))

    @pl.when(i >= nused_ref[0])
    def _():
        o_ref[...] = jnp.zeros_like(o_ref)


def _combine_kernel(pos_ref, w_ref, t_ref, x_ref, sgu_ref, sdn_ref, gain_ref, g2_ref, y_hbm, o_ref, buf, sem):
    tm = o_ref.shape[0]

    def row_copy(n, k):
        return pltpu.make_async_copy(y_hbm.at[pl.ds(pos_ref[n * TOP_K + k], 1)], buf.at[k, pl.ds(n, 1)], sem)

    def issue(n, carry):
        for k in range(TOP_K):
            row_copy(n, k).start()
        return carry

    def drain(n, carry):
        for k in range(TOP_K):
            row_copy(n, k).wait()
        return carry

    lax.fori_loop(0, tm, issue, 0)
    ff = sdn_ref.shape[0]
    au = jnp.dot(t_ref[...].astype(BF16), sgu_ref[...], preferred_element_type=F32)
    a, u = au[:, :ff], au[:, ff:]
    f = jnp.dot(((a * jax.nn.sigmoid(a)) * u).astype(BF16), sdn_ref[...], preferred_element_type=F32)
    lax.fori_loop(0, tm, drain, 0)
    r_lo, r_hi = None, None
    for k in range(TOP_K):
        y_lo, y_hi = _unpack_bf16_pairs(buf[k])
        wk = w_ref[:, k:k + 1]
        r_lo = y_lo * wk if r_lo is None else r_lo + y_lo * wk
        r_hi = y_hi * wk if r_hi is None else r_hi + y_hi * wk
    f = f + jnp.concatenate([r_lo, r_hi], axis=1)
    o_ref[...] = x_ref[...] + g2_ref[...] * (f * lax.rsqrt(jnp.mean(f * f, axis=-1, keepdims=True) + EPS)
                                             * gain_ref[...])


def _moe(t, x_res, router_w, router_b, w_gu, w_down, sh_gu, sh_down, layer, gain, g2_rows, rows_per_gate):
    n, d = t.shape
    ne = router_w.shape[1]
    ff2 = w_gu.shape[-1]
    params = pltpu.CompilerParams(dimension_semantics=("arbitrary",), vmem_limit_bytes=VMEM_LIMIT)
    tm = ROUTER_TILE
    eidx, wsel, cnt = pl.pallas_call(
        _router_kernel,
        out_shape=[jax.ShapeDtypeStruct((TOP_K, n), jnp.int32), jax.ShapeDtypeStruct((TOP_K, n), F32),
                   jax.ShapeDtypeStruct((ne, LANE), F32)],
        grid=(n // tm,),
        in_specs=[pl.BlockSpec((tm, d), lambda i: (i, 0)), pl.BlockSpec((ne, d), lambda i: (0, 0)),
                  pl.BlockSpec((ne, 1), lambda i: (0, 0))],
        out_specs=[pl.BlockSpec((TOP_K, tm), lambda i: (0, i)), pl.BlockSpec((TOP_K, tm), lambda i: (0, i)),
                   pl.BlockSpec((ne, LANE), lambda i: (0, 0))],
        compiler_params=params, name="moe_router",
    )(t, router_w.T, router_b.reshape(ne, 1))
    bm = MOE_ROWS
    counts = cnt[:, 0].astype(jnp.int32)
    padded = (counts + bm - 1) // bm * bm
    pad_end = jnp.cumsum(padded)
    pad_start = pad_end - padded
    n_blocks = (n * TOP_K + ne * (bm - 1)) // bm + 1
    blk_first = jnp.arange(n_blocks, dtype=jnp.int32) * bm
    blk_e = jnp.minimum(jnp.sum((pad_end[None, :] <= blk_first[:, None]).astype(jnp.int32), axis=1), ne - 1)
    n_used = (pad_end[-1] // bm).astype(jnp.int32).reshape(1)
    pos = pl.pallas_call(
        _pos_kernel,
        out_shape=jax.ShapeDtypeStruct((TOP_K, n), jnp.int32),
        grid=(n // tm,),
        in_specs=[pl.BlockSpec((TOP_K, tm), lambda i: (0, i)), pl.BlockSpec((ne, 1), lambda i: (0, 0))],
        out_specs=pl.BlockSpec((TOP_K, tm), lambda i: (0, i)),
        scratch_shapes=[pltpu.VMEM((ne, LANE), F32)],
        compiler_params=params, name="moe_positions",
    )(eidx, pad_start.astype(F32).reshape(ne, 1))
    ts = SCATTER_TILE
    p = n_blocks * bm
    pos_tok = pos.T.reshape(n * TOP_K)
    pos_spec = pl.BlockSpec((ts * TOP_K,), lambda i: (i,), memory_space=pltpu.SMEM)
    dp = d // 2
    xs = pl.pallas_call(
        functools.partial(_dispatch_kernel, n_blocks=n_blocks),
        out_shape=jax.ShapeDtypeStruct((p, dp), jnp.int32),
        grid_spec=pltpu.PrefetchScalarGridSpec(
            num_scalar_prefetch=3,
            grid=(n // ts,),
            in_specs=[pl.BlockSpec((ts * TOP_K,), lambda i, *_: (i,), memory_space=pltpu.SMEM),
                      pl.BlockSpec((ts, d), lambda i, *_: (i, 0))],
            out_specs=pl.BlockSpec(memory_space=pl.ANY),
            scratch_shapes=[pltpu.VMEM((ts, dp), jnp.int32), pltpu.VMEM((bm, dp), jnp.int32),
                            pltpu.SemaphoreType.DMA, pltpu.SemaphoreType.DMA],
        ),
        compiler_params=params, name="moe_dispatch",
    )(pad_start + counts, padded - counts, n_used, pos_tok, t)

    is_start = jnp.concatenate([jnp.ones((1,), bool), blk_e[1:] != blk_e[:-1]])
    run_id = jnp.cumsum(is_start.astype(jnp.int32)) - 1
    ids = jnp.arange(ne, dtype=jnp.int32)
    later_busy = jnp.where((counts[None, :] > 0) & (ids[None, :] > ids[:, None]), ids[None, :], ne)
    next_busy = jnp.min(later_busy, axis=1)
    next_e = jnp.sum(jnp.where(blk_e[:, None] == ids[None, :], next_busy[None, :], 0), axis=1)
    next_e = jnp.where(next_e >= ne, -1, next_e).astype(jnp.int32)

    def x_map(i, blk_e, run, nxt, nused):
        return (jnp.minimum(i, nused[0] - 1), 0)

    y_p = pl.pallas_call(
        functools.partial(_moe_ffn_kernel, layer=layer),
        out_shape=jax.ShapeDtypeStruct((p, dp), jnp.int32),
        grid_spec=pltpu.PrefetchScalarGridSpec(
            num_scalar_prefetch=4,
            grid=(n_blocks,),
            in_specs=[pl.BlockSpec((bm, dp), x_map),
                      pl.BlockSpec(memory_space=pl.ANY), pl.BlockSpec(memory_space=pl.ANY)],
            out_specs=pl.BlockSpec((bm, dp), lambda i, *_: (i, 0)),
            scratch_shapes=[pltpu.VMEM((2, d, ff2), F32), pltpu.VMEM((2, ff2 // 2, d), F32),
                            pltpu.VMEM((d, ff2), BF16), pltpu.VMEM((ff2 // 2, d), BF16),
                            pltpu.SemaphoreType.DMA((2, 2))],
        ),
        compiler_params=params, name="moe_expert_ffn",
    )(blk_e, run_id, next_e, n_used, xs, w_gu, w_down)
    rows = pl.BlockSpec((ts, d), lambda i: (i, 0))
    whole = lambda a: pl.BlockSpec(a.shape, lambda i: (0,) * a.ndim)
    last_gate = g2_rows.shape[0] - 1
    gate_spec = pl.BlockSpec((None, 1, d), lambda i: (jnp.minimum(i // (rows_per_gate // ts), last_gate), 0, 0))
    consts = [sh_gu.astype(BF16), sh_down.astype(BF16), gain.reshape(1, d)]
    return pl.pallas_call(
        _combine_kernel,
        out_shape=jax.ShapeDtypeStruct((n, d), F32),
        grid=(n // ts,),
        in_specs=([pos_spec, pl.BlockSpec((ts, TOP_K), lambda i: (i, 0)), rows, rows]
                  + [whole(a) for a in consts] + [gate_spec, pl.BlockSpec(memory_space=pl.ANY)]),
        out_specs=rows,
        scratch_shapes=[pltpu.VMEM((TOP_K, ts, dp), jnp.int32), pltpu.SemaphoreType.DMA],
        compiler_params=params, name="moe_combine",
    )(pos_tok, wsel.T, t, x_res, *consts, g2_rows, y_p)


def kernel(x, c, ctx, c_ctx, ada_w, ada_b, norm_g, w_in, w_out, hy_conv_w, hy_conv_b, hy_ffn_w1, hy_ffn_b1, hy_ffn_w2, hy_ffn_b2, hy_ffn_w3, hy_freq, hy_bias, hy_norm, hg_lb_logits, hg_norm, gla_a_up, gla_a_b, gla_norm, ml_conv_w, ml_conv_b, ml_gate_b, ml_norm, router_w, router_b, exp_w_gu, exp_w_down, sh_w_gu, sh_w_down):
    bsz, seq, d = x.shape
    n_ctx = ctx.shape[1]
    depth = ada_w.shape[0]
    rows = seq // GRID_W
    x = x + _pos_embed_2d(rows, d)[None]
    xc = ctx
    lb_cum = jnp.cumsum(jax.nn.softmax(hg_lb_logits, axis=0), axis=0)
    lower_bounds = lb_cum - lb_cum[0:1]
    for l in range(depth):
        with_ctx = l < depth - 1
        mod = (jax.nn.silu(c) @ ada_w[l] + ada_b[l])[:, None, :]
        mod_c = jax.nn.silu(c_ctx) @ ada_w[l] + ada_b[l]
        sh1, sc1, g1, sh2, sc2, g2 = jnp.split(mod, 6, axis=-1)
        csh1, csc1, cg1, csh2, csc2, cg2 = jnp.split(mod_c, 6, axis=-1)
        w_in_l = _arrange_w_in(w_in[l])
        ctx_rows = lambda a: jnp.broadcast_to(a.reshape(1, 1, d), (bsz, 1, d))
        pl3 = _in_proj(x, norm_g[l, 0], sc1, sh1, w_in_l)
        pc3 = _in_proj(xc, norm_g[l, 0], ctx_rows(csc1), ctx_rows(csh1), w_in_l)
        hy_args = (hy_conv_w[l], hy_conv_b[l], hy_ffn_w1[l], hy_ffn_b1[l], hy_ffn_w2[l], hy_ffn_b2[l],
                   hy_ffn_w3[l], hy_freq[l], hy_bias[l], hy_norm[l])
        y_hy = _hyena_long(pl3, *hy_args)
        oc_hg, o_hg = _hgrn2(pc3, pl3, lower_bounds[l])
        oc_gla, o_gla = _gla(pc3, pl3, gla_a_up[l], gla_a_b[l])
        hc_ml, h_ml = _mlstm(pc3, pl3, ml_conv_w[l], ml_conv_b[l], ml_gate_b[l])
        head_gains = jnp.stack([hg_norm[l], gla_norm[l], ml_norm[l]])
        out_args = (w_out[l], head_gains, norm_g[l, 1], norm_g[l, 2])
        x, h = _mixer_out(y_hy, o_hg, o_gla, h_ml, pl3, x, *out_args, jnp.concatenate([g1, sc2, sh2], axis=1))
        moe_args = (router_w[l], router_b[l], exp_w_gu, exp_w_down, sh_w_gu[l], sh_w_down[l], l, norm_g[l, 3])
        if with_ctx:
            mod_ctx = jnp.broadcast_to(jnp.stack([cg1, csc2, csh2])[None], (bsz, 3, d))
            xc, hc = _mixer_out(_hyena(pc3, *hy_args), oc_hg, oc_gla, hc_ml, pc3, xc, *out_args, mod_ctx)
            tokens = lambda a, ac: jnp.concatenate([a.reshape(bsz * seq, d), ac.reshape(bsz * n_ctx, d)], axis=0)
            gates = jnp.concatenate([g2, cg2.reshape(1, 1, d)], axis=0)
            x_all = _moe(tokens(h, hc), tokens(x, xc), *moe_args, gates, seq)
            x = x_all[:bsz * seq].reshape(bsz, seq, d)
            xc = x_all[bsz * seq:].reshape(bsz, n_ctx, d)
        else:
            x = _moe(h.reshape(bsz * seq, d), x.reshape(bsz * seq, d), *moe_args, g2, seq).reshape(bsz, seq, d)
    return x
```

```python
import functools
import math

import jax
import jax.numpy as jnp
import numpy as np
from jax import lax
from jax.experimental import pallas as pl
from jax.experimental.pallas import tpu as pltpu

F32 = jnp.float32
BF16 = jnp.bfloat16

D_MODEL = 1024
GRID_W = 64
EPS = 1e-6
POS_BASE = 10000.0
GROUP_W = D_MODEL // 4
SHORT_CONV = 3
HY_W = GROUP_W
HY_ORDER = 2
HY_EMB = 33
HY_BANDS = (HY_EMB - 1) // 2
HY_FAST_DECAY = 0.3
HY_SLOW_DECAY = 1.5
HY_DECAY_TARGET = 1e-2
HG_H = 4
HG_W = GROUP_W
HG_DK = HG_W // HG_H
GLA_H = 4
GLA_KW = GROUP_W // 2
GLA_VW = GROUP_W
GLA_DK = GLA_KW // GLA_H
GLA_DV = GLA_VW // GLA_H
GLA_RANK = 16
GLA_NORMALIZER = 16.0
ML_H = 4
ML_W = GROUP_W
ML_DH = ML_W // ML_H
CHUNK_GATED = 16
CHUNK_ML = 64
TOP_K = 8
N_EXPERT_GROUPS = 8
TOPK_GROUPS = 4
EXPERT_FF = 256
ROUTED_SCALE = 2.5
IN_SPLITS = (HY_W, HY_W, HY_W,
             HG_W, HG_W, HG_W, HG_W, HG_W,
             GLA_KW, GLA_KW, GLA_VW, GLA_RANK, GLA_RANK, GLA_VW,
             ML_W, ML_W, ML_W, 4 * ML_H, ML_W)
P_ORDER = (0, 1, 2, 3, 4, 5, 6, 7, 8, 9, 10, 13, 14, 15, 16, 18, 11, 12, 17)
COL_HY = 0
COL_HG_Q, COL_HG_I, COL_HG_ZF, COL_HG_ZB, COL_HG_G = 3, 4, 5, 6, 7
COL_GLA_QK, COL_GLA_V, COL_GLA_R = 8, 9, 10
COL_ML_Q, COL_ML_K, COL_ML_V, COL_ML_O = 11, 12, 13, 14
N_WIDE = 15

LANE = 128
SUBLANE = 8
V7X_VMEM_BYTES = 64 * 1024 * 1024
ROW_TILE = 512
MOE_ROWS = 256
ROUTER_TILE = 256
SCATTER_TILE = 512
GLR_TILE = 128
VMEM_LIMIT = V7X_VMEM_BYTES * 7 // 8
NEG_INF = float("-inf")
COL_NARROW = N_WIDE * GROUP_W // LANE
NARROW_ML_GATES = N_WIDE * GROUP_W + 2 * GLA_RANK
P_WIDTH = N_WIDE * GROUP_W + LANE


def _arrange_w_in(w):
    offs = np.concatenate([[0], np.cumsum(IN_SPLITS)])
    cols = [w[:, offs[i]:offs[i + 1]] for i in P_ORDER]
    used = sum(IN_SPLITS)
    return jnp.concatenate(cols + [jnp.zeros((w.shape[0], P_WIDTH - used), w.dtype)], axis=1).astype(BF16)


def _in_proj_kernel(x_ref, g_ref, sc_ref, sh_ref, w_ref, o_ref):
    x = x_ref[...]
    y = x * lax.rsqrt(jnp.mean(x * x, axis=-1, keepdims=True) + EPS) * g_ref[...]
    h = y * (1.0 + sc_ref[...]) + sh_ref[...]
    o_ref[...] = jnp.dot(h.astype(BF16), w_ref[...], preferred_element_type=F32)


def _in_proj(x, gain, scale, shift, w):
    g, r, d = x.shape
    n = w.shape[1]
    tm = min(ROW_TILE, r)
    assert r % tm == 0
    return pl.pallas_call(
        _in_proj_kernel,
        out_shape=jax.ShapeDtypeStruct((g, r, n), F32),
        grid=(g, r // tm),
        in_specs=[pl.BlockSpec((None, tm, d), lambda b, i: (b, i, 0)),
                  pl.BlockSpec((1, d), lambda b, i: (0, 0)),
                  pl.BlockSpec((None, 1, d), lambda b, i: (b, 0, 0)),
                  pl.BlockSpec((None, 1, d), lambda b, i: (b, 0, 0)),
                  pl.BlockSpec((d, n), lambda b, i: (0, 0))],
        out_specs=pl.BlockSpec((None, tm, n), lambda b, i: (b, i, 0)),
        compiler_params=pltpu.CompilerParams(dimension_semantics=("arbitrary", "arbitrary"),
                                             vmem_limit_bytes=VMEM_LIMIT),
        name="input_projection",
    )(x, gain.reshape(1, d), scale, shift, w)


def _mixer_out_kernel(hy_ref, hgf_ref, hgb_ref, glf_ref, glb_ref, mlf_ref, mlb_ref, ghg_ref, ggl_ref, gml_ref,
                      x_ref, w_ref, hn_ref, n1_ref, n2_ref, mod_ref, xo_ref, h_ref):
    gw = hy_ref.shape[-1]
    seg = jnp.where(lax.broadcasted_iota(jnp.int32, (gw, gw), 0) // HG_DK
                    == lax.broadcasted_iota(jnp.int32, (gw, gw), 1) // HG_DK, 1.0 / HG_DK, 0.0)

    def head_norm(o, gain):
        ms = jnp.dot(o * o, seg, precision=lax.Precision.HIGHEST, preferred_element_type=F32)
        return o * lax.rsqrt(ms + EPS) * gain

    silu = lambda a: a * jax.nn.sigmoid(a)
    groups = (hy_ref[...],
              head_norm(hgf_ref[...] + hgb_ref[...], hn_ref[0:1]) * silu(ghg_ref[...]),
              head_norm(glf_ref[...] + glb_ref[...], hn_ref[1:2]) * silu(ggl_ref[...]),
              jax.nn.sigmoid(gml_ref[...]) * head_norm(mlf_ref[...] + mlb_ref[...], hn_ref[2:3]))
    y = None
    for i, part in enumerate(groups):
        term = jnp.dot(part.astype(BF16), w_ref[i * gw:(i + 1) * gw, :], preferred_element_type=F32)
        y = term if y is None else y + term
    rms = lambda a, g: a * lax.rsqrt(jnp.mean(a * a, axis=-1, keepdims=True) + EPS) * g
    x = x_ref[...] + mod_ref[0:1] * rms(y, n1_ref[...])
    xo_ref[...] = x
    h_ref[...] = rms(x, n2_ref[...]) * (1.0 + mod_ref[1:2]) + mod_ref[2:3]


def _mixer_out(y_hy, o_hg, o_gla, h_ml, p3, x, w_out, head_gains, gain1, gain2, mod):
    assert HG_DK == GLA_DV == ML_DH and HG_H == GLA_H == ML_H
    bsz, t, d = x.shape
    gw = GROUP_W
    tm = min(ROW_TILE, t)
    part = pl.BlockSpec((None, tm, gw), lambda b, i: (b, i, 0))
    gate = lambda c: pl.BlockSpec((None, tm, gw), lambda b, i: (b, i, c))
    full = pl.BlockSpec((None, tm, d), lambda b, i: (b, i, 0))
    whole = lambda a: pl.BlockSpec(a.shape, lambda b, i: (0,) * a.ndim)
    consts = [w_out.astype(BF16), head_gains, gain1.reshape(1, d), gain2.reshape(1, d)]
    return pl.pallas_call(
        _mixer_out_kernel,
        out_shape=[jax.ShapeDtypeStruct((bsz, t, d), F32), jax.ShapeDtypeStruct((bsz, t, d), F32)],
        grid=(bsz, t // tm),
        in_specs=([part] * 7 + [gate(COL_HG_G), gate(COL_GLA_R), gate(COL_ML_O), full]
                  + [whole(a) for a in consts] + [pl.BlockSpec((None, 3, d), lambda b, i: (b, 0, 0))]),
        out_specs=[full, full],
        compiler_params=pltpu.CompilerParams(dimension_semantics=("arbitrary", "arbitrary"),
                                             vmem_limit_bytes=VMEM_LIMIT),
        name="mixer_output",
    )(y_hy, *o_hg, *o_gla, *h_ml, p3, p3, p3, x, *consts, mod)


def _rms(x, g):
    return x * lax.rsqrt(jnp.mean(x * x, axis=-1, keepdims=True) + EPS) * g


def _short_conv_kernel(prev_ref, cur_ref, next_ref, w_ref, b_ref, o_ref, *, act):
    j = pl.program_id(1)
    u = cur_ref[...]
    tt = u.shape[0]
    row = lax.broadcasted_iota(jnp.int32, u.shape, 0)
    before = jnp.where(j > 0, prev_ref[SUBLANE - 1:SUBLANE, :], 0.0)
    after = jnp.where(j < pl.num_programs(1) - 1, next_ref[0:1, :], 0.0)
    up = jnp.where(row == 0, before, pltpu.roll(u, 1, 0))
    dn = jnp.where(row == tt - 1, after, pltpu.roll(u, tt - 1, 0))
    y = w_ref[0:1] * up + w_ref[1:2] * u + w_ref[2:3] * dn + b_ref[...]
    if act:
        y = y * jax.nn.sigmoid(y)
    o_ref[...] = y


def _short_conv(p3, col0, ncols, w, b, act):
    assert SHORT_CONV == 3
    bsz, t, _ = p3.shape
    tt = min(ROW_TILE, t)
    halo = tt // SUBLANE
    last = t // SUBLANE - 1
    gw = GROUP_W
    cur = pl.BlockSpec((None, tt, gw), lambda bi, j, c: (bi, j, col0 + c))
    prev = pl.BlockSpec((None, SUBLANE, gw), lambda bi, j, c: (bi, jnp.maximum(j * halo - 1, 0), col0 + c))
    nxt = pl.BlockSpec((None, SUBLANE, gw), lambda bi, j, c: (bi, jnp.minimum((j + 1) * halo, last), col0 + c))
    return pl.pallas_call(
        functools.partial(_short_conv_kernel, act=act),
        out_shape=jax.ShapeDtypeStruct((bsz, t, ncols * gw), F32),
        grid=(bsz, t // tt, ncols),
        in_specs=[prev, cur, nxt, pl.BlockSpec((SHORT_CONV, gw), lambda bi, j, c: (0, c)),
                  pl.BlockSpec((1, gw), lambda bi, j, c: (0, c))],
        out_specs=pl.BlockSpec((None, tt, gw), lambda bi, j, c: (bi, j, c)),
        compiler_params=pltpu.CompilerParams(dimension_semantics=("arbitrary",) * 3, vmem_limit_bytes=VMEM_LIMIT),
        name="short_conv",
    )(p3, p3, p3, w, b.reshape(1, ncols * gw))


def _pos_embed_2d(rows, d):
    r = jnp.repeat(jnp.arange(rows, dtype=F32), GRID_W)
    col = (jnp.arange(rows * GRID_W) % GRID_W).astype(F32)
    quarter = d // 4
    omega = 1.0 / (POS_BASE ** (jnp.arange(quarter, dtype=F32) / quarter))

    def axis_emb(p):
        ang = p[:, None] * omega[None, :]
        return jnp.concatenate([jnp.sin(ang), jnp.cos(ang)], axis=-1)

    return jnp.concatenate([axis_emb(r), axis_emb(col)], axis=-1)


def _hyena_spectra(L, w1, b1, w2, b2, w3, freq):
    t = jnp.linspace(0.0, 1.0, L, dtype=F32)[:, None]
    w = 2.0 * math.pi * jnp.arange(L, dtype=F32)[:, None] / L
    bands = jnp.linspace(1e-4, HY_BANDS - 1, HY_BANDS, dtype=F32)[None, :]
    feats = jnp.concatenate([t, jnp.cos(bands * w), -jnp.sin(bands * w)], axis=-1)
    z = jnp.sin(freq[0] * (feats @ w1 + b1))
    z = jnp.sin(freq[1] * (z @ w2 + b2))
    h = (z @ w3).reshape(L, HY_ORDER, 2, HY_W)
    max_decay = math.log(HY_DECAY_TARGET) / HY_FAST_DECAY
    min_decay = math.log(HY_DECAY_TARGET) / HY_SLOW_DECAY
    deltas = jnp.abs(jnp.linspace(min_decay, max_decay, HY_W, dtype=F32))
    h = h * jnp.exp(-t[:, :, None, None] * deltas)
    fwd = h[:, :, 0]
    bwd = h[1:, :, 1][::-1]
    l1 = jnp.sum(jnp.abs(fwd), axis=0) + jnp.sum(jnp.abs(bwd), axis=0)
    filt = jnp.concatenate([fwd, jnp.zeros((1, HY_ORDER, HY_W), F32), bwd], axis=0) / l1
    return jnp.fft.rfft(filt, axis=0)


def _fft_conv(u, spec, bias):
    L = u.shape[1]
    y = jnp.fft.irfft(jnp.fft.rfft(u, n=2 * L, axis=1) * spec, n=2 * L, axis=1)[:, :L]
    return y + u * bias


def _hyena(p3, conv_w, conv_b, w1, b1, w2, b2, w3, freq, bias, norm_g):
    u = _short_conv(p3, COL_HY, 3, conv_w, conv_b, False)
    v, x1, x2 = u[..., :HY_W], u[..., HY_W:2 * HY_W], u[..., 2 * HY_W:]
    spec = _hyena_spectra(u.shape[1], w1, b1, w2, b2, w3, freq)
    z = x1 * _fft_conv(v, spec[:, 0], bias[0])
    y = x2 * _fft_conv(z, spec[:, 1], bias[1])
    return _rms(y, norm_g)


FFT_N1 = 128
FFT_N2 = 128
FFT_KTILE = 8
FFT_NTILE = 4096


def _dft_tables(n1, n2):
    n = n1 * n2
    k = np.arange(n1)
    f1 = np.exp(-2j * np.pi * np.outer(k, k) / n1)
    f2 = np.exp(-2j * np.pi * np.outer(np.arange(n2), np.arange(n2)) / n2)
    tw = np.exp(-2j * np.pi * np.outer(np.arange(n1), np.arange(n2)) / n)
    as32 = lambda a: jnp.asarray(np.ascontiguousarray(a), F32)
    f1_fwd = as32(np.concatenate([f1.real, f1.imag], axis=0))
    f1_inv = as32(np.concatenate([f1.real, f1.imag], axis=1) / n)
    f2_inv = as32(np.block([[f2.real, f2.imag], [-f2.imag, f2.real]]))
    return f1_fwd, f1_inv, as32(f2.real), as32(f2.imag), f2_inv, as32(tw.real), as32(tw.imag)


def _stage_kernel(w_ref, x_ref, o_ref):
    o_ref[...] = jnp.dot(w_ref[...].astype(BF16), x_ref[...].astype(BF16),
                         preferred_element_type=F32).astype(o_ref.dtype)


def _stage_matmul(w, x, out_dtype):
    g, k, n = x.shape
    m = w.shape[0]
    tn = FFT_NTILE
    return pl.pallas_call(
        _stage_kernel,
        out_shape=jax.ShapeDtypeStruct((g, m, n), out_dtype),
        grid=(g, n // tn),
        in_specs=[pl.BlockSpec((m, k), lambda b, j: (0, 0)), pl.BlockSpec((None, k, tn), lambda b, j: (b, 0, j))],
        out_specs=pl.BlockSpec((None, m, tn), lambda b, j: (b, 0, j)),
        compiler_params=pltpu.CompilerParams(dimension_semantics=("arbitrary", "arbitrary"),
                                             vmem_limit_bytes=VMEM_LIMIT),
        name="dft_stage",
    )(w, x)


def _twiddled_f2(f2r, f2i, tr, ti):
    gr = f2r * tr - f2i * ti
    gi = f2r * ti + f2i * tr
    return jnp.concatenate([jnp.concatenate([gr, -gi], axis=1), jnp.concatenate([gi, gr], axis=1)], axis=0)


def _spectrum_kernel(a_ref, f2r_ref, f2i_ref, tr_ref, ti_ref, x_ref):
    n2 = f2r_ref.shape[0]
    tr = tr_ref[...]
    ti = ti_ref[...]
    for i in range(a_ref.shape[0]):
        g = _twiddled_f2(f2r_ref[...], f2i_ref[...], tr[i:i + 1], ti[i:i + 1])
        x_ref[i] = jnp.dot(g.astype(BF16), a_ref[i], preferred_element_type=F32)


def _conv_mid_kernel(a_ref, h_ref, f2r_ref, f2i_ref, f2inv_ref, tr_ref, ti_ref, z_ref):
    n2 = f2r_ref.shape[0]
    tr = tr_ref[...]
    ti = ti_ref[...]
    tr_col = tr.T
    ti_col = ti.T
    f2inv = f2inv_ref[...].astype(BF16)
    for i in range(a_ref.shape[0]):
        g = _twiddled_f2(f2r_ref[...], f2i_ref[...], tr[i:i + 1], ti[i:i + 1])
        x = jnp.dot(g.astype(BF16), a_ref[i], preferred_element_type=F32)
        xr, xi = x[:n2], x[n2:]
        hr, hi = h_ref[i, :n2], h_ref[i, n2:]
        y = jnp.concatenate([hr * xr - hi * xi, hr * xi + hi * xr], axis=0)
        w = jnp.dot(f2inv, y.astype(BF16), preferred_element_type=F32)
        wr, wi = w[:n2], w[n2:]
        cr, ci = tr_col[:, i:i + 1], ti_col[:, i:i + 1]
        z_ref[i] = jnp.concatenate([cr * wr + ci * wi, cr * wi - ci * wr], axis=0).astype(z_ref.dtype)


def _dft_mid_specs(g, c):
    n1, n2, kt = FFT_N1, FFT_N2, FFT_KTILE
    blk = pl.BlockSpec((None, kt, 2 * n2, c), lambda b, j: (b, j, 0, 0))
    const = lambda r, cc: pl.BlockSpec((r, cc), lambda b, j: (0, 0))
    twid = pl.BlockSpec((kt, n2), lambda b, j: (j, 0))
    params = pltpu.CompilerParams(dimension_semantics=("arbitrary", "arbitrary"), vmem_limit_bytes=VMEM_LIMIT)
    return blk, const, twid, params, (g, n1 // kt)


def _to_k1_major(a2d, c):
    g = a2d.shape[0]
    return a2d.reshape(g, 2, FFT_N1, FFT_N2, c).transpose(0, 2, 1, 3, 4).reshape(g, FFT_N1, 2 * FFT_N2, c)


def _filter_spectrum(filt, tables):
    g, n, c = filt.shape
    f1_fwd, _, f2r, f2i, _, twr, twi = tables
    a = _stage_matmul(f1_fwd, filt.reshape(g, FFT_N1, FFT_N2 * c), BF16)
    blk, const, twid, params, grid = _dft_mid_specs(g, c)
    return pl.pallas_call(
        _spectrum_kernel,
        out_shape=jax.ShapeDtypeStruct((g, FFT_N1, 2 * FFT_N2, c), F32),
        grid=grid,
        in_specs=[blk, const(FFT_N2, FFT_N2), const(FFT_N2, FFT_N2), twid, twid],
        out_specs=blk, compiler_params=params, name="dft_spectrum",
    )(_to_k1_major(a, c), f2r, f2i, twr, twi)


def _long_conv(u, spec, tables):
    g, l, c = u.shape
    f1_fwd, f1_inv, f2r, f2i, f2inv, twr, twi = tables
    half = l // FFT_N2
    a = _stage_matmul(f1_fwd[:, :half], u.reshape(g, half, FFT_N2 * c), BF16)
    blk, const, twid, params, grid = _dft_mid_specs(g, c)
    hspec = pl.BlockSpec((FFT_KTILE, 2 * FFT_N2, c), lambda b, j: (j, 0, 0))
    z = pl.pallas_call(
        _conv_mid_kernel,
        out_shape=jax.ShapeDtypeStruct((g, FFT_N1, 2 * FFT_N2, c), BF16),
        grid=grid,
        in_specs=[blk, hspec, const(FFT_N2, FFT_N2), const(FFT_N2, FFT_N2), const(2 * FFT_N2, 2 * FFT_N2),
                  twid, twid],
        out_specs=blk, compiler_params=params, name="dft_conv_mid",
    )(_to_k1_major(a, c), spec, f2r, f2i, f2inv, twr, twi)
    z2d = z.reshape(g, FFT_N1, 2, FFT_N2, c).transpose(0, 2, 1, 3, 4).reshape(g, 2 * FFT_N1, FFT_N2 * c)
    y = _stage_matmul(f1_inv[:half], z2d, F32)
    return y.reshape(g, l, c)


def _hyena_filters(L, w1, b1, w2, b2, w3, freq):
    t = jnp.linspace(0.0, 1.0, L, dtype=F32)[:, None]
    w = 2.0 * math.pi * jnp.arange(L, dtype=F32)[:, None] / L
    bands = jnp.linspace(1e-4, HY_BANDS - 1, HY_BANDS, dtype=F32)[None, :]
    feats = jnp.concatenate([t, jnp.cos(bands * w), -jnp.sin(bands * w)], axis=-1)
    max_decay = math.log(HY_DECAY_TARGET) / HY_FAST_DECAY
    min_decay = math.log(HY_DECAY_TARGET) / HY_SLOW_DECAY
    deltas = jnp.abs(jnp.linspace(min_decay, max_decay, HY_W, dtype=F32))
    w3d = w3.reshape(w3.shape[0], HY_ORDER, 2, HY_W)

    def side(f, tt, direction):
        z = jnp.sin(freq[0] * (f @ w1 + b1))
        z = jnp.sin(freq[1] * (z @ w2 + b2))
        h = (z @ w3d[:, :, direction].reshape(w3.shape[0], HY_ORDER * HY_W)).reshape(-1, HY_ORDER, HY_W)
        return h * jnp.exp(-tt[:, :, None] * deltas)

    fwd = side(feats, t, 0)
    bwd = side(feats[::-1], t[::-1], 1)[:L - 1]
    l1 = jnp.sum(jnp.abs(fwd), axis=0) + jnp.sum(jnp.abs(bwd), axis=0)
    return jnp.concatenate([fwd, jnp.zeros((1, HY_ORDER, HY_W), F32), bwd], axis=0) / l1


def _hyena_long(p3, conv_w, conv_b, w1, b1, w2, b2, w3, freq, bias, norm_g):
    u = _short_conv(p3, COL_HY, 3, conv_w, conv_b, False)
    v, x1, x2 = u[..., :HY_W], u[..., HY_W:2 * HY_W], u[..., 2 * HY_W:]
    L = u.shape[1]
    assert 2 * L == FFT_N1 * FFT_N2
    tables = _dft_tables(FFT_N1, FFT_N2)
    filt = _hyena_filters(L, w1, b1, w2, b2, w3, freq)
    spec = _filter_spectrum(jnp.moveaxis(filt, 1, 0), tables)
    z = x1 * (_long_conv(v, spec[0], tables) + v * bias[0])
    y = x2 * (_long_conv(z, spec[1], tables) + z * bias[1])
    return _rms(y, norm_g)


def _glr_direction(q, k, v, g, st_ref, reverse, nh):
    tc, hk = k.shape
    hv = v.shape[1]
    c = CHUNK_GATED
    hi = lax.Precision.HIGHEST
    ti = lax.broadcasted_iota(jnp.int32, (tc, tc), 0)
    tj = lax.broadcasted_iota(jnp.int32, (tc, tc), 1)
    same = (ti // c) == (tj // c)
    seen = (tj >= ti) if reverse else (tj <= ti)
    bcum = jnp.dot(jnp.where(same, jnp.where(seen, 1.0, 0.0), 0.0), g, precision=hi, preferred_element_type=F32)
    btot = jnp.dot(jnp.where(same, 1.0, 0.0), g, precision=hi, preferred_element_type=F32)
    qd = q * jnp.exp(bcum)
    kd = k * jnp.exp(btot - bcum)
    dec = jnp.exp(btot)
    head_sum = jnp.where(lax.broadcasted_iota(jnp.int32, (hk, hv), 0) // (hk // nh)
                         == lax.broadcasted_iota(jnp.int32, (hk, hv), 1) // (hv // nh), 1.0, 0.0).astype(BF16)
    in_chunk = lax.broadcasted_iota(jnp.int32, (tc, hk), 0) % c
    o = jnp.zeros((tc, hv), F32)
    for lag in range(c):
        if lag == 0:
            ks, bs, vs = k, bcum, v
        else:
            shift = tc - lag if reverse else lag
            ks, bs, vs = pltpu.roll(k, shift, 0), pltpu.roll(bcum, shift, 0), pltpu.roll(v, shift, 0)
        valid = (in_chunk + lag <= c - 1) if reverse else (in_chunk >= lag)
        x = q * ks * jnp.exp(jnp.where(valid, bcum - bs, NEG_INF))
        o = o + jnp.dot(x.astype(BF16), head_sum, preferred_element_type=F32) * vs
    head_mask = (lax.broadcasted_iota(jnp.int32, (hv, hk), 0) // (hv // nh)
                 == lax.broadcasted_iota(jnp.int32, (hv, hk), 1) // (hk // nh))
    st = st_ref[...]
    nch = tc // c
    outs = [None] * nch
    for ci in (range(nch - 1, -1, -1) if reverse else range(nch)):
        sl = slice(ci * c, (ci + 1) * c)
        outs[ci] = lax.dot_general(qd[sl].astype(BF16), st.astype(BF16), (((1,), (1,)), ((), ())),
                                   preferred_element_type=F32)
        ds = lax.dot_general(v[sl].astype(BF16), kd[sl].astype(BF16), (((0,), (0,)), ((), ())),
                             preferred_element_type=F32)
        st = st * dec[ci * c:ci * c + 1] + jnp.where(head_mask, ds, 0.0)
    st_ref[...] = st
    return o + jnp.concatenate(outs, axis=0)


def _log_sigmoid(z):
    return jnp.minimum(z, 0.0) - jnp.log1p(jnp.exp(-jnp.abs(z)))


def _hgrn2_kernel(qf_ref, if_ref, zf_ref, qb_ref, ib_ref, zb_ref, lb_ref, s0f_ref, s0b_ref,
                  of_ref, ob_ref, sf_ref, sb_ref, stf, stb):
    j = pl.program_id(1)

    @pl.when(j == 0)
    def _():
        stf[...] = s0f_ref[...]
        stb[...] = s0b_ref[...]

    one_minus_lb, log_lb, log_ub = lb_ref[0:1], lb_ref[1:2], lb_ref[2:3]

    def gate(z):
        return one_minus_lb * jax.nn.sigmoid(-z), jnp.logaddexp(log_lb, log_ub + _log_sigmoid(z))

    silu = lambda a: a * jax.nn.sigmoid(a)
    k_f, g_f = gate(zf_ref[...])
    k_b, g_b = gate(zb_ref[...])
    of_ref[...] = _glr_direction(silu(qf_ref[...]), k_f, if_ref[...], g_f, stf, False, HG_H)
    ob_ref[...] = _glr_direction(silu(qb_ref[...]), k_b, ib_ref[...], g_b, stb, True, HG_H)

    @pl.when(j == pl.num_programs(1) - 1)
    def _():
        sf_ref[...] = stf[...]
        sb_ref[...] = stb[...]


def _gla_kernel(qkf_ref, vf_ref, nf_ref, qkb_ref, vb_ref, nb_ref, aup_ref, ab_ref, s0f_ref, s0b_ref,
                of_ref, ob_ref, sf_ref, sb_ref, stf, stb):
    j = pl.program_id(1)

    @pl.when(j == 0)
    def _():
        stf[...] = s0f_ref[...]
        stb[...] = s0b_ref[...]

    def gate(narrow, idx):
        a = narrow[:, idx * GLA_RANK:(idx + 1) * GLA_RANK]
        lin = jnp.dot(a.astype(BF16), aup_ref[idx].astype(BF16), preferred_element_type=F32) + ab_ref[idx]
        return _log_sigmoid(lin) / GLA_NORMALIZER

    qk_f = qkf_ref[...]
    qk_b = qkb_ref[...]
    of_ref[...] = _glr_direction(qk_f[:, :GLA_KW] * GLA_DK ** -0.5, qk_f[:, GLA_KW:], vf_ref[...],
                                 gate(nf_ref[...], 0), stf, False, GLA_H)
    ob_ref[...] = _glr_direction(qk_b[:, :GLA_KW] * GLA_DK ** -0.5, qk_b[:, GLA_KW:], vb_ref[...],
                                 gate(nb_ref[...], 1), stb, True, GLA_H)

    @pl.when(j == pl.num_programs(1) - 1)
    def _():
        sf_ref[...] = stf[...]
        sb_ref[...] = stb[...]


def _glr_call(kernel_fn, p3, fwd_cols, bwd_cols, consts, s0_f, s0_b, hk, hv, name):
    bsz, t, _ = p3.shape
    tc = GLR_TILE
    nsb = t // tc
    fwd = lambda w, c: pl.BlockSpec((None, tc, w), lambda b, j: (b, j, c))
    bwd = lambda w, c: pl.BlockSpec((None, tc, w), lambda b, j: (b, nsb - 1 - j, c))
    whole = lambda a: pl.BlockSpec(a.shape, lambda b, j: (0,) * a.ndim)
    st = pl.BlockSpec((None, hv, hk), lambda b, j: (b, 0, 0))
    out_f = pl.BlockSpec((None, tc, hv), lambda b, j: (b, j, 0))
    out_b = pl.BlockSpec((None, tc, hv), lambda b, j: (b, nsb - 1 - j, 0))
    return pl.pallas_call(
        kernel_fn,
        out_shape=[jax.ShapeDtypeStruct((bsz, t, hv), F32), jax.ShapeDtypeStruct((bsz, t, hv), F32),
                   jax.ShapeDtypeStruct((bsz, hv, hk), F32), jax.ShapeDtypeStruct((bsz, hv, hk), F32)],
        grid=(bsz, nsb),
        in_specs=([fwd(w, c) for w, c in fwd_cols] + [bwd(w, c) for w, c in bwd_cols]
                  + [whole(a) for a in consts] + [st, st]),
        out_specs=[out_f, out_b, st, st],
        scratch_shapes=[pltpu.VMEM((hv, hk), F32), pltpu.VMEM((hv, hk), F32)],
        compiler_params=pltpu.CompilerParams(dimension_semantics=("arbitrary", "arbitrary"),
                                             vmem_limit_bytes=VMEM_LIMIT),
        name=name,
    )(*([p3] * (len(fwd_cols) + len(bwd_cols))), *consts, s0_f, s0_b)


def _hgrn2_seq(p3, lb, s0_f, s0_b):
    lb_rows = jnp.stack([1.0 - lb, jnp.log(lb), jnp.log1p(-lb)])
    cols = lambda z: [(GROUP_W, COL_HG_Q), (GROUP_W, COL_HG_I), (GROUP_W, z)]
    o_f, o_b, s_f, s_b = _glr_call(_hgrn2_kernel, p3, cols(COL_HG_ZF), cols(COL_HG_ZB), [lb_rows],
                                   s0_f, s0_b, HG_W, HG_W, "hgrn2_recurrence")
    return (o_f, o_b), s_f, s_b


def _hgrn2(pc3, pl3, lb):
    s0 = jnp.zeros((pl3.shape[0], HG_W, HG_W), F32)
    oc, s_f, s_b = _hgrn2_seq(pc3, lb, s0, s0)
    o, _, _ = _hgrn2_seq(pl3, lb, s_f, s_b)
    return oc, o


def _gla_seq(p3, a_up, a_b, s0_f, s0_b):
    cols = [(GROUP_W, COL_GLA_QK), (GROUP_W, COL_GLA_V), (LANE, COL_NARROW)]
    o_f, o_b, s_f, s_b = _glr_call(_gla_kernel, p3, cols, cols, [a_up, a_b.reshape(2, 1, GLA_KW)],
                                   s0_f, s0_b, GLA_KW, GLA_VW, "gla_recurrence")
    return (o_f, o_b), s_f, s_b


def _gla(pc3, pl3, a_up, a_b):
    s0 = jnp.zeros((pl3.shape[0], GLA_VW, GLA_KW), F32)
    oc, s_f, s_b = _gla_seq(pc3, a_up, a_b, s0, s0)
    o, _, _ = _gla_seq(pl3, a_up, a_b, s_f, s_b)
    return oc, o


def _mlstm_direction(q, k, v, igx, lfx, s_ref, n_ref, m_ref, reverse, nh):
    tc, w = q.shape
    seg = w // nh
    assert tc == seg
    hi = lax.Precision.HIGHEST
    ti = lax.broadcasted_iota(jnp.int32, (tc, tc), 0)
    tj = lax.broadcasted_iota(jnp.int32, (tc, tc), 1)
    seen = (tj >= ti) if reverse else (tj <= ti)
    b = jnp.dot(jnp.where(seen, 1.0, 0.0), lfx, precision=hi, preferred_element_type=F32)
    bl = b[0:1] if reverse else b[tc - 1:tc]
    a = bl - b + igx
    ma = jnp.max(a, axis=0, keepdims=True)
    kw = jnp.exp(a - ma) * k
    s_prev = s_ref[...]
    n_prev = n_ref[...]
    m_prev = m_ref[...]
    lane = lax.broadcasted_iota(jnp.int32, (tc, w), 1)
    row = lax.broadcasted_iota(jnp.int32, (tc, w), 0)
    same_head = (lax.broadcasted_iota(jnp.int32, (w, w), 0) // seg
                 == lax.broadcasted_iota(jnp.int32, (w, w), 1) // seg)
    kexp = jnp.where(same_head, jnp.concatenate([k] * nh, axis=0), 0.0)
    vexp = jnp.where(same_head, jnp.concatenate([v] * nh, axis=0), 0.0)
    scores = lax.dot_general(q.astype(BF16), kexp.astype(BF16), (((1,), (1,)), ((), ())),
                             preferred_element_type=F32)
    s_lane = lane % seg
    by_src = jnp.sum(jnp.where(s_lane == row, igx - b, 0.0), axis=0, keepdims=True)
    ok = (s_lane >= row) if reverse else (s_lane <= row)
    dmat = jnp.where(ok, b + by_src, NEG_INF)
    inter = b + m_prev
    head_of_lane = lane // seg
    seg_max = jnp.full((tc, w), NEG_INF, F32)
    for h in range(nh):
        in_h = head_of_lane == h
        seg_max = jnp.where(in_h, jnp.max(jnp.where(in_h, dmat, NEG_INF), axis=1, keepdims=True), seg_max)
    m_t = jnp.maximum(inter, seg_max)
    wq = jnp.exp(dmat - m_t) * scores
    w_int = jnp.exp(inter - m_t)
    head_sum = jnp.where(same_head, 1.0, 0.0).astype(BF16)
    num = (jnp.dot(wq.astype(BF16), vexp.astype(BF16), preferred_element_type=F32)
           + w_int * jnp.dot(q.astype(BF16), s_prev.astype(BF16), preferred_element_type=F32))
    den = (jnp.dot(wq.astype(BF16), head_sum, preferred_element_type=F32)
           + w_int * jnp.dot((q * n_prev).astype(BF16), head_sum, preferred_element_type=F32))
    h_out = num / jnp.maximum(jnp.abs(den), jnp.exp(-m_t))
    m_new = jnp.maximum(bl + m_prev, ma)
    d_old = jnp.exp(bl + m_prev - m_new)
    d_new = jnp.exp(ma - m_new)
    ds = lax.dot_general(kw.astype(BF16), v.astype(BF16), (((0,), (0,)), ((), ())), preferred_element_type=F32)
    s_ref[...] = d_old * s_prev + d_new * jnp.where(same_head, ds, 0.0)
    n_ref[...] = d_old * n_prev + d_new * jnp.sum(kw, axis=0, keepdims=True)
    m_ref[...] = m_new
    return h_out


def _mlstm_kernel(qf_ref, kf_ref, vf_ref, igf_ref, lff_ref, qb_ref, kb_ref, vb_ref, igb_ref, lfb_ref,
                  s0f_ref, n0f_ref, m0f_ref, s0b_ref, n0b_ref, m0b_ref,
                  hf_ref, hb_ref, sf_ref, nf_ref, mf_ref, sb_ref, nb_ref, mb_ref,
                  s_f, n_f, m_f, s_b, n_b, m_b, *, nh):
    j = pl.program_id(1)

    @pl.when(j == 0)
    def _():
        s_f[...] = s0f_ref[...]
        n_f[...] = n0f_ref[...]
        m_f[...] = m0f_ref[...]
        s_b[...] = s0b_ref[...]
        n_b[...] = n0b_ref[...]
        m_b[...] = m0b_ref[...]

    hf_ref[...] = _mlstm_direction(qf_ref[...], kf_ref[...], vf_ref[...], igf_ref[...], lff_ref[...],
                                   s_f, n_f, m_f, False, nh)
    hb_ref[...] = _mlstm_direction(qb_ref[...], kb_ref[...], vb_ref[...], igb_ref[...], lfb_ref[...],
                                   s_b, n_b, m_b, True, nh)

    @pl.when(j == pl.num_programs(1) - 1)
    def _():
        sf_ref[...] = s_f[...]
        nf_ref[...] = n_f[...]
        mf_ref[...] = m_f[...]
        sb_ref[...] = s_b[...]
        nb_ref[...] = n_b[...]
        mb_ref[...] = m_b[...]


def _mlstm_bidir(q, k, v, ig_f, lf_f, ig_b, lf_b, st_f, st_b, nh):
    bsz, t, w = q.shape
    tc = CHUNK_ML
    nsb = t // tc
    fwd = pl.BlockSpec((None, tc, w), lambda b, j: (b, j, 0))
    bwd = pl.BlockSpec((None, tc, w), lambda b, j: (b, nsb - 1 - j, 0))
    mat = pl.BlockSpec((None, w, w), lambda b, j: (b, 0, 0))
    vec = pl.BlockSpec((None, 1, w), lambda b, j: (b, 0, 0))
    sds = jax.ShapeDtypeStruct
    state_shapes = [sds((bsz, w, w), F32), sds((bsz, 1, w), F32), sds((bsz, 1, w), F32)]
    outs = pl.pallas_call(
        functools.partial(_mlstm_kernel, nh=nh),
        out_shape=[sds((bsz, t, w), F32), sds((bsz, t, w), F32)] + state_shapes + state_shapes,
        grid=(bsz, nsb),
        in_specs=[fwd] * 5 + [bwd] * 5 + [mat, vec, vec] * 2,
        out_specs=[fwd, bwd] + [mat, vec, vec] * 2,
        scratch_shapes=[pltpu.VMEM((w, w), F32), pltpu.VMEM((1, w), F32), pltpu.VMEM((1, w), F32)] * 2,
        compiler_params=pltpu.CompilerParams(dimension_semantics=("arbitrary", "arbitrary"),
                                             vmem_limit_bytes=VMEM_LIMIT),
        name="mlstm_recurrence",
    )(q, k, v, ig_f, lf_f, q, k, v, ig_b, lf_b, *st_f, *st_b)
    return outs[0], outs[1], tuple(outs[2:5]), tuple(outs[5:8])


def _mlstm_seq(p3, conv_w, conv_b, gate_b, st_f, st_b):
    qk = _short_conv(p3, COL_ML_Q, 2, conv_w, conv_b, True)
    v = p3[..., COL_ML_V * GROUP_W:(COL_ML_V + 1) * GROUP_W]
    bsz, t, _ = p3.shape
    gates = p3[..., NARROW_ML_GATES:NARROW_ML_GATES + 4 * ML_H]
    gt = gates.reshape(bsz, t, 4, ML_H) + gate_b
    expand = lambda a: jnp.repeat(a, ML_DH, axis=-1)
    h_f, h_b, fin_f, fin_b = _mlstm_bidir(
        qk[..., :ML_W], qk[..., ML_W:] * ML_DH ** -0.5, v,
        expand(gt[:, :, 0]), expand(jax.nn.log_sigmoid(gt[:, :, 1])),
        expand(gt[:, :, 2]), expand(jax.nn.log_sigmoid(gt[:, :, 3])), st_f, st_b, ML_H)
    return (h_f, h_b), fin_f, fin_b


def _mlstm(pc3, pl3, conv_w, conv_b, gate_b):
    bsz = pl3.shape[0]
    st0 = (jnp.zeros((bsz, ML_W, ML_W), F32), jnp.zeros((bsz, 1, ML_W), F32), jnp.zeros((bsz, 1, ML_W), F32))
    hc, st_f, st_b = _mlstm_seq(pc3, conv_w, conv_b, gate_b, st0, st0)
    h, _, _ = _mlstm_seq(pl3, conv_w, conv_b, gate_b, st_f, st_b)
    return hc, h


def _router_kernel(h_ref, wt_ref, b_ref, eidx_ref, wsel_ref, cnt_ref):
    i = pl.program_id(0)
    tm = h_ref.shape[0]
    ne = wt_ref.shape[0]
    per_group = ne // N_EXPERT_GROUPS
    logits = lax.dot_general(wt_ref[...], h_ref[...], (((1,), (1,)), ((), ())),
                             preferred_element_type=F32, precision=lax.Precision.HIGHEST)
    s = jax.nn.sigmoid(logits)
    sel = s + b_ref[...]
    row = lax.broadcasted_iota(jnp.int32, (ne, tm), 0)
    gs = []
    for g in range(N_EXPERT_GROUPS):
        blk = sel[g * per_group:(g + 1) * per_group]
        r = lax.broadcasted_iota(jnp.int32, blk.shape, 0)
        m1 = jnp.max(blk, axis=0, keepdims=True)
        i1 = jnp.min(jnp.where(blk == m1, r, per_group), axis=0, keepdims=True)
        m2 = jnp.max(jnp.where(r == i1, NEG_INF, blk), axis=0, keepdims=True)
        gs.append(m1 + m2)
    grp = jnp.concatenate(gs, axis=0)
    grow = lax.broadcasted_iota(jnp.int32, grp.shape, 0)
    gsel = jnp.zeros(grp.shape, F32)
    for _ in range(TOPK_GROUPS):
        m = jnp.max(grp, axis=0, keepdims=True)
        gi = jnp.min(jnp.where(grp == m, grow, N_EXPERT_GROUPS), axis=0, keepdims=True)
        hit = grow == gi
        gsel = jnp.where(hit, 1.0, gsel)
        grp = jnp.where(hit, NEG_INF, grp)
    masked = jnp.concatenate(
        [jnp.where(gsel[g:g + 1] > 0.0, sel[g * per_group:(g + 1) * per_group], NEG_INF)
         for g in range(N_EXPERT_GROUPS)], axis=0)
    eis, ws = [], []
    picked = jnp.zeros((ne, tm), F32)
    for _ in range(TOP_K):
        m = jnp.max(masked, axis=0, keepdims=True)
        ei = jnp.min(jnp.where(masked == m, row, ne), axis=0, keepdims=True)
        hit = row == ei
        ws.append(jnp.sum(jnp.where(hit, s, 0.0), axis=0, keepdims=True))
        eis.append(ei)
        picked = jnp.where(hit, 1.0, picked)
        masked = jnp.where(hit, NEG_INF, masked)
    w = jnp.concatenate(ws, axis=0)
    eidx_ref[...] = jnp.concatenate(eis, axis=0)
    wsel_ref[...] = w / jnp.sum(w, axis=0, keepdims=True) * ROUTED_SCALE
    tot = jnp.dot(picked.astype(BF16), jnp.ones((tm, LANE), BF16), preferred_element_type=F32)

    @pl.when(i == 0)
    def _():
        cnt_ref[...] = jnp.zeros_like(cnt_ref)

    cnt_ref[...] += tot


def _pos_kernel(eidx_ref, base_ref, pos_ref, carry_ref):
    i = pl.program_id(0)
    tm = eidx_ref.shape[1]
    ne = base_ref.shape[0]

    @pl.when(i == 0)
    def _():
        carry_ref[...] = jnp.zeros_like(carry_ref)

    eidx = eidx_ref[...]
    row = lax.broadcasted_iota(jnp.int32, (ne, tm), 0)
    picked = jnp.zeros((ne, tm), F32)
    for k in range(TOP_K):
        picked = jnp.where(row == eidx[k:k + 1], 1.0, picked)
    pb = picked.astype(BF16)
    before = jnp.where(lax.broadcasted_iota(jnp.int32, (tm, tm), 0) < lax.broadcasted_iota(jnp.int32, (tm, tm), 1),
                       1.0, 0.0).astype(BF16)
    rank = jnp.dot(pb, before, preferred_element_type=F32)
    tot = jnp.dot(pb, jnp.ones((tm, LANE), BF16), preferred_element_type=F32)
    dest = rank + (base_ref[...] + carry_ref[:, 0:1])
    pos = [jnp.sum(jnp.where(row == eidx[k:k + 1], dest, 0.0), axis=0, keepdims=True) for k in range(TOP_K)]
    pos_ref[...] = jnp.concatenate(pos, axis=0).astype(jnp.int32)
    carry_ref[...] += tot


def _pack_bf16_pairs(x):
    half = x.shape[1] // 2
    bits = lambda a: pltpu.bitcast(a.astype(BF16).astype(F32), jnp.int32)
    return (bits(x[:, half:]) & -65536) | lax.shift_right_logical(bits(x[:, :half]), 16)


def _unpack_bf16_pairs(w):
    return pltpu.bitcast(lax.shift_left(w, 16), F32), pltpu.bitcast(w & -65536, F32)


def _dispatch_kernel(zstart_ref, zlen_ref, nused_ref, pos_ref, h_ref, xs_ref, packed, zeros, sem, zsem, *,
                     n_blocks):
    tm = h_ref.shape[0]
    bm = zeros.shape[0]
    packed[...] = _pack_bf16_pairs(h_ref[...])

    @pl.when(pl.program_id(0) == 0)
    def _():
        zeros[...] = jnp.zeros_like(zeros)

        def zero_copy(start, size):
            return pltpu.make_async_copy(zeros.at[pl.ds(0, size)], xs_ref.at[pl.ds(start, size)], zsem)

        def pieces(e, act):
            start = zstart_ref[e]
            rem = zlen_ref[e]
            ragged = rem & (SUBLANE - 1)
            for q in range(SUBLANE - 1):
                @pl.when(q < ragged)
                def _(q=q):
                    act(zero_copy(start + q, 1))

            start = pl.multiple_of(start + ragged, SUBLANE)
            size = bm // 2
            while size >= SUBLANE:
                @pl.when((rem & size) != 0)
                def _(start=start, size=size):
                    act(zero_copy(start, size))

                start = pl.multiple_of(start + (rem & size), SUBLANE)
                size //= 2

        def loop(act):
            def per_expert(e, carry):
                pieces(e, act)
                return carry

            def per_block(b, carry):
                act(zero_copy(pl.multiple_of(b * bm, bm), bm))
                return carry

            lax.fori_loop(0, zstart_ref.shape[0], per_expert, 0)
            lax.fori_loop(nused_ref[0], n_blocks, per_block, 0)

        loop(lambda cp: cp.start())
        loop(lambda cp: cp.wait())

    def row_copy(n, k):
        return pltpu.make_async_copy(packed.at[pl.ds(n, 1)], xs_ref.at[pl.ds(pos_ref[n * TOP_K + k], 1)], sem)

    def issue(n, carry):
        for k in range(TOP_K):
            row_copy(n, k).start()
        return carry

    lax.fori_loop(0, tm, issue, 0)
    for _ in range(TOP_K):
        pltpu.make_async_copy(packed, xs_ref.at[pl.ds(0, tm)], sem).wait()


def _moe_ffn_kernel(blk_e_ref, run_ref, next_e_ref, nused_ref, x_ref, wgu_hbm, wdn_hbm, o_ref,
                    wgu_f, wdn_f, wgu_s, wdn_s, sems, *, layer):
    i = pl.program_id(0)
    e = blk_e_ref[i]
    run = run_ref[i]
    first_of_run = (i == 0) | (run != run_ref[jnp.maximum(i - 1, 0)])

    def fetch(expert, slot):
        return (pltpu.make_async_copy(wgu_hbm.at[layer, expert], wgu_f.at[slot], sems.at[0, slot]),
                pltpu.make_async_copy(wdn_hbm.at[layer, expert], wdn_f.at[slot], sems.at[1, slot]))

    @pl.when(i < nused_ref[0])
    def _():
        @pl.when(first_of_run)
        def _():
            slot = run & 1

            @pl.when(i == 0)
            def _():
                for cp in fetch(e, 0):
                    cp.start()

            for cp in fetch(e, slot):
                cp.wait()
            wgu_s[...] = wgu_f[slot].astype(BF16)
            wdn_s[...] = wdn_f[slot].astype(BF16)

            @pl.when(next_e_ref[i] >= 0)
            def _():
                for cp in fetch(next_e_ref[i], 1 - slot):
                    cp.start()

        x_lo, x_hi = _unpack_bf16_pairs(x_ref[...])
        half = x_lo.shape[1]
        au = (jnp.dot(x_lo.astype(BF16), wgu_s[:half, :], preferred_element_type=F32)
              + jnp.dot(x_hi.astype(BF16), wgu_s[half:, :], preferred_element_type=F32))
        a = au[:, :EXPERT_FF]
        u = au[:, EXPERT_FF:]
        h = (a * jax.nn.sigmoid(a)) * u
        o_ref[...] = _pack_bf16_pairs(jnp.dot(h.astype(BF16), wdn_s[...], preferred_element_type=F32))

    @pl.when(i >= nused_ref[0])
    def _():
        o_ref[...] = jnp.zeros_like(o_ref)


def _combine_kernel(pos_ref, w_ref, t_ref, x_ref, sgu_ref, sdn_ref, gain_ref, g2_ref, y_hbm, o_ref, buf, sem):
    tm = o_ref.shape[0]

    def row_copy(n, k):
        return pltpu.make_async_copy(y_hbm.at[pl.ds(pos_ref[n * TOP_K + k], 1)], buf.at[k, pl.ds(n, 1)], sem)

    def issue(n, carry):
        for k in range(TOP_K):
            row_copy(n, k).start()
        return carry

    lax.fori_loop(0, tm, issue, 0)
    ff = sdn_ref.shape[0]
    au = jnp.dot(t_ref[...].astype(BF16), sgu_ref[...], preferred_element_type=F32)
    a, u = au[:, :ff], au[:, ff:]
    f = jnp.dot(((a * jax.nn.sigmoid(a)) * u).astype(BF16), sdn_ref[...], preferred_element_type=F32)
    for k in range(TOP_K):
        pltpu.make_async_copy(y_hbm.at[pl.ds(0, tm)], buf.at[k], sem).wait()
    r_lo, r_hi = None, None
    for k in range(TOP_K):
        y_lo, y_hi = _unpack_bf16_pairs(buf[k])
        wk = w_ref[:, k:k + 1]
        r_lo = y_lo * wk if r_lo is None else r_lo + y_lo * wk
        r_hi = y_hi * wk if r_hi is None else r_hi + y_hi * wk
    f = f + jnp.concatenate([r_lo, r_hi], axis=1)
    o_ref[...] = x_ref[...] + g2_ref[...] * (f * lax.rsqrt(jnp.mean(f * f, axis=-1, keepdims=True) + EPS)
                                             * gain_ref[...])


def _moe(t, x_res, router_w, router_b, w_gu, w_down, sh_gu, sh_down, layer, gain, g2_rows, rows_per_gate):
    n, d = t.shape
    ne = router_w.shape[1]
    ff2 = w_gu.shape[-1]
    params = pltpu.CompilerParams(dimension_semantics=("arbitrary",), vmem_limit_bytes=VMEM_LIMIT)
    tm = ROUTER_TILE
    eidx, wsel, cnt = pl.pallas_call(
        _router_kernel,
        out_shape=[jax.ShapeDtypeStruct((TOP_K, n), jnp.int32), jax.ShapeDtypeStruct((TOP_K, n), F32),
                   jax.ShapeDtypeStruct((ne, LANE), F32)],
        grid=(n // tm,),
        in_specs=[pl.BlockSpec((tm, d), lambda i: (i, 0)), pl.BlockSpec((ne, d), lambda i: (0, 0)),
                  pl.BlockSpec((ne, 1), lambda i: (0, 0))],
        out_specs=[pl.BlockSpec((TOP_K, tm), lambda i: (0, i)), pl.BlockSpec((TOP_K, tm), lambda i: (0, i)),
                   pl.BlockSpec((ne, LANE), lambda i: (0, 0))],
        compiler_params=params, name="moe_router",
    )(t, router_w.T, router_b.reshape(ne, 1))
    bm = MOE_ROWS
    counts = cnt[:, 0].astype(jnp.int32)
    padded = (counts + bm - 1) // bm * bm
    pad_end = jnp.cumsum(padded)
    pad_start = pad_end - padded
    n_blocks = (n * TOP_K + ne * (bm - 1)) // bm + 1
    blk_first = jnp.arange(n_blocks, dtype=jnp.int32) * bm
    blk_e = jnp.minimum(jnp.sum((pad_end[None, :] <= blk_first[:, None]).astype(jnp.int32), axis=1), ne - 1)
    n_used = (pad_end[-1] // bm).astype(jnp.int32).reshape(1)
    pos = pl.pallas_call(
        _pos_kernel,
        out_shape=jax.ShapeDtypeStruct((TOP_K, n), jnp.int32),
        grid=(n // tm,),
        in_specs=[pl.BlockSpec((TOP_K, tm), lambda i: (0, i)), pl.BlockSpec((ne, 1), lambda i: (0, 0))],
        out_specs=pl.BlockSpec((TOP_K, tm), lambda i: (0, i)),
        scratch_shapes=[pltpu.VMEM((ne, LANE), F32)],
        compiler_params=params, name="moe_positions",
    )(eidx, pad_start.astype(F32).reshape(ne, 1))
    ts = SCATTER_TILE
    p = n_blocks * bm
    pos_tok = pos.T.reshape(n * TOP_K)
    pos_spec = pl.BlockSpec((ts * TOP_K,), lambda i: (i,), memory_space=pltpu.SMEM)
    dp = d // 2
    xs = pl.pallas_call(
        functools.partial(_dispatch_kernel, n_blocks=n_blocks),
        out_shape=jax.ShapeDtypeStruct((p, dp), jnp.int32),
        grid_spec=pltpu.PrefetchScalarGridSpec(
            num_scalar_prefetch=3,
            grid=(n // ts,),
            in_specs=[pl.BlockSpec((ts * TOP_K,), lambda i, *_: (i,), memory_space=pltpu.SMEM),
                      pl.BlockSpec((ts, d), lambda i, *_: (i, 0))],
            out_specs=pl.BlockSpec(memory_space=pl.ANY),
            scratch_shapes=[pltpu.VMEM((ts, dp), jnp.int32), pltpu.VMEM((bm, dp), jnp.int32),
                            pltpu.SemaphoreType.DMA, pltpu.SemaphoreType.DMA],
        ),
        compiler_params=params, name="moe_dispatch",
    )(pad_start + counts, padded - counts, n_used, pos_tok, t)

    is_start = jnp.concatenate([jnp.ones((1,), bool), blk_e[1:] != blk_e[:-1]])
    run_id = jnp.cumsum(is_start.astype(jnp.int32)) - 1
    ids = jnp.arange(ne, dtype=jnp.int32)
    later_busy = jnp.where((counts[None, :] > 0) & (ids[None, :] > ids[:, None]), ids[None, :], ne)
    next_busy = jnp.min(later_busy, axis=1)
    next_e = jnp.sum(jnp.where(blk_e[:, None] == ids[None, :], next_busy[None, :], 0), axis=1)
    next_e = jnp.where(next_e >= ne, -1, next_e).astype(jnp.int32)

    def x_map(i, blk_e, run, nxt, nused):
        return (jnp.minimum(i, nused[0] - 1), 0)

    y_p = pl.pallas_call(
        functools.partial(_moe_ffn_kernel, layer=layer),
        out_shape=jax.ShapeDtypeStruct((p, dp), jnp.int32),
        grid_spec=pltpu.PrefetchScalarGridSpec(
            num_scalar_prefetch=4,
            grid=(n_blocks,),
            in_specs=[pl.BlockSpec((bm, dp), x_map),
                      pl.BlockSpec(memory_space=pl.ANY), pl.BlockSpec(memory_space=pl.ANY)],
            out_specs=pl.BlockSpec((bm, dp), lambda i, *_: (i, 0)),
            scratch_shapes=[pltpu.VMEM((2, d, ff2), F32), pltpu.VMEM((2, ff2 // 2, d), F32),
                            pltpu.VMEM((d, ff2), BF16), pltpu.VMEM((ff2 // 2, d), BF16),
                            pltpu.SemaphoreType.DMA((2, 2))],
        ),
        compiler_params=params, name="moe_expert_ffn",
    )(blk_e, run_id, next_e, n_used, xs, w_gu, w_down)
    rows = pl.BlockSpec((ts, d), lambda i: (i, 0))
    whole = lambda a: pl.BlockSpec(a.shape, lambda i: (0,) * a.ndim)
    last_gate = g2_rows.shape[0] - 1
    gate_spec = pl.BlockSpec((None, 1, d), lambda i: (jnp.minimum(i // (rows_per_gate // ts), last_gate), 0, 0))
    consts = [sh_gu.astype(BF16), sh_down.astype(BF16), gain.reshape(1, d)]
    return pl.pallas_call(
        _combine_kernel,
        out_shape=jax.ShapeDtypeStruct((n, d), F32),
        grid=(n // ts,),
        in_specs=([pos_spec, pl.BlockSpec((ts, TOP_K), lambda i: (i, 0)), rows, rows]
                  + [whole(a) for a in consts] + [gate_spec, pl.BlockSpec(memory_space=pl.ANY)]),
        out_specs=rows,
        scratch_shapes=[pltpu.VMEM((TOP_K, ts, dp), jnp.int32), pltpu.SemaphoreType.DMA],
        compiler_params=params, name="moe_combine",
    )(pos_tok, wsel.T, t, x_res, *consts, g2_rows, y_p)


def kernel(x, c, ctx, c_ctx, ada_w, ada_b, norm_g, w_in, w_out, hy_conv_w, hy_conv_b, hy_ffn_w1, hy_ffn_b1, hy_ffn_w2, hy_ffn_b2, hy_ffn_w3, hy_freq, hy_bias, hy_norm, hg_lb_logits, hg_norm, gla_a_up, gla_a_b, gla_norm, ml_conv_w, ml_conv_b, ml_gate_b, ml_norm, router_w, router_b, exp_w_gu, exp_w_down, sh_w_gu, sh_w_down):
    bsz, seq, d = x.shape
    n_ctx = ctx.shape[1]
    depth = ada_w.shape[0]
    rows = seq // GRID_W
    x = x + _pos_embed_2d(rows, d)[None]
    xc = ctx
    lb_cum = jnp.cumsum(jax.nn.softmax(hg_lb_logits, axis=0), axis=0)
    lower_bounds = lb_cum - lb_cum[0:1]
    for l in range(depth):
        with_ctx = l < depth - 1
        mod = (jax.nn.silu(c) @ ada_w[l] + ada_b[l])[:, None, :]
        mod_c = jax.nn.silu(c_ctx) @ ada_w[l] + ada_b[l]
        sh1, sc1, g1, sh2, sc2, g2 = jnp.split(mod, 6, axis=-1)
        csh1, csc1, cg1, csh2, csc2, cg2 = jnp.split(mod_c, 6, axis=-1)
        w_in_l = _arrange_w_in(w_in[l])
        ctx_rows = lambda a: jnp.broadcast_to(a.reshape(1, 1, d), (bsz, 1, d))
        pl3 = _in_proj(x, norm_g[l, 0], sc1, sh1, w_in_l)
        pc3 = _in_proj(xc, norm_g[l, 0], ctx_rows(csc1), ctx_rows(csh1), w_in_l)
        hy_args = (hy_conv_w[l], hy_conv_b[l], hy_ffn_w1[l], hy_ffn_b1[l], hy_ffn_w2[l], hy_ffn_b2[l],
                   hy_ffn_w3[l], hy_freq[l], hy_bias[l], hy_norm[l])
        y_hy = _hyena_long(pl3, *hy_args)
        oc_hg, o_hg = _hgrn2(pc3, pl3, lower_bounds[l])
        oc_gla, o_gla = _gla(pc3, pl3, gla_a_up[l], gla_a_b[l])
        hc_ml, h_ml = _mlstm(pc3, pl3, ml_conv_w[l], ml_conv_b[l], ml_gate_b[l])
        head_gains = jnp.stack([hg_norm[l], gla_norm[l], ml_norm[l]])
        out_args = (w_out[l], head_gains, norm_g[l, 1], norm_g[l, 2])
        x, h = _mixer_out(y_hy, o_hg, o_gla, h_ml, pl3, x, *out_args, jnp.concatenate([g1, sc2, sh2], axis=1))
        moe_args = (router_w[l], router_b[l], exp_w_gu, exp_w_down, sh_w_gu[l], sh_w_down[l], l, norm_g[l, 3])
        if with_ctx:
            mod_ctx = jnp.broadcast_to(jnp.stack([cg1, csc2, csh2])[None], (bsz, 3, d))
            xc, hc = _mixer_out(_hyena(pc3, *hy_args), oc_hg, oc_gla, hc_ml, pc3, xc, *out_args, mod_ctx)
            tokens = lambda a, ac: jnp.concatenate([a.reshape(bsz * seq, d), ac.reshape(bsz * n_ctx, d)], axis=0)
            gates = jnp.concatenate([g2, cg2.reshape(1, 1, d)], axis=0)
            x_all = _moe(tokens(h, hc), tokens(x, xc), *moe_args, gates, seq)
            x = x_all[:bsz * seq].reshape(bsz, seq, d)
            xc = x_all[bsz * seq:].reshape(bsz, n_ctx, d)
        else:
            x = _moe(h.reshape(bsz * seq, d), x.reshape(bsz * seq, d), *moe_args, g2, seq).reshape(bsz, seq, d)
    return x
```

```python
import functools
import math

import jax
import jax.numpy as jnp
import numpy as np
from jax import lax
from jax.experimental import pallas as pl
from jax.experimental.pallas import tpu as pltpu

F32 = jnp.float32
BF16 = jnp.bfloat16

D_MODEL = 1024
GRID_W = 64
EPS = 1e-6
POS_BASE = 10000.0
GROUP_W = D_MODEL // 4
SHORT_CONV = 3
HY_W = GROUP_W
HY_ORDER = 2
HY_EMB = 33
HY_BANDS = (HY_EMB - 1) // 2
HY_FAST_DECAY = 0.3
HY_SLOW_DECAY = 1.5
HY_DECAY_TARGET = 1e-2
HG_H = 4
HG_W = GROUP_W
HG_DK = HG_W // HG_H
GLA_H = 4
GLA_KW = GROUP_W // 2
GLA_VW = GROUP_W
GLA_DK = GLA_KW // GLA_H
GLA_DV = GLA_VW // GLA_H
GLA_RANK = 16
GLA_NORMALIZER = 16.0
ML_H = 4
ML_W = GROUP_W
ML_DH = ML_W // ML_H
CHUNK_GATED = 16
CHUNK_ML = 64
TOP_K = 8
N_EXPERT_GROUPS = 8
TOPK_GROUPS = 4
EXPERT_FF = 256
ROUTED_SCALE = 2.5
IN_SPLITS = (HY_W, HY_W, HY_W,
             HG_W, HG_W, HG_W, HG_W, HG_W,
             GLA_KW, GLA_KW, GLA_VW, GLA_RANK, GLA_RANK, GLA_VW,
             ML_W, ML_W, ML_W, 4 * ML_H, ML_W)
P_ORDER = (0, 1, 2, 3, 4, 5, 6, 7, 8, 9, 10, 13, 14, 15, 16, 18, 11, 12, 17)
COL_HY = 0
COL_HG_Q, COL_HG_I, COL_HG_ZF, COL_HG_ZB, COL_HG_G = 3, 4, 5, 6, 7
COL_GLA_QK, COL_GLA_V, COL_GLA_R = 8, 9, 10
COL_ML_Q, COL_ML_K, COL_ML_V, COL_ML_O = 11, 12, 13, 14
N_WIDE = 15

LANE = 128
SUBLANE = 8
V7X_VMEM_BYTES = 64 * 1024 * 1024
ROW_TILE = 512
MOE_ROWS = 256
ROUTER_TILE = 256
SCATTER_TILE = 512
GLR_TILE_ELEMS = 128 * 256
VMEM_LIMIT = V7X_VMEM_BYTES * 7 // 8
NEG_INF = float("-inf")
COL_NARROW = N_WIDE * GROUP_W // LANE
NARROW_ML_GATES = N_WIDE * GROUP_W + 2 * GLA_RANK
P_WIDTH = N_WIDE * GROUP_W + LANE


def _arrange_w_in(w):
    offs = np.concatenate([[0], np.cumsum(IN_SPLITS)])
    cols = [w[:, offs[i]:offs[i + 1]] for i in P_ORDER]
    used = sum(IN_SPLITS)
    return jnp.concatenate(cols + [jnp.zeros((w.shape[0], P_WIDTH - used), w.dtype)], axis=1).astype(BF16)


def _in_proj_kernel(x_ref, g_ref, sc_ref, sh_ref, w_ref, o_ref):
    x = x_ref[...]
    y = x * lax.rsqrt(jnp.mean(x * x, axis=-1, keepdims=True) + EPS) * g_ref[...]
    h = y * (1.0 + sc_ref[...]) + sh_ref[...]
    o_ref[...] = jnp.dot(h.astype(BF16), w_ref[...], preferred_element_type=F32)


def _in_proj(x, gain, scale, shift, w):
    g, r, d = x.shape
    n = w.shape[1]
    tm = min(ROW_TILE, r)
    assert r % tm == 0
    return pl.pallas_call(
        _in_proj_kernel,
        out_shape=jax.ShapeDtypeStruct((g, r, n), F32),
        grid=(g, r // tm),
        in_specs=[pl.BlockSpec((None, tm, d), lambda b, i: (b, i, 0)),
                  pl.BlockSpec((1, d), lambda b, i: (0, 0)),
                  pl.BlockSpec((None, 1, d), lambda b, i: (b, 0, 0)),
                  pl.BlockSpec((None, 1, d), lambda b, i: (b, 0, 0)),
                  pl.BlockSpec((d, n), lambda b, i: (0, 0))],
        out_specs=pl.BlockSpec((None, tm, n), lambda b, i: (b, i, 0)),
        compiler_params=pltpu.CompilerParams(dimension_semantics=("arbitrary", "arbitrary"),
                                             vmem_limit_bytes=VMEM_LIMIT),
        name="input_projection",
    )(x, gain.reshape(1, d), scale, shift, w)


def _mixer_out_kernel(hy_ref, hgf_ref, hgb_ref, glf_ref, glb_ref, mlf_ref, mlb_ref, ghg_ref, ggl_ref, gml_ref,
                      x_ref, w_ref, hn_ref, n1_ref, n2_ref, mod_ref, xo_ref, h_ref):
    gw = hy_ref.shape[-1]
    seg = jnp.where(lax.broadcasted_iota(jnp.int32, (gw, gw), 0) // HG_DK
                    == lax.broadcasted_iota(jnp.int32, (gw, gw), 1) // HG_DK, 1.0 / HG_DK, 0.0)

    def head_norm(o, gain):
        ms = jnp.dot(o * o, seg, precision=lax.Precision.HIGHEST, preferred_element_type=F32)
        return o * lax.rsqrt(ms + EPS) * gain

    silu = lambda a: a * jax.nn.sigmoid(a)
    groups = (hy_ref[...],
              head_norm(hgf_ref[...] + hgb_ref[...], hn_ref[0:1]) * silu(ghg_ref[...]),
              head_norm(glf_ref[...] + glb_ref[...], hn_ref[1:2]) * silu(ggl_ref[...]),
              jax.nn.sigmoid(gml_ref[...]) * head_norm(mlf_ref[...] + mlb_ref[...], hn_ref[2:3]))
    y = None
    for i, part in enumerate(groups):
        term = jnp.dot(part.astype(BF16), w_ref[i * gw:(i + 1) * gw, :], preferred_element_type=F32)
        y = term if y is None else y + term
    rms = lambda a, g: a * lax.rsqrt(jnp.mean(a * a, axis=-1, keepdims=True) + EPS) * g
    x = x_ref[...] + mod_ref[0:1] * rms(y, n1_ref[...])
    xo_ref[...] = x
    h_ref[...] = rms(x, n2_ref[...]) * (1.0 + mod_ref[1:2]) + mod_ref[2:3]


def _mixer_out(y_hy, o_hg, o_gla, h_ml, p3, x, w_out, head_gains, gain1, gain2, mod):
    assert HG_DK == GLA_DV == ML_DH and HG_H == GLA_H == ML_H
    bsz, t, d = x.shape
    gw = GROUP_W
    tm = min(ROW_TILE, t)
    part = pl.BlockSpec((None, tm, gw), lambda b, i: (b, i, 0))
    gate = lambda c: pl.BlockSpec((None, tm, gw), lambda b, i: (b, i, c))
    full = pl.BlockSpec((None, tm, d), lambda b, i: (b, i, 0))
    whole = lambda a: pl.BlockSpec(a.shape, lambda b, i: (0,) * a.ndim)
    consts = [w_out.astype(BF16), head_gains, gain1.reshape(1, d), gain2.reshape(1, d)]
    return pl.pallas_call(
        _mixer_out_kernel,
        out_shape=[jax.ShapeDtypeStruct((bsz, t, d), F32), jax.ShapeDtypeStruct((bsz, t, d), F32)],
        grid=(bsz, t // tm),
        in_specs=([part] * 7 + [gate(COL_HG_G), gate(COL_GLA_R), gate(COL_ML_O), full]
                  + [whole(a) for a in consts] + [pl.BlockSpec((None, 3, d), lambda b, i: (b, 0, 0))]),
        out_specs=[full, full],
        compiler_params=pltpu.CompilerParams(dimension_semantics=("arbitrary", "arbitrary"),
                                             vmem_limit_bytes=VMEM_LIMIT),
        name="mixer_output",
    )(y_hy, *o_hg, *o_gla, *h_ml, p3, p3, p3, x, *consts, mod)


def _rms(x, g):
    return x * lax.rsqrt(jnp.mean(x * x, axis=-1, keepdims=True) + EPS) * g


def _short_conv_kernel(prev_ref, cur_ref, next_ref, w_ref, b_ref, o_ref, *, act):
    j = pl.program_id(1)
    u = cur_ref[...]
    tt = u.shape[0]
    row = lax.broadcasted_iota(jnp.int32, u.shape, 0)
    before = jnp.where(j > 0, prev_ref[SUBLANE - 1:SUBLANE, :], 0.0)
    after = jnp.where(j < pl.num_programs(1) - 1, next_ref[0:1, :], 0.0)
    up = jnp.where(row == 0, before, pltpu.roll(u, 1, 0))
    dn = jnp.where(row == tt - 1, after, pltpu.roll(u, tt - 1, 0))
    y = w_ref[0:1] * up + w_ref[1:2] * u + w_ref[2:3] * dn + b_ref[...]
    if act:
        y = y * jax.nn.sigmoid(y)
    o_ref[...] = y


def _short_conv(p3, col0, ncols, w, b, act):
    assert SHORT_CONV == 3
    bsz, t, _ = p3.shape
    tt = min(ROW_TILE, t)
    halo = tt // SUBLANE
    last = t // SUBLANE - 1
    gw = GROUP_W
    cur = pl.BlockSpec((None, tt, gw), lambda bi, j, c: (bi, j, col0 + c))
    prev = pl.BlockSpec((None, SUBLANE, gw), lambda bi, j, c: (bi, jnp.maximum(j * halo - 1, 0), col0 + c))
    nxt = pl.BlockSpec((None, SUBLANE, gw), lambda bi, j, c: (bi, jnp.minimum((j + 1) * halo, last), col0 + c))
    return pl.pallas_call(
        functools.partial(_short_conv_kernel, act=act),
        out_shape=jax.ShapeDtypeStruct((bsz, t, ncols * gw), F32),
        grid=(bsz, t // tt, ncols),
        in_specs=[prev, cur, nxt, pl.BlockSpec((SHORT_CONV, gw), lambda bi, j, c: (0, c)),
                  pl.BlockSpec((1, gw), lambda bi, j, c: (0, c))],
        out_specs=pl.BlockSpec((None, tt, gw), lambda bi, j, c: (bi, j, c)),
        compiler_params=pltpu.CompilerParams(dimension_semantics=("arbitrary",) * 3, vmem_limit_bytes=VMEM_LIMIT),
        name="short_conv",
    )(p3, p3, p3, w, b.reshape(1, ncols * gw))


def _pos_embed_2d(rows, d):
    r = jnp.repeat(jnp.arange(rows, dtype=F32), GRID_W)
    col = (jnp.arange(rows * GRID_W) % GRID_W).astype(F32)
    quarter = d // 4
    omega = 1.0 / (POS_BASE ** (jnp.arange(quarter, dtype=F32) / quarter))

    def axis_emb(p):
        ang = p[:, None] * omega[None, :]
        return jnp.concatenate([jnp.sin(ang), jnp.cos(ang)], axis=-1)

    return jnp.concatenate([axis_emb(r), axis_emb(col)], axis=-1)


def _hyena_spectra(L, w1, b1, w2, b2, w3, freq):
    t = jnp.linspace(0.0, 1.0, L, dtype=F32)[:, None]
    w = 2.0 * math.pi * jnp.arange(L, dtype=F32)[:, None] / L
    bands = jnp.linspace(1e-4, HY_BANDS - 1, HY_BANDS, dtype=F32)[None, :]
    feats = jnp.concatenate([t, jnp.cos(bands * w), -jnp.sin(bands * w)], axis=-1)
    z = jnp.sin(freq[0] * (feats @ w1 + b1))
    z = jnp.sin(freq[1] * (z @ w2 + b2))
    h = (z @ w3).reshape(L, HY_ORDER, 2, HY_W)
    max_decay = math.log(HY_DECAY_TARGET) / HY_FAST_DECAY
    min_decay = math.log(HY_DECAY_TARGET) / HY_SLOW_DECAY
    deltas = jnp.abs(jnp.linspace(min_decay, max_decay, HY_W, dtype=F32))
    h = h * jnp.exp(-t[:, :, None, None] * deltas)
    fwd = h[:, :, 0]
    bwd = h[1:, :, 1][::-1]
    l1 = jnp.sum(jnp.abs(fwd), axis=0) + jnp.sum(jnp.abs(bwd), axis=0)
    filt = jnp.concatenate([fwd, jnp.zeros((1, HY_ORDER, HY_W), F32), bwd], axis=0) / l1
    return jnp.fft.rfft(filt, axis=0)


def _fft_conv(u, spec, bias):
    L = u.shape[1]
    y = jnp.fft.irfft(jnp.fft.rfft(u, n=2 * L, axis=1) * spec, n=2 * L, axis=1)[:, :L]
    return y + u * bias


def _hyena(p3, conv_w, conv_b, w1, b1, w2, b2, w3, freq, bias, norm_g):
    u = _short_conv(p3, COL_HY, 3, conv_w, conv_b, False)
    v, x1, x2 = u[..., :HY_W], u[..., HY_W:2 * HY_W], u[..., 2 * HY_W:]
    spec = _hyena_spectra(u.shape[1], w1, b1, w2, b2, w3, freq)
    z = x1 * _fft_conv(v, spec[:, 0], bias[0])
    y = x2 * _fft_conv(z, spec[:, 1], bias[1])
    return _rms(y, norm_g)


FFT_N1 = 128
FFT_N2 = 128
FFT_KTILE = 8
FFT_NTILE = 4096


def _dft_tables(n1, n2):
    n = n1 * n2
    k = np.arange(n1)
    f1 = np.exp(-2j * np.pi * np.outer(k, k) / n1)
    f2 = np.exp(-2j * np.pi * np.outer(np.arange(n2), np.arange(n2)) / n2)
    tw = np.exp(-2j * np.pi * np.outer(np.arange(n1), np.arange(n2)) / n)
    as32 = lambda a: jnp.asarray(np.ascontiguousarray(a), F32)
    f1_fwd = as32(np.concatenate([f1.real, f1.imag], axis=0))
    f1_inv = as32(np.concatenate([f1.real, f1.imag], axis=1) / n)
    f2_inv = as32(np.block([[f2.real, f2.imag], [-f2.imag, f2.real]]))
    return f1_fwd, f1_inv, as32(f2.real), as32(f2.imag), f2_inv, as32(tw.real), as32(tw.imag)


def _stage_kernel(w_ref, x_ref, o_ref):
    o_ref[...] = jnp.dot(w_ref[...].astype(BF16), x_ref[...].astype(BF16),
                         preferred_element_type=F32).astype(o_ref.dtype)


def _stage_matmul(w, x, out_dtype):
    g, k, n = x.shape
    m = w.shape[0]
    tn = FFT_NTILE
    return pl.pallas_call(
        _stage_kernel,
        out_shape=jax.ShapeDtypeStruct((g, m, n), out_dtype),
        grid=(g, n // tn),
        in_specs=[pl.BlockSpec((m, k), lambda b, j: (0, 0)), pl.BlockSpec((None, k, tn), lambda b, j: (b, 0, j))],
        out_specs=pl.BlockSpec((None, m, tn), lambda b, j: (b, 0, j)),
        compiler_params=pltpu.CompilerParams(dimension_semantics=("arbitrary", "arbitrary"),
                                             vmem_limit_bytes=VMEM_LIMIT),
        name="dft_stage",
    )(w, x)


def _twiddled_f2(f2r, f2i, tr, ti):
    gr = f2r * tr - f2i * ti
    gi = f2r * ti + f2i * tr
    return jnp.concatenate([jnp.concatenate([gr, -gi], axis=1), jnp.concatenate([gi, gr], axis=1)], axis=0)


def _spectrum_kernel(a_ref, f2r_ref, f2i_ref, tr_ref, ti_ref, x_ref):
    n2 = f2r_ref.shape[0]
    tr = tr_ref[...]
    ti = ti_ref[...]
    for i in range(a_ref.shape[0]):
        g = _twiddled_f2(f2r_ref[...], f2i_ref[...], tr[i:i + 1], ti[i:i + 1])
        x_ref[i] = jnp.dot(g.astype(BF16), a_ref[i], preferred_element_type=F32)


def _conv_mid_kernel(a_ref, h_ref, f2r_ref, f2i_ref, f2inv_ref, tr_ref, ti_ref, z_ref):
    n2 = f2r_ref.shape[0]
    tr = tr_ref[...]
    ti = ti_ref[...]
    tr_col = tr.T
    ti_col = ti.T
    f2inv = f2inv_ref[...].astype(BF16)
    for i in range(a_ref.shape[0]):
        g = _twiddled_f2(f2r_ref[...], f2i_ref[...], tr[i:i + 1], ti[i:i + 1])
        x = jnp.dot(g.astype(BF16), a_ref[i], preferred_element_type=F32)
        xr, xi = x[:n2], x[n2:]
        hr, hi = h_ref[i, :n2], h_ref[i, n2:]
        y = jnp.concatenate([hr * xr - hi * xi, hr * xi + hi * xr], axis=0)
        w = jnp.dot(f2inv, y.astype(BF16), preferred_element_type=F32)
        wr, wi = w[:n2], w[n2:]
        cr, ci = tr_col[:, i:i + 1], ti_col[:, i:i + 1]
        z_ref[i] = jnp.concatenate([cr * wr + ci * wi, cr * wi - ci * wr], axis=0).astype(z_ref.dtype)


def _dft_mid_specs(g, c):
    n1, n2, kt = FFT_N1, FFT_N2, FFT_KTILE
    blk = pl.BlockSpec((None, kt, 2 * n2, c), lambda b, j: (b, j, 0, 0))
    const = lambda r, cc: pl.BlockSpec((r, cc), lambda b, j: (0, 0))
    twid = pl.BlockSpec((kt, n2), lambda b, j: (j, 0))
    params = pltpu.CompilerParams(dimension_semantics=("arbitrary", "arbitrary"), vmem_limit_bytes=VMEM_LIMIT)
    return blk, const, twid, params, (g, n1 // kt)


def _to_k1_major(a2d, c):
    g = a2d.shape[0]
    return a2d.reshape(g, 2, FFT_N1, FFT_N2, c).transpose(0, 2, 1, 3, 4).reshape(g, FFT_N1, 2 * FFT_N2, c)


def _filter_spectrum(filt, tables):
    g, n, c = filt.shape
    f1_fwd, _, f2r, f2i, _, twr, twi = tables
    a = _stage_matmul(f1_fwd, filt.reshape(g, FFT_N1, FFT_N2 * c), BF16)
    blk, const, twid, params, grid = _dft_mid_specs(g, c)
    return pl.pallas_call(
        _spectrum_kernel,
        out_shape=jax.ShapeDtypeStruct((g, FFT_N1, 2 * FFT_N2, c), F32),
        grid=grid,
        in_specs=[blk, const(FFT_N2, FFT_N2), const(FFT_N2, FFT_N2), twid, twid],
        out_specs=blk, compiler_params=params, name="dft_spectrum",
    )(_to_k1_major(a, c), f2r, f2i, twr, twi)


def _long_conv(u, spec, tables):
    g, l, c = u.shape
    f1_fwd, f1_inv, f2r, f2i, f2inv, twr, twi = tables
    half = l // FFT_N2
    a = _stage_matmul(f1_fwd[:, :half], u.reshape(g, half, FFT_N2 * c), BF16)
    blk, const, twid, params, grid = _dft_mid_specs(g, c)
    hspec = pl.BlockSpec((FFT_KTILE, 2 * FFT_N2, c), lambda b, j: (j, 0, 0))
    z = pl.pallas_call(
        _conv_mid_kernel,
        out_shape=jax.ShapeDtypeStruct((g, FFT_N1, 2 * FFT_N2, c), BF16),
        grid=grid,
        in_specs=[blk, hspec, const(FFT_N2, FFT_N2), const(FFT_N2, FFT_N2), const(2 * FFT_N2, 2 * FFT_N2),
                  twid, twid],
        out_specs=blk, compiler_params=params, name="dft_conv_mid",
    )(_to_k1_major(a, c), spec, f2r, f2i, f2inv, twr, twi)
    z2d = z.reshape(g, FFT_N1, 2, FFT_N2, c).transpose(0, 2, 1, 3, 4).reshape(g, 2 * FFT_N1, FFT_N2 * c)
    y = _stage_matmul(f1_inv[:half], z2d, F32)
    return y.reshape(g, l, c)


def _hyena_filters(L, w1, b1, w2, b2, w3, freq):
    t = jnp.linspace(0.0, 1.0, L, dtype=F32)[:, None]
    w = 2.0 * math.pi * jnp.arange(L, dtype=F32)[:, None] / L
    bands = jnp.linspace(1e-4, HY_BANDS - 1, HY_BANDS, dtype=F32)[None, :]
    feats = jnp.concatenate([t, jnp.cos(bands * w), -jnp.sin(bands * w)], axis=-1)
    max_decay = math.log(HY_DECAY_TARGET) / HY_FAST_DECAY
    min_decay = math.log(HY_DECAY_TARGET) / HY_SLOW_DECAY
    deltas = jnp.abs(jnp.linspace(min_decay, max_decay, HY_W, dtype=F32))
    w3d = w3.reshape(w3.shape[0], HY_ORDER, 2, HY_W)

    def side(f, tt, direction):
        z = jnp.sin(freq[0] * (f @ w1 + b1))
        z = jnp.sin(freq[1] * (z @ w2 + b2))
        h = (z @ w3d[:, :, direction].reshape(w3.shape[0], HY_ORDER * HY_W)).reshape(-1, HY_ORDER, HY_W)
        return h * jnp.exp(-tt[:, :, None] * deltas)

    fwd = side(feats, t, 0)
    bwd = side(feats[::-1], t[::-1], 1)[:L - 1]
    l1 = jnp.sum(jnp.abs(fwd), axis=0) + jnp.sum(jnp.abs(bwd), axis=0)
    return jnp.concatenate([fwd, jnp.zeros((1, HY_ORDER, HY_W), F32), bwd], axis=0) / l1


def _hyena_long(p3, conv_w, conv_b, w1, b1, w2, b2, w3, freq, bias, norm_g):
    u = _short_conv(p3, COL_HY, 3, conv_w, conv_b, False)
    v, x1, x2 = u[..., :HY_W], u[..., HY_W:2 * HY_W], u[..., 2 * HY_W:]
    L = u.shape[1]
    assert 2 * L == FFT_N1 * FFT_N2
    tables = _dft_tables(FFT_N1, FFT_N2)
    filt = _hyena_filters(L, w1, b1, w2, b2, w3, freq)
    spec = _filter_spectrum(jnp.moveaxis(filt, 1, 0), tables)
    z = x1 * (_long_conv(v, spec[0], tables) + v * bias[0])
    y = x2 * (_long_conv(z, spec[1], tables) + z * bias[1])
    return _rms(y, norm_g)


def _glr_direction(q, k, v, g, st_ref, reverse, nh):
    tc, hk = k.shape
    hv = v.shape[1]
    c = CHUNK_GATED
    hi = lax.Precision.HIGHEST
    ti = lax.broadcasted_iota(jnp.int32, (tc, tc), 0)
    tj = lax.broadcasted_iota(jnp.int32, (tc, tc), 1)
    same = (ti // c) == (tj // c)
    seen = (tj >= ti) if reverse else (tj <= ti)
    bcum = jnp.dot(jnp.where(same, jnp.where(seen, 1.0, 0.0), 0.0), g, precision=hi, preferred_element_type=F32)
    btot = jnp.dot(jnp.where(same, 1.0, 0.0), g, precision=hi, preferred_element_type=F32)
    qd = q * jnp.exp(bcum)
    kd = k * jnp.exp(btot - bcum)
    dec = jnp.exp(btot)
    head_sum = jnp.where(lax.broadcasted_iota(jnp.int32, (hk, hv), 0) // (hk // nh)
                         == lax.broadcasted_iota(jnp.int32, (hk, hv), 1) // (hv // nh), 1.0, 0.0).astype(BF16)
    in_chunk = lax.broadcasted_iota(jnp.int32, (tc, hk), 0) % c
    o = jnp.zeros((tc, hv), F32)
    for lag in range(c):
        if lag == 0:
            ks, bs, vs = k, bcum, v
        else:
            shift = tc - lag if reverse else lag
            ks, bs, vs = pltpu.roll(k, shift, 0), pltpu.roll(bcum, shift, 0), pltpu.roll(v, shift, 0)
        valid = (in_chunk + lag <= c - 1) if reverse else (in_chunk >= lag)
        x = q * ks * jnp.exp(jnp.where(valid, bcum - bs, NEG_INF))
        o = o + jnp.dot(x.astype(BF16), head_sum, preferred_element_type=F32) * vs
    head_mask = (lax.broadcasted_iota(jnp.int32, (hv, hk), 0) // (hv // nh)
                 == lax.broadcasted_iota(jnp.int32, (hv, hk), 1) // (hk // nh))
    st = st_ref[...]
    nch = tc // c
    outs = [None] * nch
    for ci in (range(nch - 1, -1, -1) if reverse else range(nch)):
        sl = slice(ci * c, (ci + 1) * c)
        outs[ci] = lax.dot_general(qd[sl].astype(BF16), st.astype(BF16), (((1,), (1,)), ((), ())),
                                   preferred_element_type=F32)
        ds = lax.dot_general(v[sl].astype(BF16), kd[sl].astype(BF16), (((0,), (0,)), ((), ())),
                             preferred_element_type=F32)
        st = st * dec[ci * c:ci * c + 1] + jnp.where(head_mask, ds, 0.0)
    st_ref[...] = st
    return o + jnp.concatenate(outs, axis=0)


def _log_sigmoid(z):
    return jnp.minimum(z, 0.0) - jnp.log1p(jnp.exp(-jnp.abs(z)))


def _hgrn2_kernel(qf_ref, if_ref, zf_ref, qb_ref, ib_ref, zb_ref, lb_ref, s0f_ref, s0b_ref,
                  of_ref, ob_ref, sf_ref, sb_ref, stf, stb):
    j = pl.program_id(1)

    @pl.when(j == 0)
    def _():
        stf[...] = s0f_ref[...]
        stb[...] = s0b_ref[...]

    one_minus_lb, log_lb, log_ub = lb_ref[0:1], lb_ref[1:2], lb_ref[2:3]

    def gate(z):
        return one_minus_lb * jax.nn.sigmoid(-z), jnp.logaddexp(log_lb, log_ub + _log_sigmoid(z))

    silu = lambda a: a * jax.nn.sigmoid(a)
    k_f, g_f = gate(zf_ref[...])
    k_b, g_b = gate(zb_ref[...])
    of_ref[...] = _glr_direction(silu(qf_ref[...]), k_f, if_ref[...], g_f, stf, False, HG_H)
    ob_ref[...] = _glr_direction(silu(qb_ref[...]), k_b, ib_ref[...], g_b, stb, True, HG_H)

    @pl.when(j == pl.num_programs(1) - 1)
    def _():
        sf_ref[...] = stf[...]
        sb_ref[...] = stb[...]


def _gla_kernel(qkf_ref, vf_ref, nf_ref, qkb_ref, vb_ref, nb_ref, aup_ref, ab_ref, s0f_ref, s0b_ref,
                of_ref, ob_ref, sf_ref, sb_ref, stf, stb):
    j = pl.program_id(1)

    @pl.when(j == 0)
    def _():
        stf[...] = s0f_ref[...]
        stb[...] = s0b_ref[...]

    def gate(narrow, idx):
        a = narrow[:, idx * GLA_RANK:(idx + 1) * GLA_RANK]
        lin = jnp.dot(a.astype(BF16), aup_ref[idx].astype(BF16), preferred_element_type=F32) + ab_ref[idx]
        return _log_sigmoid(lin) / GLA_NORMALIZER

    qk_f = qkf_ref[...]
    qk_b = qkb_ref[...]
    of_ref[...] = _glr_direction(qk_f[:, :GLA_KW] * GLA_DK ** -0.5, qk_f[:, GLA_KW:], vf_ref[...],
                                 gate(nf_ref[...], 0), stf, False, GLA_H)
    ob_ref[...] = _glr_direction(qk_b[:, :GLA_KW] * GLA_DK ** -0.5, qk_b[:, GLA_KW:], vb_ref[...],
                                 gate(nb_ref[...], 1), stb, True, GLA_H)

    @pl.when(j == pl.num_programs(1) - 1)
    def _():
        sf_ref[...] = stf[...]
        sb_ref[...] = stb[...]


def _glr_call(kernel_fn, p3, fwd_cols, bwd_cols, consts, s0_f, s0_b, hk, hv, name):
    bsz, t, _ = p3.shape
    tc = min(GLR_TILE_ELEMS // hk, t)
    nsb = t // tc
    fwd = lambda w, c: pl.BlockSpec((None, tc, w), lambda b, j: (b, j, c))
    bwd = lambda w, c: pl.BlockSpec((None, tc, w), lambda b, j: (b, nsb - 1 - j, c))
    whole = lambda a: pl.BlockSpec(a.shape, lambda b, j: (0,) * a.ndim)
    st = pl.BlockSpec((None, hv, hk), lambda b, j: (b, 0, 0))
    out_f = pl.BlockSpec((None, tc, hv), lambda b, j: (b, j, 0))
    out_b = pl.BlockSpec((None, tc, hv), lambda b, j: (b, nsb - 1 - j, 0))
    return pl.pallas_call(
        kernel_fn,
        out_shape=[jax.ShapeDtypeStruct((bsz, t, hv), F32), jax.ShapeDtypeStruct((bsz, t, hv), F32),
                   jax.ShapeDtypeStruct((bsz, hv, hk), F32), jax.ShapeDtypeStruct((bsz, hv, hk), F32)],
        grid=(bsz, nsb),
        in_specs=([fwd(w, c) for w, c in fwd_cols] + [bwd(w, c) for w, c in bwd_cols]
                  + [whole(a) for a in consts] + [st, st]),
        out_specs=[out_f, out_b, st, st],
        scratch_shapes=[pltpu.VMEM((hv, hk), F32), pltpu.VMEM((hv, hk), F32)],
        compiler_params=pltpu.CompilerParams(dimension_semantics=("arbitrary", "arbitrary"),
                                             vmem_limit_bytes=VMEM_LIMIT),
        name=name,
    )(*([p3] * (len(fwd_cols) + len(bwd_cols))), *consts, s0_f, s0_b)


def _hgrn2_seq(p3, lb, s0_f, s0_b):
    lb_rows = jnp.stack([1.0 - lb, jnp.log(lb), jnp.log1p(-lb)])
    cols = lambda z: [(GROUP_W, COL_HG_Q), (GROUP_W, COL_HG_I), (GROUP_W, z)]
    o_f, o_b, s_f, s_b = _glr_call(_hgrn2_kernel, p3, cols(COL_HG_ZF), cols(COL_HG_ZB), [lb_rows],
                                   s0_f, s0_b, HG_W, HG_W, "hgrn2_recurrence")
    return (o_f, o_b), s_f, s_b


def _hgrn2(pc3, pl3, lb):
    s0 = jnp.zeros((pl3.shape[0], HG_W, HG_W), F32)
    oc, s_f, s_b = _hgrn2_seq(pc3, lb, s0, s0)
    o, _, _ = _hgrn2_seq(pl3, lb, s_f, s_b)
    return oc, o


def _gla_seq(p3, a_up, a_b, s0_f, s0_b):
    cols = [(GROUP_W, COL_GLA_QK), (GROUP_W, COL_GLA_V), (LANE, COL_NARROW)]
    o_f, o_b, s_f, s_b = _glr_call(_gla_kernel, p3, cols, cols, [a_up, a_b.reshape(2, 1, GLA_KW)],
                                   s0_f, s0_b, GLA_KW, GLA_VW, "gla_recurrence")
    return (o_f, o_b), s_f, s_b


def _gla(pc3, pl3, a_up, a_b):
    s0 = jnp.zeros((pl3.shape[0], GLA_VW, GLA_KW), F32)
    oc, s_f, s_b = _gla_seq(pc3, a_up, a_b, s0, s0)
    o, _, _ = _gla_seq(pl3, a_up, a_b, s_f, s_b)
    return oc, o


def _mlstm_direction(q, k, v, igx, lfx, s_ref, n_ref, m_ref, reverse, nh):
    tc, w = q.shape
    seg = w // nh
    assert tc == seg
    hi = lax.Precision.HIGHEST
    ti = lax.broadcasted_iota(jnp.int32, (tc, tc), 0)
    tj = lax.broadcasted_iota(jnp.int32, (tc, tc), 1)
    seen = (tj >= ti) if reverse else (tj <= ti)
    b = jnp.dot(jnp.where(seen, 1.0, 0.0), lfx, precision=hi, preferred_element_type=F32)
    bl = b[0:1] if reverse else b[tc - 1:tc]
    a = bl - b + igx
    ma = jnp.max(a, axis=0, keepdims=True)
    kw = jnp.exp(a - ma) * k
    s_prev = s_ref[...]
    n_prev = n_ref[...]
    m_prev = m_ref[...]
    lane = lax.broadcasted_iota(jnp.int32, (tc, w), 1)
    row = lax.broadcasted_iota(jnp.int32, (tc, w), 0)
    same_head = (lax.broadcasted_iota(jnp.int32, (w, w), 0) // seg
                 == lax.broadcasted_iota(jnp.int32, (w, w), 1) // seg)
    kexp = jnp.where(same_head, jnp.concatenate([k] * nh, axis=0), 0.0)
    vexp = jnp.where(same_head, jnp.concatenate([v] * nh, axis=0), 0.0)
    scores = lax.dot_general(q.astype(BF16), kexp.astype(BF16), (((1,), (1,)), ((), ())),
                             preferred_element_type=F32)
    s_lane = lane % seg
    by_src = jnp.sum(jnp.where(s_lane == row, igx - b, 0.0), axis=0, keepdims=True)
    ok = (s_lane >= row) if reverse else (s_lane <= row)
    dmat = jnp.where(ok, b + by_src, NEG_INF)
    inter = b + m_prev
    head_of_lane = lane // seg
    seg_max = jnp.full((tc, w), NEG_INF, F32)
    for h in range(nh):
        in_h = head_of_lane == h
        seg_max = jnp.where(in_h, jnp.max(jnp.where(in_h, dmat, NEG_INF), axis=1, keepdims=True), seg_max)
    m_t = jnp.maximum(inter, seg_max)
    wq = jnp.exp(dmat - m_t) * scores
    w_int = jnp.exp(inter - m_t)
    head_sum = jnp.where(same_head, 1.0, 0.0).astype(BF16)
    num = (jnp.dot(wq.astype(BF16), vexp.astype(BF16), preferred_element_type=F32)
           + w_int * jnp.dot(q.astype(BF16), s_prev.astype(BF16), preferred_element_type=F32))
    den = (jnp.dot(wq.astype(BF16), head_sum, preferred_element_type=F32)
           + w_int * jnp.dot((q * n_prev).astype(BF16), head_sum, preferred_element_type=F32))
    h_out = num / jnp.maximum(jnp.abs(den), jnp.exp(-m_t))
    m_new = jnp.maximum(bl + m_prev, ma)
    d_old = jnp.exp(bl + m_prev - m_new)
    d_new = jnp.exp(ma - m_new)
    ds = lax.dot_general(kw.astype(BF16), v.astype(BF16), (((0,), (0,)), ((), ())), preferred_element_type=F32)
    s_ref[...] = d_old * s_prev + d_new * jnp.where(same_head, ds, 0.0)
    n_ref[...] = d_old * n_prev + d_new * jnp.sum(kw, axis=0, keepdims=True)
    m_ref[...] = m_new
    return h_out


def _mlstm_kernel(qf_ref, kf_ref, vf_ref, igf_ref, lff_ref, qb_ref, kb_ref, vb_ref, igb_ref, lfb_ref,
                  s0f_ref, n0f_ref, m0f_ref, s0b_ref, n0b_ref, m0b_ref,
                  hf_ref, hb_ref, sf_ref, nf_ref, mf_ref, sb_ref, nb_ref, mb_ref,
                  s_f, n_f, m_f, s_b, n_b, m_b, *, nh):
    j = pl.program_id(1)

    @pl.when(j == 0)
    def _():
        s_f[...] = s0f_ref[...]
        n_f[...] = n0f_ref[...]
        m_f[...] = m0f_ref[...]
        s_b[...] = s0b_ref[...]
        n_b[...] = n0b_ref[...]
        m_b[...] = m0b_ref[...]

    hf_ref[...] = _mlstm_direction(qf_ref[...], kf_ref[...], vf_ref[...], igf_ref[...], lff_ref[...],
                                   s_f, n_f, m_f, False, nh)
    hb_ref[...] = _mlstm_direction(qb_ref[...], kb_ref[...], vb_ref[...], igb_ref[...], lfb_ref[...],
                                   s_b, n_b, m_b, True, nh)

    @pl.when(j == pl.num_programs(1) - 1)
    def _():
        sf_ref[...] = s_f[...]
        nf_ref[...] = n_f[...]
        mf_ref[...] = m_f[...]
        sb_ref[...] = s_b[...]
        nb_ref[...] = n_b[...]
        mb_ref[...] = m_b[...]


def _mlstm_bidir(q, k, v, ig_f, lf_f, ig_b, lf_b, st_f, st_b, nh):
    bsz, t, w = q.shape
    tc = CHUNK_ML
    nsb = t // tc
    fwd = pl.BlockSpec((None, tc, w), lambda b, j: (b, j, 0))
    bwd = pl.BlockSpec((None, tc, w), lambda b, j: (b, nsb - 1 - j, 0))
    mat = pl.BlockSpec((None, w, w), lambda b, j: (b, 0, 0))
    vec = pl.BlockSpec((None, 1, w), lambda b, j: (b, 0, 0))
    sds = jax.ShapeDtypeStruct
    state_shapes = [sds((bsz, w, w), F32), sds((bsz, 1, w), F32), sds((bsz, 1, w), F32)]
    outs = pl.pallas_call(
        functools.partial(_mlstm_kernel, nh=nh),
        out_shape=[sds((bsz, t, w), F32), sds((bsz, t, w), F32)] + state_shapes + state_shapes,
        grid=(bsz, nsb),
        in_specs=[fwd] * 5 + [bwd] * 5 + [mat, vec, vec] * 2,
        out_specs=[fwd, bwd] + [mat, vec, vec] * 2,
        scratch_shapes=[pltpu.VMEM((w, w), F32), pltpu.VMEM((1, w), F32), pltpu.VMEM((1, w), F32)] * 2,
        compiler_params=pltpu.CompilerParams(dimension_semantics=("arbitrary", "arbitrary"),
                                             vmem_limit_bytes=VMEM_LIMIT),
        name="mlstm_recurrence",
    )(q, k, v, ig_f, lf_f, q, k, v, ig_b, lf_b, *st_f, *st_b)
    return outs[0], outs[1], tuple(outs[2:5]), tuple(outs[5:8])


def _mlstm_seq(p3, conv_w, conv_b, gate_b, st_f, st_b):
    qk = _short_conv(p3, COL_ML_Q, 2, conv_w, conv_b, True)
    v = p3[..., COL_ML_V * GROUP_W:(COL_ML_V + 1) * GROUP_W]
    bsz, t, _ = p3.shape
    gates = p3[..., NARROW_ML_GATES:NARROW_ML_GATES + 4 * ML_H]
    gt = gates.reshape(bsz, t, 4, ML_H) + gate_b
    expand = lambda a: jnp.repeat(a, ML_DH, axis=-1)
    h_f, h_b, fin_f, fin_b = _mlstm_bidir(
        qk[..., :ML_W], qk[..., ML_W:] * ML_DH ** -0.5, v,
        expand(gt[:, :, 0]), expand(jax.nn.log_sigmoid(gt[:, :, 1])),
        expand(gt[:, :, 2]), expand(jax.nn.log_sigmoid(gt[:, :, 3])), st_f, st_b, ML_H)
    return (h_f, h_b), fin_f, fin_b


def _mlstm(pc3, pl3, conv_w, conv_b, gate_b):
    bsz = pl3.shape[0]
    st0 = (jnp.zeros((bsz, ML_W, ML_W), F32), jnp.zeros((bsz, 1, ML_W), F32), jnp.zeros((bsz, 1, ML_W), F32))
    hc, st_f, st_b = _mlstm_seq(pc3, conv_w, conv_b, gate_b, st0, st0)
    h, _, _ = _mlstm_seq(pl3, conv_w, conv_b, gate_b, st_f, st_b)
    return hc, h


def _router_kernel(h_ref, wt_ref, b_ref, eidx_ref, wsel_ref, cnt_ref):
    i = pl.program_id(0)
    tm = h_ref.shape[0]
    ne = wt_ref.shape[0]
    per_group = ne // N_EXPERT_GROUPS
    logits = lax.dot_general(wt_ref[...], h_ref[...], (((1,), (1,)), ((), ())),
                             preferred_element_type=F32, precision=lax.Precision.HIGHEST)
    s = jax.nn.sigmoid(logits)
    sel = s + b_ref[...]
    row = lax.broadcasted_iota(jnp.int32, (ne, tm), 0)
    gs = []
    for g in range(N_EXPERT_GROUPS):
        blk = sel[g * per_group:(g + 1) * per_group]
        r = lax.broadcasted_iota(jnp.int32, blk.shape, 0)
        m1 = jnp.max(blk, axis=0, keepdims=True)
        i1 = jnp.min(jnp.where(blk == m1, r, per_group), axis=0, keepdims=True)
        m2 = jnp.max(jnp.where(r == i1, NEG_INF, blk), axis=0, keepdims=True)
        gs.append(m1 + m2)
    grp = jnp.concatenate(gs, axis=0)
    grow = lax.broadcasted_iota(jnp.int32, grp.shape, 0)
    gsel = jnp.zeros(grp.shape, F32)
    for _ in range(TOPK_GROUPS):
        m = jnp.max(grp, axis=0, keepdims=True)
        gi = jnp.min(jnp.where(grp == m, grow, N_EXPERT_GROUPS), axis=0, keepdims=True)
        hit = grow == gi
        gsel = jnp.where(hit, 1.0, gsel)
        grp = jnp.where(hit, NEG_INF, grp)
    masked = jnp.concatenate(
        [jnp.where(gsel[g:g + 1] > 0.0, sel[g * per_group:(g + 1) * per_group], NEG_INF)
         for g in range(N_EXPERT_GROUPS)], axis=0)
    eis, ws = [], []
    picked = jnp.zeros((ne, tm), F32)
    for _ in range(TOP_K):
        m = jnp.max(masked, axis=0, keepdims=True)
        ei = jnp.min(jnp.where(masked == m, row, ne), axis=0, keepdims=True)
        hit = row == ei
        ws.append(jnp.sum(jnp.where(hit, s, 0.0), axis=0, keepdims=True))
        eis.append(ei)
        picked = jnp.where(hit, 1.0, picked)
        masked = jnp.where(hit, NEG_INF, masked)
    w = jnp.concatenate(ws, axis=0)
    eidx_ref[...] = jnp.concatenate(eis, axis=0)
    wsel_ref[...] = w / jnp.sum(w, axis=0, keepdims=True) * ROUTED_SCALE
    tot = jnp.dot(picked.astype(BF16), jnp.ones((tm, LANE), BF16), preferred_element_type=F32)

    @pl.when(i == 0)
    def _():
        cnt_ref[...] = jnp.zeros_like(cnt_ref)

    cnt_ref[...] += tot


def _pos_kernel(eidx_ref, base_ref, pos_ref, carry_ref):
    i = pl.program_id(0)
    tm = eidx_ref.shape[1]
    ne = base_ref.shape[0]

    @pl.when(i == 0)
    def _():
        carry_ref[...] = jnp.zeros_like(carry_ref)

    eidx = eidx_ref[...]
    row = lax.broadcasted_iota(jnp.int32, (ne, tm), 0)
    picked = jnp.zeros((ne, tm), F32)
    for k in range(TOP_K):
        picked = jnp.where(row == eidx[k:k + 1], 1.0, picked)
    pb = picked.astype(BF16)
    before = jnp.where(lax.broadcasted_iota(jnp.int32, (tm, tm), 0) < lax.broadcasted_iota(jnp.int32, (tm, tm), 1),
                       1.0, 0.0).astype(BF16)
    rank = jnp.dot(pb, before, preferred_element_type=F32)
    tot = jnp.dot(pb, jnp.ones((tm, LANE), BF16), preferred_element_type=F32)
    dest = rank + (base_ref[...] + carry_ref[:, 0:1])
    pos = [jnp.sum(jnp.where(row == eidx[k:k + 1], dest, 0.0), axis=0, keepdims=True) for k in range(TOP_K)]
    pos_ref[...] = jnp.concatenate(pos, axis=0).astype(jnp.int32)
    carry_ref[...] += tot


def _pack_bf16_pairs(x):
    half = x.shape[1] // 2
    bits = lambda a: pltpu.bitcast(a.astype(BF16).astype(F32), jnp.int32)
    return (bits(x[:, half:]) & -65536) | lax.shift_right_logical(bits(x[:, :half]), 16)


def _unpack_bf16_pairs(w):
    return pltpu.bitcast(lax.shift_left(w, 16), F32), pltpu.bitcast(w & -65536, F32)


def _dispatch_kernel(zstart_ref, zlen_ref, nused_ref, pos_ref, h_ref, xs_ref, packed, zeros, sem, zsem, *,
                     n_blocks):
    tm = h_ref.shape[0]
    bm = zeros.shape[0]
    packed[...] = _pack_bf16_pairs(h_ref[...])

    @pl.when(pl.program_id(0) == 0)
    def _():
        zeros[...] = jnp.zeros_like(zeros)

        def zero_copy(start, size):
            return pltpu.make_async_copy(zeros.at[pl.ds(0, size)], xs_ref.at[pl.ds(start, size)], zsem)

        def pieces(e, act):
            start = zstart_ref[e]
            rem = zlen_ref[e]
            ragged = rem & (SUBLANE - 1)
            for q in range(SUBLANE - 1):
                @pl.when(q < ragged)
                def _(q=q):
                    act(zero_copy(start + q, 1))

            start = pl.multiple_of(start + ragged, SUBLANE)
            size = bm // 2
            while size >= SUBLANE:
                @pl.when((rem & size) != 0)
                def _(start=start, size=size):
                    act(zero_copy(start, size))

                start = pl.multiple_of(start + (rem & size), SUBLANE)
                size //= 2

        def loop(act):
            def per_expert(e, carry):
                pieces(e, act)
                return carry

            def per_block(b, carry):
                act(zero_copy(pl.multiple_of(b * bm, bm), bm))
                return carry

            lax.fori_loop(0, zstart_ref.shape[0], per_expert, 0)
            lax.fori_loop(nused_ref[0], n_blocks, per_block, 0)

        loop(lambda cp: cp.start())
        loop(lambda cp: cp.wait())

    def row_copy(n, k):
        return pltpu.make_async_copy(packed.at[pl.ds(n, 1)], xs_ref.at[pl.ds(pos_ref[n * TOP_K + k], 1)], sem)

    def issue(n, carry):
        for k in range(TOP_K):
            row_copy(n, k).start()
        return carry

    lax.fori_loop(0, tm, issue, 0)
    for _ in range(TOP_K):
        pltpu.make_async_copy(packed, xs_ref.at[pl.ds(0, tm)], sem).wait()


def _moe_ffn_kernel(blk_e_ref, run_ref, next_e_ref, nused_ref, x_ref, wgu_hbm, wdn_hbm, o_ref,
                    wgu_f, wdn_f, wgu_s, wdn_s, sems, *, layer):
    i = pl.program_id(0)
    e = blk_e_ref[i]
    run = run_ref[i]
    first_of_run = (i == 0) | (run != run_ref[jnp.maximum(i - 1, 0)])

    def fetch(expert, slot):
        return (pltpu.make_async_copy(wgu_hbm.at[layer, expert], wgu_f.at[slot], sems.at[0, slot]),
                pltpu.make_async_copy(wdn_hbm.at[layer, expert], wdn_f.at[slot], sems.at[1, slot]))

    @pl.when(i < nused_ref[0])
    def _():
        @pl.when(first_of_run)
        def _():
            slot = run & 1

            @pl.when(i == 0)
            def _():
                for cp in fetch(e, 0):
                    cp.start()

            for cp in fetch(e, slot):
                cp.wait()
            wgu_s[...] = wgu_f[slot].astype(BF16)
            wdn_s[...] = wdn_f[slot].astype(BF16)

            @pl.when(next_e_ref[i] >= 0)
            def _():
                for cp in fetch(next_e_ref[i], 1 - slot):
                    cp.start()

        x_lo, x_hi = _unpack_bf16_pairs(x_ref[...])
        half = x_lo.shape[1]
        au = (jnp.dot(x_lo.astype(BF16), wgu_s[:half, :], preferred_element_type=F32)
              + jnp.dot(x_hi.astype(BF16), wgu_s[half:, :], preferred_element_type=F32))
        a = au[:, :EXPERT_FF]
        u = au[:, EXPERT_FF:]
        h = (a * jax.nn.sigmoid(a)) * u
        o_ref[...] = _pack_bf16_pairs(jnp.dot(h.astype(BF16), wdn_s[...], preferred_element_type=F32))

    @pl.when(i >= nused_ref[0])
    def _():
        o_ref[...] = jnp.zeros_like(o_ref)


def _combine_kernel(pos_ref, w_ref, t_ref, x_ref, sgu_ref, sdn_ref, gain_ref, g2_ref, y_hbm, o_ref, buf, sem):
    tm = o_ref.shape[0]

    def row_copy(n, k):
        return pltpu.make_async_copy(y_hbm.at[pl.ds(pos_ref[n * TOP_K + k], 1)], buf.at[k, pl.ds(n, 1)], sem)

    def issue(n, carry):
        for k in range(TOP_K):
            row_copy(n, k).start()
        return carry

    lax.fori_loop(0, tm, issue, 0)
    ff = sdn_ref.shape[0]
    au = jnp.dot(t_ref[...].astype(BF16), sgu_ref[...], preferred_element_type=F32)
    a, u = au[:, :ff], au[:, ff:]
    f = jnp.dot(((a * jax.nn.sigmoid(a)) * u).astype(BF16), sdn_ref[...], preferred_element_type=F32)
    for k in range(TOP_K):
        pltpu.make_async_copy(y_hbm.at[pl.ds(0, tm)], buf.at[k], sem).wait()
    r_lo, r_hi = None, None
    for k in range(TOP_K):
        y_lo, y_hi = _unpack_bf16_pairs(buf[k])
        wk = w_ref[:, k:k + 1]
        r_lo = y_lo * wk if r_lo is None else r_lo + y_lo * wk
        r_hi = y_hi * wk if r_hi is None else r_hi + y_hi * wk
    f = f + jnp.concatenate([r_lo, r_hi], axis=1)
    o_ref[...] = x_ref[...] + g2_ref[...] * (f * lax.rsqrt(jnp.mean(f * f, axis=-1, keepdims=True) + EPS)
                                             * gain_ref[...])


def _moe(t, x_res, router_w, router_b, w_gu, w_down, sh_gu, sh_down, layer, gain, g2_rows, rows_per_gate):
    n, d = t.shape
    ne = router_w.shape[1]
    ff2 = w_gu.shape[-1]
    params = pltpu.CompilerParams(dimension_semantics=("arbitrary",), vmem_limit_bytes=VMEM_LIMIT)
    tm = ROUTER_TILE
    eidx, wsel, cnt = pl.pallas_call(
        _router_kernel,
        out_shape=[jax.ShapeDtypeStruct((TOP_K, n), jnp.int32), jax.ShapeDtypeStruct((TOP_K, n), F32),
                   jax.ShapeDtypeStruct((ne, LANE), F32)],
        grid=(n // tm,),
        in_specs=[pl.BlockSpec((tm, d), lambda i: (i, 0)), pl.BlockSpec((ne, d), lambda i: (0, 0)),
                  pl.BlockSpec((ne, 1), lambda i: (0, 0))],
        out_specs=[pl.BlockSpec((TOP_K, tm), lambda i: (0, i)), pl.BlockSpec((TOP_K, tm), lambda i: (0, i)),
                   pl.BlockSpec((ne, LANE), lambda i: (0, 0))],
        compiler_params=params, name="moe_router",
    )(t, router_w.T, router_b.reshape(ne, 1))
    bm = MOE_ROWS
    counts = cnt[:, 0].astype(jnp.int32)
    padded = (counts + bm - 1) // bm * bm
    pad_end = jnp.cumsum(padded)
    pad_start = pad_end - padded
    n_blocks = (n * TOP_K + ne * (bm - 1)) // bm + 1
    blk_first = jnp.arange(n_blocks, dtype=jnp.int32) * bm
    blk_e = jnp.minimum(jnp.sum((pad_end[None, :] <= blk_first[:, None]).astype(jnp.int32), axis=1), ne - 1)
    n_used = (pad_end[-1] // bm).astype(jnp.int32).reshape(1)
    pos = pl.pallas_call(
        _pos_kernel,
        out_shape=jax.ShapeDtypeStruct((TOP_K, n), jnp.int32),
        grid=(n // tm,),
        in_specs=[pl.BlockSpec((TOP_K, tm), lambda i: (0, i)), pl.BlockSpec((ne, 1), lambda i: (0, 0))],
        out_specs=pl.BlockSpec((TOP_K, tm), lambda i: (0, i)),
        scratch_shapes=[pltpu.VMEM((ne, LANE), F32)],
        compiler_params=params, name="moe_positions",
    )(eidx, pad_start.astype(F32).reshape(ne, 1))
    ts = SCATTER_TILE
    p = n_blocks * bm
    pos_tok = pos.T.reshape(n * TOP_K)
    pos_spec = pl.BlockSpec((ts * TOP_K,), lambda i: (i,), memory_space=pltpu.SMEM)
    dp = d // 2
    xs = pl.pallas_call(
        functools.partial(_dispatch_kernel, n_blocks=n_blocks),
        out_shape=jax.ShapeDtypeStruct((p, dp), jnp.int32),
        grid_spec=pltpu.PrefetchScalarGridSpec(
            num_scalar_prefetch=3,
            grid=(n // ts,),
            in_specs=[pl.BlockSpec((ts * TOP_K,), lambda i, *_: (i,), memory_space=pltpu.SMEM),
                      pl.BlockSpec((ts, d), lambda i, *_: (i, 0))],
            out_specs=pl.BlockSpec(memory_space=pl.ANY),
            scratch_shapes=[pltpu.VMEM((ts, dp), jnp.int32), pltpu.VMEM((bm, dp), jnp.int32),
                            pltpu.SemaphoreType.DMA, pltpu.SemaphoreType.DMA],
        ),
        compiler_params=params, name="moe_dispatch",
    )(pad_start + counts, padded - counts, n_used, pos_tok, t)

    is_start = jnp.concatenate([jnp.ones((1,), bool), blk_e[1:] != blk_e[:-1]])
    run_id = jnp.cumsum(is_start.astype(jnp.int32)) - 1
    ids = jnp.arange(ne, dtype=jnp.int32)
    later_busy = jnp.where((counts[None, :] > 0) & (ids[None, :] > ids[:, None]), ids[None, :], ne)
    next_busy = jnp.min(later_busy, axis=1)
    next_e = jnp.sum(jnp.where(blk_e[:, None] == ids[None, :], next_busy[None, :], 0), axis=1)
    next_e = jnp.where(next_e >= ne, -1, next_e).astype(jnp.int32)

    def x_map(i, blk_e, run, nxt, nused):
        return (jnp.minimum(i, nused[0] - 1), 0)

    y_p = pl.pallas_call(
        functools.partial(_moe_ffn_kernel, layer=layer),
        out_shape=jax.ShapeDtypeStruct((p, dp), jnp.int32),
        grid_spec=pltpu.PrefetchScalarGridSpec(
            num_scalar_prefetch=4,
            grid=(n_blocks,),
            in_specs=[pl.BlockSpec((bm, dp), x_map),
                      pl.BlockSpec(memory_space=pl.ANY), pl.BlockSpec(memory_space=pl.ANY)],
            out_specs=pl.BlockSpec((bm, dp), lambda i, *_: (i, 0)),
            scratch_shapes=[pltpu.VMEM((2, d, ff2), F32), pltpu.VMEM((2, ff2 // 2, d), F32),
                            pltpu.VMEM((d, ff2), BF16), pltpu.VMEM((ff2 // 2, d), BF16),
                            pltpu.SemaphoreType.DMA((2, 2))],
        ),
        compiler_params=params, name="moe_expert_ffn",
    )(blk_e, run_id, next_e, n_used, xs, w_gu, w_down)
    rows = pl.BlockSpec((ts, d), lambda i: (i, 0))
    whole = lambda a: pl.BlockSpec(a.shape, lambda i: (0,) * a.ndim)
    last_gate = g2_rows.shape[0] - 1
    gate_spec = pl.BlockSpec((None, 1, d), lambda i: (jnp.minimum(i // (rows_per_gate // ts), last_gate), 0, 0))
    consts = [sh_gu.astype(BF16), sh_down.astype(BF16), gain.reshape(1, d)]
    return pl.pallas_call(
        _combine_kernel,
        out_shape=jax.ShapeDtypeStruct((n, d), F32),
        grid=(n // ts,),
        in_specs=([pos_spec, pl.BlockSpec((ts, TOP_K), lambda i: (i, 0)), rows, rows]
                  + [whole(a) for a in consts] + [gate_spec, pl.BlockSpec(memory_space=pl.ANY)]),
        out_specs=rows,
        scratch_shapes=[pltpu.VMEM((TOP_K, ts, dp), jnp.int32), pltpu.SemaphoreType.DMA],
        compiler_params=params, name="moe_combine",
    )(pos_tok, wsel.T, t, x_res, *consts, g2_rows, y_p)


def kernel(x, c, ctx, c_ctx, ada_w, ada_b, norm_g, w_in, w_out, hy_conv_w, hy_conv_b, hy_ffn_w1, hy_ffn_b1, hy_ffn_w2, hy_ffn_b2, hy_ffn_w3, hy_freq, hy_bias, hy_norm, hg_lb_logits, hg_norm, gla_a_up, gla_a_b, gla_norm, ml_conv_w, ml_conv_b, ml_gate_b, ml_norm, router_w, router_b, exp_w_gu, exp_w_down, sh_w_gu, sh_w_down):
    bsz, seq, d = x.shape
    n_ctx = ctx.shape[1]
    depth = ada_w.shape[0]
    rows = seq // GRID_W
    x = x + _pos_embed_2d(rows, d)[None]
    xc = ctx
    lb_cum = jnp.cumsum(jax.nn.softmax(hg_lb_logits, axis=0), axis=0)
    lower_bounds = lb_cum - lb_cum[0:1]
    for l in range(depth):
        with_ctx = l < depth - 1
        mod = (jax.nn.silu(c) @ ada_w[l] + ada_b[l])[:, None, :]
        mod_c = jax.nn.silu(c_ctx) @ ada_w[l] + ada_b[l]
        sh1, sc1, g1, sh2, sc2, g2 = jnp.split(mod, 6, axis=-1)
        csh1, csc1, cg1, csh2, csc2, cg2 = jnp.split(mod_c, 6, axis=-1)
        w_in_l = _arrange_w_in(w_in[l])
        ctx_rows = lambda a: jnp.broadcast_to(a.reshape(1, 1, d), (bsz, 1, d))
        pl3 = _in_proj(x, norm_g[l, 0], sc1, sh1, w_in_l)
        pc3 = _in_proj(xc, norm_g[l, 0], ctx_rows(csc1), ctx_rows(csh1), w_in_l)
        hy_args = (hy_conv_w[l], hy_conv_b[l], hy_ffn_w1[l], hy_ffn_b1[l], hy_ffn_w2[l], hy_ffn_b2[l],
                   hy_ffn_w3[l], hy_freq[l], hy_bias[l], hy_norm[l])
        y_hy = _hyena_long(pl3, *hy_args)
        oc_hg, o_hg = _hgrn2(pc3, pl3, lower_bounds[l])
        oc_gla, o_gla = _gla(pc3, pl3, gla_a_up[l], gla_a_b[l])
        hc_ml, h_ml = _mlstm(pc3, pl3, ml_conv_w[l], ml_conv_b[l], ml_gate_b[l])
        head_gains = jnp.stack([hg_norm[l], gla_norm[l], ml_norm[l]])
        out_args = (w_out[l], head_gains, norm_g[l, 1], norm_g[l, 2])
        x, h = _mixer_out(y_hy, o_hg, o_gla, h_ml, pl3, x, *out_args, jnp.concatenate([g1, sc2, sh2], axis=1))
        moe_args = (router_w[l], router_b[l], exp_w_gu, exp_w_down, sh_w_gu[l], sh_w_down[l], l, norm_g[l, 3])
        if with_ctx:
            mod_ctx = jnp.broadcast_to(jnp.stack([cg1, csc2, csh2])[None], (bsz, 3, d))
            xc, hc = _mixer_out(_hyena(pc3, *hy_args), oc_hg, oc_gla, hc_ml, pc3, xc, *out_args, mod_ctx)
            tokens = lambda a, ac: jnp.concatenate([a.reshape(bsz * seq, d), ac.reshape(bsz * n_ctx, d)], axis=0)
            gates = jnp.concatenate([g2, cg2.reshape(1, 1, d)], axis=0)
            x_all = _moe(tokens(h, hc), tokens(x, xc), *moe_args, gates, seq)
            x = x_all[:bsz * seq].reshape(bsz, seq, d)
            xc = x_all[bsz * seq:].reshape(bsz, n_ctx, d)
        else:
            x = _moe(h.reshape(bsz * seq, d), x.reshape(bsz * seq, d), *moe_args, g2, seq).reshape(bsz, seq, d)
    return x
```

```python
import functools
import math

import jax
import jax.numpy as jnp
import numpy as np
from jax import lax
from jax.experimental import pallas as pl
from jax.experimental.pallas import tpu as pltpu

F32 = jnp.float32
BF16 = jnp.bfloat16

D_MODEL = 1024
GRID_W = 64
EPS = 1e-6
POS_BASE = 10000.0
GROUP_W = D_MODEL // 4
SHORT_CONV = 3
HY_W = GROUP_W
HY_ORDER = 2
HY_EMB = 33
HY_BANDS = (HY_EMB - 1) // 2
HY_FAST_DECAY = 0.3
HY_SLOW_DECAY = 1.5
HY_DECAY_TARGET = 1e-2
HG_H = 4
HG_W = GROUP_W
HG_DK = HG_W // HG_H
GLA_H = 4
GLA_KW = GROUP_W // 2
GLA_VW = GROUP_W
GLA_DK = GLA_KW // GLA_H
GLA_DV = GLA_VW // GLA_H
GLA_RANK = 16
GLA_NORMALIZER = 16.0
ML_H = 4
ML_W = GROUP_W
ML_DH = ML_W // ML_H
CHUNK_GATED = 16
CHUNK_ML = 64
TOP_K = 8
N_EXPERT_GROUPS = 8
TOPK_GROUPS = 4
EXPERT_FF = 256
ROUTED_SCALE = 2.5
IN_SPLITS = (HY_W, HY_W, HY_W,
             HG_W, HG_W, HG_W, HG_W, HG_W,
             GLA_KW, GLA_KW, GLA_VW, GLA_RANK, GLA_RANK, GLA_VW,
             ML_W, ML_W, ML_W, 4 * ML_H, ML_W)
P_ORDER = (0, 1, 2, 3, 4, 5, 6, 7, 8, 9, 10, 13, 14, 15, 16, 18, 11, 12, 17)
COL_HY = 0
COL_HG_Q, COL_HG_I, COL_HG_ZF, COL_HG_ZB, COL_HG_G = 3, 4, 5, 6, 7
COL_GLA_QK, COL_GLA_V, COL_GLA_R = 8, 9, 10
COL_ML_Q, COL_ML_K, COL_ML_V, COL_ML_O = 11, 12, 13, 14
N_WIDE = 15

LANE = 128
SUBLANE = 8
V7X_VMEM_BYTES = 64 * 1024 * 1024
ROW_TILE = 512
MOE_ROWS = 256
ROUTER_TILE = 256
SCATTER_TILE = 512
GLR_TILE_ELEMS = 128 * 256
VMEM_LIMIT = V7X_VMEM_BYTES * 7 // 8
NEG_INF = float("-inf")
COL_NARROW = N_WIDE * GROUP_W // LANE
NARROW_ML_GATES = N_WIDE * GROUP_W + 2 * GLA_RANK
P_WIDTH = N_WIDE * GROUP_W + LANE


def _arrange_w_in(w):
    offs = np.concatenate([[0], np.cumsum(IN_SPLITS)])
    cols = [w[:, offs[i]:offs[i + 1]] for i in P_ORDER]
    used = sum(IN_SPLITS)
    return jnp.concatenate(cols + [jnp.zeros((w.shape[0], P_WIDTH - used), w.dtype)], axis=1).astype(BF16)


def _in_proj_kernel(x_ref, g_ref, sc_ref, sh_ref, w_ref, o_ref):
    x = x_ref[...]
    y = x * lax.rsqrt(jnp.mean(x * x, axis=-1, keepdims=True) + EPS) * g_ref[...]
    h = y * (1.0 + sc_ref[...]) + sh_ref[...]
    o_ref[...] = jnp.dot(h.astype(BF16), w_ref[...], preferred_element_type=F32)


def _in_proj(x, gain, scale, shift, w):
    g, r, d = x.shape
    n = w.shape[1]
    tm = min(ROW_TILE, r)
    assert r % tm == 0
    return pl.pallas_call(
        _in_proj_kernel,
        out_shape=jax.ShapeDtypeStruct((g, r, n), F32),
        grid=(g, r // tm),
        in_specs=[pl.BlockSpec((None, tm, d), lambda b, i: (b, i, 0)),
                  pl.BlockSpec((1, d), lambda b, i: (0, 0)),
                  pl.BlockSpec((None, 1, d), lambda b, i: (b, 0, 0)),
                  pl.BlockSpec((None, 1, d), lambda b, i: (b, 0, 0)),
                  pl.BlockSpec((d, n), lambda b, i: (0, 0))],
        out_specs=pl.BlockSpec((None, tm, n), lambda b, i: (b, i, 0)),
        compiler_params=pltpu.CompilerParams(dimension_semantics=("arbitrary", "arbitrary"),
                                             vmem_limit_bytes=VMEM_LIMIT),
        name="input_projection",
    )(x, gain.reshape(1, d), scale, shift, w)


def _mixer_out_kernel(hy_ref, hgf_ref, hgb_ref, glf_ref, glb_ref, mlf_ref, mlb_ref, ghg_ref, ggl_ref, gml_ref,
                      x_ref, w_ref, hn_ref, n1_ref, n2_ref, mod_ref, xo_ref, h_ref):
    gw = hy_ref.shape[-1]
    seg = jnp.where(lax.broadcasted_iota(jnp.int32, (gw, gw), 0) // HG_DK
                    == lax.broadcasted_iota(jnp.int32, (gw, gw), 1) // HG_DK, 1.0 / HG_DK, 0.0)

    def head_norm(o, gain):
        ms = jnp.dot(o * o, seg, precision=lax.Precision.HIGHEST, preferred_element_type=F32)
        return o * lax.rsqrt(ms + EPS) * gain

    silu = lambda a: a * jax.nn.sigmoid(a)
    groups = (hy_ref[...],
              head_norm(hgf_ref[...] + hgb_ref[...], hn_ref[0:1]) * silu(ghg_ref[...]),
              head_norm(glf_ref[...] + glb_ref[...], hn_ref[1:2]) * silu(ggl_ref[...]),
              jax.nn.sigmoid(gml_ref[...]) * head_norm(mlf_ref[...] + mlb_ref[...], hn_ref[2:3]))
    y = None
    for i, part in enumerate(groups):
        term = jnp.dot(part.astype(BF16), w_ref[i * gw:(i + 1) * gw, :], preferred_element_type=F32)
        y = term if y is None else y + term
    rms = lambda a, g: a * lax.rsqrt(jnp.mean(a * a, axis=-1, keepdims=True) + EPS) * g
    x = x_ref[...] + mod_ref[0:1] * rms(y, n1_ref[...])
    xo_ref[...] = x
    h_ref[...] = rms(x, n2_ref[...]) * (1.0 + mod_ref[1:2]) + mod_ref[2:3]


def _mixer_out(y_hy, o_hg, o_gla, h_ml, p3, x, w_out, head_gains, gain1, gain2, mod):
    assert HG_DK == GLA_DV == ML_DH and HG_H == GLA_H == ML_H
    bsz, t, d = x.shape
    gw = GROUP_W
    tm = min(ROW_TILE, t)
    part = pl.BlockSpec((None, tm, gw), lambda b, i: (b, i, 0))
    gate = lambda c: pl.BlockSpec((None, tm, gw), lambda b, i: (b, i, c))
    full = pl.BlockSpec((None, tm, d), lambda b, i: (b, i, 0))
    whole = lambda a: pl.BlockSpec(a.shape, lambda b, i: (0,) * a.ndim)
    consts = [w_out.astype(BF16), head_gains, gain1.reshape(1, d), gain2.reshape(1, d)]
    return pl.pallas_call(
        _mixer_out_kernel,
        out_shape=[jax.ShapeDtypeStruct((bsz, t, d), F32), jax.ShapeDtypeStruct((bsz, t, d), F32)],
        grid=(bsz, t // tm),
        in_specs=([part] * 7 + [gate(COL_HG_G), gate(COL_GLA_R), gate(COL_ML_O), full]
                  + [whole(a) for a in consts] + [pl.BlockSpec((None, 3, d), lambda b, i: (b, 0, 0))]),
        out_specs=[full, full],
        compiler_params=pltpu.CompilerParams(dimension_semantics=("arbitrary", "arbitrary"),
                                             vmem_limit_bytes=VMEM_LIMIT),
        name="mixer_output",
    )(y_hy, *o_hg, *o_gla, *h_ml, p3, p3, p3, x, *consts, mod)


def _rms(x, g):
    return x * lax.rsqrt(jnp.mean(x * x, axis=-1, keepdims=True) + EPS) * g


def _short_conv_kernel(prev_ref, cur_ref, next_ref, w_ref, b_ref, o_ref, *, act):
    j = pl.program_id(1)
    u = cur_ref[...]
    tt = u.shape[0]
    row = lax.broadcasted_iota(jnp.int32, u.shape, 0)
    before = jnp.where(j > 0, prev_ref[SUBLANE - 1:SUBLANE, :], 0.0)
    after = jnp.where(j < pl.num_programs(1) - 1, next_ref[0:1, :], 0.0)
    up = jnp.where(row == 0, before, pltpu.roll(u, 1, 0))
    dn = jnp.where(row == tt - 1, after, pltpu.roll(u, tt - 1, 0))
    y = w_ref[0:1] * up + w_ref[1:2] * u + w_ref[2:3] * dn + b_ref[...]
    if act:
        y = y * jax.nn.sigmoid(y)
    o_ref[...] = y


def _short_conv(p3, col0, ncols, w, b, act):
    assert SHORT_CONV == 3
    bsz, t, _ = p3.shape
    tt = min(ROW_TILE, t)
    halo = tt // SUBLANE
    last = t // SUBLANE - 1
    gw = GROUP_W
    cur = pl.BlockSpec((None, tt, gw), lambda bi, j, c: (bi, j, col0 + c))
    prev = pl.BlockSpec((None, SUBLANE, gw), lambda bi, j, c: (bi, jnp.maximum(j * halo - 1, 0), col0 + c))
    nxt = pl.BlockSpec((None, SUBLANE, gw), lambda bi, j, c: (bi, jnp.minimum((j + 1) * halo, last), col0 + c))
    return pl.pallas_call(
        functools.partial(_short_conv_kernel, act=act),
        out_shape=jax.ShapeDtypeStruct((bsz, t, ncols * gw), F32),
        grid=(bsz, t // tt, ncols),
        in_specs=[prev, cur, nxt, pl.BlockSpec((SHORT_CONV, gw), lambda bi, j, c: (0, c)),
                  pl.BlockSpec((1, gw), lambda bi, j, c: (0, c))],
        out_specs=pl.BlockSpec((None, tt, gw), lambda bi, j, c: (bi, j, c)),
        compiler_params=pltpu.CompilerParams(dimension_semantics=("arbitrary",) * 3, vmem_limit_bytes=VMEM_LIMIT),
        name="short_conv",
    )(p3, p3, p3, w, b.reshape(1, ncols * gw))


def _pos_embed_2d(rows, d):
    r = jnp.repeat(jnp.arange(rows, dtype=F32), GRID_W)
    col = (jnp.arange(rows * GRID_W) % GRID_W).astype(F32)
    quarter = d // 4
    omega = 1.0 / (POS_BASE ** (jnp.arange(quarter, dtype=F32) / quarter))

    def axis_emb(p):
        ang = p[:, None] * omega[None, :]
        return jnp.concatenate([jnp.sin(ang), jnp.cos(ang)], axis=-1)

    return jnp.concatenate([axis_emb(r), axis_emb(col)], axis=-1)


def _hyena_spectra(L, w1, b1, w2, b2, w3, freq):
    t = jnp.linspace(0.0, 1.0, L, dtype=F32)[:, None]
    w = 2.0 * math.pi * jnp.arange(L, dtype=F32)[:, None] / L
    bands = jnp.linspace(1e-4, HY_BANDS - 1, HY_BANDS, dtype=F32)[None, :]
    feats = jnp.concatenate([t, jnp.cos(bands * w), -jnp.sin(bands * w)], axis=-1)
    z = jnp.sin(freq[0] * (feats @ w1 + b1))
    z = jnp.sin(freq[1] * (z @ w2 + b2))
    h = (z @ w3).reshape(L, HY_ORDER, 2, HY_W)
    max_decay = math.log(HY_DECAY_TARGET) / HY_FAST_DECAY
    min_decay = math.log(HY_DECAY_TARGET) / HY_SLOW_DECAY
    deltas = jnp.abs(jnp.linspace(min_decay, max_decay, HY_W, dtype=F32))
    h = h * jnp.exp(-t[:, :, None, None] * deltas)
    fwd = h[:, :, 0]
    bwd = h[1:, :, 1][::-1]
    l1 = jnp.sum(jnp.abs(fwd), axis=0) + jnp.sum(jnp.abs(bwd), axis=0)
    filt = jnp.concatenate([fwd, jnp.zeros((1, HY_ORDER, HY_W), F32), bwd], axis=0) / l1
    return jnp.fft.rfft(filt, axis=0)


def _fft_conv(u, spec, bias):
    L = u.shape[1]
    y = jnp.fft.irfft(jnp.fft.rfft(u, n=2 * L, axis=1) * spec, n=2 * L, axis=1)[:, :L]
    return y + u * bias


def _hyena(p3, conv_w, conv_b, w1, b1, w2, b2, w3, freq, bias, norm_g):
    u = _short_conv(p3, COL_HY, 3, conv_w, conv_b, False)
    v, x1, x2 = u[..., :HY_W], u[..., HY_W:2 * HY_W], u[..., 2 * HY_W:]
    spec = _hyena_spectra(u.shape[1], w1, b1, w2, b2, w3, freq)
    z = x1 * _fft_conv(v, spec[:, 0], bias[0])
    y = x2 * _fft_conv(z, spec[:, 1], bias[1])
    return _rms(y, norm_g)


FFT_N1 = 128
FFT_N2 = 128
FFT_KTILE = 8
FFT_NTILE = 4096


def _dft_tables(n1, n2):
    n = n1 * n2
    k = np.arange(n1)
    f1 = np.exp(-2j * np.pi * np.outer(k, k) / n1)
    f2 = np.exp(-2j * np.pi * np.outer(np.arange(n2), np.arange(n2)) / n2)
    tw = np.exp(-2j * np.pi * np.outer(np.arange(n1), np.arange(n2)) / n)
    as32 = lambda a: jnp.asarray(np.ascontiguousarray(a), F32)
    f1_fwd = as32(np.concatenate([f1.real, f1.imag], axis=0))
    f1_inv = as32(np.concatenate([f1.real, f1.imag], axis=1) / n)
    f2_inv = as32(np.block([[f2.real, f2.imag], [-f2.imag, f2.real]]))
    return f1_fwd, f1_inv, as32(f2.real), as32(f2.imag), f2_inv, as32(tw.real), as32(tw.imag)


def _stage_kernel(w_ref, x_ref, o_ref):
    o_ref[...] = jnp.dot(w_ref[...].astype(BF16), x_ref[...].astype(BF16),
                         preferred_element_type=F32).astype(o_ref.dtype)


def _stage_matmul(w, x, out_dtype):
    g, k, n = x.shape
    m = w.shape[0]
    tn = FFT_NTILE
    return pl.pallas_call(
        _stage_kernel,
        out_shape=jax.ShapeDtypeStruct((g, m, n), out_dtype),
        grid=(g, n // tn),
        in_specs=[pl.BlockSpec((m, k), lambda b, j: (0, 0)), pl.BlockSpec((None, k, tn), lambda b, j: (b, 0, j))],
        out_specs=pl.BlockSpec((None, m, tn), lambda b, j: (b, 0, j)),
        compiler_params=pltpu.CompilerParams(dimension_semantics=("arbitrary", "arbitrary"),
                                             vmem_limit_bytes=VMEM_LIMIT),
        name="dft_stage",
    )(w, x)


def _twiddled_f2(f2r, f2i, tr, ti):
    gr = f2r * tr - f2i * ti
    gi = f2r * ti + f2i * tr
    return jnp.concatenate([jnp.concatenate([gr, -gi], axis=1), jnp.concatenate([gi, gr], axis=1)], axis=0)


def _spectrum_kernel(a_ref, f2r_ref, f2i_ref, tr_ref, ti_ref, x_ref):
    n2 = f2r_ref.shape[0]
    tr = tr_ref[...]
    ti = ti_ref[...]
    for i in range(a_ref.shape[0]):
        g = _twiddled_f2(f2r_ref[...], f2i_ref[...], tr[i:i + 1], ti[i:i + 1])
        x_ref[i] = jnp.dot(g.astype(BF16), a_ref[i], preferred_element_type=F32)


def _conv_mid_kernel(a_ref, h_ref, f2r_ref, f2i_ref, f2inv_ref, tr_ref, ti_ref, z_ref):
    n2 = f2r_ref.shape[0]
    tr = tr_ref[...]
    ti = ti_ref[...]
    tr_col = tr.T
    ti_col = ti.T
    f2inv = f2inv_ref[...].astype(BF16)
    for i in range(a_ref.shape[0]):
        g = _twiddled_f2(f2r_ref[...], f2i_ref[...], tr[i:i + 1], ti[i:i + 1])
        x = jnp.dot(g.astype(BF16), a_ref[i], preferred_element_type=F32)
        xr, xi = x[:n2], x[n2:]
        hr, hi = h_ref[i, :n2], h_ref[i, n2:]
        y = jnp.concatenate([hr * xr - hi * xi, hr * xi + hi * xr], axis=0)
        w = jnp.dot(f2inv, y.astype(BF16), preferred_element_type=F32)
        wr, wi = w[:n2], w[n2:]
        cr, ci = tr_col[:, i:i + 1], ti_col[:, i:i + 1]
        z_ref[i] = jnp.concatenate([cr * wr + ci * wi, cr * wi - ci * wr], axis=0).astype(z_ref.dtype)


def _dft_mid_specs(g, c):
    n1, n2, kt = FFT_N1, FFT_N2, FFT_KTILE
    blk = pl.BlockSpec((None, kt, 2 * n2, c), lambda b, j: (b, j, 0, 0))
    const = lambda r, cc: pl.BlockSpec((r, cc), lambda b, j: (0, 0))
    twid = pl.BlockSpec((kt, n2), lambda b, j: (j, 0))
    params = pltpu.CompilerParams(dimension_semantics=("arbitrary", "arbitrary"), vmem_limit_bytes=VMEM_LIMIT)
    return blk, const, twid, params, (g, n1 // kt)


def _to_k1_major(a2d, c):
    g = a2d.shape[0]
    return a2d.reshape(g, 2, FFT_N1, FFT_N2, c).transpose(0, 2, 1, 3, 4).reshape(g, FFT_N1, 2 * FFT_N2, c)


def _filter_spectrum(filt, tables):
    g, n, c = filt.shape
    f1_fwd, _, f2r, f2i, _, twr, twi = tables
    a = _stage_matmul(f1_fwd, filt.reshape(g, FFT_N1, FFT_N2 * c), BF16)
    blk, const, twid, params, grid = _dft_mid_specs(g, c)
    return pl.pallas_call(
        _spectrum_kernel,
        out_shape=jax.ShapeDtypeStruct((g, FFT_N1, 2 * FFT_N2, c), F32),
        grid=grid,
        in_specs=[blk, const(FFT_N2, FFT_N2), const(FFT_N2, FFT_N2), twid, twid],
        out_specs=blk, compiler_params=params, name="dft_spectrum",
    )(_to_k1_major(a, c), f2r, f2i, twr, twi)


def _long_conv(u, spec, tables):
    g, l, c = u.shape
    f1_fwd, f1_inv, f2r, f2i, f2inv, twr, twi = tables
    half = l // FFT_N2
    a = _stage_matmul(f1_fwd[:, :half], u.reshape(g, half, FFT_N2 * c), BF16)
    blk, const, twid, params, grid = _dft_mid_specs(g, c)
    hspec = pl.BlockSpec((FFT_KTILE, 2 * FFT_N2, c), lambda b, j: (j, 0, 0))
    z = pl.pallas_call(
        _conv_mid_kernel,
        out_shape=jax.ShapeDtypeStruct((g, FFT_N1, 2 * FFT_N2, c), BF16),
        grid=grid,
        in_specs=[blk, hspec, const(FFT_N2, FFT_N2), const(FFT_N2, FFT_N2), const(2 * FFT_N2, 2 * FFT_N2),
                  twid, twid],
        out_specs=blk, compiler_params=params, name="dft_conv_mid",
    )(_to_k1_major(a, c), spec, f2r, f2i, f2inv, twr, twi)
    z2d = z.reshape(g, FFT_N1, 2, FFT_N2, c).transpose(0, 2, 1, 3, 4).reshape(g, 2 * FFT_N1, FFT_N2 * c)
    y = _stage_matmul(f1_inv[:half], z2d, F32)
    return y.reshape(g, l, c)


def _hyena_filters(L, w1, b1, w2, b2, w3, freq):
    t = jnp.linspace(0.0, 1.0, L, dtype=F32)[:, None]
    w = 2.0 * math.pi * jnp.arange(L, dtype=F32)[:, None] / L
    bands = jnp.linspace(1e-4, HY_BANDS - 1, HY_BANDS, dtype=F32)[None, :]
    feats = jnp.concatenate([t, jnp.cos(bands * w), -jnp.sin(bands * w)], axis=-1)
    max_decay = math.log(HY_DECAY_TARGET) / HY_FAST_DECAY
    min_decay = math.log(HY_DECAY_TARGET) / HY_SLOW_DECAY
    deltas = jnp.abs(jnp.linspace(min_decay, max_decay, HY_W, dtype=F32))
    w3d = w3.reshape(w3.shape[0], HY_ORDER, 2, HY_W)

    def side(f, tt, direction):
        z = jnp.sin(freq[0] * (f @ w1 + b1))
        z = jnp.sin(freq[1] * (z @ w2 + b2))
        h = (z @ w3d[:, :, direction].reshape(w3.shape[0], HY_ORDER * HY_W)).reshape(-1, HY_ORDER, HY_W)
        return h * jnp.exp(-tt[:, :, None] * deltas)

    fwd = side(feats, t, 0)
    bwd = side(feats[::-1], t[::-1], 1)[:L - 1]
    l1 = jnp.sum(jnp.abs(fwd), axis=0) + jnp.sum(jnp.abs(bwd), axis=0)
    return jnp.concatenate([fwd, jnp.zeros((1, HY_ORDER, HY_W), F32), bwd], axis=0) / l1


def _hyena_long(p3, conv_w, conv_b, w1, b1, w2, b2, w3, freq, bias, norm_g):
    u = _short_conv(p3, COL_HY, 3, conv_w, conv_b, False)
    v, x1, x2 = u[..., :HY_W], u[..., HY_W:2 * HY_W], u[..., 2 * HY_W:]
    L = u.shape[1]
    assert 2 * L == FFT_N1 * FFT_N2
    tables = _dft_tables(FFT_N1, FFT_N2)
    filt = _hyena_filters(L, w1, b1, w2, b2, w3, freq)
    spec = _filter_spectrum(jnp.moveaxis(filt, 1, 0), tables)
    z = x1 * (_long_conv(v, spec[0], tables) + v * bias[0])
    y = x2 * (_long_conv(z, spec[1], tables) + z * bias[1])
    return _rms(y, norm_g)


def _glr_direction(q, k, v, g, st_ref, reverse, nh):
    tc, hk = k.shape
    hv = v.shape[1]
    c = CHUNK_GATED
    hi = lax.Precision.HIGHEST
    ti = lax.broadcasted_iota(jnp.int32, (tc, tc), 0)
    tj = lax.broadcasted_iota(jnp.int32, (tc, tc), 1)
    same = (ti // c) == (tj // c)
    seen = (tj >= ti) if reverse else (tj <= ti)
    bcum = jnp.dot(jnp.where(same, jnp.where(seen, 1.0, 0.0), 0.0), g, precision=hi, preferred_element_type=F32)
    btot = jnp.dot(jnp.where(same, 1.0, 0.0), g, precision=hi, preferred_element_type=F32)
    qd = q * jnp.exp(bcum)
    kd = k * jnp.exp(btot - bcum)
    dec = jnp.exp(btot)
    head_sum = jnp.where(lax.broadcasted_iota(jnp.int32, (hk, hv), 0) // (hk // nh)
                         == lax.broadcasted_iota(jnp.int32, (hk, hv), 1) // (hv // nh), 1.0, 0.0).astype(BF16)
    in_chunk = lax.broadcasted_iota(jnp.int32, (tc, hk), 0) % c
    o = jnp.zeros((tc, hv), F32)
    for lag in range(c):
        if lag == 0:
            ks, bs, vs = k, bcum, v
        else:
            shift = tc - lag if reverse else lag
            ks, bs, vs = pltpu.roll(k, shift, 0), pltpu.roll(bcum, shift, 0), pltpu.roll(v, shift, 0)
        valid = (in_chunk + lag <= c - 1) if reverse else (in_chunk >= lag)
        x = q * ks * jnp.exp(jnp.where(valid, bcum - bs, NEG_INF))
        o = o + jnp.dot(x.astype(BF16), head_sum, preferred_element_type=F32) * vs
    head_mask = (lax.broadcasted_iota(jnp.int32, (hv, hk), 0) // (hv // nh)
                 == lax.broadcasted_iota(jnp.int32, (hv, hk), 1) // (hk // nh))
    st = st_ref[...]
    nch = tc // c
    outs = [None] * nch
    for ci in (range(nch - 1, -1, -1) if reverse else range(nch)):
        sl = slice(ci * c, (ci + 1) * c)
        outs[ci] = lax.dot_general(qd[sl].astype(BF16), st.astype(BF16), (((1,), (1,)), ((), ())),
                                   preferred_element_type=F32)
        ds = lax.dot_general(v[sl].astype(BF16), kd[sl].astype(BF16), (((0,), (0,)), ((), ())),
                             preferred_element_type=F32)
        st = st * dec[ci * c:ci * c + 1] + jnp.where(head_mask, ds, 0.0)
    st_ref[...] = st
    return o + jnp.concatenate(outs, axis=0)


def _log_sigmoid(z):
    return jnp.minimum(z, 0.0) - jnp.log1p(jnp.exp(-jnp.abs(z)))


def _hgrn2_kernel(qf_ref, if_ref, zf_ref, qb_ref, ib_ref, zb_ref, lb_ref, s0f_ref, s0b_ref,
                  of_ref, ob_ref, sf_ref, sb_ref, stf, stb):
    j = pl.program_id(1)

    @pl.when(j == 0)
    def _():
        stf[...] = s0f_ref[...]
        stb[...] = s0b_ref[...]

    one_minus_lb, log_lb, log_ub = lb_ref[0:1], lb_ref[1:2], lb_ref[2:3]

    def gate(z):
        return one_minus_lb * jax.nn.sigmoid(-z), jnp.logaddexp(log_lb, log_ub + _log_sigmoid(z))

    silu = lambda a: a * jax.nn.sigmoid(a)
    k_f, g_f = gate(zf_ref[...])
    k_b, g_b = gate(zb_ref[...])
    of_ref[...] = _glr_direction(silu(qf_ref[...]), k_f, if_ref[...], g_f, stf, False, HG_H)
    ob_ref[...] = _glr_direction(silu(qb_ref[...]), k_b, ib_ref[...], g_b, stb, True, HG_H)

    @pl.when(j == pl.num_programs(1) - 1)
    def _():
        sf_ref[...] = stf[...]
        sb_ref[...] = stb[...]


def _gla_kernel(qkf_ref, vf_ref, nf_ref, qkb_ref, vb_ref, nb_ref, aup_ref, ab_ref, s0f_ref, s0b_ref,
                of_ref, ob_ref, sf_ref, sb_ref, stf, stb):
    j = pl.program_id(1)

    @pl.when(j == 0)
    def _():
        stf[...] = s0f_ref[...]
        stb[...] = s0b_ref[...]

    def gate(narrow, idx):
        a = narrow[:, idx * GLA_RANK:(idx + 1) * GLA_RANK]
        lin = jnp.dot(a.astype(BF16), aup_ref[idx].astype(BF16), preferred_element_type=F32) + ab_ref[idx]
        return _log_sigmoid(lin) / GLA_NORMALIZER

    qk_f = qkf_ref[...]
    qk_b = qkb_ref[...]
    of_ref[...] = _glr_direction(qk_f[:, :GLA_KW] * GLA_DK ** -0.5, qk_f[:, GLA_KW:], vf_ref[...],
                                 gate(nf_ref[...], 0), stf, False, GLA_H)
    ob_ref[...] = _glr_direction(qk_b[:, :GLA_KW] * GLA_DK ** -0.5, qk_b[:, GLA_KW:], vb_ref[...],
                                 gate(nb_ref[...], 1), stb, True, GLA_H)

    @pl.when(j == pl.num_programs(1) - 1)
    def _():
        sf_ref[...] = stf[...]
        sb_ref[...] = stb[...]


def _glr_call(kernel_fn, p3, fwd_cols, bwd_cols, consts, s0_f, s0_b, hk, hv, name):
    bsz, t, _ = p3.shape
    tc = min(GLR_TILE_ELEMS // hk, t)
    nsb = t // tc
    fwd = lambda w, c: pl.BlockSpec((None, tc, w), lambda b, j: (b, j, c))
    bwd = lambda w, c: pl.BlockSpec((None, tc, w), lambda b, j: (b, nsb - 1 - j, c))
    whole = lambda a: pl.BlockSpec(a.shape, lambda b, j: (0,) * a.ndim)
    st = pl.BlockSpec((None, hv, hk), lambda b, j: (b, 0, 0))
    out_f = pl.BlockSpec((None, tc, hv), lambda b, j: (b, j, 0))
    out_b = pl.BlockSpec((None, tc, hv), lambda b, j: (b, nsb - 1 - j, 0))
    return pl.pallas_call(
        kernel_fn,
        out_shape=[jax.ShapeDtypeStruct((bsz, t, hv), F32), jax.ShapeDtypeStruct((bsz, t, hv), F32),
                   jax.ShapeDtypeStruct((bsz, hv, hk), F32), jax.ShapeDtypeStruct((bsz, hv, hk), F32)],
        grid=(bsz, nsb),
        in_specs=([fwd(w, c) for w, c in fwd_cols] + [bwd(w, c) for w, c in bwd_cols]
                  + [whole(a) for a in consts] + [st, st]),
        out_specs=[out_f, out_b, st, st],
        scratch_shapes=[pltpu.VMEM((hv, hk), F32), pltpu.VMEM((hv, hk), F32)],
        compiler_params=pltpu.CompilerParams(dimension_semantics=("arbitrary", "arbitrary"),
                                             vmem_limit_bytes=VMEM_LIMIT),
        name=name,
    )(*([p3] * (len(fwd_cols) + len(bwd_cols))), *consts, s0_f, s0_b)


def _hgrn2_seq(p3, lb, s0_f, s0_b):
    lb_rows = jnp.stack([1.0 - lb, jnp.log(lb), jnp.log1p(-lb)])
    cols = lambda z: [(GROUP_W, COL_HG_Q), (GROUP_W, COL_HG_I), (GROUP_W, z)]
    o_f, o_b, s_f, s_b = _glr_call(_hgrn2_kernel, p3, cols(COL_HG_ZF), cols(COL_HG_ZB), [lb_rows],
                                   s0_f, s0_b, HG_W, HG_W, "hgrn2_recurrence")
    return (o_f, o_b), s_f, s_b


def _hgrn2(pc3, pl3, lb):
    s0 = jnp.zeros((pl3.shape[0], HG_W, HG_W), F32)
    oc, s_f, s_b = _hgrn2_seq(pc3, lb, s0, s0)
    o, _, _ = _hgrn2_seq(pl3, lb, s_f, s_b)
    return oc, o


def _gla_seq(p3, a_up, a_b, s0_f, s0_b):
    cols = [(GROUP_W, COL_GLA_QK), (GROUP_W, COL_GLA_V), (LANE, COL_NARROW)]
    o_f, o_b, s_f, s_b = _glr_call(_gla_kernel, p3, cols, cols, [a_up, a_b.reshape(2, 1, GLA_KW)],
                                   s0_f, s0_b, GLA_KW, GLA_VW, "gla_recurrence")
    return (o_f, o_b), s_f, s_b


def _gla(pc3, pl3, a_up, a_b):
    s0 = jnp.zeros((pl3.shape[0], GLA_VW, GLA_KW), F32)
    oc, s_f, s_b = _gla_seq(pc3, a_up, a_b, s0, s0)
    o, _, _ = _gla_seq(pl3, a_up, a_b, s_f, s_b)
    return oc, o


def _mlstm_direction(q, k, v, igx, lfx, s_ref, n_ref, m_ref, reverse, nh):
    tc, w = q.shape
    seg = w // nh
    assert tc == seg
    hi = lax.Precision.HIGHEST
    ti = lax.broadcasted_iota(jnp.int32, (tc, tc), 0)
    tj = lax.broadcasted_iota(jnp.int32, (tc, tc), 1)
    seen = (tj >= ti) if reverse else (tj <= ti)
    b = jnp.dot(jnp.where(seen, 1.0, 0.0), lfx, precision=hi, preferred_element_type=F32)
    bl = b[0:1] if reverse else b[tc - 1:tc]
    a = bl - b + igx
    ma = jnp.max(a, axis=0, keepdims=True)
    kw = jnp.exp(a - ma) * k
    s_prev = s_ref[...]
    n_prev = n_ref[...]
    m_prev = m_ref[...]
    lane = lax.broadcasted_iota(jnp.int32, (tc, w), 1)
    row = lax.broadcasted_iota(jnp.int32, (tc, w), 0)
    same_head = (lax.broadcasted_iota(jnp.int32, (w, w), 0) // seg
                 == lax.broadcasted_iota(jnp.int32, (w, w), 1) // seg)
    kexp = jnp.where(same_head, jnp.concatenate([k] * nh, axis=0), 0.0)
    vexp = jnp.where(same_head, jnp.concatenate([v] * nh, axis=0), 0.0)
    scores = lax.dot_general(q.astype(BF16), kexp.astype(BF16), (((1,), (1,)), ((), ())),
                             preferred_element_type=F32)
    s_lane = lane % seg
    by_src = jnp.sum(jnp.where(s_lane == row, igx - b, 0.0), axis=0, keepdims=True)
    ok = (s_lane >= row) if reverse else (s_lane <= row)
    dmat = jnp.where(ok, b + by_src, NEG_INF)
    inter = b + m_prev
    head_of_lane = lane // seg
    seg_max = jnp.full((tc, w), NEG_INF, F32)
    for h in range(nh):
        in_h = head_of_lane == h
        seg_max = jnp.where(in_h, jnp.max(jnp.where(in_h, dmat, NEG_INF), axis=1, keepdims=True), seg_max)
    m_t = jnp.maximum(inter, seg_max)
    wq = jnp.exp(dmat - m_t) * scores
    w_int = jnp.exp(inter - m_t)
    head_sum = jnp.where(same_head, 1.0, 0.0).astype(BF16)
    num = (jnp.dot(wq.astype(BF16), vexp.astype(BF16), preferred_element_type=F32)
           + w_int * jnp.dot(q.astype(BF16), s_prev.astype(BF16), preferred_element_type=F32))
    den = (jnp.dot(wq.astype(BF16), head_sum, preferred_element_type=F32)
           + w_int * jnp.dot((q * n_prev).astype(BF16), head_sum, preferred_element_type=F32))
    h_out = num / jnp.maximum(jnp.abs(den), jnp.exp(-m_t))
    m_new = jnp.maximum(bl + m_prev, ma)
    d_old = jnp.exp(bl + m_prev - m_new)
    d_new = jnp.exp(ma - m_new)
    ds = lax.dot_general(kw.astype(BF16), v.astype(BF16), (((0,), (0,)), ((), ())), preferred_element_type=F32)
    s_ref[...] = d_old * s_prev + d_new * jnp.where(same_head, ds, 0.0)
    n_ref[...] = d_old * n_prev + d_new * jnp.sum(kw, axis=0, keepdims=True)
    m_ref[...] = m_new
    return h_out


def _mlstm_kernel(qf_ref, kf_ref, vf_ref, igf_ref, lff_ref, qb_ref, kb_ref, vb_ref, igb_ref, lfb_ref,
                  s0f_ref, n0f_ref, m0f_ref, s0b_ref, n0b_ref, m0b_ref,
                  hf_ref, hb_ref, sf_ref, nf_ref, mf_ref, sb_ref, nb_ref, mb_ref,
                  s_f, n_f, m_f, s_b, n_b, m_b, *, nh):
    j = pl.program_id(1)

    @pl.when(j == 0)
    def _():
        s_f[...] = s0f_ref[...]
        n_f[...] = n0f_ref[...]
        m_f[...] = m0f_ref[...]
        s_b[...] = s0b_ref[...]
        n_b[...] = n0b_ref[...]
        m_b[...] = m0b_ref[...]

    hf_ref[...] = _mlstm_direction(qf_ref[...], kf_ref[...], vf_ref[...], igf_ref[...], lff_ref[...],
                                   s_f, n_f, m_f, False, nh)
    hb_ref[...] = _mlstm_direction(qb_ref[...], kb_ref[...], vb_ref[...], igb_ref[...], lfb_ref[...],
                                   s_b, n_b, m_b, True, nh)

    @pl.when(j == pl.num_programs(1) - 1)
    def _():
        sf_ref[...] = s_f[...]
        nf_ref[...] = n_f[...]
        mf_ref[...] = m_f[...]
        sb_ref[...] = s_b[...]
        nb_ref[...] = n_b[...]
        mb_ref[...] = m_b[...]


def _mlstm_bidir(q, k, v, ig_f, lf_f, ig_b, lf_b, st_f, st_b, nh):
    bsz, t, w = q.shape
    tc = CHUNK_ML
    nsb = t // tc
    fwd = pl.BlockSpec((None, tc, w), lambda b, j: (b, j, 0))
    bwd = pl.BlockSpec((None, tc, w), lambda b, j: (b, nsb - 1 - j, 0))
    mat = pl.BlockSpec((None, w, w), lambda b, j: (b, 0, 0))
    vec = pl.BlockSpec((None, 1, w), lambda b, j: (b, 0, 0))
    sds = jax.ShapeDtypeStruct
    state_shapes = [sds((bsz, w, w), F32), sds((bsz, 1, w), F32), sds((bsz, 1, w), F32)]
    outs = pl.pallas_call(
        functools.partial(_mlstm_kernel, nh=nh),
        out_shape=[sds((bsz, t, w), F32), sds((bsz, t, w), F32)] + state_shapes + state_shapes,
        grid=(bsz, nsb),
        in_specs=[fwd] * 5 + [bwd] * 5 + [mat, vec, vec] * 2,
        out_specs=[fwd, bwd] + [mat, vec, vec] * 2,
        scratch_shapes=[pltpu.VMEM((w, w), F32), pltpu.VMEM((1, w), F32), pltpu.VMEM((1, w), F32)] * 2,
        compiler_params=pltpu.CompilerParams(dimension_semantics=("arbitrary", "arbitrary"),
                                             vmem_limit_bytes=VMEM_LIMIT),
        name="mlstm_recurrence",
    )(q, k, v, ig_f, lf_f, q, k, v, ig_b, lf_b, *st_f, *st_b)
    return outs[0], outs[1], tuple(outs[2:5]), tuple(outs[5:8])


def _mlstm_seq(p3, conv_w, conv_b, gate_b, st_f, st_b):
    qk = _short_conv(p3, COL_ML_Q, 2, conv_w, conv_b, True)
    v = p3[..., COL_ML_V * GROUP_W:(COL_ML_V + 1) * GROUP_W]
    bsz, t, _ = p3.shape
    gates = p3[..., NARROW_ML_GATES:NARROW_ML_GATES + 4 * ML_H]
    gt = gates.reshape(bsz, t, 4, ML_H) + gate_b
    expand = lambda a: jnp.repeat(a, ML_DH, axis=-1)
    h_f, h_b, fin_f, fin_b = _mlstm_bidir(
        qk[..., :ML_W], qk[..., ML_W:] * ML_DH ** -0.5, v,
        expand(gt[:, :, 0]), expand(jax.nn.log_sigmoid(gt[:, :, 1])),
        expand(gt[:, :, 2]), expand(jax.nn.log_sigmoid(gt[:, :, 3])), st_f, st_b, ML_H)
    return (h_f, h_b), fin_f, fin_b


def _mlstm(pc3, pl3, conv_w, conv_b, gate_b):
    bsz = pl3.shape[0]
    st0 = (jnp.zeros((bsz, ML_W, ML_W), F32), jnp.zeros((bsz, 1, ML_W), F32), jnp.zeros((bsz, 1, ML_W), F32))
    hc, st_f, st_b = _mlstm_seq(pc3, conv_w, conv_b, gate_b, st0, st0)
    h, _, _ = _mlstm_seq(pl3, conv_w, conv_b, gate_b, st_f, st_b)
    return hc, h


def _router_kernel(h_ref, wt_ref, b_ref, eidx_ref, wsel_ref, cnt_ref):
    i = pl.program_id(0)
    tm = h_ref.shape[0]
    ne = wt_ref.shape[0]
    per_group = ne // N_EXPERT_GROUPS
    logits = lax.dot_general(wt_ref[...], h_ref[...], (((1,), (1,)), ((), ())),
                             preferred_element_type=F32, precision=lax.Precision.HIGHEST)
    s = jax.nn.sigmoid(logits)
    sel = s + b_ref[...]
    row = lax.broadcasted_iota(jnp.int32, (ne, tm), 0)
    gs = []
    for g in range(N_EXPERT_GROUPS):
        blk = sel[g * per_group:(g + 1) * per_group]
        r = lax.broadcasted_iota(jnp.int32, blk.shape, 0)
        m1 = jnp.max(blk, axis=0, keepdims=True)
        i1 = jnp.min(jnp.where(blk == m1, r, per_group), axis=0, keepdims=True)
        m2 = jnp.max(jnp.where(r == i1, NEG_INF, blk), axis=0, keepdims=True)
        gs.append(m1 + m2)
    grp = jnp.concatenate(gs, axis=0)
    grow = lax.broadcasted_iota(jnp.int32, grp.shape, 0)
    gsel = jnp.zeros(grp.shape, F32)
    for _ in range(TOPK_GROUPS):
        m = jnp.max(grp, axis=0, keepdims=True)
        gi = jnp.min(jnp.where(grp == m, grow, N_EXPERT_GROUPS), axis=0, keepdims=True)
        hit = grow == gi
        gsel = jnp.where(hit, 1.0, gsel)
        grp = jnp.where(hit, NEG_INF, grp)
    masked = jnp.concatenate(
        [jnp.where(gsel[g:g + 1] > 0.0, sel[g * per_group:(g + 1) * per_group], NEG_INF)
         for g in range(N_EXPERT_GROUPS)], axis=0)
    eis, ws = [], []
    picked = jnp.zeros((ne, tm), F32)
    for _ in range(TOP_K):
        m = jnp.max(masked, axis=0, keepdims=True)
        ei = jnp.min(jnp.where(masked == m, row, ne), axis=0, keepdims=True)
        hit = row == ei
        ws.append(jnp.sum(jnp.where(hit, s, 0.0), axis=0, keepdims=True))
        eis.append(ei)
        picked = jnp.where(hit, 1.0, picked)
        masked = jnp.where(hit, NEG_INF, masked)
    w = jnp.concatenate(ws, axis=0)
    eidx_ref[...] = jnp.concatenate(eis, axis=0)
    wsel_ref[...] = w / jnp.sum(w, axis=0, keepdims=True) * ROUTED_SCALE
    tot = jnp.dot(picked.astype(BF16), jnp.ones((tm, LANE), BF16), preferred_element_type=F32)

    @pl.when(i == 0)
    def _():
        cnt_ref[...] = jnp.zeros_like(cnt_ref)

    cnt_ref[...] += tot


def _pos_kernel(eidx_ref, base_ref, pos_ref, carry_ref):
    i = pl.program_id(0)
    tm = eidx_ref.shape[1]
    ne = base_ref.shape[0]

    @pl.when(i == 0)
    def _():
        carry_ref[...] = jnp.zeros_like(carry_ref)

    eidx = eidx_ref[...]
    row = lax.broadcasted_iota(jnp.int32, (ne, tm), 0)
    picked = jnp.zeros((ne, tm), F32)
    for k in range(TOP_K):
        picked = jnp.where(row == eidx[k:k + 1], 1.0, picked)
    pb = picked.astype(BF16)
    before = jnp.where(lax.broadcasted_iota(jnp.int32, (tm, tm), 0) < lax.broadcasted_iota(jnp.int32, (tm, tm), 1),
                       1.0, 0.0).astype(BF16)
    rank = jnp.dot(pb, before, preferred_element_type=F32)
    tot = jnp.dot(pb, jnp.ones((tm, LANE), BF16), preferred_element_type=F32)
    dest = rank + (base_ref[...] + carry_ref[:, 0:1])
    pos = [jnp.sum(jnp.where(row == eidx[k:k + 1], dest, 0.0), axis=0, keepdims=True) for k in range(TOP_K)]
    pos_ref[...] = jnp.concatenate(pos, axis=0).astype(jnp.int32)
    carry_ref[...] += tot


def _pack_bf16_pairs(x):
    half = x.shape[1] // 2
    bits = lambda a: pltpu.bitcast(a.astype(BF16).astype(F32), jnp.int32)
    return (bits(x[:, half:]) & -65536) | lax.shift_right_logical(bits(x[:, :half]), 16)


def _unpack_bf16_pairs(w):
    return pltpu.bitcast(lax.shift_left(w, 16), F32), pltpu.bitcast(w & -65536, F32)


def _rows_to_tiles(ref, words):
    for s in range(ref.shape[-2]):
        ref[:, s, :] = words[:, s * LANE:(s + 1) * LANE]


def _tiles_to_rows(ref):
    return jnp.concatenate([ref[:, s, :] for s in range(ref.shape[-2])], axis=1)


def _dispatch_kernel(zstart_ref, zlen_ref, nused_ref, pos_ref, h_ref, xs_ref, packed, zeros, sem, zsem, *,
                     n_blocks):
    tm = h_ref.shape[0]
    bm = zeros.shape[0]
    _rows_to_tiles(packed, _pack_bf16_pairs(h_ref[...]))

    @pl.when(pl.program_id(0) == 0)
    def _():
        zeros[...] = jnp.zeros_like(zeros)

        def zero_copy(start, size):
            return pltpu.make_async_copy(zeros.at[pl.ds(0, size)], xs_ref.at[pl.ds(start, size)], zsem)

        def pieces(e, act):
            start = zstart_ref[e]
            rem = zlen_ref[e]
            ragged = rem & (SUBLANE - 1)
            for q in range(SUBLANE - 1):
                @pl.when(q < ragged)
                def _(q=q):
                    act(zero_copy(start + q, 1))

            start = pl.multiple_of(start + ragged, SUBLANE)
            size = bm // 2
            while size >= SUBLANE:
                @pl.when((rem & size) != 0)
                def _(start=start, size=size):
                    act(zero_copy(start, size))

                start = pl.multiple_of(start + (rem & size), SUBLANE)
                size //= 2

        def loop(act):
            def per_expert(e, carry):
                pieces(e, act)
                return carry

            def per_block(b, carry):
                act(zero_copy(pl.multiple_of(b * bm, bm), bm))
                return carry

            lax.fori_loop(0, zstart_ref.shape[0], per_expert, 0)
            lax.fori_loop(nused_ref[0], n_blocks, per_block, 0)

        loop(lambda cp: cp.start())
        loop(lambda cp: cp.wait())

    def row_copy(n, k):
        return pltpu.make_async_copy(packed.at[n], xs_ref.at[pos_ref[n * TOP_K + k]], sem)

    def issue(n, carry):
        for k in range(TOP_K):
            row_copy(n, k).start()
        return carry

    lax.fori_loop(0, tm, issue, 0)
    for _ in range(TOP_K):
        pltpu.make_async_copy(packed, xs_ref.at[pl.ds(0, tm)], sem).wait()


def _moe_ffn_kernel(blk_e_ref, run_ref, next_e_ref, nused_ref, x_ref, wgu_hbm, wdn_hbm, o_ref,
                    wgu_f, wdn_f, wgu_s, wdn_s, sems, *, layer):
    i = pl.program_id(0)
    e = blk_e_ref[i]
    run = run_ref[i]
    first_of_run = (i == 0) | (run != run_ref[jnp.maximum(i - 1, 0)])

    def fetch(expert, slot):
        return (pltpu.make_async_copy(wgu_hbm.at[layer, expert], wgu_f.at[slot], sems.at[0, slot]),
                pltpu.make_async_copy(wdn_hbm.at[layer, expert], wdn_f.at[slot], sems.at[1, slot]))

    @pl.when(i < nused_ref[0])
    def _():
        @pl.when(first_of_run)
        def _():
            slot = run & 1

            @pl.when(i == 0)
            def _():
                for cp in fetch(e, 0):
                    cp.start()

            for cp in fetch(e, slot):
                cp.wait()
            wgu_s[...] = wgu_f[slot].astype(BF16)
            wdn_s[...] = wdn_f[slot].astype(BF16)

            @pl.when(next_e_ref[i] >= 0)
            def _():
                for cp in fetch(next_e_ref[i], 1 - slot):
                    cp.start()

        x_lo, x_hi = _unpack_bf16_pairs(_tiles_to_rows(x_ref))
        half = x_lo.shape[1]
        au = (jnp.dot(x_lo.astype(BF16), wgu_s[:half, :], preferred_element_type=F32)
              + jnp.dot(x_hi.astype(BF16), wgu_s[half:, :], preferred_element_type=F32))
        a = au[:, :EXPERT_FF]
        u = au[:, EXPERT_FF:]
        h = (a * jax.nn.sigmoid(a)) * u
        _rows_to_tiles(o_ref, _pack_bf16_pairs(jnp.dot(h.astype(BF16), wdn_s[...], preferred_element_type=F32)))

    @pl.when(i >= nused_ref[0])
    def _():
        o_ref[...] = jnp.zeros_like(o_ref)


def _combine_kernel(pos_ref, w_ref, t_ref, x_ref, sgu_ref, sdn_ref, gain_ref, g2_ref, y_hbm, o_ref, buf, sem):
    tm = o_ref.shape[0]

    def row_copy(n, k):
        return pltpu.make_async_copy(y_hbm.at[pos_ref[n * TOP_K + k]], buf.at[k, n], sem)

    def issue(n, carry):
        for k in range(TOP_K):
            row_copy(n, k).start()
        return carry

    lax.fori_loop(0, tm, issue, 0)
    ff = sdn_ref.shape[0]
    au = jnp.dot(t_ref[...].astype(BF16), sgu_ref[...], preferred_element_type=F32)
    a, u = au[:, :ff], au[:, ff:]
    f = jnp.dot(((a * jax.nn.sigmoid(a)) * u).astype(BF16), sdn_ref[...], preferred_element_type=F32)
    for k in range(TOP_K):
        pltpu.make_async_copy(y_hbm.at[pl.ds(0, tm)], buf.at[k], sem).wait()
    r_lo, r_hi = None, None
    for k in range(TOP_K):
        y_lo, y_hi = _unpack_bf16_pairs(_tiles_to_rows(buf.at[k]))
        wk = w_ref[:, k:k + 1]
        r_lo = y_lo * wk if r_lo is None else r_lo + y_lo * wk
        r_hi = y_hi * wk if r_hi is None else r_hi + y_hi * wk
    f = f + jnp.concatenate([r_lo, r_hi], axis=1)
    o_ref[...] = x_ref[...] + g2_ref[...] * (f * lax.rsqrt(jnp.mean(f * f, axis=-1, keepdims=True) + EPS)
                                             * gain_ref[...])


def _moe(t, x_res, router_w, router_b, w_gu, w_down, sh_gu, sh_down, layer, gain, g2_rows, rows_per_gate):
    n, d = t.shape
    ne = router_w.shape[1]
    ff2 = w_gu.shape[-1]
    params = pltpu.CompilerParams(dimension_semantics=("arbitrary",), vmem_limit_bytes=VMEM_LIMIT)
    tm = ROUTER_TILE
    eidx, wsel, cnt = pl.pallas_call(
        _router_kernel,
        out_shape=[jax.ShapeDtypeStruct((TOP_K, n), jnp.int32), jax.ShapeDtypeStruct((TOP_K, n), F32),
                   jax.ShapeDtypeStruct((ne, LANE), F32)],
        grid=(n // tm,),
        in_specs=[pl.BlockSpec((tm, d), lambda i: (i, 0)), pl.BlockSpec((ne, d), lambda i: (0, 0)),
                  pl.BlockSpec((ne, 1), lambda i: (0, 0))],
        out_specs=[pl.BlockSpec((TOP_K, tm), lambda i: (0, i)), pl.BlockSpec((TOP_K, tm), lambda i: (0, i)),
                   pl.BlockSpec((ne, LANE), lambda i: (0, 0))],
        compiler_params=params, name="moe_router",
    )(t, router_w.T, router_b.reshape(ne, 1))
    bm = MOE_ROWS
    counts = cnt[:, 0].astype(jnp.int32)
    padded = (counts + bm - 1) // bm * bm
    pad_end = jnp.cumsum(padded)
    pad_start = pad_end - padded
    n_blocks = (n * TOP_K + ne * (bm - 1)) // bm + 1
    blk_first = jnp.arange(n_blocks, dtype=jnp.int32) * bm
    blk_e = jnp.minimum(jnp.sum((pad_end[None, :] <= blk_first[:, None]).astype(jnp.int32), axis=1), ne - 1)
    n_used = (pad_end[-1] // bm).astype(jnp.int32).reshape(1)
    pos = pl.pallas_call(
        _pos_kernel,
        out_shape=jax.ShapeDtypeStruct((TOP_K, n), jnp.int32),
        grid=(n // tm,),
        in_specs=[pl.BlockSpec((TOP_K, tm), lambda i: (0, i)), pl.BlockSpec((ne, 1), lambda i: (0, 0))],
        out_specs=pl.BlockSpec((TOP_K, tm), lambda i: (0, i)),
        scratch_shapes=[pltpu.VMEM((ne, LANE), F32)],
        compiler_params=params, name="moe_positions",
    )(eidx, pad_start.astype(F32).reshape(ne, 1))
    ts = SCATTER_TILE
    p = n_blocks * bm
    pos_tok = pos.T.reshape(n * TOP_K)
    pos_spec = pl.BlockSpec((ts * TOP_K,), lambda i: (i,), memory_space=pltpu.SMEM)
    rt = d // 2 // LANE
    xs = pl.pallas_call(
        functools.partial(_dispatch_kernel, n_blocks=n_blocks),
        out_shape=jax.ShapeDtypeStruct((p, rt, LANE), jnp.int32),
        grid_spec=pltpu.PrefetchScalarGridSpec(
            num_scalar_prefetch=3,
            grid=(n // ts,),
            in_specs=[pl.BlockSpec((ts * TOP_K,), lambda i, *_: (i,), memory_space=pltpu.SMEM),
                      pl.BlockSpec((ts, d), lambda i, *_: (i, 0))],
            out_specs=pl.BlockSpec(memory_space=pl.ANY),
            scratch_shapes=[pltpu.VMEM((ts, rt, LANE), jnp.int32), pltpu.VMEM((bm, rt, LANE), jnp.int32),
                            pltpu.SemaphoreType.DMA, pltpu.SemaphoreType.DMA],
        ),
        compiler_params=params, name="moe_dispatch",
    )(pad_start + counts, padded - counts, n_used, pos_tok, t)

    is_start = jnp.concatenate([jnp.ones((1,), bool), blk_e[1:] != blk_e[:-1]])
    run_id = jnp.cumsum(is_start.astype(jnp.int32)) - 1
    ids = jnp.arange(ne, dtype=jnp.int32)
    later_busy = jnp.where((counts[None, :] > 0) & (ids[None, :] > ids[:, None]), ids[None, :], ne)
    next_busy = jnp.min(later_busy, axis=1)
    next_e = jnp.sum(jnp.where(blk_e[:, None] == ids[None, :], next_busy[None, :], 0), axis=1)
    next_e = jnp.where(next_e >= ne, -1, next_e).astype(jnp.int32)

    def x_map(i, blk_e, run, nxt, nused):
        return (jnp.minimum(i, nused[0] - 1), 0, 0)

    y_p = pl.pallas_call(
        functools.partial(_moe_ffn_kernel, layer=layer),
        out_shape=jax.ShapeDtypeStruct((p, rt, LANE), jnp.int32),
        grid_spec=pltpu.PrefetchScalarGridSpec(
            num_scalar_prefetch=4,
            grid=(n_blocks,),
            in_specs=[pl.BlockSpec((bm, rt, LANE), x_map),
                      pl.BlockSpec(memory_space=pl.ANY), pl.BlockSpec(memory_space=pl.ANY)],
            out_specs=pl.BlockSpec((bm, rt, LANE), lambda i, *_: (i, 0, 0)),
            scratch_shapes=[pltpu.VMEM((2, d, ff2), F32), pltpu.VMEM((2, ff2 // 2, d), F32),
                            pltpu.VMEM((d, ff2), BF16), pltpu.VMEM((ff2 // 2, d), BF16),
                            pltpu.SemaphoreType.DMA((2, 2))],
        ),
        compiler_params=params, name="moe_expert_ffn",
    )(blk_e, run_id, next_e, n_used, xs, w_gu, w_down)
    rows = pl.BlockSpec((ts, d), lambda i: (i, 0))
    whole = lambda a: pl.BlockSpec(a.shape, lambda i: (0,) * a.ndim)
    last_gate = g2_rows.shape[0] - 1
    gate_spec = pl.BlockSpec((None, 1, d), lambda i: (jnp.minimum(i // (rows_per_gate // ts), last_gate), 0, 0))
    consts = [sh_gu.astype(BF16), sh_down.astype(BF16), gain.reshape(1, d)]
    return pl.pallas_call(
        _combine_kernel,
        out_shape=jax.ShapeDtypeStruct((n, d), F32),
        grid=(n // ts,),
        in_specs=([pos_spec, pl.BlockSpec((ts, TOP_K), lambda i: (i, 0)), rows, rows]
                  + [whole(a) for a in consts] + [gate_spec, pl.BlockSpec(memory_space=pl.ANY)]),
        out_specs=rows,
        scratch_shapes=[pltpu.VMEM((TOP_K, ts, rt, LANE), jnp.int32), pltpu.SemaphoreType.DMA],
        compiler_params=params, name="moe_combine",
    )(pos_tok, wsel.T, t, x_res, *consts, g2_rows, y_p)


def kernel(x, c, ctx, c_ctx, ada_w, ada_b, norm_g, w_in, w_out, hy_conv_w, hy_conv_b, hy_ffn_w1, hy_ffn_b1, hy_ffn_w2, hy_ffn_b2, hy_ffn_w3, hy_freq, hy_bias, hy_norm, hg_lb_logits, hg_norm, gla_a_up, gla_a_b, gla_norm, ml_conv_w, ml_conv_b, ml_gate_b, ml_norm, router_w, router_b, exp_w_gu, exp_w_down, sh_w_gu, sh_w_down):
    bsz, seq, d = x.shape
    n_ctx = ctx.shape[1]
    depth = ada_w.shape[0]
    rows = seq // GRID_W
    x = x + _pos_embed_2d(rows, d)[None]
    xc = ctx
    lb_cum = jnp.cumsum(jax.nn.softmax(hg_lb_logits, axis=0), axis=0)
    lower_bounds = lb_cum - lb_cum[0:1]
    for l in range(depth):
        with_ctx = l < depth - 1
        mod = (jax.nn.silu(c) @ ada_w[l] + ada_b[l])[:, None, :]
        mod_c = jax.nn.silu(c_ctx) @ ada_w[l] + ada_b[l]
        sh1, sc1, g1, sh2, sc2, g2 = jnp.split(mod, 6, axis=-1)
        csh1, csc1, cg1, csh2, csc2, cg2 = jnp.split(mod_c, 6, axis=-1)
        w_in_l = _arrange_w_in(w_in[l])
        ctx_rows = lambda a: jnp.broadcast_to(a.reshape(1, 1, d), (bsz, 1, d))
        pl3 = _in_proj(x, norm_g[l, 0], sc1, sh1, w_in_l)
        pc3 = _in_proj(xc, norm_g[l, 0], ctx_rows(csc1), ctx_rows(csh1), w_in_l)
        hy_args = (hy_conv_w[l], hy_conv_b[l], hy_ffn_w1[l], hy_ffn_b1[l], hy_ffn_w2[l], hy_ffn_b2[l],
                   hy_ffn_w3[l], hy_freq[l], hy_bias[l], hy_norm[l])
        y_hy = _hyena_long(pl3, *hy_args)
        oc_hg, o_hg = _hgrn2(pc3, pl3, lower_bounds[l])
        oc_gla, o_gla = _gla(pc3, pl3, gla_a_up[l], gla_a_b[l])
        hc_ml, h_ml = _mlstm(pc3, pl3, ml_conv_w[l], ml_conv_b[l], ml_gate_b[l])
        head_gains = jnp.stack([hg_norm[l], gla_norm[l], ml_norm[l]])
        out_args = (w_out[l], head_gains, norm_g[l, 1], norm_g[l, 2])
        x, h = _mixer_out(y_hy, o_hg, o_gla, h_ml, pl3, x, *out_args, jnp.concatenate([g1, sc2, sh2], axis=1))
        moe_args = (router_w[l], router_b[l], exp_w_gu, exp_w_down, sh_w_gu[l], sh_w_down[l], l, norm_g[l, 3])
        if with_ctx:
            mod_ctx = jnp.broadcast_to(jnp.stack([cg1, csc2, csh2])[None], (bsz, 3, d))
            xc, hc = _mixer_out(_hyena(pc3, *hy_args), oc_hg, oc_gla, hc_ml, pc3, xc, *out_args, mod_ctx)
            tokens = lambda a, ac: jnp.concatenate([a.reshape(bsz * seq, d), ac.reshape(bsz * n_ctx, d)], axis=0)
            gates = jnp.concatenate([g2, cg2.reshape(1, 1, d)], axis=0)
            x_all = _moe(tokens(h, hc), tokens(x, xc), *moe_args, gates, seq)
            x = x_all[:bsz * seq].reshape(bsz, seq, d)
            xc = x_all[bsz * seq:].reshape(bsz, n_ctx, d)
        else:
            x = _moe(h.reshape(bsz * seq, d), x.reshape(bsz * seq, d), *moe_args, g2, seq).reshape(bsz, seq, d)
    return x
```
